```python
import math
import jax, jax.numpy as jnp
from jax import lax
import numpy as np

D_MODEL = 1024
BATCH = 1
SEQ = 16384
DEPTH = 2

D_MIX = 1024
HEAD_DIM = 64
N_ATTN_HEADS = 8
ATTN_WIDTH = N_ATTN_HEADS * HEAD_DIM
N_KV_HEADS = 2
GQA_REP = N_ATTN_HEADS // N_KV_HEADS
KV_W = N_KV_HEADS * HEAD_DIM
N_BRANCH = 3
CONV_WIDTH = D_MIX - ATTN_WIDTH
N_CONV_GROUPS = 8
CONV_K = 3
CMP_LEN = 32
CMP_STRIDE = 16
CMP_HIDDEN = 128
SEL_LEN = 64
SEL_TOPK = 16
WINDOW = 512
Q_BLOCK = 128
N_BUCKETS = 32
MAX_DISTANCE = 128
D_FF = 2816
N_EXPERTS = 8
TOP_K = 2
D_FF_EXPERT = 2816
EPS = 1e-6
NEG = -1e30
FORCE = 1e9
IN_COLS = ATTN_WIDTH + 6 * KV_W + N_BRANCH * N_ATTN_HEADS + 3 * CONV_WIDTH

kernel_name = "hymba_nsa_shortconv_moe_trunk"


def rms_norm(x, g):
    xf = x.astype(jnp.float32)
    y = xf * lax.rsqrt(jnp.mean(xf * xf, axis=-1, keepdims=True) + EPS)
    return (y * g.astype(jnp.float32)).astype(x.dtype)


def rel_bucket(dist):
    n = jnp.maximum(dist, 0)
    max_exact = N_BUCKETS // 2
    nf = jnp.maximum(n, max_exact).astype(jnp.float32)
    large = max_exact + (jnp.log(nf / max_exact) / math.log(MAX_DISTANCE / max_exact)
                         * (N_BUCKETS - max_exact)).astype(jnp.int32)
    large = jnp.minimum(large, N_BUCKETS - 1)
    return jnp.where(n < max_exact, n, large)


def masked_softmax(s, mask):
    s = jnp.where(mask, s, NEG)
    m = jnp.max(s, axis=-1, keepdims=True)
    e = jnp.exp(s - m) * mask
    return e / jnp.maximum(jnp.sum(e, axis=-1, keepdims=True), 1e-30)


def compress(kv, pos, w1, w2):
    B, S, G, dh = kv.shape
    n_cmp = (S - CMP_LEN) // CMP_STRIDE + 1
    idx = jnp.arange(n_cmp)[:, None] * CMP_STRIDE + jnp.arange(CMP_LEN)[None, :]
    blocks = kv[:, idx] + pos[None, None, :, None, :]
    flat = blocks.transpose(0, 1, 3, 2, 4).reshape(B, n_cmp, G, CMP_LEN * dh)
    return jax.nn.gelu(flat @ w1) @ w2


def nsa_attention(q, kc, vc, ks, vs, kw, vw, gates, rel_bias):
    B, S = q.shape[0], q.shape[1]
    G, R, dh, T = N_KV_HEADS, GQA_REP, HEAD_DIM, Q_BLOCK
    n_cmp = kc.shape[1]
    n_sel = S // SEL_LEN
    topk = min(SEL_TOPK, n_sel)
    scale = HEAD_DIM ** -0.5
    cmp_end = jnp.arange(n_cmp) * CMP_STRIDE + CMP_LEN - 1
    ks_blk = ks.reshape(B, n_sel, SEL_LEN, G, dh).transpose(0, 3, 1, 2, 4)
    vs_blk = vs.reshape(B, n_sel, SEL_LEN, G, dh).transpose(0, 3, 1, 2, 4)
    kw_pad = jnp.pad(kw, ((0, 0), (WINDOW, 0), (0, 0), (0, 0)))
    vw_pad = jnp.pad(vw, ((0, 0), (WINDOW, 0), (0, 0), (0, 0)))
    bias_gnr = rel_bias.reshape(N_BUCKETS, G, R).transpose(1, 0, 2)
    pad_l = CMP_LEN // CMP_STRIDE - 1
    ratio = SEL_LEN // CMP_STRIDE
    pad_r = ratio * n_sel - n_cmp
    gather = jax.vmap(jax.vmap(lambda kb, ib: kb[ib]))
    g_ar = jnp.arange(G)[None, :, None, None, None]

    def head_bias(dist):
        b = rel_bias[rel_bucket(dist)]
        return b.reshape(dist.shape[0], dist.shape[1], G, R).transpose(2, 3, 0, 1)[None].astype(jnp.float32)

    def block(c):
        t0 = c * T
        qg = lax.dynamic_slice_in_dim(q, t0, T, axis=1).reshape(B, T, G, R, dh)
        gc = lax.dynamic_slice_in_dim(gates, t0, T, axis=1).reshape(B, T, G, R, N_BRANCH)
        tpos = t0 + jnp.arange(T)
        dist_c = tpos[:, None] - cmp_end[None, :]
        s = jnp.einsum('btgrd,bngd->bgrtn', qg, kc).astype(jnp.float32) * scale + head_bias(dist_c)
        p_cmp = masked_softmax(s, (dist_c >= 0)[None, None, None])
        o_cmp = jnp.einsum('bgrtn,bngd->btgrd', p_cmp.astype(vc.dtype), vc)
        imp = jnp.pad(p_cmp.sum(axis=2), ((0, 0), (0, 0), (0, 0), (pad_l, pad_r)))
        imp_sel = 0.0
        for m in range(ratio):
            for n in range(CMP_LEN // CMP_STRIDE):
                st = m - n + pad_l
                imp_sel = imp_sel + lax.slice_in_dim(imp, st, st + ratio * (n_sel - 1) + 1, stride=ratio, axis=3)
        cur = tpos // SEL_LEN
        j = jnp.arange(n_sel)[None, :]
        forced = (j == 0) | (j == cur[:, None]) | (j == cur[:, None] - 1)
        score = jnp.where(forced, FORCE, imp_sel)
        score = jnp.where(j > cur[:, None], NEG, score)
        _, idx = lax.top_k(score, topk)
        blk_ok = idx <= cur[None, None, :, None]
        kg = gather(ks_blk, idx)
        vg = gather(vs_blk, idx)
        kpos = idx[..., None] * SEL_LEN + jnp.arange(SEL_LEN)
        dist_s = tpos[None, None, :, None, None] - kpos
        mask_s = blk_ok[..., None] & (dist_s >= 0)
        bias_s = jnp.moveaxis(bias_gnr[g_ar, rel_bucket(dist_s)], -1, 2).astype(jnp.float32)
        s = jnp.einsum('btgrd,bgtkld->bgrtkl', qg, kg).astype(jnp.float32) * scale + bias_s
        p = masked_softmax(s.reshape(B, G, R, T, topk * SEL_LEN),
                           mask_s.reshape(B, G, 1, T, topk * SEL_LEN)).reshape(s.shape)
        o_slc = jnp.einsum('bgrtkl,bgtkld->btgrd', p.astype(vg.dtype), vg)
        kwc = lax.dynamic_slice_in_dim(kw_pad, t0, T + WINDOW, axis=1)
        vwc = lax.dynamic_slice_in_dim(vw_pad, t0, T + WINDOW, axis=1)
        wpos = t0 - WINDOW + jnp.arange(T + WINDOW)
        dist_w = tpos[:, None] - wpos[None, :]
        mask_w = (dist_w >= 0) & (dist_w < WINDOW) & (wpos[None, :] >= 0)
        s = jnp.einsum('btgrd,bsgd->bgrts', qg, kwc).astype(jnp.float32) * scale + head_bias(dist_w)
        p = masked_softmax(s, mask_w[None, None, None])
        o_win = jnp.einsum('bgrts,bsgd->btgrd', p.astype(vwc.dtype), vwc)
        o = gc[..., 0:1] * o_cmp + gc[..., 1:2] * o_slc + gc[..., 2:3] * o_win
        return o.reshape(B, T, ATTN_WIDTH)

    out = lax.map(block, jnp.arange(S // T))
    return out.transpose(1, 0, 2, 3).reshape(B, S, ATTN_WIDTH)


def short_conv(bg, cg, hin, w):
    z = cg * hin
    y = lax.conv_general_dilated(z, w[:, None, :].astype(z.dtype), window_strides=(1,),
                                 padding=[(CONV_K - 1, 0)],
                                 dimension_numbers=('NWC', 'WIO', 'NWC'),
                                 feature_group_count=CONV_WIDTH)
    return bg * y


def swiglu(h, wg, wu, wd):
    return (jax.nn.silu(h @ wg) * (h @ wu)) @ wd


def moe_swiglu(h, router_w, router_b, wg, wu, wd):
    B, S, D = h.shape
    ht = h.reshape(-1, D)
    n_tok = ht.shape[0]
    logits = (ht @ router_w).astype(jnp.float32) + router_b.astype(jnp.float32)
    top_logit, top_idx = lax.top_k(logits, TOP_K)
    weights = jax.nn.softmax(top_logit, axis=-1)
    flat_e = top_idx.reshape(-1)
    flat_tok = jnp.repeat(jnp.arange(n_tok), TOP_K)
    order = jnp.argsort(flat_e)
    tok_sorted = flat_tok[order]
    xs = ht[tok_sorted]
    group_sizes = jnp.bincount(flat_e, length=N_EXPERTS).astype(jnp.int32)
    a = lax.ragged_dot(xs, wg, group_sizes)
    u = lax.ragged_dot(xs, wu, group_sizes)
    y = lax.ragged_dot(jax.nn.silu(a) * u, wd, group_sizes)
    y = y * weights.reshape(-1)[order][:, None].astype(y.dtype)
    out = jnp.zeros_like(ht).at[tok_sorted].add(y)
    return out.reshape(B, S, D)


def setup_inputs(seed: int = 0) -> dict:
    key = jax.random.key(seed)
    ks = jax.random.split(key, 32)
    f32 = jnp.float32
    n_dense = (DEPTH + 1) // 2
    n_moe = DEPTH // 2
    nrm = lambda k, shape, s: jax.random.normal(k, shape, f32) * s
    gain = lambda k, shape: 1.0 + 0.02 * jax.random.normal(k, shape, f32)
    return {
        "x": jax.random.normal(ks[0], (BATCH, SEQ, D_MODEL), f32),
        "rel_bias": nrm(ks[1], (N_BUCKETS, N_ATTN_HEADS), 0.5),
        "norm1": gain(ks[2], (DEPTH, D_MODEL)),
        "w_in": nrm(ks[3], (DEPTH, D_MODEL, IN_COLS), D_MODEL ** -0.5),
        "q_norm": gain(ks[4], (DEPTH, HEAD_DIM)),
        "k_norm": gain(ks[5], (DEPTH, N_BRANCH, HEAD_DIM)),
        "cmp_pos_k": nrm(ks[6], (DEPTH, CMP_LEN, HEAD_DIM), 0.1),
        "cmp_pos_v": nrm(ks[7], (DEPTH, CMP_LEN, HEAD_DIM), 0.1),
        "cmp_k_w1": nrm(ks[8], (DEPTH, CMP_LEN * HEAD_DIM, CMP_HIDDEN), (CMP_LEN * HEAD_DIM) ** -0.5),
        "cmp_k_w2": nrm(ks[9], (DEPTH, CMP_HIDDEN, HEAD_DIM), CMP_HIDDEN ** -0.5),
        "cmp_v_w1": nrm(ks[10], (DEPTH, CMP_LEN * HEAD_DIM, CMP_HIDDEN), (CMP_LEN * HEAD_DIM) ** -0.5),
        "cmp_v_w2": nrm(ks[11], (DEPTH, CMP_HIDDEN, HEAD_DIM), CMP_HIDDEN ** -0.5),
        "conv_w": nrm(ks[12], (DEPTH, CONV_K, CONV_WIDTH), CONV_K ** -0.5),
        "attn_out_norm": gain(ks[13], (DEPTH, ATTN_WIDTH)),
        "conv_out_norm": gain(ks[14], (DEPTH, CONV_WIDTH)),
        "w_out": nrm(ks[15], (DEPTH, D_MIX, D_MODEL), D_MIX ** -0.5),
        "norm2": gain(ks[16], (DEPTH, D_MODEL)),
        "ffn_w_gate": nrm(ks[17], (n_dense, D_MODEL, D_FF), D_MODEL ** -0.5),
        "ffn_w_up": nrm(ks[18], (n_dense, D_MODEL, D_FF), D_MODEL ** -0.5),
        "ffn_w_down": nrm(ks[19], (n_dense, D_FF, D_MODEL), D_FF ** -0.5),
        "router_w": nrm(ks[20], (n_moe, D_MODEL, N_EXPERTS), D_MODEL ** -0.5),
        "router_b": nrm(ks[21], (n_moe, N_EXPERTS), 0.01),
        "moe_w_gate": nrm(ks[22], (n_moe, N_EXPERTS, D_MODEL, D_FF_EXPERT), D_MODEL ** -0.5),
        "moe_w_up": nrm(ks[23], (n_moe, N_EXPERTS, D_MODEL, D_FF_EXPERT), D_MODEL ** -0.5),
        "moe_w_down": nrm(ks[24], (n_moe, N_EXPERTS, D_FF_EXPERT, D_MODEL), D_FF_EXPERT ** -0.5),
    }


def reference(x, rel_bias, norm1, w_in, q_norm, k_norm, cmp_pos_k, cmp_pos_v, cmp_k_w1, cmp_k_w2,
              cmp_v_w1, cmp_v_w2, conv_w, attn_out_norm, conv_out_norm, w_out, norm2,
              ffn_w_gate, ffn_w_up, ffn_w_down, router_w, router_b, moe_w_gate, moe_w_up, moe_w_down):
    B, S, _ = x.shape
    G, H, dh = N_KV_HEADS, N_ATTN_HEADS, HEAD_DIM
    sizes = [ATTN_WIDTH] + [KV_W] * 6 + [N_BRANCH * H] + [CONV_WIDTH] * 3
    split_at = [int(v) for v in np.cumsum(sizes)[:-1]]
    for layer in range(DEPTH):
        h = rms_norm(x, norm1[layer])
        proj = h @ w_in[layer]
        q, kc, vc, ksl, vsl, kwn, vwn, gts, cb, cc, ch = jnp.split(proj, split_at, axis=-1)
        q = rms_norm(q.reshape(B, S, H, dh), q_norm[layer])
        kc = rms_norm(compress(kc.reshape(B, S, G, dh), cmp_pos_k[layer], cmp_k_w1[layer], cmp_k_w2[layer]),
                      k_norm[layer, 0])
        vc = compress(vc.reshape(B, S, G, dh), cmp_pos_v[layer], cmp_v_w1[layer], cmp_v_w2[layer])
        ksl = rms_norm(ksl.reshape(B, S, G, dh), k_norm[layer, 1])
        kwn = rms_norm(kwn.reshape(B, S, G, dh), k_norm[layer, 2])
        gts = jax.nn.sigmoid(gts.reshape(B, S, H, N_BRANCH))
        o_attn = nsa_attention(q, kc, vc, ksl, vsl.reshape(B, S, G, dh), kwn,
                               vwn.reshape(B, S, G, dh), gts, rel_bias)
        o_conv = short_conv(cb, cc, ch, conv_w[layer])
        mixed = jnp.concatenate([rms_norm(o_attn, attn_out_norm[layer]),
                                 rms_norm(o_conv, conv_out_norm[layer])], axis=-1)
        x = x + mixed @ w_out[layer]
        h = rms_norm(x, norm2[layer])
        if layer % 2 == 0:
            i = layer // 2
            x = x + swiglu(h, ffn_w_gate[i], ffn_w_up[i], ffn_w_down[i])
        else:
            i = layer // 2
            x = x + moe_swiglu(h, router_w[i], router_b[i], moe_w_gate[i], moe_w_up[i], moe_w_down[i])
    return x
```

```python
import functools
import math

import numpy as np
import jax
import jax.numpy as jnp
from jax import lax
from jax.experimental import pallas as pl
from jax.experimental.pallas import tpu as pltpu

F32 = jnp.float32
BF16 = jnp.bfloat16
I32 = jnp.int32

D_MODEL = 1024
HEAD_DIM = 64
N_HEADS = 8
N_KV = 2
GQA = N_HEADS // N_KV
ATTN_W = N_HEADS * HEAD_DIM
KV_W = N_KV * HEAD_DIM
N_BRANCH = 3
CONV_W = 512
CONV_K = 3
CMP_LEN = 32
CMP_STRIDE = 16
CMP_HIDDEN = 128
SEL_LEN = 64
SEL_TOPK = 16
WINDOW = 512
Q_BLOCK = 128
N_BUCKETS = 32
MAX_DISTANCE = 128
D_FF = 2816
N_EXPERTS = 8
EPS = 1e-6
NEG = -1e30
FORCE = 1e9
MASK_BIG = 2.0 ** 60
SCALE = HEAD_DIM ** -0.5

LANES = 128
V_ROWS = 80
CHUNK = 512
BLOCKS_PER_CHUNK = CHUNK // SEL_LEN
TC_LEAD = 24
TC_ROWS = 88
TC_WIN = 48
VMEM_LIMIT = 56 * 1024 * 1024


def _bucket_np(dist):
    n = np.maximum(dist, 0)
    max_exact = N_BUCKETS // 2
    nf = np.maximum(n, max_exact).astype(np.float64)
    v = np.log(nf / max_exact) / math.log(MAX_DISTANCE / max_exact) * (N_BUCKETS - max_exact)
    frac = np.abs(v - np.round(v))
    assert np.all((frac > 1e-6) | (n <= max_exact) | (n >= MAX_DISTANCE)), "bucket boundary is precision dependent"
    large = np.minimum(max_exact + (v + 1e-9).astype(np.int32), N_BUCKETS - 1)
    return np.where(n < max_exact, n, large).astype(np.int32)


def _index_tables():
    tl = np.arange(Q_BLOCK)[None, :]
    r = np.arange(WINDOW + Q_BLOCK + WINDOW)[:, None]
    d = tl + WINDOW - r
    idx_w = np.where((d >= 0) & (d < WINDOW), _bucket_np(d), -1)
    r = np.arange(3 * Q_BLOCK)[:, None]
    d = tl + Q_BLOCK - r
    idx_s = np.where(d >= 0, _bucket_np(d), -1)
    r = np.arange(TC_ROWS)[:, None] - TC_LEAD
    d = tl - CMP_STRIDE * r + (CMP_STRIDE * 16 - (CMP_LEN - 1))
    idx_c = np.where((d >= 0) & (r < 32), _bucket_np(d), -1)
    return idx_w.astype(np.int32), idx_s.astype(np.int32), idx_c.astype(np.int32)


def _tables_kernel(rb_ref, iw_ref, is_ref, ic_ref, tw_ref, ts_ref, tc_ref, b31_ref):
    for h in range(N_HEADS):
        g, r = divmod(h, GQA)
        lanes = slice(r * LANES, (r + 1) * LANES)
        for idx_ref, out_ref in ((iw_ref, tw_ref), (is_ref, ts_ref), (ic_ref, tc_ref)):
            out_ref[g, :, lanes] = jnp.full(idx_ref.shape, NEG, F32)

            def body(b, carry, idx_ref=idx_ref, out_ref=out_ref, g=g, lanes=lanes, h=h):
                out_ref[g, :, lanes] = jnp.where(idx_ref[...] == b, rb_ref[b, h], out_ref[g, :, lanes])
                return carry

            lax.fori_loop(0, N_BUCKETS, body, 0)
        b31_ref[g, :, lanes] = jnp.full((8, LANES), rb_ref[N_BUCKETS - 1, h], F32)


def _bias_tables(rel_bias):
    idx_w, idx_s, idx_c = _index_tables()
    width = GQA * LANES
    out_shape = (
        jax.ShapeDtypeStruct((N_KV, idx_w.shape[0], width), F32),
        jax.ShapeDtypeStruct((N_KV, idx_s.shape[0], width), F32),
        jax.ShapeDtypeStruct((N_KV, idx_c.shape[0], width), F32),
        jax.ShapeDtypeStruct((N_KV, 8, width), F32),
    )
    vmem = pl.BlockSpec(memory_space=pltpu.VMEM)
    return pl.pallas_call(
        _tables_kernel,
        out_shape=out_shape,
        in_specs=[pl.BlockSpec(memory_space=pltpu.SMEM), vmem, vmem, vmem],
        out_specs=(vmem, vmem, vmem, vmem),
        name="bias_tables",
    )(rel_bias, jnp.asarray(idx_w), jnp.asarray(idx_s), jnp.asarray(idx_c))


def _in_proj_kernel(x_ref, g1_ref, wtok_ref, wfeat_ref, ind_ref, kgain_ref, aug_ref, gq_ref, convw_ref, cgain_ref,
                    qT_ref, gT_ref, vsT_ref, vwT_ref, ks_ref, kw_ref, kcv_ref, convn_ref, zs_ref):
    tm = x_ref.shape[0]
    i = pl.program_id(0)
    x = x_ref[...]
    ms = jnp.mean(x * x, axis=-1, keepdims=True)
    h = (x * lax.rsqrt(ms + EPS) * g1_ref[...]).astype(BF16)
    tok = jnp.dot(h, wtok_ref[...], preferred_element_type=F32)
    feat = lax.dot_general(wfeat_ref[...], h, (((1,), (1,)), ((), ())),
                           preferred_element_type=F32)

    kcv_ref[...] = tok[:, 0:2 * KV_W].astype(BF16)
    kk = tok[:, 2 * KV_W:4 * KV_W]
    ssq = jnp.dot(kk * kk, ind_ref[...], preferred_element_type=F32)
    kn = kk * lax.rsqrt(ssq * (1.0 / HEAD_DIM) + EPS) * kgain_ref[...]
    ksl = kn[:, 0:KV_W]
    lane = lax.broadcasted_iota(I32, (tm, LANES), 1)
    aug = aug_ref[...]
    ks_ref[0] = jnp.where(lane < HEAD_DIM, ksl, aug).astype(BF16)
    ks_ref[1] = jnp.where(lane < HEAD_DIM, pltpu.roll(ksl, HEAD_DIM, 1), aug).astype(BF16)
    kw_ref[...] = kn[:, KV_W:2 * KV_W].astype(BF16)

    c0 = 4 * KV_W
    cb = tok[:, c0:c0 + CONV_W]
    cc = tok[:, c0 + CONV_W:c0 + 2 * CONV_W]
    ch = tok[:, c0 + 2 * CONV_W:c0 + 3 * CONV_W]
    z = cc * ch

    @pl.when(i == 0)
    def _():
        zs_ref[0:8, :] = jnp.zeros((8, CONV_W), F32)

    zs_ref[8:8 + tm, :] = z
    z1 = zs_ref[7:7 + tm, :]
    z2 = zs_ref[6:6 + tm, :]
    w = convw_ref[...]
    y = w[0:1, :] * z2 + w[1:2, :] * z1 + w[2:3, :] * z
    zs_ref[0:8, :] = z[tm - 8:tm, :]
    oc = cb * y
    msc = jnp.mean(oc * oc, axis=-1, keepdims=True)
    convn_ref[...] = (oc * lax.rsqrt(msc + EPS) * cgain_ref[...]).astype(BF16)

    q = feat[0:ATTN_W].reshape(N_HEADS, HEAD_DIM, tm)
    qss = jnp.sum(q * q, axis=1, keepdims=True)
    qn = q * lax.rsqrt(qss * (1.0 / HEAD_DIM) + EPS) * gq_ref[...][None]
    qT_ref[...] = (qn * SCALE).reshape(ATTN_W, tm).astype(BF16)
    ones_rows = (lax.broadcasted_iota(I32, (V_ROWS - HEAD_DIM, tm), 0) == 0).astype(BF16)
    for g in range(N_KV):
        r0 = ATTN_W + g * HEAD_DIM
        vsT_ref[g, 0:HEAD_DIM, :] = feat[r0:r0 + HEAD_DIM].astype(BF16)
        vsT_ref[g, HEAD_DIM:V_ROWS, :] = ones_rows
        r1 = ATTN_W + KV_W + g * HEAD_DIM
        vwT_ref[g, 0:HEAD_DIM, :] = feat[r1:r1 + HEAD_DIM].astype(BF16)
        vwT_ref[g, HEAD_DIM:V_ROWS, :] = ones_rows
    g0 = ATTN_W + 2 * KV_W
    gT_ref[...] = jax.nn.sigmoid(feat[g0:g0 + N_BRANCH * N_HEADS])


_IN_TM = 512


def _in_proj(x2, g1, wtok, wfeat, kgain, gq, convw, cgain):
    S = x2.shape[0]
    tm = _IN_TM
    nt = S // tm
    ind = np.kron(np.eye(2 * N_KV, dtype=np.float32), np.ones((HEAD_DIM, HEAD_DIM), np.float32))
    aug = np.zeros((tm, LANES), np.float32)
    blk = (np.arange(tm) // SEL_LEN) % BLOCKS_PER_CHUNK
    aug[np.arange(tm), HEAD_DIM + blk] = 1.0
    aug[:, HEAD_DIM + BLOCKS_PER_CHUNK:HEAD_DIM + BLOCKS_PER_CHUNK + 2] = 1.0
    const = lambda shape: pl.BlockSpec(shape, lambda i: (0,) * len(shape))
    out_shape = (
        jax.ShapeDtypeStruct((ATTN_W, S), BF16),
        jax.ShapeDtypeStruct((N_BRANCH * N_HEADS, S), F32),
        jax.ShapeDtypeStruct((N_KV, V_ROWS, S), BF16),
        jax.ShapeDtypeStruct((N_KV, V_ROWS, S), BF16),
        jax.ShapeDtypeStruct((N_KV, S, LANES), BF16),
        jax.ShapeDtypeStruct((S, LANES), BF16),
        jax.ShapeDtypeStruct((S, 2 * KV_W), BF16),
        jax.ShapeDtypeStruct((S, CONV_W), BF16),
    )
    out_specs = (
        pl.BlockSpec((ATTN_W, tm), lambda i: (0, i)),
        pl.BlockSpec((N_BRANCH * N_HEADS, tm), lambda i: (0, i)),
        pl.BlockSpec((N_KV, V_ROWS, tm), lambda i: (0, 0, i)),
        pl.BlockSpec((N_KV, V_ROWS, tm), lambda i: (0, 0, i)),
        pl.BlockSpec((N_KV, tm, LANES), lambda i: (0, i, 0)),
        pl.BlockSpec((tm, LANES), lambda i: (i, 0)),
        pl.BlockSpec((tm, 2 * KV_W), lambda i: (i, 0)),
        pl.BlockSpec((tm, CONV_W), lambda i: (i, 0)),
    )
    in_specs = [
        pl.BlockSpec((tm, D_MODEL), lambda i: (i, 0)),
        const((1, D_MODEL)),
        const(wtok.shape),
        const(wfeat.shape),
        const(ind.shape),
        const((1, 2 * KV_W)),
        const(aug.shape),
        const((HEAD_DIM, tm)),
        const((8, CONV_W)),
        const((1, CONV_W)),
    ]
    return pl.pallas_call(
        _in_proj_kernel,
        grid=(nt,),
        in_specs=in_specs,
        out_specs=out_specs,
        out_shape=out_shape,
        scratch_shapes=[pltpu.VMEM((tm + 8, CONV_W), F32)],
        compiler_params=pltpu.CompilerParams(dimension_semantics=("arbitrary",), vmem_limit_bytes=VMEM_LIMIT),
        name="in_proj",
    )(x2, g1, wtok, wfeat, jnp.asarray(ind), kgain, jnp.asarray(aug), gq, convw, cgain)


def _compress_kernel(r_ref, wtop_ref, wbot_ref, pos_ref, w1_ref, w2k_ref, w2vT_ref, kgain_ref, kc_ref, vcT_ref):
    nr = r_ref.shape[0]
    r = r_ref[...]
    u = jnp.dot(r, wtop_ref[...], preferred_element_type=F32)
    lo = jnp.dot(r, wbot_ref[...], preferred_element_type=F32)
    bias = jnp.dot(pos_ref[...], w1_ref[...], preferred_element_type=F32)[0:1, :]
    bias4 = jnp.concatenate([bias[:, 0:CMP_HIDDEN]] * N_KV + [bias[:, CMP_HIDDEN:]] * N_KV, axis=1)
    hid = u + pltpu.roll(lo, nr - 1, 0) + bias4
    act = jax.nn.gelu(hid)
    for g in range(N_KV):
        ak = act[:, g * CMP_HIDDEN:(g + 1) * CMP_HIDDEN].astype(BF16)
        av = act[:, (N_KV + g) * CMP_HIDDEN:(N_KV + g + 1) * CMP_HIDDEN].astype(BF16)
        kc = jnp.dot(ak, w2k_ref[...], preferred_element_type=F32)
        ssq = jnp.sum(kc * kc, axis=-1, keepdims=True)
        kc_ref[g] = (kc * lax.rsqrt(ssq * (1.0 / HEAD_DIM) + EPS) * kgain_ref[...]).astype(BF16)
        vcT_ref[g] = lax.dot_general(w2vT_ref[...], av, (((1,), (1,)), ((), ())),
                                     preferred_element_type=F32).astype(BF16)


def _compress(kcv, wtop, wbot, pos, w1, w2k, w2vT, kgain):
    S = kcv.shape[0]
    nr = S // CMP_STRIDE
    r = kcv.reshape(nr, CMP_STRIDE * 2 * KV_W)
    vmem = pl.BlockSpec(memory_space=pltpu.VMEM)
    return pl.pallas_call(
        _compress_kernel,
        out_shape=(jax.ShapeDtypeStruct((N_KV, nr, LANES), BF16),
                   jax.ShapeDtypeStruct((N_KV, HEAD_DIM, nr), BF16)),
        in_specs=[vmem] * 8,
        out_specs=(vmem, vmem),
        compiler_params=pltpu.CompilerParams(vmem_limit_bytes=VMEM_LIMIT),
        name="compress",
    )(r, wtop, wbot, pos, w1, w2k, w2vT, kgain)


def _tile4(row):
    return jnp.concatenate([row] * GQA, axis=1)


def _attn_kernel(qT_ref, gT_ref, ks_ref, vsT_ref, kw_ref, vwT_ref, kc_ref, vcT_ref,
                 tw_ref, ts_ref, tc_ref, b31_ref, gain_ref, out_ref,
                 sc_ref, imp_ref, sel_ref, aoct_ref, qs_ref):
    ncr = kc_ref.shape[1]
    nsel = sel_ref.shape[1]
    noct = aoct_ref.shape[1]
    width = GQA * LANES
    c = pl.program_id(0)
    t0 = c * Q_BLOCK
    tl = lax.broadcasted_iota(I32, (1, LANES), 1)
    cur = 2 * c + (tl >= SEL_LEN).astype(I32)
    zb = jnp.maximum(2 * c - 2, 0)
    zs = pl.multiple_of(zb * SEL_LEN, Q_BLOCK)
    ts_off = pl.multiple_of(zs - (t0 - Q_BLOCK), Q_BLOCK)
    zeros_q = jnp.zeros((HEAD_DIM, width), BF16)

    o_cmp, o_win, m_init, acc_init, gates = [], [], [], [], []
    for g in range(N_KV):
        qg = jnp.concatenate([qT_ref[(GQA * g + r) * HEAD_DIM:(GQA * g + r + 1) * HEAD_DIM, :]
                              for r in range(GQA)], axis=1)
        qc = jnp.concatenate([qg, zeros_q], axis=0)
        qw = qc if g == 0 else jnp.concatenate([zeros_q, qg], axis=0)
        b31row = b31_ref[g, 0:1, :]

        lo = pl.multiple_of(jnp.clip((8 * c - 16) // 16 * 16, 0, ncr - TC_WIN), 16)
        tc_off = pl.multiple_of(lo - (8 * c - 16) + TC_LEAD, 8)
        s = jnp.dot(kc_ref[g], qc, preferred_element_type=F32)
        row = lax.broadcasted_iota(I32, (ncr, width), 0)
        sc_ref[...] = s + jnp.where(row < lo, b31row, NEG)
        s_loc = jnp.dot(kc_ref[g, pl.ds(lo, TC_WIN), :], qc, preferred_element_type=F32)
        sc_ref[pl.ds(lo, TC_WIN), :] = s_loc + tc_ref[g, pl.ds(tc_off, TC_WIN), :]
        s = sc_ref[...]
        m = jnp.max(s, axis=0, keepdims=True)
        e = jnp.exp(s - m)
        l = jnp.sum(e, axis=0, keepdims=True)
        col_ok = _tile4(t0 + tl) >= CMP_LEN - 1
        p = e * jnp.where(col_ok, 1.0 / jnp.maximum(l, 1e-30), 0.0)
        o_cmp.append(jnp.dot(vcT_ref[g], p.astype(BF16), preferred_element_type=F32))
        imp = (p[:, 0:LANES] + p[:, LANES:2 * LANES]) + p[:, 2 * LANES:3 * LANES] + p[:, 3 * LANES:4 * LANES]

        imp_ref[0:8, :] = jnp.zeros((8, LANES), F32)
        imp_ref[8:8 + ncr, :] = imp
        imp_ref[8 + ncr:16 + ncr, :] = jnp.zeros((8, LANES), F32)
        taps = (0, -1, 1, 0, 2, 1, 3, 2)
        isel = jnp.zeros((nsel, LANES), F32)
        for off in taps:
            isel = isel + imp_ref[pl.ds(8 + off, nsel, stride=4), :]
        j = lax.broadcasted_iota(I32, (nsel, LANES), 0)
        forced = (j == 0) | (j == cur) | (j == cur - 1)
        score = jnp.where(forced, FORCE, isel)
        score = jnp.where(j > cur, NEG, score)

        def pick(_, carry, j=j):
            score, picked = carry
            mx = jnp.max(score, axis=0, keepdims=True)
            first = jnp.min(jnp.where(score == mx, j, nsel), axis=0, keepdims=True)
            hit = j == first
            return jnp.where(hit, -jnp.inf, score), jnp.where(hit, 1.0, picked)

        _, picked = lax.fori_loop(0, SEL_TOPK, pick, (score, jnp.zeros((nsel, LANES), F32)))
        chosen = (picked > 0.5) & (j <= cur)
        sel_ref[g] = jnp.where(chosen, 0.0, NEG)
        a_far = _tile4(jnp.where(chosen & (j < zb), 0.0, -MASK_BIG)).reshape(noct, BLOCKS_PER_CHUNK, width)
        b_hi = b31row.astype(BF16).astype(F32)
        erow = lax.broadcasted_iota(I32, (8, width), 0)
        extra = jnp.where(erow == 0, b_hi, jnp.where(erow == 1, b31row - b_hi, 0.0))
        aoct_ref[g] = jnp.concatenate([a_far, jnp.broadcast_to(extra[None], (noct, 8, width))], axis=1).astype(BF16)
        qs_ref[g, 0:HEAD_DIM, :] = qg
        qs_ref[g, HEAD_DIM + 16:LANES, :] = jnp.zeros((LANES - HEAD_DIM - 16, width), BF16)

        sz = jnp.dot(ks_ref[g, pl.ds(zs, 2 * Q_BLOCK), :], qc, preferred_element_type=F32)
        sz = sz + ts_ref[g, pl.ds(ts_off, 2 * Q_BLOCK), :]
        zone_sel = jnp.concatenate(
            [jnp.broadcast_to(_tile4(sel_ref[g, pl.ds(zb + i, 1), :]), (SEL_LEN, width)) for i in range(4)], axis=0)
        sz = sz + zone_sel
        mz = jnp.max(sz, axis=0, keepdims=True)
        pz = jnp.exp(sz - mz)
        m_init.append(mz)
        acc_init.append(jnp.dot(vsT_ref[g, :, pl.ds(zs, 2 * Q_BLOCK)], pz.astype(BF16), preferred_element_type=F32))

        ws = pl.multiple_of(jnp.maximum(t0 - WINDOW, 0), Q_BLOCK)
        tw_off = pl.multiple_of(ws - (t0 - WINDOW), Q_BLOCK)
        nw = WINDOW + Q_BLOCK
        sw = jnp.dot(kw_ref[pl.ds(ws, nw), :], qw, preferred_element_type=F32) + tw_ref[g, pl.ds(tw_off, nw), :]
        mw = jnp.max(sw, axis=0, keepdims=True)
        pw = jnp.exp(sw - mw)
        accw = jnp.dot(vwT_ref[g, :, pl.ds(ws, nw)], pw.astype(BF16), preferred_element_type=F32)
        o_win.append(accw[0:HEAD_DIM] * (1.0 / jnp.maximum(accw[HEAD_DIM:HEAD_DIM + 1], 1e-30)))
        gates.append([jnp.concatenate([gT_ref[br * N_HEADS + GQA * g + r:br * N_HEADS + GQA * g + r + 1, :]
                                       for r in range(GQA)], axis=1) for br in range(N_BRANCH)])

    nfar = (zb + BLOCKS_PER_CHUNK - 1) // BLOCKS_PER_CHUNK

    def far(u, carry):
        out = []
        k0 = pl.multiple_of(u * CHUNK, CHUNK)
        for g in range(N_KV):
            m, acc = carry[2 * g], carry[2 * g + 1]
            qs_ref[g, HEAD_DIM:HEAD_DIM + 16, :] = aoct_ref[g, u]
            s = jnp.dot(ks_ref[g, pl.ds(k0, CHUNK), :], qs_ref[g], preferred_element_type=F32)
            mn = jnp.maximum(m, jnp.max(s, axis=0, keepdims=True))
            alpha = jnp.exp(m - mn)
            p = jnp.exp(s - mn)
            acc = alpha * acc + jnp.dot(vsT_ref[g, :, pl.ds(k0, CHUNK)], p.astype(BF16), preferred_element_type=F32)
            out += [mn, acc]
        return tuple(out)

    carry = lax.fori_loop(0, nfar, far, (m_init[0], acc_init[0], m_init[1], acc_init[1]))

    o_all, ssq = [], jnp.zeros((1, LANES), F32)
    for g in range(N_KV):
        acc = carry[2 * g + 1]
        o_sel = acc[0:HEAD_DIM] * (1.0 / jnp.maximum(acc[HEAD_DIM:HEAD_DIM + 1], 1e-30))
        o = gates[g][0] * o_cmp[g] + gates[g][1] * o_sel + gates[g][2] * o_win[g]
        o_all.append(o)
        cs = jnp.sum(o * o, axis=0, keepdims=True)
        ssq = ssq + ((cs[:, 0:LANES] + cs[:, LANES:2 * LANES]) + (cs[:, 2 * LANES:3 * LANES] + cs[:, 3 * LANES:]))
    inv = _tile4(lax.rsqrt(ssq * (1.0 / ATTN_W) + EPS))
    for g in range(N_KV):
        on = o_all[g] * inv * gain_ref[g]
        for k in range(GQA // 2):
            pair = jnp.concatenate([on[:, (2 * k) * LANES:(2 * k + 1) * LANES],
                                    on[:, (2 * k + 1) * LANES:(2 * k + 2) * LANES]], axis=0)
            col = (GQA * g + 2 * k) * HEAD_DIM
            out_ref[:, col:col + 2 * HEAD_DIM] = pair.T.astype(BF16)


def _attention(qT, gT, ks, vsT, kw, vwT, kc, vcT, tables, gain_b):
    S = qT.shape[1]
    nq = S // Q_BLOCK
    ncr = kc.shape[1]
    nsel = S // SEL_LEN
    noct = S // CHUNK
    width = GQA * LANES
    tw, ts, tc, b31 = tables
    vmem = pl.BlockSpec(memory_space=pltpu.VMEM)
    in_specs = [
        pl.BlockSpec((ATTN_W, Q_BLOCK), lambda c: (0, c)),
        pl.BlockSpec((N_BRANCH * N_HEADS, Q_BLOCK), lambda c: (0, c)),
    ] + [vmem] * 11
    return pl.pallas_call(
        _attn_kernel,
        grid=(nq,),
        in_specs=in_specs,
        out_specs=pl.BlockSpec((Q_BLOCK, ATTN_W), lambda c: (c, 0)),
        out_shape=jax.ShapeDtypeStruct((S, ATTN_W), BF16),
        scratch_shapes=[
            pltpu.VMEM((ncr, width), F32),
            pltpu.VMEM((ncr + 16, LANES), F32),
            pltpu.VMEM((N_KV, nsel, LANES), F32),
            pltpu.VMEM((N_KV, noct, 16, width), BF16),
            pltpu.VMEM((N_KV, LANES, width), BF16),
        ],
        compiler_params=pltpu.CompilerParams(dimension_semantics=("arbitrary",), vmem_limit_bytes=VMEM_LIMIT),
        name="nsa_attention",
    )(qT, gT, ks, vsT, kw, vwT, kc, vcT, tw, ts, tc, b31, gain_b)


def _mix_and_norm(x_ref, attn_ref, conv_ref, wout_ref, g2_ref):
    x1 = x_ref[...] + jnp.dot(attn_ref[...], wout_ref[0:ATTN_W, :], preferred_element_type=F32) \
        + jnp.dot(conv_ref[...], wout_ref[ATTN_W:, :], preferred_element_type=F32)
    ms = jnp.mean(x1 * x1, axis=-1, keepdims=True)
    h2 = x1 * lax.rsqrt(ms + EPS) * g2_ref[...]
    return x1, h2


_FF_CHUNK = 1408


def _ffn_kernel(x_ref, attn_ref, conv_ref, wout_ref, g2_ref, wg_ref, wu_ref, wd_ref, out_ref):
    x1, h2 = _mix_and_norm(x_ref, attn_ref, conv_ref, wout_ref, g2_ref)
    h2 = h2.astype(BF16)
    a = jnp.dot(h2, wg_ref[...], preferred_element_type=F32)
    u = jnp.dot(h2, wu_ref[...], preferred_element_type=F32)
    y = (a * jax.nn.sigmoid(a) * u).astype(BF16)
    out_ref[...] = x1 + jnp.dot(y, wd_ref[...], preferred_element_type=F32)


_FFN_TM = 512


def _outproj_ffn(x2, attn_n, conv_n, wout, g2, wg, wu, wd):
    S = x2.shape[0]
    tm = _FFN_TM
    resident = pl.BlockSpec(memory_space=pltpu.VMEM)
    return pl.pallas_call(
        _ffn_kernel,
        grid=(S // tm,),
        in_specs=[
            pl.BlockSpec((tm, D_MODEL), lambda i: (i, 0)),
            pl.BlockSpec((tm, ATTN_W), lambda i: (i, 0)),
            pl.BlockSpec((tm, CONV_W), lambda i: (i, 0)),
            resident, resident, resident, resident, resident,
        ],
        out_specs=pl.BlockSpec((tm, D_MODEL), lambda i: (i, 0)),
        out_shape=jax.ShapeDtypeStruct((S, D_MODEL), F32),
        compiler_params=pltpu.CompilerParams(dimension_semantics=("arbitrary",), vmem_limit_bytes=VMEM_LIMIT),
        name="outproj_ffn",
    )(x2, attn_n, conv_n, wout, g2, wg, wu, wd)


def _router_kernel(x_ref, attn_ref, conv_ref, wout_ref, g2_ref, rw_ref, rb_ref, x1_ref, h2_ref, gates_ref):
    tm = x_ref.shape[0]
    x1, h2 = _mix_and_norm(x_ref, attn_ref, conv_ref, wout_ref, g2_ref)
    x1_ref[...] = x1
    h2b = h2.astype(BF16)
    h2_ref[...] = h2b
    lane = lax.broadcasted_iota(I32, (tm, LANES), 1)
    logits = jnp.dot(h2b, rw_ref[...], preferred_element_type=F32) + rb_ref[...]
    logits = jnp.where(lane < N_EXPERTS, logits, -jnp.inf)
    m1 = jnp.max(logits, axis=-1, keepdims=True)
    i1 = jnp.min(jnp.where(logits == m1, lane, LANES), axis=-1, keepdims=True)
    rest = jnp.where(lane == i1, -jnp.inf, logits)
    m2 = jnp.max(rest, axis=-1, keepdims=True)
    i2 = jnp.min(jnp.where(rest == m2, lane, LANES), axis=-1, keepdims=True)
    e2 = jnp.exp(m2 - m1)
    den = 1.0 + e2
    gates_ref[...] = jnp.where(lane == i1, 1.0 / den, jnp.where(lane == i2, e2 / den, 0.0))


def _outproj_router(x2, attn_n, conv_n, wout, g2, rw, rb):
    S = x2.shape[0]
    tm = _FFN_TM
    const = lambda shape: pl.BlockSpec(shape, lambda i: (0,) * len(shape))
    return pl.pallas_call(
        _router_kernel,
        grid=(S // tm,),
        in_specs=[
            pl.BlockSpec((tm, D_MODEL), lambda i: (i, 0)),
            pl.BlockSpec((tm, ATTN_W), lambda i: (i, 0)),
            pl.BlockSpec((tm, CONV_W), lambda i: (i, 0)),
            const(wout.shape), const((1, D_MODEL)), const(rw.shape), const((1, LANES)),
        ],
        out_specs=(pl.BlockSpec((tm, D_MODEL), lambda i: (i, 0)),
                   pl.BlockSpec((tm, D_MODEL), lambda i: (i, 0)),
                   pl.BlockSpec((tm, LANES), lambda i: (i, 0))),
        out_shape=(jax.ShapeDtypeStruct((S, D_MODEL), F32),
                   jax.ShapeDtypeStruct((S, D_MODEL), BF16),
                   jax.ShapeDtypeStruct((S, LANES), F32)),
        compiler_params=pltpu.CompilerParams(dimension_semantics=("arbitrary",), vmem_limit_bytes=VMEM_LIMIT),
        name="outproj_router",
    )(x2, attn_n, conv_n, wout, g2, rw, rb)


def _moe_kernel(h_ref, x1_ref, gates_ref, wg_ref, wu_ref, wd_ref, out_ref, acc_ref):
    tm = h_ref.shape[0]
    e = pl.program_id(1)
    f = pl.program_id(2)

    @pl.when((e == 0) & (f == 0))
    def _():
        acc_ref[...] = x1_ref[...]

    h = h_ref[...]
    a = jnp.dot(h, wg_ref[0], preferred_element_type=F32)
    u = jnp.dot(h, wu_ref[0], preferred_element_type=F32)
    lane = lax.broadcasted_iota(I32, (tm, LANES), 1)
    gate = jnp.sum(jnp.where(lane == e, gates_ref[...], 0.0), axis=-1, keepdims=True)
    y = (a * jax.nn.sigmoid(a) * u * gate).astype(BF16)
    acc_ref[...] += jnp.dot(y, wd_ref[0], preferred_element_type=F32)

    @pl.when((e == pl.num_programs(1) - 1) & (f == pl.num_programs(2) - 1))
    def _():
        out_ref[...] = acc_ref[...]


_MOE_TM = 1024


def _moe(h2, x1, gates, wg, wu, wd):
    S = h2.shape[0]
    tm = _MOE_TM
    nf = D_FF // _FF_CHUNK
    return pl.pallas_call(
        _moe_kernel,
        grid=(S // tm, N_EXPERTS, nf),
        in_specs=[
            pl.BlockSpec((tm, D_MODEL), lambda i, e, f: (i, 0)),
            pl.BlockSpec((tm, D_MODEL), lambda i, e, f: (i, 0)),
            pl.BlockSpec((tm, LANES), lambda i, e, f: (i, 0)),
            pl.BlockSpec((1, D_MODEL, _FF_CHUNK), lambda i, e, f: (e, 0, f)),
            pl.BlockSpec((1, D_MODEL, _FF_CHUNK), lambda i, e, f: (e, 0, f)),
            pl.BlockSpec((1, _FF_CHUNK, D_MODEL), lambda i, e, f: (e, f, 0)),
        ],
        out_specs=pl.BlockSpec((tm, D_MODEL), lambda i, e, f: (i, 0)),
        out_shape=jax.ShapeDtypeStruct((S, D_MODEL), F32),
        scratch_shapes=[pltpu.VMEM((tm, D_MODEL), F32)],
        compiler_params=pltpu.CompilerParams(dimension_semantics=("arbitrary", "arbitrary", "arbitrary"),
                                             vmem_limit_bytes=VMEM_LIMIT),
        name="moe_dense",
    )(h2, x1, gates, wg, wu, wd)


def _split_w_in(w):
    o = np.cumsum([0, ATTN_W] + [KV_W] * 6 + [N_BRANCH * N_HEADS] + [CONV_W] * 3)
    q, kc, vc, ksl, vsl, kwn, vwn, gts, cb, cc, ch = (w[:, o[i]:o[i + 1]] for i in range(11))
    perm = np.array([h * N_BRANCH + br for br in range(N_BRANCH) for h in range(N_HEADS)])
    wtok = jnp.concatenate([kc, vc, ksl, kwn, cb, cc, ch], axis=1).astype(BF16)
    feat = jnp.concatenate([q, vsl, vwn, gts[:, perm], jnp.zeros((D_MODEL, 8), w.dtype)], axis=1)
    return wtok, feat.T.astype(BF16)


def _expand_cmp_w1(w1k, w1v):
    def half(w, l0):
        wl = w.reshape(CMP_LEN, HEAD_DIM, CMP_HIDDEN)[l0:l0 + CMP_STRIDE]
        return wl
    tops = []
    for l0 in (0, CMP_STRIDE):
        blk = jnp.zeros((CMP_STRIDE, 2 * N_KV, HEAD_DIM, 2 * N_KV, CMP_HIDDEN), F32)
        for s, w in ((0, w1k), (1, w1v)):
            for g in range(N_KV):
                blk = blk.at[:, s * N_KV + g, :, s * N_KV + g, :].set(half(w, l0))
        tops.append(blk.reshape(CMP_STRIDE * 2 * KV_W, 2 * N_KV * CMP_HIDDEN).astype(BF16))
    return tops


def kernel(x, rel_bias, norm1, w_in, q_norm, k_norm, cmp_pos_k, cmp_pos_v, cmp_k_w1, cmp_k_w2, cmp_v_w1, cmp_v_w2,
           conv_w, attn_out_norm, conv_out_norm, w_out, norm2, ffn_w_gate, ffn_w_up, ffn_w_down, router_w, router_b,
           moe_w_gate, moe_w_up, moe_w_down):
    B, S, _ = x.shape
    assert B == 1 and S % CHUNK == 0 and S >= WINDOW + Q_BLOCK
    depth = norm1.shape[0]
    x2 = x.reshape(S, D_MODEL)
    tables = _bias_tables(rel_bias)
    for layer in range(depth):
        wtok, wfeat = _split_w_in(w_in[layer])
        kgain = jnp.concatenate([jnp.tile(k_norm[layer, 1], N_KV), jnp.tile(k_norm[layer, 2], N_KV)])[None, :]
        gq = jnp.broadcast_to(q_norm[layer][:, None], (HEAD_DIM, _IN_TM))
        convw = jnp.zeros((8, CONV_W), F32).at[0:CONV_K].set(conv_w[layer])
        qT, gT, vsT, vwT, ks, kw, kcv, conv_n = _in_proj(
            x2, norm1[layer][None, :], wtok, wfeat, kgain, gq, convw, conv_out_norm[layer][None, :])

        wtop, wbot = _expand_cmp_w1(cmp_k_w1[layer], cmp_v_w1[layer])
        pos = jnp.zeros((8, 2 * CMP_LEN * HEAD_DIM), F32)
        pos = pos.at[0, 0:CMP_LEN * HEAD_DIM].set(cmp_pos_k[layer].reshape(-1))
        pos = pos.at[0, CMP_LEN * HEAD_DIM:].set(cmp_pos_v[layer].reshape(-1))
        w1 = jnp.zeros((2 * CMP_LEN * HEAD_DIM, 2 * CMP_HIDDEN), F32)
        w1 = w1.at[0:CMP_LEN * HEAD_DIM, 0:CMP_HIDDEN].set(cmp_k_w1[layer])
        w1 = w1.at[CMP_LEN * HEAD_DIM:, CMP_HIDDEN:].set(cmp_v_w1[layer])
        w2k = jnp.zeros((CMP_HIDDEN, LANES), F32).at[:, 0:HEAD_DIM].set(cmp_k_w2[layer]).astype(BF16)
        w2vT = cmp_v_w2[layer].T.astype(BF16)
        kcgain = jnp.zeros((1, LANES), F32).at[0, 0:HEAD_DIM].set(k_norm[layer, 0])
        kc, vcT = _compress(kcv, wtop, wbot, pos, w1, w2k, w2vT, kcgain)

        gain_b = jnp.broadcast_to(attn_out_norm[layer].reshape(N_KV, GQA, HEAD_DIM).transpose(0, 2, 1)[:, :, :, None],
                                  (N_KV, HEAD_DIM, GQA, LANES)).reshape(N_KV, HEAD_DIM, GQA * LANES)
        attn_n = _attention(qT, gT, ks, vsT, kw, vwT, kc, vcT, tables, gain_b)

        wout = w_out[layer].astype(BF16)
        g2 = norm2[layer][None, :]
        i = layer // 2
        if layer % 2 == 0:
            x2 = _outproj_ffn(x2, attn_n, conv_n, wout, g2, ffn_w_gate[i].astype(BF16), ffn_w_up[i].astype(BF16),
                              ffn_w_down[i].astype(BF16))
        else:
            rw = jnp.zeros((D_MODEL, LANES), F32).at[:, 0:N_EXPERTS].set(router_w[i]).astype(BF16)
            rb = jnp.zeros((1, LANES), F32).at[0, 0:N_EXPERTS].set(router_b[i])
            x1, h2, gates = _outproj_router(x2, attn_n, conv_n, wout, g2, rw, rb)
            x2 = _moe(h2, x1, gates, moe_w_gate[i].astype(BF16), moe_w_up[i].astype(BF16), moe_w_down[i].astype(BF16))
    return x2.reshape(B, S, D_MODEL)
```

```python
import functools
import math

import numpy as np
import jax
import jax.numpy as jnp
from jax import lax
from jax.experimental import pallas as pl
from jax.experimental.pallas import tpu as pltpu

F32 = jnp.float32
BF16 = jnp.bfloat16
I32 = jnp.int32

D_MODEL = 1024
HEAD_DIM = 64
N_HEADS = 8
N_KV = 2
GQA = N_HEADS // N_KV
ATTN_W = N_HEADS * HEAD_DIM
KV_W = N_KV * HEAD_DIM
N_BRANCH = 3
CONV_W = 512
CONV_K = 3
CMP_LEN = 32
CMP_STRIDE = 16
CMP_HIDDEN = 128
SEL_LEN = 64
SEL_TOPK = 16
WINDOW = 512
Q_BLOCK = 128
N_BUCKETS = 32
MAX_DISTANCE = 128
D_FF = 2816
N_EXPERTS = 8
EPS = 1e-6
NEG = -1e30
FORCE = 1e9
MASK_BIG = 2.0 ** 60
SCALE = HEAD_DIM ** -0.5

LANES = 128
V_ROWS = 80
CHUNK = 512
BLOCKS_PER_CHUNK = CHUNK // SEL_LEN
TC_LEAD = 24
TC_ROWS = 88
TC_WIN = 48
VMEM_LIMIT = 56 * 1024 * 1024


def _bucket_np(dist):
    n = np.maximum(dist, 0)
    max_exact = N_BUCKETS // 2
    nf = np.maximum(n, max_exact).astype(np.float64)
    v = np.log(nf / max_exact) / math.log(MAX_DISTANCE / max_exact) * (N_BUCKETS - max_exact)
    frac = np.abs(v - np.round(v))
    assert np.all((frac > 1e-6) | (n <= max_exact) | (n >= MAX_DISTANCE)), "bucket boundary is precision dependent"
    large = np.minimum(max_exact + (v + 1e-9).astype(np.int32), N_BUCKETS - 1)
    return np.where(n < max_exact, n, large).astype(np.int32)


def _index_tables():
    tl = np.arange(Q_BLOCK)[None, :]
    r = np.arange(WINDOW + Q_BLOCK + WINDOW)[:, None]
    d = tl + WINDOW - r
    idx_w = np.where((d >= 0) & (d < WINDOW), _bucket_np(d), -1)
    r = np.arange(3 * Q_BLOCK)[:, None]
    d = tl + Q_BLOCK - r
    idx_s = np.where(d >= 0, _bucket_np(d), -1)
    r = np.arange(TC_ROWS)[:, None] - TC_LEAD
    d = tl - CMP_STRIDE * r + (CMP_STRIDE * 16 - (CMP_LEN - 1))
    idx_c = np.where((d >= 0) & (r < 32), _bucket_np(d), -1)
    return idx_w.astype(np.int32), idx_s.astype(np.int32), idx_c.astype(np.int32)


def _tables_kernel(rb_ref, iw_ref, is_ref, ic_ref, tw_ref, ts_ref, tc_ref, b31_ref):
    for h in range(N_HEADS):
        g, r = divmod(h, GQA)
        lanes = slice(r * LANES, (r + 1) * LANES)
        for idx_ref, out_ref in ((iw_ref, tw_ref), (is_ref, ts_ref), (ic_ref, tc_ref)):
            out_ref[g, :, lanes] = jnp.full(idx_ref.shape, NEG, F32)

            def body(b, carry, idx_ref=idx_ref, out_ref=out_ref, g=g, lanes=lanes, h=h):
                out_ref[g, :, lanes] = jnp.where(idx_ref[...] == b, rb_ref[b, h], out_ref[g, :, lanes])
                return carry

            lax.fori_loop(0, N_BUCKETS, body, 0)
        b31_ref[g, :, lanes] = jnp.full((8, LANES), rb_ref[N_BUCKETS - 1, h], F32)


def _bias_tables(rel_bias):
    idx_w, idx_s, idx_c = _index_tables()
    width = GQA * LANES
    out_shape = (
        jax.ShapeDtypeStruct((N_KV, idx_w.shape[0], width), F32),
        jax.ShapeDtypeStruct((N_KV, idx_s.shape[0], width), F32),
        jax.ShapeDtypeStruct((N_KV, idx_c.shape[0], width), F32),
        jax.ShapeDtypeStruct((N_KV, 8, width), F32),
    )
    vmem = pl.BlockSpec(memory_space=pltpu.VMEM)
    return pl.pallas_call(
        _tables_kernel,
        out_shape=out_shape,
        in_specs=[pl.BlockSpec(memory_space=pltpu.SMEM), vmem, vmem, vmem],
        out_specs=(vmem, vmem, vmem, vmem),
        name="bias_tables",
    )(rel_bias, jnp.asarray(idx_w), jnp.asarray(idx_s), jnp.asarray(idx_c))


def _in_proj_kernel(x_ref, g1_ref, wtok_ref, wfeat_ref, ind_ref, kgain_ref, aug_ref, gq_ref, convw_ref, cgain_ref,
                    qT_ref, gT_ref, vsT_ref, vwT_ref, ks_ref, kw_ref, kcv_ref, convn_ref, zs_ref):
    tm = x_ref.shape[0]
    i = pl.program_id(0)
    x = x_ref[...]
    ms = jnp.mean(x * x, axis=-1, keepdims=True)
    h = (x * lax.rsqrt(ms + EPS) * g1_ref[...]).astype(BF16)
    tok = jnp.dot(h, wtok_ref[...], preferred_element_type=F32)
    feat = lax.dot_general(wfeat_ref[...], h, (((1,), (1,)), ((), ())),
                           preferred_element_type=F32)

    kcv_ref[...] = tok[:, 0:2 * KV_W].astype(BF16)
    kk = tok[:, 2 * KV_W:4 * KV_W]
    ssq = jnp.dot(kk * kk, ind_ref[...], preferred_element_type=F32)
    kn = kk * lax.rsqrt(ssq * (1.0 / HEAD_DIM) + EPS) * kgain_ref[...]
    ksl = kn[:, 0:KV_W]
    lane = lax.broadcasted_iota(I32, (tm, LANES), 1)
    aug = aug_ref[...]
    ks_ref[0] = jnp.where(lane < HEAD_DIM, ksl, aug).astype(BF16)
    ks_ref[1] = jnp.where(lane < HEAD_DIM, pltpu.roll(ksl, HEAD_DIM, 1), aug).astype(BF16)
    kw_ref[...] = kn[:, KV_W:2 * KV_W].astype(BF16)

    c0 = 4 * KV_W
    cb = tok[:, c0:c0 + CONV_W]
    cc = tok[:, c0 + CONV_W:c0 + 2 * CONV_W]
    ch = tok[:, c0 + 2 * CONV_W:c0 + 3 * CONV_W]
    z = cc * ch

    @pl.when(i == 0)
    def _():
        zs_ref[0:8, :] = jnp.zeros((8, CONV_W), F32)

    zs_ref[8:8 + tm, :] = z
    z1 = zs_ref[7:7 + tm, :]
    z2 = zs_ref[6:6 + tm, :]
    w = convw_ref[...]
    y = w[0:1, :] * z2 + w[1:2, :] * z1 + w[2:3, :] * z
    zs_ref[0:8, :] = z[tm - 8:tm, :]
    oc = cb * y
    msc = jnp.mean(oc * oc, axis=-1, keepdims=True)
    convn_ref[...] = (oc * lax.rsqrt(msc + EPS) * cgain_ref[...]).astype(BF16)

    q = feat[0:ATTN_W].reshape(N_HEADS, HEAD_DIM, tm)
    qss = jnp.sum(q * q, axis=1, keepdims=True)
    qn = q * lax.rsqrt(qss * (1.0 / HEAD_DIM) + EPS) * gq_ref[...][None]
    qT_ref[...] = (qn * SCALE).reshape(ATTN_W, tm).astype(BF16)
    ones_rows = (lax.broadcasted_iota(I32, (V_ROWS - HEAD_DIM, tm), 0) == 0).astype(BF16)
    for g in range(N_KV):
        r0 = ATTN_W + g * HEAD_DIM
        vsT_ref[g, 0:HEAD_DIM, :] = feat[r0:r0 + HEAD_DIM].astype(BF16)
        vsT_ref[g, HEAD_DIM:V_ROWS, :] = ones_rows
        r1 = ATTN_W + KV_W + g * HEAD_DIM
        vwT_ref[g, 0:HEAD_DIM, :] = feat[r1:r1 + HEAD_DIM].astype(BF16)
        vwT_ref[g, HEAD_DIM:V_ROWS, :] = ones_rows
    g0 = ATTN_W + 2 * KV_W
    gT_ref[...] = jax.nn.sigmoid(feat[g0:g0 + N_BRANCH * N_HEADS])


_IN_TM = 512


def _in_proj(x2, g1, wtok, wfeat, kgain, gq, convw, cgain):
    S = x2.shape[0]
    tm = _IN_TM
    nt = S // tm
    ind = np.kron(np.eye(2 * N_KV, dtype=np.float32), np.ones((HEAD_DIM, HEAD_DIM), np.float32))
    aug = np.zeros((tm, LANES), np.float32)
    blk = (np.arange(tm) // SEL_LEN) % BLOCKS_PER_CHUNK
    aug[np.arange(tm), HEAD_DIM + blk] = 1.0
    aug[:, HEAD_DIM + BLOCKS_PER_CHUNK:HEAD_DIM + BLOCKS_PER_CHUNK + 2] = 1.0
    const = lambda shape: pl.BlockSpec(shape, lambda i: (0,) * len(shape))
    out_shape = (
        jax.ShapeDtypeStruct((ATTN_W, S), BF16),
        jax.ShapeDtypeStruct((N_BRANCH * N_HEADS, S), F32),
        jax.ShapeDtypeStruct((N_KV, V_ROWS, S), BF16),
        jax.ShapeDtypeStruct((N_KV, V_ROWS, S), BF16),
        jax.ShapeDtypeStruct((N_KV, S, LANES), BF16),
        jax.ShapeDtypeStruct((S, LANES), BF16),
        jax.ShapeDtypeStruct((S, 2 * KV_W), BF16),
        jax.ShapeDtypeStruct((S, CONV_W), BF16),
    )
    out_specs = (
        pl.BlockSpec((ATTN_W, tm), lambda i: (0, i)),
        pl.BlockSpec((N_BRANCH * N_HEADS, tm), lambda i: (0, i)),
        pl.BlockSpec((N_KV, V_ROWS, tm), lambda i: (0, 0, i)),
        pl.BlockSpec((N_KV, V_ROWS, tm), lambda i: (0, 0, i)),
        pl.BlockSpec((N_KV, tm, LANES), lambda i: (0, i, 0)),
        pl.BlockSpec((tm, LANES), lambda i: (i, 0)),
        pl.BlockSpec((tm, 2 * KV_W), lambda i: (i, 0)),
        pl.BlockSpec((tm, CONV_W), lambda i: (i, 0)),
    )
    in_specs = [
        pl.BlockSpec((tm, D_MODEL), lambda i: (i, 0)),
        const((1, D_MODEL)),
        const(wtok.shape),
        const(wfeat.shape),
        const(ind.shape),
        const((1, 2 * KV_W)),
        const(aug.shape),
        const((HEAD_DIM, tm)),
        const((8, CONV_W)),
        const((1, CONV_W)),
    ]
    return pl.pallas_call(
        _in_proj_kernel,
        grid=(nt,),
        in_specs=in_specs,
        out_specs=out_specs,
        out_shape=out_shape,
        scratch_shapes=[pltpu.VMEM((tm + 8, CONV_W), F32)],
        compiler_params=pltpu.CompilerParams(dimension_semantics=("arbitrary",), vmem_limit_bytes=VMEM_LIMIT),
        name="in_proj",
    )(x2, g1, wtok, wfeat, jnp.asarray(ind), kgain, jnp.asarray(aug), gq, convw, cgain)


def _compress_kernel(r_ref, wtop_ref, wbot_ref, pos_ref, w1_ref, w2k_ref, w2vT_ref, kgain_ref, kc_ref, vcT_ref):
    nr = r_ref.shape[0]
    r = r_ref[...]
    u = jnp.dot(r, wtop_ref[...], preferred_element_type=F32)
    lo = jnp.dot(r, wbot_ref[...], preferred_element_type=F32)
    bias = jnp.dot(pos_ref[...], w1_ref[...], preferred_element_type=F32)[0:1, :]
    bias4 = jnp.concatenate([bias[:, 0:CMP_HIDDEN]] * N_KV + [bias[:, CMP_HIDDEN:]] * N_KV, axis=1)
    hid = u + pltpu.roll(lo, nr - 1, 0) + bias4
    act = jax.nn.gelu(hid)
    for g in range(N_KV):
        ak = act[:, g * CMP_HIDDEN:(g + 1) * CMP_HIDDEN].astype(BF16)
        av = act[:, (N_KV + g) * CMP_HIDDEN:(N_KV + g + 1) * CMP_HIDDEN].astype(BF16)
        kc = jnp.dot(ak, w2k_ref[...], preferred_element_type=F32)
        ssq = jnp.sum(kc * kc, axis=-1, keepdims=True)
        kc_ref[g] = (kc * lax.rsqrt(ssq * (1.0 / HEAD_DIM) + EPS) * kgain_ref[...]).astype(BF16)
        vcT_ref[g] = lax.dot_general(w2vT_ref[...], av, (((1,), (1,)), ((), ())),
                                     preferred_element_type=F32).astype(BF16)


def _compress(kcv, wtop, wbot, pos, w1, w2k, w2vT, kgain):
    S = kcv.shape[0]
    nr = S // CMP_STRIDE
    r = kcv.reshape(nr, CMP_STRIDE * 2 * KV_W)
    vmem = pl.BlockSpec(memory_space=pltpu.VMEM)
    return pl.pallas_call(
        _compress_kernel,
        out_shape=(jax.ShapeDtypeStruct((N_KV, nr, LANES), BF16),
                   jax.ShapeDtypeStruct((N_KV, HEAD_DIM, nr), BF16)),
        in_specs=[vmem] * 8,
        out_specs=(vmem, vmem),
        compiler_params=pltpu.CompilerParams(vmem_limit_bytes=VMEM_LIMIT),
        name="compress",
    )(r, wtop, wbot, pos, w1, w2k, w2vT, kgain)


def _tile4(row):
    return jnp.concatenate([row] * GQA, axis=1)


def _attn_kernel(qT_ref, gT_ref, ks_ref, vsT_ref, kw_ref, vwT_ref, kc_ref, vcT_ref,
                 tw_ref, ts_ref, tc_ref, b31_ref, gain_ref, out_ref,
                 sc_ref, imp_ref, sel_ref, aoct_ref, qs_ref, sa_ref, sb_ref):
    ncr = kc_ref.shape[1]
    nsel = sel_ref.shape[1]
    noct = aoct_ref.shape[1] - 1
    width = GQA * LANES
    c = pl.program_id(0)
    t0 = c * Q_BLOCK
    tl = lax.broadcasted_iota(I32, (1, LANES), 1)
    cur = 2 * c + (tl >= SEL_LEN).astype(I32)
    zb = jnp.maximum(2 * c - 2, 0)
    zs = pl.multiple_of(zb * SEL_LEN, Q_BLOCK)
    ts_off = pl.multiple_of(zs - (t0 - Q_BLOCK), Q_BLOCK)
    zeros_q = jnp.zeros((HEAD_DIM, width), BF16)

    o_cmp, o_win, m_init, acc_init, gates = [], [], [], [], []
    for g in range(N_KV):
        qg = jnp.concatenate([qT_ref[(GQA * g + r) * HEAD_DIM:(GQA * g + r + 1) * HEAD_DIM, :]
                              for r in range(GQA)], axis=1)
        qc = jnp.concatenate([qg, zeros_q], axis=0)
        qw = qc if g == 0 else jnp.concatenate([zeros_q, qg], axis=0)
        b31row = b31_ref[g, 0:1, :]

        lo = pl.multiple_of(jnp.clip((8 * c - 16) // 16 * 16, 0, ncr - TC_WIN), 16)
        tc_off = pl.multiple_of(lo - (8 * c - 16) + TC_LEAD, 8)
        s = jnp.dot(kc_ref[g], qc, preferred_element_type=F32)
        row = lax.broadcasted_iota(I32, (ncr, width), 0)
        sc_ref[...] = s + jnp.where(row < lo, b31row, NEG)
        s_loc = jnp.dot(kc_ref[g, pl.ds(lo, TC_WIN), :], qc, preferred_element_type=F32)
        sc_ref[pl.ds(lo, TC_WIN), :] = s_loc + tc_ref[g, pl.ds(tc_off, TC_WIN), :]
        s = sc_ref[...]
        m = jnp.max(s, axis=0, keepdims=True)
        e = jnp.exp(s - m)
        l = jnp.sum(e, axis=0, keepdims=True)
        col_ok = _tile4(t0 + tl) >= CMP_LEN - 1
        p = e * jnp.where(col_ok, 1.0 / jnp.maximum(l, 1e-30), 0.0)
        o_cmp.append(jnp.dot(vcT_ref[g], p.astype(BF16), preferred_element_type=F32))
        imp = (p[:, 0:LANES] + p[:, LANES:2 * LANES]) + p[:, 2 * LANES:3 * LANES] + p[:, 3 * LANES:4 * LANES]

        imp_ref[0:8, :] = jnp.zeros((8, LANES), F32)
        imp_ref[8:8 + ncr, :] = imp
        imp_ref[8 + ncr:16 + ncr, :] = jnp.zeros((8, LANES), F32)
        taps = (0, -1, 1, 0, 2, 1, 3, 2)
        isel = jnp.zeros((nsel, LANES), F32)
        for off in taps:
            isel = isel + imp_ref[pl.ds(8 + off, nsel, stride=4), :]
        j = lax.broadcasted_iota(I32, (nsel, LANES), 0)
        forced = (j == 0) | (j == cur) | (j == cur - 1)
        score = jnp.where(forced, FORCE, isel)
        score = jnp.where(j > cur, NEG, score)

        def pick(_, carry, j=j):
            score, picked = carry
            mx = jnp.max(score, axis=0, keepdims=True)
            first = jnp.min(jnp.where(score == mx, j, nsel), axis=0, keepdims=True)
            hit = j == first
            return jnp.where(hit, -jnp.inf, score), jnp.where(hit, 1.0, picked)

        _, picked = lax.fori_loop(0, SEL_TOPK, pick, (score, jnp.zeros((nsel, LANES), F32)))
        chosen = (picked > 0.5) & (j <= cur)
        sel_ref[g] = jnp.where(chosen, 0.0, NEG)
        a_far = _tile4(jnp.where(chosen & (j < zb), 0.0, -MASK_BIG)).reshape(noct, BLOCKS_PER_CHUNK, width)
        b_hi = b31row.astype(BF16).astype(F32)
        erow = lax.broadcasted_iota(I32, (8, width), 0)
        extra = jnp.where(erow == 0, b_hi, jnp.where(erow == 1, b31row - b_hi, 0.0))
        aoct_ref[g, 0:noct] = jnp.concatenate(
            [a_far, jnp.broadcast_to(extra[None], (noct, 8, width))], axis=1).astype(BF16)
        aoct_ref[g, noct] = jnp.concatenate([jnp.full((8, width), -MASK_BIG, F32), extra], axis=0).astype(BF16)
        qs_ref[g, 0:HEAD_DIM, :] = qg
        qs_ref[g, HEAD_DIM + 16:LANES, :] = jnp.zeros((LANES - HEAD_DIM - 16, width), BF16)

        sz = jnp.dot(ks_ref[g, pl.ds(zs, 2 * Q_BLOCK), :], qc, preferred_element_type=F32)
        sz = sz + ts_ref[g, pl.ds(ts_off, 2 * Q_BLOCK), :]
        zone_sel = jnp.concatenate(
            [jnp.broadcast_to(_tile4(sel_ref[g, pl.ds(zb + i, 1), :]), (SEL_LEN, width)) for i in range(4)], axis=0)
        sz = sz + zone_sel
        mz = jnp.max(sz, axis=0, keepdims=True)
        pz = jnp.exp(sz - mz)
        m_init.append(mz)
        acc_init.append(jnp.dot(vsT_ref[g, :, pl.ds(zs, 2 * Q_BLOCK)], pz.astype(BF16), preferred_element_type=F32))

        ws = pl.multiple_of(jnp.maximum(t0 - WINDOW, 0), Q_BLOCK)
        tw_off = pl.multiple_of(ws - (t0 - WINDOW), Q_BLOCK)
        nw = WINDOW + Q_BLOCK
        sw = jnp.dot(kw_ref[pl.ds(ws, nw), :], qw, preferred_element_type=F32) + tw_ref[g, pl.ds(tw_off, nw), :]
        mw = jnp.max(sw, axis=0, keepdims=True)
        pw = jnp.exp(sw - mw)
        accw = jnp.dot(vwT_ref[g, :, pl.ds(ws, nw)], pw.astype(BF16), preferred_element_type=F32)
        o_win.append(accw[0:HEAD_DIM] * (1.0 / jnp.maximum(accw[HEAD_DIM:HEAD_DIM + 1], 1e-30)))
        gates.append([jnp.concatenate([gT_ref[br * N_HEADS + GQA * g + r:br * N_HEADS + GQA * g + r + 1, :]
                                       for r in range(GQA)], axis=1) for br in range(N_BRANCH)])

    nfar = (zb + BLOCKS_PER_CHUNK - 1) // BLOCKS_PER_CHUNK

    def score(u, buf_ref):
        ud = jnp.minimum(u, noct - 1)
        k0 = pl.multiple_of(ud * CHUNK, CHUNK)
        oct_id = jnp.where(u < nfar, u, noct)
        cms = []
        for g in range(N_KV):
            qs_ref[g, HEAD_DIM:HEAD_DIM + 16, :] = aoct_ref[g, oct_id]
            s = jnp.dot(ks_ref[g, pl.ds(k0, CHUNK), :], qs_ref[g], preferred_element_type=F32)
            buf_ref[g] = s
            cms.append(jnp.max(s, axis=0, keepdims=True))
        return cms

    def accumulate(u, buf_ref, state, cms):
        ud = jnp.minimum(u, noct - 1)
        k0 = pl.multiple_of(ud * CHUNK, CHUNK)
        out = []
        for g in range(N_KV):
            m, acc = state[2 * g], state[2 * g + 1]
            mn = jnp.maximum(m, cms[g])
            alpha = jnp.exp(m - mn)
            p = jnp.exp(buf_ref[g] - mn)
            acc = alpha * acc + jnp.dot(vsT_ref[g, :, pl.ds(k0, CHUNK)], p.astype(BF16), preferred_element_type=F32)
            out += [mn, acc]
        return out

    def far(i, carry):
        u = 2 * i
        cm_b = score(u + 1, sb_ref)
        state = accumulate(u, sa_ref, carry[0:4], carry[4:6])
        cm_a = score(u + 2, sa_ref)
        state = accumulate(u + 1, sb_ref, state, cm_b)
        return tuple(state + cm_a)

    first = score(0, sa_ref)
    npairs = jnp.maximum(nfar, 1) // 2
    carry = lax.fori_loop(0, npairs, far, (m_init[0], acc_init[0], m_init[1], acc_init[1], first[0], first[1]))
    carry = accumulate(2 * npairs, sa_ref, carry[0:4], carry[4:6])

    o_all, ssq = [], jnp.zeros((1, LANES), F32)
    for g in range(N_KV):
        acc = carry[2 * g + 1]
        o_sel = acc[0:HEAD_DIM] * (1.0 / jnp.maximum(acc[HEAD_DIM:HEAD_DIM + 1], 1e-30))
        o = gates[g][0] * o_cmp[g] + gates[g][1] * o_sel + gates[g][2] * o_win[g]
        o_all.append(o)
        cs = jnp.sum(o * o, axis=0, keepdims=True)
        ssq = ssq + ((cs[:, 0:LANES] + cs[:, LANES:2 * LANES]) + (cs[:, 2 * LANES:3 * LANES] + cs[:, 3 * LANES:]))
    inv = _tile4(lax.rsqrt(ssq * (1.0 / ATTN_W) + EPS))
    for g in range(N_KV):
        on = o_all[g] * inv * gain_ref[g]
        for k in range(GQA // 2):
            pair = jnp.concatenate([on[:, (2 * k) * LANES:(2 * k + 1) * LANES],
                                    on[:, (2 * k + 1) * LANES:(2 * k + 2) * LANES]], axis=0)
            col = (GQA * g + 2 * k) * HEAD_DIM
            out_ref[:, col:col + 2 * HEAD_DIM] = pair.T.astype(BF16)


def _attention(qT, gT, ks, vsT, kw, vwT, kc, vcT, tables, gain_b):
    S = qT.shape[1]
    nq = S // Q_BLOCK
    ncr = kc.shape[1]
    nsel = S // SEL_LEN
    noct = S // CHUNK
    width = GQA * LANES
    tw, ts, tc, b31 = tables
    vmem = pl.BlockSpec(memory_space=pltpu.VMEM)
    in_specs = [
        pl.BlockSpec((ATTN_W, Q_BLOCK), lambda c: (0, c)),
        pl.BlockSpec((N_BRANCH * N_HEADS, Q_BLOCK), lambda c: (0, c)),
    ] + [vmem] * 11
    return pl.pallas_call(
        _attn_kernel,
        grid=(nq,),
        in_specs=in_specs,
        out_specs=pl.BlockSpec((Q_BLOCK, ATTN_W), lambda c: (c, 0)),
        out_shape=jax.ShapeDtypeStruct((S, ATTN_W), BF16),
        scratch_shapes=[
            pltpu.VMEM((ncr, width), F32),
            pltpu.VMEM((ncr + 16, LANES), F32),
            pltpu.VMEM((N_KV, nsel, LANES), F32),
            pltpu.VMEM((N_KV, noct + 1, 16, width), BF16),
            pltpu.VMEM((N_KV, LANES, width), BF16),
            pltpu.VMEM((N_KV, CHUNK, width), F32),
            pltpu.VMEM((N_KV, CHUNK, width), F32),
        ],
        compiler_params=pltpu.CompilerParams(dimension_semantics=("arbitrary",), vmem_limit_bytes=VMEM_LIMIT),
        name="nsa_attention",
    )(qT, gT, ks, vsT, kw, vwT, kc, vcT, tw, ts, tc, b31, gain_b)


def _mix_and_norm(x_ref, attn_ref, conv_ref, wout_ref, g2_ref):
    x1 = x_ref[...] + jnp.dot(attn_ref[...], wout_ref[0:ATTN_W, :], preferred_element_type=F32) \
        + jnp.dot(conv_ref[...], wout_ref[ATTN_W:, :], preferred_element_type=F32)
    ms = jnp.mean(x1 * x1, axis=-1, keepdims=True)
    h2 = x1 * lax.rsqrt(ms + EPS) * g2_ref[...]
    return x1, h2


_FF_CHUNK = 1408


def _ffn_kernel(x_ref, attn_ref, conv_ref, wout_ref, g2_ref, wg_ref, wu_ref, wd_ref, out_ref):
    x1, h2 = _mix_and_norm(x_ref, attn_ref, conv_ref, wout_ref, g2_ref)
    h2 = h2.astype(BF16)
    a = jnp.dot(h2, wg_ref[...], preferred_element_type=F32)
    u = jnp.dot(h2, wu_ref[...], preferred_element_type=F32)
    y = (a * jax.nn.sigmoid(a) * u).astype(BF16)
    out_ref[...] = x1 + jnp.dot(y, wd_ref[...], preferred_element_type=F32)


_FFN_TM = 512


def _outproj_ffn(x2, attn_n, conv_n, wout, g2, wg, wu, wd):
    S = x2.shape[0]
    tm = _FFN_TM
    resident = pl.BlockSpec(memory_space=pltpu.VMEM)
    return pl.pallas_call(
        _ffn_kernel,
        grid=(S // tm,),
        in_specs=[
            pl.BlockSpec((tm, D_MODEL), lambda i: (i, 0)),
            pl.BlockSpec((tm, ATTN_W), lambda i: (i, 0)),
            pl.BlockSpec((tm, CONV_W), lambda i: (i, 0)),
            resident, resident, resident, resident, resident,
        ],
        out_specs=pl.BlockSpec((tm, D_MODEL), lambda i: (i, 0)),
        out_shape=jax.ShapeDtypeStruct((S, D_MODEL), F32),
        compiler_params=pltpu.CompilerParams(dimension_semantics=("arbitrary",), vmem_limit_bytes=VMEM_LIMIT),
        name="outproj_ffn",
    )(x2, attn_n, conv_n, wout, g2, wg, wu, wd)


def _router_kernel(x_ref, attn_ref, conv_ref, wout_ref, g2_ref, rw_ref, rb_ref, x1_ref, h2_ref, gates_ref):
    tm = x_ref.shape[0]
    x1, h2 = _mix_and_norm(x_ref, attn_ref, conv_ref, wout_ref, g2_ref)
    x1_ref[...] = x1
    h2b = h2.astype(BF16)
    h2_ref[...] = h2b
    lane = lax.broadcasted_iota(I32, (tm, LANES), 1)
    logits = jnp.dot(h2b, rw_ref[...], preferred_element_type=F32) + rb_ref[...]
    logits = jnp.where(lane < N_EXPERTS, logits, -jnp.inf)
    m1 = jnp.max(logits, axis=-1, keepdims=True)
    i1 = jnp.min(jnp.where(logits == m1, lane, LANES), axis=-1, keepdims=True)
    rest = jnp.where(lane == i1, -jnp.inf, logits)
    m2 = jnp.max(rest, axis=-1, keepdims=True)
    i2 = jnp.min(jnp.where(rest == m2, lane, LANES), axis=-1, keepdims=True)
    e2 = jnp.exp(m2 - m1)
    den = 1.0 + e2
    gates_ref[...] = jnp.where(lane == i1, 1.0 / den, jnp.where(lane == i2, e2 / den, 0.0))


def _outproj_router(x2, attn_n, conv_n, wout, g2, rw, rb):
    S = x2.shape[0]
    tm = _FFN_TM
    const = lambda shape: pl.BlockSpec(shape, lambda i: (0,) * len(shape))
    return pl.pallas_call(
        _router_kernel,
        grid=(S // tm,),
        in_specs=[
            pl.BlockSpec((tm, D_MODEL), lambda i: (i, 0)),
            pl.BlockSpec((tm, ATTN_W), lambda i: (i, 0)),
            pl.BlockSpec((tm, CONV_W), lambda i: (i, 0)),
            const(wout.shape), const((1, D_MODEL)), const(rw.shape), const((1, LANES)),
        ],
        out_specs=(pl.BlockSpec((tm, D_MODEL), lambda i: (i, 0)),
                   pl.BlockSpec((tm, D_MODEL), lambda i: (i, 0)),
                   pl.BlockSpec((tm, LANES), lambda i: (i, 0))),
        out_shape=(jax.ShapeDtypeStruct((S, D_MODEL), F32),
                   jax.ShapeDtypeStruct((S, D_MODEL), BF16),
                   jax.ShapeDtypeStruct((S, LANES), F32)),
        compiler_params=pltpu.CompilerParams(dimension_semantics=("arbitrary",), vmem_limit_bytes=VMEM_LIMIT),
        name="outproj_router",
    )(x2, attn_n, conv_n, wout, g2, rw, rb)


def _moe_kernel(h_ref, x1_ref, gates_ref, wg_ref, wu_ref, wd_ref, out_ref, acc_ref):
    tm = h_ref.shape[0]
    e = pl.program_id(1)
    f = pl.program_id(2)

    @pl.when((e == 0) & (f == 0))
    def _():
        acc_ref[...] = x1_ref[...]

    h = h_ref[...]
    a = jnp.dot(h, wg_ref[0], preferred_element_type=F32)
    u = jnp.dot(h, wu_ref[0], preferred_element_type=F32)
    lane = lax.broadcasted_iota(I32, (tm, LANES), 1)
    gate = jnp.sum(jnp.where(lane == e, gates_ref[...], 0.0), axis=-1, keepdims=True)
    y = (a * jax.nn.sigmoid(a) * u * gate).astype(BF16)
    acc_ref[...] += jnp.dot(y, wd_ref[0], preferred_element_type=F32)

    @pl.when((e == pl.num_programs(1) - 1) & (f == pl.num_programs(2) - 1))
    def _():
        out_ref[...] = acc_ref[...]


_MOE_TM = 1024


def _moe(h2, x1, gates, wg, wu, wd):
    S = h2.shape[0]
    tm = _MOE_TM
    nf = D_FF // _FF_CHUNK
    return pl.pallas_call(
        _moe_kernel,
        grid=(S // tm, N_EXPERTS, nf),
        in_specs=[
            pl.BlockSpec((tm, D_MODEL), lambda i, e, f: (i, 0)),
            pl.BlockSpec((tm, D_MODEL), lambda i, e, f: (i, 0)),
            pl.BlockSpec((tm, LANES), lambda i, e, f: (i, 0)),
            pl.BlockSpec((1, D_MODEL, _FF_CHUNK), lambda i, e, f: (e, 0, f)),
            pl.BlockSpec((1, D_MODEL, _FF_CHUNK), lambda i, e, f: (e, 0, f)),
            pl.BlockSpec((1, _FF_CHUNK, D_MODEL), lambda i, e, f: (e, f, 0)),
        ],
        out_specs=pl.BlockSpec((tm, D_MODEL), lambda i, e, f: (i, 0)),
        out_shape=jax.ShapeDtypeStruct((S, D_MODEL), F32),
        scratch_shapes=[pltpu.VMEM((tm, D_MODEL), F32)],
        compiler_params=pltpu.CompilerParams(dimension_semantics=("arbitrary", "arbitrary", "arbitrary"),
                                             vmem_limit_bytes=VMEM_LIMIT),
        name="moe_dense",
    )(h2, x1, gates, wg, wu, wd)


def _split_w_in(w):
    o = np.cumsum([0, ATTN_W] + [KV_W] * 6 + [N_BRANCH * N_HEADS] + [CONV_W] * 3)
    q, kc, vc, ksl, vsl, kwn, vwn, gts, cb, cc, ch = (w[:, o[i]:o[i + 1]] for i in range(11))
    perm = np.array([h * N_BRANCH + br for br in range(N_BRANCH) for h in range(N_HEADS)])
    wtok = jnp.concatenate([kc, vc, ksl, kwn, cb, cc, ch], axis=1).astype(BF16)
    feat = jnp.concatenate([q, vsl, vwn, gts[:, perm], jnp.zeros((D_MODEL, 8), w.dtype)], axis=1)
    return wtok, feat.T.astype(BF16)


def _expand_cmp_w1(w1k, w1v):
    def half(w, l0):
        wl = w.reshape(CMP_LEN, HEAD_DIM, CMP_HIDDEN)[l0:l0 + CMP_STRIDE]
        return wl
    tops = []
    for l0 in (0, CMP_STRIDE):
        blk = jnp.zeros((CMP_STRIDE, 2 * N_KV, HEAD_DIM, 2 * N_KV, CMP_HIDDEN), F32)
        for s, w in ((0, w1k), (1, w1v)):
            for g in range(N_KV):
                blk = blk.at[:, s * N_KV + g, :, s * N_KV + g, :].set(half(w, l0))
        tops.append(blk.reshape(CMP_STRIDE * 2 * KV_W, 2 * N_KV * CMP_HIDDEN).astype(BF16))
    return tops


def kernel(x, rel_bias, norm1, w_in, q_norm, k_norm, cmp_pos_k, cmp_pos_v, cmp_k_w1, cmp_k_w2, cmp_v_w1, cmp_v_w2,
           conv_w, attn_out_norm, conv_out_norm, w_out, norm2, ffn_w_gate, ffn_w_up, ffn_w_down, router_w, router_b,
           moe_w_gate, moe_w_up, moe_w_down):
    B, S, _ = x.shape
    assert B == 1 and S % CHUNK == 0 and S >= WINDOW + Q_BLOCK
    depth = norm1.shape[0]
    x2 = x.reshape(S, D_MODEL)
    tables = _bias_tables(rel_bias)
    for layer in range(depth):
        wtok, wfeat = _split_w_in(w_in[layer])
        kgain = jnp.concatenate([jnp.tile(k_norm[layer, 1], N_KV), jnp.tile(k_norm[layer, 2], N_KV)])[None, :]
        gq = jnp.broadcast_to(q_norm[layer][:, None], (HEAD_DIM, _IN_TM))
        convw = jnp.zeros((8, CONV_W), F32).at[0:CONV_K].set(conv_w[layer])
        qT, gT, vsT, vwT, ks, kw, kcv, conv_n = _in_proj(
            x2, norm1[layer][None, :], wtok, wfeat, kgain, gq, convw, conv_out_norm[layer][None, :])

        wtop, wbot = _expand_cmp_w1(cmp_k_w1[layer], cmp_v_w1[layer])
        pos = jnp.zeros((8, 2 * CMP_LEN * HEAD_DIM), F32)
        pos = pos.at[0, 0:CMP_LEN * HEAD_DIM].set(cmp_pos_k[layer].reshape(-1))
        pos = pos.at[0, CMP_LEN * HEAD_DIM:].set(cmp_pos_v[layer].reshape(-1))
        w1 = jnp.zeros((2 * CMP_LEN * HEAD_DIM, 2 * CMP_HIDDEN), F32)
        w1 = w1.at[0:CMP_LEN * HEAD_DIM, 0:CMP_HIDDEN].set(cmp_k_w1[layer])
        w1 = w1.at[CMP_LEN * HEAD_DIM:, CMP_HIDDEN:].set(cmp_v_w1[layer])
        w2k = jnp.zeros((CMP_HIDDEN, LANES), F32).at[:, 0:HEAD_DIM].set(cmp_k_w2[layer]).astype(BF16)
        w2vT = cmp_v_w2[layer].T.astype(BF16)
        kcgain = jnp.zeros((1, LANES), F32).at[0, 0:HEAD_DIM].set(k_norm[layer, 0])
        kc, vcT = _compress(kcv, wtop, wbot, pos, w1, w2k, w2vT, kcgain)

        gain_b = jnp.broadcast_to(attn_out_norm[layer].reshape(N_KV, GQA, HEAD_DIM).transpose(0, 2, 1)[:, :, :, None],
                                  (N_KV, HEAD_DIM, GQA, LANES)).reshape(N_KV, HEAD_DIM, GQA * LANES)
        attn_n = _attention(qT, gT, ks, vsT, kw, vwT, kc, vcT, tables, gain_b)

        wout = w_out[layer].astype(BF16)
        g2 = norm2[layer][None, :]
        i = layer // 2
        if layer % 2 == 0:
            x2 = _outproj_ffn(x2, attn_n, conv_n, wout, g2, ffn_w_gate[i].astype(BF16), ffn_w_up[i].astype(BF16),
                              ffn_w_down[i].astype(BF16))
        else:
            rw = jnp.zeros((D_MODEL, LANES), F32).at[:, 0:N_EXPERTS].set(router_w[i]).astype(BF16)
            rb = jnp.zeros((1, LANES), F32).at[0, 0:N_EXPERTS].set(router_b[i])
            x1, h2, gates = _outproj_router(x2, attn_n, conv_n, wout, g2, rw, rb)
            x2 = _moe(h2, x1, gates, moe_w_gate[i].astype(BF16), moe_w_up[i].astype(BF16), moe_w_down[i].astype(BF16))
    return x2.reshape(B, S, D_MODEL)
```

```python
import functools
import math

import numpy as np
import jax
import jax.numpy as jnp
from jax import lax
from jax.experimental import pallas as pl
from jax.experimental.pallas import tpu as pltpu

F32 = jnp.float32
BF16 = jnp.bfloat16
I32 = jnp.int32

D_MODEL = 1024
HEAD_DIM = 64
N_HEADS = 8
N_KV = 2
GQA = N_HEADS // N_KV
ATTN_W = N_HEADS * HEAD_DIM
KV_W = N_KV * HEAD_DIM
N_BRANCH = 3
CONV_W = 512
CONV_K = 3
CMP_LEN = 32
CMP_STRIDE = 16
CMP_HIDDEN = 128
SEL_LEN = 64
SEL_TOPK = 16
WINDOW = 512
Q_BLOCK = 128
N_BUCKETS = 32
MAX_DISTANCE = 128
D_FF = 2816
N_EXPERTS = 8
EPS = 1e-6
NEG = -1e30
FORCE = 1e9
MASK_BIG = 2.0 ** 60
SCALE = HEAD_DIM ** -0.5

LANES = 128
V_ROWS = 80
CHUNK = 512
BLOCKS_PER_CHUNK = CHUNK // SEL_LEN
TC_LEAD = 24
TC_ROWS = 88
TC_WIN = 48
VMEM_LIMIT = 56 * 1024 * 1024


def _bucket_np(dist):
    n = np.maximum(dist, 0)
    max_exact = N_BUCKETS // 2
    nf = np.maximum(n, max_exact).astype(np.float64)
    v = np.log(nf / max_exact) / math.log(MAX_DISTANCE / max_exact) * (N_BUCKETS - max_exact)
    frac = np.abs(v - np.round(v))
    assert np.all((frac > 1e-6) | (n <= max_exact) | (n >= MAX_DISTANCE)), "bucket boundary is precision dependent"
    large = np.minimum(max_exact + (v + 1e-9).astype(np.int32), N_BUCKETS - 1)
    return np.where(n < max_exact, n, large).astype(np.int32)


def _index_tables():
    tl = np.arange(Q_BLOCK)[None, :]
    r = np.arange(WINDOW + Q_BLOCK + WINDOW)[:, None]
    d = tl + WINDOW - r
    idx_w = np.where((d >= 0) & (d < WINDOW), _bucket_np(d), -1)
    r = np.arange(3 * Q_BLOCK)[:, None]
    d = tl + Q_BLOCK - r
    idx_s = np.where(d >= 0, _bucket_np(d), -1)
    r = np.arange(TC_ROWS)[:, None] - TC_LEAD
    d = tl - CMP_STRIDE * r + (CMP_STRIDE * 16 - (CMP_LEN - 1))
    idx_c = np.where((d >= 0) & (r < 32), _bucket_np(d), -1)
    return idx_w.astype(np.int32), idx_s.astype(np.int32), idx_c.astype(np.int32)


def _tables_kernel(rb_ref, iw_ref, is_ref, ic_ref, tw_ref, ts_ref, tc_ref, b31_ref):
    for h in range(N_HEADS):
        g, r = divmod(h, GQA)
        lanes = slice(r * LANES, (r + 1) * LANES)
        for idx_ref, out_ref in ((iw_ref, tw_ref), (is_ref, ts_ref), (ic_ref, tc_ref)):
            out_ref[g, :, lanes] = jnp.full(idx_ref.shape, NEG, F32)

            def body(b, carry, idx_ref=idx_ref, out_ref=out_ref, g=g, lanes=lanes, h=h):
                out_ref[g, :, lanes] = jnp.where(idx_ref[...] == b, rb_ref[b, h], out_ref[g, :, lanes])
                return carry

            lax.fori_loop(0, N_BUCKETS, body, 0)
        b31_ref[g, :, lanes] = jnp.full((8, LANES), rb_ref[N_BUCKETS - 1, h], F32)


def _bias_tables(rel_bias):
    idx_w, idx_s, idx_c = _index_tables()
    width = GQA * LANES
    out_shape = (
        jax.ShapeDtypeStruct((N_KV, idx_w.shape[0], width), F32),
        jax.ShapeDtypeStruct((N_KV, idx_s.shape[0], width), F32),
        jax.ShapeDtypeStruct((N_KV, idx_c.shape[0], width), F32),
        jax.ShapeDtypeStruct((N_KV, 8, width), F32),
    )
    vmem = pl.BlockSpec(memory_space=pltpu.VMEM)
    return pl.pallas_call(
        _tables_kernel,
        out_shape=out_shape,
        in_specs=[pl.BlockSpec(memory_space=pltpu.SMEM), vmem, vmem, vmem],
        out_specs=(vmem, vmem, vmem, vmem),
        name="bias_tables",
    )(rel_bias, jnp.asarray(idx_w), jnp.asarray(idx_s), jnp.asarray(idx_c))


def _in_proj_kernel(x_ref, g1_ref, wtok_ref, wfeat_ref, ind_ref, kgain_ref, aug_ref, gq_ref, convw_ref, cgain_ref,
                    qT_ref, gT_ref, vsT_ref, vwT_ref, ks_ref, kw_ref, kcv_ref, convn_ref, zs_ref):
    tm = x_ref.shape[0]
    i = pl.program_id(0)
    x = x_ref[...]
    ms = jnp.mean(x * x, axis=-1, keepdims=True)
    h = (x * lax.rsqrt(ms + EPS) * g1_ref[...]).astype(BF16)
    tok = jnp.dot(h, wtok_ref[...], preferred_element_type=F32)
    feat = lax.dot_general(wfeat_ref[...], h, (((1,), (1,)), ((), ())),
                           preferred_element_type=F32)

    kcv_ref[...] = tok[:, 0:2 * KV_W].astype(BF16)
    kk = tok[:, 2 * KV_W:4 * KV_W]
    ssq = jnp.dot(kk * kk, ind_ref[...], preferred_element_type=F32)
    kn = kk * lax.rsqrt(ssq * (1.0 / HEAD_DIM) + EPS) * kgain_ref[...]
    ksl = kn[:, 0:KV_W]
    lane = lax.broadcasted_iota(I32, (tm, LANES), 1)
    aug = aug_ref[...]
    ks_ref[0] = jnp.where(lane < HEAD_DIM, ksl, aug).astype(BF16)
    ks_ref[1] = jnp.where(lane < HEAD_DIM, pltpu.roll(ksl, HEAD_DIM, 1), aug).astype(BF16)
    kw_ref[...] = kn[:, KV_W:2 * KV_W].astype(BF16)

    c0 = 4 * KV_W
    cb = tok[:, c0:c0 + CONV_W]
    cc = tok[:, c0 + CONV_W:c0 + 2 * CONV_W]
    ch = tok[:, c0 + 2 * CONV_W:c0 + 3 * CONV_W]
    z = cc * ch

    @pl.when(i == 0)
    def _():
        zs_ref[0:8, :] = jnp.zeros((8, CONV_W), F32)

    zs_ref[8:8 + tm, :] = z
    z1 = zs_ref[7:7 + tm, :]
    z2 = zs_ref[6:6 + tm, :]
    w = convw_ref[...]
    y = w[0:1, :] * z2 + w[1:2, :] * z1 + w[2:3, :] * z
    zs_ref[0:8, :] = z[tm - 8:tm, :]
    oc = cb * y
    msc = jnp.mean(oc * oc, axis=-1, keepdims=True)
    convn_ref[...] = (oc * lax.rsqrt(msc + EPS) * cgain_ref[...]).astype(BF16)

    q = feat[0:ATTN_W].reshape(N_HEADS, HEAD_DIM, tm)
    qss = jnp.sum(q * q, axis=1, keepdims=True)
    qn = q * lax.rsqrt(qss * (1.0 / HEAD_DIM) + EPS) * gq_ref[...][None]
    qT_ref[...] = (qn * SCALE).reshape(ATTN_W, tm).astype(BF16)
    ones_rows = (lax.broadcasted_iota(I32, (V_ROWS - HEAD_DIM, tm), 0) == 0).astype(BF16)
    for g in range(N_KV):
        r0 = ATTN_W + g * HEAD_DIM
        vsT_ref[g, 0:HEAD_DIM, :] = feat[r0:r0 + HEAD_DIM].astype(BF16)
        vsT_ref[g, HEAD_DIM:V_ROWS, :] = ones_rows
        r1 = ATTN_W + KV_W + g * HEAD_DIM
        vwT_ref[g, 0:HEAD_DIM, :] = feat[r1:r1 + HEAD_DIM].astype(BF16)
        vwT_ref[g, HEAD_DIM:V_ROWS, :] = ones_rows
    g0 = ATTN_W + 2 * KV_W
    gT_ref[...] = jax.nn.sigmoid(feat[g0:g0 + N_BRANCH * N_HEADS])


_IN_TM = 512


def _in_proj(x2, g1, wtok, wfeat, kgain, gq, convw, cgain):
    S = x2.shape[0]
    tm = _IN_TM
    nt = S // tm
    ind = np.kron(np.eye(2 * N_KV, dtype=np.float32), np.ones((HEAD_DIM, HEAD_DIM), np.float32))
    aug = np.zeros((tm, LANES), np.float32)
    blk = (np.arange(tm) // SEL_LEN) % BLOCKS_PER_CHUNK
    aug[np.arange(tm), HEAD_DIM + blk] = 1.0
    aug[:, HEAD_DIM + BLOCKS_PER_CHUNK:HEAD_DIM + BLOCKS_PER_CHUNK + 2] = 1.0
    const = lambda shape: pl.BlockSpec(shape, lambda i: (0,) * len(shape))
    out_shape = (
        jax.ShapeDtypeStruct((ATTN_W, S), BF16),
        jax.ShapeDtypeStruct((N_BRANCH * N_HEADS, S), F32),
        jax.ShapeDtypeStruct((N_KV, V_ROWS, S), BF16),
        jax.ShapeDtypeStruct((N_KV, V_ROWS, S), BF16),
        jax.ShapeDtypeStruct((N_KV, S, LANES), BF16),
        jax.ShapeDtypeStruct((S, LANES), BF16),
        jax.ShapeDtypeStruct((S, 2 * KV_W), BF16),
        jax.ShapeDtypeStruct((S, CONV_W), BF16),
    )
    out_specs = (
        pl.BlockSpec((ATTN_W, tm), lambda i: (0, i)),
        pl.BlockSpec((N_BRANCH * N_HEADS, tm), lambda i: (0, i)),
        pl.BlockSpec((N_KV, V_ROWS, tm), lambda i: (0, 0, i)),
        pl.BlockSpec((N_KV, V_ROWS, tm), lambda i: (0, 0, i)),
        pl.BlockSpec((N_KV, tm, LANES), lambda i: (0, i, 0)),
        pl.BlockSpec((tm, LANES), lambda i: (i, 0)),
        pl.BlockSpec((tm, 2 * KV_W), lambda i: (i, 0)),
        pl.BlockSpec((tm, CONV_W), lambda i: (i, 0)),
    )
    in_specs = [
        pl.BlockSpec((tm, D_MODEL), lambda i: (i, 0)),
        const((1, D_MODEL)),
        const(wtok.shape),
        const(wfeat.shape),
        const(ind.shape),
        const((1, 2 * KV_W)),
        const(aug.shape),
        const((HEAD_DIM, tm)),
        const((8, CONV_W)),
        const((1, CONV_W)),
    ]
    return pl.pallas_call(
        _in_proj_kernel,
        grid=(nt,),
        in_specs=in_specs,
        out_specs=out_specs,
        out_shape=out_shape,
        scratch_shapes=[pltpu.VMEM((tm + 8, CONV_W), F32)],
        compiler_params=pltpu.CompilerParams(dimension_semantics=("arbitrary",), vmem_limit_bytes=VMEM_LIMIT),
        name="in_proj",
    )(x2, g1, wtok, wfeat, jnp.asarray(ind), kgain, jnp.asarray(aug), gq, convw, cgain)


def _compress_kernel(r_ref, wtop_ref, wbot_ref, pos_ref, w1_ref, w2k_ref, w2vT_ref, kgain_ref, kc_ref, vcT_ref):
    nr = r_ref.shape[0]
    r = r_ref[...]
    u = jnp.dot(r, wtop_ref[...], preferred_element_type=F32)
    lo = jnp.dot(r, wbot_ref[...], preferred_element_type=F32)
    bias = jnp.dot(pos_ref[...], w1_ref[...], preferred_element_type=F32)[0:1, :]
    bias4 = jnp.concatenate([bias[:, 0:CMP_HIDDEN]] * N_KV + [bias[:, CMP_HIDDEN:]] * N_KV, axis=1)
    hid = u + pltpu.roll(lo, nr - 1, 0) + bias4
    act = jax.nn.gelu(hid)
    for g in range(N_KV):
        ak = act[:, g * CMP_HIDDEN:(g + 1) * CMP_HIDDEN].astype(BF16)
        av = act[:, (N_KV + g) * CMP_HIDDEN:(N_KV + g + 1) * CMP_HIDDEN].astype(BF16)
        kc = jnp.dot(ak, w2k_ref[...], preferred_element_type=F32)
        ssq = jnp.sum(kc * kc, axis=-1, keepdims=True)
        kc_ref[g] = (kc * lax.rsqrt(ssq * (1.0 / HEAD_DIM) + EPS) * kgain_ref[...]).astype(BF16)
        vcT_ref[g] = lax.dot_general(w2vT_ref[...], av, (((1,), (1,)), ((), ())),
                                     preferred_element_type=F32).astype(BF16)


def _compress(kcv, wtop, wbot, pos, w1, w2k, w2vT, kgain):
    S = kcv.shape[0]
    nr = S // CMP_STRIDE
    r = kcv.reshape(nr, CMP_STRIDE * 2 * KV_W)
    vmem = pl.BlockSpec(memory_space=pltpu.VMEM)
    return pl.pallas_call(
        _compress_kernel,
        out_shape=(jax.ShapeDtypeStruct((N_KV, nr, LANES), BF16),
                   jax.ShapeDtypeStruct((N_KV, HEAD_DIM, nr), BF16)),
        in_specs=[vmem] * 8,
        out_specs=(vmem, vmem),
        compiler_params=pltpu.CompilerParams(vmem_limit_bytes=VMEM_LIMIT),
        name="compress",
    )(r, wtop, wbot, pos, w1, w2k, w2vT, kgain)


def _tile4(row):
    return jnp.concatenate([row] * GQA, axis=1)


def _attn_kernel(qT_ref, gT_ref, ks_ref, vsT_ref, kw_ref, vwT_ref, kc_ref, vcT_ref,
                 tw_ref, ts_ref, tc_ref, b31_ref, gain_ref, out_ref,
                 sc_ref, imp_ref, sel_ref, aoct_ref, qs_ref, sa_ref, sb_ref):
    ncr = kc_ref.shape[1]
    nsel = sel_ref.shape[1]
    noct = aoct_ref.shape[1] - 1
    width = GQA * LANES
    c = pl.program_id(0)
    t0 = c * Q_BLOCK
    tl = lax.broadcasted_iota(I32, (1, LANES), 1)
    cur = 2 * c + (tl >= SEL_LEN).astype(I32)
    zb = jnp.maximum(2 * c - 2, 0)
    zs = pl.multiple_of(zb * SEL_LEN, Q_BLOCK)
    ts_off = pl.multiple_of(zs - (t0 - Q_BLOCK), Q_BLOCK)
    zeros_q = jnp.zeros((HEAD_DIM, width), BF16)

    o_cmp, o_win, m_init, acc_init, gates = [], [], [], [], []
    for g in range(N_KV):
        qg = jnp.concatenate([qT_ref[(GQA * g + r) * HEAD_DIM:(GQA * g + r + 1) * HEAD_DIM, :]
                              for r in range(GQA)], axis=1)
        qc = jnp.concatenate([qg, zeros_q], axis=0)
        qw = qc if g == 0 else jnp.concatenate([zeros_q, qg], axis=0)
        b31row = b31_ref[g, 0:1, :]

        lo = pl.multiple_of(jnp.clip((8 * c - 16) // 16 * 16, 0, ncr - TC_WIN), 16)
        tc_off = pl.multiple_of(lo - (8 * c - 16) + TC_LEAD, 8)
        s = jnp.dot(kc_ref[g], qc, preferred_element_type=F32)
        row = lax.broadcasted_iota(I32, (ncr, width), 0)
        sc_ref[...] = s + jnp.where(row < lo, b31row, NEG)
        s_loc = jnp.dot(kc_ref[g, pl.ds(lo, TC_WIN), :], qc, preferred_element_type=F32)
        sc_ref[pl.ds(lo, TC_WIN), :] = s_loc + tc_ref[g, pl.ds(tc_off, TC_WIN), :]
        s = sc_ref[...]
        m = jnp.max(s, axis=0, keepdims=True)
        e = jnp.exp(s - m)
        l = jnp.sum(e, axis=0, keepdims=True)
        col_ok = _tile4(t0 + tl) >= CMP_LEN - 1
        p = e * jnp.where(col_ok, 1.0 / jnp.maximum(l, 1e-30), 0.0)
        o_cmp.append(jnp.dot(vcT_ref[g], p.astype(BF16), preferred_element_type=F32))
        imp = (p[:, 0:LANES] + p[:, LANES:2 * LANES]) + p[:, 2 * LANES:3 * LANES] + p[:, 3 * LANES:4 * LANES]

        imp_ref[0:8, :] = jnp.zeros((8, LANES), F32)
        imp_ref[8:8 + ncr, :] = imp
        imp_ref[8 + ncr:16 + ncr, :] = jnp.zeros((8, LANES), F32)
        taps = (0, -1, 1, 0, 2, 1, 3, 2)
        isel = jnp.zeros((nsel, LANES), F32)
        for off in taps:
            isel = isel + imp_ref[pl.ds(8 + off, nsel, stride=4), :]
        j = lax.broadcasted_iota(I32, (nsel, LANES), 0)
        forced = (j == 0) | (j == cur) | (j == cur - 1)
        score = jnp.where(forced, FORCE, isel)
        score = jnp.where(j > cur, NEG, score)

        def pick(_, carry, j=j):
            score, picked = carry
            mx = jnp.max(score, axis=0, keepdims=True)
            first = jnp.min(jnp.where(score == mx, j, nsel), axis=0, keepdims=True)
            hit = j == first
            return jnp.where(hit, -jnp.inf, score), jnp.where(hit, 1.0, picked)

        _, picked = lax.fori_loop(0, SEL_TOPK, pick, (score, jnp.zeros((nsel, LANES), F32)))
        chosen = (picked > 0.5) & (j <= cur)
        sel_ref[g] = jnp.where(chosen, 0.0, NEG)
        a_far = _tile4(jnp.where(chosen & (j < zb), 0.0, -MASK_BIG)).reshape(noct, BLOCKS_PER_CHUNK, width)
        b_hi = b31row.astype(BF16).astype(F32)
        erow = lax.broadcasted_iota(I32, (8, width), 0)
        extra = jnp.where(erow == 0, b_hi, jnp.where(erow == 1, b31row - b_hi, 0.0))
        aoct_ref[g, 0:noct] = jnp.concatenate(
            [a_far, jnp.broadcast_to(extra[None], (noct, 8, width))], axis=1).astype(BF16)
        aoct_ref[g, noct] = jnp.concatenate([jnp.full((8, width), -MASK_BIG, F32), extra], axis=0).astype(BF16)
        qs_ref[g, 0:HEAD_DIM, :] = qg
        qs_ref[g, HEAD_DIM + 16:LANES, :] = jnp.zeros((LANES - HEAD_DIM - 16, width), BF16)

        sz = jnp.dot(ks_ref[g, pl.ds(zs, 2 * Q_BLOCK), :], qc, preferred_element_type=F32)
        sz = sz + ts_ref[g, pl.ds(ts_off, 2 * Q_BLOCK), :]
        zone_sel = jnp.concatenate(
            [jnp.broadcast_to(_tile4(sel_ref[g, pl.ds(zb + i, 1), :]), (SEL_LEN, width)) for i in range(4)], axis=0)
        sz = sz + zone_sel
        mz = jnp.max(sz, axis=0, keepdims=True)
        pz = jnp.exp(sz - mz)
        m_init.append(mz)
        acc_init.append(jnp.dot(vsT_ref[g, :, pl.ds(zs, 2 * Q_BLOCK)], pz.astype(BF16), preferred_element_type=F32))

        ws = pl.multiple_of(jnp.maximum(t0 - WINDOW, 0), Q_BLOCK)
        tw_off = pl.multiple_of(ws - (t0 - WINDOW), Q_BLOCK)
        nw = WINDOW + Q_BLOCK
        sw = jnp.dot(kw_ref[pl.ds(ws, nw), :], qw, preferred_element_type=F32) + tw_ref[g, pl.ds(tw_off, nw), :]
        mw = jnp.max(sw, axis=0, keepdims=True)
        pw = jnp.exp(sw - mw)
        accw = jnp.dot(vwT_ref[g, :, pl.ds(ws, nw)], pw.astype(BF16), preferred_element_type=F32)
        o_win.append(accw[0:HEAD_DIM] * (1.0 / jnp.maximum(accw[HEAD_DIM:HEAD_DIM + 1], 1e-30)))
        gates.append([jnp.concatenate([gT_ref[br * N_HEADS + GQA * g + r:br * N_HEADS + GQA * g + r + 1, :]
                                       for r in range(GQA)], axis=1) for br in range(N_BRANCH)])

    nfar = (zb + BLOCKS_PER_CHUNK - 1) // BLOCKS_PER_CHUNK

    def score(u, buf_ref):
        ud = jnp.minimum(u, noct - 1)
        k0 = pl.multiple_of(ud * CHUNK, CHUNK)
        oct_id = jnp.where(u < nfar, u, noct)
        cms = []
        for g in range(N_KV):
            qs_ref[g, HEAD_DIM:HEAD_DIM + 16, :] = aoct_ref[g, oct_id]
            s = jnp.dot(ks_ref[g, pl.ds(k0, CHUNK), :], qs_ref[g], preferred_element_type=F32)
            buf_ref[g] = s
            cms.append(jnp.max(s, axis=0, keepdims=True))
        return cms

    def accumulate(u, buf_ref, state, cms):
        ud = jnp.minimum(u, noct - 1)
        k0 = pl.multiple_of(ud * CHUNK, CHUNK)
        out = []
        for g in range(N_KV):
            m, acc = state[2 * g], state[2 * g + 1]
            mn = jnp.maximum(m, cms[g])
            alpha = jnp.exp(m - mn)
            p = jnp.exp(buf_ref[g] - mn)
            acc = alpha * acc + jnp.dot(vsT_ref[g, :, pl.ds(k0, CHUNK)], p.astype(BF16), preferred_element_type=F32)
            out += [mn, acc]
        return out

    def far(i, carry):
        u = 2 * i
        cm_b = score(u + 1, sb_ref)
        state = accumulate(u, sa_ref, carry[0:4], carry[4:6])
        cm_a = score(u + 2, sa_ref)
        state = accumulate(u + 1, sb_ref, state, cm_b)
        return tuple(state + cm_a)

    first = score(0, sa_ref)
    npairs = jnp.maximum(nfar, 1) // 2
    carry = lax.fori_loop(0, npairs, far, (m_init[0], acc_init[0], m_init[1], acc_init[1], first[0], first[1]))
    carry = accumulate(2 * npairs, sa_ref, carry[0:4], carry[4:6])

    o_all, ssq = [], jnp.zeros((1, LANES), F32)
    for g in range(N_KV):
        acc = carry[2 * g + 1]
        o_sel = acc[0:HEAD_DIM] * (1.0 / jnp.maximum(acc[HEAD_DIM:HEAD_DIM + 1], 1e-30))
        o = gates[g][0] * o_cmp[g] + gates[g][1] * o_sel + gates[g][2] * o_win[g]
        o_all.append(o)
        cs = jnp.sum(o * o, axis=0, keepdims=True)
        ssq = ssq + ((cs[:, 0:LANES] + cs[:, LANES:2 * LANES]) + (cs[:, 2 * LANES:3 * LANES] + cs[:, 3 * LANES:]))
    inv = _tile4(lax.rsqrt(ssq * (1.0 / ATTN_W) + EPS))
    for g in range(N_KV):
        on = o_all[g] * inv * gain_ref[g]
        for k in range(GQA // 2):
            pair = jnp.concatenate([on[:, (2 * k) * LANES:(2 * k + 1) * LANES],
                                    on[:, (2 * k + 1) * LANES:(2 * k + 2) * LANES]], axis=0)
            col = (GQA * g + 2 * k) * HEAD_DIM
            out_ref[:, col:col + 2 * HEAD_DIM] = pair.T.astype(BF16)


def _attention(qT, gT, ks, vsT, kw, vwT, kc, vcT, tables, gain_b):
    S = qT.shape[1]
    nq = S // Q_BLOCK
    ncr = kc.shape[1]
    nsel = S // SEL_LEN
    noct = S // CHUNK
    width = GQA * LANES
    tw, ts, tc, b31 = tables
    vmem = pl.BlockSpec(memory_space=pltpu.VMEM)
    in_specs = [
        pl.BlockSpec((ATTN_W, Q_BLOCK), lambda c: (0, c)),
        pl.BlockSpec((N_BRANCH * N_HEADS, Q_BLOCK), lambda c: (0, c)),
    ] + [vmem] * 11
    return pl.pallas_call(
        _attn_kernel,
        grid=(nq,),
        in_specs=in_specs,
        out_specs=pl.BlockSpec((Q_BLOCK, ATTN_W), lambda c: (c, 0)),
        out_shape=jax.ShapeDtypeStruct((S, ATTN_W), BF16),
        scratch_shapes=[
            pltpu.VMEM((ncr, width), F32),
            pltpu.VMEM((ncr + 16, LANES), F32),
            pltpu.VMEM((N_KV, nsel, LANES), F32),
            pltpu.VMEM((N_KV, noct + 1, 16, width), BF16),
            pltpu.VMEM((N_KV, LANES, width), BF16),
            pltpu.VMEM((N_KV, CHUNK, width), F32),
            pltpu.VMEM((N_KV, CHUNK, width), F32),
        ],
        compiler_params=pltpu.CompilerParams(dimension_semantics=("arbitrary",), vmem_limit_bytes=VMEM_LIMIT),
        name="nsa_attention",
    )(qT, gT, ks, vsT, kw, vwT, kc, vcT, tw, ts, tc, b31, gain_b)


def _mix_and_norm(x_ref, attn_ref, conv_ref, wout_ref, g2_ref):
    x1 = x_ref[...] + jnp.dot(attn_ref[...], wout_ref[0:ATTN_W, :], preferred_element_type=F32) \
        + jnp.dot(conv_ref[...], wout_ref[ATTN_W:, :], preferred_element_type=F32)
    ms = jnp.mean(x1 * x1, axis=-1, keepdims=True)
    h2 = x1 * lax.rsqrt(ms + EPS) * g2_ref[...]
    return x1, h2


def _ffn_kernel(x_ref, attn_ref, conv_ref, wout_ref, g2_ref, wg_ref, wu_ref, wd_ref, out_ref):
    x1, h2 = _mix_and_norm(x_ref, attn_ref, conv_ref, wout_ref, g2_ref)
    h2 = h2.astype(BF16)
    a = jnp.dot(h2, wg_ref[...], preferred_element_type=F32)
    u = jnp.dot(h2, wu_ref[...], preferred_element_type=F32)
    y = (a * jax.nn.sigmoid(a) * u).astype(BF16)
    out_ref[...] = x1 + jnp.dot(y, wd_ref[...], preferred_element_type=F32)


_FFN_TM = 512
_FF_CHUNK = D_FF // 2


def _outproj_ffn(x2, attn_n, conv_n, wout, g2, wg, wu, wd):
    S = x2.shape[0]
    tm = _FFN_TM
    resident = pl.BlockSpec(memory_space=pltpu.VMEM)
    return pl.pallas_call(
        _ffn_kernel,
        grid=(S // tm,),
        in_specs=[
            pl.BlockSpec((tm, D_MODEL), lambda i: (i, 0)),
            pl.BlockSpec((tm, ATTN_W), lambda i: (i, 0)),
            pl.BlockSpec((tm, CONV_W), lambda i: (i, 0)),
            resident, resident, resident, resident, resident,
        ],
        out_specs=pl.BlockSpec((tm, D_MODEL), lambda i: (i, 0)),
        out_shape=jax.ShapeDtypeStruct((S, D_MODEL), F32),
        compiler_params=pltpu.CompilerParams(dimension_semantics=("arbitrary",), vmem_limit_bytes=VMEM_LIMIT),
        name="outproj_ffn",
    )(x2, attn_n, conv_n, wout, g2, wg, wu, wd)


def _router_kernel(x_ref, attn_ref, conv_ref, wout_ref, g2_ref, rw_ref, rb_ref, tri_ref,
                   x1_ref, h2_ref, route_ref, cnt_ref, run_ref):
    tm = x_ref.shape[0]
    i = pl.program_id(0)

    @pl.when(i == 0)
    def _():
        run_ref[...] = jnp.zeros(run_ref.shape, F32)

    x1, h2 = _mix_and_norm(x_ref, attn_ref, conv_ref, wout_ref, g2_ref)
    x1_ref[...] = x1
    h2b = h2.astype(BF16)
    h2_ref[...] = h2b.astype(F32)
    lane = lax.broadcasted_iota(I32, (tm, LANES), 1)
    logits = jnp.dot(h2b, rw_ref[...], preferred_element_type=F32) + rb_ref[...]
    logits = jnp.where(lane < N_EXPERTS, logits, -jnp.inf)
    m1 = jnp.max(logits, axis=-1, keepdims=True)
    i1 = jnp.min(jnp.where(logits == m1, lane, LANES), axis=-1, keepdims=True)
    rest = jnp.where(lane == i1, -jnp.inf, logits)
    m2 = jnp.max(rest, axis=-1, keepdims=True)
    i2 = jnp.min(jnp.where(rest == m2, lane, LANES), axis=-1, keepdims=True)
    e2 = jnp.exp(m2 - m1)
    den = 1.0 + e2
    oh1 = (lane == i1).astype(F32)
    oh2 = (lane == i2).astype(F32)
    both = oh1 + oh2
    before = run_ref[0:1, :] + jnp.dot(tri_ref[...], both.astype(BF16), preferred_element_type=F32)
    rank1 = jnp.sum(before * oh1, axis=-1, keepdims=True)
    rank2 = jnp.sum(before * oh2, axis=-1, keepdims=True)
    fields = (i1.astype(F32), i2.astype(F32), rank1, rank2, 1.0 / den, e2 / den)
    route = jnp.zeros((tm, LANES), F32)
    for k, v in enumerate(fields):
        route = jnp.where(lane == k, v, route)
    route_ref[...] = route
    run_ref[...] = run_ref[...] + jnp.sum(both, axis=0, keepdims=True)
    cnt_ref[...] = run_ref[...]


def _outproj_router(x2, attn_n, conv_n, wout, g2, rw, rb):
    S = x2.shape[0]
    tm = _FFN_TM
    tri = np.tril(np.ones((tm, tm), np.float32), -1)
    const = lambda shape: pl.BlockSpec(shape, lambda i: (0,) * len(shape))
    return pl.pallas_call(
        _router_kernel,
        grid=(S // tm,),
        in_specs=[
            pl.BlockSpec((tm, D_MODEL), lambda i: (i, 0)),
            pl.BlockSpec((tm, ATTN_W), lambda i: (i, 0)),
            pl.BlockSpec((tm, CONV_W), lambda i: (i, 0)),
            const(wout.shape), const((1, D_MODEL)), const(rw.shape), const((1, LANES)), const((tm, tm)),
        ],
        out_specs=(pl.BlockSpec((tm, D_MODEL), lambda i: (i, 0)),
                   pl.BlockSpec((tm, D_MODEL), lambda i: (i, 0)),
                   pl.BlockSpec((tm, LANES), lambda i: (i, 0)),
                   const((8, LANES))),
        out_shape=(jax.ShapeDtypeStruct((S, D_MODEL), F32),
                   jax.ShapeDtypeStruct((S, D_MODEL), F32),
                   jax.ShapeDtypeStruct((S, LANES), F32),
                   jax.ShapeDtypeStruct((8, LANES), F32)),
        scratch_shapes=[pltpu.VMEM((8, LANES), F32)],
        compiler_params=pltpu.CompilerParams(dimension_semantics=("arbitrary",), vmem_limit_bytes=VMEM_LIMIT),
        name="outproj_router",
    )(x2, attn_n, conv_n, wout, g2, rw, rb, jnp.asarray(tri, BF16))


_ROW_TM = 256
_EXP_TM = 512
_DMA_UNROLL = 8


def _row_copy(src_ref, src_row, dst_ref, dst_row, sem):
    return pltpu.make_async_copy(src_ref.at[pl.ds(src_row, 1), :], dst_ref.at[pl.ds(dst_row, 1), :], sem)


def _dispatch_kernel(pos_ref, h_ref, zero_ref, xs_ref, sem):
    del zero_ref
    tm = h_ref.shape[0]

    def issue(t, carry):
        for k in range(2):
            _row_copy(h_ref, t, xs_ref, pos_ref[0, 0, 2 * t + k], sem).start()
        return carry

    lax.fori_loop(0, tm, issue, 0, unroll=_DMA_UNROLL)

    def drain(t, carry):
        for k in range(2):
            _row_copy(h_ref, 0, xs_ref, 0, sem).wait()
        return carry

    lax.fori_loop(0, tm, drain, 0, unroll=_DMA_UNROLL)


def _dispatch(pos3, h2, n_rows):
    S = h2.shape[0]
    tm = _ROW_TM
    zeros = jnp.zeros((n_rows, D_MODEL), F32)
    return pl.pallas_call(
        _dispatch_kernel,
        grid=(S // tm,),
        in_specs=[
            pl.BlockSpec((1, 1, 2 * tm), lambda i: (i, 0, 0), memory_space=pltpu.SMEM),
            pl.BlockSpec((tm, D_MODEL), lambda i: (i, 0)),
            pl.BlockSpec(memory_space=pl.ANY),
        ],
        out_specs=pl.BlockSpec(memory_space=pl.ANY),
        out_shape=jax.ShapeDtypeStruct((n_rows, D_MODEL), F32),
        scratch_shapes=[pltpu.SemaphoreType.DMA],
        input_output_aliases={2: 0},
        compiler_params=pltpu.CompilerParams(dimension_semantics=("arbitrary",), vmem_limit_bytes=VMEM_LIMIT),
        name="moe_dispatch",
    )(pos3, h2, zeros)


def _experts_kernel(te_ref, tb_ref, nt_ref, xs_ref, wg_ref, wu_ref, wd_ref, ys_ref, acc_ref):
    i = pl.program_id(0)
    f = pl.program_id(1)

    @pl.when(i < nt_ref[0])
    def _():
        x = xs_ref[...].astype(BF16)
        a = jnp.dot(x, wg_ref[0], preferred_element_type=F32)
        u = jnp.dot(x, wu_ref[0], preferred_element_type=F32)
        y = (a * jax.nn.sigmoid(a) * u).astype(BF16)
        part = jnp.dot(y, wd_ref[0], preferred_element_type=F32)

        @pl.when(f == 0)
        def _():
            acc_ref[...] = part

        @pl.when(f == pl.num_programs(1) - 1)
        def _():
            ys_ref[...] = acc_ref[...] + part

    @pl.when(i >= nt_ref[0])
    def _():
        ys_ref[...] = jnp.zeros(ys_ref.shape, F32)


def _experts(tile_e, tile_b, n_tiles, xs, wg, wu, wd):
    n_rows = xs.shape[0]
    tm = _EXP_TM
    nf = D_FF // _FF_CHUNK
    assert nf == 2
    grid_spec = pltpu.PrefetchScalarGridSpec(
        num_scalar_prefetch=3,
        grid=(n_rows // tm, nf),
        in_specs=[
            pl.BlockSpec((tm, D_MODEL), lambda i, f, te, tb, nt: (tb[i], 0)),
            pl.BlockSpec((1, D_MODEL, _FF_CHUNK), lambda i, f, te, tb, nt: (te[i], 0, f)),
            pl.BlockSpec((1, D_MODEL, _FF_CHUNK), lambda i, f, te, tb, nt: (te[i], 0, f)),
            pl.BlockSpec((1, _FF_CHUNK, D_MODEL), lambda i, f, te, tb, nt: (te[i], f, 0)),
        ],
        out_specs=pl.BlockSpec((tm, D_MODEL), lambda i, f, te, tb, nt: (tb[i], 0)),
        scratch_shapes=[pltpu.VMEM((tm, D_MODEL), F32)],
    )
    return pl.pallas_call(
        _experts_kernel,
        grid_spec=grid_spec,
        out_shape=jax.ShapeDtypeStruct((n_rows, D_MODEL), F32),
        compiler_params=pltpu.CompilerParams(dimension_semantics=("arbitrary", "arbitrary"),
                                             vmem_limit_bytes=VMEM_LIMIT),
        name="moe_experts",
    )(tile_e, tile_b, n_tiles, xs, wg, wu, wd)


def _combine_kernel(pos_ref, x1_ref, route_ref, ys_ref, out_ref, y1_ref, y2_ref, sem):
    tm = x1_ref.shape[0]

    def issue(t, carry):
        _row_copy(ys_ref, pos_ref[0, 0, 2 * t], y1_ref, t, sem).start()
        _row_copy(ys_ref, pos_ref[0, 0, 2 * t + 1], y2_ref, t, sem).start()
        return carry

    lax.fori_loop(0, tm, issue, 0, unroll=_DMA_UNROLL)

    def drain(t, carry):
        _row_copy(ys_ref, 0, y1_ref, 0, sem).wait()
        _row_copy(ys_ref, 0, y2_ref, 0, sem).wait()
        return carry

    lax.fori_loop(0, tm, drain, 0, unroll=_DMA_UNROLL)
    lane = lax.broadcasted_iota(I32, (tm, LANES), 1)
    route = route_ref[...]
    w1 = jnp.sum(jnp.where(lane == 4, route, 0.0), axis=-1, keepdims=True)
    w2 = jnp.sum(jnp.where(lane == 5, route, 0.0), axis=-1, keepdims=True)
    out_ref[...] = x1_ref[...] + (y1_ref[...] * w1 + y2_ref[...] * w2)


def _combine(pos3, x1, route, ys):
    S = x1.shape[0]
    tm = _ROW_TM
    return pl.pallas_call(
        _combine_kernel,
        grid=(S // tm,),
        in_specs=[
            pl.BlockSpec((1, 1, 2 * tm), lambda i: (i, 0, 0), memory_space=pltpu.SMEM),
            pl.BlockSpec((tm, D_MODEL), lambda i: (i, 0)),
            pl.BlockSpec((tm, LANES), lambda i: (i, 0)),
            pl.BlockSpec(memory_space=pl.ANY),
        ],
        out_specs=pl.BlockSpec((tm, D_MODEL), lambda i: (i, 0)),
        out_shape=jax.ShapeDtypeStruct((S, D_MODEL), F32),
        scratch_shapes=[pltpu.VMEM((tm, D_MODEL), F32), pltpu.VMEM((tm, D_MODEL), F32), pltpu.SemaphoreType.DMA],
        compiler_params=pltpu.CompilerParams(dimension_semantics=("arbitrary",), vmem_limit_bytes=VMEM_LIMIT),
        name="moe_combine",
    )(pos3, x1, route, ys)


def _moe(h2, x1, route, counts, wg, wu, wd):
    S = h2.shape[0]
    tm = _EXP_TM
    n_tiles_max = 2 * S // tm + N_EXPERTS
    cnt = counts[0, 0:N_EXPERTS].astype(I32)
    tiles = (cnt + tm - 1) // tm
    first = jnp.cumsum(tiles) - tiles
    n_tiles = jnp.sum(tiles)
    idx = jnp.arange(n_tiles_max, dtype=I32)
    last = jnp.minimum(idx, n_tiles - 1)
    tile_e = (jnp.sum(last[:, None] >= first[None, :], axis=1) - 1).astype(I32)
    eid = route[:, 0:2].astype(I32)
    pos = first[eid] * tm + route[:, 2:4].astype(I32)
    pos3 = pos.reshape(S // _ROW_TM, 1, 2 * _ROW_TM)
    xs = _dispatch(pos3, h2, n_tiles_max * tm)
    ys = _experts(tile_e, idx, n_tiles.reshape(1), xs, wg, wu, wd)
    return _combine(pos3, x1, route, ys)


def _split_w_in(w):
    o = np.cumsum([0, ATTN_W] + [KV_W] * 6 + [N_BRANCH * N_HEADS] + [CONV_W] * 3)
    q, kc, vc, ksl, vsl, kwn, vwn, gts, cb, cc, ch = (w[:, o[i]:o[i + 1]] for i in range(11))
    perm = np.array([h * N_BRANCH + br for br in range(N_BRANCH) for h in range(N_HEADS)])
    wtok = jnp.concatenate([kc, vc, ksl, kwn, cb, cc, ch], axis=1).astype(BF16)
    feat = jnp.concatenate([q, vsl, vwn, gts[:, perm], jnp.zeros((D_MODEL, 8), w.dtype)], axis=1)
    return wtok, feat.T.astype(BF16)


def _expand_cmp_w1(w1k, w1v):
    def half(w, l0):
        wl = w.reshape(CMP_LEN, HEAD_DIM, CMP_HIDDEN)[l0:l0 + CMP_STRIDE]
        return wl
    tops = []
    for l0 in (0, CMP_STRIDE):
        blk = jnp.zeros((CMP_STRIDE, 2 * N_KV, HEAD_DIM, 2 * N_KV, CMP_HIDDEN), F32)
        for s, w in ((0, w1k), (1, w1v)):
            for g in range(N_KV):
                blk = blk.at[:, s * N_KV + g, :, s * N_KV + g, :].set(half(w, l0))
        tops.append(blk.reshape(CMP_STRIDE * 2 * KV_W, 2 * N_KV * CMP_HIDDEN).astype(BF16))
    return tops


def kernel(x, rel_bias, norm1, w_in, q_norm, k_norm, cmp_pos_k, cmp_pos_v, cmp_k_w1, cmp_k_w2, cmp_v_w1, cmp_v_w2,
           conv_w, attn_out_norm, conv_out_norm, w_out, norm2, ffn_w_gate, ffn_w_up, ffn_w_down, router_w, router_b,
           moe_w_gate, moe_w_up, moe_w_down):
    B, S, _ = x.shape
    assert B == 1 and S % CHUNK == 0 and S >= WINDOW + Q_BLOCK
    depth = norm1.shape[0]
    x2 = x.reshape(S, D_MODEL)
    tables = _bias_tables(rel_bias)
    for layer in range(depth):
        wtok, wfeat = _split_w_in(w_in[layer])
        kgain = jnp.concatenate([jnp.tile(k_norm[layer, 1], N_KV), jnp.tile(k_norm[layer, 2], N_KV)])[None, :]
        gq = jnp.broadcast_to(q_norm[layer][:, None], (HEAD_DIM, _IN_TM))
        convw = jnp.zeros((8, CONV_W), F32).at[0:CONV_K].set(conv_w[layer])
        qT, gT, vsT, vwT, ks, kw, kcv, conv_n = _in_proj(
            x2, norm1[layer][None, :], wtok, wfeat, kgain, gq, convw, conv_out_norm[layer][None, :])

        wtop, wbot = _expand_cmp_w1(cmp_k_w1[layer], cmp_v_w1[layer])
        pos = jnp.zeros((8, 2 * CMP_LEN * HEAD_DIM), F32)
        pos = pos.at[0, 0:CMP_LEN * HEAD_DIM].set(cmp_pos_k[layer].reshape(-1))
        pos = pos.at[0, CMP_LEN * HEAD_DIM:].set(cmp_pos_v[layer].reshape(-1))
        w1 = jnp.zeros((2 * CMP_LEN * HEAD_DIM, 2 * CMP_HIDDEN), F32)
        w1 = w1.at[0:CMP_LEN * HEAD_DIM, 0:CMP_HIDDEN].set(cmp_k_w1[layer])
        w1 = w1.at[CMP_LEN * HEAD_DIM:, CMP_HIDDEN:].set(cmp_v_w1[layer])
        w2k = jnp.zeros((CMP_HIDDEN, LANES), F32).at[:, 0:HEAD_DIM].set(cmp_k_w2[layer]).astype(BF16)
        w2vT = cmp_v_w2[layer].T.astype(BF16)
        kcgain = jnp.zeros((1, LANES), F32).at[0, 0:HEAD_DIM].set(k_norm[layer, 0])
        kc, vcT = _compress(kcv, wtop, wbot, pos, w1, w2k, w2vT, kcgain)

        gain_b = jnp.broadcast_to(attn_out_norm[layer].reshape(N_KV, GQA, HEAD_DIM).transpose(0, 2, 1)[:, :, :, None],
                                  (N_KV, HEAD_DIM, GQA, LANES)).reshape(N_KV, HEAD_DIM, GQA * LANES)
        attn_n = _attention(qT, gT, ks, vsT, kw, vwT, kc, vcT, tables, gain_b)

        wout = w_out[layer].astype(BF16)
        g2 = norm2[layer][None, :]
        i = layer // 2
        if layer % 2 == 0:
            x2 = _outproj_ffn(x2, attn_n, conv_n, wout, g2, ffn_w_gate[i].astype(BF16), ffn_w_up[i].astype(BF16),
                              ffn_w_down[i].astype(BF16))
        else:
            rw = jnp.zeros((D_MODEL, LANES), F32).at[:, 0:N_EXPERTS].set(router_w[i]).astype(BF16)
            rb = jnp.zeros((1, LANES), F32).at[0, 0:N_EXPERTS].set(router_b[i])
            x1, h2, route, counts = _outproj_router(x2, attn_n, conv_n, wout, g2, rw, rb)
            x2 = _moe(h2, x1, route, counts, moe_w_gate[i].astype(BF16), moe_w_up[i].astype(BF16),
                      moe_w_down[i].astype(BF16))
    return x2.reshape(B, S, D_MODEL)
```

```python
import functools
import math

import numpy as np
import jax
import jax.numpy as jnp
from jax import lax
from jax.experimental import pallas as pl
from jax.experimental.pallas import tpu as pltpu

F32 = jnp.float32
BF16 = jnp.bfloat16
I32 = jnp.int32

D_MODEL = 1024
HEAD_DIM = 64
N_HEADS = 8
N_KV = 2
GQA = N_HEADS // N_KV
ATTN_W = N_HEADS * HEAD_DIM
KV_W = N_KV * HEAD_DIM
N_BRANCH = 3
CONV_W = 512
CONV_K = 3
CMP_LEN = 32
CMP_STRIDE = 16
CMP_HIDDEN = 128
SEL_LEN = 64
SEL_TOPK = 16
WINDOW = 512
Q_BLOCK = 128
N_BUCKETS = 32
MAX_DISTANCE = 128
D_FF = 2816
N_EXPERTS = 8
EPS = 1e-6
NEG = -1e30
FORCE = 1e9
MASK_BIG = 2.0 ** 60
SCALE = HEAD_DIM ** -0.5
LOG2E = math.log2(math.e)

LANES = 128
V_ROWS = 80
CHUNK = 512
BLOCKS_PER_CHUNK = CHUNK // SEL_LEN
TC_LEAD = 24
TC_ROWS = 88
TC_WIN = 48
VMEM_LIMIT = 56 * 1024 * 1024


def _bucket_np(dist):
    n = np.maximum(dist, 0)
    max_exact = N_BUCKETS // 2
    nf = np.maximum(n, max_exact).astype(np.float64)
    v = np.log(nf / max_exact) / math.log(MAX_DISTANCE / max_exact) * (N_BUCKETS - max_exact)
    frac = np.abs(v - np.round(v))
    assert np.all((frac > 1e-6) | (n <= max_exact) | (n >= MAX_DISTANCE)), "bucket boundary is precision dependent"
    large = np.minimum(max_exact + (v + 1e-9).astype(np.int32), N_BUCKETS - 1)
    return np.where(n < max_exact, n, large).astype(np.int32)


def _index_tables():
    tl = np.arange(Q_BLOCK)[None, :]
    r = np.arange(WINDOW + Q_BLOCK + WINDOW)[:, None]
    d = tl + WINDOW - r
    idx_w = np.where((d >= 0) & (d < WINDOW), _bucket_np(d), -1)
    r = np.arange(3 * Q_BLOCK)[:, None]
    d = tl + Q_BLOCK - r
    idx_s = np.where(d >= 0, _bucket_np(d), -1)
    r = np.arange(TC_ROWS)[:, None] - TC_LEAD
    d = tl - CMP_STRIDE * r + (CMP_STRIDE * 16 - (CMP_LEN - 1))
    idx_c = np.where((d >= 0) & (r < 32), _bucket_np(d), -1)
    return idx_w.astype(np.int32), idx_s.astype(np.int32), idx_c.astype(np.int32)


def _tables_kernel(rb_ref, iw_ref, is_ref, ic_ref, tw_ref, ts_ref, tc_ref, b31_ref):
    for h in range(N_HEADS):
        g, r = divmod(h, GQA)
        lanes = slice(r * LANES, (r + 1) * LANES)
        for idx_ref, out_ref in ((iw_ref, tw_ref), (is_ref, ts_ref), (ic_ref, tc_ref)):
            out_ref[g, :, lanes] = jnp.full(idx_ref.shape, NEG, F32)

            def body(b, carry, idx_ref=idx_ref, out_ref=out_ref, g=g, lanes=lanes, h=h):
                out_ref[g, :, lanes] = jnp.where(idx_ref[...] == b, rb_ref[b, h] * LOG2E, out_ref[g, :, lanes])
                return carry

            lax.fori_loop(0, N_BUCKETS, body, 0)
        b31_ref[g, :, lanes] = jnp.full((8, LANES), rb_ref[N_BUCKETS - 1, h] * LOG2E, F32)


def _bias_tables(rel_bias):
    idx_w, idx_s, idx_c = _index_tables()
    width = GQA * LANES
    out_shape = (
        jax.ShapeDtypeStruct((N_KV, idx_w.shape[0], width), F32),
        jax.ShapeDtypeStruct((N_KV, idx_s.shape[0], width), F32),
        jax.ShapeDtypeStruct((N_KV, idx_c.shape[0], width), F32),
        jax.ShapeDtypeStruct((N_KV, 8, width), F32),
    )
    vmem = pl.BlockSpec(memory_space=pltpu.VMEM)
    return pl.pallas_call(
        _tables_kernel,
        out_shape=out_shape,
        in_specs=[pl.BlockSpec(memory_space=pltpu.SMEM), vmem, vmem, vmem],
        out_specs=(vmem, vmem, vmem, vmem),
        name="bias_tables",
    )(rel_bias, jnp.asarray(idx_w), jnp.asarray(idx_s), jnp.asarray(idx_c))


def _in_proj_kernel(x_ref, g1_ref, wtok_ref, wfeat_ref, ind_ref, kgain_ref, aug_ref, gq_ref, convw_ref, cgain_ref,
                    qT_ref, gT_ref, vsT_ref, vwT_ref, ks_ref, kw_ref, kcv_ref, convn_ref, zs_ref):
    tm = x_ref.shape[0]
    i = pl.program_id(0)
    x = x_ref[...]
    ms = jnp.mean(x * x, axis=-1, keepdims=True)
    h = (x * lax.rsqrt(ms + EPS) * g1_ref[...]).astype(BF16)
    tok = jnp.dot(h, wtok_ref[...], preferred_element_type=F32)
    feat = lax.dot_general(wfeat_ref[...], h, (((1,), (1,)), ((), ())),
                           preferred_element_type=F32)

    kcv_ref[...] = tok[:, 0:2 * KV_W].astype(BF16)
    kk = tok[:, 2 * KV_W:4 * KV_W]
    ssq = jnp.dot(kk * kk, ind_ref[...], preferred_element_type=F32)
    kn = kk * lax.rsqrt(ssq * (1.0 / HEAD_DIM) + EPS) * kgain_ref[...]
    ksl = kn[:, 0:KV_W]
    lane = lax.broadcasted_iota(I32, (tm, LANES), 1)
    aug = aug_ref[...]
    ks_ref[0] = jnp.where(lane < HEAD_DIM, ksl, aug).astype(BF16)
    ks_ref[1] = jnp.where(lane < HEAD_DIM, pltpu.roll(ksl, HEAD_DIM, 1), aug).astype(BF16)
    kw_ref[...] = kn[:, KV_W:2 * KV_W].astype(BF16)

    c0 = 4 * KV_W
    cb = tok[:, c0:c0 + CONV_W]
    cc = tok[:, c0 + CONV_W:c0 + 2 * CONV_W]
    ch = tok[:, c0 + 2 * CONV_W:c0 + 3 * CONV_W]
    z = cc * ch

    @pl.when(i == 0)
    def _():
        zs_ref[0:8, :] = jnp.zeros((8, CONV_W), F32)

    zs_ref[8:8 + tm, :] = z
    z1 = zs_ref[7:7 + tm, :]
    z2 = zs_ref[6:6 + tm, :]
    w = convw_ref[...]
    y = w[0:1, :] * z2 + w[1:2, :] * z1 + w[2:3, :] * z
    zs_ref[0:8, :] = z[tm - 8:tm, :]
    oc = cb * y
    msc = jnp.mean(oc * oc, axis=-1, keepdims=True)
    convn_ref[...] = (oc * lax.rsqrt(msc + EPS) * cgain_ref[...]).astype(BF16)

    q = feat[0:ATTN_W].reshape(N_HEADS, HEAD_DIM, tm)
    qss = jnp.sum(q * q, axis=1, keepdims=True)
    qn = q * lax.rsqrt(qss * (1.0 / HEAD_DIM) + EPS) * gq_ref[...][None]
    qT_ref[...] = (qn * (SCALE * LOG2E)).reshape(ATTN_W, tm).astype(BF16)
    ones_rows = (lax.broadcasted_iota(I32, (V_ROWS - HEAD_DIM, tm), 0) == 0).astype(BF16)
    for g in range(N_KV):
        r0 = ATTN_W + g * HEAD_DIM
        vsT_ref[g, 0:HEAD_DIM, :] = feat[r0:r0 + HEAD_DIM].astype(BF16)
        vsT_ref[g, HEAD_DIM:V_ROWS, :] = ones_rows
        r1 = ATTN_W + KV_W + g * HEAD_DIM
        vwT_ref[g, 0:HEAD_DIM, :] = feat[r1:r1 + HEAD_DIM].astype(BF16)
        vwT_ref[g, HEAD_DIM:V_ROWS, :] = ones_rows
    g0 = ATTN_W + 2 * KV_W
    gT_ref[...] = jax.nn.sigmoid(feat[g0:g0 + N_BRANCH * N_HEADS])


_IN_TM = 512


def _in_proj(x2, g1, wtok, wfeat, kgain, gq, convw, cgain):
    S = x2.shape[0]
    tm = _IN_TM
    nt = S // tm
    ind = np.kron(np.eye(2 * N_KV, dtype=np.float32), np.ones((HEAD_DIM, HEAD_DIM), np.float32))
    aug = np.zeros((tm, LANES), np.float32)
    blk = (np.arange(tm) // SEL_LEN) % BLOCKS_PER_CHUNK
    aug[np.arange(tm), HEAD_DIM + blk] = 1.0
    aug[:, HEAD_DIM + BLOCKS_PER_CHUNK:HEAD_DIM + BLOCKS_PER_CHUNK + 2] = 1.0
    const = lambda shape: pl.BlockSpec(shape, lambda i: (0,) * len(shape))
    out_shape = (
        jax.ShapeDtypeStruct((ATTN_W, S), BF16),
        jax.ShapeDtypeStruct((N_BRANCH * N_HEADS, S), F32),
        jax.ShapeDtypeStruct((N_KV, V_ROWS, S), BF16),
        jax.ShapeDtypeStruct((N_KV, V_ROWS, S), BF16),
        jax.ShapeDtypeStruct((N_KV, S, LANES), BF16),
        jax.ShapeDtypeStruct((S, LANES), BF16),
        jax.ShapeDtypeStruct((S, 2 * KV_W), BF16),
        jax.ShapeDtypeStruct((S, CONV_W), BF16),
    )
    out_specs = (
        pl.BlockSpec((ATTN_W, tm), lambda i: (0, i)),
        pl.BlockSpec((N_BRANCH * N_HEADS, tm), lambda i: (0, i)),
        pl.BlockSpec((N_KV, V_ROWS, tm), lambda i: (0, 0, i)),
        pl.BlockSpec((N_KV, V_ROWS, tm), lambda i: (0, 0, i)),
        pl.BlockSpec((N_KV, tm, LANES), lambda i: (0, i, 0)),
        pl.BlockSpec((tm, LANES), lambda i: (i, 0)),
        pl.BlockSpec((tm, 2 * KV_W), lambda i: (i, 0)),
        pl.BlockSpec((tm, CONV_W), lambda i: (i, 0)),
    )
    in_specs = [
        pl.BlockSpec((tm, D_MODEL), lambda i: (i, 0)),
        const((1, D_MODEL)),
        const(wtok.shape),
        const(wfeat.shape),
        const(ind.shape),
        const((1, 2 * KV_W)),
        const(aug.shape),
        const((HEAD_DIM, tm)),
        const((8, CONV_W)),
        const((1, CONV_W)),
    ]
    return pl.pallas_call(
        _in_proj_kernel,
        grid=(nt,),
        in_specs=in_specs,
        out_specs=out_specs,
        out_shape=out_shape,
        scratch_shapes=[pltpu.VMEM((tm + 8, CONV_W), F32)],
        compiler_params=pltpu.CompilerParams(dimension_semantics=("arbitrary",), vmem_limit_bytes=VMEM_LIMIT),
        name="in_proj",
    )(x2, g1, wtok, wfeat, jnp.asarray(ind), kgain, jnp.asarray(aug), gq, convw, cgain)


def _compress_kernel(r_ref, wtop_ref, wbot_ref, pos_ref, w1_ref, w2k_ref, w2vT_ref, kgain_ref, kc_ref, vcT_ref):
    nr = r_ref.shape[0]
    r = r_ref[...]
    u = jnp.dot(r, wtop_ref[...], preferred_element_type=F32)
    lo = jnp.dot(r, wbot_ref[...], preferred_element_type=F32)
    bias = jnp.dot(pos_ref[...], w1_ref[...], preferred_element_type=F32)[0:1, :]
    bias4 = jnp.concatenate([bias[:, 0:CMP_HIDDEN]] * N_KV + [bias[:, CMP_HIDDEN:]] * N_KV, axis=1)
    hid = u + pltpu.roll(lo, nr - 1, 0) + bias4
    act = jax.nn.gelu(hid)
    for g in range(N_KV):
        ak = act[:, g * CMP_HIDDEN:(g + 1) * CMP_HIDDEN].astype(BF16)
        av = act[:, (N_KV + g) * CMP_HIDDEN:(N_KV + g + 1) * CMP_HIDDEN].astype(BF16)
        kc = jnp.dot(ak, w2k_ref[...], preferred_element_type=F32)
        ssq = jnp.sum(kc * kc, axis=-1, keepdims=True)
        kc_ref[g] = (kc * lax.rsqrt(ssq * (1.0 / HEAD_DIM) + EPS) * kgain_ref[...]).astype(BF16)
        vcT_ref[g] = lax.dot_general(w2vT_ref[...], av, (((1,), (1,)), ((), ())),
                                     preferred_element_type=F32).astype(BF16)


def _compress(kcv, wtop, wbot, pos, w1, w2k, w2vT, kgain):
    S = kcv.shape[0]
    nr = S // CMP_STRIDE
    r = kcv.reshape(nr, CMP_STRIDE * 2 * KV_W)
    vmem = pl.BlockSpec(memory_space=pltpu.VMEM)
    return pl.pallas_call(
        _compress_kernel,
        out_shape=(jax.ShapeDtypeStruct((N_KV, nr, LANES), BF16),
                   jax.ShapeDtypeStruct((N_KV, HEAD_DIM, nr), BF16)),
        in_specs=[vmem] * 8,
        out_specs=(vmem, vmem),
        compiler_params=pltpu.CompilerParams(vmem_limit_bytes=VMEM_LIMIT),
        name="compress",
    )(r, wtop, wbot, pos, w1, w2k, w2vT, kgain)


def _tile4(row):
    return jnp.concatenate([row] * GQA, axis=1)


def _attn_kernel(qT_ref, gT_ref, ks_ref, vsT_ref, kw_ref, vwT_ref, kc_ref, vcT_ref,
                 tw_ref, ts_ref, tc_ref, b31_ref, gain_ref, out_ref,
                 sc_ref, imp_ref, aoct_ref, aall_ref, qs_ref, sa_ref, sb_ref):
    ncr = kc_ref.shape[1]
    nsel = aall_ref.shape[1] * BLOCKS_PER_CHUNK
    noct = aoct_ref.shape[1] - 1
    width = GQA * LANES
    c = pl.program_id(0)
    t0 = c * Q_BLOCK
    tl = lax.broadcasted_iota(I32, (1, LANES), 1)
    cur = 2 * c + (tl >= SEL_LEN).astype(I32)
    zb = jnp.maximum(2 * c - 2, 0)
    zs = pl.multiple_of(zb * SEL_LEN, Q_BLOCK)
    ts_off = pl.multiple_of(zs - (t0 - Q_BLOCK), Q_BLOCK)
    zeros_q = jnp.zeros((HEAD_DIM, width), BF16)

    col_ok = _tile4(t0 + tl) >= CMP_LEN - 1
    j = lax.broadcasted_iota(I32, (nsel, LANES), 0)
    forced = (j == 0) | (j == cur) | (j == cur - 1)
    imp_ref[0:8, :] = jnp.zeros((8, LANES), F32)
    imp_ref[8 + ncr:16 + ncr, :] = jnp.zeros((8, LANES), F32)

    o_cmp, qgs, scores0 = [], [], []
    for g in range(N_KV):
        qg = jnp.concatenate([qT_ref[(GQA * g + r) * HEAD_DIM:(GQA * g + r + 1) * HEAD_DIM, :]
                              for r in range(GQA)], axis=1)
        qc = jnp.concatenate([qg, zeros_q], axis=0)
        qgs.append(qg)
        b31row = b31_ref[g, 0:1, :]
        lo = pl.multiple_of(jnp.clip((8 * c - 16) // 16 * 16, 0, ncr - TC_WIN), 16)
        tc_off = pl.multiple_of(lo - (8 * c - 16) + TC_LEAD, 8)
        s = jnp.dot(kc_ref[g], qc, preferred_element_type=F32)
        row = lax.broadcasted_iota(I32, (ncr, width), 0)
        sc_ref[...] = s + jnp.where(row < lo, b31row, NEG)
        s_loc = jnp.dot(kc_ref[g, pl.ds(lo, TC_WIN), :], qc, preferred_element_type=F32)
        sc_ref[pl.ds(lo, TC_WIN), :] = s_loc + tc_ref[g, pl.ds(tc_off, TC_WIN), :]
        s = sc_ref[...]
        m = jnp.max(s, axis=0, keepdims=True)
        e = jnp.exp2(s - m)
        l = jnp.sum(e, axis=0, keepdims=True)
        p = e * jnp.where(col_ok, 1.0 / jnp.maximum(l, 1e-30), 0.0)
        o_cmp.append(jnp.dot(vcT_ref[g], p.astype(BF16), preferred_element_type=F32))
        imp_ref[8:8 + ncr, :] = (
            (p[:, 0:LANES] + p[:, LANES:2 * LANES]) + p[:, 2 * LANES:3 * LANES] + p[:, 3 * LANES:4 * LANES])

        isel = jnp.zeros((nsel, LANES), F32)
        for off in (0, -1, 1, 0, 2, 1, 3, 2):
            isel = isel + imp_ref[pl.ds(8 + off, nsel, stride=4), :]
        score = jnp.where(forced, -jnp.inf, isel)
        scores0.append(jnp.where(j > cur, NEG, score))

    def pick(_, score):
        mx = jnp.max(score, axis=0, keepdims=True)
        first = jnp.min(jnp.where(score == mx, j, nsel), axis=0, keepdims=True)
        return jnp.where(j == first, -jnp.inf, score)

    picks = [lax.fori_loop(0, SEL_TOPK - 3, pick, scores0[g]) for g in range(N_KV)]

    o_win, m_init, acc_init, gates = [], [], [], []
    for g in range(N_KV):
        qg = qgs[g]
        qw = jnp.concatenate([qg, zeros_q] if g == 0 else [zeros_q, qg], axis=0)
        b31row = b31_ref[g, 0:1, :]
        chosen = (picks[g] == -jnp.inf) & (j <= cur)
        a_all = _tile4(jnp.where(chosen, 0.0, -MASK_BIG)).reshape(noct, BLOCKS_PER_CHUNK, width)
        a_far = _tile4(jnp.where(chosen & (j < zb), 0.0, -MASK_BIG)).reshape(noct, BLOCKS_PER_CHUNK, width)
        b_hi = b31row.astype(BF16).astype(F32)
        erow = lax.broadcasted_iota(I32, (8, width), 0)
        extra = jnp.where(erow == 0, b_hi, jnp.where(erow == 1, b31row - b_hi, 0.0))
        aoct_ref[g, 0:noct] = jnp.concatenate(
            [a_far, jnp.broadcast_to(extra[None], (noct, 8, width))], axis=1).astype(BF16)
        aoct_ref[g, noct] = jnp.concatenate([jnp.full((8, width), -MASK_BIG, F32), extra], axis=0).astype(BF16)
        aall_ref[g] = jnp.concatenate([a_all, jnp.zeros((noct, 8, width), F32)], axis=1).astype(BF16)
        qs_ref[g, 0:HEAD_DIM, :] = qg
        qs_ref[g, HEAD_DIM + 16:LANES, :] = jnp.zeros((LANES - HEAD_DIM - 16, width), BF16)

        halves = []
        for h in range(2):
            qs_ref[g, HEAD_DIM:HEAD_DIM + 16, :] = aall_ref[g, (zb + 2 * h) // BLOCKS_PER_CHUNK]
            sh = jnp.dot(ks_ref[g, pl.ds(zs + h * Q_BLOCK, Q_BLOCK), :], qs_ref[g], preferred_element_type=F32)
            halves.append(sh + ts_ref[g, pl.ds(ts_off + h * Q_BLOCK, Q_BLOCK), :])
        sz = jnp.concatenate(halves, axis=0)
        mz = jnp.max(sz, axis=0, keepdims=True)
        pz = jnp.exp2(sz - mz)
        m_init.append(mz)
        acc_init.append(jnp.dot(vsT_ref[g, :, pl.ds(zs, 2 * Q_BLOCK)], pz.astype(BF16), preferred_element_type=F32))

        ws = pl.multiple_of(jnp.maximum(t0 - WINDOW, 0), Q_BLOCK)
        tw_off = pl.multiple_of(ws - (t0 - WINDOW), Q_BLOCK)
        nw = WINDOW + Q_BLOCK
        sw = jnp.dot(kw_ref[pl.ds(ws, nw), :], qw, preferred_element_type=F32) + tw_ref[g, pl.ds(tw_off, nw), :]
        mw = jnp.max(sw, axis=0, keepdims=True)
        pw = jnp.exp2(sw - mw)
        accw = jnp.dot(vwT_ref[g, :, pl.ds(ws, nw)], pw.astype(BF16), preferred_element_type=F32)
        o_win.append(accw[0:HEAD_DIM] * (1.0 / jnp.maximum(accw[HEAD_DIM:HEAD_DIM + 1], 1e-30)))
        gates.append([jnp.concatenate([gT_ref[br * N_HEADS + GQA * g + r:br * N_HEADS + GQA * g + r + 1, :]
                                       for r in range(GQA)], axis=1) for br in range(N_BRANCH)])

    nfar = (zb + BLOCKS_PER_CHUNK - 1) // BLOCKS_PER_CHUNK

    nsub = 1
    sub = CHUNK // nsub

    def score(g, u, buf_ref, h):
        k0 = pl.multiple_of(jnp.minimum(u, noct - 1) * CHUNK + h * sub, sub)
        s = jnp.dot(ks_ref[g, pl.ds(k0, sub), :], qs_ref[g], preferred_element_type=F32)
        buf_ref[g, h * sub:(h + 1) * sub, :] = s
        return jnp.max(s, axis=0, keepdims=True)

    def set_mask_rows(g, u):
        oct_id = jnp.where(u < nfar, u, noct)
        qs_ref[g, HEAD_DIM:HEAD_DIM + 16, :] = aoct_ref[g, oct_id]

    def accumulate(g, u, buf_ref, h, mn, acc):
        k0 = pl.multiple_of(jnp.minimum(u, noct - 1) * CHUNK + h * sub, sub)
        p = jnp.exp2(buf_ref[g, h * sub:(h + 1) * sub, :] - mn)
        return acc + jnp.dot(vsT_ref[g, :, pl.ds(k0, sub)], p.astype(BF16), preferred_element_type=F32)

    def step(u, cur_ref, nxt_ref, state, cms):
        out_state, out_cms = [], []
        for g in range(N_KV):
            m, acc = state[2 * g], state[2 * g + 1]
            mn = jnp.maximum(m, cms[g])
            acc = jnp.exp2(m - mn) * acc
            set_mask_rows(g, u + 1)
            cm = None
            for h in range(nsub):
                ch = score(g, u + 1, nxt_ref, h)
                cm = ch if cm is None else jnp.maximum(cm, ch)
                acc = accumulate(g, u, cur_ref, h, mn, acc)
            out_cms.append(cm)
            out_state += [mn, acc]
        return out_state, out_cms

    def far(i, carry):
        state, cm_b = step(2 * i, sa_ref, sb_ref, carry[0:4], carry[4:6])
        state, cm_a = step(2 * i + 1, sb_ref, sa_ref, state, cm_b)
        return tuple(state + cm_a)

    first = []
    for g in range(N_KV):
        set_mask_rows(g, 0)
        cm = None
        for h in range(nsub):
            ch = score(g, 0, sa_ref, h)
            cm = ch if cm is None else jnp.maximum(cm, ch)
        first.append(cm)
    npairs = jnp.maximum(nfar, 1) // 2
    carry = lax.fori_loop(0, npairs, far, (m_init[0], acc_init[0], m_init[1], acc_init[1], first[0], first[1]))
    final = []
    for g in range(N_KV):
        m, acc = carry[2 * g], carry[2 * g + 1]
        mn = jnp.maximum(m, carry[4 + g])
        acc = jnp.exp2(m - mn) * acc
        for h in range(nsub):
            acc = accumulate(g, 2 * npairs, sa_ref, h, mn, acc)
        final += [mn, acc]
    carry = final

    o_all, ssq = [], jnp.zeros((1, LANES), F32)
    for g in range(N_KV):
        acc = carry[2 * g + 1]
        o_sel = acc[0:HEAD_DIM] * (1.0 / jnp.maximum(acc[HEAD_DIM:HEAD_DIM + 1], 1e-30))
        o = gates[g][0] * o_cmp[g] + gates[g][1] * o_sel + gates[g][2] * o_win[g]
        o_all.append(o)
        cs = jnp.sum(o * o, axis=0, keepdims=True)
        ssq = ssq + ((cs[:, 0:LANES] + cs[:, LANES:2 * LANES]) + (cs[:, 2 * LANES:3 * LANES] + cs[:, 3 * LANES:]))
    inv = _tile4(lax.rsqrt(ssq * (1.0 / ATTN_W) + EPS))
    for g in range(N_KV):
        on = o_all[g] * inv * gain_ref[g]
        for k in range(GQA // 2):
            pair = jnp.concatenate([on[:, (2 * k) * LANES:(2 * k + 1) * LANES],
                                    on[:, (2 * k + 1) * LANES:(2 * k + 2) * LANES]], axis=0)
            col = (GQA * g + 2 * k) * HEAD_DIM
            out_ref[:, col:col + 2 * HEAD_DIM] = pair.T.astype(BF16)


def _attention(qT, gT, ks, vsT, kw, vwT, kc, vcT, tables, gain_b):
    S = qT.shape[1]
    nq = S // Q_BLOCK
    ncr = kc.shape[1]
    nsel = S // SEL_LEN
    noct = S // CHUNK
    width = GQA * LANES
    tw, ts, tc, b31 = tables
    vmem = pl.BlockSpec(memory_space=pltpu.VMEM)
    in_specs = [
        pl.BlockSpec((ATTN_W, Q_BLOCK), lambda c: (0, c)),
        pl.BlockSpec((N_BRANCH * N_HEADS, Q_BLOCK), lambda c: (0, c)),
    ] + [vmem] * 11
    return pl.pallas_call(
        _attn_kernel,
        grid=(nq,),
        in_specs=in_specs,
        out_specs=pl.BlockSpec((Q_BLOCK, ATTN_W), lambda c: (c, 0)),
        out_shape=jax.ShapeDtypeStruct((S, ATTN_W), BF16),
        scratch_shapes=[
            pltpu.VMEM((ncr, width), F32),
            pltpu.VMEM((ncr + 16, LANES), F32),
            pltpu.VMEM((N_KV, noct + 1, 16, width), BF16),
            pltpu.VMEM((N_KV, noct, 16, width), BF16),
            pltpu.VMEM((N_KV, LANES, width), BF16),
            pltpu.VMEM((N_KV, CHUNK, width), F32),
            pltpu.VMEM((N_KV, CHUNK, width), F32),
        ],
        compiler_params=pltpu.CompilerParams(dimension_semantics=("arbitrary",), vmem_limit_bytes=VMEM_LIMIT),
        name="nsa_attention",
    )(qT, gT, ks, vsT, kw, vwT, kc, vcT, tw, ts, tc, b31, gain_b)


def _mix_and_norm(x_ref, attn_ref, conv_ref, wout_ref, g2_ref):
    x1 = x_ref[...] + jnp.dot(attn_ref[...], wout_ref[0:ATTN_W, :], preferred_element_type=F32) \
        + jnp.dot(conv_ref[...], wout_ref[ATTN_W:, :], preferred_element_type=F32)
    ms = jnp.mean(x1 * x1, axis=-1, keepdims=True)
    h2 = x1 * lax.rsqrt(ms + EPS) * g2_ref[...]
    return x1, h2


def _ffn_kernel(x_ref, attn_ref, conv_ref, wout_ref, g2_ref, wg_ref, wu_ref, wd_ref, out_ref):
    x1, h2 = _mix_and_norm(x_ref, attn_ref, conv_ref, wout_ref, g2_ref)
    h2 = h2.astype(BF16)
    a = jnp.dot(h2, wg_ref[...], preferred_element_type=F32)
    u = jnp.dot(h2, wu_ref[...], preferred_element_type=F32)
    y = (a * jax.nn.sigmoid(a) * u).astype(BF16)
    out_ref[...] = x1 + jnp.dot(y, wd_ref[...], preferred_element_type=F32)


_FFN_TM = 512


def _outproj_ffn(x2, attn_n, conv_n, wout, g2, wg, wu, wd):
    S = x2.shape[0]
    tm = _FFN_TM
    resident = pl.BlockSpec(memory_space=pltpu.VMEM)
    return pl.pallas_call(
        _ffn_kernel,
        grid=(S // tm,),
        in_specs=[
            pl.BlockSpec((tm, D_MODEL), lambda i: (i, 0)),
            pl.BlockSpec((tm, ATTN_W), lambda i: (i, 0)),
            pl.BlockSpec((tm, CONV_W), lambda i: (i, 0)),
            resident, resident, resident, resident, resident,
        ],
        out_specs=pl.BlockSpec((tm, D_MODEL), lambda i: (i, 0)),
        out_shape=jax.ShapeDtypeStruct((S, D_MODEL), F32),
        compiler_params=pltpu.CompilerParams(dimension_semantics=("arbitrary",), vmem_limit_bytes=VMEM_LIMIT),
        name="outproj_ffn",
    )(x2, attn_n, conv_n, wout, g2, wg, wu, wd)


def _router_kernel(x_ref, attn_ref, conv_ref, wout_ref, g2_ref, rw_ref, rb_ref, tri_ref,
                   x1_ref, h2_ref, route_ref, cnt_ref, run_ref):
    tm = x_ref.shape[0]
    i = pl.program_id(0)

    @pl.when(i == 0)
    def _():
        run_ref[...] = jnp.zeros(run_ref.shape, F32)

    x1, h2 = _mix_and_norm(x_ref, attn_ref, conv_ref, wout_ref, g2_ref)
    x1_ref[...] = x1
    h2b = h2.astype(BF16)
    h2_ref[...] = h2b.astype(F32)
    lane = lax.broadcasted_iota(I32, (tm, LANES), 1)
    logits = jnp.dot(h2b, rw_ref[...], preferred_element_type=F32) + rb_ref[...]
    logits = jnp.where(lane < N_EXPERTS, logits, -jnp.inf)
    m1 = jnp.max(logits, axis=-1, keepdims=True)
    i1 = jnp.min(jnp.where(logits == m1, lane, LANES), axis=-1, keepdims=True)
    rest = jnp.where(lane == i1, -jnp.inf, logits)
    m2 = jnp.max(rest, axis=-1, keepdims=True)
    i2 = jnp.min(jnp.where(rest == m2, lane, LANES), axis=-1, keepdims=True)
    e2 = jnp.exp(m2 - m1)
    den = 1.0 + e2
    oh1 = (lane == i1).astype(F32)
    oh2 = (lane == i2).astype(F32)
    both = oh1 + oh2
    before = run_ref[0:1, :] + jnp.dot(tri_ref[...], both.astype(BF16), preferred_element_type=F32)
    rank1 = jnp.sum(before * oh1, axis=-1, keepdims=True)
    rank2 = jnp.sum(before * oh2, axis=-1, keepdims=True)
    fields = (i1.astype(F32), i2.astype(F32), rank1, rank2, 1.0 / den, e2 / den)
    route = jnp.zeros((tm, LANES), F32)
    for k, v in enumerate(fields):
        route = jnp.where(lane == k, v, route)
    route_ref[...] = route
    run_ref[...] = run_ref[...] + jnp.sum(both, axis=0, keepdims=True)
    cnt_ref[...] = run_ref[...]


def _outproj_router(x2, attn_n, conv_n, wout, g2, rw, rb):
    S = x2.shape[0]
    tm = _FFN_TM
    tri = np.tril(np.ones((tm, tm), np.float32), -1)
    const = lambda shape: pl.BlockSpec(shape, lambda i: (0,) * len(shape))
    return pl.pallas_call(
        _router_kernel,
        grid=(S // tm,),
        in_specs=[
            pl.BlockSpec((tm, D_MODEL), lambda i: (i, 0)),
            pl.BlockSpec((tm, ATTN_W), lambda i: (i, 0)),
            pl.BlockSpec((tm, CONV_W), lambda i: (i, 0)),
            const(wout.shape), const((1, D_MODEL)), const(rw.shape), const((1, LANES)), const((tm, tm)),
        ],
        out_specs=(pl.BlockSpec((tm, D_MODEL), lambda i: (i, 0)),
                   pl.BlockSpec((tm, D_MODEL), lambda i: (i, 0)),
                   pl.BlockSpec((tm, LANES), lambda i: (i, 0)),
                   const((8, LANES))),
        out_shape=(jax.ShapeDtypeStruct((S, D_MODEL), F32),
                   jax.ShapeDtypeStruct((S, D_MODEL), F32),
                   jax.ShapeDtypeStruct((S, LANES), F32),
                   jax.ShapeDtypeStruct((8, LANES), F32)),
        scratch_shapes=[pltpu.VMEM((8, LANES), F32)],
        compiler_params=pltpu.CompilerParams(dimension_semantics=("arbitrary",), vmem_limit_bytes=VMEM_LIMIT),
        name="outproj_router",
    )(x2, attn_n, conv_n, wout, g2, rw, rb, jnp.asarray(tri, BF16))


_ROW_TM = 256
_EXP_TM = 512
_DMA_UNROLL = 8


def _row_copy(src_ref, src_row, dst_ref, dst_row, sem):
    return pltpu.make_async_copy(src_ref.at[pl.ds(src_row, 1), :], dst_ref.at[pl.ds(dst_row, 1), :], sem)


def _dispatch_kernel(pos_ref, h_ref, zero_ref, xs_ref, sem):
    del zero_ref
    tm = h_ref.shape[0]

    def issue(t, carry):
        for k in range(2):
            _row_copy(h_ref, t, xs_ref, pos_ref[0, 0, 2 * t + k], sem).start()
        return carry

    lax.fori_loop(0, tm, issue, 0, unroll=_DMA_UNROLL)

    def drain(t, carry):
        for k in range(2):
            _row_copy(h_ref, 0, xs_ref, 0, sem).wait()
        return carry

    lax.fori_loop(0, tm, drain, 0, unroll=_DMA_UNROLL)


def _dispatch(pos3, h2, n_rows):
    S = h2.shape[0]
    tm = _ROW_TM
    zeros = jnp.zeros((n_rows, D_MODEL), F32)
    return pl.pallas_call(
        _dispatch_kernel,
        grid=(S // tm,),
        in_specs=[
            pl.BlockSpec((1, 1, 2 * tm), lambda i: (i, 0, 0), memory_space=pltpu.SMEM),
            pl.BlockSpec((tm, D_MODEL), lambda i: (i, 0)),
            pl.BlockSpec(memory_space=pl.ANY),
        ],
        out_specs=pl.BlockSpec(memory_space=pl.ANY),
        out_shape=jax.ShapeDtypeStruct((n_rows, D_MODEL), F32),
        scratch_shapes=[pltpu.SemaphoreType.DMA],
        input_output_aliases={2: 0},
        compiler_params=pltpu.CompilerParams(dimension_semantics=("arbitrary",), vmem_limit_bytes=VMEM_LIMIT),
        name="moe_dispatch",
    )(pos3, h2, zeros)


def _experts_kernel(te_ref, tb_ref, nt_ref, xs_ref, wg_ref, wu_ref, wd_ref, ys_ref):
    i = pl.program_id(0)

    @pl.when(i < nt_ref[0])
    def _():
        x = xs_ref[...].astype(BF16)
        a = jnp.dot(x, wg_ref[0], preferred_element_type=F32)
        u = jnp.dot(x, wu_ref[0], preferred_element_type=F32)
        y = (a * jax.nn.sigmoid(a) * u).astype(BF16)
        ys_ref[...] = jnp.dot(y, wd_ref[0], preferred_element_type=F32)

    @pl.when(i >= nt_ref[0])
    def _():
        ys_ref[...] = jnp.zeros(ys_ref.shape, F32)


def _experts(tile_e, tile_b, n_tiles, xs, wg, wu, wd):
    n_rows = xs.shape[0]
    tm = _EXP_TM
    weights = lambda shape: pl.BlockSpec(shape, lambda i, te, tb, nt: (te[i], 0, 0), pipeline_mode=pl.Buffered(1))
    grid_spec = pltpu.PrefetchScalarGridSpec(
        num_scalar_prefetch=3,
        grid=(n_rows // tm,),
        in_specs=[
            pl.BlockSpec((tm, D_MODEL), lambda i, te, tb, nt: (tb[i], 0)),
            weights((1, D_MODEL, D_FF)), weights((1, D_MODEL, D_FF)), weights((1, D_FF, D_MODEL)),
        ],
        out_specs=pl.BlockSpec((tm, D_MODEL), lambda i, te, tb, nt: (tb[i], 0)),
    )
    return pl.pallas_call(
        _experts_kernel,
        grid_spec=grid_spec,
        out_shape=jax.ShapeDtypeStruct((n_rows, D_MODEL), F32),
        compiler_params=pltpu.CompilerParams(dimension_semantics=("arbitrary",), vmem_limit_bytes=VMEM_LIMIT),
        name="moe_experts",
    )(tile_e, tile_b, n_tiles, xs, wg, wu, wd)


def _combine_kernel(pos_ref, x1_ref, route_ref, ys_ref, out_ref, y1_ref, y2_ref, sem):
    tm = x1_ref.shape[0]

    def issue(t, carry):
        _row_copy(ys_ref, pos_ref[0, 0, 2 * t], y1_ref, t, sem).start()
        _row_copy(ys_ref, pos_ref[0, 0, 2 * t + 1], y2_ref, t, sem).start()
        return carry

    lax.fori_loop(0, tm, issue, 0, unroll=_DMA_UNROLL)

    def drain(t, carry):
        _row_copy(ys_ref, 0, y1_ref, 0, sem).wait()
        _row_copy(ys_ref, 0, y2_ref, 0, sem).wait()
        return carry

    lax.fori_loop(0, tm, drain, 0, unroll=_DMA_UNROLL)
    lane = lax.broadcasted_iota(I32, (tm, LANES), 1)
    route = route_ref[...]
    w1 = jnp.sum(jnp.where(lane == 4, route, 0.0), axis=-1, keepdims=True)
    w2 = jnp.sum(jnp.where(lane == 5, route, 0.0), axis=-1, keepdims=True)
    out_ref[...] = x1_ref[...] + (y1_ref[...] * w1 + y2_ref[...] * w2)


def _combine(pos3, x1, route, ys):
    S = x1.shape[0]
    tm = _ROW_TM
    return pl.pallas_call(
        _combine_kernel,
        grid=(S // tm,),
        in_specs=[
            pl.BlockSpec((1, 1, 2 * tm), lambda i: (i, 0, 0), memory_space=pltpu.SMEM),
            pl.BlockSpec((tm, D_MODEL), lambda i: (i, 0)),
            pl.BlockSpec((tm, LANES), lambda i: (i, 0)),
            pl.BlockSpec(memory_space=pl.ANY),
        ],
        out_specs=pl.BlockSpec((tm, D_MODEL), lambda i: (i, 0)),
        out_shape=jax.ShapeDtypeStruct((S, D_MODEL), F32),
        scratch_shapes=[pltpu.VMEM((tm, D_MODEL), F32), pltpu.VMEM((tm, D_MODEL), F32), pltpu.SemaphoreType.DMA],
        compiler_params=pltpu.CompilerParams(dimension_semantics=("arbitrary",), vmem_limit_bytes=VMEM_LIMIT),
        name="moe_combine",
    )(pos3, x1, route, ys)


def _moe(h2, x1, route, counts, wg, wu, wd):
    S = h2.shape[0]
    tm = _EXP_TM
    n_tiles_max = 2 * S // tm + N_EXPERTS
    cnt = counts[0, 0:N_EXPERTS].astype(I32)
    tiles = (cnt + tm - 1) // tm
    first = jnp.cumsum(tiles) - tiles
    n_tiles = jnp.sum(tiles)
    idx = jnp.arange(n_tiles_max, dtype=I32)
    last = jnp.minimum(idx, n_tiles - 1)
    tile_e = (jnp.sum(last[:, None] >= first[None, :], axis=1) - 1).astype(I32)
    eid = route[:, 0:2].astype(I32)
    pos = first[eid] * tm + route[:, 2:4].astype(I32)
    pos3 = pos.reshape(S // _ROW_TM, 1, 2 * _ROW_TM)
    xs = _dispatch(pos3, h2, n_tiles_max * tm)
    ys = _experts(tile_e, idx, n_tiles.reshape(1), xs, wg, wu, wd)
    return _combine(pos3, x1, route, ys)


def _split_w_in(w):
    o = np.cumsum([0, ATTN_W] + [KV_W] * 6 + [N_BRANCH * N_HEADS] + [CONV_W] * 3)
    q, kc, vc, ksl, vsl, kwn, vwn, gts, cb, cc, ch = (w[:, o[i]:o[i + 1]] for i in range(11))
    perm = np.array([h * N_BRANCH + br for br in range(N_BRANCH) for h in range(N_HEADS)])
    wtok = jnp.concatenate([kc, vc, ksl, kwn, cb, cc, ch], axis=1).astype(BF16)
    feat = jnp.concatenate([q, vsl, vwn, gts[:, perm], jnp.zeros((D_MODEL, 8), w.dtype)], axis=1)
    return wtok, feat.T.astype(BF16)


def _expand_cmp_w1(w1k, w1v):
    def half(w, l0):
        wl = w.reshape(CMP_LEN, HEAD_DIM, CMP_HIDDEN)[l0:l0 + CMP_STRIDE]
        return wl
    tops = []
    for l0 in (0, CMP_STRIDE):
        blk = jnp.zeros((CMP_STRIDE, 2 * N_KV, HEAD_DIM, 2 * N_KV, CMP_HIDDEN), F32)
        for s, w in ((0, w1k), (1, w1v)):
            for g in range(N_KV):
                blk = blk.at[:, s * N_KV + g, :, s * N_KV + g, :].set(half(w, l0))
        tops.append(blk.reshape(CMP_STRIDE * 2 * KV_W, 2 * N_KV * CMP_HIDDEN).astype(BF16))
    return tops


def kernel(x, rel_bias, norm1, w_in, q_norm, k_norm, cmp_pos_k, cmp_pos_v, cmp_k_w1, cmp_k_w2, cmp_v_w1, cmp_v_w2,
           conv_w, attn_out_norm, conv_out_norm, w_out, norm2, ffn_w_gate, ffn_w_up, ffn_w_down, router_w, router_b,
           moe_w_gate, moe_w_up, moe_w_down):
    B, S, _ = x.shape
    assert B == 1 and S % CHUNK == 0 and S >= WINDOW + Q_BLOCK
    depth = norm1.shape[0]
    x2 = x.reshape(S, D_MODEL)
    tables = _bias_tables(rel_bias)
    for layer in range(depth):
        wtok, wfeat = _split_w_in(w_in[layer])
        kgain = jnp.concatenate([jnp.tile(k_norm[layer, 1], N_KV), jnp.tile(k_norm[layer, 2], N_KV)])[None, :]
        gq = jnp.broadcast_to(q_norm[layer][:, None], (HEAD_DIM, _IN_TM))
        convw = jnp.zeros((8, CONV_W), F32).at[0:CONV_K].set(conv_w[layer])
        qT, gT, vsT, vwT, ks, kw, kcv, conv_n = _in_proj(
            x2, norm1[layer][None, :], wtok, wfeat, kgain, gq, convw, conv_out_norm[layer][None, :])

        wtop, wbot = _expand_cmp_w1(cmp_k_w1[layer], cmp_v_w1[layer])
        pos = jnp.zeros((8, 2 * CMP_LEN * HEAD_DIM), F32)
        pos = pos.at[0, 0:CMP_LEN * HEAD_DIM].set(cmp_pos_k[layer].reshape(-1))
        pos = pos.at[0, CMP_LEN * HEAD_DIM:].set(cmp_pos_v[layer].reshape(-1))
        w1 = jnp.zeros((2 * CMP_LEN * HEAD_DIM, 2 * CMP_HIDDEN), F32)
        w1 = w1.at[0:CMP_LEN * HEAD_DIM, 0:CMP_HIDDEN].set(cmp_k_w1[layer])
        w1 = w1.at[CMP_LEN * HEAD_DIM:, CMP_HIDDEN:].set(cmp_v_w1[layer])
        w2k = jnp.zeros((CMP_HIDDEN, LANES), F32).at[:, 0:HEAD_DIM].set(cmp_k_w2[layer]).astype(BF16)
        w2vT = cmp_v_w2[layer].T.astype(BF16)
        kcgain = jnp.zeros((1, LANES), F32).at[0, 0:HEAD_DIM].set(k_norm[layer, 0])
        kc, vcT = _compress(kcv, wtop, wbot, pos, w1, w2k, w2vT, kcgain)

        gain_b = jnp.broadcast_to(attn_out_norm[layer].reshape(N_KV, GQA, HEAD_DIM).transpose(0, 2, 1)[:, :, :, None],
                                  (N_KV, HEAD_DIM, GQA, LANES)).reshape(N_KV, HEAD_DIM, GQA * LANES)
        attn_n = _attention(qT, gT, ks, vsT, kw, vwT, kc, vcT, tables, gain_b)

        wout = w_out[layer].astype(BF16)
        g2 = norm2[layer][None, :]
        i = layer // 2
        if layer % 2 == 0:
            x2 = _outproj_ffn(x2, attn_n, conv_n, wout, g2, ffn_w_gate[i].astype(BF16), ffn_w_up[i].astype(BF16),
                              ffn_w_down[i].astype(BF16))
        else:
            rw = jnp.zeros((D_MODEL, LANES), F32).at[:, 0:N_EXPERTS].set(router_w[i]).astype(BF16)
            rb = jnp.zeros((1, LANES), F32).at[0, 0:N_EXPERTS].set(router_b[i])
            x1, h2, route, counts = _outproj_router(x2, attn_n, conv_n, wout, g2, rw, rb)
            x2 = _moe(h2, x1, route, counts, moe_w_gate[i].astype(BF16), moe_w_up[i].astype(BF16),
                      moe_w_down[i].astype(BF16))
    return x2.reshape(B, S, D_MODEL)
```

```python
import functools
import math

import numpy as np
import jax
import jax.numpy as jnp
from jax import lax
from jax.experimental import pallas as pl
from jax.experimental.pallas import tpu as pltpu

F32 = jnp.float32
BF16 = jnp.bfloat16
I32 = jnp.int32

D_MODEL = 1024
HEAD_DIM = 64
N_HEADS = 8
N_KV = 2
GQA = N_HEADS // N_KV
ATTN_W = N_HEADS * HEAD_DIM
KV_W = N_KV * HEAD_DIM
N_BRANCH = 3
CONV_W = 512
CONV_K = 3
CMP_LEN = 32
CMP_STRIDE = 16
CMP_HIDDEN = 128
SEL_LEN = 64
SEL_TOPK = 16
WINDOW = 512
Q_BLOCK = 128
N_BUCKETS = 32
MAX_DISTANCE = 128
D_FF = 2816
N_EXPERTS = 8
EPS = 1e-6
NEG = -1e30
FORCE = 1e9
MASK_BIG = 2.0 ** 60
SCALE = HEAD_DIM ** -0.5
LOG2E = math.log2(math.e)

LANES = 128
V_ROWS = 80
CHUNK = 512
BLOCKS_PER_CHUNK = CHUNK // SEL_LEN
TC_LEAD = 24
TC_ROWS = 88
TC_WIN = 48
VMEM_LIMIT = 56 * 1024 * 1024


def _bucket_np(dist):
    n = np.maximum(dist, 0)
    max_exact = N_BUCKETS // 2
    nf = np.maximum(n, max_exact).astype(np.float64)
    v = np.log(nf / max_exact) / math.log(MAX_DISTANCE / max_exact) * (N_BUCKETS - max_exact)
    frac = np.abs(v - np.round(v))
    assert np.all((frac > 1e-6) | (n <= max_exact) | (n >= MAX_DISTANCE)), "bucket boundary is precision dependent"
    large = np.minimum(max_exact + (v + 1e-9).astype(np.int32), N_BUCKETS - 1)
    return np.where(n < max_exact, n, large).astype(np.int32)


def _index_tables():
    tl = np.arange(Q_BLOCK)[None, :]
    r = np.arange(WINDOW + Q_BLOCK + WINDOW)[:, None]
    d = tl + WINDOW - r
    idx_w = np.where((d >= 0) & (d < WINDOW), _bucket_np(d), -1)
    r = np.arange(3 * Q_BLOCK)[:, None]
    d = tl + Q_BLOCK - r
    idx_s = np.where(d >= 0, _bucket_np(d), -1)
    r = np.arange(TC_ROWS)[:, None] - TC_LEAD
    d = tl - CMP_STRIDE * r + (CMP_STRIDE * 16 - (CMP_LEN - 1))
    idx_c = np.where((d >= 0) & (r < 32), _bucket_np(d), -1)
    return idx_w.astype(np.int32), idx_s.astype(np.int32), idx_c.astype(np.int32)


def _tables_kernel(rb_ref, iw_ref, is_ref, ic_ref, tw_ref, ts_ref, tc_ref, b31_ref):
    for h in range(N_HEADS):
        g, r = divmod(h, GQA)
        lanes = slice(r * LANES, (r + 1) * LANES)
        for idx_ref, out_ref in ((iw_ref, tw_ref), (is_ref, ts_ref), (ic_ref, tc_ref)):
            out_ref[g, :, lanes] = jnp.full(idx_ref.shape, NEG, F32)

            def body(b, carry, idx_ref=idx_ref, out_ref=out_ref, g=g, lanes=lanes, h=h):
                out_ref[g, :, lanes] = jnp.where(idx_ref[...] == b, rb_ref[b, h] * LOG2E, out_ref[g, :, lanes])
                return carry

            lax.fori_loop(0, N_BUCKETS, body, 0)
        b31_ref[g, :, lanes] = jnp.full((8, LANES), rb_ref[N_BUCKETS - 1, h] * LOG2E, F32)


def _bias_tables(rel_bias):
    idx_w, idx_s, idx_c = _index_tables()
    width = GQA * LANES
    out_shape = (
        jax.ShapeDtypeStruct((N_KV, idx_w.shape[0], width), F32),
        jax.ShapeDtypeStruct((N_KV, idx_s.shape[0], width), F32),
        jax.ShapeDtypeStruct((N_KV, idx_c.shape[0], width), F32),
        jax.ShapeDtypeStruct((N_KV, 8, width), F32),
    )
    vmem = pl.BlockSpec(memory_space=pltpu.VMEM)
    return pl.pallas_call(
        _tables_kernel,
        out_shape=out_shape,
        in_specs=[pl.BlockSpec(memory_space=pltpu.SMEM), vmem, vmem, vmem],
        out_specs=(vmem, vmem, vmem, vmem),
        name="bias_tables",
    )(rel_bias, jnp.asarray(idx_w), jnp.asarray(idx_s), jnp.asarray(idx_c))


def _in_proj_kernel(x_ref, g1_ref, wtok_ref, wfeat_ref, ind_ref, kgain_ref, aug_ref, gq_ref, convw_ref, cgain_ref,
                    qT_ref, gT_ref, vsT_ref, vwT_ref, ks_ref, kw_ref, kcv_ref, convn_ref, zs_ref):
    tm = x_ref.shape[0]
    i = pl.program_id(0)
    x = x_ref[...]
    ms = jnp.mean(x * x, axis=-1, keepdims=True)
    h = (x * lax.rsqrt(ms + EPS) * g1_ref[...]).astype(BF16)
    tok = jnp.dot(h, wtok_ref[...], preferred_element_type=F32)
    feat = lax.dot_general(wfeat_ref[...], h, (((1,), (1,)), ((), ())),
                           preferred_element_type=F32)

    kcv_ref[...] = tok[:, 0:2 * KV_W].astype(BF16)
    kk = tok[:, 2 * KV_W:4 * KV_W]
    ssq = jnp.dot(kk * kk, ind_ref[...], preferred_element_type=F32)
    kn = kk * lax.rsqrt(ssq * (1.0 / HEAD_DIM) + EPS) * kgain_ref[...]
    ksl = kn[:, 0:KV_W]
    lane = lax.broadcasted_iota(I32, (tm, LANES), 1)
    aug = aug_ref[...]
    ks_ref[0] = jnp.where(lane < HEAD_DIM, ksl, aug).astype(BF16)
    ks_ref[1] = jnp.where(lane < HEAD_DIM, pltpu.roll(ksl, HEAD_DIM, 1), aug).astype(BF16)
    kw_ref[...] = kn[:, KV_W:2 * KV_W].astype(BF16)

    c0 = 4 * KV_W
    cb = tok[:, c0:c0 + CONV_W]
    cc = tok[:, c0 + CONV_W:c0 + 2 * CONV_W]
    ch = tok[:, c0 + 2 * CONV_W:c0 + 3 * CONV_W]
    z = cc * ch

    @pl.when(i == 0)
    def _():
        zs_ref[0:8, :] = jnp.zeros((8, CONV_W), F32)

    zs_ref[8:8 + tm, :] = z
    z1 = zs_ref[7:7 + tm, :]
    z2 = zs_ref[6:6 + tm, :]
    w = convw_ref[...]
    y = w[0:1, :] * z2 + w[1:2, :] * z1 + w[2:3, :] * z
    zs_ref[0:8, :] = z[tm - 8:tm, :]
    oc = cb * y
    msc = jnp.mean(oc * oc, axis=-1, keepdims=True)
    convn_ref[...] = (oc * lax.rsqrt(msc + EPS) * cgain_ref[...]).astype(BF16)

    q = feat[0:ATTN_W].reshape(N_HEADS, HEAD_DIM, tm)
    qss = jnp.sum(q * q, axis=1, keepdims=True)
    qn = q * lax.rsqrt(qss * (1.0 / HEAD_DIM) + EPS) * gq_ref[...][None]
    qT_ref[...] = (qn * (SCALE * LOG2E)).reshape(ATTN_W, tm).astype(BF16)
    ones_rows = (lax.broadcasted_iota(I32, (V_ROWS - HEAD_DIM, tm), 0) == 0).astype(BF16)
    for g in range(N_KV):
        r0 = ATTN_W + g * HEAD_DIM
        vsT_ref[g, 0:HEAD_DIM, :] = feat[r0:r0 + HEAD_DIM].astype(BF16)
        vsT_ref[g, HEAD_DIM:V_ROWS, :] = ones_rows
        r1 = ATTN_W + KV_W + g * HEAD_DIM
        vwT_ref[g, 0:HEAD_DIM, :] = feat[r1:r1 + HEAD_DIM].astype(BF16)
        vwT_ref[g, HEAD_DIM:V_ROWS, :] = ones_rows
    g0 = ATTN_W + 2 * KV_W
    gT_ref[...] = jax.nn.sigmoid(feat[g0:g0 + N_BRANCH * N_HEADS])


_IN_TM = 512


def _in_proj(x2, g1, wtok, wfeat, kgain, gq, convw, cgain):
    S = x2.shape[0]
    tm = _IN_TM
    nt = S // tm
    ind = np.kron(np.eye(2 * N_KV, dtype=np.float32), np.ones((HEAD_DIM, HEAD_DIM), np.float32))
    aug = np.zeros((tm, LANES), np.float32)
    blk = (np.arange(tm) // SEL_LEN) % BLOCKS_PER_CHUNK
    aug[np.arange(tm), HEAD_DIM + blk] = 1.0
    aug[:, HEAD_DIM + BLOCKS_PER_CHUNK:HEAD_DIM + BLOCKS_PER_CHUNK + 2] = 1.0
    const = lambda shape: pl.BlockSpec(shape, lambda i: (0,) * len(shape))
    out_shape = (
        jax.ShapeDtypeStruct((ATTN_W, S), BF16),
        jax.ShapeDtypeStruct((N_BRANCH * N_HEADS, S), F32),
        jax.ShapeDtypeStruct((N_KV, V_ROWS, S), BF16),
        jax.ShapeDtypeStruct((N_KV, V_ROWS, S), BF16),
        jax.ShapeDtypeStruct((N_KV, S, LANES), BF16),
        jax.ShapeDtypeStruct((S, LANES), BF16),
        jax.ShapeDtypeStruct((S, 2 * KV_W), BF16),
        jax.ShapeDtypeStruct((S, CONV_W), BF16),
    )
    out_specs = (
        pl.BlockSpec((ATTN_W, tm), lambda i: (0, i)),
        pl.BlockSpec((N_BRANCH * N_HEADS, tm), lambda i: (0, i)),
        pl.BlockSpec((N_KV, V_ROWS, tm), lambda i: (0, 0, i)),
        pl.BlockSpec((N_KV, V_ROWS, tm), lambda i: (0, 0, i)),
        pl.BlockSpec((N_KV, tm, LANES), lambda i: (0, i, 0)),
        pl.BlockSpec((tm, LANES), lambda i: (i, 0)),
        pl.BlockSpec((tm, 2 * KV_W), lambda i: (i, 0)),
        pl.BlockSpec((tm, CONV_W), lambda i: (i, 0)),
    )
    in_specs = [
        pl.BlockSpec((tm, D_MODEL), lambda i: (i, 0)),
        const((1, D_MODEL)),
        const(wtok.shape),
        const(wfeat.shape),
        const(ind.shape),
        const((1, 2 * KV_W)),
        const(aug.shape),
        const((HEAD_DIM, tm)),
        const((8, CONV_W)),
        const((1, CONV_W)),
    ]
    return pl.pallas_call(
        _in_proj_kernel,
        grid=(nt,),
        in_specs=in_specs,
        out_specs=out_specs,
        out_shape=out_shape,
        scratch_shapes=[pltpu.VMEM((tm + 8, CONV_W), F32)],
        compiler_params=pltpu.CompilerParams(dimension_semantics=("arbitrary",), vmem_limit_bytes=VMEM_LIMIT),
        name="in_proj",
    )(x2, g1, wtok, wfeat, jnp.asarray(ind), kgain, jnp.asarray(aug), gq, convw, cgain)


def _compress_kernel(r_ref, wtop_ref, wbot_ref, pos_ref, w1_ref, w2k_ref, w2vT_ref, kgain_ref, kc_ref, vcT_ref):
    nr = r_ref.shape[0]
    r = r_ref[...]
    u = jnp.dot(r, wtop_ref[...], preferred_element_type=F32)
    lo = jnp.dot(r, wbot_ref[...], preferred_element_type=F32)
    bias = jnp.dot(pos_ref[...], w1_ref[...], preferred_element_type=F32)[0:1, :]
    bias4 = jnp.concatenate([bias[:, 0:CMP_HIDDEN]] * N_KV + [bias[:, CMP_HIDDEN:]] * N_KV, axis=1)
    hid = u + pltpu.roll(lo, nr - 1, 0) + bias4
    act = jax.nn.gelu(hid)
    for g in range(N_KV):
        ak = act[:, g * CMP_HIDDEN:(g + 1) * CMP_HIDDEN].astype(BF16)
        av = act[:, (N_KV + g) * CMP_HIDDEN:(N_KV + g + 1) * CMP_HIDDEN].astype(BF16)
        kc = jnp.dot(ak, w2k_ref[...], preferred_element_type=F32)
        ssq = jnp.sum(kc * kc, axis=-1, keepdims=True)
        kc_ref[g] = (kc * lax.rsqrt(ssq * (1.0 / HEAD_DIM) + EPS) * kgain_ref[...]).astype(BF16)
        vcT_ref[g] = lax.dot_general(w2vT_ref[...], av, (((1,), (1,)), ((), ())),
                                     preferred_element_type=F32).astype(BF16)


def _compress(kcv, wtop, wbot, pos, w1, w2k, w2vT, kgain):
    S = kcv.shape[0]
    nr = S // CMP_STRIDE
    r = kcv.reshape(nr, CMP_STRIDE * 2 * KV_W)
    vmem = pl.BlockSpec(memory_space=pltpu.VMEM)
    return pl.pallas_call(
        _compress_kernel,
        out_shape=(jax.ShapeDtypeStruct((N_KV, nr, LANES), BF16),
                   jax.ShapeDtypeStruct((N_KV, HEAD_DIM, nr), BF16)),
        in_specs=[vmem] * 8,
        out_specs=(vmem, vmem),
        compiler_params=pltpu.CompilerParams(vmem_limit_bytes=VMEM_LIMIT),
        name="compress",
    )(r, wtop, wbot, pos, w1, w2k, w2vT, kgain)


def _tile4(row):
    return jnp.concatenate([row] * GQA, axis=1)


def _attn_kernel(qT_ref, gT_ref, ks_ref, vsT_ref, kw_ref, vwT_ref, kc_ref, vcT_ref,
                 tw_ref, ts_ref, tc_ref, b31_ref, gain_ref, out_ref,
                 sc_ref, sw_ref, imp_ref, aoct_ref, aall_ref, qs_ref, sa_ref, sb_ref, qz_ref, qsb_ref):
    ncr = kc_ref.shape[1]
    nsel = aall_ref.shape[1] * BLOCKS_PER_CHUNK
    noct = aoct_ref.shape[1] - 1
    width = GQA * LANES
    c = pl.program_id(0)
    t0 = c * Q_BLOCK
    tl = lax.broadcasted_iota(I32, (1, LANES), 1)
    cur = 2 * c + (tl >= SEL_LEN).astype(I32)
    zb = jnp.maximum(2 * c - 2, 0)
    zs = pl.multiple_of(zb * SEL_LEN, Q_BLOCK)
    ts_off = pl.multiple_of(zs - (t0 - Q_BLOCK), Q_BLOCK)
    zeros_q = jnp.zeros((HEAD_DIM, width), BF16)

    col_ok = _tile4(t0 + tl) >= CMP_LEN - 1
    j = lax.broadcasted_iota(I32, (nsel, LANES), 0)
    forced = (j == 0) | (j == cur) | (j == cur - 1)
    for g in range(N_KV):
        imp_ref[g, 0:8, :] = jnp.zeros((8, LANES), F32)
        imp_ref[g, 8 + ncr:16 + ncr, :] = jnp.zeros((8, LANES), F32)

    ws = pl.multiple_of(jnp.maximum(t0 - WINDOW, 0), Q_BLOCK)
    tw_off = pl.multiple_of(ws - (t0 - WINDOW), Q_BLOCK)
    nw = WINDOW + Q_BLOCK
    lo = pl.multiple_of(jnp.clip((8 * c - 16) // 16 * 16, 0, ncr - TC_WIN), 16)
    tc_off = pl.multiple_of(lo - (8 * c - 16) + TC_LEAD, 8)
    qgs = [jnp.concatenate([qT_ref[(GQA * g + r) * HEAD_DIM:(GQA * g + r + 1) * HEAD_DIM, :]
                            for r in range(GQA)], axis=1) for g in range(N_KV)]

    for g in range(N_KV):
        qc = jnp.concatenate([qgs[g], zeros_q], axis=0)
        b31row = b31_ref[g, 0:1, :]
        s = jnp.dot(kc_ref[g], qc, preferred_element_type=F32)
        row = lax.broadcasted_iota(I32, (ncr, width), 0)
        sc_ref[g] = s + jnp.where(row < lo, b31row, NEG)
        s_loc = jnp.dot(kc_ref[g, pl.ds(lo, TC_WIN), :], qc, preferred_element_type=F32)
        sc_ref[g, pl.ds(lo, TC_WIN), :] = s_loc + tc_ref[g, pl.ds(tc_off, TC_WIN), :]

    for g in range(N_KV):
        qw = jnp.concatenate([qgs[g], zeros_q] if g == 0 else [zeros_q, qgs[g]], axis=0)
        sw_ref[g] = (jnp.dot(kw_ref[pl.ds(ws, nw), :], qw, preferred_element_type=F32)
                     + tw_ref[g, pl.ds(tw_off, nw), :])

    o_cmp, scores0 = [], []
    for g in range(N_KV):
        s = sc_ref[g]
        m = jnp.max(s, axis=0, keepdims=True)
        e = jnp.exp2(s - m)
        l = jnp.sum(e, axis=0, keepdims=True)
        p = e * jnp.where(col_ok, 1.0 / jnp.maximum(l, 1e-30), 0.0)
        o_cmp.append(jnp.dot(vcT_ref[g], p.astype(BF16), preferred_element_type=F32))
        imp_ref[g, 8:8 + ncr, :] = (
            (p[:, 0:LANES] + p[:, LANES:2 * LANES]) + p[:, 2 * LANES:3 * LANES] + p[:, 3 * LANES:4 * LANES])
        isel = jnp.zeros((nsel, LANES), F32)
        for off in (0, -1, 1, 0, 2, 1, 3, 2):
            isel = isel + imp_ref[g, pl.ds(8 + off, nsel, stride=4), :]
        score = jnp.where(forced, -jnp.inf, isel)
        scores0.append(jnp.where(j > cur, NEG, score))

    o_win = []
    for g in range(N_KV):
        sw = sw_ref[g]
        mw = jnp.max(sw, axis=0, keepdims=True)
        pw = jnp.exp2(sw - mw)
        accw = jnp.dot(vwT_ref[g, :, pl.ds(ws, nw)], pw.astype(BF16), preferred_element_type=F32)
        o_win.append(accw[0:HEAD_DIM] * (1.0 / jnp.maximum(accw[HEAD_DIM:HEAD_DIM + 1], 1e-30)))

    picks = list(scores0)
    for _ in range(SEL_TOPK - 3):
        for g in range(N_KV):
            mx = jnp.max(picks[g], axis=0, keepdims=True)
            first = jnp.min(jnp.where(picks[g] == mx, j, nsel), axis=0, keepdims=True)
            picks[g] = jnp.where(j == first, -jnp.inf, picks[g])

    m_init, acc_init, gates = [], [], []
    for g in range(N_KV):
        qg = qgs[g]
        b31row = b31_ref[g, 0:1, :]
        chosen = (picks[g] == -jnp.inf) & (j <= cur)
        a_all = _tile4(jnp.where(chosen, 0.0, -MASK_BIG)).reshape(noct, BLOCKS_PER_CHUNK, width)
        a_far = _tile4(jnp.where(chosen & (j < zb), 0.0, -MASK_BIG)).reshape(noct, BLOCKS_PER_CHUNK, width)
        b_hi = b31row.astype(BF16).astype(F32)
        erow = lax.broadcasted_iota(I32, (8, width), 0)
        extra = jnp.where(erow == 0, b_hi, jnp.where(erow == 1, b31row - b_hi, 0.0))
        aoct_ref[g, 0:noct] = jnp.concatenate(
            [a_far, jnp.broadcast_to(extra[None], (noct, 8, width))], axis=1).astype(BF16)
        aoct_ref[g, noct] = jnp.concatenate([jnp.full((8, width), -MASK_BIG, F32), extra], axis=0).astype(BF16)
        aall_ref[g] = jnp.concatenate([a_all, jnp.zeros((noct, 8, width), F32)], axis=1).astype(BF16)
        for q_ref in (qs_ref, qsb_ref):
            q_ref[g, 0:HEAD_DIM, :] = qg
            q_ref[g, HEAD_DIM + 16:LANES, :] = jnp.zeros((LANES - HEAD_DIM - 16, width), BF16)

        halves = []
        for h in range(2):
            qz_ref[2 * g + h, HEAD_DIM:HEAD_DIM + 16, :] = aall_ref[g, (zb + 2 * h) // BLOCKS_PER_CHUNK]
            qz_ref[2 * g + h, 0:HEAD_DIM, :] = qg
            qz_ref[2 * g + h, HEAD_DIM + 16:LANES, :] = jnp.zeros((LANES - HEAD_DIM - 16, width), BF16)
            sh = jnp.dot(ks_ref[g, pl.ds(zs + h * Q_BLOCK, Q_BLOCK), :], qz_ref[2 * g + h], preferred_element_type=F32)
            halves.append(sh + ts_ref[g, pl.ds(ts_off + h * Q_BLOCK, Q_BLOCK), :])
        sz = jnp.concatenate(halves, axis=0)
        mz = jnp.max(sz, axis=0, keepdims=True)
        pz = jnp.exp2(sz - mz)
        m_init.append(mz)
        acc_init.append(jnp.dot(vsT_ref[g, :, pl.ds(zs, 2 * Q_BLOCK)], pz.astype(BF16), preferred_element_type=F32))
        gates.append([jnp.concatenate([gT_ref[br * N_HEADS + GQA * g + r:br * N_HEADS + GQA * g + r + 1, :]
                                       for r in range(GQA)], axis=1) for br in range(N_BRANCH)])

    nfar = (zb + BLOCKS_PER_CHUNK - 1) // BLOCKS_PER_CHUNK

    nsub = 1
    sub = CHUNK // nsub

    def score(g, u, buf_ref, h, q_ref):
        k0 = pl.multiple_of(jnp.minimum(u, noct - 1) * CHUNK + h * sub, sub)
        s = jnp.dot(ks_ref[g, pl.ds(k0, sub), :], q_ref[g], preferred_element_type=F32)
        buf_ref[g, h * sub:(h + 1) * sub, :] = s
        return jnp.max(s, axis=0, keepdims=True)

    def set_mask_rows(g, u, q_ref):
        oct_id = jnp.where(u < nfar, u, noct)
        q_ref[g, HEAD_DIM:HEAD_DIM + 16, :] = aoct_ref[g, oct_id]

    def accumulate(g, u, buf_ref, h, mn, acc):
        k0 = pl.multiple_of(jnp.minimum(u, noct - 1) * CHUNK + h * sub, sub)
        p = jnp.exp2(buf_ref[g, h * sub:(h + 1) * sub, :] - mn)
        return acc + jnp.dot(vsT_ref[g, :, pl.ds(k0, sub)], p.astype(BF16), preferred_element_type=F32)

    def step(u, cur_ref, nxt_ref, q_ref, state, cms):
        out_state, out_cms = [], []
        for g in range(N_KV):
            m, acc = state[2 * g], state[2 * g + 1]
            mn = jnp.maximum(m, cms[g])
            acc = jnp.exp2(m - mn) * acc
            set_mask_rows(g, u + 1, q_ref)
            cm = None
            for h in range(nsub):
                ch = score(g, u + 1, nxt_ref, h, q_ref)
                cm = ch if cm is None else jnp.maximum(cm, ch)
                acc = accumulate(g, u, cur_ref, h, mn, acc)
            out_cms.append(cm)
            out_state += [mn, acc]
        return out_state, out_cms

    def far(i, carry):
        state, cm_b = step(2 * i, sa_ref, sb_ref, qsb_ref, carry[0:4], carry[4:6])
        state, cm_a = step(2 * i + 1, sb_ref, sa_ref, qs_ref, state, cm_b)
        return tuple(state + cm_a)

    first = []
    for g in range(N_KV):
        set_mask_rows(g, 0, qs_ref)
        cm = None
        for h in range(nsub):
            ch = score(g, 0, sa_ref, h, qs_ref)
            cm = ch if cm is None else jnp.maximum(cm, ch)
        first.append(cm)
    npairs = jnp.maximum(nfar, 1) // 2
    carry = lax.fori_loop(0, npairs, far, (m_init[0], acc_init[0], m_init[1], acc_init[1], first[0], first[1]))
    final = []
    for g in range(N_KV):
        m, acc = carry[2 * g], carry[2 * g + 1]
        mn = jnp.maximum(m, carry[4 + g])
        acc = jnp.exp2(m - mn) * acc
        for h in range(nsub):
            acc = accumulate(g, 2 * npairs, sa_ref, h, mn, acc)
        final += [mn, acc]
    carry = final

    o_all, ssq = [], jnp.zeros((1, LANES), F32)
    for g in range(N_KV):
        acc = carry[2 * g + 1]
        o_sel = acc[0:HEAD_DIM] * (1.0 / jnp.maximum(acc[HEAD_DIM:HEAD_DIM + 1], 1e-30))
        o = gates[g][0] * o_cmp[g] + gates[g][1] * o_sel + gates[g][2] * o_win[g]
        o_all.append(o)
        cs = jnp.sum(o * o, axis=0, keepdims=True)
        ssq = ssq + ((cs[:, 0:LANES] + cs[:, LANES:2 * LANES]) + (cs[:, 2 * LANES:3 * LANES] + cs[:, 3 * LANES:]))
    inv = _tile4(lax.rsqrt(ssq * (1.0 / ATTN_W) + EPS))
    for g in range(N_KV):
        on = o_all[g] * inv * gain_ref[g]
        for k in range(GQA // 2):
            pair = jnp.concatenate([on[:, (2 * k) * LANES:(2 * k + 1) * LANES],
                                    on[:, (2 * k + 1) * LANES:(2 * k + 2) * LANES]], axis=0)
            col = (GQA * g + 2 * k) * HEAD_DIM
            out_ref[:, col:col + 2 * HEAD_DIM] = pair.T.astype(BF16)


def _attention(qT, gT, ks, vsT, kw, vwT, kc, vcT, tables, gain_b):
    S = qT.shape[1]
    nq = S // Q_BLOCK
    ncr = kc.shape[1]
    nsel = S // SEL_LEN
    noct = S // CHUNK
    width = GQA * LANES
    tw, ts, tc, b31 = tables
    vmem = pl.BlockSpec(memory_space=pltpu.VMEM)
    in_specs = [
        pl.BlockSpec((ATTN_W, Q_BLOCK), lambda c: (0, c)),
        pl.BlockSpec((N_BRANCH * N_HEADS, Q_BLOCK), lambda c: (0, c)),
    ] + [vmem] * 11
    return pl.pallas_call(
        _attn_kernel,
        grid=(nq,),
        in_specs=in_specs,
        out_specs=pl.BlockSpec((Q_BLOCK, ATTN_W), lambda c: (c, 0)),
        out_shape=jax.ShapeDtypeStruct((S, ATTN_W), BF16),
        scratch_shapes=[
            pltpu.VMEM((N_KV, ncr, width), F32),
            pltpu.VMEM((N_KV, WINDOW + Q_BLOCK, width), F32),
            pltpu.VMEM((N_KV, ncr + 16, LANES), F32),
            pltpu.VMEM((N_KV, noct + 1, 16, width), BF16),
            pltpu.VMEM((N_KV, noct, 16, width), BF16),
            pltpu.VMEM((N_KV, LANES, width), BF16),
            pltpu.VMEM((N_KV, CHUNK, width), F32),
            pltpu.VMEM((N_KV, CHUNK, width), F32),
            pltpu.VMEM((2 * N_KV, LANES, width), BF16),
            pltpu.VMEM((N_KV, LANES, width), BF16),
        ],
        compiler_params=pltpu.CompilerParams(dimension_semantics=("arbitrary",), vmem_limit_bytes=VMEM_LIMIT),
        name="nsa_attention",
    )(qT, gT, ks, vsT, kw, vwT, kc, vcT, tw, ts, tc, b31, gain_b)


def _mix_and_norm(x_ref, attn_ref, conv_ref, wout_ref, g2_ref):
    x1 = x_ref[...] + jnp.dot(attn_ref[...], wout_ref[0:ATTN_W, :], preferred_element_type=F32) \
        + jnp.dot(conv_ref[...], wout_ref[ATTN_W:, :], preferred_element_type=F32)
    ms = jnp.mean(x1 * x1, axis=-1, keepdims=True)
    h2 = x1 * lax.rsqrt(ms + EPS) * g2_ref[...]
    return x1, h2


def _ffn_kernel(x_ref, attn_ref, conv_ref, wout_ref, g2_ref, wg_ref, wu_ref, wd_ref, out_ref):
    x1, h2 = _mix_and_norm(x_ref, attn_ref, conv_ref, wout_ref, g2_ref)
    h2 = h2.astype(BF16)
    a = jnp.dot(h2, wg_ref[...], preferred_element_type=F32)
    u = jnp.dot(h2, wu_ref[...], preferred_element_type=F32)
    y = (a * jax.nn.sigmoid(a) * u).astype(BF16)
    out_ref[...] = x1 + jnp.dot(y, wd_ref[...], preferred_element_type=F32)


_FFN_TM = 512


def _outproj_ffn(x2, attn_n, conv_n, wout, g2, wg, wu, wd):
    S = x2.shape[0]
    tm = _FFN_TM
    resident = pl.BlockSpec(memory_space=pltpu.VMEM)
    return pl.pallas_call(
        _ffn_kernel,
        grid=(S // tm,),
        in_specs=[
            pl.BlockSpec((tm, D_MODEL), lambda i: (i, 0)),
            pl.BlockSpec((tm, ATTN_W), lambda i: (i, 0)),
            pl.BlockSpec((tm, CONV_W), lambda i: (i, 0)),
            resident, resident, resident, resident, resident,
        ],
        out_specs=pl.BlockSpec((tm, D_MODEL), lambda i: (i, 0)),
        out_shape=jax.ShapeDtypeStruct((S, D_MODEL), F32),
        compiler_params=pltpu.CompilerParams(dimension_semantics=("arbitrary",), vmem_limit_bytes=VMEM_LIMIT),
        name="outproj_ffn",
    )(x2, attn_n, conv_n, wout, g2, wg, wu, wd)


def _router_kernel(x_ref, attn_ref, conv_ref, wout_ref, g2_ref, rw_ref, rb_ref, tri_ref,
                   x1_ref, h2_ref, route_ref, cnt_ref, run_ref):
    tm = x_ref.shape[0]
    i = pl.program_id(0)

    @pl.when(i == 0)
    def _():
        run_ref[...] = jnp.zeros(run_ref.shape, F32)

    x1, h2 = _mix_and_norm(x_ref, attn_ref, conv_ref, wout_ref, g2_ref)
    x1_ref[...] = x1
    h2b = h2.astype(BF16)
    h2_ref[...] = h2b.astype(F32)
    lane = lax.broadcasted_iota(I32, (tm, LANES), 1)
    logits = jnp.dot(h2b, rw_ref[...], preferred_element_type=F32) + rb_ref[...]
    logits = jnp.where(lane < N_EXPERTS, logits, -jnp.inf)
    m1 = jnp.max(logits, axis=-1, keepdims=True)
    i1 = jnp.min(jnp.where(logits == m1, lane, LANES), axis=-1, keepdims=True)
    rest = jnp.where(lane == i1, -jnp.inf, logits)
    m2 = jnp.max(rest, axis=-1, keepdims=True)
    i2 = jnp.min(jnp.where(rest == m2, lane, LANES), axis=-1, keepdims=True)
    e2 = jnp.exp(m2 - m1)
    den = 1.0 + e2
    oh1 = (lane == i1).astype(F32)
    oh2 = (lane == i2).astype(F32)
    both = oh1 + oh2
    before = run_ref[0:1, :] + jnp.dot(tri_ref[...], both.astype(BF16), preferred_element_type=F32)
    rank1 = jnp.sum(before * oh1, axis=-1, keepdims=True)
    rank2 = jnp.sum(before * oh2, axis=-1, keepdims=True)
    fields = (i1.astype(F32), i2.astype(F32), rank1, rank2, 1.0 / den, e2 / den)
    route = jnp.zeros((tm, LANES), F32)
    for k, v in enumerate(fields):
        route = jnp.where(lane == k, v, route)
    route_ref[...] = route
    run_ref[...] = run_ref[...] + jnp.sum(both, axis=0, keepdims=True)
    cnt_ref[...] = run_ref[...]


def _outproj_router(x2, attn_n, conv_n, wout, g2, rw, rb):
    S = x2.shape[0]
    tm = _FFN_TM
    tri = np.tril(np.ones((tm, tm), np.float32), -1)
    const = lambda shape: pl.BlockSpec(shape, lambda i: (0,) * len(shape))
    return pl.pallas_call(
        _router_kernel,
        grid=(S // tm,),
        in_specs=[
            pl.BlockSpec((tm, D_MODEL), lambda i: (i, 0)),
            pl.BlockSpec((tm, ATTN_W), lambda i: (i, 0)),
            pl.BlockSpec((tm, CONV_W), lambda i: (i, 0)),
            const(wout.shape), const((1, D_MODEL)), const(rw.shape), const((1, LANES)), const((tm, tm)),
        ],
        out_specs=(pl.BlockSpec((tm, D_MODEL), lambda i: (i, 0)),
                   pl.BlockSpec((tm, D_MODEL), lambda i: (i, 0)),
                   pl.BlockSpec((tm, LANES), lambda i: (i, 0)),
                   const((8, LANES))),
        out_shape=(jax.ShapeDtypeStruct((S, D_MODEL), F32),
                   jax.ShapeDtypeStruct((S, D_MODEL), F32),
                   jax.ShapeDtypeStruct((S, LANES), F32),
                   jax.ShapeDtypeStruct((8, LANES), F32)),
        scratch_shapes=[pltpu.VMEM((8, LANES), F32)],
        compiler_params=pltpu.CompilerParams(dimension_semantics=("arbitrary",), vmem_limit_bytes=VMEM_LIMIT),
        name="outproj_router",
    )(x2, attn_n, conv_n, wout, g2, rw, rb, jnp.asarray(tri, BF16))


_ROW_TM = 256
_EXP_TM = 512
_DMA_UNROLL = 8


def _row_copy(src_ref, src_row, dst_ref, dst_row, sem):
    return pltpu.make_async_copy(src_ref.at[pl.ds(src_row, 1), :], dst_ref.at[pl.ds(dst_row, 1), :], sem)


def _dispatch_kernel(pos_ref, h_ref, zero_ref, xs_ref, sem):
    del zero_ref
    tm = h_ref.shape[0]

    def issue(t, carry):
        for k in range(2):
            _row_copy(h_ref, t, xs_ref, pos_ref[0, 0, 2 * t + k], sem).start(priority=k)
        return carry

    lax.fori_loop(0, tm, issue, 0, unroll=_DMA_UNROLL)

    def drain(t, carry):
        for k in range(2):
            _row_copy(h_ref, 0, xs_ref, 0, sem).wait()
        return carry

    lax.fori_loop(0, tm, drain, 0, unroll=_DMA_UNROLL)


def _dispatch(pos3, h2, n_rows):
    S = h2.shape[0]
    tm = _ROW_TM
    zeros = jnp.zeros((n_rows, D_MODEL), F32)
    return pl.pallas_call(
        _dispatch_kernel,
        grid=(S // tm,),
        in_specs=[
            pl.BlockSpec((1, 1, 2 * tm), lambda i: (i, 0, 0), memory_space=pltpu.SMEM),
            pl.BlockSpec((tm, D_MODEL), lambda i: (i, 0)),
            pl.BlockSpec(memory_space=pl.ANY),
        ],
        out_specs=pl.BlockSpec(memory_space=pl.ANY),
        out_shape=jax.ShapeDtypeStruct((n_rows, D_MODEL), F32),
        scratch_shapes=[pltpu.SemaphoreType.DMA],
        input_output_aliases={2: 0},
        compiler_params=pltpu.CompilerParams(dimension_semantics=("arbitrary",), vmem_limit_bytes=VMEM_LIMIT),
        name="moe_dispatch",
    )(pos3, h2, zeros)


def _experts_kernel(te_ref, tb_ref, nt_ref, xs_ref, wg_ref, wu_ref, wd_ref, ys_ref):
    i = pl.program_id(0)

    @pl.when(i < nt_ref[0])
    def _():
        x = xs_ref[...].astype(BF16)
        a = jnp.dot(x, wg_ref[0], preferred_element_type=F32)
        u = jnp.dot(x, wu_ref[0], preferred_element_type=F32)
        y = (a * jax.nn.sigmoid(a) * u).astype(BF16)
        ys_ref[...] = jnp.dot(y, wd_ref[0], preferred_element_type=F32)

    @pl.when(i >= nt_ref[0])
    def _():
        ys_ref[...] = jnp.zeros(ys_ref.shape, F32)


def _experts(tile_e, tile_b, n_tiles, xs, wg, wu, wd):
    n_rows = xs.shape[0]
    tm = _EXP_TM
    weights = lambda shape: pl.BlockSpec(shape, lambda i, te, tb, nt: (te[i], 0, 0), pipeline_mode=pl.Buffered(1))
    grid_spec = pltpu.PrefetchScalarGridSpec(
        num_scalar_prefetch=3,
        grid=(n_rows // tm,),
        in_specs=[
            pl.BlockSpec((tm, D_MODEL), lambda i, te, tb, nt: (tb[i], 0)),
            weights((1, D_MODEL, D_FF)), weights((1, D_MODEL, D_FF)), weights((1, D_FF, D_MODEL)),
        ],
        out_specs=pl.BlockSpec((tm, D_MODEL), lambda i, te, tb, nt: (tb[i], 0)),
    )
    return pl.pallas_call(
        _experts_kernel,
        grid_spec=grid_spec,
        out_shape=jax.ShapeDtypeStruct((n_rows, D_MODEL), F32),
        compiler_params=pltpu.CompilerParams(dimension_semantics=("arbitrary",), vmem_limit_bytes=VMEM_LIMIT),
        name="moe_experts",
    )(tile_e, tile_b, n_tiles, xs, wg, wu, wd)


def _combine_kernel(pos_ref, x1_ref, route_ref, ys_ref, out_ref, y1_ref, y2_ref, sem):
    tm = x1_ref.shape[0]

    def issue(t, carry):
        _row_copy(ys_ref, pos_ref[0, 0, 2 * t], y1_ref, t, sem).start(priority=0)
        _row_copy(ys_ref, pos_ref[0, 0, 2 * t + 1], y2_ref, t, sem).start(priority=1)
        return carry

    lax.fori_loop(0, tm, issue, 0, unroll=_DMA_UNROLL)

    def drain(t, carry):
        _row_copy(ys_ref, 0, y1_ref, 0, sem).wait()
        _row_copy(ys_ref, 0, y2_ref, 0, sem).wait()
        return carry

    lax.fori_loop(0, tm, drain, 0, unroll=_DMA_UNROLL)
    lane = lax.broadcasted_iota(I32, (tm, LANES), 1)
    route = route_ref[...]
    w1 = jnp.sum(jnp.where(lane == 4, route, 0.0), axis=-1, keepdims=True)
    w2 = jnp.sum(jnp.where(lane == 5, route, 0.0), axis=-1, keepdims=True)
    out_ref[...] = x1_ref[...] + (y1_ref[...] * w1 + y2_ref[...] * w2)


def _combine(pos3, x1, route, ys):
    S = x1.shape[0]
    tm = _ROW_TM
    return pl.pallas_call(
        _combine_kernel,
        grid=(S // tm,),
        in_specs=[
            pl.BlockSpec((1, 1, 2 * tm), lambda i: (i, 0, 0), memory_space=pltpu.SMEM),
            pl.BlockSpec((tm, D_MODEL), lambda i: (i, 0)),
            pl.BlockSpec((tm, LANES), lambda i: (i, 0)),
            pl.BlockSpec(memory_space=pl.ANY),
        ],
        out_specs=pl.BlockSpec((tm, D_MODEL), lambda i: (i, 0)),
        out_shape=jax.ShapeDtypeStruct((S, D_MODEL), F32),
        scratch_shapes=[pltpu.VMEM((tm, D_MODEL), F32), pltpu.VMEM((tm, D_MODEL), F32), pltpu.SemaphoreType.DMA],
        compiler_params=pltpu.CompilerParams(dimension_semantics=("arbitrary",), vmem_limit_bytes=VMEM_LIMIT),
        name="moe_combine",
    )(pos3, x1, route, ys)


def _moe(h2, x1, route, counts, wg, wu, wd):
    S = h2.shape[0]
    tm = _EXP_TM
    n_tiles_max = 2 * S // tm + N_EXPERTS
    cnt = counts[0, 0:N_EXPERTS].astype(I32)
    tiles = (cnt + tm - 1) // tm
    first = jnp.cumsum(tiles) - tiles
    n_tiles = jnp.sum(tiles)
    idx = jnp.arange(n_tiles_max, dtype=I32)
    last = jnp.minimum(idx, n_tiles - 1)
    tile_e = (jnp.sum(last[:, None] >= first[None, :], axis=1) - 1).astype(I32)
    eid = route[:, 0:2].astype(I32)
    pos = first[eid] * tm + route[:, 2:4].astype(I32)
    pos3 = pos.reshape(S // _ROW_TM, 1, 2 * _ROW_TM)
    xs = _dispatch(pos3, h2, n_tiles_max * tm)
    ys = _experts(tile_e, idx, n_tiles.reshape(1), xs, wg, wu, wd)
    return _combine(pos3, x1, route, ys)


def _split_w_in(w):
    o = np.cumsum([0, ATTN_W] + [KV_W] * 6 + [N_BRANCH * N_HEADS] + [CONV_W] * 3)
    q, kc, vc, ksl, vsl, kwn, vwn, gts, cb, cc, ch = (w[:, o[i]:o[i + 1]] for i in range(11))
    perm = np.array([h * N_BRANCH + br for br in range(N_BRANCH) for h in range(N_HEADS)])
    wtok = jnp.concatenate([kc, vc, ksl, kwn, cb, cc, ch], axis=1).astype(BF16)
    feat = jnp.concatenate([q, vsl, vwn, gts[:, perm], jnp.zeros((D_MODEL, 8), w.dtype)], axis=1)
    return wtok, feat.T.astype(BF16)


def _expand_cmp_w1(w1k, w1v):
    def half(w, l0):
        wl = w.reshape(CMP_LEN, HEAD_DIM, CMP_HIDDEN)[l0:l0 + CMP_STRIDE]
        return wl
    tops = []
    for l0 in (0, CMP_STRIDE):
        blk = jnp.zeros((CMP_STRIDE, 2 * N_KV, HEAD_DIM, 2 * N_KV, CMP_HIDDEN), F32)
        for s, w in ((0, w1k), (1, w1v)):
            for g in range(N_KV):
                blk = blk.at[:, s * N_KV + g, :, s * N_KV + g, :].set(half(w, l0))
        tops.append(blk.reshape(CMP_STRIDE * 2 * KV_W, 2 * N_KV * CMP_HIDDEN).astype(BF16))
    return tops


def kernel(x, rel_bias, norm1, w_in, q_norm, k_norm, cmp_pos_k, cmp_pos_v, cmp_k_w1, cmp_k_w2, cmp_v_w1, cmp_v_w2,
           conv_w, attn_out_norm, conv_out_norm, w_out, norm2, ffn_w_gate, ffn_w_up, ffn_w_down, router_w, router_b,
           moe_w_gate, moe_w_up, moe_w_down):
    B, S, _ = x.shape
    assert B == 1 and S % CHUNK == 0 and S >= WINDOW + Q_BLOCK
    depth = norm1.shape[0]
    x2 = x.reshape(S, D_MODEL)
    tables = _bias_tables(rel_bias)
    for layer in range(depth):
        wtok, wfeat = _split_w_in(w_in[layer])
        kgain = jnp.concatenate([jnp.tile(k_norm[layer, 1], N_KV), jnp.tile(k_norm[layer, 2], N_KV)])[None, :]
        gq = jnp.broadcast_to(q_norm[layer][:, None], (HEAD_DIM, _IN_TM))
        convw = jnp.zeros((8, CONV_W), F32).at[0:CONV_K].set(conv_w[layer])
        qT, gT, vsT, vwT, ks, kw, kcv, conv_n = _in_proj(
            x2, norm1[layer][None, :], wtok, wfeat, kgain, gq, convw, conv_out_norm[layer][None, :])

        wtop, wbot = _expand_cmp_w1(cmp_k_w1[layer], cmp_v_w1[layer])
        pos = jnp.zeros((8, 2 * CMP_LEN * HEAD_DIM), F32)
        pos = pos.at[0, 0:CMP_LEN * HEAD_DIM].set(cmp_pos_k[layer].reshape(-1))
        pos = pos.at[0, CMP_LEN * HEAD_DIM:].set(cmp_pos_v[layer].reshape(-1))
        w1 = jnp.zeros((2 * CMP_LEN * HEAD_DIM, 2 * CMP_HIDDEN), F32)
        w1 = w1.at[0:CMP_LEN * HEAD_DIM, 0:CMP_HIDDEN].set(cmp_k_w1[layer])
        w1 = w1.at[CMP_LEN * HEAD_DIM:, CMP_HIDDEN:].set(cmp_v_w1[layer])
        w2k = jnp.zeros((CMP_HIDDEN, LANES), F32).at[:, 0:HEAD_DIM].set(cmp_k_w2[layer]).astype(BF16)
        w2vT = cmp_v_w2[layer].T.astype(BF16)
        kcgain = jnp.zeros((1, LANES), F32).at[0, 0:HEAD_DIM].set(k_norm[layer, 0])
        kc, vcT = _compress(kcv, wtop, wbot, pos, w1, w2k, w2vT, kcgain)

        gain_b = jnp.broadcast_to(attn_out_norm[layer].reshape(N_KV, GQA, HEAD_DIM).transpose(0, 2, 1)[:, :, :, None],
                                  (N_KV, HEAD_DIM, GQA, LANES)).reshape(N_KV, HEAD_DIM, GQA * LANES)
        attn_n = _attention(qT, gT, ks, vsT, kw, vwT, kc, vcT, tables, gain_b)

        wout = w_out[layer].astype(BF16)
        g2 = norm2[layer][None, :]
        i = layer // 2
        if layer % 2 == 0:
            x2 = _outproj_ffn(x2, attn_n, conv_n, wout, g2, ffn_w_gate[i].astype(BF16), ffn_w_up[i].astype(BF16),
                              ffn_w_down[i].astype(BF16))
        else:
            rw = jnp.zeros((D_MODEL, LANES), F32).at[:, 0:N_EXPERTS].set(router_w[i]).astype(BF16)
            rb = jnp.zeros((1, LANES), F32).at[0, 0:N_EXPERTS].set(router_b[i])
            x1, h2, route, counts = _outproj_router(x2, attn_n, conv_n, wout, g2, rw, rb)
            x2 = _moe(h2, x1, route, counts, moe_w_gate[i].astype(BF16), moe_w_up[i].astype(BF16),
                      moe_w_down[i].astype(BF16))
    return x2.reshape(B, S, D_MODEL)
```

```python
import functools
import math

import numpy as np
import jax
import jax.numpy as jnp
from jax import lax
from jax.experimental import pallas as pl
from jax.experimental.pallas import tpu as pltpu

F32 = jnp.float32
BF16 = jnp.bfloat16
I32 = jnp.int32

D_MODEL = 1024
HEAD_DIM = 64
N_HEADS = 8
N_KV = 2
GQA = N_HEADS // N_KV
ATTN_W = N_HEADS * HEAD_DIM
KV_W = N_KV * HEAD_DIM
N_BRANCH = 3
CONV_W = 512
CONV_K = 3
CMP_LEN = 32
CMP_STRIDE = 16
CMP_HIDDEN = 128
SEL_LEN = 64
SEL_TOPK = 16
WINDOW = 512
Q_BLOCK = 128
N_BUCKETS = 32
MAX_DISTANCE = 128
D_FF = 2816
N_EXPERTS = 8
EPS = 1e-6
NEG = -1e30
FORCE = 1e9
MASK_BIG = 2.0 ** 60
SCALE = HEAD_DIM ** -0.5
LOG2E = math.log2(math.e)

LANES = 128
V_ROWS = 80
CHUNK = 512
BLOCKS_PER_CHUNK = CHUNK // SEL_LEN
TC_LEAD = 24
TC_ROWS = 88
TC_WIN = 48
VMEM_LIMIT = 56 * 1024 * 1024


def _bucket_np(dist):
    n = np.maximum(dist, 0)
    max_exact = N_BUCKETS // 2
    nf = np.maximum(n, max_exact).astype(np.float64)
    v = np.log(nf / max_exact) / math.log(MAX_DISTANCE / max_exact) * (N_BUCKETS - max_exact)
    frac = np.abs(v - np.round(v))
    assert np.all((frac > 1e-6) | (n <= max_exact) | (n >= MAX_DISTANCE)), "bucket boundary is precision dependent"
    large = np.minimum(max_exact + (v + 1e-9).astype(np.int32), N_BUCKETS - 1)
    return np.where(n < max_exact, n, large).astype(np.int32)


def _index_tables():
    tl = np.arange(Q_BLOCK)[None, :]
    r = np.arange(WINDOW + Q_BLOCK + WINDOW)[:, None]
    d = tl + WINDOW - r
    idx_w = np.where((d >= 0) & (d < WINDOW), _bucket_np(d), -1)
    r = np.arange(3 * Q_BLOCK)[:, None]
    d = tl + Q_BLOCK - r
    idx_s = np.where(d >= 0, _bucket_np(d), -1)
    r = np.arange(TC_ROWS)[:, None] - TC_LEAD
    d = tl - CMP_STRIDE * r + (CMP_STRIDE * 16 - (CMP_LEN - 1))
    idx_c = np.where((d >= 0) & (r < 32), _bucket_np(d), -1)
    return idx_w.astype(np.int32), idx_s.astype(np.int32), idx_c.astype(np.int32)


def _tables_kernel(rb_ref, iw_ref, is_ref, ic_ref, tw_ref, ts_ref, tc_ref, b31_ref):
    for h in range(N_HEADS):
        g, r = divmod(h, GQA)
        lanes = slice(r * LANES, (r + 1) * LANES)
        for idx_ref, out_ref in ((iw_ref, tw_ref), (is_ref, ts_ref), (ic_ref, tc_ref)):
            out_ref[g, :, lanes] = jnp.full(idx_ref.shape, NEG, F32)

            def body(b, carry, idx_ref=idx_ref, out_ref=out_ref, g=g, lanes=lanes, h=h):
                out_ref[g, :, lanes] = jnp.where(idx_ref[...] == b, rb_ref[b, h] * LOG2E, out_ref[g, :, lanes])
                return carry

            lax.fori_loop(0, N_BUCKETS, body, 0)
        b31_ref[g, :, lanes] = jnp.full((8, LANES), rb_ref[N_BUCKETS - 1, h] * LOG2E, F32)


def _bias_tables(rel_bias):
    idx_w, idx_s, idx_c = _index_tables()
    width = GQA * LANES
    out_shape = (
        jax.ShapeDtypeStruct((N_KV, idx_w.shape[0], width), F32),
        jax.ShapeDtypeStruct((N_KV, idx_s.shape[0], width), F32),
        jax.ShapeDtypeStruct((N_KV, idx_c.shape[0], width), F32),
        jax.ShapeDtypeStruct((N_KV, 8, width), F32),
    )
    vmem = pl.BlockSpec(memory_space=pltpu.VMEM)
    return pl.pallas_call(
        _tables_kernel,
        out_shape=out_shape,
        in_specs=[pl.BlockSpec(memory_space=pltpu.SMEM), vmem, vmem, vmem],
        out_specs=(vmem, vmem, vmem, vmem),
        name="bias_tables",
    )(rel_bias, jnp.asarray(idx_w), jnp.asarray(idx_s), jnp.asarray(idx_c))


def _in_proj_kernel(x_ref, g1_ref, wtok_ref, wfeat_ref, ind_ref, kgain_ref, aug_ref, gq_ref, convw_ref, cgain_ref,
                    qT_ref, gT_ref, vsT_ref, vwT_ref, ks_ref, kw_ref, kcv_ref, convn_ref, zs_ref):
    tm = x_ref.shape[0]
    i = pl.program_id(0)
    x = x_ref[...]
    ms = jnp.mean(x * x, axis=-1, keepdims=True)
    h = (x * lax.rsqrt(ms + EPS) * g1_ref[...]).astype(BF16)
    tok = jnp.dot(h, wtok_ref[...], preferred_element_type=F32)
    feat = lax.dot_general(wfeat_ref[...], h, (((1,), (1,)), ((), ())),
                           preferred_element_type=F32)

    kcv_ref[...] = tok[:, 0:2 * KV_W].astype(BF16)
    kk = tok[:, 2 * KV_W:4 * KV_W]
    ssq = jnp.dot(kk * kk, ind_ref[...], preferred_element_type=F32)
    kn = kk * lax.rsqrt(ssq * (1.0 / HEAD_DIM) + EPS) * kgain_ref[...]
    ksl = kn[:, 0:KV_W]
    lane = lax.broadcasted_iota(I32, (tm, LANES), 1)
    aug = aug_ref[...]
    ks_ref[0] = jnp.where(lane < HEAD_DIM, ksl, aug).astype(BF16)
    ks_ref[1] = jnp.where(lane < HEAD_DIM, pltpu.roll(ksl, HEAD_DIM, 1), aug).astype(BF16)
    kw_ref[...] = kn[:, KV_W:2 * KV_W].astype(BF16)

    c0 = 4 * KV_W
    cb = tok[:, c0:c0 + CONV_W]
    cc = tok[:, c0 + CONV_W:c0 + 2 * CONV_W]
    ch = tok[:, c0 + 2 * CONV_W:c0 + 3 * CONV_W]
    z = cc * ch

    @pl.when(i == 0)
    def _():
        zs_ref[0:8, :] = jnp.zeros((8, CONV_W), F32)

    zs_ref[8:8 + tm, :] = z
    z1 = zs_ref[7:7 + tm, :]
    z2 = zs_ref[6:6 + tm, :]
    w = convw_ref[...]
    y = w[0:1, :] * z2 + w[1:2, :] * z1 + w[2:3, :] * z
    zs_ref[0:8, :] = z[tm - 8:tm, :]
    oc = cb * y
    msc = jnp.mean(oc * oc, axis=-1, keepdims=True)
    convn_ref[...] = (oc * lax.rsqrt(msc + EPS) * cgain_ref[...]).astype(BF16)

    q = feat[0:ATTN_W].reshape(N_HEADS, HEAD_DIM, tm)
    qss = jnp.sum(q * q, axis=1, keepdims=True)
    qn = q * lax.rsqrt(qss * (1.0 / HEAD_DIM) + EPS) * gq_ref[...][None]
    qT_ref[...] = (qn * (SCALE * LOG2E)).reshape(ATTN_W, tm).astype(BF16)
    ones_rows = (lax.broadcasted_iota(I32, (V_ROWS - HEAD_DIM, tm), 0) == 0).astype(BF16)
    for g in range(N_KV):
        r0 = ATTN_W + g * HEAD_DIM
        vsT_ref[g, 0:HEAD_DIM, :] = feat[r0:r0 + HEAD_DIM].astype(BF16)
        vsT_ref[g, HEAD_DIM:V_ROWS, :] = ones_rows
        r1 = ATTN_W + KV_W + g * HEAD_DIM
        vwT_ref[g, 0:HEAD_DIM, :] = feat[r1:r1 + HEAD_DIM].astype(BF16)
        vwT_ref[g, HEAD_DIM:V_ROWS, :] = ones_rows
    g0 = ATTN_W + 2 * KV_W
    gT_ref[...] = jax.nn.sigmoid(feat[g0:g0 + N_BRANCH * N_HEADS])


_IN_TM = 512


def _in_proj(x2, g1, wtok, wfeat, kgain, gq, convw, cgain):
    S = x2.shape[0]
    tm = _IN_TM
    nt = S // tm
    ind = np.kron(np.eye(2 * N_KV, dtype=np.float32), np.ones((HEAD_DIM, HEAD_DIM), np.float32))
    aug = np.zeros((tm, LANES), np.float32)
    blk = (np.arange(tm) // SEL_LEN) % BLOCKS_PER_CHUNK
    aug[np.arange(tm), HEAD_DIM + blk] = 1.0
    aug[:, HEAD_DIM + BLOCKS_PER_CHUNK:HEAD_DIM + BLOCKS_PER_CHUNK + 2] = 1.0
    const = lambda shape: pl.BlockSpec(shape, lambda i: (0,) * len(shape))
    out_shape = (
        jax.ShapeDtypeStruct((ATTN_W, S), BF16),
        jax.ShapeDtypeStruct((N_BRANCH * N_HEADS, S), F32),
        jax.ShapeDtypeStruct((N_KV, V_ROWS, S), BF16),
        jax.ShapeDtypeStruct((N_KV, V_ROWS, S), BF16),
        jax.ShapeDtypeStruct((N_KV, S, LANES), BF16),
        jax.ShapeDtypeStruct((S, LANES), BF16),
        jax.ShapeDtypeStruct((S, 2 * KV_W), BF16),
        jax.ShapeDtypeStruct((S, CONV_W), BF16),
    )
    out_specs = (
        pl.BlockSpec((ATTN_W, tm), lambda i: (0, i)),
        pl.BlockSpec((N_BRANCH * N_HEADS, tm), lambda i: (0, i)),
        pl.BlockSpec((N_KV, V_ROWS, tm), lambda i: (0, 0, i)),
        pl.BlockSpec((N_KV, V_ROWS, tm), lambda i: (0, 0, i)),
        pl.BlockSpec((N_KV, tm, LANES), lambda i: (0, i, 0)),
        pl.BlockSpec((tm, LANES), lambda i: (i, 0)),
        pl.BlockSpec((tm, 2 * KV_W), lambda i: (i, 0)),
        pl.BlockSpec((tm, CONV_W), lambda i: (i, 0)),
    )
    in_specs = [
        pl.BlockSpec((tm, D_MODEL), lambda i: (i, 0)),
        const((1, D_MODEL)),
        const(wtok.shape),
        const(wfeat.shape),
        const(ind.shape),
        const((1, 2 * KV_W)),
        const(aug.shape),
        const((HEAD_DIM, tm)),
        const((8, CONV_W)),
        const((1, CONV_W)),
    ]
    return pl.pallas_call(
        _in_proj_kernel,
        grid=(nt,),
        in_specs=in_specs,
        out_specs=out_specs,
        out_shape=out_shape,
        scratch_shapes=[pltpu.VMEM((tm + 8, CONV_W), F32)],
        compiler_params=pltpu.CompilerParams(dimension_semantics=("arbitrary",), vmem_limit_bytes=VMEM_LIMIT),
        name="in_proj",
    )(x2, g1, wtok, wfeat, jnp.asarray(ind), kgain, jnp.asarray(aug), gq, convw, cgain)


def _compress_kernel(r_ref, wtop_ref, wbot_ref, pos_ref, w1_ref, w2k_ref, w2vT_ref, kgain_ref, kc_ref, vcT_ref):
    nr = r_ref.shape[0]
    r = r_ref[...]
    u = jnp.dot(r, wtop_ref[...], preferred_element_type=F32)
    lo = jnp.dot(r, wbot_ref[...], preferred_element_type=F32)
    bias = jnp.dot(pos_ref[...], w1_ref[...], preferred_element_type=F32)[0:1, :]
    bias4 = jnp.concatenate([bias[:, 0:CMP_HIDDEN]] * N_KV + [bias[:, CMP_HIDDEN:]] * N_KV, axis=1)
    hid = u + pltpu.roll(lo, nr - 1, 0) + bias4
    act = jax.nn.gelu(hid)
    for g in range(N_KV):
        ak = act[:, g * CMP_HIDDEN:(g + 1) * CMP_HIDDEN].astype(BF16)
        av = act[:, (N_KV + g) * CMP_HIDDEN:(N_KV + g + 1) * CMP_HIDDEN].astype(BF16)
        kc = jnp.dot(ak, w2k_ref[...], preferred_element_type=F32)
        ssq = jnp.sum(kc * kc, axis=-1, keepdims=True)
        kc_ref[g] = (kc * lax.rsqrt(ssq * (1.0 / HEAD_DIM) + EPS) * kgain_ref[...]).astype(BF16)
        vcT_ref[g] = lax.dot_general(w2vT_ref[...], av, (((1,), (1,)), ((), ())),
                                     preferred_element_type=F32).astype(BF16)


def _compress(kcv, wtop, wbot, pos, w1, w2k, w2vT, kgain):
    S = kcv.shape[0]
    nr = S // CMP_STRIDE
    r = kcv.reshape(nr, CMP_STRIDE * 2 * KV_W)
    vmem = pl.BlockSpec(memory_space=pltpu.VMEM)
    return pl.pallas_call(
        _compress_kernel,
        out_shape=(jax.ShapeDtypeStruct((N_KV, nr, LANES), BF16),
                   jax.ShapeDtypeStruct((N_KV, HEAD_DIM, nr), BF16)),
        in_specs=[vmem] * 8,
        out_specs=(vmem, vmem),
        compiler_params=pltpu.CompilerParams(vmem_limit_bytes=VMEM_LIMIT),
        name="compress",
    )(r, wtop, wbot, pos, w1, w2k, w2vT, kgain)


def _tile4(row):
    return jnp.concatenate([row] * GQA, axis=1)


def _attn_kernel(qT_ref, gT_ref, ks_ref, vsT_ref, kw_ref, vwT_ref, kc_ref, vcT_ref,
                 tw_ref, ts_ref, tc_ref, b31_ref, gain_ref, out_ref,
                 sc_ref, sw_ref, imp_ref, aoct_ref, aall_ref, qs_ref, sa_ref, sb_ref, qz_ref, qsb_ref):
    ncr = kc_ref.shape[1]
    nsel = aall_ref.shape[1] * BLOCKS_PER_CHUNK
    noct = aoct_ref.shape[1] - 1
    width = GQA * LANES
    c = pl.program_id(0)
    t0 = c * Q_BLOCK
    tl = lax.broadcasted_iota(I32, (1, LANES), 1)
    cur = 2 * c + (tl >= SEL_LEN).astype(I32)
    zb = jnp.maximum(2 * c - 2, 0)
    zs = pl.multiple_of(zb * SEL_LEN, Q_BLOCK)
    ts_off = pl.multiple_of(zs - (t0 - Q_BLOCK), Q_BLOCK)
    zeros_q = jnp.zeros((HEAD_DIM, width), BF16)

    col_ok = _tile4(t0 + tl) >= CMP_LEN - 1
    j = lax.broadcasted_iota(I32, (nsel, LANES), 0)
    forced = (j == 0) | (j == cur) | (j == cur - 1)
    for g in range(N_KV):
        imp_ref[g, 0:8, :] = jnp.zeros((8, LANES), F32)
        imp_ref[g, 8 + ncr:16 + ncr, :] = jnp.zeros((8, LANES), F32)

    ws = pl.multiple_of(jnp.maximum(t0 - WINDOW, 0), Q_BLOCK)
    tw_off = pl.multiple_of(ws - (t0 - WINDOW), Q_BLOCK)
    nw = WINDOW + Q_BLOCK
    lo = pl.multiple_of(jnp.clip((8 * c - 16) // 16 * 16, 0, ncr - TC_WIN), 16)
    tc_off = pl.multiple_of(lo - (8 * c - 16) + TC_LEAD, 8)
    qgs = [jnp.concatenate([qT_ref[(GQA * g + r) * HEAD_DIM:(GQA * g + r + 1) * HEAD_DIM, :]
                            for r in range(GQA)], axis=1) for g in range(N_KV)]

    for g in range(N_KV):
        qc = jnp.concatenate([qgs[g], zeros_q], axis=0)
        b31row = b31_ref[g, 0:1, :]
        s = jnp.dot(kc_ref[g], qc, preferred_element_type=F32)
        row = lax.broadcasted_iota(I32, (ncr, width), 0)
        sc_ref[g] = s + jnp.where(row < lo, b31row, NEG)
        s_loc = jnp.dot(kc_ref[g, pl.ds(lo, TC_WIN), :], qc, preferred_element_type=F32)
        sc_ref[g, pl.ds(lo, TC_WIN), :] = s_loc + tc_ref[g, pl.ds(tc_off, TC_WIN), :]

    for g in range(N_KV):
        qw = jnp.concatenate([qgs[g], zeros_q] if g == 0 else [zeros_q, qgs[g]], axis=0)
        sw_ref[g] = (jnp.dot(kw_ref[pl.ds(ws, nw), :], qw, preferred_element_type=F32)
                     + tw_ref[g, pl.ds(tw_off, nw), :])

    o_cmp, scores0 = [], []
    for g in range(N_KV):
        s = sc_ref[g]
        m = jnp.max(s, axis=0, keepdims=True)
        e = jnp.exp2(s - m)
        l = jnp.sum(e, axis=0, keepdims=True)
        p = e * jnp.where(col_ok, 1.0 / jnp.maximum(l, 1e-30), 0.0)
        o_cmp.append(jnp.dot(vcT_ref[g], p.astype(BF16), preferred_element_type=F32))
        imp_ref[g, 8:8 + ncr, :] = (
            (p[:, 0:LANES] + p[:, LANES:2 * LANES]) + p[:, 2 * LANES:3 * LANES] + p[:, 3 * LANES:4 * LANES])
        isel = jnp.zeros((nsel, LANES), F32)
        for off in (0, -1, 1, 0, 2, 1, 3, 2):
            isel = isel + imp_ref[g, pl.ds(8 + off, nsel, stride=4), :]
        score = jnp.where(forced, -jnp.inf, isel)
        scores0.append(jnp.where(j > cur, NEG, score))

    o_win = []
    for g in range(N_KV):
        sw = sw_ref[g]
        mw = jnp.max(sw, axis=0, keepdims=True)
        pw = jnp.exp2(sw - mw)
        accw = jnp.dot(vwT_ref[g, :, pl.ds(ws, nw)], pw.astype(BF16), preferred_element_type=F32)
        o_win.append(accw[0:HEAD_DIM] * (1.0 / jnp.maximum(accw[HEAD_DIM:HEAD_DIM + 1], 1e-30)))

    picks = list(scores0)
    for _ in range(SEL_TOPK - 3):
        for g in range(N_KV):
            mx = jnp.max(picks[g], axis=0, keepdims=True)
            first = jnp.min(jnp.where(picks[g] == mx, j, nsel), axis=0, keepdims=True)
            picks[g] = jnp.where(j == first, -jnp.inf, picks[g])

    m_init, acc_init, gates = [], [], []
    for g in range(N_KV):
        qg = qgs[g]
        b31row = b31_ref[g, 0:1, :]
        chosen = (picks[g] == -jnp.inf) & (j <= cur)
        a_all = _tile4(jnp.where(chosen, 0.0, -MASK_BIG)).reshape(noct, BLOCKS_PER_CHUNK, width)
        a_far = _tile4(jnp.where(chosen & (j < zb), 0.0, -MASK_BIG)).reshape(noct, BLOCKS_PER_CHUNK, width)
        b_hi = b31row.astype(BF16).astype(F32)
        erow = lax.broadcasted_iota(I32, (8, width), 0)
        extra = jnp.where(erow == 0, b_hi, jnp.where(erow == 1, b31row - b_hi, 0.0))
        aoct_ref[g, 0:noct] = jnp.concatenate(
            [a_far, jnp.broadcast_to(extra[None], (noct, 8, width))], axis=1).astype(BF16)
        aoct_ref[g, noct] = jnp.concatenate([jnp.full((8, width), -MASK_BIG, F32), extra], axis=0).astype(BF16)
        aall_ref[g] = jnp.concatenate([a_all, jnp.zeros((noct, 8, width), F32)], axis=1).astype(BF16)
        for q_ref in (qs_ref, qsb_ref):
            q_ref[g, 0:HEAD_DIM, :] = qg
            q_ref[g, HEAD_DIM + 16:LANES, :] = jnp.zeros((LANES - HEAD_DIM - 16, width), BF16)

        halves = []
        for h in range(2):
            qz_ref[2 * g + h, HEAD_DIM:HEAD_DIM + 16, :] = aall_ref[g, (zb + 2 * h) // BLOCKS_PER_CHUNK]
            qz_ref[2 * g + h, 0:HEAD_DIM, :] = qg
            qz_ref[2 * g + h, HEAD_DIM + 16:LANES, :] = jnp.zeros((LANES - HEAD_DIM - 16, width), BF16)
            sh = jnp.dot(ks_ref[g, pl.ds(zs + h * Q_BLOCK, Q_BLOCK), :], qz_ref[2 * g + h], preferred_element_type=F32)
            halves.append(sh + ts_ref[g, pl.ds(ts_off + h * Q_BLOCK, Q_BLOCK), :])
        sz = jnp.concatenate(halves, axis=0)
        mz = jnp.max(sz, axis=0, keepdims=True)
        pz = jnp.exp2(sz - mz)
        m_init.append(mz)
        acc_init.append(jnp.dot(vsT_ref[g, :, pl.ds(zs, 2 * Q_BLOCK)], pz.astype(BF16), preferred_element_type=F32))
        gates.append([jnp.concatenate([gT_ref[br * N_HEADS + GQA * g + r:br * N_HEADS + GQA * g + r + 1, :]
                                       for r in range(GQA)], axis=1) for br in range(N_BRANCH)])

    nfar = (zb + BLOCKS_PER_CHUNK - 1) // BLOCKS_PER_CHUNK

    nsub = 1
    sub = CHUNK // nsub

    def score(g, u, buf_ref, h, q_ref):
        k0 = pl.multiple_of(jnp.minimum(u, noct - 1) * CHUNK + h * sub, sub)
        s = jnp.dot(ks_ref[g, pl.ds(k0, sub), :], q_ref[g], preferred_element_type=F32)
        buf_ref[g, h * sub:(h + 1) * sub, :] = s
        return jnp.max(s, axis=0, keepdims=True)

    def set_mask_rows(g, u, q_ref):
        oct_id = jnp.where(u < nfar, u, noct)
        q_ref[g, HEAD_DIM:HEAD_DIM + 16, :] = aoct_ref[g, oct_id]

    def accumulate(g, u, buf_ref, h, mn, acc):
        k0 = pl.multiple_of(jnp.minimum(u, noct - 1) * CHUNK + h * sub, sub)
        p = jnp.exp2(buf_ref[g, h * sub:(h + 1) * sub, :] - mn)
        return acc + jnp.dot(vsT_ref[g, :, pl.ds(k0, sub)], p.astype(BF16), preferred_element_type=F32)

    def step(u, cur_ref, nxt_ref, q_ref, state, cms):
        out_state, out_cms = [], []
        for g in range(N_KV):
            m, acc = state[2 * g], state[2 * g + 1]
            mn = jnp.maximum(m, cms[g])
            acc = jnp.exp2(m - mn) * acc
            set_mask_rows(g, u + 1, q_ref)
            cm = None
            for h in range(nsub):
                ch = score(g, u + 1, nxt_ref, h, q_ref)
                cm = ch if cm is None else jnp.maximum(cm, ch)
                acc = accumulate(g, u, cur_ref, h, mn, acc)
            out_cms.append(cm)
            out_state += [mn, acc]
        return out_state, out_cms

    def far(i, carry):
        state, cm_b = step(2 * i, sa_ref, sb_ref, qsb_ref, carry[0:4], carry[4:6])
        state, cm_a = step(2 * i + 1, sb_ref, sa_ref, qs_ref, state, cm_b)
        return tuple(state + cm_a)

    first = []
    for g in range(N_KV):
        set_mask_rows(g, 0, qs_ref)
        cm = None
        for h in range(nsub):
            ch = score(g, 0, sa_ref, h, qs_ref)
            cm = ch if cm is None else jnp.maximum(cm, ch)
        first.append(cm)
    npairs = jnp.maximum(nfar, 1) // 2
    carry = lax.fori_loop(0, npairs, far, (m_init[0], acc_init[0], m_init[1], acc_init[1], first[0], first[1]))
    final = []
    for g in range(N_KV):
        m, acc = carry[2 * g], carry[2 * g + 1]
        mn = jnp.maximum(m, carry[4 + g])
        acc = jnp.exp2(m - mn) * acc
        for h in range(nsub):
            acc = accumulate(g, 2 * npairs, sa_ref, h, mn, acc)
        final += [mn, acc]
    carry = final

    o_all, ssq = [], jnp.zeros((1, LANES), F32)
    for g in range(N_KV):
        acc = carry[2 * g + 1]
        o_sel = acc[0:HEAD_DIM] * (1.0 / jnp.maximum(acc[HEAD_DIM:HEAD_DIM + 1], 1e-30))
        o = gates[g][0] * o_cmp[g] + gates[g][1] * o_sel + gates[g][2] * o_win[g]
        o_all.append(o)
        cs = jnp.sum(o * o, axis=0, keepdims=True)
        ssq = ssq + ((cs[:, 0:LANES] + cs[:, LANES:2 * LANES]) + (cs[:, 2 * LANES:3 * LANES] + cs[:, 3 * LANES:]))
    inv = _tile4(lax.rsqrt(ssq * (1.0 / ATTN_W) + EPS))
    for g in range(N_KV):
        on = o_all[g] * inv * gain_ref[g]
        for k in range(GQA // 2):
            pair = jnp.concatenate([on[:, (2 * k) * LANES:(2 * k + 1) * LANES],
                                    on[:, (2 * k + 1) * LANES:(2 * k + 2) * LANES]], axis=0)
            col = (GQA * g + 2 * k) * HEAD_DIM
            out_ref[:, col:col + 2 * HEAD_DIM] = pair.T.astype(BF16)


def _attention(qT, gT, ks, vsT, kw, vwT, kc, vcT, tables, gain_b):
    S = qT.shape[1]
    nq = S // Q_BLOCK
    ncr = kc.shape[1]
    nsel = S // SEL_LEN
    noct = S // CHUNK
    width = GQA * LANES
    tw, ts, tc, b31 = tables
    vmem = pl.BlockSpec(memory_space=pltpu.VMEM)
    in_specs = [
        pl.BlockSpec((ATTN_W, Q_BLOCK), lambda c: (0, c)),
        pl.BlockSpec((N_BRANCH * N_HEADS, Q_BLOCK), lambda c: (0, c)),
    ] + [vmem] * 11
    return pl.pallas_call(
        _attn_kernel,
        grid=(nq,),
        in_specs=in_specs,
        out_specs=pl.BlockSpec((Q_BLOCK, ATTN_W), lambda c: (c, 0)),
        out_shape=jax.ShapeDtypeStruct((S, ATTN_W), BF16),
        scratch_shapes=[
            pltpu.VMEM((N_KV, ncr, width), F32),
            pltpu.VMEM((N_KV, WINDOW + Q_BLOCK, width), F32),
            pltpu.VMEM((N_KV, ncr + 16, LANES), F32),
            pltpu.VMEM((N_KV, noct + 1, 16, width), BF16),
            pltpu.VMEM((N_KV, noct, 16, width), BF16),
            pltpu.VMEM((N_KV, LANES, width), BF16),
            pltpu.VMEM((N_KV, CHUNK, width), F32),
            pltpu.VMEM((N_KV, CHUNK, width), F32),
            pltpu.VMEM((2 * N_KV, LANES, width), BF16),
            pltpu.VMEM((N_KV, LANES, width), BF16),
        ],
        compiler_params=pltpu.CompilerParams(dimension_semantics=("arbitrary",), vmem_limit_bytes=VMEM_LIMIT),
        name="nsa_attention",
    )(qT, gT, ks, vsT, kw, vwT, kc, vcT, tw, ts, tc, b31, gain_b)


def _mix_and_norm(x_ref, attn_ref, conv_ref, wout_ref, g2_ref):
    x1 = x_ref[...] + jnp.dot(attn_ref[...], wout_ref[0:ATTN_W, :], preferred_element_type=F32) \
        + jnp.dot(conv_ref[...], wout_ref[ATTN_W:, :], preferred_element_type=F32)
    ms = jnp.mean(x1 * x1, axis=-1, keepdims=True)
    h2 = x1 * lax.rsqrt(ms + EPS) * g2_ref[...]
    return x1, h2


def _ffn_kernel(x_ref, attn_ref, conv_ref, wout_ref, g2_ref, wg_ref, wu_ref, wd_ref, out_ref):
    x1, h2 = _mix_and_norm(x_ref, attn_ref, conv_ref, wout_ref, g2_ref)
    h2 = h2.astype(BF16)
    a = jnp.dot(h2, wg_ref[...], preferred_element_type=F32)
    u = jnp.dot(h2, wu_ref[...], preferred_element_type=F32)
    y = (a * jax.nn.sigmoid(a) * u).astype(BF16)
    out_ref[...] = x1 + jnp.dot(y, wd_ref[...], preferred_element_type=F32)


_FFN_TM = 512


def _outproj_ffn(x2, attn_n, conv_n, wout, g2, wg, wu, wd):
    S = x2.shape[0]
    tm = _FFN_TM
    resident = pl.BlockSpec(memory_space=pltpu.VMEM)
    return pl.pallas_call(
        _ffn_kernel,
        grid=(S // tm,),
        in_specs=[
            pl.BlockSpec((tm, D_MODEL), lambda i: (i, 0)),
            pl.BlockSpec((tm, ATTN_W), lambda i: (i, 0)),
            pl.BlockSpec((tm, CONV_W), lambda i: (i, 0)),
            resident, resident, resident, resident, resident,
        ],
        out_specs=pl.BlockSpec((tm, D_MODEL), lambda i: (i, 0)),
        out_shape=jax.ShapeDtypeStruct((S, D_MODEL), F32),
        compiler_params=pltpu.CompilerParams(dimension_semantics=("arbitrary",), vmem_limit_bytes=VMEM_LIMIT),
        name="outproj_ffn",
    )(x2, attn_n, conv_n, wout, g2, wg, wu, wd)


TOKEN_TILE = (D_MODEL // LANES, LANES)


def _to_token_tiles(ref, rows):
    x = jnp.stack([rows[:, k * LANES:(k + 1) * LANES] for k in range(TOKEN_TILE[0])], axis=0)
    ref[...] = pltpu.einshape('ktl->tkl', x)


def _from_token_tiles(ref):
    x = pltpu.einshape('tkl->ktl', ref[...])
    return jnp.concatenate([x[k] for k in range(TOKEN_TILE[0])], axis=1)


def _router_kernel(x_ref, attn_ref, conv_ref, wout_ref, g2_ref, rw_ref, rb_ref, tri_ref,
                   x1_ref, h2_ref, route_ref, cnt_ref, run_ref):
    tm = x_ref.shape[0]
    i = pl.program_id(0)

    @pl.when(i == 0)
    def _():
        run_ref[...] = jnp.zeros(run_ref.shape, F32)

    x1, h2 = _mix_and_norm(x_ref, attn_ref, conv_ref, wout_ref, g2_ref)
    x1_ref[...] = x1
    h2b = h2.astype(BF16)
    _to_token_tiles(h2_ref, h2b.astype(F32))
    lane = lax.broadcasted_iota(I32, (tm, LANES), 1)
    logits = jnp.dot(h2b, rw_ref[...], preferred_element_type=F32) + rb_ref[...]
    logits = jnp.where(lane < N_EXPERTS, logits, -jnp.inf)
    m1 = jnp.max(logits, axis=-1, keepdims=True)
    i1 = jnp.min(jnp.where(logits == m1, lane, LANES), axis=-1, keepdims=True)
    rest = jnp.where(lane == i1, -jnp.inf, logits)
    m2 = jnp.max(rest, axis=-1, keepdims=True)
    i2 = jnp.min(jnp.where(rest == m2, lane, LANES), axis=-1, keepdims=True)
    e2 = jnp.exp(m2 - m1)
    den = 1.0 + e2
    oh1 = (lane == i1).astype(F32)
    oh2 = (lane == i2).astype(F32)
    both = oh1 + oh2
    before = run_ref[0:1, :] + jnp.dot(tri_ref[...], both.astype(BF16), preferred_element_type=F32)
    rank1 = jnp.sum(before * oh1, axis=-1, keepdims=True)
    rank2 = jnp.sum(before * oh2, axis=-1, keepdims=True)
    fields = (i1.astype(F32), i2.astype(F32), rank1, rank2, 1.0 / den, e2 / den)
    route = jnp.zeros((tm, LANES), F32)
    for k, v in enumerate(fields):
        route = jnp.where(lane == k, v, route)
    route_ref[...] = route
    run_ref[...] = run_ref[...] + jnp.sum(both, axis=0, keepdims=True)
    cnt_ref[...] = run_ref[...]


def _outproj_router(x2, attn_n, conv_n, wout, g2, rw, rb):
    S = x2.shape[0]
    tm = _FFN_TM
    tri = np.tril(np.ones((tm, tm), np.float32), -1)
    const = lambda shape: pl.BlockSpec(shape, lambda i: (0,) * len(shape))
    return pl.pallas_call(
        _router_kernel,
        grid=(S // tm,),
        in_specs=[
            pl.BlockSpec((tm, D_MODEL), lambda i: (i, 0)),
            pl.BlockSpec((tm, ATTN_W), lambda i: (i, 0)),
            pl.BlockSpec((tm, CONV_W), lambda i: (i, 0)),
            const(wout.shape), const((1, D_MODEL)), const(rw.shape), const((1, LANES)), const((tm, tm)),
        ],
        out_specs=(pl.BlockSpec((tm, D_MODEL), lambda i: (i, 0)),
                   pl.BlockSpec((tm,) + TOKEN_TILE, lambda i: (i, 0, 0)),
                   pl.BlockSpec((tm, LANES), lambda i: (i, 0)),
                   const((8, LANES))),
        out_shape=(jax.ShapeDtypeStruct((S, D_MODEL), F32),
                   jax.ShapeDtypeStruct((S,) + TOKEN_TILE, F32),
                   jax.ShapeDtypeStruct((S, LANES), F32),
                   jax.ShapeDtypeStruct((8, LANES), F32)),
        scratch_shapes=[pltpu.VMEM((8, LANES), F32)],
        compiler_params=pltpu.CompilerParams(dimension_semantics=("arbitrary",), vmem_limit_bytes=VMEM_LIMIT),
        name="outproj_router",
    )(x2, attn_n, conv_n, wout, g2, rw, rb, jnp.asarray(tri, BF16))


_ROW_TM = 256
_EXP_TM = 512
_DMA_UNROLL = 8


def _row_copy(src_ref, src_row, dst_ref, dst_row, sem):
    return pltpu.make_async_copy(src_ref.at[src_row], dst_ref.at[dst_row], sem)


def _dispatch_kernel(pos_ref, h_ref, zero_ref, xs_ref, sem):
    del zero_ref
    tm = h_ref.shape[0]

    def issue(t, carry):
        for k in range(2):
            _row_copy(h_ref, t, xs_ref, pos_ref[0, 0, 2 * t + k], sem).start(priority=k)
        return carry

    lax.fori_loop(0, tm, issue, 0, unroll=_DMA_UNROLL)

    def drain(t, carry):
        for k in range(2):
            _row_copy(h_ref, 0, xs_ref, 0, sem).wait()
        return carry

    lax.fori_loop(0, tm, drain, 0, unroll=_DMA_UNROLL)


def _dispatch(pos3, h2, n_rows):
    S = h2.shape[0]
    tm = _ROW_TM
    zeros = jnp.zeros((n_rows,) + TOKEN_TILE, F32)
    return pl.pallas_call(
        _dispatch_kernel,
        grid=(S // tm,),
        in_specs=[
            pl.BlockSpec((1, 1, 2 * tm), lambda i: (i, 0, 0), memory_space=pltpu.SMEM),
            pl.BlockSpec((tm,) + TOKEN_TILE, lambda i: (i, 0, 0)),
            pl.BlockSpec(memory_space=pl.ANY),
        ],
        out_specs=pl.BlockSpec(memory_space=pl.ANY),
        out_shape=jax.ShapeDtypeStruct((n_rows,) + TOKEN_TILE, F32),
        scratch_shapes=[pltpu.SemaphoreType.DMA],
        input_output_aliases={2: 0},
        compiler_params=pltpu.CompilerParams(dimension_semantics=("arbitrary",), vmem_limit_bytes=VMEM_LIMIT),
        name="moe_dispatch",
    )(pos3, h2, zeros)


def _experts_kernel(te_ref, tb_ref, nt_ref, xs_ref, wg_ref, wu_ref, wd_ref, ys_ref):
    i = pl.program_id(0)

    @pl.when(i < nt_ref[0])
    def _():
        x = _from_token_tiles(xs_ref).astype(BF16)
        a = jnp.dot(x, wg_ref[0], preferred_element_type=F32)
        u = jnp.dot(x, wu_ref[0], preferred_element_type=F32)
        y = (a * jax.nn.sigmoid(a) * u).astype(BF16)
        _to_token_tiles(ys_ref, jnp.dot(y, wd_ref[0], preferred_element_type=F32))

    @pl.when(i >= nt_ref[0])
    def _():
        ys_ref[...] = jnp.zeros(ys_ref.shape, F32)


def _experts(tile_e, tile_b, n_tiles, xs, wg, wu, wd):
    n_rows = xs.shape[0]
    tm = _EXP_TM
    weights = lambda shape: pl.BlockSpec(shape, lambda i, te, tb, nt: (te[i], 0, 0), pipeline_mode=pl.Buffered(1))
    grid_spec = pltpu.PrefetchScalarGridSpec(
        num_scalar_prefetch=3,
        grid=(n_rows // tm,),
        in_specs=[
            pl.BlockSpec((tm,) + TOKEN_TILE, lambda i, te, tb, nt: (tb[i], 0, 0)),
            weights((1, D_MODEL, D_FF)), weights((1, D_MODEL, D_FF)), weights((1, D_FF, D_MODEL)),
        ],
        out_specs=pl.BlockSpec((tm,) + TOKEN_TILE, lambda i, te, tb, nt: (tb[i], 0, 0)),
    )
    return pl.pallas_call(
        _experts_kernel,
        grid_spec=grid_spec,
        out_shape=jax.ShapeDtypeStruct((n_rows,) + TOKEN_TILE, F32),
        compiler_params=pltpu.CompilerParams(dimension_semantics=("arbitrary",), vmem_limit_bytes=VMEM_LIMIT),
        name="moe_experts",
    )(tile_e, tile_b, n_tiles, xs, wg, wu, wd)


def _combine_kernel(pos_ref, x1_ref, route_ref, ys_ref, out_ref, y1_ref, y2_ref, sem):
    tm = x1_ref.shape[0]

    def issue(t, carry):
        _row_copy(ys_ref, pos_ref[0, 0, 2 * t], y1_ref, t, sem).start(priority=0)
        _row_copy(ys_ref, pos_ref[0, 0, 2 * t + 1], y2_ref, t, sem).start(priority=1)
        return carry

    lax.fori_loop(0, tm, issue, 0, unroll=_DMA_UNROLL)

    def drain(t, carry):
        _row_copy(ys_ref, 0, y1_ref, 0, sem).wait()
        _row_copy(ys_ref, 0, y2_ref, 0, sem).wait()
        return carry

    lax.fori_loop(0, tm, drain, 0, unroll=_DMA_UNROLL)
    lane = lax.broadcasted_iota(I32, (tm, LANES), 1)
    route = route_ref[...]
    w1 = jnp.sum(jnp.where(lane == 4, route, 0.0), axis=-1, keepdims=True)
    w2 = jnp.sum(jnp.where(lane == 5, route, 0.0), axis=-1, keepdims=True)
    out_ref[...] = x1_ref[...] + (_from_token_tiles(y1_ref) * w1 + _from_token_tiles(y2_ref) * w2)


def _combine(pos3, x1, route, ys):
    S = x1.shape[0]
    tm = _ROW_TM
    return pl.pallas_call(
        _combine_kernel,
        grid=(S // tm,),
        in_specs=[
            pl.BlockSpec((1, 1, 2 * tm), lambda i: (i, 0, 0), memory_space=pltpu.SMEM),
            pl.BlockSpec((tm, D_MODEL), lambda i: (i, 0)),
            pl.BlockSpec((tm, LANES), lambda i: (i, 0)),
            pl.BlockSpec(memory_space=pl.ANY),
        ],
        out_specs=pl.BlockSpec((tm, D_MODEL), lambda i: (i, 0)),
        out_shape=jax.ShapeDtypeStruct((S, D_MODEL), F32),
        scratch_shapes=[pltpu.VMEM((tm,) + TOKEN_TILE, F32), pltpu.VMEM((tm,) + TOKEN_TILE, F32),
                        pltpu.SemaphoreType.DMA],
        compiler_params=pltpu.CompilerParams(dimension_semantics=("arbitrary",), vmem_limit_bytes=VMEM_LIMIT),
        name="moe_combine",
    )(pos3, x1, route, ys)


def _moe(h2, x1, route, counts, wg, wu, wd):
    S = h2.shape[0]
    tm = _EXP_TM
    n_tiles_max = 2 * S // tm + N_EXPERTS
    cnt = counts[0, 0:N_EXPERTS].astype(I32)
    tiles = (cnt + tm - 1) // tm
    first = jnp.cumsum(tiles) - tiles
    n_tiles = jnp.sum(tiles)
    idx = jnp.arange(n_tiles_max, dtype=I32)
    last = jnp.minimum(idx, n_tiles - 1)
    tile_e = (jnp.sum(last[:, None] >= first[None, :], axis=1) - 1).astype(I32)
    eid = route[:, 0:2].astype(I32)
    pos = first[eid] * tm + route[:, 2:4].astype(I32)
    pos3 = pos.reshape(S // _ROW_TM, 1, 2 * _ROW_TM)
    xs = _dispatch(pos3, h2, n_tiles_max * tm)
    ys = _experts(tile_e, idx, n_tiles.reshape(1), xs, wg, wu, wd)
    return _combine(pos3, x1, route, ys)


def _split_w_in(w):
    o = np.cumsum([0, ATTN_W] + [KV_W] * 6 + [N_BRANCH * N_HEADS] + [CONV_W] * 3)
    q, kc, vc, ksl, vsl, kwn, vwn, gts, cb, cc, ch = (w[:, o[i]:o[i + 1]] for i in range(11))
    perm = np.array([h * N_BRANCH + br for br in range(N_BRANCH) for h in range(N_HEADS)])
    wtok = jnp.concatenate([kc, vc, ksl, kwn, cb, cc, ch], axis=1).astype(BF16)
    feat = jnp.concatenate([q, vsl, vwn, gts[:, perm], jnp.zeros((D_MODEL, 8), w.dtype)], axis=1)
    return wtok, feat.T.astype(BF16)


def _expand_cmp_w1(w1k, w1v):
    def half(w, l0):
        wl = w.reshape(CMP_LEN, HEAD_DIM, CMP_HIDDEN)[l0:l0 + CMP_STRIDE]
        return wl
    tops = []
    for l0 in (0, CMP_STRIDE):
        blk = jnp.zeros((CMP_STRIDE, 2 * N_KV, HEAD_DIM, 2 * N_KV, CMP_HIDDEN), F32)
        for s, w in ((0, w1k), (1, w1v)):
            for g in range(N_KV):
                blk = blk.at[:, s * N_KV + g, :, s * N_KV + g, :].set(half(w, l0))
        tops.append(blk.reshape(CMP_STRIDE * 2 * KV_W, 2 * N_KV * CMP_HIDDEN).astype(BF16))
    return tops


def kernel(x, rel_bias, norm1, w_in, q_norm, k_norm, cmp_pos_k, cmp_pos_v, cmp_k_w1, cmp_k_w2, cmp_v_w1, cmp_v_w2,
           conv_w, attn_out_norm, conv_out_norm, w_out, norm2, ffn_w_gate, ffn_w_up, ffn_w_down, router_w, router_b,
           moe_w_gate, moe_w_up, moe_w_down):
    B, S, _ = x.shape
    assert B == 1 and S % CHUNK == 0 and S >= WINDOW + Q_BLOCK
    depth = norm1.shape[0]
    x2 = x.reshape(S, D_MODEL)
    tables = _bias_tables(rel_bias)
    for layer in range(depth):
        wtok, wfeat = _split_w_in(w_in[layer])
        kgain = jnp.concatenate([jnp.tile(k_norm[layer, 1], N_KV), jnp.tile(k_norm[layer, 2], N_KV)])[None, :]
        gq = jnp.broadcast_to(q_norm[layer][:, None], (HEAD_DIM, _IN_TM))
        convw = jnp.zeros((8, CONV_W), F32).at[0:CONV_K].set(conv_w[layer])
        qT, gT, vsT, vwT, ks, kw, kcv, conv_n = _in_proj(
            x2, norm1[layer][None, :], wtok, wfeat, kgain, gq, convw, conv_out_norm[layer][None, :])

        wtop, wbot = _expand_cmp_w1(cmp_k_w1[layer], cmp_v_w1[layer])
        pos = jnp.zeros((8, 2 * CMP_LEN * HEAD_DIM), F32)
        pos = pos.at[0, 0:CMP_LEN * HEAD_DIM].set(cmp_pos_k[layer].reshape(-1))
        pos = pos.at[0, CMP_LEN * HEAD_DIM:].set(cmp_pos_v[layer].reshape(-1))
        w1 = jnp.zeros((2 * CMP_LEN * HEAD_DIM, 2 * CMP_HIDDEN), F32)
        w1 = w1.at[0:CMP_LEN * HEAD_DIM, 0:CMP_HIDDEN].set(cmp_k_w1[layer])
        w1 = w1.at[CMP_LEN * HEAD_DIM:, CMP_HIDDEN:].set(cmp_v_w1[layer])
        w2k = jnp.zeros((CMP_HIDDEN, LANES), F32).at[:, 0:HEAD_DIM].set(cmp_k_w2[layer]).astype(BF16)
        w2vT = cmp_v_w2[layer].T.astype(BF16)
        kcgain = jnp.zeros((1, LANES), F32).at[0, 0:HEAD_DIM].set(k_norm[layer, 0])
        kc, vcT = _compress(kcv, wtop, wbot, pos, w1, w2k, w2vT, kcgain)

        gain_b = jnp.broadcast_to(attn_out_norm[layer].reshape(N_KV, GQA, HEAD_DIM).transpose(0, 2, 1)[:, :, :, None],
                                  (N_KV, HEAD_DIM, GQA, LANES)).reshape(N_KV, HEAD_DIM, GQA * LANES)
        attn_n = _attention(qT, gT, ks, vsT, kw, vwT, kc, vcT, tables, gain_b)

        wout = w_out[layer].astype(BF16)
        g2 = norm2[layer][None, :]
        i = layer // 2
        if layer % 2 == 0:
            x2 = _outproj_ffn(x2, attn_n, conv_n, wout, g2, ffn_w_gate[i].astype(BF16), ffn_w_up[i].astype(BF16),
                              ffn_w_down[i].astype(BF16))
        else:
            rw = jnp.zeros((D_MODEL, LANES), F32).at[:, 0:N_EXPERTS].set(router_w[i]).astype(BF16)
            rb = jnp.zeros((1, LANES), F32).at[0, 0:N_EXPERTS].set(router_b[i])
            x1, h2, route, counts = _outproj_router(x2, attn_n, conv_n, wout, g2, rw, rb)
            x2 = _moe(h2, x1, route, counts, moe_w_gate[i].astype(BF16), moe_w_up[i].astype(BF16),
                      moe_w_down[i].astype(BF16))
    return x2.reshape(B, S, D_MODEL)
```

```python
import functools
import math

import numpy as np
import jax
import jax.numpy as jnp
from jax import lax
from jax.experimental import pallas as pl
from jax.experimental.pallas import tpu as pltpu

F32 = jnp.float32
BF16 = jnp.bfloat16
I32 = jnp.int32

D_MODEL = 1024
HEAD_DIM = 64
N_HEADS = 8
N_KV = 2
GQA = N_HEADS // N_KV
ATTN_W = N_HEADS * HEAD_DIM
KV_W = N_KV * HEAD_DIM
N_BRANCH = 3
CONV_W = 512
CONV_K = 3
CMP_LEN = 32
CMP_STRIDE = 16
CMP_HIDDEN = 128
SEL_LEN = 64
SEL_TOPK = 16
WINDOW = 512
Q_BLOCK = 128
N_BUCKETS = 32
MAX_DISTANCE = 128
D_FF = 2816
N_EXPERTS = 8
EPS = 1e-6
NEG = -1e30
FORCE = 1e9
MASK_BIG = 2.0 ** 60
SCALE = HEAD_DIM ** -0.5
LOG2E = math.log2(math.e)

LANES = 128
V_ROWS = 80
CHUNK = 512
BLOCKS_PER_CHUNK = CHUNK // SEL_LEN
TC_LEAD = 24
TC_ROWS = 88
TC_WIN = 48
VMEM_LIMIT = 56 * 1024 * 1024


def _bucket_np(dist):
    n = np.maximum(dist, 0)
    max_exact = N_BUCKETS // 2
    nf = np.maximum(n, max_exact).astype(np.float64)
    v = np.log(nf / max_exact) / math.log(MAX_DISTANCE / max_exact) * (N_BUCKETS - max_exact)
    frac = np.abs(v - np.round(v))
    assert np.all((frac > 1e-6) | (n <= max_exact) | (n >= MAX_DISTANCE)), "bucket boundary is precision dependent"
    large = np.minimum(max_exact + (v + 1e-9).astype(np.int32), N_BUCKETS - 1)
    return np.where(n < max_exact, n, large).astype(np.int32)


def _index_tables():
    tl = np.arange(Q_BLOCK)[None, :]
    r = np.arange(WINDOW + Q_BLOCK + WINDOW)[:, None]
    d = tl + WINDOW - r
    idx_w = np.where((d >= 0) & (d < WINDOW), _bucket_np(d), -1)
    r = np.arange(3 * Q_BLOCK)[:, None]
    d = tl + Q_BLOCK - r
    idx_s = np.where(d >= 0, _bucket_np(d), -1)
    r = np.arange(TC_ROWS)[:, None] - TC_LEAD
    d = tl - CMP_STRIDE * r + (CMP_STRIDE * 16 - (CMP_LEN - 1))
    idx_c = np.where((d >= 0) & (r < 32), _bucket_np(d), -1)
    return idx_w.astype(np.int32), idx_s.astype(np.int32), idx_c.astype(np.int32)


def _tables_kernel(rb_ref, iw_ref, is_ref, ic_ref, tw_ref, ts_ref, tc_ref, b31_ref):
    for h in range(N_HEADS):
        g, r = divmod(h, GQA)
        lanes = slice(r * LANES, (r + 1) * LANES)
        for idx_ref, out_ref in ((iw_ref, tw_ref), (is_ref, ts_ref), (ic_ref, tc_ref)):
            out_ref[g, :, lanes] = jnp.full(idx_ref.shape, NEG, F32)

            def body(b, carry, idx_ref=idx_ref, out_ref=out_ref, g=g, lanes=lanes, h=h):
                out_ref[g, :, lanes] = jnp.where(idx_ref[...] == b, rb_ref[b, h] * LOG2E, out_ref[g, :, lanes])
                return carry

            lax.fori_loop(0, N_BUCKETS, body, 0)
        b31_ref[g, :, lanes] = jnp.full((8, LANES), rb_ref[N_BUCKETS - 1, h] * LOG2E, F32)


def _bias_tables(rel_bias):
    idx_w, idx_s, idx_c = _index_tables()
    width = GQA * LANES
    out_shape = (
        jax.ShapeDtypeStruct((N_KV, idx_w.shape[0], width), F32),
        jax.ShapeDtypeStruct((N_KV, idx_s.shape[0], width), F32),
        jax.ShapeDtypeStruct((N_KV, idx_c.shape[0], width), F32),
        jax.ShapeDtypeStruct((N_KV, 8, width), F32),
    )
    vmem = pl.BlockSpec(memory_space=pltpu.VMEM)
    return pl.pallas_call(
        _tables_kernel,
        out_shape=out_shape,
        in_specs=[pl.BlockSpec(memory_space=pltpu.SMEM), vmem, vmem, vmem],
        out_specs=(vmem, vmem, vmem, vmem),
        name="bias_tables",
    )(rel_bias, jnp.asarray(idx_w), jnp.asarray(idx_s), jnp.asarray(idx_c))


def _in_proj_kernel(x_ref, g1_ref, wtok_ref, wfeat_ref, ind_ref, kgain_ref, aug_ref, gq_ref, convw_ref, cgain_ref,
                    qT_ref, gT_ref, vsT_ref, vwT_ref, ks_ref, kw_ref, kcv_ref, convn_ref, zs_ref):
    tm = x_ref.shape[0]
    i = pl.program_id(0)
    x = x_ref[...]
    ms = jnp.mean(x * x, axis=-1, keepdims=True)
    h = (x * lax.rsqrt(ms + EPS) * g1_ref[...]).astype(BF16)
    tok = jnp.dot(h, wtok_ref[...], preferred_element_type=F32)
    feat = lax.dot_general(wfeat_ref[...], h, (((1,), (1,)), ((), ())),
                           preferred_element_type=F32)

    kcv_ref[...] = tok[:, 0:2 * KV_W].astype(BF16)
    kk = tok[:, 2 * KV_W:4 * KV_W]
    ssq = jnp.dot(kk * kk, ind_ref[...], preferred_element_type=F32)
    kn = kk * lax.rsqrt(ssq * (1.0 / HEAD_DIM) + EPS) * kgain_ref[...]
    ksl = kn[:, 0:KV_W]
    lane = lax.broadcasted_iota(I32, (tm, LANES), 1)
    aug = aug_ref[...]
    ks_ref[0] = jnp.where(lane < HEAD_DIM, ksl, aug).astype(BF16)
    ks_ref[1] = jnp.where(lane < HEAD_DIM, pltpu.roll(ksl, HEAD_DIM, 1), aug).astype(BF16)
    kw_ref[...] = kn[:, KV_W:2 * KV_W].astype(BF16)

    c0 = 4 * KV_W
    cb = tok[:, c0:c0 + CONV_W]
    cc = tok[:, c0 + CONV_W:c0 + 2 * CONV_W]
    ch = tok[:, c0 + 2 * CONV_W:c0 + 3 * CONV_W]
    z = cc * ch

    @pl.when(i == 0)
    def _():
        zs_ref[0:8, :] = jnp.zeros((8, CONV_W), F32)

    zs_ref[8:8 + tm, :] = z
    z1 = zs_ref[7:7 + tm, :]
    z2 = zs_ref[6:6 + tm, :]
    w = convw_ref[...]
    y = w[0:1, :] * z2 + w[1:2, :] * z1 + w[2:3, :] * z
    zs_ref[0:8, :] = z[tm - 8:tm, :]
    oc = cb * y
    msc = jnp.mean(oc * oc, axis=-1, keepdims=True)
    convn_ref[...] = (oc * lax.rsqrt(msc + EPS) * cgain_ref[...]).astype(BF16)

    q = feat[0:ATTN_W].reshape(N_HEADS, HEAD_DIM, tm)
    qss = jnp.sum(q * q, axis=1, keepdims=True)
    qn = q * lax.rsqrt(qss * (1.0 / HEAD_DIM) + EPS) * gq_ref[...][None]
    qT_ref[...] = (qn * (SCALE * LOG2E)).reshape(ATTN_W, tm).astype(BF16)
    ones_rows = (lax.broadcasted_iota(I32, (V_ROWS - HEAD_DIM, tm), 0) == 0).astype(BF16)
    for g in range(N_KV):
        r0 = ATTN_W + g * HEAD_DIM
        vsT_ref[g, 0:HEAD_DIM, :] = feat[r0:r0 + HEAD_DIM].astype(BF16)
        vsT_ref[g, HEAD_DIM:V_ROWS, :] = ones_rows
        r1 = ATTN_W + KV_W + g * HEAD_DIM
        vwT_ref[g, 0:HEAD_DIM, :] = feat[r1:r1 + HEAD_DIM].astype(BF16)
        vwT_ref[g, HEAD_DIM:V_ROWS, :] = ones_rows
    g0 = ATTN_W + 2 * KV_W
    gT_ref[...] = jax.nn.sigmoid(feat[g0:g0 + N_BRANCH * N_HEADS])


_IN_TM = 512


def _in_proj(x2, g1, wtok, wfeat, kgain, gq, convw, cgain):
    S = x2.shape[0]
    tm = _IN_TM
    nt = S // tm
    ind = np.kron(np.eye(2 * N_KV, dtype=np.float32), np.ones((HEAD_DIM, HEAD_DIM), np.float32))
    aug = np.zeros((tm, LANES), np.float32)
    blk = (np.arange(tm) // SEL_LEN) % BLOCKS_PER_CHUNK
    aug[np.arange(tm), HEAD_DIM + blk] = 1.0
    aug[:, HEAD_DIM + BLOCKS_PER_CHUNK:HEAD_DIM + BLOCKS_PER_CHUNK + 2] = 1.0
    const = lambda shape: pl.BlockSpec(shape, lambda i: (0,) * len(shape))
    out_shape = (
        jax.ShapeDtypeStruct((ATTN_W, S), BF16),
        jax.ShapeDtypeStruct((N_BRANCH * N_HEADS, S), F32),
        jax.ShapeDtypeStruct((N_KV, V_ROWS, S), BF16),
        jax.ShapeDtypeStruct((N_KV, V_ROWS, S), BF16),
        jax.ShapeDtypeStruct((N_KV, S, LANES), BF16),
        jax.ShapeDtypeStruct((S, LANES), BF16),
        jax.ShapeDtypeStruct((S, 2 * KV_W), BF16),
        jax.ShapeDtypeStruct((S, CONV_W), BF16),
    )
    out_specs = (
        pl.BlockSpec((ATTN_W, tm), lambda i: (0, i)),
        pl.BlockSpec((N_BRANCH * N_HEADS, tm), lambda i: (0, i)),
        pl.BlockSpec((N_KV, V_ROWS, tm), lambda i: (0, 0, i)),
        pl.BlockSpec((N_KV, V_ROWS, tm), lambda i: (0, 0, i)),
        pl.BlockSpec((N_KV, tm, LANES), lambda i: (0, i, 0)),
        pl.BlockSpec((tm, LANES), lambda i: (i, 0)),
        pl.BlockSpec((tm, 2 * KV_W), lambda i: (i, 0)),
        pl.BlockSpec((tm, CONV_W), lambda i: (i, 0)),
    )
    in_specs = [
        pl.BlockSpec((tm, D_MODEL), lambda i: (i, 0)),
        const((1, D_MODEL)),
        const(wtok.shape),
        const(wfeat.shape),
        const(ind.shape),
        const((1, 2 * KV_W)),
        const(aug.shape),
        const((HEAD_DIM, tm)),
        const((8, CONV_W)),
        const((1, CONV_W)),
    ]
    return pl.pallas_call(
        _in_proj_kernel,
        grid=(nt,),
        in_specs=in_specs,
        out_specs=out_specs,
        out_shape=out_shape,
        scratch_shapes=[pltpu.VMEM((tm + 8, CONV_W), F32)],
        compiler_params=pltpu.CompilerParams(dimension_semantics=("arbitrary",), vmem_limit_bytes=VMEM_LIMIT),
        name="in_proj",
    )(x2, g1, wtok, wfeat, jnp.asarray(ind), kgain, jnp.asarray(aug), gq, convw, cgain)


def _compress_kernel(r_ref, wtop_ref, wbot_ref, posk_ref, posv_ref, w1k_ref, w1v_ref, w2k_ref, w2vT_ref, kgain_ref,
                     kc_ref, vcT_ref):
    nr = r_ref.shape[0]
    r = r_ref[...]
    u = jnp.dot(r, wtop_ref[...], preferred_element_type=F32)
    lo = jnp.dot(r, wbot_ref[...], preferred_element_type=F32)
    bias_k = jnp.dot(posk_ref[...], w1k_ref[...], preferred_element_type=F32)[0:1, :]
    bias_v = jnp.dot(posv_ref[...], w1v_ref[...], preferred_element_type=F32)[0:1, :]
    bias4 = jnp.concatenate([bias_k] * N_KV + [bias_v] * N_KV, axis=1)
    hid = u + pltpu.roll(lo, nr - 1, 0) + bias4
    act = jax.nn.gelu(hid)
    for g in range(N_KV):
        ak = act[:, g * CMP_HIDDEN:(g + 1) * CMP_HIDDEN].astype(BF16)
        av = act[:, (N_KV + g) * CMP_HIDDEN:(N_KV + g + 1) * CMP_HIDDEN].astype(BF16)
        kc = jnp.dot(ak, w2k_ref[...], preferred_element_type=F32)
        ssq = jnp.sum(kc * kc, axis=-1, keepdims=True)
        kc_ref[g] = (kc * lax.rsqrt(ssq * (1.0 / HEAD_DIM) + EPS) * kgain_ref[...]).astype(BF16)
        vcT_ref[g] = lax.dot_general(w2vT_ref[...], av, (((1,), (1,)), ((), ())),
                                     preferred_element_type=F32).astype(BF16)


def _compress(kcv, wtop, wbot, posk, posv, w1k, w1v, w2k, w2vT, kgain):
    S = kcv.shape[0]
    nr = S // CMP_STRIDE
    r = kcv.reshape(nr, CMP_STRIDE * 2 * KV_W)
    vmem = pl.BlockSpec(memory_space=pltpu.VMEM)
    return pl.pallas_call(
        _compress_kernel,
        out_shape=(jax.ShapeDtypeStruct((N_KV, nr, LANES), BF16),
                   jax.ShapeDtypeStruct((N_KV, HEAD_DIM, nr), BF16)),
        in_specs=[vmem] * 10,
        out_specs=(vmem, vmem),
        compiler_params=pltpu.CompilerParams(vmem_limit_bytes=VMEM_LIMIT),
        name="compress",
    )(r, wtop, wbot, posk, posv, w1k, w1v, w2k, w2vT, kgain)


def _tile4(row):
    return jnp.concatenate([row] * GQA, axis=1)


def _attn_kernel(qT_ref, gT_ref, ks_ref, vsT_ref, kw_ref, vwT_ref, kc_ref, vcT_ref,
                 tw_ref, ts_ref, tc_ref, b31_ref, gain_ref, out_ref,
                 sc_ref, sw_ref, imp_ref, aoct_ref, aall_ref, qs_ref, sa_ref, sb_ref, qz_ref, qsb_ref, fin_ref):
    ncr = kc_ref.shape[1]
    nsel = aall_ref.shape[1] * BLOCKS_PER_CHUNK
    noct = aoct_ref.shape[1] - 1
    width = GQA * LANES
    c = pl.program_id(0)
    t0 = c * Q_BLOCK
    tl = lax.broadcasted_iota(I32, (1, LANES), 1)
    cur = 2 * c + (tl >= SEL_LEN).astype(I32)
    zb = jnp.maximum(2 * c - 2, 0)
    zs = pl.multiple_of(zb * SEL_LEN, Q_BLOCK)
    ts_off = pl.multiple_of(zs - (t0 - Q_BLOCK), Q_BLOCK)
    zeros_q = jnp.zeros((HEAD_DIM, width), BF16)

    col_ok = _tile4(t0 + tl) >= CMP_LEN - 1
    j = lax.broadcasted_iota(I32, (nsel, LANES), 0)
    forced = (j == 0) | (j == cur) | (j == cur - 1)
    for g in range(N_KV):
        imp_ref[g, 0:8, :] = jnp.zeros((8, LANES), F32)
        imp_ref[g, 8 + ncr:16 + ncr, :] = jnp.zeros((8, LANES), F32)

    ws = pl.multiple_of(jnp.maximum(t0 - WINDOW, 0), Q_BLOCK)
    tw_off = pl.multiple_of(ws - (t0 - WINDOW), Q_BLOCK)
    nw = WINDOW + Q_BLOCK
    lo = pl.multiple_of(jnp.clip((8 * c - 16) // 16 * 16, 0, ncr - TC_WIN), 16)
    tc_off = pl.multiple_of(lo - (8 * c - 16) + TC_LEAD, 8)
    qgs = [jnp.concatenate([qT_ref[(GQA * g + r) * HEAD_DIM:(GQA * g + r + 1) * HEAD_DIM, :]
                            for r in range(GQA)], axis=1) for g in range(N_KV)]

    for g in range(N_KV):
        qc = jnp.concatenate([qgs[g], zeros_q], axis=0)
        b31row = b31_ref[g, 0:1, :]
        s = jnp.dot(kc_ref[g], qc, preferred_element_type=F32)
        row = lax.broadcasted_iota(I32, (ncr, width), 0)
        sc_ref[g] = s + jnp.where(row < lo, b31row, NEG)
        s_loc = jnp.dot(kc_ref[g, pl.ds(lo, TC_WIN), :], qc, preferred_element_type=F32)
        sc_ref[g, pl.ds(lo, TC_WIN), :] = s_loc + tc_ref[g, pl.ds(tc_off, TC_WIN), :]

    for g in range(N_KV):
        qw = jnp.concatenate([qgs[g], zeros_q] if g == 0 else [zeros_q, qgs[g]], axis=0)
        sw_ref[g] = (jnp.dot(kw_ref[pl.ds(ws, nw), :], qw, preferred_element_type=F32)
                     + tw_ref[g, pl.ds(tw_off, nw), :])

    o_cmp, scores0 = [], []
    for g in range(N_KV):
        s = sc_ref[g]
        m = jnp.max(s, axis=0, keepdims=True)
        e = jnp.exp2(s - m)
        l = jnp.sum(e, axis=0, keepdims=True)
        p = e * jnp.where(col_ok, 1.0 / jnp.maximum(l, 1e-30), 0.0)
        o_cmp.append(jnp.dot(vcT_ref[g], p.astype(BF16), preferred_element_type=F32))
        imp_ref[g, 8:8 + ncr, :] = (
            (p[:, 0:LANES] + p[:, LANES:2 * LANES]) + p[:, 2 * LANES:3 * LANES] + p[:, 3 * LANES:4 * LANES])
        isel = jnp.zeros((nsel, LANES), F32)
        for off in (0, -1, 1, 0, 2, 1, 3, 2):
            isel = isel + imp_ref[g, pl.ds(8 + off, nsel, stride=4), :]
        score = jnp.where(forced, -jnp.inf, isel)
        scores0.append(jnp.where(j > cur, NEG, score))

    o_win = []
    for g in range(N_KV):
        sw = sw_ref[g]
        mw = jnp.max(sw, axis=0, keepdims=True)
        pw = jnp.exp2(sw - mw)
        accw = jnp.dot(vwT_ref[g, :, pl.ds(ws, nw)], pw.astype(BF16), preferred_element_type=F32)
        o_win.append(accw[0:HEAD_DIM] * (1.0 / jnp.maximum(accw[HEAD_DIM:HEAD_DIM + 1], 1e-30)))

    picks = list(scores0)
    for _ in range(SEL_TOPK - 3):
        for g in range(N_KV):
            mx = jnp.max(picks[g], axis=0, keepdims=True)
            first = jnp.min(jnp.where(picks[g] == mx, j, nsel), axis=0, keepdims=True)
            picks[g] = jnp.where(j == first, -jnp.inf, picks[g])

    m_init, acc_init, gates = [], [], []
    for g in range(N_KV):
        qg = qgs[g]
        b31row = b31_ref[g, 0:1, :]
        chosen = (picks[g] == -jnp.inf) & (j <= cur)
        a_all = _tile4(jnp.where(chosen, 0.0, -MASK_BIG)).reshape(noct, BLOCKS_PER_CHUNK, width)
        a_far = _tile4(jnp.where(chosen & (j < zb), 0.0, -MASK_BIG)).reshape(noct, BLOCKS_PER_CHUNK, width)
        b_hi = b31row.astype(BF16).astype(F32)
        erow = lax.broadcasted_iota(I32, (8, width), 0)
        extra = jnp.where(erow == 0, b_hi, jnp.where(erow == 1, b31row - b_hi, 0.0))
        aoct_ref[g, 0:noct] = jnp.concatenate(
            [a_far, jnp.broadcast_to(extra[None], (noct, 8, width))], axis=1).astype(BF16)
        aoct_ref[g, noct] = jnp.concatenate([jnp.full((8, width), -MASK_BIG, F32), extra], axis=0).astype(BF16)
        aall_ref[g] = jnp.concatenate([a_all, jnp.zeros((noct, 8, width), F32)], axis=1).astype(BF16)
        for q_ref in (qs_ref, qsb_ref):
            q_ref[g, 0:HEAD_DIM, :] = qg
            q_ref[g, HEAD_DIM + 16:LANES, :] = jnp.zeros((LANES - HEAD_DIM - 16, width), BF16)

        halves = []
        for h in range(2):
            qz_ref[2 * g + h, HEAD_DIM:HEAD_DIM + 16, :] = aall_ref[g, (zb + 2 * h) // BLOCKS_PER_CHUNK]
            qz_ref[2 * g + h, 0:HEAD_DIM, :] = qg
            qz_ref[2 * g + h, HEAD_DIM + 16:LANES, :] = jnp.zeros((LANES - HEAD_DIM - 16, width), BF16)
            sh = jnp.dot(ks_ref[g, pl.ds(zs + h * Q_BLOCK, Q_BLOCK), :], qz_ref[2 * g + h], preferred_element_type=F32)
            halves.append(sh + ts_ref[g, pl.ds(ts_off + h * Q_BLOCK, Q_BLOCK), :])
        sz = jnp.concatenate(halves, axis=0)
        mz = jnp.max(sz, axis=0, keepdims=True)
        pz = jnp.exp2(sz - mz)
        m_init.append(mz)
        acc_init.append(jnp.dot(vsT_ref[g, :, pl.ds(zs, 2 * Q_BLOCK)], pz.astype(BF16), preferred_element_type=F32))
        gates.append([jnp.concatenate([gT_ref[br * N_HEADS + GQA * g + r:br * N_HEADS + GQA * g + r + 1, :]
                                       for r in range(GQA)], axis=1) for br in range(N_BRANCH)])

    nfar = (zb + BLOCKS_PER_CHUNK - 1) // BLOCKS_PER_CHUNK

    nsub = 1
    sub = CHUNK // nsub

    def score(g, u, buf_ref, h, q_ref):
        k0 = pl.multiple_of(jnp.minimum(u, noct - 1) * CHUNK + h * sub, sub)
        s = jnp.dot(ks_ref[g, pl.ds(k0, sub), :], q_ref[g], preferred_element_type=F32)
        buf_ref[g, h * sub:(h + 1) * sub, :] = s
        return jnp.max(s, axis=0, keepdims=True)

    def set_mask_rows(g, u, q_ref):
        oct_id = jnp.where(u < nfar, u, noct)
        q_ref[g, HEAD_DIM:HEAD_DIM + 16, :] = aoct_ref[g, oct_id]

    def accumulate(g, u, buf_ref, h, mn, acc):
        k0 = pl.multiple_of(jnp.minimum(u, noct - 1) * CHUNK + h * sub, sub)
        p = jnp.exp2(buf_ref[g, h * sub:(h + 1) * sub, :] - mn)
        return acc + jnp.dot(vsT_ref[g, :, pl.ds(k0, sub)], p.astype(BF16), preferred_element_type=F32)

    def step(u, cur_ref, nxt_ref, q_ref, state, cms):
        out_state, out_cms = [], []
        for g in range(N_KV):
            m, acc = state[2 * g], state[2 * g + 1]
            mn = jnp.maximum(m, cms[g])
            acc = jnp.exp2(m - mn) * acc
            set_mask_rows(g, u + 1, q_ref)
            cm = None
            for h in range(nsub):
                ch = score(g, u + 1, nxt_ref, h, q_ref)
                cm = ch if cm is None else jnp.maximum(cm, ch)
                acc = accumulate(g, u, cur_ref, h, mn, acc)
            out_cms.append(cm)
            out_state += [mn, acc]
        return out_state, out_cms

    def far(i, carry):
        state, cm_b = step(2 * i, sa_ref, sb_ref, qsb_ref, carry[0:4], carry[4:6])
        state, cm_a = step(2 * i + 1, sb_ref, sa_ref, qs_ref, state, cm_b)
        return tuple(state + cm_a)

    first = []
    for g in range(N_KV):
        set_mask_rows(g, 0, qs_ref)
        cm = None
        for h in range(nsub):
            ch = score(g, 0, sa_ref, h, qs_ref)
            cm = ch if cm is None else jnp.maximum(cm, ch)
        first.append(cm)
    nchunks = jnp.maximum(nfar, 1)
    npairs = (nchunks - 1) // 2
    carry = lax.fori_loop(0, npairs, far, (m_init[0], acc_init[0], m_init[1], acc_init[1], first[0], first[1]))
    u_last = 2 * npairs

    def finish(two_left):
        for g in range(N_KV):
            m, acc, cm = carry[2 * g], carry[2 * g + 1], carry[4 + g]
            mn = jnp.maximum(m, cm)
            acc = jnp.exp2(m - mn) * acc
            if two_left:
                set_mask_rows(g, u_last + 1, qsb_ref)
                cm_b = score(g, u_last + 1, sb_ref, 0, qsb_ref)
            acc = accumulate(g, u_last, sa_ref, 0, mn, acc)
            if two_left:
                mn_b = jnp.maximum(mn, cm_b)
                acc = accumulate(g, u_last + 1, sb_ref, 0, mn_b, jnp.exp2(mn - mn_b) * acc)
            fin_ref[g] = acc

    pl.when(nchunks - u_last == 1)(lambda: finish(False))
    pl.when(nchunks - u_last == 2)(lambda: finish(True))
    carry = [None, fin_ref[0], None, fin_ref[1]]

    o_all, ssq = [], jnp.zeros((1, LANES), F32)
    for g in range(N_KV):
        acc = carry[2 * g + 1]
        o_sel = acc[0:HEAD_DIM] * (1.0 / jnp.maximum(acc[HEAD_DIM:HEAD_DIM + 1], 1e-30))
        o = gates[g][0] * o_cmp[g] + gates[g][1] * o_sel + gates[g][2] * o_win[g]
        o_all.append(o)
        cs = jnp.sum(o * o, axis=0, keepdims=True)
        ssq = ssq + ((cs[:, 0:LANES] + cs[:, LANES:2 * LANES]) + (cs[:, 2 * LANES:3 * LANES] + cs[:, 3 * LANES:]))
    inv = _tile4(lax.rsqrt(ssq * (1.0 / ATTN_W) + EPS))
    for g in range(N_KV):
        on = o_all[g] * inv * gain_ref[g]
        for k in range(GQA // 2):
            pair = jnp.concatenate([on[:, (2 * k) * LANES:(2 * k + 1) * LANES],
                                    on[:, (2 * k + 1) * LANES:(2 * k + 2) * LANES]], axis=0)
            col = (GQA * g + 2 * k) * HEAD_DIM
            out_ref[:, col:col + 2 * HEAD_DIM] = pair.T.astype(BF16)


def _attention(qT, gT, ks, vsT, kw, vwT, kc, vcT, tables, gain_b):
    S = qT.shape[1]
    nq = S // Q_BLOCK
    ncr = kc.shape[1]
    nsel = S // SEL_LEN
    noct = S // CHUNK
    width = GQA * LANES
    tw, ts, tc, b31 = tables
    vmem = pl.BlockSpec(memory_space=pltpu.VMEM)
    in_specs = [
        pl.BlockSpec((ATTN_W, Q_BLOCK), lambda c: (0, c)),
        pl.BlockSpec((N_BRANCH * N_HEADS, Q_BLOCK), lambda c: (0, c)),
    ] + [vmem] * 11
    return pl.pallas_call(
        _attn_kernel,
        grid=(nq,),
        in_specs=in_specs,
        out_specs=pl.BlockSpec((Q_BLOCK, ATTN_W), lambda c: (c, 0)),
        out_shape=jax.ShapeDtypeStruct((S, ATTN_W), BF16),
        scratch_shapes=[
            pltpu.VMEM((N_KV, ncr, width), F32),
            pltpu.VMEM((N_KV, WINDOW + Q_BLOCK, width), F32),
            pltpu.VMEM((N_KV, ncr + 16, LANES), F32),
            pltpu.VMEM((N_KV, noct + 1, 16, width), BF16),
            pltpu.VMEM((N_KV, noct, 16, width), BF16),
            pltpu.VMEM((N_KV, LANES, width), BF16),
            pltpu.VMEM((N_KV, CHUNK, width), F32),
            pltpu.VMEM((N_KV, CHUNK, width), F32),
            pltpu.VMEM((2 * N_KV, LANES, width), BF16),
            pltpu.VMEM((N_KV, LANES, width), BF16),
            pltpu.VMEM((N_KV, V_ROWS, width), F32),
        ],
        compiler_params=pltpu.CompilerParams(dimension_semantics=("arbitrary",), vmem_limit_bytes=VMEM_LIMIT),
        name="nsa_attention",
    )(qT, gT, ks, vsT, kw, vwT, kc, vcT, tw, ts, tc, b31, gain_b)


def _mix_and_norm(x_ref, attn_ref, conv_ref, wout_ref, g2_ref):
    x1 = x_ref[...] + jnp.dot(attn_ref[...], wout_ref[0:ATTN_W, :], preferred_element_type=F32) \
        + jnp.dot(conv_ref[...], wout_ref[ATTN_W:, :], preferred_element_type=F32)
    ms = jnp.mean(x1 * x1, axis=-1, keepdims=True)
    h2 = x1 * lax.rsqrt(ms + EPS) * g2_ref[...]
    return x1, h2


def _ffn_kernel(x_ref, attn_ref, conv_ref, wout_ref, g2_ref, wg_ref, wu_ref, wd_ref, out_ref):
    x1, h2 = _mix_and_norm(x_ref, attn_ref, conv_ref, wout_ref, g2_ref)
    h2 = h2.astype(BF16)
    a = jnp.dot(h2, wg_ref[...], preferred_element_type=F32)
    u = jnp.dot(h2, wu_ref[...], preferred_element_type=F32)
    y = (a * jax.nn.sigmoid(a) * u).astype(BF16)
    out_ref[...] = x1 + jnp.dot(y, wd_ref[...], preferred_element_type=F32)


_FFN_TM = 512


def _outproj_ffn(x2, attn_n, conv_n, wout, g2, wg, wu, wd):
    S = x2.shape[0]
    tm = _FFN_TM
    resident = pl.BlockSpec(memory_space=pltpu.VMEM)
    return pl.pallas_call(
        _ffn_kernel,
        grid=(S // tm,),
        in_specs=[
            pl.BlockSpec((tm, D_MODEL), lambda i: (i, 0)),
            pl.BlockSpec((tm, ATTN_W), lambda i: (i, 0)),
            pl.BlockSpec((tm, CONV_W), lambda i: (i, 0)),
            resident, resident, resident, resident, resident,
        ],
        out_specs=pl.BlockSpec((tm, D_MODEL), lambda i: (i, 0)),
        out_shape=jax.ShapeDtypeStruct((S, D_MODEL), F32),
        compiler_params=pltpu.CompilerParams(dimension_semantics=("arbitrary",), vmem_limit_bytes=VMEM_LIMIT),
        name="outproj_ffn",
    )(x2, attn_n, conv_n, wout, g2, wg, wu, wd)


TOKEN_TILE = (D_MODEL // LANES, LANES)


def _to_token_tiles(ref, rows):
    x = jnp.stack([rows[:, k * LANES:(k + 1) * LANES] for k in range(TOKEN_TILE[0])], axis=0)
    ref[...] = pltpu.einshape('ktl->tkl', x)


def _from_token_tiles(ref):
    x = pltpu.einshape('tkl->ktl', ref[...])
    return jnp.concatenate([x[k] for k in range(TOKEN_TILE[0])], axis=1)


def _router_kernel(x_ref, attn_ref, conv_ref, wout_ref, g2_ref, rw_ref, rb_ref, tri_ref,
                   x1_ref, h2_ref, route_ref, cnt_ref, run_ref):
    tm = x_ref.shape[0]
    i = pl.program_id(0)

    @pl.when(i == 0)
    def _():
        run_ref[...] = jnp.zeros(run_ref.shape, F32)

    x1, h2 = _mix_and_norm(x_ref, attn_ref, conv_ref, wout_ref, g2_ref)
    x1_ref[...] = x1
    h2b = h2.astype(BF16)
    _to_token_tiles(h2_ref, h2b.astype(F32))
    lane = lax.broadcasted_iota(I32, (tm, LANES), 1)
    logits = jnp.dot(h2b, rw_ref[...], preferred_element_type=F32) + rb_ref[...]
    logits = jnp.where(lane < N_EXPERTS, logits, -jnp.inf)
    m1 = jnp.max(logits, axis=-1, keepdims=True)
    i1 = jnp.min(jnp.where(logits == m1, lane, LANES), axis=-1, keepdims=True)
    rest = jnp.where(lane == i1, -jnp.inf, logits)
    m2 = jnp.max(rest, axis=-1, keepdims=True)
    i2 = jnp.min(jnp.where(rest == m2, lane, LANES), axis=-1, keepdims=True)
    e2 = jnp.exp(m2 - m1)
    den = 1.0 + e2
    oh1 = (lane == i1).astype(F32)
    oh2 = (lane == i2).astype(F32)
    both = oh1 + oh2
    before = run_ref[0:1, :] + jnp.dot(tri_ref[...], both.astype(BF16), preferred_element_type=F32)
    rank1 = jnp.sum(before * oh1, axis=-1, keepdims=True)
    rank2 = jnp.sum(before * oh2, axis=-1, keepdims=True)
    fields = (i1.astype(F32), i2.astype(F32), rank1, rank2, 1.0 / den, e2 / den)
    route = jnp.zeros((tm, LANES), F32)
    for k, v in enumerate(fields):
        route = jnp.where(lane == k, v, route)
    route_ref[...] = route
    run_ref[...] = run_ref[...] + jnp.sum(both, axis=0, keepdims=True)
    cnt_ref[...] = run_ref[...]


def _outproj_router(x2, attn_n, conv_n, wout, g2, rw, rb):
    S = x2.shape[0]
    tm = _FFN_TM
    tri = np.tril(np.ones((tm, tm), np.float32), -1)
    const = lambda shape: pl.BlockSpec(shape, lambda i: (0,) * len(shape))
    return pl.pallas_call(
        _router_kernel,
        grid=(S // tm,),
        in_specs=[
            pl.BlockSpec((tm, D_MODEL), lambda i: (i, 0)),
            pl.BlockSpec((tm, ATTN_W), lambda i: (i, 0)),
            pl.BlockSpec((tm, CONV_W), lambda i: (i, 0)),
            const(wout.shape), const((1, D_MODEL)), const(rw.shape), const((1, LANES)), const((tm, tm)),
        ],
        out_specs=(pl.BlockSpec((tm, D_MODEL), lambda i: (i, 0)),
                   pl.BlockSpec((tm,) + TOKEN_TILE, lambda i: (i, 0, 0)),
                   pl.BlockSpec((tm, LANES), lambda i: (i, 0)),
                   const((8, LANES))),
        out_shape=(jax.ShapeDtypeStruct((S, D_MODEL), F32),
                   jax.ShapeDtypeStruct((S,) + TOKEN_TILE, F32),
                   jax.ShapeDtypeStruct((S, LANES), F32),
                   jax.ShapeDtypeStruct((8, LANES), F32)),
        scratch_shapes=[pltpu.VMEM((8, LANES), F32)],
        compiler_params=pltpu.CompilerParams(dimension_semantics=("arbitrary",), vmem_limit_bytes=VMEM_LIMIT),
        name="outproj_router",
    )(x2, attn_n, conv_n, wout, g2, rw, rb, jnp.asarray(tri, BF16))


_ROW_TM = 256
_EXP_TM = 512
_DMA_UNROLL = 8


def _row_copy(src_ref, src_row, dst_ref, dst_row, sem):
    return pltpu.make_async_copy(src_ref.at[src_row], dst_ref.at[dst_row], sem)


def _dispatch_kernel(pos_ref, h_ref, zero_ref, xs_ref, sem):
    del zero_ref
    tm = h_ref.shape[0]

    def issue(t, carry):
        for k in range(2):
            _row_copy(h_ref, t, xs_ref, pos_ref[0, 0, 2 * t + k], sem).start(priority=k)
        return carry

    lax.fori_loop(0, tm, issue, 0, unroll=_DMA_UNROLL)

    def drain(t, carry):
        for k in range(2):
            _row_copy(h_ref, 0, xs_ref, 0, sem).wait()
        return carry

    lax.fori_loop(0, tm, drain, 0, unroll=_DMA_UNROLL)


def _dispatch(pos3, h2, n_rows):
    S = h2.shape[0]
    tm = _ROW_TM
    zeros = jnp.zeros((n_rows,) + TOKEN_TILE, F32)
    return pl.pallas_call(
        _dispatch_kernel,
        grid=(S // tm,),
        in_specs=[
            pl.BlockSpec((1, 1, 2 * tm), lambda i: (i, 0, 0), memory_space=pltpu.SMEM),
            pl.BlockSpec((tm,) + TOKEN_TILE, lambda i: (i, 0, 0)),
            pl.BlockSpec(memory_space=pl.ANY),
        ],
        out_specs=pl.BlockSpec(memory_space=pl.ANY),
        out_shape=jax.ShapeDtypeStruct((n_rows,) + TOKEN_TILE, F32),
        scratch_shapes=[pltpu.SemaphoreType.DMA],
        input_output_aliases={2: 0},
        compiler_params=pltpu.CompilerParams(dimension_semantics=("arbitrary",), vmem_limit_bytes=VMEM_LIMIT),
        name="moe_dispatch",
    )(pos3, h2, zeros)


def _experts_kernel(te_ref, tb_ref, nt_ref, xs_ref, wg_ref, wu_ref, wd_ref, ys_ref):
    i = pl.program_id(0)

    @pl.when(i < nt_ref[0])
    def _():
        x = _from_token_tiles(xs_ref).astype(BF16)
        a = jnp.dot(x, wg_ref[0], preferred_element_type=F32)
        u = jnp.dot(x, wu_ref[0], preferred_element_type=F32)
        y = (a * jax.nn.sigmoid(a) * u).astype(BF16)
        _to_token_tiles(ys_ref, jnp.dot(y, wd_ref[0], preferred_element_type=F32))

    @pl.when(i >= nt_ref[0])
    def _():
        ys_ref[...] = jnp.zeros(ys_ref.shape, F32)


def _experts(tile_e, tile_b, n_tiles, xs, wg, wu, wd):
    n_rows = xs.shape[0]
    tm = _EXP_TM
    weights = lambda shape: pl.BlockSpec(shape, lambda i, te, tb, nt: (te[i], 0, 0), pipeline_mode=pl.Buffered(1))
    grid_spec = pltpu.PrefetchScalarGridSpec(
        num_scalar_prefetch=3,
        grid=(n_rows // tm,),
        in_specs=[
            pl.BlockSpec((tm,) + TOKEN_TILE, lambda i, te, tb, nt: (tb[i], 0, 0)),
            weights((1, D_MODEL, D_FF)), weights((1, D_MODEL, D_FF)), weights((1, D_FF, D_MODEL)),
        ],
        out_specs=pl.BlockSpec((tm,) + TOKEN_TILE, lambda i, te, tb, nt: (tb[i], 0, 0)),
    )
    return pl.pallas_call(
        _experts_kernel,
        grid_spec=grid_spec,
        out_shape=jax.ShapeDtypeStruct((n_rows,) + TOKEN_TILE, F32),
        compiler_params=pltpu.CompilerParams(dimension_semantics=("arbitrary",), vmem_limit_bytes=VMEM_LIMIT),
        name="moe_experts",
    )(tile_e, tile_b, n_tiles, xs, wg, wu, wd)


def _combine_kernel(pos_ref, x1_ref, route_ref, ys_ref, out_ref, y1_ref, y2_ref, sem):
    tm = x1_ref.shape[0]

    def issue(t, carry):
        _row_copy(ys_ref, pos_ref[0, 0, 2 * t], y1_ref, t, sem).start(priority=0)
        _row_copy(ys_ref, pos_ref[0, 0, 2 * t + 1], y2_ref, t, sem).start(priority=1)
        return carry

    lax.fori_loop(0, tm, issue, 0, unroll=_DMA_UNROLL)

    def drain(t, carry):
        _row_copy(ys_ref, 0, y1_ref, 0, sem).wait()
        _row_copy(ys_ref, 0, y2_ref, 0, sem).wait()
        return carry

    lax.fori_loop(0, tm, drain, 0, unroll=_DMA_UNROLL)
    lane = lax.broadcasted_iota(I32, (tm, LANES), 1)
    route = route_ref[...]
    w1 = jnp.sum(jnp.where(lane == 4, route, 0.0), axis=-1, keepdims=True)
    w2 = jnp.sum(jnp.where(lane == 5, route, 0.0), axis=-1, keepdims=True)
    out_ref[...] = x1_ref[...] + (_from_token_tiles(y1_ref) * w1 + _from_token_tiles(y2_ref) * w2)


def _combine(pos3, x1, route, ys):
    S = x1.shape[0]
    tm = _ROW_TM
    return pl.pallas_call(
        _combine_kernel,
        grid=(S // tm,),
        in_specs=[
            pl.BlockSpec((1, 1, 2 * tm), lambda i: (i, 0, 0), memory_space=pltpu.SMEM),
            pl.BlockSpec((tm, D_MODEL), lambda i: (i, 0)),
            pl.BlockSpec((tm, LANES), lambda i: (i, 0)),
            pl.BlockSpec(memory_space=pl.ANY),
        ],
        out_specs=pl.BlockSpec((tm, D_MODEL), lambda i: (i, 0)),
        out_shape=jax.ShapeDtypeStruct((S, D_MODEL), F32),
        scratch_shapes=[pltpu.VMEM((tm,) + TOKEN_TILE, F32), pltpu.VMEM((tm,) + TOKEN_TILE, F32),
                        pltpu.SemaphoreType.DMA],
        compiler_params=pltpu.CompilerParams(dimension_semantics=("arbitrary",), vmem_limit_bytes=VMEM_LIMIT),
        name="moe_combine",
    )(pos3, x1, route, ys)


def _moe(h2, x1, route, counts, wg, wu, wd):
    S = h2.shape[0]
    tm = _EXP_TM
    n_tiles_max = 2 * S // tm + N_EXPERTS
    cnt = counts[0, 0:N_EXPERTS].astype(I32)
    tiles = (cnt + tm - 1) // tm
    first = jnp.cumsum(tiles) - tiles
    n_tiles = jnp.sum(tiles)
    idx = jnp.arange(n_tiles_max, dtype=I32)
    last = jnp.minimum(idx, n_tiles - 1)
    tile_e = (jnp.sum(last[:, None] >= first[None, :], axis=1) - 1).astype(I32)
    eid = route[:, 0:2].astype(I32)
    pos = first[eid] * tm + route[:, 2:4].astype(I32)
    pos3 = pos.reshape(S // _ROW_TM, 1, 2 * _ROW_TM)
    xs = _dispatch(pos3, h2, n_tiles_max * tm)
    ys = _experts(tile_e, idx, n_tiles.reshape(1), xs, wg, wu, wd)
    return _combine(pos3, x1, route, ys)


def _split_w_in(w):
    o = np.cumsum([0, ATTN_W] + [KV_W] * 6 + [N_BRANCH * N_HEADS] + [CONV_W] * 3)
    q, kc, vc, ksl, vsl, kwn, vwn, gts, cb, cc, ch = (w[:, o[i]:o[i + 1]] for i in range(11))
    perm = np.array([h * N_BRANCH + br for br in range(N_BRANCH) for h in range(N_HEADS)])
    wtok = jnp.concatenate([kc, vc, ksl, kwn, cb, cc, ch], axis=1).astype(BF16)
    feat = jnp.concatenate([q, vsl, vwn, gts[:, perm], jnp.zeros((D_MODEL, 8), w.dtype)], axis=1)
    return wtok, feat.T.astype(BF16)


def _expand_cmp_w1(w1k, w1v):
    kinds = jnp.stack([w1k, w1v]).reshape(2, CMP_LEN, HEAD_DIM, CMP_HIDDEN)
    per_col = jnp.repeat(kinds, N_KV, axis=0)
    eye = jnp.eye(2 * N_KV, dtype=F32)
    out = []
    for l0 in (0, CMP_STRIDE):
        w = per_col[:, l0:l0 + CMP_STRIDE].transpose(1, 0, 2, 3)
        blk = w[:, :, :, None, :] * eye[None, :, None, :, None]
        out.append(blk.reshape(CMP_STRIDE * 2 * KV_W, 2 * N_KV * CMP_HIDDEN).astype(BF16))
    return out


def kernel(x, rel_bias, norm1, w_in, q_norm, k_norm, cmp_pos_k, cmp_pos_v, cmp_k_w1, cmp_k_w2, cmp_v_w1, cmp_v_w2,
           conv_w, attn_out_norm, conv_out_norm, w_out, norm2, ffn_w_gate, ffn_w_up, ffn_w_down, router_w, router_b,
           moe_w_gate, moe_w_up, moe_w_down):
    B, S, _ = x.shape
    assert B == 1 and S % CHUNK == 0 and S >= WINDOW + Q_BLOCK
    depth = norm1.shape[0]
    x2 = x.reshape(S, D_MODEL)
    tables = _bias_tables(rel_bias)
    for layer in range(depth):
        wtok, wfeat = _split_w_in(w_in[layer])
        kgain = jnp.concatenate([jnp.tile(k_norm[layer, 1], N_KV), jnp.tile(k_norm[layer, 2], N_KV)])[None, :]
        gq = jnp.broadcast_to(q_norm[layer][:, None], (HEAD_DIM, _IN_TM))
        convw = jnp.pad(conv_w[layer], ((0, 8 - CONV_K), (0, 0)))
        qT, gT, vsT, vwT, ks, kw, kcv, conv_n = _in_proj(
            x2, norm1[layer][None, :], wtok, wfeat, kgain, gq, convw, conv_out_norm[layer][None, :])

        wtop, wbot = _expand_cmp_w1(cmp_k_w1[layer], cmp_v_w1[layer])
        posk = jnp.broadcast_to(cmp_pos_k[layer].reshape(1, -1), (8, CMP_LEN * HEAD_DIM))
        posv = jnp.broadcast_to(cmp_pos_v[layer].reshape(1, -1), (8, CMP_LEN * HEAD_DIM))
        w2k = jnp.pad(cmp_k_w2[layer], ((0, 0), (0, LANES - HEAD_DIM))).astype(BF16)
        w2vT = cmp_v_w2[layer].T.astype(BF16)
        kcgain = jnp.pad(k_norm[layer, 0], (0, LANES - HEAD_DIM))[None, :]
        kc, vcT = _compress(kcv, wtop, wbot, posk, posv, cmp_k_w1[layer], cmp_v_w1[layer], w2k, w2vT, kcgain)

        gain_b = jnp.broadcast_to(attn_out_norm[layer].reshape(N_KV, GQA, HEAD_DIM).transpose(0, 2, 1)[:, :, :, None],
                                  (N_KV, HEAD_DIM, GQA, LANES)).reshape(N_KV, HEAD_DIM, GQA * LANES)
        attn_n = _attention(qT, gT, ks, vsT, kw, vwT, kc, vcT, tables, gain_b)

        wout = w_out[layer].astype(BF16)
        g2 = norm2[layer][None, :]
        i = layer // 2
        if layer % 2 == 0:
            x2 = _outproj_ffn(x2, attn_n, conv_n, wout, g2, ffn_w_gate[i].astype(BF16), ffn_w_up[i].astype(BF16),
                              ffn_w_down[i].astype(BF16))
        else:
            rw = jnp.pad(router_w[i], ((0, 0), (0, LANES - N_EXPERTS))).astype(BF16)
            rb = jnp.pad(router_b[i], (0, LANES - N_EXPERTS))[None, :]
            x1, h2, route, counts = _outproj_router(x2, attn_n, conv_n, wout, g2, rw, rb)
            x2 = _moe(h2, x1, route, counts, moe_w_gate[i].astype(BF16), moe_w_up[i].astype(BF16),
                      moe_w_down[i].astype(BF16))
    return x2.reshape(B, S, D_MODEL)
```

```python
import functools
import math

import numpy as np
import jax
import jax.numpy as jnp
from jax import lax
from jax.experimental import pallas as pl
from jax.experimental.pallas import tpu as pltpu

F32 = jnp.float32
BF16 = jnp.bfloat16
I32 = jnp.int32

D_MODEL = 1024
HEAD_DIM = 64
N_HEADS = 8
N_KV = 2
GQA = N_HEADS // N_KV
ATTN_W = N_HEADS * HEAD_DIM
KV_W = N_KV * HEAD_DIM
N_BRANCH = 3
CONV_W = 512
CONV_K = 3
CMP_LEN = 32
CMP_STRIDE = 16
CMP_HIDDEN = 128
SEL_LEN = 64
SEL_TOPK = 16
WINDOW = 512
Q_BLOCK = 128
N_BUCKETS = 32
MAX_DISTANCE = 128
D_FF = 2816
N_EXPERTS = 8
EPS = 1e-6
NEG = -1e30
FORCE = 1e9
MASK_BIG = 2.0 ** 60
SCALE = HEAD_DIM ** -0.5
LOG2E = math.log2(math.e)

LANES = 128
V_ROWS = 80
CHUNK = 512
BLOCKS_PER_CHUNK = CHUNK // SEL_LEN
TC_LEAD = 24
TC_ROWS = 88
TC_WIN = 48
VMEM_LIMIT = 56 * 1024 * 1024


def _bucket_np(dist):
    n = np.maximum(dist, 0)
    max_exact = N_BUCKETS // 2
    nf = np.maximum(n, max_exact).astype(np.float64)
    v = np.log(nf / max_exact) / math.log(MAX_DISTANCE / max_exact) * (N_BUCKETS - max_exact)
    frac = np.abs(v - np.round(v))
    assert np.all((frac > 1e-6) | (n <= max_exact) | (n >= MAX_DISTANCE)), "bucket boundary is precision dependent"
    large = np.minimum(max_exact + (v + 1e-9).astype(np.int32), N_BUCKETS - 1)
    return np.where(n < max_exact, n, large).astype(np.int32)


def _index_tables():
    tl = np.arange(Q_BLOCK)[None, :]
    r = np.arange(WINDOW + Q_BLOCK + WINDOW)[:, None]
    d = tl + WINDOW - r
    idx_w = np.where((d >= 0) & (d < WINDOW), _bucket_np(d), -1)
    r = np.arange(3 * Q_BLOCK)[:, None]
    d = tl + Q_BLOCK - r
    idx_s = np.where(d >= 0, _bucket_np(d), -1)
    r = np.arange(TC_ROWS)[:, None] - TC_LEAD
    d = tl - CMP_STRIDE * r + (CMP_STRIDE * 16 - (CMP_LEN - 1))
    idx_c = np.where((d >= 0) & (r < 32), _bucket_np(d), -1)
    return idx_w.astype(np.int32), idx_s.astype(np.int32), idx_c.astype(np.int32)


def _tables_kernel(rb_ref, iw_ref, is_ref, ic_ref, tw_ref, ts_ref, tc_ref, b31_ref):
    for h in range(N_HEADS):
        g, r = divmod(h, GQA)
        lanes = slice(r * LANES, (r + 1) * LANES)
        for idx_ref, out_ref in ((iw_ref, tw_ref), (is_ref, ts_ref), (ic_ref, tc_ref)):
            out_ref[g, :, lanes] = jnp.full(idx_ref.shape, NEG, F32)

            def body(b, carry, idx_ref=idx_ref, out_ref=out_ref, g=g, lanes=lanes, h=h):
                out_ref[g, :, lanes] = jnp.where(idx_ref[...] == b, rb_ref[b, h] * LOG2E, out_ref[g, :, lanes])
                return carry

            lax.fori_loop(0, N_BUCKETS, body, 0)
        far_bias = rb_ref[N_BUCKETS - 1, h] * LOG2E
        tc_ref[g, :, lanes] = jnp.where(ic_ref[...] >= 0, tc_ref[g, :, lanes] - far_bias, NEG)
        b31_ref[g, :, lanes] = jnp.full((8, LANES), far_bias, F32)


def _bias_tables(rel_bias):
    idx_w, idx_s, idx_c = _index_tables()
    width = GQA * LANES
    out_shape = (
        jax.ShapeDtypeStruct((N_KV, idx_w.shape[0], width), F32),
        jax.ShapeDtypeStruct((N_KV, idx_s.shape[0], width), F32),
        jax.ShapeDtypeStruct((N_KV, idx_c.shape[0], width), F32),
        jax.ShapeDtypeStruct((N_KV, 8, width), F32),
    )
    vmem = pl.BlockSpec(memory_space=pltpu.VMEM)
    return pl.pallas_call(
        _tables_kernel,
        out_shape=out_shape,
        in_specs=[pl.BlockSpec(memory_space=pltpu.SMEM), vmem, vmem, vmem],
        out_specs=(vmem, vmem, vmem, vmem),
        name="bias_tables",
    )(rel_bias, jnp.asarray(idx_w), jnp.asarray(idx_s), jnp.asarray(idx_c))


def _in_proj_kernel(x_ref, g1_ref, wtok_ref, wfeat_ref, ind_ref, kgain_ref, aug_ref, gq_ref, convw_ref, cgain_ref,
                    qT_ref, gT_ref, vsT_ref, vwT_ref, ks_ref, kw_ref, kcv_ref, convn_ref, zs_ref):
    tm = x_ref.shape[0]
    i = pl.program_id(0)
    x = x_ref[...]
    ms = jnp.mean(x * x, axis=-1, keepdims=True)
    h = (x * lax.rsqrt(ms + EPS) * g1_ref[...]).astype(BF16)
    tok = jnp.dot(h, wtok_ref[...], preferred_element_type=F32)
    feat = lax.dot_general(wfeat_ref[...], h, (((1,), (1,)), ((), ())),
                           preferred_element_type=F32)

    kcv_ref[...] = tok[:, 0:2 * KV_W].astype(BF16)
    kk = tok[:, 2 * KV_W:4 * KV_W]
    ssq = jnp.dot(kk * kk, ind_ref[...], preferred_element_type=F32)
    kn = kk * lax.rsqrt(ssq * (1.0 / HEAD_DIM) + EPS) * kgain_ref[...]
    ksl = kn[:, 0:KV_W]
    lane = lax.broadcasted_iota(I32, (tm, LANES), 1)
    aug = aug_ref[...]
    ks_ref[0] = jnp.where(lane < HEAD_DIM, ksl, aug).astype(BF16)
    ks_ref[1] = jnp.where(lane < HEAD_DIM, pltpu.roll(ksl, HEAD_DIM, 1), aug).astype(BF16)
    kw_ref[...] = kn[:, KV_W:2 * KV_W].astype(BF16)

    c0 = 4 * KV_W
    cb = tok[:, c0:c0 + CONV_W]
    cc = tok[:, c0 + CONV_W:c0 + 2 * CONV_W]
    ch = tok[:, c0 + 2 * CONV_W:c0 + 3 * CONV_W]
    z = cc * ch

    @pl.when(i == 0)
    def _():
        zs_ref[0:8, :] = jnp.zeros((8, CONV_W), F32)

    zs_ref[8:8 + tm, :] = z
    z1 = zs_ref[7:7 + tm, :]
    z2 = zs_ref[6:6 + tm, :]
    w = convw_ref[...]
    y = w[0:1, :] * z2 + w[1:2, :] * z1 + w[2:3, :] * z
    zs_ref[0:8, :] = z[tm - 8:tm, :]
    oc = cb * y
    msc = jnp.mean(oc * oc, axis=-1, keepdims=True)
    convn_ref[...] = (oc * lax.rsqrt(msc + EPS) * cgain_ref[...]).astype(BF16)

    q = feat[0:ATTN_W].reshape(N_HEADS, HEAD_DIM, tm)
    qss = jnp.sum(q * q, axis=1, keepdims=True)
    qn = q * lax.rsqrt(qss * (1.0 / HEAD_DIM) + EPS) * gq_ref[...][None]
    qT_ref[...] = (qn * (SCALE * LOG2E)).reshape(ATTN_W, tm).astype(BF16)
    ones_rows = (lax.broadcasted_iota(I32, (V_ROWS - HEAD_DIM, tm), 0) == 0).astype(BF16)
    for g in range(N_KV):
        r0 = ATTN_W + g * HEAD_DIM
        vsT_ref[g, 0:HEAD_DIM, :] = feat[r0:r0 + HEAD_DIM].astype(BF16)
        vsT_ref[g, HEAD_DIM:V_ROWS, :] = ones_rows
        r1 = ATTN_W + KV_W + g * HEAD_DIM
        vwT_ref[g, 0:HEAD_DIM, :] = feat[r1:r1 + HEAD_DIM].astype(BF16)
        vwT_ref[g, HEAD_DIM:V_ROWS, :] = ones_rows
    g0 = ATTN_W + 2 * KV_W
    gT_ref[...] = jax.nn.sigmoid(feat[g0:g0 + N_BRANCH * N_HEADS])


_IN_TM = 512


def _in_proj(x2, g1, wtok, wfeat, kgain, gq, convw, cgain):
    S = x2.shape[0]
    tm = _IN_TM
    nt = S // tm
    ind = np.kron(np.eye(2 * N_KV, dtype=np.float32), np.ones((HEAD_DIM, HEAD_DIM), np.float32))
    aug = np.zeros((tm, LANES), np.float32)
    blk = (np.arange(tm) // SEL_LEN) % BLOCKS_PER_CHUNK
    aug[np.arange(tm), HEAD_DIM + blk] = 1.0
    aug[:, HEAD_DIM + BLOCKS_PER_CHUNK:HEAD_DIM + BLOCKS_PER_CHUNK + 2] = 1.0
    const = lambda shape: pl.BlockSpec(shape, lambda i: (0,) * len(shape))
    out_shape = (
        jax.ShapeDtypeStruct((ATTN_W, S), BF16),
        jax.ShapeDtypeStruct((N_BRANCH * N_HEADS, S), F32),
        jax.ShapeDtypeStruct((N_KV, V_ROWS, S), BF16),
        jax.ShapeDtypeStruct((N_KV, V_ROWS, S), BF16),
        jax.ShapeDtypeStruct((N_KV, S, LANES), BF16),
        jax.ShapeDtypeStruct((S, LANES), BF16),
        jax.ShapeDtypeStruct((S, 2 * KV_W), BF16),
        jax.ShapeDtypeStruct((S, CONV_W), BF16),
    )
    out_specs = (
        pl.BlockSpec((ATTN_W, tm), lambda i: (0, i)),
        pl.BlockSpec((N_BRANCH * N_HEADS, tm), lambda i: (0, i)),
        pl.BlockSpec((N_KV, V_ROWS, tm), lambda i: (0, 0, i)),
        pl.BlockSpec((N_KV, V_ROWS, tm), lambda i: (0, 0, i)),
        pl.BlockSpec((N_KV, tm, LANES), lambda i: (0, i, 0)),
        pl.BlockSpec((tm, LANES), lambda i: (i, 0)),
        pl.BlockSpec((tm, 2 * KV_W), lambda i: (i, 0)),
        pl.BlockSpec((tm, CONV_W), lambda i: (i, 0)),
    )
    in_specs = [
        pl.BlockSpec((tm, D_MODEL), lambda i: (i, 0)),
        const((1, D_MODEL)),
        const(wtok.shape),
        const(wfeat.shape),
        const(ind.shape),
        const((1, 2 * KV_W)),
        const(aug.shape),
        const((HEAD_DIM, tm)),
        const((8, CONV_W)),
        const((1, CONV_W)),
    ]
    return pl.pallas_call(
        _in_proj_kernel,
        grid=(nt,),
        in_specs=in_specs,
        out_specs=out_specs,
        out_shape=out_shape,
        scratch_shapes=[pltpu.VMEM((tm + 8, CONV_W), F32)],
        compiler_params=pltpu.CompilerParams(dimension_semantics=("arbitrary",), vmem_limit_bytes=VMEM_LIMIT),
        name="in_proj",
    )(x2, g1, wtok, wfeat, jnp.asarray(ind), kgain, jnp.asarray(aug), gq, convw, cgain)


def _compress_kernel(r_ref, wtop_ref, wbot_ref, posk_ref, posv_ref, w1k_ref, w1v_ref, w2k_ref, w2vT_ref, kgain_ref,
                     kc_ref, vcT_ref):
    nr = r_ref.shape[0]
    r = r_ref[...]
    u = jnp.dot(r, wtop_ref[...], preferred_element_type=F32)
    lo = jnp.dot(r, wbot_ref[...], preferred_element_type=F32)
    bias_k = jnp.dot(posk_ref[...], w1k_ref[...], preferred_element_type=F32)[0:1, :]
    bias_v = jnp.dot(posv_ref[...], w1v_ref[...], preferred_element_type=F32)[0:1, :]
    bias4 = jnp.concatenate([bias_k] * N_KV + [bias_v] * N_KV, axis=1)
    hid = u + pltpu.roll(lo, nr - 1, 0) + bias4
    act = jax.nn.gelu(hid)
    for g in range(N_KV):
        ak = act[:, g * CMP_HIDDEN:(g + 1) * CMP_HIDDEN].astype(BF16)
        av = act[:, (N_KV + g) * CMP_HIDDEN:(N_KV + g + 1) * CMP_HIDDEN].astype(BF16)
        kc = jnp.dot(ak, w2k_ref[...], preferred_element_type=F32)
        ssq = jnp.sum(kc * kc, axis=-1, keepdims=True)
        kc_ref[g] = (kc * lax.rsqrt(ssq * (1.0 / HEAD_DIM) + EPS) * kgain_ref[...]).astype(BF16)
        vcT_ref[g] = lax.dot_general(w2vT_ref[...], av, (((1,), (1,)), ((), ())),
                                     preferred_element_type=F32).astype(BF16)


def _compress(kcv, wtop, wbot, posk, posv, w1k, w1v, w2k, w2vT, kgain):
    S = kcv.shape[0]
    nr = S // CMP_STRIDE
    r = kcv.reshape(nr, CMP_STRIDE * 2 * KV_W)
    vmem = pl.BlockSpec(memory_space=pltpu.VMEM)
    return pl.pallas_call(
        _compress_kernel,
        out_shape=(jax.ShapeDtypeStruct((N_KV, nr, LANES), BF16),
                   jax.ShapeDtypeStruct((N_KV, HEAD_DIM, nr), BF16)),
        in_specs=[vmem] * 10,
        out_specs=(vmem, vmem),
        compiler_params=pltpu.CompilerParams(vmem_limit_bytes=VMEM_LIMIT),
        name="compress",
    )(r, wtop, wbot, posk, posv, w1k, w1v, w2k, w2vT, kgain)


def _tile4(row):
    return jnp.concatenate([row] * GQA, axis=1)


def _attn_kernel(qT_ref, gT_ref, ks_ref, vsT_ref, kw_ref, vwT_ref, kc_ref, vcT_ref,
                 tw_ref, ts_ref, tc_ref, b31_ref, gain_ref, out_ref,
                 sc_ref, sw_ref, imp_ref, aoct_ref, aall_ref, qs_ref, sa_ref, sb_ref, qz_ref, qsb_ref, fin_ref):
    ncr = kc_ref.shape[1]
    nsel = aall_ref.shape[1] * BLOCKS_PER_CHUNK
    noct = aoct_ref.shape[1] - 1
    width = GQA * LANES
    c = pl.program_id(0)
    t0 = c * Q_BLOCK
    tl = lax.broadcasted_iota(I32, (1, LANES), 1)
    cur = 2 * c + (tl >= SEL_LEN).astype(I32)
    zb = jnp.maximum(2 * c - 2, 0)
    zs = pl.multiple_of(zb * SEL_LEN, Q_BLOCK)
    ts_off = pl.multiple_of(zs - (t0 - Q_BLOCK), Q_BLOCK)
    zeros_q = jnp.zeros((HEAD_DIM, width), BF16)

    col_ok = _tile4(t0 + tl) >= CMP_LEN - 1
    j = lax.broadcasted_iota(I32, (nsel, LANES), 0)
    forced = (j == 0) | (j == cur) | (j == cur - 1)
    for g in range(N_KV):
        imp_ref[g, 0:8, :] = jnp.zeros((8, LANES), F32)
        imp_ref[g, 8 + ncr:16 + ncr, :] = jnp.zeros((8, LANES), F32)

    ws = pl.multiple_of(jnp.maximum(t0 - WINDOW, 0), Q_BLOCK)
    tw_off = pl.multiple_of(ws - (t0 - WINDOW), Q_BLOCK)
    nw = WINDOW + Q_BLOCK
    lo = pl.multiple_of(jnp.clip((8 * c - 16) // 16 * 16, 0, ncr - TC_WIN), 16)
    tc_off = pl.multiple_of(lo - (8 * c - 16) + TC_LEAD, 8)
    qgs = [jnp.concatenate([qT_ref[(GQA * g + r) * HEAD_DIM:(GQA * g + r + 1) * HEAD_DIM, :]
                            for r in range(GQA)], axis=1) for g in range(N_KV)]

    for g in range(N_KV):
        qc = jnp.concatenate([qgs[g], zeros_q], axis=0)
        s = jnp.dot(kc_ref[g], qc, preferred_element_type=F32)
        row = lax.broadcasted_iota(I32, (ncr, width), 0)
        sc_ref[g] = jnp.where(row < lo, s, NEG)
        s_loc = jnp.dot(kc_ref[g, pl.ds(lo, TC_WIN), :], qc, preferred_element_type=F32)
        sc_ref[g, pl.ds(lo, TC_WIN), :] = s_loc + tc_ref[g, pl.ds(tc_off, TC_WIN), :]

    for g in range(N_KV):
        qw = jnp.concatenate([qgs[g], zeros_q] if g == 0 else [zeros_q, qgs[g]], axis=0)
        sw_ref[g] = (jnp.dot(kw_ref[pl.ds(ws, nw), :], qw, preferred_element_type=F32)
                     + tw_ref[g, pl.ds(tw_off, nw), :])

    o_cmp, scores0 = [], []
    for g in range(N_KV):
        s = sc_ref[g]
        m = jnp.max(s, axis=0, keepdims=True)
        e = jnp.exp2(s - m)
        l = jnp.sum(e, axis=0, keepdims=True)
        p = e * jnp.where(col_ok, 1.0 / jnp.maximum(l, 1e-30), 0.0)
        o_cmp.append(jnp.dot(vcT_ref[g], p.astype(BF16), preferred_element_type=F32))
        imp_ref[g, 8:8 + ncr, :] = (
            (p[:, 0:LANES] + p[:, LANES:2 * LANES]) + p[:, 2 * LANES:3 * LANES] + p[:, 3 * LANES:4 * LANES])
        isel = jnp.zeros((nsel, LANES), F32)
        for off in (0, -1, 1, 0, 2, 1, 3, 2):
            isel = isel + imp_ref[g, pl.ds(8 + off, nsel, stride=4), :]
        score = jnp.where(forced, -jnp.inf, isel)
        scores0.append(jnp.where(j > cur, NEG, score))

    o_win = []
    for g in range(N_KV):
        sw = sw_ref[g]
        mw = jnp.max(sw, axis=0, keepdims=True)
        pw = jnp.exp2(sw - mw)
        accw = jnp.dot(vwT_ref[g, :, pl.ds(ws, nw)], pw.astype(BF16), preferred_element_type=F32)
        o_win.append(accw[0:HEAD_DIM] * (1.0 / jnp.maximum(accw[HEAD_DIM:HEAD_DIM + 1], 1e-30)))

    picks = list(scores0)
    for _ in range(SEL_TOPK - 3):
        for g in range(N_KV):
            mx = jnp.max(picks[g], axis=0, keepdims=True)
            first = jnp.min(jnp.where(picks[g] == mx, j, nsel), axis=0, keepdims=True)
            picks[g] = jnp.where(j == first, -jnp.inf, picks[g])

    m_init, acc_init, gates = [], [], []
    for g in range(N_KV):
        qg = qgs[g]
        b31row = b31_ref[g, 0:1, :]
        chosen = (picks[g] == -jnp.inf) & (j <= cur)
        a_all = _tile4(jnp.where(chosen, 0.0, -MASK_BIG)).reshape(noct, BLOCKS_PER_CHUNK, width)
        a_far = _tile4(jnp.where(chosen & (j < zb), 0.0, -MASK_BIG)).reshape(noct, BLOCKS_PER_CHUNK, width)
        b_hi = b31row.astype(BF16).astype(F32)
        erow = lax.broadcasted_iota(I32, (8, width), 0)
        extra = jnp.where(erow == 0, b_hi, jnp.where(erow == 1, b31row - b_hi, 0.0))
        aoct_ref[g, 0:noct] = jnp.concatenate(
            [a_far, jnp.broadcast_to(extra[None], (noct, 8, width))], axis=1).astype(BF16)
        aoct_ref[g, noct] = jnp.concatenate([jnp.full((8, width), -MASK_BIG, F32), extra], axis=0).astype(BF16)
        aall_ref[g] = jnp.concatenate([a_all, jnp.zeros((noct, 8, width), F32)], axis=1).astype(BF16)
        for q_ref in (qs_ref, qsb_ref):
            q_ref[g, 0:HEAD_DIM, :] = qg
            q_ref[g, HEAD_DIM + 16:LANES, :] = jnp.zeros((LANES - HEAD_DIM - 16, width), BF16)

        halves = []
        for h in range(2):
            qz_ref[2 * g + h, HEAD_DIM:HEAD_DIM + 16, :] = aall_ref[g, (zb + 2 * h) // BLOCKS_PER_CHUNK]
            qz_ref[2 * g + h, 0:HEAD_DIM, :] = qg
            qz_ref[2 * g + h, HEAD_DIM + 16:LANES, :] = jnp.zeros((LANES - HEAD_DIM - 16, width), BF16)
            sh = jnp.dot(ks_ref[g, pl.ds(zs + h * Q_BLOCK, Q_BLOCK), :], qz_ref[2 * g + h], preferred_element_type=F32)
            halves.append(sh + ts_ref[g, pl.ds(ts_off + h * Q_BLOCK, Q_BLOCK), :])
        sz = jnp.concatenate(halves, axis=0)
        mz = jnp.max(sz, axis=0, keepdims=True)
        pz = jnp.exp2(sz - mz)
        m_init.append(mz)
        acc_init.append(jnp.dot(vsT_ref[g, :, pl.ds(zs, 2 * Q_BLOCK)], pz.astype(BF16), preferred_element_type=F32))
        gates.append([jnp.concatenate([gT_ref[br * N_HEADS + GQA * g + r:br * N_HEADS + GQA * g + r + 1, :]
                                       for r in range(GQA)], axis=1) for br in range(N_BRANCH)])

    nfar = (zb + BLOCKS_PER_CHUNK - 1) // BLOCKS_PER_CHUNK

    nsub = 1
    sub = CHUNK // nsub

    def score(g, u, buf_ref, h, q_ref):
        k0 = pl.multiple_of(jnp.minimum(u, noct - 1) * CHUNK + h * sub, sub)
        s = jnp.dot(ks_ref[g, pl.ds(k0, sub), :], q_ref[g], preferred_element_type=F32)
        buf_ref[g, h * sub:(h + 1) * sub, :] = s
        return jnp.max(s, axis=0, keepdims=True)

    def set_mask_rows(g, u, q_ref):
        oct_id = jnp.where(u < nfar, u, noct)
        q_ref[g, HEAD_DIM:HEAD_DIM + 16, :] = aoct_ref[g, oct_id]

    def accumulate(g, u, buf_ref, h, mn, acc):
        k0 = pl.multiple_of(jnp.minimum(u, noct - 1) * CHUNK + h * sub, sub)
        p = jnp.exp2(buf_ref[g, h * sub:(h + 1) * sub, :] - mn)
        return acc + jnp.dot(vsT_ref[g, :, pl.ds(k0, sub)], p.astype(BF16), preferred_element_type=F32)

    def step(u, cur_ref, nxt_ref, q_ref, state, cms):
        out_state, out_cms = [], []
        for g in range(N_KV):
            m, acc = state[2 * g], state[2 * g + 1]
            mn = jnp.maximum(m, cms[g])
            acc = jnp.exp2(m - mn) * acc
            set_mask_rows(g, u + 1, q_ref)
            cm = None
            for h in range(nsub):
                ch = score(g, u + 1, nxt_ref, h, q_ref)
                cm = ch if cm is None else jnp.maximum(cm, ch)
                acc = accumulate(g, u, cur_ref, h, mn, acc)
            out_cms.append(cm)
            out_state += [mn, acc]
        return out_state, out_cms

    def far(i, carry):
        state, cm_b = step(2 * i, sa_ref, sb_ref, qsb_ref, carry[0:4], carry[4:6])
        state, cm_a = step(2 * i + 1, sb_ref, sa_ref, qs_ref, state, cm_b)
        return tuple(state + cm_a)

    first = []
    for g in range(N_KV):
        set_mask_rows(g, 0, qs_ref)
        cm = None
        for h in range(nsub):
            ch = score(g, 0, sa_ref, h, qs_ref)
            cm = ch if cm is None else jnp.maximum(cm, ch)
        first.append(cm)
    nchunks = jnp.maximum(nfar, 1)
    npairs = (nchunks - 1) // 2
    carry = lax.fori_loop(0, npairs, far, (m_init[0], acc_init[0], m_init[1], acc_init[1], first[0], first[1]))
    u_last = 2 * npairs

    def finish(two_left):
        for g in range(N_KV):
            m, acc, cm = carry[2 * g], carry[2 * g + 1], carry[4 + g]
            mn = jnp.maximum(m, cm)
            acc = jnp.exp2(m - mn) * acc
            if two_left:
                set_mask_rows(g, u_last + 1, qsb_ref)
                cm_b = score(g, u_last + 1, sb_ref, 0, qsb_ref)
            acc = accumulate(g, u_last, sa_ref, 0, mn, acc)
            if two_left:
                mn_b = jnp.maximum(mn, cm_b)
                acc = accumulate(g, u_last + 1, sb_ref, 0, mn_b, jnp.exp2(mn - mn_b) * acc)
            fin_ref[g] = acc

    pl.when(nchunks - u_last == 1)(lambda: finish(False))
    pl.when(nchunks - u_last == 2)(lambda: finish(True))
    carry = [None, fin_ref[0], None, fin_ref[1]]

    o_all, ssq = [], jnp.zeros((1, LANES), F32)
    for g in range(N_KV):
        acc = carry[2 * g + 1]
        o_sel = acc[0:HEAD_DIM] * (1.0 / jnp.maximum(acc[HEAD_DIM:HEAD_DIM + 1], 1e-30))
        o = gates[g][0] * o_cmp[g] + gates[g][1] * o_sel + gates[g][2] * o_win[g]
        o_all.append(o)
        cs = jnp.sum(o * o, axis=0, keepdims=True)
        ssq = ssq + ((cs[:, 0:LANES] + cs[:, LANES:2 * LANES]) + (cs[:, 2 * LANES:3 * LANES] + cs[:, 3 * LANES:]))
    inv = _tile4(lax.rsqrt(ssq * (1.0 / ATTN_W) + EPS))
    for g in range(N_KV):
        on = o_all[g] * inv * gain_ref[g]
        for k in range(GQA // 2):
            pair = jnp.concatenate([on[:, (2 * k) * LANES:(2 * k + 1) * LANES],
                                    on[:, (2 * k + 1) * LANES:(2 * k + 2) * LANES]], axis=0)
            col = (GQA * g + 2 * k) * HEAD_DIM
            out_ref[:, col:col + 2 * HEAD_DIM] = pair.T.astype(BF16)


def _attention(qT, gT, ks, vsT, kw, vwT, kc, vcT, tables, gain_b):
    S = qT.shape[1]
    nq = S // Q_BLOCK
    ncr = kc.shape[1]
    nsel = S // SEL_LEN
    noct = S // CHUNK
    width = GQA * LANES
    tw, ts, tc, b31 = tables
    vmem = pl.BlockSpec(memory_space=pltpu.VMEM)
    in_specs = [
        pl.BlockSpec((ATTN_W, Q_BLOCK), lambda c: (0, c)),
        pl.BlockSpec((N_BRANCH * N_HEADS, Q_BLOCK), lambda c: (0, c)),
    ] + [vmem] * 11
    return pl.pallas_call(
        _attn_kernel,
        grid=(nq,),
        in_specs=in_specs,
        out_specs=pl.BlockSpec((Q_BLOCK, ATTN_W), lambda c: (c, 0)),
        out_shape=jax.ShapeDtypeStruct((S, ATTN_W), BF16),
        scratch_shapes=[
            pltpu.VMEM((N_KV, ncr, width), F32),
            pltpu.VMEM((N_KV, WINDOW + Q_BLOCK, width), F32),
            pltpu.VMEM((N_KV, ncr + 16, LANES), F32),
            pltpu.VMEM((N_KV, noct + 1, 16, width), BF16),
            pltpu.VMEM((N_KV, noct, 16, width), BF16),
            pltpu.VMEM((N_KV, LANES, width), BF16),
            pltpu.VMEM((N_KV, CHUNK, width), F32),
            pltpu.VMEM((N_KV, CHUNK, width), F32),
            pltpu.VMEM((2 * N_KV, LANES, width), BF16),
            pltpu.VMEM((N_KV, LANES, width), BF16),
            pltpu.VMEM((N_KV, V_ROWS, width), F32),
        ],
        compiler_params=pltpu.CompilerParams(dimension_semantics=("arbitrary",), vmem_limit_bytes=VMEM_LIMIT),
        name="nsa_attention",
    )(qT, gT, ks, vsT, kw, vwT, kc, vcT, tw, ts, tc, b31, gain_b)


def _mix_and_norm(x_ref, attn_ref, conv_ref, wout_ref, g2_ref):
    x1 = x_ref[...] + jnp.dot(attn_ref[...], wout_ref[0:ATTN_W, :], preferred_element_type=F32) \
        + jnp.dot(conv_ref[...], wout_ref[ATTN_W:, :], preferred_element_type=F32)
    ms = jnp.mean(x1 * x1, axis=-1, keepdims=True)
    h2 = x1 * lax.rsqrt(ms + EPS) * g2_ref[...]
    return x1, h2


def _ffn_kernel(x_ref, attn_ref, conv_ref, wout_ref, g2_ref, wg_ref, wu_ref, wd_ref, out_ref):
    x1, h2 = _mix_and_norm(x_ref, attn_ref, conv_ref, wout_ref, g2_ref)
    h2 = h2.astype(BF16)
    a = jnp.dot(h2, wg_ref[...], preferred_element_type=F32)
    u = jnp.dot(h2, wu_ref[...], preferred_element_type=F32)
    y = (a * jax.nn.sigmoid(a) * u).astype(BF16)
    out_ref[...] = x1 + jnp.dot(y, wd_ref[...], preferred_element_type=F32)


_FFN_TM = 512


def _outproj_ffn(x2, attn_n, conv_n, wout, g2, wg, wu, wd):
    S = x2.shape[0]
    tm = _FFN_TM
    resident = pl.BlockSpec(memory_space=pltpu.VMEM)
    return pl.pallas_call(
        _ffn_kernel,
        grid=(S // tm,),
        in_specs=[
            pl.BlockSpec((tm, D_MODEL), lambda i: (i, 0)),
            pl.BlockSpec((tm, ATTN_W), lambda i: (i, 0)),
            pl.BlockSpec((tm, CONV_W), lambda i: (i, 0)),
            resident, resident, resident, resident, resident,
        ],
        out_specs=pl.BlockSpec((tm, D_MODEL), lambda i: (i, 0)),
        out_shape=jax.ShapeDtypeStruct((S, D_MODEL), F32),
        compiler_params=pltpu.CompilerParams(dimension_semantics=("arbitrary",), vmem_limit_bytes=VMEM_LIMIT),
        name="outproj_ffn",
    )(x2, attn_n, conv_n, wout, g2, wg, wu, wd)


TOKEN_TILE = (D_MODEL // LANES, LANES)


def _to_token_tiles(ref, rows):
    x = jnp.stack([rows[:, k * LANES:(k + 1) * LANES] for k in range(TOKEN_TILE[0])], axis=0)
    ref[...] = pltpu.einshape('ktl->tkl', x)


def _from_token_tiles(ref):
    x = pltpu.einshape('tkl->ktl', ref[...])
    return jnp.concatenate([x[k] for k in range(TOKEN_TILE[0])], axis=1)


def _router_kernel(x_ref, attn_ref, conv_ref, wout_ref, g2_ref, rw_ref, rb_ref, tri_ref,
                   x1_ref, h2_ref, route_ref, cnt_ref, run_ref):
    tm = x_ref.shape[0]
    i = pl.program_id(0)

    @pl.when(i == 0)
    def _():
        run_ref[...] = jnp.zeros(run_ref.shape, F32)

    x1, h2 = _mix_and_norm(x_ref, attn_ref, conv_ref, wout_ref, g2_ref)
    x1_ref[...] = x1
    h2b = h2.astype(BF16)
    _to_token_tiles(h2_ref, h2b.astype(F32))
    lane = lax.broadcasted_iota(I32, (tm, LANES), 1)
    logits = jnp.dot(h2b, rw_ref[...], preferred_element_type=F32) + rb_ref[...]
    logits = jnp.where(lane < N_EXPERTS, logits, -jnp.inf)
    m1 = jnp.max(logits, axis=-1, keepdims=True)
    i1 = jnp.min(jnp.where(logits == m1, lane, LANES), axis=-1, keepdims=True)
    rest = jnp.where(lane == i1, -jnp.inf, logits)
    m2 = jnp.max(rest, axis=-1, keepdims=True)
    i2 = jnp.min(jnp.where(rest == m2, lane, LANES), axis=-1, keepdims=True)
    e2 = jnp.exp(m2 - m1)
    den = 1.0 + e2
    oh1 = (lane == i1).astype(F32)
    oh2 = (lane == i2).astype(F32)
    both = oh1 + oh2
    before = run_ref[0:1, :] + jnp.dot(tri_ref[...], both.astype(BF16), preferred_element_type=F32)
    rank1 = jnp.sum(before * oh1, axis=-1, keepdims=True)
    rank2 = jnp.sum(before * oh2, axis=-1, keepdims=True)
    fields = (i1.astype(F32), i2.astype(F32), rank1, rank2, 1.0 / den, e2 / den)
    route = jnp.zeros((tm, LANES), F32)
    for k, v in enumerate(fields):
        route = jnp.where(lane == k, v, route)
    route_ref[...] = route
    run_ref[...] = run_ref[...] + jnp.sum(both, axis=0, keepdims=True)
    cnt_ref[...] = run_ref[...]


def _outproj_router(x2, attn_n, conv_n, wout, g2, rw, rb):
    S = x2.shape[0]
    tm = _FFN_TM
    tri = np.tril(np.ones((tm, tm), np.float32), -1)
    const = lambda shape: pl.BlockSpec(shape, lambda i: (0,) * len(shape))
    return pl.pallas_call(
        _router_kernel,
        grid=(S // tm,),
        in_specs=[
            pl.BlockSpec((tm, D_MODEL), lambda i: (i, 0)),
            pl.BlockSpec((tm, ATTN_W), lambda i: (i, 0)),
            pl.BlockSpec((tm, CONV_W), lambda i: (i, 0)),
            const(wout.shape), const((1, D_MODEL)), const(rw.shape), const((1, LANES)), const((tm, tm)),
        ],
        out_specs=(pl.BlockSpec((tm, D_MODEL), lambda i: (i, 0)),
                   pl.BlockSpec((tm,) + TOKEN_TILE, lambda i: (i, 0, 0)),
                   pl.BlockSpec((tm, LANES), lambda i: (i, 0)),
                   const((8, LANES))),
        out_shape=(jax.ShapeDtypeStruct((S, D_MODEL), F32),
                   jax.ShapeDtypeStruct((S,) + TOKEN_TILE, F32),
                   jax.ShapeDtypeStruct((S, LANES), F32),
                   jax.ShapeDtypeStruct((8, LANES), F32)),
        scratch_shapes=[pltpu.VMEM((8, LANES), F32)],
        compiler_params=pltpu.CompilerParams(dimension_semantics=("arbitrary",), vmem_limit_bytes=VMEM_LIMIT),
        name="outproj_router",
    )(x2, attn_n, conv_n, wout, g2, rw, rb, jnp.asarray(tri, BF16))


_ROW_TM = 256
_EXP_TM = 512
_DMA_UNROLL = 8


def _row_copy(src_ref, src_row, dst_ref, dst_row, sem):
    return pltpu.make_async_copy(src_ref.at[src_row], dst_ref.at[dst_row], sem)


def _dispatch_kernel(pos_ref, h_ref, zero_ref, xs_ref, hbuf_ref, sem):
    del zero_ref
    tm = h_ref.shape[0]
    i = pl.program_id(0)
    slot = i % 2
    hbuf_ref[slot] = h_ref[...]

    def issue(t, carry):
        for k in range(2):
            _row_copy(hbuf_ref.at[slot], t, xs_ref, pos_ref[0, 0, 2 * t + k], sem.at[slot]).start(priority=k)
        return carry

    lax.fori_loop(0, tm, issue, 0, unroll=_DMA_UNROLL)

    def drain(which):
        def body(t, carry):
            for k in range(2):
                _row_copy(hbuf_ref.at[which], 0, xs_ref, 0, sem.at[which]).wait()
            return carry
        lax.fori_loop(0, tm, body, 0, unroll=_DMA_UNROLL)

    @pl.when(i > 0)
    def _():
        drain(1 - slot)

    @pl.when(i == pl.num_programs(0) - 1)
    def _():
        drain(slot)


def _dispatch(pos3, h2, n_rows):
    S = h2.shape[0]
    tm = _ROW_TM
    zeros = jnp.zeros((n_rows,) + TOKEN_TILE, F32)
    return pl.pallas_call(
        _dispatch_kernel,
        grid=(S // tm,),
        in_specs=[
            pl.BlockSpec((1, 1, 2 * tm), lambda i: (i, 0, 0), memory_space=pltpu.SMEM),
            pl.BlockSpec((tm,) + TOKEN_TILE, lambda i: (i, 0, 0)),
            pl.BlockSpec(memory_space=pl.ANY),
        ],
        out_specs=pl.BlockSpec(memory_space=pl.ANY),
        out_shape=jax.ShapeDtypeStruct((n_rows,) + TOKEN_TILE, F32),
        scratch_shapes=[pltpu.VMEM((2, tm) + TOKEN_TILE, F32), pltpu.SemaphoreType.DMA((2,))],
        input_output_aliases={2: 0},
        compiler_params=pltpu.CompilerParams(dimension_semantics=("arbitrary",), vmem_limit_bytes=VMEM_LIMIT),
        name="moe_dispatch",
    )(pos3, h2, zeros)


def _experts_kernel(te_ref, tb_ref, nt_ref, xs_ref, wg_ref, wu_ref, wd_ref, ys_ref):
    i = pl.program_id(0)

    @pl.when(i < nt_ref[0])
    def _():
        x = _from_token_tiles(xs_ref).astype(BF16)
        a = jnp.dot(x, wg_ref[0], preferred_element_type=F32)
        u = jnp.dot(x, wu_ref[0], preferred_element_type=F32)
        y = (a * jax.nn.sigmoid(a) * u).astype(BF16)
        _to_token_tiles(ys_ref, jnp.dot(y, wd_ref[0], preferred_element_type=F32))

    @pl.when(i >= nt_ref[0])
    def _():
        ys_ref[...] = jnp.zeros(ys_ref.shape, F32)


def _experts(tile_e, tile_b, n_tiles, xs, wg, wu, wd):
    n_rows = xs.shape[0]
    tm = _EXP_TM
    weights = lambda shape: pl.BlockSpec(shape, lambda i, te, tb, nt: (te[i], 0, 0), pipeline_mode=pl.Buffered(1))
    grid_spec = pltpu.PrefetchScalarGridSpec(
        num_scalar_prefetch=3,
        grid=(n_rows // tm,),
        in_specs=[
            pl.BlockSpec((tm,) + TOKEN_TILE, lambda i, te, tb, nt: (tb[i], 0, 0)),
            weights((1, D_MODEL, D_FF)), weights((1, D_MODEL, D_FF)), weights((1, D_FF, D_MODEL)),
        ],
        out_specs=pl.BlockSpec((tm,) + TOKEN_TILE, lambda i, te, tb, nt: (tb[i], 0, 0)),
    )
    return pl.pallas_call(
        _experts_kernel,
        grid_spec=grid_spec,
        out_shape=jax.ShapeDtypeStruct((n_rows,) + TOKEN_TILE, F32),
        compiler_params=pltpu.CompilerParams(dimension_semantics=("arbitrary",), vmem_limit_bytes=VMEM_LIMIT),
        name="moe_experts",
    )(tile_e, tile_b, n_tiles, xs, wg, wu, wd)


def _combine_kernel(pos_ref, nxt_ref, x1_ref, route_ref, ys_ref, out_ref, y1_ref, y2_ref, sem):
    tm = x1_ref.shape[0]
    i = pl.program_id(0)
    slot = i % 2

    def issue(src_pos_ref, which):
        def body(t, carry):
            _row_copy(ys_ref, src_pos_ref[0, 0, 2 * t], y1_ref.at[which], t, sem.at[which]).start(priority=0)
            _row_copy(ys_ref, src_pos_ref[0, 0, 2 * t + 1], y2_ref.at[which], t, sem.at[which]).start(priority=1)
            return carry
        lax.fori_loop(0, tm, body, 0, unroll=_DMA_UNROLL)

    @pl.when(i == 0)
    def _():
        issue(pos_ref, slot)

    @pl.when(i + 1 < pl.num_programs(0))
    def _():
        issue(nxt_ref, 1 - slot)

    def drain(t, carry):
        _row_copy(ys_ref, 0, y1_ref.at[slot], 0, sem.at[slot]).wait()
        _row_copy(ys_ref, 0, y2_ref.at[slot], 0, sem.at[slot]).wait()
        return carry

    lax.fori_loop(0, tm, drain, 0, unroll=_DMA_UNROLL)
    lane = lax.broadcasted_iota(I32, (tm, LANES), 1)
    route = route_ref[...]
    w1 = jnp.sum(jnp.where(lane == 4, route, 0.0), axis=-1, keepdims=True)
    w2 = jnp.sum(jnp.where(lane == 5, route, 0.0), axis=-1, keepdims=True)
    out_ref[...] = x1_ref[...] + (_from_token_tiles(y1_ref.at[slot]) * w1 + _from_token_tiles(y2_ref.at[slot]) * w2)


def _combine(pos3, x1, route, ys):
    S = x1.shape[0]
    tm = _ROW_TM
    last = S // tm - 1
    return pl.pallas_call(
        _combine_kernel,
        grid=(S // tm,),
        in_specs=[
            pl.BlockSpec((1, 1, 2 * tm), lambda i: (i, 0, 0), memory_space=pltpu.SMEM),
            pl.BlockSpec((1, 1, 2 * tm), lambda i: (jnp.minimum(i + 1, last), 0, 0), memory_space=pltpu.SMEM),
            pl.BlockSpec((tm, D_MODEL), lambda i: (i, 0)),
            pl.BlockSpec((tm, LANES), lambda i: (i, 0)),
            pl.BlockSpec(memory_space=pl.ANY),
        ],
        out_specs=pl.BlockSpec((tm, D_MODEL), lambda i: (i, 0)),
        out_shape=jax.ShapeDtypeStruct((S, D_MODEL), F32),
        scratch_shapes=[pltpu.VMEM((2, tm) + TOKEN_TILE, F32), pltpu.VMEM((2, tm) + TOKEN_TILE, F32),
                        pltpu.SemaphoreType.DMA((2,))],
        compiler_params=pltpu.CompilerParams(dimension_semantics=("arbitrary",), vmem_limit_bytes=VMEM_LIMIT),
        name="moe_combine",
    )(pos3, pos3, x1, route, ys)


def _moe(h2, x1, route, counts, wg, wu, wd):
    S = h2.shape[0]
    tm = _EXP_TM
    n_tiles_max = 2 * S // tm + N_EXPERTS
    cnt = counts[0, 0:N_EXPERTS].astype(I32)
    tiles = (cnt + tm - 1) // tm
    first = jnp.cumsum(tiles) - tiles
    n_tiles = jnp.sum(tiles)
    idx = jnp.arange(n_tiles_max, dtype=I32)
    last = jnp.minimum(idx, n_tiles - 1)
    tile_e = (jnp.sum(last[:, None] >= first[None, :], axis=1) - 1).astype(I32)
    eid = route[:, 0:2].astype(I32)
    pos = first[eid] * tm + route[:, 2:4].astype(I32)
    pos3 = pos.reshape(S // _ROW_TM, 1, 2 * _ROW_TM)
    xs = _dispatch(pos3, h2, n_tiles_max * tm)
    ys = _experts(tile_e, idx, n_tiles.reshape(1), xs, wg, wu, wd)
    return _combine(pos3, x1, route, ys)


def _split_w_in(w):
    o = np.cumsum([0, ATTN_W] + [KV_W] * 6 + [N_BRANCH * N_HEADS] + [CONV_W] * 3)
    q, kc, vc, ksl, vsl, kwn, vwn, gts, cb, cc, ch = (w[:, o[i]:o[i + 1]] for i in range(11))
    perm = np.array([h * N_BRANCH + br for br in range(N_BRANCH) for h in range(N_HEADS)])
    wtok = jnp.concatenate([kc, vc, ksl, kwn, cb, cc, ch], axis=1).astype(BF16)
    feat = jnp.concatenate([q, vsl, vwn, gts[:, perm], jnp.zeros((D_MODEL, 8), w.dtype)], axis=1)
    return wtok, feat.T.astype(BF16)


def _expand_cmp_w1(w1k, w1v):
    kinds = jnp.stack([w1k, w1v]).reshape(2, CMP_LEN, HEAD_DIM, CMP_HIDDEN)
    per_col = jnp.repeat(kinds, N_KV, axis=0)
    eye = jnp.eye(2 * N_KV, dtype=F32)
    out = []
    for l0 in (0, CMP_STRIDE):
        w = per_col[:, l0:l0 + CMP_STRIDE].transpose(1, 0, 2, 3)
        blk = w[:, :, :, None, :] * eye[None, :, None, :, None]
        out.append(blk.reshape(CMP_STRIDE * 2 * KV_W, 2 * N_KV * CMP_HIDDEN).astype(BF16))
    return out


def kernel(x, rel_bias, norm1, w_in, q_norm, k_norm, cmp_pos_k, cmp_pos_v, cmp_k_w1, cmp_k_w2, cmp_v_w1, cmp_v_w2,
           conv_w, attn_out_norm, conv_out_norm, w_out, norm2, ffn_w_gate, ffn_w_up, ffn_w_down, router_w, router_b,
           moe_w_gate, moe_w_up, moe_w_down):
    B, S, _ = x.shape
    assert B == 1 and S % CHUNK == 0 and S >= WINDOW + Q_BLOCK
    depth = norm1.shape[0]
    x2 = x.reshape(S, D_MODEL)
    tables = _bias_tables(rel_bias)
    for layer in range(depth):
        wtok, wfeat = _split_w_in(w_in[layer])
        kgain = jnp.concatenate([jnp.tile(k_norm[layer, 1], N_KV), jnp.tile(k_norm[layer, 2], N_KV)])[None, :]
        gq = jnp.broadcast_to(q_norm[layer][:, None], (HEAD_DIM, _IN_TM))
        convw = jnp.pad(conv_w[layer], ((0, 8 - CONV_K), (0, 0)))
        qT, gT, vsT, vwT, ks, kw, kcv, conv_n = _in_proj(
            x2, norm1[layer][None, :], wtok, wfeat, kgain, gq, convw, conv_out_norm[layer][None, :])

        wtop, wbot = _expand_cmp_w1(cmp_k_w1[layer], cmp_v_w1[layer])
        posk = jnp.broadcast_to(cmp_pos_k[layer].reshape(1, -1), (8, CMP_LEN * HEAD_DIM))
        posv = jnp.broadcast_to(cmp_pos_v[layer].reshape(1, -1), (8, CMP_LEN * HEAD_DIM))
        w2k = jnp.pad(cmp_k_w2[layer], ((0, 0), (0, LANES - HEAD_DIM))).astype(BF16)
        w2vT = cmp_v_w2[layer].T.astype(BF16)
        kcgain = jnp.pad(k_norm[layer, 0], (0, LANES - HEAD_DIM))[None, :]
        kc, vcT = _compress(kcv, wtop, wbot, posk, posv, cmp_k_w1[layer], cmp_v_w1[layer], w2k, w2vT, kcgain)

        gain_b = jnp.broadcast_to(attn_out_norm[layer].reshape(N_KV, GQA, HEAD_DIM).transpose(0, 2, 1)[:, :, :, None],
                                  (N_KV, HEAD_DIM, GQA, LANES)).reshape(N_KV, HEAD_DIM, GQA * LANES)
        attn_n = _attention(qT, gT, ks, vsT, kw, vwT, kc, vcT, tables, gain_b)

        wout = w_out[layer].astype(BF16)
        g2 = norm2[layer][None, :]
        i = layer // 2
        if layer % 2 == 0:
            x2 = _outproj_ffn(x2, attn_n, conv_n, wout, g2, ffn_w_gate[i].astype(BF16), ffn_w_up[i].astype(BF16),
                              ffn_w_down[i].astype(BF16))
        else:
            rw = jnp.pad(router_w[i], ((0, 0), (0, LANES - N_EXPERTS))).astype(BF16)
            rb = jnp.pad(router_b[i], (0, LANES - N_EXPERTS))[None, :]
            x1, h2, route, counts = _outproj_router(x2, attn_n, conv_n, wout, g2, rw, rb)
            x2 = _moe(h2, x1, route, counts, moe_w_gate[i].astype(BF16), moe_w_up[i].astype(BF16),
                      moe_w_down[i].astype(BF16))
    return x2.reshape(B, S, D_MODEL)
```

```python
import functools
import math

import numpy as np
import jax
import jax.numpy as jnp
from jax import lax
from jax.experimental import pallas as pl
from jax.experimental.pallas import tpu as pltpu

F32 = jnp.float32
BF16 = jnp.bfloat16
I32 = jnp.int32

D_MODEL = 1024
HEAD_DIM = 64
N_HEADS = 8
N_KV = 2
GQA = N_HEADS // N_KV
ATTN_W = N_HEADS * HEAD_DIM
KV_W = N_KV * HEAD_DIM
N_BRANCH = 3
CONV_W = 512
CONV_K = 3
CMP_LEN = 32
CMP_STRIDE = 16
CMP_HIDDEN = 128
SEL_LEN = 64
SEL_TOPK = 16
WINDOW = 512
Q_BLOCK = 128
N_BUCKETS = 32
MAX_DISTANCE = 128
D_FF = 2816
N_EXPERTS = 8
EPS = 1e-6
NEG = -1e30
FORCE = 1e9
MASK_BIG = 2.0 ** 60
SCALE = HEAD_DIM ** -0.5
LOG2E = math.log2(math.e)

LANES = 128
V_ROWS = 80
CHUNK = 512
BLOCKS_PER_CHUNK = CHUNK // SEL_LEN
TC_LEAD = 24
TC_ROWS = 88
TC_WIN = 48
CMP_STEP = 256
VMEM_LIMIT = 56 * 1024 * 1024


def _bucket_np(dist):
    n = np.maximum(dist, 0)
    max_exact = N_BUCKETS // 2
    nf = np.maximum(n, max_exact).astype(np.float64)
    v = np.log(nf / max_exact) / math.log(MAX_DISTANCE / max_exact) * (N_BUCKETS - max_exact)
    frac = np.abs(v - np.round(v))
    assert np.all((frac > 1e-6) | (n <= max_exact) | (n >= MAX_DISTANCE)), "bucket boundary is precision dependent"
    large = np.minimum(max_exact + (v + 1e-9).astype(np.int32), N_BUCKETS - 1)
    return np.where(n < max_exact, n, large).astype(np.int32)


def _index_tables():
    tl = np.arange(Q_BLOCK)[None, :]
    r = np.arange(WINDOW + Q_BLOCK + WINDOW)[:, None]
    d = tl + WINDOW - r
    idx_w = np.where((d >= 0) & (d < WINDOW), _bucket_np(d), -1)
    r = np.arange(3 * Q_BLOCK)[:, None]
    d = tl + Q_BLOCK - r
    idx_s = np.where(d >= 0, _bucket_np(d), -1)
    r = np.arange(TC_ROWS)[:, None] - TC_LEAD
    d = tl - CMP_STRIDE * r + (CMP_STRIDE * 16 - (CMP_LEN - 1))
    idx_c = np.where((d >= 0) & (r < 32), _bucket_np(d), -1)
    return idx_w.astype(np.int32), idx_s.astype(np.int32), idx_c.astype(np.int32)


def _tables_kernel(rb_ref, iw_ref, is_ref, ic_ref, tw_ref, ts_ref, tc_ref, b31_ref):
    for h in range(N_HEADS):
        g, r = divmod(h, GQA)
        lanes = slice(r * LANES, (r + 1) * LANES)
        for idx_ref, out_ref in ((iw_ref, tw_ref), (is_ref, ts_ref), (ic_ref, tc_ref)):
            out_ref[g, :, lanes] = jnp.full(idx_ref.shape, NEG, F32)

            def body(b, carry, idx_ref=idx_ref, out_ref=out_ref, g=g, lanes=lanes, h=h):
                out_ref[g, :, lanes] = jnp.where(idx_ref[...] == b, rb_ref[b, h] * LOG2E, out_ref[g, :, lanes])
                return carry

            lax.fori_loop(0, N_BUCKETS, body, 0)
        far_bias = rb_ref[N_BUCKETS - 1, h] * LOG2E
        tc_ref[g, :, lanes] = jnp.where(ic_ref[...] >= 0, tc_ref[g, :, lanes] - far_bias, NEG)
        b31_ref[g, :, lanes] = jnp.full((8, LANES), far_bias, F32)


def _bias_tables(rel_bias):
    idx_w, idx_s, idx_c = _index_tables()
    width = GQA * LANES
    out_shape = (
        jax.ShapeDtypeStruct((N_KV, idx_w.shape[0], width), F32),
        jax.ShapeDtypeStruct((N_KV, idx_s.shape[0], width), F32),
        jax.ShapeDtypeStruct((N_KV, idx_c.shape[0], width), F32),
        jax.ShapeDtypeStruct((N_KV, 8, width), F32),
    )
    vmem = pl.BlockSpec(memory_space=pltpu.VMEM)
    return pl.pallas_call(
        _tables_kernel,
        out_shape=out_shape,
        in_specs=[pl.BlockSpec(memory_space=pltpu.SMEM), vmem, vmem, vmem],
        out_specs=(vmem, vmem, vmem, vmem),
        name="bias_tables",
    )(rel_bias, jnp.asarray(idx_w), jnp.asarray(idx_s), jnp.asarray(idx_c))


def _in_proj_kernel(x_ref, g1_ref, wtok_ref, wfeat_ref, ind_ref, kgain_ref, aug_ref, gq_ref, convw_ref, cgain_ref,
                    qT_ref, gT_ref, vsT_ref, vwT_ref, ks_ref, kw_ref, kcv_ref, convn_ref, zs_ref):
    tm = x_ref.shape[0]
    i = pl.program_id(0)
    x = x_ref[...]
    ms = jnp.mean(x * x, axis=-1, keepdims=True)
    h = (x * lax.rsqrt(ms + EPS) * g1_ref[...]).astype(BF16)
    tok = jnp.dot(h, wtok_ref[...], preferred_element_type=F32)
    feat = lax.dot_general(wfeat_ref[...], h, (((1,), (1,)), ((), ())),
                           preferred_element_type=F32)

    kcv_ref[...] = tok[:, 0:2 * KV_W].astype(BF16)
    kk = tok[:, 2 * KV_W:4 * KV_W]
    ssq = jnp.dot(kk * kk, ind_ref[...], preferred_element_type=F32)
    kn = kk * lax.rsqrt(ssq * (1.0 / HEAD_DIM) + EPS) * kgain_ref[...]
    ksl = kn[:, 0:KV_W]
    lane = lax.broadcasted_iota(I32, (tm, LANES), 1)
    aug = aug_ref[...]
    ks_ref[0] = jnp.where(lane < HEAD_DIM, ksl, aug).astype(BF16)
    ks_ref[1] = jnp.where(lane < HEAD_DIM, pltpu.roll(ksl, HEAD_DIM, 1), aug).astype(BF16)
    kw_ref[...] = kn[:, KV_W:2 * KV_W].astype(BF16)

    c0 = 4 * KV_W
    cb = tok[:, c0:c0 + CONV_W]
    cc = tok[:, c0 + CONV_W:c0 + 2 * CONV_W]
    ch = tok[:, c0 + 2 * CONV_W:c0 + 3 * CONV_W]
    z = cc * ch

    @pl.when(i == 0)
    def _():
        zs_ref[0:8, :] = jnp.zeros((8, CONV_W), F32)

    zs_ref[8:8 + tm, :] = z
    z1 = zs_ref[7:7 + tm, :]
    z2 = zs_ref[6:6 + tm, :]
    w = convw_ref[...]
    y = w[0:1, :] * z2 + w[1:2, :] * z1 + w[2:3, :] * z
    zs_ref[0:8, :] = z[tm - 8:tm, :]
    oc = cb * y
    msc = jnp.mean(oc * oc, axis=-1, keepdims=True)
    convn_ref[...] = (oc * lax.rsqrt(msc + EPS) * cgain_ref[...]).astype(BF16)

    q = feat[0:ATTN_W].reshape(N_HEADS, HEAD_DIM, tm)
    qss = jnp.sum(q * q, axis=1, keepdims=True)
    qn = q * lax.rsqrt(qss * (1.0 / HEAD_DIM) + EPS) * gq_ref[...][None]
    qT_ref[...] = (qn * (SCALE * LOG2E)).reshape(ATTN_W, tm).astype(BF16)
    ones_rows = (lax.broadcasted_iota(I32, (V_ROWS - HEAD_DIM, tm), 0) == 0).astype(BF16)
    for g in range(N_KV):
        r0 = ATTN_W + g * HEAD_DIM
        vsT_ref[g, 0:HEAD_DIM, :] = feat[r0:r0 + HEAD_DIM].astype(BF16)
        vsT_ref[g, HEAD_DIM:V_ROWS, :] = ones_rows
        r1 = ATTN_W + KV_W + g * HEAD_DIM
        vwT_ref[g, 0:HEAD_DIM, :] = feat[r1:r1 + HEAD_DIM].astype(BF16)
        vwT_ref[g, HEAD_DIM:V_ROWS, :] = ones_rows
    g0 = ATTN_W + 2 * KV_W
    gT_ref[...] = jax.nn.sigmoid(feat[g0:g0 + N_BRANCH * N_HEADS])


_IN_TM = 512


def _in_proj(x2, g1, wtok, wfeat, kgain, gq, convw, cgain):
    S = x2.shape[0]
    tm = _IN_TM
    nt = S // tm
    ind = np.kron(np.eye(2 * N_KV, dtype=np.float32), np.ones((HEAD_DIM, HEAD_DIM), np.float32))
    aug = np.zeros((tm, LANES), np.float32)
    blk = (np.arange(tm) // SEL_LEN) % BLOCKS_PER_CHUNK
    aug[np.arange(tm), HEAD_DIM + blk] = 1.0
    aug[:, HEAD_DIM + BLOCKS_PER_CHUNK:HEAD_DIM + BLOCKS_PER_CHUNK + 2] = 1.0
    const = lambda shape: pl.BlockSpec(shape, lambda i: (0,) * len(shape))
    out_shape = (
        jax.ShapeDtypeStruct((ATTN_W, S), BF16),
        jax.ShapeDtypeStruct((N_BRANCH * N_HEADS, S), F32),
        jax.ShapeDtypeStruct((N_KV, V_ROWS, S), BF16),
        jax.ShapeDtypeStruct((N_KV, V_ROWS, S), BF16),
        jax.ShapeDtypeStruct((N_KV, S, LANES), BF16),
        jax.ShapeDtypeStruct((S, LANES), BF16),
        jax.ShapeDtypeStruct((S, 2 * KV_W), BF16),
        jax.ShapeDtypeStruct((S, CONV_W), BF16),
    )
    out_specs = (
        pl.BlockSpec((ATTN_W, tm), lambda i: (0, i)),
        pl.BlockSpec((N_BRANCH * N_HEADS, tm), lambda i: (0, i)),
        pl.BlockSpec((N_KV, V_ROWS, tm), lambda i: (0, 0, i)),
        pl.BlockSpec((N_KV, V_ROWS, tm), lambda i: (0, 0, i)),
        pl.BlockSpec((N_KV, tm, LANES), lambda i: (0, i, 0)),
        pl.BlockSpec((tm, LANES), lambda i: (i, 0)),
        pl.BlockSpec((tm, 2 * KV_W), lambda i: (i, 0)),
        pl.BlockSpec((tm, CONV_W), lambda i: (i, 0)),
    )
    in_specs = [
        pl.BlockSpec((tm, D_MODEL), lambda i: (i, 0)),
        const((1, D_MODEL)),
        const(wtok.shape),
        const(wfeat.shape),
        const(ind.shape),
        const((1, 2 * KV_W)),
        const(aug.shape),
        const((HEAD_DIM, tm)),
        const((8, CONV_W)),
        const((1, CONV_W)),
    ]
    return pl.pallas_call(
        _in_proj_kernel,
        grid=(nt,),
        in_specs=in_specs,
        out_specs=out_specs,
        out_shape=out_shape,
        scratch_shapes=[pltpu.VMEM((tm + 8, CONV_W), F32)],
        compiler_params=pltpu.CompilerParams(dimension_semantics=("arbitrary",), vmem_limit_bytes=VMEM_LIMIT),
        name="in_proj",
    )(x2, g1, wtok, wfeat, jnp.asarray(ind), kgain, jnp.asarray(aug), gq, convw, cgain)


def _compress_kernel(r_ref, wtop_ref, wbot_ref, posk_ref, posv_ref, w1k_ref, w1v_ref, w2k_ref, w2vT_ref, kgain_ref,
                     kc_ref, vcT_ref):
    nr = r_ref.shape[0]
    r = r_ref[...]
    u = jnp.dot(r, wtop_ref[...], preferred_element_type=F32)
    lo = jnp.dot(r, wbot_ref[...], preferred_element_type=F32)
    bias_k = jnp.dot(posk_ref[...], w1k_ref[...], preferred_element_type=F32)[0:1, :]
    bias_v = jnp.dot(posv_ref[...], w1v_ref[...], preferred_element_type=F32)[0:1, :]
    bias4 = jnp.concatenate([bias_k] * N_KV + [bias_v] * N_KV, axis=1)
    hid = u + pltpu.roll(lo, nr - 1, 0) + bias4
    act = jax.nn.gelu(hid)
    for g in range(N_KV):
        ak = act[:, g * CMP_HIDDEN:(g + 1) * CMP_HIDDEN].astype(BF16)
        av = act[:, (N_KV + g) * CMP_HIDDEN:(N_KV + g + 1) * CMP_HIDDEN].astype(BF16)
        kc = jnp.dot(ak, w2k_ref[...], preferred_element_type=F32)
        ssq = jnp.sum(kc * kc, axis=-1, keepdims=True)
        kc_ref[g] = (kc * lax.rsqrt(ssq * (1.0 / HEAD_DIM) + EPS) * kgain_ref[...]).astype(BF16)
        vcT_ref[g] = lax.dot_general(w2vT_ref[...], av, (((1,), (1,)), ((), ())),
                                     preferred_element_type=F32).astype(BF16)


def _compress(kcv, wtop, wbot, posk, posv, w1k, w1v, w2k, w2vT, kgain):
    S = kcv.shape[0]
    nr = S // CMP_STRIDE
    r = kcv.reshape(nr, CMP_STRIDE * 2 * KV_W)
    vmem = pl.BlockSpec(memory_space=pltpu.VMEM)
    return pl.pallas_call(
        _compress_kernel,
        out_shape=(jax.ShapeDtypeStruct((N_KV, nr, LANES), BF16),
                   jax.ShapeDtypeStruct((N_KV, HEAD_DIM, nr), BF16)),
        in_specs=[vmem] * 10,
        out_specs=(vmem, vmem),
        compiler_params=pltpu.CompilerParams(vmem_limit_bytes=VMEM_LIMIT),
        name="compress",
    )(r, wtop, wbot, posk, posv, w1k, w1v, w2k, w2vT, kgain)


def _tile4(row):
    return jnp.concatenate([row] * GQA, axis=1)


def _attn_kernel(qT_ref, gT_ref, ks_ref, vsT_ref, kw_ref, vwT_ref, kc_ref, vcT_ref,
                 tw_ref, ts_ref, tc_ref, b31_ref, gain_ref, out_ref,
                 sc_ref, sw_ref, imp_ref, aoct_ref, aall_ref, qs_ref, sa_ref, sb_ref, qz_ref, qsb_ref, fin_ref, ocmp_ref):
    ncr = kc_ref.shape[1]
    nsel = aall_ref.shape[1] * BLOCKS_PER_CHUNK
    noct = aoct_ref.shape[1] - 1
    width = GQA * LANES
    c = pl.program_id(0)
    t0 = c * Q_BLOCK
    tl = lax.broadcasted_iota(I32, (1, LANES), 1)
    cur = 2 * c + (tl >= SEL_LEN).astype(I32)
    zb = jnp.maximum(2 * c - 2, 0)
    zs = pl.multiple_of(zb * SEL_LEN, Q_BLOCK)
    ts_off = pl.multiple_of(zs - (t0 - Q_BLOCK), Q_BLOCK)
    zeros_q = jnp.zeros((HEAD_DIM, width), BF16)

    col_ok = _tile4(t0 + tl) >= CMP_LEN - 1
    j = lax.broadcasted_iota(I32, (nsel, LANES), 0)
    forced = (j == 0) | (j == cur) | (j == cur - 1)
    @pl.when(c == 0)
    def _():
        imp_ref[...] = jnp.zeros(imp_ref.shape, F32)


    ws = pl.multiple_of(jnp.maximum(t0 - WINDOW, 0), Q_BLOCK)
    tw_off = pl.multiple_of(ws - (t0 - WINDOW), Q_BLOCK)
    nw = WINDOW + Q_BLOCK
    lo = pl.multiple_of(jnp.clip((8 * c - 16) // 16 * 16, 0, ncr - TC_WIN), 16)
    tc_off = pl.multiple_of(lo - (8 * c - 16) + TC_LEAD, 8)
    qgs = [jnp.concatenate([qT_ref[(GQA * g + r) * HEAD_DIM:(GQA * g + r + 1) * HEAD_DIM, :]
                            for r in range(GQA)], axis=1) for g in range(N_KV)]

    for g in range(N_KV):
        qw = jnp.concatenate([qgs[g], zeros_q] if g == 0 else [zeros_q, qgs[g]], axis=0)
        sw_ref[g] = (jnp.dot(kw_ref[pl.ds(ws, nw), :], qw, preferred_element_type=F32)
                     + tw_ref[g, pl.ds(tw_off, nw), :])

    def compressed(rows):
        for g in range(N_KV):
            qc = jnp.concatenate([qgs[g], zeros_q], axis=0)
            s = jnp.dot(kc_ref[g, 0:rows, :], qc, preferred_element_type=F32)
            row = lax.broadcasted_iota(I32, (rows, width), 0)
            sc_ref[g, 0:rows, :] = jnp.where(row < lo, s, NEG)
            s_loc = jnp.dot(kc_ref[g, pl.ds(lo, TC_WIN), :], qc, preferred_element_type=F32)
            sc_ref[g, pl.ds(lo, TC_WIN), :] = s_loc + tc_ref[g, pl.ds(tc_off, TC_WIN), :]
        for g in range(N_KV):
            s = sc_ref[g, 0:rows, :]
            m = jnp.max(s, axis=0, keepdims=True)
            e = jnp.exp2(s - m)
            l = jnp.sum(e, axis=0, keepdims=True)
            p = e * jnp.where(col_ok, 1.0 / jnp.maximum(l, 1e-30), 0.0)
            ocmp_ref[g] = jnp.dot(vcT_ref[g, :, 0:rows], p.astype(BF16), preferred_element_type=F32)
            imp_ref[g, 8:8 + rows, :] = (
                (p[:, 0:LANES] + p[:, LANES:2 * LANES]) + p[:, 2 * LANES:3 * LANES] + p[:, 3 * LANES:4 * LANES])

    sizes = list(range(CMP_STEP, ncr, CMP_STEP)) + [ncr]
    variant = jnp.minimum((lo + TC_WIN + CMP_STEP - 1) // CMP_STEP, len(sizes)) - 1
    for v, rows in enumerate(sizes):
        pl.when(variant == v)(functools.partial(compressed, rows))

    o_cmp, scores0 = [], []
    for g in range(N_KV):
        o_cmp.append(ocmp_ref[g])
        isel = jnp.zeros((nsel, LANES), F32)
        for off in (0, -1, 1, 0, 2, 1, 3, 2):
            isel = isel + imp_ref[g, pl.ds(8 + off, nsel, stride=4), :]
        score = jnp.where(forced, -jnp.inf, isel)
        scores0.append(jnp.where(j > cur, NEG, score))

    o_win = []
    for g in range(N_KV):
        sw = sw_ref[g]
        mw = jnp.max(sw, axis=0, keepdims=True)
        pw = jnp.exp2(sw - mw)
        accw = jnp.dot(vwT_ref[g, :, pl.ds(ws, nw)], pw.astype(BF16), preferred_element_type=F32)
        o_win.append(accw[0:HEAD_DIM] * (1.0 / jnp.maximum(accw[HEAD_DIM:HEAD_DIM + 1], 1e-30)))

    picks = list(scores0)
    for _ in range(SEL_TOPK - 3):
        for g in range(N_KV):
            mx = jnp.max(picks[g], axis=0, keepdims=True)
            first = jnp.min(jnp.where(picks[g] == mx, j, nsel), axis=0, keepdims=True)
            picks[g] = jnp.where(j == first, -jnp.inf, picks[g])

    m_init, acc_init, gates = [], [], []
    for g in range(N_KV):
        qg = qgs[g]
        b31row = b31_ref[g, 0:1, :]
        chosen = (picks[g] == -jnp.inf) & (j <= cur)
        a_all = _tile4(jnp.where(chosen, 0.0, -MASK_BIG)).reshape(noct, BLOCKS_PER_CHUNK, width)
        a_far = _tile4(jnp.where(chosen & (j < zb), 0.0, -MASK_BIG)).reshape(noct, BLOCKS_PER_CHUNK, width)
        b_hi = b31row.astype(BF16).astype(F32)
        erow = lax.broadcasted_iota(I32, (8, width), 0)
        extra = jnp.where(erow == 0, b_hi, jnp.where(erow == 1, b31row - b_hi, 0.0))
        aoct_ref[g, 0:noct] = jnp.concatenate(
            [a_far, jnp.broadcast_to(extra[None], (noct, 8, width))], axis=1).astype(BF16)
        aoct_ref[g, noct] = jnp.concatenate([jnp.full((8, width), -MASK_BIG, F32), extra], axis=0).astype(BF16)
        aall_ref[g] = jnp.concatenate([a_all, jnp.zeros((noct, 8, width), F32)], axis=1).astype(BF16)
        for q_ref in (qs_ref, qsb_ref):
            q_ref[g, 0:HEAD_DIM, :] = qg
            q_ref[g, HEAD_DIM + 16:LANES, :] = jnp.zeros((LANES - HEAD_DIM - 16, width), BF16)

        halves = []
        for h in range(2):
            qz_ref[2 * g + h, HEAD_DIM:HEAD_DIM + 16, :] = aall_ref[g, (zb + 2 * h) // BLOCKS_PER_CHUNK]
            qz_ref[2 * g + h, 0:HEAD_DIM, :] = qg
            qz_ref[2 * g + h, HEAD_DIM + 16:LANES, :] = jnp.zeros((LANES - HEAD_DIM - 16, width), BF16)
            sh = jnp.dot(ks_ref[g, pl.ds(zs + h * Q_BLOCK, Q_BLOCK), :], qz_ref[2 * g + h], preferred_element_type=F32)
            halves.append(sh + ts_ref[g, pl.ds(ts_off + h * Q_BLOCK, Q_BLOCK), :])
        sz = jnp.concatenate(halves, axis=0)
        mz = jnp.max(sz, axis=0, keepdims=True)
        pz = jnp.exp2(sz - mz)
        m_init.append(mz)
        acc_init.append(jnp.dot(vsT_ref[g, :, pl.ds(zs, 2 * Q_BLOCK)], pz.astype(BF16), preferred_element_type=F32))
        gates.append([jnp.concatenate([gT_ref[br * N_HEADS + GQA * g + r:br * N_HEADS + GQA * g + r + 1, :]
                                       for r in range(GQA)], axis=1) for br in range(N_BRANCH)])

    nfar = (zb + BLOCKS_PER_CHUNK - 1) // BLOCKS_PER_CHUNK

    nsub = 1
    sub = CHUNK // nsub

    def score(g, u, buf_ref, h, q_ref):
        k0 = pl.multiple_of(jnp.minimum(u, noct - 1) * CHUNK + h * sub, sub)
        s = jnp.dot(ks_ref[g, pl.ds(k0, sub), :], q_ref[g], preferred_element_type=F32)
        buf_ref[g, h * sub:(h + 1) * sub, :] = s
        return jnp.max(s, axis=0, keepdims=True)

    def set_mask_rows(g, u, q_ref):
        oct_id = jnp.where(u < nfar, u, noct)
        q_ref[g, HEAD_DIM:HEAD_DIM + 16, :] = aoct_ref[g, oct_id]

    def accumulate(g, u, buf_ref, h, mn, acc):
        k0 = pl.multiple_of(jnp.minimum(u, noct - 1) * CHUNK + h * sub, sub)
        p = jnp.exp2(buf_ref[g, h * sub:(h + 1) * sub, :] - mn)
        return acc + jnp.dot(vsT_ref[g, :, pl.ds(k0, sub)], p.astype(BF16), preferred_element_type=F32)

    def step(u, cur_ref, nxt_ref, q_ref, state, cms):
        out_state, out_cms = [], []
        for g in range(N_KV):
            m, acc = state[2 * g], state[2 * g + 1]
            mn = jnp.maximum(m, cms[g])
            acc = jnp.exp2(m - mn) * acc
            set_mask_rows(g, u + 1, q_ref)
            cm = None
            for h in range(nsub):
                ch = score(g, u + 1, nxt_ref, h, q_ref)
                cm = ch if cm is None else jnp.maximum(cm, ch)
                acc = accumulate(g, u, cur_ref, h, mn, acc)
            out_cms.append(cm)
            out_state += [mn, acc]
        return out_state, out_cms

    def far(i, carry):
        state, cm_b = step(2 * i, sa_ref, sb_ref, qsb_ref, carry[0:4], carry[4:6])
        state, cm_a = step(2 * i + 1, sb_ref, sa_ref, qs_ref, state, cm_b)
        return tuple(state + cm_a)

    first = []
    for g in range(N_KV):
        set_mask_rows(g, 0, qs_ref)
        cm = None
        for h in range(nsub):
            ch = score(g, 0, sa_ref, h, qs_ref)
            cm = ch if cm is None else jnp.maximum(cm, ch)
        first.append(cm)
    nchunks = jnp.maximum(nfar, 1)
    npairs = (nchunks - 1) // 2
    carry = lax.fori_loop(0, npairs, far, (m_init[0], acc_init[0], m_init[1], acc_init[1], first[0], first[1]))
    u_last = 2 * npairs

    def finish(two_left):
        for g in range(N_KV):
            m, acc, cm = carry[2 * g], carry[2 * g + 1], carry[4 + g]
            mn = jnp.maximum(m, cm)
            acc = jnp.exp2(m - mn) * acc
            if two_left:
                set_mask_rows(g, u_last + 1, qsb_ref)
                cm_b = score(g, u_last + 1, sb_ref, 0, qsb_ref)
            acc = accumulate(g, u_last, sa_ref, 0, mn, acc)
            if two_left:
                mn_b = jnp.maximum(mn, cm_b)
                acc = accumulate(g, u_last + 1, sb_ref, 0, mn_b, jnp.exp2(mn - mn_b) * acc)
            fin_ref[g] = acc

    pl.when(nchunks - u_last == 1)(lambda: finish(False))
    pl.when(nchunks - u_last == 2)(lambda: finish(True))
    carry = [None, fin_ref[0], None, fin_ref[1]]

    o_all, ssq = [], jnp.zeros((1, LANES), F32)
    for g in range(N_KV):
        acc = carry[2 * g + 1]
        o_sel = acc[0:HEAD_DIM] * (1.0 / jnp.maximum(acc[HEAD_DIM:HEAD_DIM + 1], 1e-30))
        o = gates[g][0] * o_cmp[g] + gates[g][1] * o_sel + gates[g][2] * o_win[g]
        o_all.append(o)
        cs = jnp.sum(o * o, axis=0, keepdims=True)
        ssq = ssq + ((cs[:, 0:LANES] + cs[:, LANES:2 * LANES]) + (cs[:, 2 * LANES:3 * LANES] + cs[:, 3 * LANES:]))
    inv = _tile4(lax.rsqrt(ssq * (1.0 / ATTN_W) + EPS))
    for g in range(N_KV):
        on = o_all[g] * inv * gain_ref[g]
        for k in range(GQA // 2):
            pair = jnp.concatenate([on[:, (2 * k) * LANES:(2 * k + 1) * LANES],
                                    on[:, (2 * k + 1) * LANES:(2 * k + 2) * LANES]], axis=0)
            col = (GQA * g + 2 * k) * HEAD_DIM
            out_ref[:, col:col + 2 * HEAD_DIM] = pair.T.astype(BF16)


def _attention(qT, gT, ks, vsT, kw, vwT, kc, vcT, tables, gain_b):
    S = qT.shape[1]
    nq = S // Q_BLOCK
    ncr = kc.shape[1]
    nsel = S // SEL_LEN
    noct = S // CHUNK
    width = GQA * LANES
    tw, ts, tc, b31 = tables
    vmem = pl.BlockSpec(memory_space=pltpu.VMEM)
    in_specs = [
        pl.BlockSpec((ATTN_W, Q_BLOCK), lambda c: (0, c)),
        pl.BlockSpec((N_BRANCH * N_HEADS, Q_BLOCK), lambda c: (0, c)),
    ] + [vmem] * 11
    return pl.pallas_call(
        _attn_kernel,
        grid=(nq,),
        in_specs=in_specs,
        out_specs=pl.BlockSpec((Q_BLOCK, ATTN_W), lambda c: (c, 0)),
        out_shape=jax.ShapeDtypeStruct((S, ATTN_W), BF16),
        scratch_shapes=[
            pltpu.VMEM((N_KV, ncr, width), F32),
            pltpu.VMEM((N_KV, WINDOW + Q_BLOCK, width), F32),
            pltpu.VMEM((N_KV, ncr + 16, LANES), F32),
            pltpu.VMEM((N_KV, noct + 1, 16, width), BF16),
            pltpu.VMEM((N_KV, noct, 16, width), BF16),
            pltpu.VMEM((N_KV, LANES, width), BF16),
            pltpu.VMEM((N_KV, CHUNK, width), F32),
            pltpu.VMEM((N_KV, CHUNK, width), F32),
            pltpu.VMEM((2 * N_KV, LANES, width), BF16),
            pltpu.VMEM((N_KV, LANES, width), BF16),
            pltpu.VMEM((N_KV, V_ROWS, width), F32),
            pltpu.VMEM((N_KV, HEAD_DIM, width), F32),
        ],
        compiler_params=pltpu.CompilerParams(dimension_semantics=("arbitrary",), vmem_limit_bytes=VMEM_LIMIT),
        name="nsa_attention",
    )(qT, gT, ks, vsT, kw, vwT, kc, vcT, tw, ts, tc, b31, gain_b)


def _mix_and_norm(x_ref, attn_ref, conv_ref, wout_ref, g2_ref):
    x1 = x_ref[...] + jnp.dot(attn_ref[...], wout_ref[0:ATTN_W, :], preferred_element_type=F32) \
        + jnp.dot(conv_ref[...], wout_ref[ATTN_W:, :], preferred_element_type=F32)
    ms = jnp.mean(x1 * x1, axis=-1, keepdims=True)
    h2 = x1 * lax.rsqrt(ms + EPS) * g2_ref[...]
    return x1, h2


def _ffn_kernel(x_ref, attn_ref, conv_ref, wout_ref, g2_ref, wg_ref, wu_ref, wd_ref, out_ref):
    x1, h2 = _mix_and_norm(x_ref, attn_ref, conv_ref, wout_ref, g2_ref)
    h2 = h2.astype(BF16)
    a = jnp.dot(h2, wg_ref[...], preferred_element_type=F32)
    u = jnp.dot(h2, wu_ref[...], preferred_element_type=F32)
    y = (a * jax.nn.sigmoid(a) * u).astype(BF16)
    out_ref[...] = x1 + jnp.dot(y, wd_ref[...], preferred_element_type=F32)


_FFN_TM = 512


def _outproj_ffn(x2, attn_n, conv_n, wout, g2, wg, wu, wd):
    S = x2.shape[0]
    tm = _FFN_TM
    resident = pl.BlockSpec(memory_space=pltpu.VMEM)
    return pl.pallas_call(
        _ffn_kernel,
        grid=(S // tm,),
        in_specs=[
            pl.BlockSpec((tm, D_MODEL), lambda i: (i, 0)),
            pl.BlockSpec((tm, ATTN_W), lambda i: (i, 0)),
            pl.BlockSpec((tm, CONV_W), lambda i: (i, 0)),
            resident, resident, resident, resident, resident,
        ],
        out_specs=pl.BlockSpec((tm, D_MODEL), lambda i: (i, 0)),
        out_shape=jax.ShapeDtypeStruct((S, D_MODEL), F32),
        compiler_params=pltpu.CompilerParams(dimension_semantics=("arbitrary",), vmem_limit_bytes=VMEM_LIMIT),
        name="outproj_ffn",
    )(x2, attn_n, conv_n, wout, g2, wg, wu, wd)


TOKEN_TILE = (D_MODEL // LANES, LANES)


def _to_token_tiles(ref, rows):
    x = jnp.stack([rows[:, k * LANES:(k + 1) * LANES] for k in range(TOKEN_TILE[0])], axis=0)
    ref[...] = pltpu.einshape('ktl->tkl', x)


def _from_token_tiles(ref):
    x = pltpu.einshape('tkl->ktl', ref[...])
    return jnp.concatenate([x[k] for k in range(TOKEN_TILE[0])], axis=1)


def _router_kernel(x_ref, attn_ref, conv_ref, wout_ref, g2_ref, rw_ref, rb_ref, tri_ref,
                   x1_ref, h2_ref, route_ref, cnt_ref, run_ref):
    tm = x_ref.shape[0]
    i = pl.program_id(0)

    @pl.when(i == 0)
    def _():
        run_ref[...] = jnp.zeros(run_ref.shape, F32)

    x1, h2 = _mix_and_norm(x_ref, attn_ref, conv_ref, wout_ref, g2_ref)
    x1_ref[...] = x1
    h2b = h2.astype(BF16)
    _to_token_tiles(h2_ref, h2b.astype(F32))
    lane = lax.broadcasted_iota(I32, (tm, LANES), 1)
    logits = jnp.dot(h2b, rw_ref[...], preferred_element_type=F32) + rb_ref[...]
    logits = jnp.where(lane < N_EXPERTS, logits, -jnp.inf)
    m1 = jnp.max(logits, axis=-1, keepdims=True)
    i1 = jnp.min(jnp.where(logits == m1, lane, LANES), axis=-1, keepdims=True)
    rest = jnp.where(lane == i1, -jnp.inf, logits)
    m2 = jnp.max(rest, axis=-1, keepdims=True)
    i2 = jnp.min(jnp.where(rest == m2, lane, LANES), axis=-1, keepdims=True)
    e2 = jnp.exp(m2 - m1)
    den = 1.0 + e2
    oh1 = (lane == i1).astype(F32)
    oh2 = (lane == i2).astype(F32)
    both = oh1 + oh2
    before = run_ref[0:1, :] + jnp.dot(tri_ref[...], both.astype(BF16), preferred_element_type=F32)
    rank1 = jnp.sum(before * oh1, axis=-1, keepdims=True)
    rank2 = jnp.sum(before * oh2, axis=-1, keepdims=True)
    fields = (i1.astype(F32), i2.astype(F32), rank1, rank2, 1.0 / den, e2 / den)
    route = jnp.zeros((tm, LANES), F32)
    for k, v in enumerate(fields):
        route = jnp.where(lane == k, v, route)
    route_ref[...] = route
    run_ref[...] = run_ref[...] + jnp.sum(both, axis=0, keepdims=True)
    cnt_ref[...] = run_ref[...]


def _outproj_router(x2, attn_n, conv_n, wout, g2, rw, rb):
    S = x2.shape[0]
    tm = _FFN_TM
    tri = np.tril(np.ones((tm, tm), np.float32), -1)
    const = lambda shape: pl.BlockSpec(shape, lambda i: (0,) * len(shape))
    return pl.pallas_call(
        _router_kernel,
        grid=(S // tm,),
        in_specs=[
            pl.BlockSpec((tm, D_MODEL), lambda i: (i, 0)),
            pl.BlockSpec((tm, ATTN_W), lambda i: (i, 0)),
            pl.BlockSpec((tm, CONV_W), lambda i: (i, 0)),
            const(wout.shape), const((1, D_MODEL)), const(rw.shape), const((1, LANES)), const((tm, tm)),
        ],
        out_specs=(pl.BlockSpec((tm, D_MODEL), lambda i: (i, 0)),
                   pl.BlockSpec((tm,) + TOKEN_TILE, lambda i: (i, 0, 0)),
                   pl.BlockSpec((tm, LANES), lambda i: (i, 0)),
                   const((8, LANES))),
        out_shape=(jax.ShapeDtypeStruct((S, D_MODEL), F32),
                   jax.ShapeDtypeStruct((S,) + TOKEN_TILE, F32),
                   jax.ShapeDtypeStruct((S, LANES), F32),
                   jax.ShapeDtypeStruct((8, LANES), F32)),
        scratch_shapes=[pltpu.VMEM((8, LANES), F32)],
        compiler_params=pltpu.CompilerParams(dimension_semantics=("arbitrary",), vmem_limit_bytes=VMEM_LIMIT),
        name="outproj_router",
    )(x2, attn_n, conv_n, wout, g2, rw, rb, jnp.asarray(tri, BF16))


_ROW_TM = 256
_EXP_TM = 512
_DMA_UNROLL = 8


def _row_copy(src_ref, src_row, dst_ref, dst_row, sem):
    return pltpu.make_async_copy(src_ref.at[src_row], dst_ref.at[dst_row], sem)


def _dispatch_kernel(pos_ref, h_ref, zero_ref, xs_ref, hbuf_ref, sem):
    del zero_ref
    tm = h_ref.shape[0]
    i = pl.program_id(0)
    slot = i % 2
    hbuf_ref[slot] = h_ref[...]

    def issue(t, carry):
        for k in range(2):
            _row_copy(hbuf_ref.at[slot], t, xs_ref, pos_ref[0, 0, 2 * t + k], sem.at[slot]).start(priority=k)
        return carry

    lax.fori_loop(0, tm, issue, 0, unroll=_DMA_UNROLL)

    def drain(which):
        def body(t, carry):
            for k in range(2):
                _row_copy(hbuf_ref.at[which], 0, xs_ref, 0, sem.at[which]).wait()
            return carry
        lax.fori_loop(0, tm, body, 0, unroll=_DMA_UNROLL)

    @pl.when(i > 0)
    def _():
        drain(1 - slot)

    @pl.when(i == pl.num_programs(0) - 1)
    def _():
        drain(slot)


def _dispatch(pos3, h2, n_rows):
    S = h2.shape[0]
    tm = _ROW_TM
    zeros = jnp.zeros((n_rows,) + TOKEN_TILE, F32)
    return pl.pallas_call(
        _dispatch_kernel,
        grid=(S // tm,),
        in_specs=[
            pl.BlockSpec((1, 1, 2 * tm), lambda i: (i, 0, 0), memory_space=pltpu.SMEM),
            pl.BlockSpec((tm,) + TOKEN_TILE, lambda i: (i, 0, 0)),
            pl.BlockSpec(memory_space=pl.ANY),
        ],
        out_specs=pl.BlockSpec(memory_space=pl.ANY),
        out_shape=jax.ShapeDtypeStruct((n_rows,) + TOKEN_TILE, F32),
        scratch_shapes=[pltpu.VMEM((2, tm) + TOKEN_TILE, F32), pltpu.SemaphoreType.DMA((2,))],
        input_output_aliases={2: 0},
        compiler_params=pltpu.CompilerParams(dimension_semantics=("arbitrary",), vmem_limit_bytes=VMEM_LIMIT),
        name="moe_dispatch",
    )(pos3, h2, zeros)


def _experts_kernel(te_ref, tb_ref, nt_ref, xs_ref, wg_ref, wu_ref, wd_ref, ys_ref):
    i = pl.program_id(0)

    @pl.when(i < nt_ref[0])
    def _():
        x = _from_token_tiles(xs_ref).astype(BF16)
        a = jnp.dot(x, wg_ref[0], preferred_element_type=F32)
        u = jnp.dot(x, wu_ref[0], preferred_element_type=F32)
        y = (a * jax.nn.sigmoid(a) * u).astype(BF16)
        _to_token_tiles(ys_ref, jnp.dot(y, wd_ref[0], preferred_element_type=F32))

    @pl.when(i >= nt_ref[0])
    def _():
        ys_ref[...] = jnp.zeros(ys_ref.shape, F32)


def _experts(tile_e, tile_b, n_tiles, xs, wg, wu, wd):
    n_rows = xs.shape[0]
    tm = _EXP_TM
    weights = lambda shape: pl.BlockSpec(shape, lambda i, te, tb, nt: (te[i], 0, 0), pipeline_mode=pl.Buffered(1))
    grid_spec = pltpu.PrefetchScalarGridSpec(
        num_scalar_prefetch=3,
        grid=(n_rows // tm,),
        in_specs=[
            pl.BlockSpec((tm,) + TOKEN_TILE, lambda i, te, tb, nt: (tb[i], 0, 0)),
            weights((1, D_MODEL, D_FF)), weights((1, D_MODEL, D_FF)), weights((1, D_FF, D_MODEL)),
        ],
        out_specs=pl.BlockSpec((tm,) + TOKEN_TILE, lambda i, te, tb, nt: (tb[i], 0, 0)),
    )
    return pl.pallas_call(
        _experts_kernel,
        grid_spec=grid_spec,
        out_shape=jax.ShapeDtypeStruct((n_rows,) + TOKEN_TILE, F32),
        compiler_params=pltpu.CompilerParams(dimension_semantics=("arbitrary",), vmem_limit_bytes=VMEM_LIMIT),
        name="moe_experts",
    )(tile_e, tile_b, n_tiles, xs, wg, wu, wd)


def _combine_kernel(pos_ref, nxt_ref, x1_ref, route_ref, ys_ref, out_ref, y1_ref, y2_ref, sem):
    tm = x1_ref.shape[0]
    i = pl.program_id(0)
    slot = i % 2

    def issue(src_pos_ref, which):
        def body(t, carry):
            _row_copy(ys_ref, src_pos_ref[0, 0, 2 * t], y1_ref.at[which], t, sem.at[which]).start(priority=0)
            _row_copy(ys_ref, src_pos_ref[0, 0, 2 * t + 1], y2_ref.at[which], t, sem.at[which]).start(priority=1)
            return carry
        lax.fori_loop(0, tm, body, 0, unroll=_DMA_UNROLL)

    @pl.when(i == 0)
    def _():
        issue(pos_ref, slot)

    @pl.when(i + 1 < pl.num_programs(0))
    def _():
        issue(nxt_ref, 1 - slot)

    def drain(t, carry):
        _row_copy(ys_ref, 0, y1_ref.at[slot], 0, sem.at[slot]).wait()
        _row_copy(ys_ref, 0, y2_ref.at[slot], 0, sem.at[slot]).wait()
        return carry

    lax.fori_loop(0, tm, drain, 0, unroll=_DMA_UNROLL)
    lane = lax.broadcasted_iota(I32, (tm, LANES), 1)
    route = route_ref[...]
    w1 = jnp.sum(jnp.where(lane == 4, route, 0.0), axis=-1, keepdims=True)
    w2 = jnp.sum(jnp.where(lane == 5, route, 0.0), axis=-1, keepdims=True)
    out_ref[...] = x1_ref[...] + (_from_token_tiles(y1_ref.at[slot]) * w1 + _from_token_tiles(y2_ref.at[slot]) * w2)


def _combine(pos3, x1, route, ys):
    S = x1.shape[0]
    tm = _ROW_TM
    last = S // tm - 1
    return pl.pallas_call(
        _combine_kernel,
        grid=(S // tm,),
        in_specs=[
            pl.BlockSpec((1, 1, 2 * tm), lambda i: (i, 0, 0), memory_space=pltpu.SMEM),
            pl.BlockSpec((1, 1, 2 * tm), lambda i: (jnp.minimum(i + 1, last), 0, 0), memory_space=pltpu.SMEM),
            pl.BlockSpec((tm, D_MODEL), lambda i: (i, 0)),
            pl.BlockSpec((tm, LANES), lambda i: (i, 0)),
            pl.BlockSpec(memory_space=pl.ANY),
        ],
        out_specs=pl.BlockSpec((tm, D_MODEL), lambda i: (i, 0)),
        out_shape=jax.ShapeDtypeStruct((S, D_MODEL), F32),
        scratch_shapes=[pltpu.VMEM((2, tm) + TOKEN_TILE, F32), pltpu.VMEM((2, tm) + TOKEN_TILE, F32),
                        pltpu.SemaphoreType.DMA((2,))],
        compiler_params=pltpu.CompilerParams(dimension_semantics=("arbitrary",), vmem_limit_bytes=VMEM_LIMIT),
        name="moe_combine",
    )(pos3, pos3, x1, route, ys)


def _moe(h2, x1, route, counts, wg, wu, wd):
    S = h2.shape[0]
    tm = _EXP_TM
    n_tiles_max = 2 * S // tm + N_EXPERTS
    cnt = counts[0, 0:N_EXPERTS].astype(I32)
    tiles = (cnt + tm - 1) // tm
    first = jnp.cumsum(tiles) - tiles
    n_tiles = jnp.sum(tiles)
    idx = jnp.arange(n_tiles_max, dtype=I32)
    last = jnp.minimum(idx, n_tiles - 1)
    tile_e = (jnp.sum(last[:, None] >= first[None, :], axis=1) - 1).astype(I32)
    eid = route[:, 0:2].astype(I32)
    pos = first[eid] * tm + route[:, 2:4].astype(I32)
    pos3 = pos.reshape(S // _ROW_TM, 1, 2 * _ROW_TM)
    xs = _dispatch(pos3, h2, n_tiles_max * tm)
    ys = _experts(tile_e, idx, n_tiles.reshape(1), xs, wg, wu, wd)
    return _combine(pos3, x1, route, ys)


def _split_w_in(w):
    o = np.cumsum([0, ATTN_W] + [KV_W] * 6 + [N_BRANCH * N_HEADS] + [CONV_W] * 3)
    q, kc, vc, ksl, vsl, kwn, vwn, gts, cb, cc, ch = (w[:, o[i]:o[i + 1]] for i in range(11))
    perm = np.array([h * N_BRANCH + br for br in range(N_BRANCH) for h in range(N_HEADS)])
    wtok = jnp.concatenate([kc, vc, ksl, kwn, cb, cc, ch], axis=1).astype(BF16)
    feat = jnp.concatenate([q, vsl, vwn, gts[:, perm], jnp.zeros((D_MODEL, 8), w.dtype)], axis=1)
    return wtok, feat.T.astype(BF16)


def _expand_cmp_w1(w1k, w1v):
    kinds = jnp.stack([w1k, w1v]).reshape(2, CMP_LEN, HEAD_DIM, CMP_HIDDEN)
    per_col = jnp.repeat(kinds, N_KV, axis=0)
    eye = jnp.eye(2 * N_KV, dtype=F32)
    out = []
    for l0 in (0, CMP_STRIDE):
        w = per_col[:, l0:l0 + CMP_STRIDE].transpose(1, 0, 2, 3)
        blk = w[:, :, :, None, :] * eye[None, :, None, :, None]
        out.append(blk.reshape(CMP_STRIDE * 2 * KV_W, 2 * N_KV * CMP_HIDDEN).astype(BF16))
    return out


def kernel(x, rel_bias, norm1, w_in, q_norm, k_norm, cmp_pos_k, cmp_pos_v, cmp_k_w1, cmp_k_w2, cmp_v_w1, cmp_v_w2,
           conv_w, attn_out_norm, conv_out_norm, w_out, norm2, ffn_w_gate, ffn_w_up, ffn_w_down, router_w, router_b,
           moe_w_gate, moe_w_up, moe_w_down):
    B, S, _ = x.shape
    assert B == 1 and S % CHUNK == 0 and S >= WINDOW + Q_BLOCK
    depth = norm1.shape[0]
    x2 = x.reshape(S, D_MODEL)
    tables = _bias_tables(rel_bias)
    for layer in range(depth):
        wtok, wfeat = _split_w_in(w_in[layer])
        kgain = jnp.concatenate([jnp.tile(k_norm[layer, 1], N_KV), jnp.tile(k_norm[layer, 2], N_KV)])[None, :]
        gq = jnp.broadcast_to(q_norm[layer][:, None], (HEAD_DIM, _IN_TM))
        convw = jnp.pad(conv_w[layer], ((0, 8 - CONV_K), (0, 0)))
        qT, gT, vsT, vwT, ks, kw, kcv, conv_n = _in_proj(
            x2, norm1[layer][None, :], wtok, wfeat, kgain, gq, convw, conv_out_norm[layer][None, :])

        wtop, wbot = _expand_cmp_w1(cmp_k_w1[layer], cmp_v_w1[layer])
        posk = jnp.broadcast_to(cmp_pos_k[layer].reshape(1, -1), (8, CMP_LEN * HEAD_DIM))
        posv = jnp.broadcast_to(cmp_pos_v[layer].reshape(1, -1), (8, CMP_LEN * HEAD_DIM))
        w2k = jnp.pad(cmp_k_w2[layer], ((0, 0), (0, LANES - HEAD_DIM))).astype(BF16)
        w2vT = cmp_v_w2[layer].T.astype(BF16)
        kcgain = jnp.pad(k_norm[layer, 0], (0, LANES - HEAD_DIM))[None, :]
        kc, vcT = _compress(kcv, wtop, wbot, posk, posv, cmp_k_w1[layer], cmp_v_w1[layer], w2k, w2vT, kcgain)

        gain_b = jnp.broadcast_to(attn_out_norm[layer].reshape(N_KV, GQA, HEAD_DIM).transpose(0, 2, 1)[:, :, :, None],
                                  (N_KV, HEAD_DIM, GQA, LANES)).reshape(N_KV, HEAD_DIM, GQA * LANES)
        attn_n = _attention(qT, gT, ks, vsT, kw, vwT, kc, vcT, tables, gain_b)

        wout = w_out[layer].astype(BF16)
        g2 = norm2[layer][None, :]
        i = layer // 2
        if layer % 2 == 0:
            x2 = _outproj_ffn(x2, attn_n, conv_n, wout, g2, ffn_w_gate[i].astype(BF16), ffn_w_up[i].astype(BF16),
                              ffn_w_down[i].astype(BF16))
        else:
            rw = jnp.pad(router_w[i], ((0, 0), (0, LANES - N_EXPERTS))).astype(BF16)
            rb = jnp.pad(router_b[i], (0, LANES - N_EXPERTS))[None, :]
            x1, h2, route, counts = _outproj_router(x2, attn_n, conv_n, wout, g2, rw, rb)
            x2 = _moe(h2, x1, route, counts, moe_w_gate[i].astype(BF16), moe_w_up[i].astype(BF16),
                      moe_w_down[i].astype(BF16))
    return x2.reshape(B, S, D_MODEL)
```

```python
import functools
import math

import numpy as np
import jax
import jax.numpy as jnp
from jax import lax
from jax.experimental import pallas as pl
from jax.experimental.pallas import tpu as pltpu

F32 = jnp.float32
BF16 = jnp.bfloat16
I32 = jnp.int32

D_MODEL = 1024
HEAD_DIM = 64
N_HEADS = 8
N_KV = 2
GQA = N_HEADS // N_KV
ATTN_W = N_HEADS * HEAD_DIM
KV_W = N_KV * HEAD_DIM
N_BRANCH = 3
CONV_W = 512
CONV_K = 3
CMP_LEN = 32
CMP_STRIDE = 16
CMP_HIDDEN = 128
SEL_LEN = 64
SEL_TOPK = 16
WINDOW = 512
Q_BLOCK = 128
N_BUCKETS = 32
MAX_DISTANCE = 128
D_FF = 2816
N_EXPERTS = 8
EPS = 1e-6
NEG = -1e30
FORCE = 1e9
MASK_BIG = 2.0 ** 60
SCALE = HEAD_DIM ** -0.5
LOG2E = math.log2(math.e)

LANES = 128
V_ROWS = 80
CHUNK = 512
BLOCKS_PER_CHUNK = CHUNK // SEL_LEN
TC_LEAD = 24
TC_ROWS = 88
TC_WIN = 48
CMP_STEP = 256
VMEM_LIMIT = 56 * 1024 * 1024


def _bucket_np(dist):
    n = np.maximum(dist, 0)
    max_exact = N_BUCKETS // 2
    nf = np.maximum(n, max_exact).astype(np.float64)
    v = np.log(nf / max_exact) / math.log(MAX_DISTANCE / max_exact) * (N_BUCKETS - max_exact)
    frac = np.abs(v - np.round(v))
    assert np.all((frac > 1e-6) | (n <= max_exact) | (n >= MAX_DISTANCE)), "bucket boundary is precision dependent"
    large = np.minimum(max_exact + (v + 1e-9).astype(np.int32), N_BUCKETS - 1)
    return np.where(n < max_exact, n, large).astype(np.int32)


def _index_tables():
    tl = np.arange(Q_BLOCK)[None, :]
    r = np.arange(WINDOW + Q_BLOCK + WINDOW)[:, None]
    d = tl + WINDOW - r
    idx_w = np.where((d >= 0) & (d < WINDOW), _bucket_np(d), -1)
    r = np.arange(3 * Q_BLOCK)[:, None]
    d = tl + Q_BLOCK - r
    idx_s = np.where(d >= 0, _bucket_np(d), -1)
    r = np.arange(TC_ROWS)[:, None] - TC_LEAD
    d = tl - CMP_STRIDE * r + (CMP_STRIDE * 16 - (CMP_LEN - 1))
    idx_c = np.where((d >= 0) & (r < 32), _bucket_np(d), -1)
    return idx_w.astype(np.int32), idx_s.astype(np.int32), idx_c.astype(np.int32)


def _tables_kernel(rb_ref, iw_ref, is_ref, ic_ref, tw_ref, ts_ref, tc_ref, b31_ref):
    for h in range(N_HEADS):
        g, r = divmod(h, GQA)
        lanes = slice(r * LANES, (r + 1) * LANES)
        for idx_ref, out_ref in ((iw_ref, tw_ref), (is_ref, ts_ref), (ic_ref, tc_ref)):
            out_ref[g, :, lanes] = jnp.full(idx_ref.shape, NEG, F32)

            def body(b, carry, idx_ref=idx_ref, out_ref=out_ref, g=g, lanes=lanes, h=h):
                out_ref[g, :, lanes] = jnp.where(idx_ref[...] == b, rb_ref[b, h] * LOG2E, out_ref[g, :, lanes])
                return carry

            lax.fori_loop(0, N_BUCKETS, body, 0)
        far_bias = rb_ref[N_BUCKETS - 1, h] * LOG2E
        tc_ref[g, :, lanes] = jnp.where(ic_ref[...] >= 0, tc_ref[g, :, lanes] - far_bias, NEG)
        b31_ref[g, :, lanes] = jnp.full((8, LANES), far_bias, F32)


def _bias_tables(rel_bias):
    idx_w, idx_s, idx_c = _index_tables()
    width = GQA * LANES
    out_shape = (
        jax.ShapeDtypeStruct((N_KV, idx_w.shape[0], width), F32),
        jax.ShapeDtypeStruct((N_KV, idx_s.shape[0], width), F32),
        jax.ShapeDtypeStruct((N_KV, idx_c.shape[0], width), F32),
        jax.ShapeDtypeStruct((N_KV, 8, width), F32),
    )
    vmem = pl.BlockSpec(memory_space=pltpu.VMEM)
    return pl.pallas_call(
        _tables_kernel,
        out_shape=out_shape,
        in_specs=[pl.BlockSpec(memory_space=pltpu.SMEM), vmem, vmem, vmem],
        out_specs=(vmem, vmem, vmem, vmem),
        name="bias_tables",
    )(rel_bias, jnp.asarray(idx_w), jnp.asarray(idx_s), jnp.asarray(idx_c))


def _in_proj_kernel(x_ref, g1_ref, wtok_ref, wfeat_ref, ind_ref, kgain_ref, aug_ref, gq_ref, convw_ref, cgain_ref,
                    qT_ref, gT_ref, vsT_ref, vwT_ref, ks_ref, kw_ref, kcv_ref, convn_ref, zs_ref, carry_ref):
    tm = _IN_TM
    i = pl.program_id(0)

    @pl.when(i == 0)
    def _():
        carry_ref[...] = jnp.zeros(carry_ref.shape, F32)

    prev_tail = carry_ref[...]
    for part in range(_IN_SPLIT):
        rows = slice(part * tm, (part + 1) * tm)
        x = x_ref[rows, :]
        ms = jnp.mean(x * x, axis=-1, keepdims=True)
        h = (x * lax.rsqrt(ms + EPS) * g1_ref[...]).astype(BF16)
        tok = jnp.dot(h, wtok_ref[...], preferred_element_type=F32)
        feat = lax.dot_general(wfeat_ref[...], h, (((1,), (1,)), ((), ())),
                               preferred_element_type=F32)

        kcv_ref[rows, :] = tok[:, 0:2 * KV_W].astype(BF16)
        kk = tok[:, 2 * KV_W:4 * KV_W]
        sq = kk * kk
        sq_hi = sq.astype(BF16)
        sq_lo = (sq - sq_hi.astype(F32)).astype(BF16)
        ssq = jnp.dot(jnp.concatenate([sq_hi, sq_lo], axis=1), ind_ref[...],
                      preferred_element_type=F32)
        kn = kk * lax.rsqrt(ssq * (1.0 / HEAD_DIM) + EPS) * kgain_ref[...]
        ksl = kn[:, 0:KV_W]
        lane = lax.broadcasted_iota(I32, (tm, LANES), 1)
        aug = aug_ref[...]
        ks_ref[0, rows, :] = jnp.where(lane < HEAD_DIM, ksl, aug).astype(BF16)
        ks_ref[1, rows, :] = jnp.where(lane < HEAD_DIM, pltpu.roll(ksl, HEAD_DIM, 1), aug).astype(BF16)
        kw_ref[rows, :] = kn[:, KV_W:2 * KV_W].astype(BF16)

        c0 = 4 * KV_W
        cb = tok[:, c0:c0 + CONV_W]
        cc = tok[:, c0 + CONV_W:c0 + 2 * CONV_W]
        ch = tok[:, c0 + 2 * CONV_W:c0 + 3 * CONV_W]
        z = cc * ch
        zs_ref[part, 0:8, :] = prev_tail
        zs_ref[part, 8:8 + tm, :] = z
        z1 = zs_ref[part, 7:7 + tm, :]
        z2 = zs_ref[part, 6:6 + tm, :]
        w = convw_ref[...]
        y = w[0:1, :] * z2 + w[1:2, :] * z1 + w[2:3, :] * z
        prev_tail = z[tm - 8:tm, :]
        oc = cb * y
        msc = jnp.mean(oc * oc, axis=-1, keepdims=True)
        convn_ref[rows, :] = (oc * lax.rsqrt(msc + EPS) * cgain_ref[...]).astype(BF16)

        q = feat[0:ATTN_W].reshape(N_HEADS, HEAD_DIM, tm)
        qss = jnp.sum(q * q, axis=1, keepdims=True)
        qn = q * lax.rsqrt(qss * (1.0 / HEAD_DIM) + EPS) * gq_ref[...][None]
        qT_ref[:, rows] = (qn * (SCALE * LOG2E)).reshape(ATTN_W, tm).astype(BF16)
        ones_rows = (lax.broadcasted_iota(I32, (V_ROWS - HEAD_DIM, tm), 0) == 0).astype(BF16)
        for g in range(N_KV):
            r0 = ATTN_W + g * HEAD_DIM
            vsT_ref[g, 0:HEAD_DIM, rows] = feat[r0:r0 + HEAD_DIM].astype(BF16)
            vsT_ref[g, HEAD_DIM:V_ROWS, rows] = ones_rows
            r1 = ATTN_W + KV_W + g * HEAD_DIM
            vwT_ref[g, 0:HEAD_DIM, rows] = feat[r1:r1 + HEAD_DIM].astype(BF16)
            vwT_ref[g, HEAD_DIM:V_ROWS, rows] = ones_rows
        g0 = ATTN_W + 2 * KV_W
        gT_ref[:, rows] = jax.nn.sigmoid(feat[g0:g0 + N_BRANCH * N_HEADS])
    carry_ref[...] = prev_tail


_IN_SPLIT = 2
_IN_TM = 512


def _in_proj(x2, g1, wtok, wfeat, kgain, gq, convw, cgain):
    S = x2.shape[0]
    tm = _IN_TM
    tb = _IN_SPLIT * tm
    nt = S // tb
    ind = np.kron(np.eye(2 * N_KV, dtype=np.float32), np.ones((HEAD_DIM, HEAD_DIM), np.float32))
    ind = np.concatenate([ind, ind], axis=0)
    aug = np.zeros((tm, LANES), np.float32)
    blk = (np.arange(tm) // SEL_LEN) % BLOCKS_PER_CHUNK
    aug[np.arange(tm), HEAD_DIM + blk] = 1.0
    aug[:, HEAD_DIM + BLOCKS_PER_CHUNK:HEAD_DIM + BLOCKS_PER_CHUNK + 2] = 1.0
    const = lambda shape: pl.BlockSpec(shape, lambda i: (0,) * len(shape))
    out_shape = (
        jax.ShapeDtypeStruct((ATTN_W, S), BF16),
        jax.ShapeDtypeStruct((N_BRANCH * N_HEADS, S), F32),
        jax.ShapeDtypeStruct((N_KV, V_ROWS, S), BF16),
        jax.ShapeDtypeStruct((N_KV, V_ROWS, S), BF16),
        jax.ShapeDtypeStruct((N_KV, S, LANES), BF16),
        jax.ShapeDtypeStruct((S, LANES), BF16),
        jax.ShapeDtypeStruct((S, 2 * KV_W), BF16),
        jax.ShapeDtypeStruct((S, CONV_W), BF16),
    )
    out_specs = (
        pl.BlockSpec((ATTN_W, tb), lambda i: (0, i)),
        pl.BlockSpec((N_BRANCH * N_HEADS, tb), lambda i: (0, i)),
        pl.BlockSpec((N_KV, V_ROWS, tb), lambda i: (0, 0, i)),
        pl.BlockSpec((N_KV, V_ROWS, tb), lambda i: (0, 0, i)),
        pl.BlockSpec((N_KV, tb, LANES), lambda i: (0, i, 0)),
        pl.BlockSpec((tb, LANES), lambda i: (i, 0)),
        pl.BlockSpec((tb, 2 * KV_W), lambda i: (i, 0)),
        pl.BlockSpec((tb, CONV_W), lambda i: (i, 0)),
    )
    in_specs = [
        pl.BlockSpec((tb, D_MODEL), lambda i: (i, 0)),
        const((1, D_MODEL)),
        const(wtok.shape),
        const(wfeat.shape),
        const(ind.shape),
        const((1, 2 * KV_W)),
        const(aug.shape),
        const((HEAD_DIM, tm)),
        const((8, CONV_W)),
        const((1, CONV_W)),
    ]
    return pl.pallas_call(
        _in_proj_kernel,
        grid=(nt,),
        in_specs=in_specs,
        out_specs=out_specs,
        out_shape=out_shape,
        scratch_shapes=[pltpu.VMEM((_IN_SPLIT, tm + 8, CONV_W), F32), pltpu.VMEM((8, CONV_W), F32)],
        compiler_params=pltpu.CompilerParams(dimension_semantics=("arbitrary",), vmem_limit_bytes=VMEM_LIMIT),
        name="in_proj",
    )(x2, g1, wtok, wfeat, jnp.asarray(ind, BF16), kgain, jnp.asarray(aug), gq, convw, cgain)


def _compress_kernel(r_ref, wtop_ref, wbot_ref, posk_ref, posv_ref, w1k_ref, w1v_ref, w2k_ref, w2vT_ref, kgain_ref,
                     kc_ref, vcT_ref):
    nr = r_ref.shape[0]
    r = r_ref[...]
    u = jnp.dot(r, wtop_ref[...], preferred_element_type=F32)
    lo = jnp.dot(r, wbot_ref[...], preferred_element_type=F32)
    bias_k = jnp.dot(posk_ref[...], w1k_ref[...], preferred_element_type=F32)[0:1, :]
    bias_v = jnp.dot(posv_ref[...], w1v_ref[...], preferred_element_type=F32)[0:1, :]
    bias4 = jnp.concatenate([bias_k] * N_KV + [bias_v] * N_KV, axis=1)
    hid = u + pltpu.roll(lo, nr - 1, 0) + bias4
    act = jax.nn.gelu(hid)
    for g in range(N_KV):
        ak = act[:, g * CMP_HIDDEN:(g + 1) * CMP_HIDDEN].astype(BF16)
        av = act[:, (N_KV + g) * CMP_HIDDEN:(N_KV + g + 1) * CMP_HIDDEN].astype(BF16)
        kc = jnp.dot(ak, w2k_ref[...], preferred_element_type=F32)
        ssq = jnp.sum(kc * kc, axis=-1, keepdims=True)
        kc_ref[g] = (kc * lax.rsqrt(ssq * (1.0 / HEAD_DIM) + EPS) * kgain_ref[...]).astype(BF16)
        vcT_ref[g] = lax.dot_general(w2vT_ref[...], av, (((1,), (1,)), ((), ())),
                                     preferred_element_type=F32).astype(BF16)


def _compress(kcv, wtop, wbot, posk, posv, w1k, w1v, w2k, w2vT, kgain):
    S = kcv.shape[0]
    nr = S // CMP_STRIDE
    r = kcv.reshape(nr, CMP_STRIDE * 2 * KV_W)
    vmem = pl.BlockSpec(memory_space=pltpu.VMEM)
    return pl.pallas_call(
        _compress_kernel,
        out_shape=(jax.ShapeDtypeStruct((N_KV, nr, LANES), BF16),
                   jax.ShapeDtypeStruct((N_KV, HEAD_DIM, nr), BF16)),
        in_specs=[vmem] * 10,
        out_specs=(vmem, vmem),
        compiler_params=pltpu.CompilerParams(vmem_limit_bytes=VMEM_LIMIT),
        name="compress",
    )(r, wtop, wbot, posk, posv, w1k, w1v, w2k, w2vT, kgain)


def _tile4(row):
    return jnp.concatenate([row] * GQA, axis=1)


def _attn_kernel(qT_ref, gT_ref, ks_ref, vsT_ref, kw_ref, vwT_ref, kc_ref, vcT_ref,
                 tw_ref, ts_ref, tc_ref, b31_ref, gain_ref, out_ref,
                 sc_ref, sw_ref, imp_ref, aoct_ref, aall_ref, qs_ref, sa_ref, sb_ref, qz_ref, qsb_ref, fin_ref, ocmp_ref):
    ncr = kc_ref.shape[1]
    nsel = aall_ref.shape[1] * BLOCKS_PER_CHUNK
    noct = aoct_ref.shape[1] - 1
    width = GQA * LANES
    c = pl.program_id(0)
    t0 = c * Q_BLOCK
    tl = lax.broadcasted_iota(I32, (1, LANES), 1)
    cur = 2 * c + (tl >= SEL_LEN).astype(I32)
    zb = jnp.maximum(2 * c - 2, 0)
    zs = pl.multiple_of(zb * SEL_LEN, Q_BLOCK)
    ts_off = pl.multiple_of(zs - (t0 - Q_BLOCK), Q_BLOCK)
    zeros_q = jnp.zeros((HEAD_DIM, width), BF16)

    col_ok = _tile4(t0 + tl) >= CMP_LEN - 1
    j = lax.broadcasted_iota(I32, (nsel, LANES), 0)
    forced = (j == 0) | (j == cur) | (j == cur - 1)
    @pl.when(c == 0)
    def _():
        imp_ref[...] = jnp.zeros(imp_ref.shape, F32)


    ws = pl.multiple_of(jnp.maximum(t0 - WINDOW, 0), Q_BLOCK)
    tw_off = pl.multiple_of(ws - (t0 - WINDOW), Q_BLOCK)
    nw = WINDOW + Q_BLOCK
    lo = pl.multiple_of(jnp.clip((8 * c - 16) // 16 * 16, 0, ncr - TC_WIN), 16)
    tc_off = pl.multiple_of(lo - (8 * c - 16) + TC_LEAD, 8)
    qgs = [jnp.concatenate([qT_ref[(GQA * g + r) * HEAD_DIM:(GQA * g + r + 1) * HEAD_DIM, :]
                            for r in range(GQA)], axis=1) for g in range(N_KV)]

    for g in range(N_KV):
        qw = jnp.concatenate([qgs[g], zeros_q] if g == 0 else [zeros_q, qgs[g]], axis=0)
        sw_ref[g] = (jnp.dot(kw_ref[pl.ds(ws, nw), :], qw, preferred_element_type=F32)
                     + tw_ref[g, pl.ds(tw_off, nw), :])

    def compressed(rows):
        for g in range(N_KV):
            qc = jnp.concatenate([qgs[g], zeros_q], axis=0)
            s = jnp.dot(kc_ref[g, 0:rows, :], qc, preferred_element_type=F32)
            row = lax.broadcasted_iota(I32, (rows, width), 0)
            sc_ref[g, 0:rows, :] = jnp.where(row < lo, s, NEG)
            s_loc = jnp.dot(kc_ref[g, pl.ds(lo, TC_WIN), :], qc, preferred_element_type=F32)
            sc_ref[g, pl.ds(lo, TC_WIN), :] = s_loc + tc_ref[g, pl.ds(tc_off, TC_WIN), :]
        for g in range(N_KV):
            s = sc_ref[g, 0:rows, :]
            m = jnp.max(s, axis=0, keepdims=True)
            e = jnp.exp2(s - m)
            l = jnp.sum(e, axis=0, keepdims=True)
            p = e * jnp.where(col_ok, 1.0 / jnp.maximum(l, 1e-30), 0.0)
            ocmp_ref[g] = jnp.dot(vcT_ref[g, :, 0:rows], p.astype(BF16), preferred_element_type=F32)
            imp_ref[g, 8:8 + rows, :] = (
                (p[:, 0:LANES] + p[:, LANES:2 * LANES]) + p[:, 2 * LANES:3 * LANES] + p[:, 3 * LANES:4 * LANES])

    sizes = list(range(CMP_STEP, ncr, CMP_STEP)) + [ncr]
    variant = jnp.minimum((lo + TC_WIN + CMP_STEP - 1) // CMP_STEP, len(sizes)) - 1
    for v, rows in enumerate(sizes):
        pl.when(variant == v)(functools.partial(compressed, rows))

    o_cmp, scores0 = [], []
    for g in range(N_KV):
        o_cmp.append(ocmp_ref[g])
        isel = jnp.zeros((nsel, LANES), F32)
        for off in (0, -1, 1, 0, 2, 1, 3, 2):
            isel = isel + imp_ref[g, pl.ds(8 + off, nsel, stride=4), :]
        score = jnp.where(forced, -jnp.inf, isel)
        scores0.append(jnp.where(j > cur, NEG, score))

    o_win = []
    for g in range(N_KV):
        sw = sw_ref[g]
        mw = jnp.max(sw, axis=0, keepdims=True)
        pw = jnp.exp2(sw - mw)
        accw = jnp.dot(vwT_ref[g, :, pl.ds(ws, nw)], pw.astype(BF16), preferred_element_type=F32)
        o_win.append(accw[0:HEAD_DIM] * (1.0 / jnp.maximum(accw[HEAD_DIM:HEAD_DIM + 1], 1e-30)))

    picks = list(scores0)
    for _ in range(SEL_TOPK - 3):
        for g in range(N_KV):
            mx = jnp.max(picks[g], axis=0, keepdims=True)
            first = jnp.min(jnp.where(picks[g] == mx, j, nsel), axis=0, keepdims=True)
            picks[g] = jnp.where(j == first, -jnp.inf, picks[g])

    m_init, acc_init, gates = [], [], []
    for g in range(N_KV):
        qg = qgs[g]
        b31row = b31_ref[g, 0:1, :]
        chosen = (picks[g] == -jnp.inf) & (j <= cur)
        a_all = _tile4(jnp.where(chosen, 0.0, -MASK_BIG)).reshape(noct, BLOCKS_PER_CHUNK, width)
        a_far = _tile4(jnp.where(chosen & (j < zb), 0.0, -MASK_BIG)).reshape(noct, BLOCKS_PER_CHUNK, width)
        b_hi = b31row.astype(BF16).astype(F32)
        erow = lax.broadcasted_iota(I32, (8, width), 0)
        extra = jnp.where(erow == 0, b_hi, jnp.where(erow == 1, b31row - b_hi, 0.0))
        aoct_ref[g, 0:noct] = jnp.concatenate(
            [a_far, jnp.broadcast_to(extra[None], (noct, 8, width))], axis=1).astype(BF16)
        aoct_ref[g, noct] = jnp.concatenate([jnp.full((8, width), -MASK_BIG, F32), extra], axis=0).astype(BF16)
        aall_ref[g] = jnp.concatenate([a_all, jnp.zeros((noct, 8, width), F32)], axis=1).astype(BF16)
        for q_ref in (qs_ref, qsb_ref):
            q_ref[g, 0:HEAD_DIM, :] = qg
            q_ref[g, HEAD_DIM + 16:LANES, :] = jnp.zeros((LANES - HEAD_DIM - 16, width), BF16)

        halves = []
        for h in range(2):
            qz_ref[2 * g + h, HEAD_DIM:HEAD_DIM + 16, :] = aall_ref[g, (zb + 2 * h) // BLOCKS_PER_CHUNK]
            qz_ref[2 * g + h, 0:HEAD_DIM, :] = qg
            qz_ref[2 * g + h, HEAD_DIM + 16:LANES, :] = jnp.zeros((LANES - HEAD_DIM - 16, width), BF16)
            sh = jnp.dot(ks_ref[g, pl.ds(zs + h * Q_BLOCK, Q_BLOCK), :], qz_ref[2 * g + h], preferred_element_type=F32)
            halves.append(sh + ts_ref[g, pl.ds(ts_off + h * Q_BLOCK, Q_BLOCK), :])
        sz = jnp.concatenate(halves, axis=0)
        mz = jnp.max(sz, axis=0, keepdims=True)
        pz = jnp.exp2(sz - mz)
        m_init.append(mz)
        acc_init.append(jnp.dot(vsT_ref[g, :, pl.ds(zs, 2 * Q_BLOCK)], pz.astype(BF16), preferred_element_type=F32))
        gates.append([jnp.concatenate([gT_ref[br * N_HEADS + GQA * g + r:br * N_HEADS + GQA * g + r + 1, :]
                                       for r in range(GQA)], axis=1) for br in range(N_BRANCH)])

    nfar = (zb + BLOCKS_PER_CHUNK - 1) // BLOCKS_PER_CHUNK

    nsub = 1
    sub = CHUNK // nsub

    def score(g, u, buf_ref, h, q_ref):
        k0 = pl.multiple_of(jnp.minimum(u, noct - 1) * CHUNK + h * sub, sub)
        s = jnp.dot(ks_ref[g, pl.ds(k0, sub), :], q_ref[g], preferred_element_type=F32)
        buf_ref[g, h * sub:(h + 1) * sub, :] = s
        return jnp.max(s, axis=0, keepdims=True)

    def set_mask_rows(g, u, q_ref):
        oct_id = jnp.where(u < nfar, u, noct)
        q_ref[g, HEAD_DIM:HEAD_DIM + 16, :] = aoct_ref[g, oct_id]

    def accumulate(g, u, buf_ref, h, mn, acc):
        k0 = pl.multiple_of(jnp.minimum(u, noct - 1) * CHUNK + h * sub, sub)
        p = jnp.exp2(buf_ref[g, h * sub:(h + 1) * sub, :] - mn)
        return acc + jnp.dot(vsT_ref[g, :, pl.ds(k0, sub)], p.astype(BF16), preferred_element_type=F32)

    def step(u, cur_ref, nxt_ref, q_ref, state, cms):
        out_state, out_cms = [], []
        for g in range(N_KV):
            m, acc = state[2 * g], state[2 * g + 1]
            mn = jnp.maximum(m, cms[g])
            acc = jnp.exp2(m - mn) * acc
            set_mask_rows(g, u + 1, q_ref)
            cm = None
            for h in range(nsub):
                ch = score(g, u + 1, nxt_ref, h, q_ref)
                cm = ch if cm is None else jnp.maximum(cm, ch)
                acc = accumulate(g, u, cur_ref, h, mn, acc)
            out_cms.append(cm)
            out_state += [mn, acc]
        return out_state, out_cms

    def far(i, carry):
        state, cm_b = step(2 * i, sa_ref, sb_ref, qsb_ref, carry[0:4], carry[4:6])
        state, cm_a = step(2 * i + 1, sb_ref, sa_ref, qs_ref, state, cm_b)
        return tuple(state + cm_a)

    first = []
    for g in range(N_KV):
        set_mask_rows(g, 0, qs_ref)
        cm = None
        for h in range(nsub):
            ch = score(g, 0, sa_ref, h, qs_ref)
            cm = ch if cm is None else jnp.maximum(cm, ch)
        first.append(cm)
    nchunks = jnp.maximum(nfar, 1)
    npairs = (nchunks - 1) // 2
    carry = lax.fori_loop(0, npairs, far, (m_init[0], acc_init[0], m_init[1], acc_init[1], first[0], first[1]))
    u_last = 2 * npairs

    def finish(two_left):
        for g in range(N_KV):
            m, acc, cm = carry[2 * g], carry[2 * g + 1], carry[4 + g]
            mn = jnp.maximum(m, cm)
            acc = jnp.exp2(m - mn) * acc
            if two_left:
                set_mask_rows(g, u_last + 1, qsb_ref)
                cm_b = score(g, u_last + 1, sb_ref, 0, qsb_ref)
            acc = accumulate(g, u_last, sa_ref, 0, mn, acc)
            if two_left:
                mn_b = jnp.maximum(mn, cm_b)
                acc = accumulate(g, u_last + 1, sb_ref, 0, mn_b, jnp.exp2(mn - mn_b) * acc)
            fin_ref[g] = acc

    pl.when(nchunks - u_last == 1)(lambda: finish(False))
    pl.when(nchunks - u_last == 2)(lambda: finish(True))
    carry = [None, fin_ref[0], None, fin_ref[1]]

    o_all, ssq = [], jnp.zeros((1, LANES), F32)
    for g in range(N_KV):
        acc = carry[2 * g + 1]
        o_sel = acc[0:HEAD_DIM] * (1.0 / jnp.maximum(acc[HEAD_DIM:HEAD_DIM + 1], 1e-30))
        o = gates[g][0] * o_cmp[g] + gates[g][1] * o_sel + gates[g][2] * o_win[g]
        o_all.append(o)
        cs = jnp.sum(o * o, axis=0, keepdims=True)
        ssq = ssq + ((cs[:, 0:LANES] + cs[:, LANES:2 * LANES]) + (cs[:, 2 * LANES:3 * LANES] + cs[:, 3 * LANES:]))
    inv = _tile4(lax.rsqrt(ssq * (1.0 / ATTN_W) + EPS))
    for g in range(N_KV):
        on = o_all[g] * inv * gain_ref[g]
        for k in range(GQA // 2):
            pair = jnp.concatenate([on[:, (2 * k) * LANES:(2 * k + 1) * LANES],
                                    on[:, (2 * k + 1) * LANES:(2 * k + 2) * LANES]], axis=0)
            col = (GQA * g + 2 * k) * HEAD_DIM
            out_ref[:, col:col + 2 * HEAD_DIM] = pair.T.astype(BF16)


def _attention(qT, gT, ks, vsT, kw, vwT, kc, vcT, tables, gain_b):
    S = qT.shape[1]
    nq = S // Q_BLOCK
    ncr = kc.shape[1]
    nsel = S // SEL_LEN
    noct = S // CHUNK
    width = GQA * LANES
    tw, ts, tc, b31 = tables
    vmem = pl.BlockSpec(memory_space=pltpu.VMEM)
    in_specs = [
        pl.BlockSpec((ATTN_W, Q_BLOCK), lambda c: (0, c)),
        pl.BlockSpec((N_BRANCH * N_HEADS, Q_BLOCK), lambda c: (0, c)),
    ] + [vmem] * 11
    return pl.pallas_call(
        _attn_kernel,
        grid=(nq,),
        in_specs=in_specs,
        out_specs=pl.BlockSpec((Q_BLOCK, ATTN_W), lambda c: (c, 0)),
        out_shape=jax.ShapeDtypeStruct((S, ATTN_W), BF16),
        scratch_shapes=[
            pltpu.VMEM((N_KV, ncr, width), F32),
            pltpu.VMEM((N_KV, WINDOW + Q_BLOCK, width), F32),
            pltpu.VMEM((N_KV, ncr + 16, LANES), F32),
            pltpu.VMEM((N_KV, noct + 1, 16, width), BF16),
            pltpu.VMEM((N_KV, noct, 16, width), BF16),
            pltpu.VMEM((N_KV, LANES, width), BF16),
            pltpu.VMEM((N_KV, CHUNK, width), F32),
            pltpu.VMEM((N_KV, CHUNK, width), F32),
            pltpu.VMEM((2 * N_KV, LANES, width), BF16),
            pltpu.VMEM((N_KV, LANES, width), BF16),
            pltpu.VMEM((N_KV, V_ROWS, width), F32),
            pltpu.VMEM((N_KV, HEAD_DIM, width), F32),
        ],
        compiler_params=pltpu.CompilerParams(dimension_semantics=("arbitrary",), vmem_limit_bytes=VMEM_LIMIT),
        name="nsa_attention",
    )(qT, gT, ks, vsT, kw, vwT, kc, vcT, tw, ts, tc, b31, gain_b)


def _mix_and_norm(x_ref, attn_ref, conv_ref, wout_ref, g2_ref):
    x1 = x_ref[...] + jnp.dot(attn_ref[...], wout_ref[0:ATTN_W, :], preferred_element_type=F32) \
        + jnp.dot(conv_ref[...], wout_ref[ATTN_W:, :], preferred_element_type=F32)
    ms = jnp.mean(x1 * x1, axis=-1, keepdims=True)
    h2 = x1 * lax.rsqrt(ms + EPS) * g2_ref[...]
    return x1, h2


def _ffn_kernel(x_ref, attn_ref, conv_ref, wout_ref, g2_ref, wg_ref, wu_ref, wd_ref, out_ref):
    x1, h2 = _mix_and_norm(x_ref, attn_ref, conv_ref, wout_ref, g2_ref)
    h2 = h2.astype(BF16)
    a = jnp.dot(h2, wg_ref[...], preferred_element_type=F32)
    u = jnp.dot(h2, wu_ref[...], preferred_element_type=F32)
    y = (a * jax.nn.sigmoid(a) * u).astype(BF16)
    out_ref[...] = x1 + jnp.dot(y, wd_ref[...], preferred_element_type=F32)


_FFN_TM = 512


def _outproj_ffn(x2, attn_n, conv_n, wout, g2, wg, wu, wd):
    S = x2.shape[0]
    tm = _FFN_TM
    resident = pl.BlockSpec(memory_space=pltpu.VMEM)
    return pl.pallas_call(
        _ffn_kernel,
        grid=(S // tm,),
        in_specs=[
            pl.BlockSpec((tm, D_MODEL), lambda i: (i, 0)),
            pl.BlockSpec((tm, ATTN_W), lambda i: (i, 0)),
            pl.BlockSpec((tm, CONV_W), lambda i: (i, 0)),
            resident, resident, resident, resident, resident,
        ],
        out_specs=pl.BlockSpec((tm, D_MODEL), lambda i: (i, 0)),
        out_shape=jax.ShapeDtypeStruct((S, D_MODEL), F32),
        compiler_params=pltpu.CompilerParams(dimension_semantics=("arbitrary",), vmem_limit_bytes=VMEM_LIMIT),
        name="outproj_ffn",
    )(x2, attn_n, conv_n, wout, g2, wg, wu, wd)


TOKEN_TILE = (D_MODEL // LANES, LANES)


def _to_token_tiles(ref, rows):
    x = jnp.stack([rows[:, k * LANES:(k + 1) * LANES] for k in range(TOKEN_TILE[0])], axis=0)
    ref[...] = pltpu.einshape('ktl->tkl', x)


def _from_token_tiles(ref):
    x = pltpu.einshape('tkl->ktl', ref[...])
    return jnp.concatenate([x[k] for k in range(TOKEN_TILE[0])], axis=1)


def _router_kernel(x_ref, attn_ref, conv_ref, wout_ref, g2_ref, rw_ref, rb_ref, tri_ref,
                   x1_ref, h2_ref, route_ref, cnt_ref, run_ref):
    tm = x_ref.shape[0]
    i = pl.program_id(0)

    @pl.when(i == 0)
    def _():
        run_ref[...] = jnp.zeros(run_ref.shape, F32)

    x1, h2 = _mix_and_norm(x_ref, attn_ref, conv_ref, wout_ref, g2_ref)
    x1_ref[...] = x1
    h2b = h2.astype(BF16)
    _to_token_tiles(h2_ref, h2b.astype(F32))
    lane = lax.broadcasted_iota(I32, (tm, LANES), 1)
    logits = jnp.dot(h2b, rw_ref[...], preferred_element_type=F32) + rb_ref[...]
    logits = jnp.where(lane < N_EXPERTS, logits, -jnp.inf)
    m1 = jnp.max(logits, axis=-1, keepdims=True)
    i1 = jnp.min(jnp.where(logits == m1, lane, LANES), axis=-1, keepdims=True)
    rest = jnp.where(lane == i1, -jnp.inf, logits)
    m2 = jnp.max(rest, axis=-1, keepdims=True)
    i2 = jnp.min(jnp.where(rest == m2, lane, LANES), axis=-1, keepdims=True)
    e2 = jnp.exp(m2 - m1)
    den = 1.0 + e2
    oh1 = (lane == i1).astype(F32)
    oh2 = (lane == i2).astype(F32)
    both = oh1 + oh2
    before = run_ref[0:1, :] + jnp.dot(tri_ref[...], both.astype(BF16), preferred_element_type=F32)
    rank1 = jnp.sum(before * oh1, axis=-1, keepdims=True)
    rank2 = jnp.sum(before * oh2, axis=-1, keepdims=True)
    fields = (i1.astype(F32), i2.astype(F32), rank1, rank2, 1.0 / den, e2 / den)
    route = jnp.zeros((tm, LANES), F32)
    for k, v in enumerate(fields):
        route = jnp.where(lane == k, v, route)
    route_ref[...] = route
    run_ref[...] = run_ref[...] + jnp.sum(both, axis=0, keepdims=True)
    cnt_ref[...] = run_ref[...]


def _outproj_router(x2, attn_n, conv_n, wout, g2, rw, rb):
    S = x2.shape[0]
    tm = _FFN_TM
    tri = np.tril(np.ones((tm, tm), np.float32), -1)
    const = lambda shape: pl.BlockSpec(shape, lambda i: (0,) * len(shape))
    return pl.pallas_call(
        _router_kernel,
        grid=(S // tm,),
        in_specs=[
            pl.BlockSpec((tm, D_MODEL), lambda i: (i, 0)),
            pl.BlockSpec((tm, ATTN_W), lambda i: (i, 0)),
            pl.BlockSpec((tm, CONV_W), lambda i: (i, 0)),
            const(wout.shape), const((1, D_MODEL)), const(rw.shape), const((1, LANES)), const((tm, tm)),
        ],
        out_specs=(pl.BlockSpec((tm, D_MODEL), lambda i: (i, 0)),
                   pl.BlockSpec((tm,) + TOKEN_TILE, lambda i: (i, 0, 0)),
                   pl.BlockSpec((tm, LANES), lambda i: (i, 0)),
                   const((8, LANES))),
        out_shape=(jax.ShapeDtypeStruct((S, D_MODEL), F32),
                   jax.ShapeDtypeStruct((S,) + TOKEN_TILE, F32),
                   jax.ShapeDtypeStruct((S, LANES), F32),
                   jax.ShapeDtypeStruct((8, LANES), F32)),
        scratch_shapes=[pltpu.VMEM((8, LANES), F32)],
        compiler_params=pltpu.CompilerParams(dimension_semantics=("arbitrary",), vmem_limit_bytes=VMEM_LIMIT),
        name="outproj_router",
    )(x2, attn_n, conv_n, wout, g2, rw, rb, jnp.asarray(tri, BF16))


_ROW_TM = 256
_EXP_TM = 512
_DMA_UNROLL = 8


def _row_copy(src_ref, src_row, dst_ref, dst_row, sem):
    return pltpu.make_async_copy(src_ref.at[src_row], dst_ref.at[dst_row], sem)


def _dispatch_kernel(pos_ref, h_ref, zero_ref, xs_ref, hbuf_ref, sem):
    del zero_ref
    tm = h_ref.shape[0]
    i = pl.program_id(0)
    slot = i % 2
    hbuf_ref[slot] = h_ref[...]

    def issue(t, carry):
        for k in range(2):
            _row_copy(hbuf_ref.at[slot], t, xs_ref, pos_ref[0, 0, 2 * t + k], sem.at[slot]).start(priority=k)
        return carry

    lax.fori_loop(0, tm, issue, 0, unroll=_DMA_UNROLL)

    def drain(which):
        def body(t, carry):
            for k in range(2):
                _row_copy(hbuf_ref.at[which], 0, xs_ref, 0, sem.at[which]).wait()
            return carry
        lax.fori_loop(0, tm, body, 0, unroll=_DMA_UNROLL)

    @pl.when(i > 0)
    def _():
        drain(1 - slot)

    @pl.when(i == pl.num_programs(0) - 1)
    def _():
        drain(slot)


def _dispatch(pos3, h2, n_rows):
    S = h2.shape[0]
    tm = _ROW_TM
    zeros = jnp.zeros((n_rows,) + TOKEN_TILE, F32)
    return pl.pallas_call(
        _dispatch_kernel,
        grid=(S // tm,),
        in_specs=[
            pl.BlockSpec((1, 1, 2 * tm), lambda i: (i, 0, 0), memory_space=pltpu.SMEM),
            pl.BlockSpec((tm,) + TOKEN_TILE, lambda i: (i, 0, 0)),
            pl.BlockSpec(memory_space=pl.ANY),
        ],
        out_specs=pl.BlockSpec(memory_space=pl.ANY),
        out_shape=jax.ShapeDtypeStruct((n_rows,) + TOKEN_TILE, F32),
        scratch_shapes=[pltpu.VMEM((2, tm) + TOKEN_TILE, F32), pltpu.SemaphoreType.DMA((2,))],
        input_output_aliases={2: 0},
        compiler_params=pltpu.CompilerParams(dimension_semantics=("arbitrary",), vmem_limit_bytes=VMEM_LIMIT),
        name="moe_dispatch",
    )(pos3, h2, zeros)


def _experts_kernel(te_ref, tb_ref, nt_ref, xs_ref, wg_ref, wu_ref, wd_ref, ys_ref):
    i = pl.program_id(0)

    @pl.when(i < nt_ref[0])
    def _():
        x = _from_token_tiles(xs_ref).astype(BF16)
        a = jnp.dot(x, wg_ref[0], preferred_element_type=F32)
        u = jnp.dot(x, wu_ref[0], preferred_element_type=F32)
        y = (a * jax.nn.sigmoid(a) * u).astype(BF16)
        _to_token_tiles(ys_ref, jnp.dot(y, wd_ref[0], preferred_element_type=F32))

    @pl.when(i >= nt_ref[0])
    def _():
        ys_ref[...] = jnp.zeros(ys_ref.shape, F32)


def _experts(tile_e, tile_b, n_tiles, xs, wg, wu, wd):
    n_rows = xs.shape[0]
    tm = _EXP_TM
    weights = lambda shape: pl.BlockSpec(shape, lambda i, te, tb, nt: (te[i], 0, 0), pipeline_mode=pl.Buffered(1))
    grid_spec = pltpu.PrefetchScalarGridSpec(
        num_scalar_prefetch=3,
        grid=(n_rows // tm,),
        in_specs=[
            pl.BlockSpec((tm,) + TOKEN_TILE, lambda i, te, tb, nt: (tb[i], 0, 0)),
            weights((1, D_MODEL, D_FF)), weights((1, D_MODEL, D_FF)), weights((1, D_FF, D_MODEL)),
        ],
        out_specs=pl.BlockSpec((tm,) + TOKEN_TILE, lambda i, te, tb, nt: (tb[i], 0, 0)),
    )
    return pl.pallas_call(
        _experts_kernel,
        grid_spec=grid_spec,
        out_shape=jax.ShapeDtypeStruct((n_rows,) + TOKEN_TILE, F32),
        compiler_params=pltpu.CompilerParams(dimension_semantics=("arbitrary",), vmem_limit_bytes=VMEM_LIMIT),
        name="moe_experts",
    )(tile_e, tile_b, n_tiles, xs, wg, wu, wd)


def _combine_kernel(pos_ref, nxt_ref, x1_ref, route_ref, ys_ref, out_ref, y1_ref, y2_ref, sem):
    tm = x1_ref.shape[0]
    i = pl.program_id(0)
    slot = i % 2

    def issue(src_pos_ref, which):
        def body(t, carry):
            _row_copy(ys_ref, src_pos_ref[0, 0, 2 * t], y1_ref.at[which], t, sem.at[which]).start(priority=0)
            _row_copy(ys_ref, src_pos_ref[0, 0, 2 * t + 1], y2_ref.at[which], t, sem.at[which]).start(priority=1)
            return carry
        lax.fori_loop(0, tm, body, 0, unroll=_DMA_UNROLL)

    @pl.when(i == 0)
    def _():
        issue(pos_ref, slot)

    @pl.when(i + 1 < pl.num_programs(0))
    def _():
        issue(nxt_ref, 1 - slot)

    def drain(t, carry):
        _row_copy(ys_ref, 0, y1_ref.at[slot], 0, sem.at[slot]).wait()
        _row_copy(ys_ref, 0, y2_ref.at[slot], 0, sem.at[slot]).wait()
        return carry

    lax.fori_loop(0, tm, drain, 0, unroll=_DMA_UNROLL)
    lane = lax.broadcasted_iota(I32, (tm, LANES), 1)
    route = route_ref[...]
    w1 = jnp.sum(jnp.where(lane == 4, route, 0.0), axis=-1, keepdims=True)
    w2 = jnp.sum(jnp.where(lane == 5, route, 0.0), axis=-1, keepdims=True)
    out_ref[...] = x1_ref[...] + (_from_token_tiles(y1_ref.at[slot]) * w1 + _from_token_tiles(y2_ref.at[slot]) * w2)


def _combine(pos3, x1, route, ys):
    S = x1.shape[0]
    tm = _ROW_TM
    last = S // tm - 1
    return pl.pallas_call(
        _combine_kernel,
        grid=(S // tm,),
        in_specs=[
            pl.BlockSpec((1, 1, 2 * tm), lambda i: (i, 0, 0), memory_space=pltpu.SMEM),
            pl.BlockSpec((1, 1, 2 * tm), lambda i: (jnp.minimum(i + 1, last), 0, 0), memory_space=pltpu.SMEM),
            pl.BlockSpec((tm, D_MODEL), lambda i: (i, 0)),
            pl.BlockSpec((tm, LANES), lambda i: (i, 0)),
            pl.BlockSpec(memory_space=pl.ANY),
        ],
        out_specs=pl.BlockSpec((tm, D_MODEL), lambda i: (i, 0)),
        out_shape=jax.ShapeDtypeStruct((S, D_MODEL), F32),
        scratch_shapes=[pltpu.VMEM((2, tm) + TOKEN_TILE, F32), pltpu.VMEM((2, tm) + TOKEN_TILE, F32),
                        pltpu.SemaphoreType.DMA((2,))],
        compiler_params=pltpu.CompilerParams(dimension_semantics=("arbitrary",), vmem_limit_bytes=VMEM_LIMIT),
        name="moe_combine",
    )(pos3, pos3, x1, route, ys)


def _moe(h2, x1, route, counts, wg, wu, wd):
    S = h2.shape[0]
    tm = _EXP_TM
    n_tiles_max = 2 * S // tm + N_EXPERTS
    cnt = counts[0, 0:N_EXPERTS].astype(I32)
    tiles = (cnt + tm - 1) // tm
    first = jnp.cumsum(tiles) - tiles
    n_tiles = jnp.sum(tiles)
    idx = jnp.arange(n_tiles_max, dtype=I32)
    last = jnp.minimum(idx, n_tiles - 1)
    tile_e = (jnp.sum(last[:, None] >= first[None, :], axis=1) - 1).astype(I32)
    eid = route[:, 0:2].astype(I32)
    pos = first[eid] * tm + route[:, 2:4].astype(I32)
    pos3 = pos.reshape(S // _ROW_TM, 1, 2 * _ROW_TM)
    xs = _dispatch(pos3, h2, n_tiles_max * tm)
    ys = _experts(tile_e, idx, n_tiles.reshape(1), xs, wg, wu, wd)
    return _combine(pos3, x1, route, ys)


def _split_w_in(w):
    o = np.cumsum([0, ATTN_W] + [KV_W] * 6 + [N_BRANCH * N_HEADS] + [CONV_W] * 3)
    q, kc, vc, ksl, vsl, kwn, vwn, gts, cb, cc, ch = (w[:, o[i]:o[i + 1]] for i in range(11))
    perm = np.array([h * N_BRANCH + br for br in range(N_BRANCH) for h in range(N_HEADS)])
    wtok = jnp.concatenate([kc, vc, ksl, kwn, cb, cc, ch], axis=1).astype(BF16)
    feat = jnp.concatenate([q, vsl, vwn, gts[:, perm], jnp.zeros((D_MODEL, 8), w.dtype)], axis=1)
    return wtok, feat.T.astype(BF16)


def _expand_cmp_w1(w1k, w1v):
    kinds = jnp.stack([w1k, w1v]).reshape(2, CMP_LEN, HEAD_DIM, CMP_HIDDEN)
    per_col = jnp.repeat(kinds, N_KV, axis=0)
    eye = jnp.eye(2 * N_KV, dtype=F32)
    out = []
    for l0 in (0, CMP_STRIDE):
        w = per_col[:, l0:l0 + CMP_STRIDE].transpose(1, 0, 2, 3)
        blk = w[:, :, :, None, :] * eye[None, :, None, :, None]
        out.append(blk.reshape(CMP_STRIDE * 2 * KV_W, 2 * N_KV * CMP_HIDDEN).astype(BF16))
    return out


def kernel(x, rel_bias, norm1, w_in, q_norm, k_norm, cmp_pos_k, cmp_pos_v, cmp_k_w1, cmp_k_w2, cmp_v_w1, cmp_v_w2,
           conv_w, attn_out_norm, conv_out_norm, w_out, norm2, ffn_w_gate, ffn_w_up, ffn_w_down, router_w, router_b,
           moe_w_gate, moe_w_up, moe_w_down):
    B, S, _ = x.shape
    assert B == 1 and S % CHUNK == 0 and S >= WINDOW + Q_BLOCK
    depth = norm1.shape[0]
    x2 = x.reshape(S, D_MODEL)
    tables = _bias_tables(rel_bias)
    for layer in range(depth):
        wtok, wfeat = _split_w_in(w_in[layer])
        kgain = jnp.concatenate([jnp.tile(k_norm[layer, 1], N_KV), jnp.tile(k_norm[layer, 2], N_KV)])[None, :]
        gq = jnp.broadcast_to(q_norm[layer][:, None], (HEAD_DIM, _IN_TM))
        convw = jnp.pad(conv_w[layer], ((0, 8 - CONV_K), (0, 0)))
        qT, gT, vsT, vwT, ks, kw, kcv, conv_n = _in_proj(
            x2, norm1[layer][None, :], wtok, wfeat, kgain, gq, convw, conv_out_norm[layer][None, :])

        wtop, wbot = _expand_cmp_w1(cmp_k_w1[layer], cmp_v_w1[layer])
        posk = jnp.broadcast_to(cmp_pos_k[layer].reshape(1, -1), (8, CMP_LEN * HEAD_DIM))
        posv = jnp.broadcast_to(cmp_pos_v[layer].reshape(1, -1), (8, CMP_LEN * HEAD_DIM))
        w2k = jnp.pad(cmp_k_w2[layer], ((0, 0), (0, LANES - HEAD_DIM))).astype(BF16)
        w2vT = cmp_v_w2[layer].T.astype(BF16)
        kcgain = jnp.pad(k_norm[layer, 0], (0, LANES - HEAD_DIM))[None, :]
        kc, vcT = _compress(kcv, wtop, wbot, posk, posv, cmp_k_w1[layer], cmp_v_w1[layer], w2k, w2vT, kcgain)

        gain_b = jnp.broadcast_to(attn_out_norm[layer].reshape(N_KV, GQA, HEAD_DIM).transpose(0, 2, 1)[:, :, :, None],
                                  (N_KV, HEAD_DIM, GQA, LANES)).reshape(N_KV, HEAD_DIM, GQA * LANES)
        attn_n = _attention(qT, gT, ks, vsT, kw, vwT, kc, vcT, tables, gain_b)

        wout = w_out[layer].astype(BF16)
        g2 = norm2[layer][None, :]
        i = layer // 2
        if layer % 2 == 0:
            x2 = _outproj_ffn(x2, attn_n, conv_n, wout, g2, ffn_w_gate[i].astype(BF16), ffn_w_up[i].astype(BF16),
                              ffn_w_down[i].astype(BF16))
        else:
            rw = jnp.pad(router_w[i], ((0, 0), (0, LANES - N_EXPERTS))).astype(BF16)
            rb = jnp.pad(router_b[i], (0, LANES - N_EXPERTS))[None, :]
            x1, h2, route, counts = _outproj_router(x2, attn_n, conv_n, wout, g2, rw, rb)
            x2 = _moe(h2, x1, route, counts, moe_w_gate[i].astype(BF16), moe_w_up[i].astype(BF16),
                      moe_w_down[i].astype(BF16))
    return x2.reshape(B, S, D_MODEL)
```

```python
import functools
import math

import numpy as np
import jax
import jax.numpy as jnp
from jax import lax
from jax.experimental import pallas as pl
from jax.experimental.pallas import tpu as pltpu

F32 = jnp.float32
BF16 = jnp.bfloat16
I32 = jnp.int32

D_MODEL = 1024
HEAD_DIM = 64
N_HEADS = 8
N_KV = 2
GQA = N_HEADS // N_KV
ATTN_W = N_HEADS * HEAD_DIM
KV_W = N_KV * HEAD_DIM
N_BRANCH = 3
CONV_W = 512
CONV_K = 3
CMP_LEN = 32
CMP_STRIDE = 16
CMP_HIDDEN = 128
SEL_LEN = 64
SEL_TOPK = 16
WINDOW = 512
Q_BLOCK = 128
N_BUCKETS = 32
MAX_DISTANCE = 128
D_FF = 2816
N_EXPERTS = 8
EPS = 1e-6
NEG = -1e30
FORCE = 1e9
MASK_BIG = 2.0 ** 60
SCALE = HEAD_DIM ** -0.5
LOG2E = math.log2(math.e)

LANES = 128
V_ROWS = 80
CHUNK = 512
BLOCKS_PER_CHUNK = CHUNK // SEL_LEN
TC_LEAD = 24
TC_ROWS = 88
TC_WIN = 48
CMP_STEP = 256
VMEM_LIMIT = 56 * 1024 * 1024


def _bucket_np(dist):
    n = np.maximum(dist, 0)
    max_exact = N_BUCKETS // 2
    nf = np.maximum(n, max_exact).astype(np.float64)
    v = np.log(nf / max_exact) / math.log(MAX_DISTANCE / max_exact) * (N_BUCKETS - max_exact)
    frac = np.abs(v - np.round(v))
    assert np.all((frac > 1e-6) | (n <= max_exact) | (n >= MAX_DISTANCE)), "bucket boundary is precision dependent"
    large = np.minimum(max_exact + (v + 1e-9).astype(np.int32), N_BUCKETS - 1)
    return np.where(n < max_exact, n, large).astype(np.int32)


def _index_tables():
    tl = np.arange(Q_BLOCK)[None, :]
    r = np.arange(WINDOW + Q_BLOCK + WINDOW)[:, None]
    d = tl + WINDOW - r
    idx_w = np.where((d >= 0) & (d < WINDOW), _bucket_np(d), -1)
    r = np.arange(3 * Q_BLOCK)[:, None]
    d = tl + Q_BLOCK - r
    idx_s = np.where(d >= 0, _bucket_np(d), -1)
    r = np.arange(TC_ROWS)[:, None] - TC_LEAD
    d = tl - CMP_STRIDE * r + (CMP_STRIDE * 16 - (CMP_LEN - 1))
    idx_c = np.where((d >= 0) & (r < 32), _bucket_np(d), -1)
    return idx_w.astype(np.int32), idx_s.astype(np.int32), idx_c.astype(np.int32)


def _tables_kernel(rb_ref, iw_ref, is_ref, ic_ref, tw_ref, ts_ref, tc_ref, b31_ref):
    for h in range(N_HEADS):
        g, r = divmod(h, GQA)
        lanes = slice(r * LANES, (r + 1) * LANES)
        for idx_ref, out_ref in ((iw_ref, tw_ref), (is_ref, ts_ref), (ic_ref, tc_ref)):
            out_ref[g, :, lanes] = jnp.full(idx_ref.shape, NEG, F32)

            def body(b, carry, idx_ref=idx_ref, out_ref=out_ref, g=g, lanes=lanes, h=h):
                out_ref[g, :, lanes] = jnp.where(idx_ref[...] == b, rb_ref[b, h] * LOG2E, out_ref[g, :, lanes])
                return carry

            lax.fori_loop(0, N_BUCKETS, body, 0)
        far_bias = rb_ref[N_BUCKETS - 1, h] * LOG2E
        tc_ref[g, :, lanes] = jnp.where(ic_ref[...] >= 0, tc_ref[g, :, lanes] - far_bias, NEG)
        b31_ref[g, :, lanes] = jnp.full((8, LANES), far_bias, F32)


def _bias_tables(rel_bias):
    idx_w, idx_s, idx_c = _index_tables()
    width = GQA * LANES
    out_shape = (
        jax.ShapeDtypeStruct((N_KV, idx_w.shape[0], width), F32),
        jax.ShapeDtypeStruct((N_KV, idx_s.shape[0], width), F32),
        jax.ShapeDtypeStruct((N_KV, idx_c.shape[0], width), F32),
        jax.ShapeDtypeStruct((N_KV, 8, width), F32),
    )
    vmem = pl.BlockSpec(memory_space=pltpu.VMEM)
    return pl.pallas_call(
        _tables_kernel,
        out_shape=out_shape,
        in_specs=[pl.BlockSpec(memory_space=pltpu.SMEM), vmem, vmem, vmem],
        out_specs=(vmem, vmem, vmem, vmem),
        name="bias_tables",
    )(rel_bias, jnp.asarray(idx_w), jnp.asarray(idx_s), jnp.asarray(idx_c))


def _in_proj_kernel(x_ref, g1_ref, wtok_ref, wfeat_ref, ind_ref, kgain_ref, aug_ref, gq_ref, convw_ref, cgain_ref,
                    qT_ref, gT_ref, vsT_ref, vwT_ref, ks_ref, kw_ref, kcv_ref, convn_ref, zs_ref, carry_ref):
    tm = _IN_TM
    i = pl.program_id(0)

    @pl.when(i == 0)
    def _():
        carry_ref[...] = jnp.zeros(carry_ref.shape, F32)

    prev_tail = carry_ref[...]
    for part in range(_IN_SPLIT):
        rows = slice(part * tm, (part + 1) * tm)
        x = x_ref[rows, :]
        ms = jnp.mean(x * x, axis=-1, keepdims=True)
        h = (x * lax.rsqrt(ms + EPS) * g1_ref[...]).astype(BF16)
        tok = jnp.dot(h, wtok_ref[...], preferred_element_type=F32)
        feat = lax.dot_general(wfeat_ref[...], h, (((1,), (1,)), ((), ())),
                               preferred_element_type=F32)

        kcv_ref[rows, :] = tok[:, 0:2 * KV_W].astype(BF16)
        kk = tok[:, 2 * KV_W:4 * KV_W]
        sq = kk * kk
        sq_hi = sq.astype(BF16)
        sq_lo = (sq - sq_hi.astype(F32)).astype(BF16)
        ssq = jnp.dot(jnp.concatenate([sq_hi, sq_lo], axis=1), ind_ref[...],
                      preferred_element_type=F32)
        kn = kk * lax.rsqrt(ssq * (1.0 / HEAD_DIM) + EPS) * kgain_ref[...]
        ksl = kn[:, 0:KV_W]
        lane = lax.broadcasted_iota(I32, (tm, LANES), 1)
        aug = aug_ref[...]
        ks_ref[0, rows, :] = jnp.where(lane < HEAD_DIM, ksl, aug).astype(BF16)
        ks_ref[1, rows, :] = jnp.where(lane < HEAD_DIM, pltpu.roll(ksl, HEAD_DIM, 1), aug).astype(BF16)
        kw_ref[rows, :] = kn[:, KV_W:2 * KV_W].astype(BF16)

        c0 = 4 * KV_W
        cb = tok[:, c0:c0 + CONV_W]
        cc = tok[:, c0 + CONV_W:c0 + 2 * CONV_W]
        ch = tok[:, c0 + 2 * CONV_W:c0 + 3 * CONV_W]
        z = cc * ch
        zs_ref[part, 0:8, :] = prev_tail
        zs_ref[part, 8:8 + tm, :] = z
        z1 = zs_ref[part, 7:7 + tm, :]
        z2 = zs_ref[part, 6:6 + tm, :]
        w = convw_ref[...]
        y = w[0:1, :] * z2 + w[1:2, :] * z1 + w[2:3, :] * z
        prev_tail = z[tm - 8:tm, :]
        oc = cb * y
        msc = jnp.mean(oc * oc, axis=-1, keepdims=True)
        convn_ref[rows, :] = (oc * lax.rsqrt(msc + EPS) * cgain_ref[...]).astype(BF16)

        q = feat[0:ATTN_W].reshape(N_HEADS, HEAD_DIM, tm)
        qss = jnp.sum(q * q, axis=1, keepdims=True)
        qn = q * lax.rsqrt(qss * (1.0 / HEAD_DIM) + EPS) * gq_ref[...][None]
        qT_ref[:, rows] = (qn * (SCALE * LOG2E)).reshape(ATTN_W, tm).astype(BF16)
        ones_rows = (lax.broadcasted_iota(I32, (V_ROWS - HEAD_DIM, tm), 0) == 0).astype(BF16)
        for g in range(N_KV):
            r0 = ATTN_W + g * HEAD_DIM
            vsT_ref[g, 0:HEAD_DIM, rows] = feat[r0:r0 + HEAD_DIM].astype(BF16)
            vsT_ref[g, HEAD_DIM:V_ROWS, rows] = ones_rows
            r1 = ATTN_W + KV_W + g * HEAD_DIM
            vwT_ref[g, 0:HEAD_DIM, rows] = feat[r1:r1 + HEAD_DIM].astype(BF16)
            vwT_ref[g, HEAD_DIM:V_ROWS, rows] = ones_rows
        g0 = ATTN_W + 2 * KV_W
        gT_ref[:, rows] = jax.nn.sigmoid(feat[g0:g0 + N_BRANCH * N_HEADS])
    carry_ref[...] = prev_tail


_IN_SPLIT = 2
_IN_TM = 512


def _in_proj(x2, g1, wtok, wfeat, kgain, gq, convw, cgain):
    S = x2.shape[0]
    tm = _IN_TM
    tb = _IN_SPLIT * tm
    nt = S // tb
    ind = np.kron(np.eye(2 * N_KV, dtype=np.float32), np.ones((HEAD_DIM, HEAD_DIM), np.float32))
    ind = np.concatenate([ind, ind], axis=0)
    aug = np.zeros((tm, LANES), np.float32)
    blk = (np.arange(tm) // SEL_LEN) % BLOCKS_PER_CHUNK
    aug[np.arange(tm), HEAD_DIM + blk] = 1.0
    aug[:, HEAD_DIM + BLOCKS_PER_CHUNK:HEAD_DIM + BLOCKS_PER_CHUNK + 2] = 1.0
    const = lambda shape: pl.BlockSpec(shape, lambda i: (0,) * len(shape))
    out_shape = (
        jax.ShapeDtypeStruct((ATTN_W, S), BF16),
        jax.ShapeDtypeStruct((N_BRANCH * N_HEADS, S), F32),
        jax.ShapeDtypeStruct((N_KV, V_ROWS, S), BF16),
        jax.ShapeDtypeStruct((N_KV, V_ROWS, S), BF16),
        jax.ShapeDtypeStruct((N_KV, S, LANES), BF16),
        jax.ShapeDtypeStruct((S, LANES), BF16),
        jax.ShapeDtypeStruct((S, 2 * KV_W), BF16),
        jax.ShapeDtypeStruct((S, CONV_W), BF16),
    )
    out_specs = (
        pl.BlockSpec((ATTN_W, tb), lambda i: (0, i)),
        pl.BlockSpec((N_BRANCH * N_HEADS, tb), lambda i: (0, i)),
        pl.BlockSpec((N_KV, V_ROWS, tb), lambda i: (0, 0, i)),
        pl.BlockSpec((N_KV, V_ROWS, tb), lambda i: (0, 0, i)),
        pl.BlockSpec((N_KV, tb, LANES), lambda i: (0, i, 0)),
        pl.BlockSpec((tb, LANES), lambda i: (i, 0)),
        pl.BlockSpec((tb, 2 * KV_W), lambda i: (i, 0)),
        pl.BlockSpec((tb, CONV_W), lambda i: (i, 0)),
    )
    in_specs = [
        pl.BlockSpec((tb, D_MODEL), lambda i: (i, 0)),
        const((1, D_MODEL)),
        const(wtok.shape),
        const(wfeat.shape),
        const(ind.shape),
        const((1, 2 * KV_W)),
        const(aug.shape),
        const((HEAD_DIM, tm)),
        const((8, CONV_W)),
        const((1, CONV_W)),
    ]
    return pl.pallas_call(
        _in_proj_kernel,
        grid=(nt,),
        in_specs=in_specs,
        out_specs=out_specs,
        out_shape=out_shape,
        scratch_shapes=[pltpu.VMEM((_IN_SPLIT, tm + 8, CONV_W), F32), pltpu.VMEM((8, CONV_W), F32)],
        compiler_params=pltpu.CompilerParams(dimension_semantics=("arbitrary",), vmem_limit_bytes=VMEM_LIMIT),
        name="in_proj",
    )(x2, g1, wtok, wfeat, jnp.asarray(ind, BF16), kgain, jnp.asarray(aug), gq, convw, cgain)


def _compress_kernel(r_ref, wtop_ref, wbot_ref, posk_ref, posv_ref, w1k_ref, w1v_ref, w2k_ref, w2vT_ref, kgain_ref,
                     kc_ref, vcT_ref):
    nr = r_ref.shape[0]
    r = r_ref[...]
    u = jnp.dot(r, wtop_ref[...], preferred_element_type=F32)
    lo = jnp.dot(r, wbot_ref[...], preferred_element_type=F32)
    bias_k = jnp.dot(posk_ref[...], w1k_ref[...], preferred_element_type=F32)[0:1, :]
    bias_v = jnp.dot(posv_ref[...], w1v_ref[...], preferred_element_type=F32)[0:1, :]
    bias4 = jnp.concatenate([bias_k] * N_KV + [bias_v] * N_KV, axis=1)
    hid = u + pltpu.roll(lo, nr - 1, 0) + bias4
    act = jax.nn.gelu(hid)
    for g in range(N_KV):
        ak = act[:, g * CMP_HIDDEN:(g + 1) * CMP_HIDDEN].astype(BF16)
        av = act[:, (N_KV + g) * CMP_HIDDEN:(N_KV + g + 1) * CMP_HIDDEN].astype(BF16)
        kc = jnp.dot(ak, w2k_ref[...], preferred_element_type=F32)
        ssq = jnp.sum(kc * kc, axis=-1, keepdims=True)
        kc_ref[g] = (kc * lax.rsqrt(ssq * (1.0 / HEAD_DIM) + EPS) * kgain_ref[...]).astype(BF16)
        vcT_ref[g] = lax.dot_general(w2vT_ref[...], av, (((1,), (1,)), ((), ())),
                                     preferred_element_type=F32).astype(BF16)


def _compress(kcv, wtop, wbot, posk, posv, w1k, w1v, w2k, w2vT, kgain):
    S = kcv.shape[0]
    nr = S // CMP_STRIDE
    r = kcv.reshape(nr, CMP_STRIDE * 2 * KV_W)
    vmem = pl.BlockSpec(memory_space=pltpu.VMEM)
    return pl.pallas_call(
        _compress_kernel,
        out_shape=(jax.ShapeDtypeStruct((N_KV, nr, LANES), BF16),
                   jax.ShapeDtypeStruct((N_KV, HEAD_DIM, nr), BF16)),
        in_specs=[vmem] * 10,
        out_specs=(vmem, vmem),
        compiler_params=pltpu.CompilerParams(vmem_limit_bytes=VMEM_LIMIT),
        name="compress",
    )(r, wtop, wbot, posk, posv, w1k, w1v, w2k, w2vT, kgain)


def _tile4(row):
    return jnp.concatenate([row] * GQA, axis=1)


def _attn_kernel(qT_ref, gT_ref, ks_ref, vsT_ref, kw_ref, vwT_ref, kc_ref, vcT_ref,
                 tw_ref, ts_ref, tc_ref, b31_ref, gain_ref, out_ref,
                 sc_ref, sw_ref, imp_ref, aoct_ref, aall_ref, qs_ref, sa_ref, sb_ref, qz_ref, qsb_ref, fin_ref, ocmp_ref, pk_ref):
    ncr = kc_ref.shape[1]
    nsel = aall_ref.shape[1] * BLOCKS_PER_CHUNK
    noct = aoct_ref.shape[1] - 1
    width = GQA * LANES
    c = pl.program_id(0)
    t0 = c * Q_BLOCK
    tl = lax.broadcasted_iota(I32, (1, LANES), 1)
    cur = 2 * c + (tl >= SEL_LEN).astype(I32)
    zb = jnp.maximum(2 * c - 2, 0)
    zs = pl.multiple_of(zb * SEL_LEN, Q_BLOCK)
    ts_off = pl.multiple_of(zs - (t0 - Q_BLOCK), Q_BLOCK)
    zeros_q = jnp.zeros((HEAD_DIM, width), BF16)

    col_ok = _tile4(t0 + tl) >= CMP_LEN - 1
    j = lax.broadcasted_iota(I32, (nsel, LANES), 0)
    forced = (j == 0) | (j == cur) | (j == cur - 1)
    @pl.when(c == 0)
    def _():
        imp_ref[...] = jnp.zeros(imp_ref.shape, F32)


    ws = pl.multiple_of(jnp.maximum(t0 - WINDOW, 0), Q_BLOCK)
    tw_off = pl.multiple_of(ws - (t0 - WINDOW), Q_BLOCK)
    nw = WINDOW + Q_BLOCK
    lo = pl.multiple_of(jnp.clip((8 * c - 16) // 16 * 16, 0, ncr - TC_WIN), 16)
    tc_off = pl.multiple_of(lo - (8 * c - 16) + TC_LEAD, 8)
    qgs = [jnp.concatenate([qT_ref[(GQA * g + r) * HEAD_DIM:(GQA * g + r + 1) * HEAD_DIM, :]
                            for r in range(GQA)], axis=1) for g in range(N_KV)]

    for g in range(N_KV):
        qw = jnp.concatenate([qgs[g], zeros_q] if g == 0 else [zeros_q, qgs[g]], axis=0)
        sw_ref[g] = (jnp.dot(kw_ref[pl.ds(ws, nw), :], qw, preferred_element_type=F32)
                     + tw_ref[g, pl.ds(tw_off, nw), :])

    def compressed(rows):
        for g in range(N_KV):
            qc = jnp.concatenate([qgs[g], zeros_q], axis=0)
            s = jnp.dot(kc_ref[g, 0:rows, :], qc, preferred_element_type=F32)
            row = lax.broadcasted_iota(I32, (rows, width), 0)
            sc_ref[g, 0:rows, :] = jnp.where(row < lo, s, NEG)
            s_loc = jnp.dot(kc_ref[g, pl.ds(lo, TC_WIN), :], qc, preferred_element_type=F32)
            sc_ref[g, pl.ds(lo, TC_WIN), :] = s_loc + tc_ref[g, pl.ds(tc_off, TC_WIN), :]
        for g in range(N_KV):
            s = sc_ref[g, 0:rows, :]
            m = jnp.max(s, axis=0, keepdims=True)
            e = jnp.exp2(s - m)
            l = jnp.sum(e, axis=0, keepdims=True)
            p = e * jnp.where(col_ok, 1.0 / jnp.maximum(l, 1e-30), 0.0)
            ocmp_ref[g] = jnp.dot(vcT_ref[g, :, 0:rows], p.astype(BF16), preferred_element_type=F32)
            imp_ref[g, 8:8 + rows, :] = (
                (p[:, 0:LANES] + p[:, LANES:2 * LANES]) + p[:, 2 * LANES:3 * LANES] + p[:, 3 * LANES:4 * LANES])

    sizes = list(range(CMP_STEP, ncr, CMP_STEP)) + [ncr]
    variant = jnp.minimum((lo + TC_WIN + CMP_STEP - 1) // CMP_STEP, len(sizes)) - 1
    for v, rows in enumerate(sizes):
        pl.when(variant == v)(functools.partial(compressed, rows))

    o_cmp, scores0 = [], []
    for g in range(N_KV):
        o_cmp.append(ocmp_ref[g])
        isel = jnp.zeros((nsel, LANES), F32)
        for off in (0, -1, 1, 0, 2, 1, 3, 2):
            isel = isel + imp_ref[g, pl.ds(8 + off, nsel, stride=4), :]
        score = jnp.where(forced, -jnp.inf, isel)
        scores0.append(jnp.where(j > cur, NEG, score))

    o_win = []
    for g in range(N_KV):
        sw = sw_ref[g]
        mw = jnp.max(sw, axis=0, keepdims=True)
        pw = jnp.exp2(sw - mw)
        accw = jnp.dot(vwT_ref[g, :, pl.ds(ws, nw)], pw.astype(BF16), preferred_element_type=F32)
        o_win.append(accw[0:HEAD_DIM] * (1.0 / jnp.maximum(accw[HEAD_DIM:HEAD_DIM + 1], 1e-30)))

    n_picks = SEL_TOPK - 3
    fast = list(scores0)
    for _ in range(n_picks):
        for g in range(N_KV):
            mx = jnp.max(fast[g], axis=0, keepdims=True)
            fast[g] = jnp.where(fast[g] == mx, -jnp.inf, fast[g])
    eligible = (j <= cur) & jnp.logical_not(forced)
    n_eligible = cur + 1 - (1 + (cur >= 1).astype(I32) + (cur >= 2).astype(I32))
    expect = jnp.minimum(n_picks, n_eligible).astype(F32)
    wrong = jnp.zeros((1, LANES), F32)
    for g in range(N_KV):
        pk_ref[g] = fast[g]
        got = jnp.sum(jnp.where((fast[g] == -jnp.inf) & eligible, 1.0, 0.0), axis=0, keepdims=True)
        wrong = jnp.maximum(wrong, jnp.abs(got - expect))

    @pl.when(jnp.max(wrong) > 0.0)
    def _():
        def pick(_, score):
            mx = jnp.max(score, axis=0, keepdims=True)
            first = jnp.min(jnp.where(score == mx, j, nsel), axis=0, keepdims=True)
            return jnp.where(j == first, -jnp.inf, score)

        for g in range(N_KV):
            pk_ref[g] = lax.fori_loop(0, n_picks, pick, scores0[g])

    picks = [pk_ref[g] for g in range(N_KV)]

    m_init, acc_init, gates = [], [], []
    for g in range(N_KV):
        qg = qgs[g]
        b31row = b31_ref[g, 0:1, :]
        chosen = (picks[g] == -jnp.inf) & (j <= cur)
        a_all = _tile4(jnp.where(chosen, 0.0, -MASK_BIG)).reshape(noct, BLOCKS_PER_CHUNK, width)
        a_far = _tile4(jnp.where(chosen & (j < zb), 0.0, -MASK_BIG)).reshape(noct, BLOCKS_PER_CHUNK, width)
        b_hi = b31row.astype(BF16).astype(F32)
        erow = lax.broadcasted_iota(I32, (8, width), 0)
        extra = jnp.where(erow == 0, b_hi, jnp.where(erow == 1, b31row - b_hi, 0.0))
        aoct_ref[g, 0:noct] = jnp.concatenate(
            [a_far, jnp.broadcast_to(extra[None], (noct, 8, width))], axis=1).astype(BF16)
        aoct_ref[g, noct] = jnp.concatenate([jnp.full((8, width), -MASK_BIG, F32), extra], axis=0).astype(BF16)
        aall_ref[g] = jnp.concatenate([a_all, jnp.zeros((noct, 8, width), F32)], axis=1).astype(BF16)
        for q_ref in (qs_ref, qsb_ref):
            q_ref[g, 0:HEAD_DIM, :] = qg
            q_ref[g, HEAD_DIM + 16:LANES, :] = jnp.zeros((LANES - HEAD_DIM - 16, width), BF16)

        halves = []
        for h in range(2):
            qz_ref[2 * g + h, HEAD_DIM:HEAD_DIM + 16, :] = aall_ref[g, (zb + 2 * h) // BLOCKS_PER_CHUNK]
            qz_ref[2 * g + h, 0:HEAD_DIM, :] = qg
            qz_ref[2 * g + h, HEAD_DIM + 16:LANES, :] = jnp.zeros((LANES - HEAD_DIM - 16, width), BF16)
            sh = jnp.dot(ks_ref[g, pl.ds(zs + h * Q_BLOCK, Q_BLOCK), :], qz_ref[2 * g + h], preferred_element_type=F32)
            halves.append(sh + ts_ref[g, pl.ds(ts_off + h * Q_BLOCK, Q_BLOCK), :])
        sz = jnp.concatenate(halves, axis=0)
        mz = jnp.max(sz, axis=0, keepdims=True)
        pz = jnp.exp2(sz - mz)
        m_init.append(mz)
        acc_init.append(jnp.dot(vsT_ref[g, :, pl.ds(zs, 2 * Q_BLOCK)], pz.astype(BF16), preferred_element_type=F32))
        gates.append([jnp.concatenate([gT_ref[br * N_HEADS + GQA * g + r:br * N_HEADS + GQA * g + r + 1, :]
                                       for r in range(GQA)], axis=1) for br in range(N_BRANCH)])

    nfar = (zb + BLOCKS_PER_CHUNK - 1) // BLOCKS_PER_CHUNK

    nsub = 1
    sub = CHUNK // nsub

    def score(g, u, buf_ref, h, q_ref):
        k0 = pl.multiple_of(jnp.minimum(u, noct - 1) * CHUNK + h * sub, sub)
        s = jnp.dot(ks_ref[g, pl.ds(k0, sub), :], q_ref[g], preferred_element_type=F32)
        buf_ref[g, h * sub:(h + 1) * sub, :] = s
        return jnp.max(s, axis=0, keepdims=True)

    def set_mask_rows(g, u, q_ref):
        oct_id = jnp.where(u < nfar, u, noct)
        q_ref[g, HEAD_DIM:HEAD_DIM + 16, :] = aoct_ref[g, oct_id]

    def accumulate(g, u, buf_ref, h, mn, acc):
        k0 = pl.multiple_of(jnp.minimum(u, noct - 1) * CHUNK + h * sub, sub)
        p = jnp.exp2(buf_ref[g, h * sub:(h + 1) * sub, :] - mn)
        return acc + jnp.dot(vsT_ref[g, :, pl.ds(k0, sub)], p.astype(BF16), preferred_element_type=F32)

    def step(u, cur_ref, nxt_ref, q_ref, state, cms):
        out_state, out_cms = [], []
        for g in range(N_KV):
            m, acc = state[2 * g], state[2 * g + 1]
            mn = jnp.maximum(m, cms[g])
            acc = jnp.exp2(m - mn) * acc
            set_mask_rows(g, u + 1, q_ref)
            cm = None
            for h in range(nsub):
                ch = score(g, u + 1, nxt_ref, h, q_ref)
                cm = ch if cm is None else jnp.maximum(cm, ch)
                acc = accumulate(g, u, cur_ref, h, mn, acc)
            out_cms.append(cm)
            out_state += [mn, acc]
        return out_state, out_cms

    def far(i, carry):
        state, cm_b = step(2 * i, sa_ref, sb_ref, qsb_ref, carry[0:4], carry[4:6])
        state, cm_a = step(2 * i + 1, sb_ref, sa_ref, qs_ref, state, cm_b)
        return tuple(state + cm_a)

    first = []
    for g in range(N_KV):
        set_mask_rows(g, 0, qs_ref)
        cm = None
        for h in range(nsub):
            ch = score(g, 0, sa_ref, h, qs_ref)
            cm = ch if cm is None else jnp.maximum(cm, ch)
        first.append(cm)
    nchunks = jnp.maximum(nfar, 1)
    npairs = (nchunks - 1) // 2
    carry = lax.fori_loop(0, npairs, far, (m_init[0], acc_init[0], m_init[1], acc_init[1], first[0], first[1]))
    u_last = 2 * npairs

    def finish(two_left):
        for g in range(N_KV):
            m, acc, cm = carry[2 * g], carry[2 * g + 1], carry[4 + g]
            mn = jnp.maximum(m, cm)
            acc = jnp.exp2(m - mn) * acc
            if two_left:
                set_mask_rows(g, u_last + 1, qsb_ref)
                cm_b = score(g, u_last + 1, sb_ref, 0, qsb_ref)
            acc = accumulate(g, u_last, sa_ref, 0, mn, acc)
            if two_left:
                mn_b = jnp.maximum(mn, cm_b)
                acc = accumulate(g, u_last + 1, sb_ref, 0, mn_b, jnp.exp2(mn - mn_b) * acc)
            fin_ref[g] = acc

    pl.when(nchunks - u_last == 1)(lambda: finish(False))
    pl.when(nchunks - u_last == 2)(lambda: finish(True))
    carry = [None, fin_ref[0], None, fin_ref[1]]

    o_all, ssq = [], jnp.zeros((1, LANES), F32)
    for g in range(N_KV):
        acc = carry[2 * g + 1]
        o_sel = acc[0:HEAD_DIM] * (1.0 / jnp.maximum(acc[HEAD_DIM:HEAD_DIM + 1], 1e-30))
        o = gates[g][0] * o_cmp[g] + gates[g][1] * o_sel + gates[g][2] * o_win[g]
        o_all.append(o)
        cs = jnp.sum(o * o, axis=0, keepdims=True)
        ssq = ssq + ((cs[:, 0:LANES] + cs[:, LANES:2 * LANES]) + (cs[:, 2 * LANES:3 * LANES] + cs[:, 3 * LANES:]))
    inv = _tile4(lax.rsqrt(ssq * (1.0 / ATTN_W) + EPS))
    for g in range(N_KV):
        on = o_all[g] * inv * gain_ref[g]
        for k in range(GQA // 2):
            pair = jnp.concatenate([on[:, (2 * k) * LANES:(2 * k + 1) * LANES],
                                    on[:, (2 * k + 1) * LANES:(2 * k + 2) * LANES]], axis=0)
            col = (GQA * g + 2 * k) * HEAD_DIM
            out_ref[:, col:col + 2 * HEAD_DIM] = pair.T.astype(BF16)


def _attention(qT, gT, ks, vsT, kw, vwT, kc, vcT, tables, gain_b):
    S = qT.shape[1]
    nq = S // Q_BLOCK
    ncr = kc.shape[1]
    nsel = S // SEL_LEN
    noct = S // CHUNK
    width = GQA * LANES
    tw, ts, tc, b31 = tables
    vmem = pl.BlockSpec(memory_space=pltpu.VMEM)
    in_specs = [
        pl.BlockSpec((ATTN_W, Q_BLOCK), lambda c: (0, c)),
        pl.BlockSpec((N_BRANCH * N_HEADS, Q_BLOCK), lambda c: (0, c)),
    ] + [vmem] * 11
    return pl.pallas_call(
        _attn_kernel,
        grid=(nq,),
        in_specs=in_specs,
        out_specs=pl.BlockSpec((Q_BLOCK, ATTN_W), lambda c: (c, 0)),
        out_shape=jax.ShapeDtypeStruct((S, ATTN_W), BF16),
        scratch_shapes=[
            pltpu.VMEM((N_KV, ncr, width), F32),
            pltpu.VMEM((N_KV, WINDOW + Q_BLOCK, width), F32),
            pltpu.VMEM((N_KV, ncr + 16, LANES), F32),
            pltpu.VMEM((N_KV, noct + 1, 16, width), BF16),
            pltpu.VMEM((N_KV, noct, 16, width), BF16),
            pltpu.VMEM((N_KV, LANES, width), BF16),
            pltpu.VMEM((N_KV, CHUNK, width), F32),
            pltpu.VMEM((N_KV, CHUNK, width), F32),
            pltpu.VMEM((2 * N_KV, LANES, width), BF16),
            pltpu.VMEM((N_KV, LANES, width), BF16),
            pltpu.VMEM((N_KV, V_ROWS, width), F32),
            pltpu.VMEM((N_KV, HEAD_DIM, width), F32),
            pltpu.VMEM((N_KV, nsel, LANES), F32),
        ],
        compiler_params=pltpu.CompilerParams(dimension_semantics=("arbitrary",), vmem_limit_bytes=VMEM_LIMIT),
        name="nsa_attention",
    )(qT, gT, ks, vsT, kw, vwT, kc, vcT, tw, ts, tc, b31, gain_b)


def _mix_and_norm(x_ref, attn_ref, conv_ref, wout_ref, g2_ref):
    x1 = x_ref[...] + jnp.dot(attn_ref[...], wout_ref[0:ATTN_W, :], preferred_element_type=F32) \
        + jnp.dot(conv_ref[...], wout_ref[ATTN_W:, :], preferred_element_type=F32)
    ms = jnp.mean(x1 * x1, axis=-1, keepdims=True)
    h2 = x1 * lax.rsqrt(ms + EPS) * g2_ref[...]
    return x1, h2


def _ffn_kernel(x_ref, attn_ref, conv_ref, wout_ref, g2_ref, wg_ref, wu_ref, wd_ref, out_ref):
    x1, h2 = _mix_and_norm(x_ref, attn_ref, conv_ref, wout_ref, g2_ref)
    h2 = h2.astype(BF16)
    a = jnp.dot(h2, wg_ref[...], preferred_element_type=F32)
    u = jnp.dot(h2, wu_ref[...], preferred_element_type=F32)
    y = (a * jax.nn.sigmoid(a) * u).astype(BF16)
    out_ref[...] = x1 + jnp.dot(y, wd_ref[...], preferred_element_type=F32)


_FFN_TM = 512


def _outproj_ffn(x2, attn_n, conv_n, wout, g2, wg, wu, wd):
    S = x2.shape[0]
    tm = _FFN_TM
    resident = pl.BlockSpec(memory_space=pltpu.VMEM)
    return pl.pallas_call(
        _ffn_kernel,
        grid=(S // tm,),
        in_specs=[
            pl.BlockSpec((tm, D_MODEL), lambda i: (i, 0)),
            pl.BlockSpec((tm, ATTN_W), lambda i: (i, 0)),
            pl.BlockSpec((tm, CONV_W), lambda i: (i, 0)),
            resident, resident, resident, resident, resident,
        ],
        out_specs=pl.BlockSpec((tm, D_MODEL), lambda i: (i, 0)),
        out_shape=jax.ShapeDtypeStruct((S, D_MODEL), F32),
        compiler_params=pltpu.CompilerParams(dimension_semantics=("arbitrary",), vmem_limit_bytes=VMEM_LIMIT),
        name="outproj_ffn",
    )(x2, attn_n, conv_n, wout, g2, wg, wu, wd)


TOKEN_TILE = (D_MODEL // LANES, LANES)


def _to_token_tiles(ref, rows):
    x = jnp.stack([rows[:, k * LANES:(k + 1) * LANES] for k in range(TOKEN_TILE[0])], axis=0)
    ref[...] = pltpu.einshape('ktl->tkl', x)


def _from_token_tiles(ref):
    x = pltpu.einshape('tkl->ktl', ref[...])
    return jnp.concatenate([x[k] for k in range(TOKEN_TILE[0])], axis=1)


def _router_kernel(x_ref, attn_ref, conv_ref, wout_ref, g2_ref, rw_ref, rb_ref, tri_ref,
                   x1_ref, h2_ref, route_ref, cnt_ref, run_ref):
    tm = x_ref.shape[0]
    i = pl.program_id(0)

    @pl.when(i == 0)
    def _():
        run_ref[...] = jnp.zeros(run_ref.shape, F32)

    x1, h2 = _mix_and_norm(x_ref, attn_ref, conv_ref, wout_ref, g2_ref)
    x1_ref[...] = x1
    h2b = h2.astype(BF16)
    _to_token_tiles(h2_ref, h2b.astype(F32))
    lane = lax.broadcasted_iota(I32, (tm, LANES), 1)
    logits = jnp.dot(h2b, rw_ref[...], preferred_element_type=F32) + rb_ref[...]
    logits = jnp.where(lane < N_EXPERTS, logits, -jnp.inf)
    m1 = jnp.max(logits, axis=-1, keepdims=True)
    i1 = jnp.min(jnp.where(logits == m1, lane, LANES), axis=-1, keepdims=True)
    rest = jnp.where(lane == i1, -jnp.inf, logits)
    m2 = jnp.max(rest, axis=-1, keepdims=True)
    i2 = jnp.min(jnp.where(rest == m2, lane, LANES), axis=-1, keepdims=True)
    e2 = jnp.exp(m2 - m1)
    den = 1.0 + e2
    oh1 = (lane == i1).astype(F32)
    oh2 = (lane == i2).astype(F32)
    both = oh1 + oh2
    before = run_ref[0:1, :] + jnp.dot(tri_ref[...], both.astype(BF16), preferred_element_type=F32)
    rank1 = jnp.sum(before * oh1, axis=-1, keepdims=True)
    rank2 = jnp.sum(before * oh2, axis=-1, keepdims=True)
    fields = (i1.astype(F32), i2.astype(F32), rank1, rank2, 1.0 / den, e2 / den)
    route = jnp.zeros((tm, LANES), F32)
    for k, v in enumerate(fields):
        route = jnp.where(lane == k, v, route)
    route_ref[...] = route
    run_ref[...] = run_ref[...] + jnp.sum(both, axis=0, keepdims=True)
    cnt_ref[...] = run_ref[...]


def _outproj_router(x2, attn_n, conv_n, wout, g2, rw, rb):
    S = x2.shape[0]
    tm = _FFN_TM
    tri = np.tril(np.ones((tm, tm), np.float32), -1)
    const = lambda shape: pl.BlockSpec(shape, lambda i: (0,) * len(shape))
    return pl.pallas_call(
        _router_kernel,
        grid=(S // tm,),
        in_specs=[
            pl.BlockSpec((tm, D_MODEL), lambda i: (i, 0)),
            pl.BlockSpec((tm, ATTN_W), lambda i: (i, 0)),
            pl.BlockSpec((tm, CONV_W), lambda i: (i, 0)),
            const(wout.shape), const((1, D_MODEL)), const(rw.shape), const((1, LANES)), const((tm, tm)),
        ],
        out_specs=(pl.BlockSpec((tm, D_MODEL), lambda i: (i, 0)),
                   pl.BlockSpec((tm,) + TOKEN_TILE, lambda i: (i, 0, 0)),
                   pl.BlockSpec((tm, LANES), lambda i: (i, 0)),
                   const((8, LANES))),
        out_shape=(jax.ShapeDtypeStruct((S, D_MODEL), F32),
                   jax.ShapeDtypeStruct((S,) + TOKEN_TILE, F32),
                   jax.ShapeDtypeStruct((S, LANES), F32),
                   jax.ShapeDtypeStruct((8, LANES), F32)),
        scratch_shapes=[pltpu.VMEM((8, LANES), F32)],
        compiler_params=pltpu.CompilerParams(dimension_semantics=("arbitrary",), vmem_limit_bytes=VMEM_LIMIT),
        name="outproj_router",
    )(x2, attn_n, conv_n, wout, g2, rw, rb, jnp.asarray(tri, BF16))


_ROW_TM = 256
_EXP_TM = 512
_DMA_UNROLL = 8


def _row_copy(src_ref, src_row, dst_ref, dst_row, sem):
    return pltpu.make_async_copy(src_ref.at[src_row], dst_ref.at[dst_row], sem)


def _dispatch_kernel(pos_ref, h_ref, zero_ref, xs_ref, hbuf_ref, sem):
    del zero_ref
    tm = h_ref.shape[0]
    i = pl.program_id(0)
    slot = i % 2
    hbuf_ref[slot] = h_ref[...]

    def issue(t, carry):
        for k in range(2):
            _row_copy(hbuf_ref.at[slot], t, xs_ref, pos_ref[0, 0, 2 * t + k], sem.at[slot]).start(priority=k)
        return carry

    lax.fori_loop(0, tm, issue, 0, unroll=_DMA_UNROLL)

    def drain(which):
        def body(t, carry):
            for k in range(2):
                _row_copy(hbuf_ref.at[which], 0, xs_ref, 0, sem.at[which]).wait()
            return carry
        lax.fori_loop(0, tm, body, 0, unroll=_DMA_UNROLL)

    @pl.when(i > 0)
    def _():
        drain(1 - slot)

    @pl.when(i == pl.num_programs(0) - 1)
    def _():
        drain(slot)


def _dispatch(pos3, h2, n_rows):
    S = h2.shape[0]
    tm = _ROW_TM
    zeros = jnp.zeros((n_rows,) + TOKEN_TILE, F32)
    return pl.pallas_call(
        _dispatch_kernel,
        grid=(S // tm,),
        in_specs=[
            pl.BlockSpec((1, 1, 2 * tm), lambda i: (i, 0, 0), memory_space=pltpu.SMEM),
            pl.BlockSpec((tm,) + TOKEN_TILE, lambda i: (i, 0, 0)),
            pl.BlockSpec(memory_space=pl.ANY),
        ],
        out_specs=pl.BlockSpec(memory_space=pl.ANY),
        out_shape=jax.ShapeDtypeStruct((n_rows,) + TOKEN_TILE, F32),
        scratch_shapes=[pltpu.VMEM((2, tm) + TOKEN_TILE, F32), pltpu.SemaphoreType.DMA((2,))],
        input_output_aliases={2: 0},
        compiler_params=pltpu.CompilerParams(dimension_semantics=("arbitrary",), vmem_limit_bytes=VMEM_LIMIT),
        name="moe_dispatch",
    )(pos3, h2, zeros)


def _experts_kernel(te_ref, tb_ref, nt_ref, xs_ref, wg_ref, wu_ref, wd_ref, ys_ref):
    i = pl.program_id(0)

    @pl.when(i < nt_ref[0])
    def _():
        x = _from_token_tiles(xs_ref).astype(BF16)
        a = jnp.dot(x, wg_ref[0], preferred_element_type=F32)
        u = jnp.dot(x, wu_ref[0], preferred_element_type=F32)
        y = (a * jax.nn.sigmoid(a) * u).astype(BF16)
        _to_token_tiles(ys_ref, jnp.dot(y, wd_ref[0], preferred_element_type=F32))

    @pl.when(i >= nt_ref[0])
    def _():
        ys_ref[...] = jnp.zeros(ys_ref.shape, F32)


def _experts(tile_e, tile_b, n_tiles, xs, wg, wu, wd):
    n_rows = xs.shape[0]
    tm = _EXP_TM
    weights = lambda shape: pl.BlockSpec(shape, lambda i, te, tb, nt: (te[i], 0, 0), pipeline_mode=pl.Buffered(1))
    grid_spec = pltpu.PrefetchScalarGridSpec(
        num_scalar_prefetch=3,
        grid=(n_rows // tm,),
        in_specs=[
            pl.BlockSpec((tm,) + TOKEN_TILE, lambda i, te, tb, nt: (tb[i], 0, 0)),
            weights((1, D_MODEL, D_FF)), weights((1, D_MODEL, D_FF)), weights((1, D_FF, D_MODEL)),
        ],
        out_specs=pl.BlockSpec((tm,) + TOKEN_TILE, lambda i, te, tb, nt: (tb[i], 0, 0)),
    )
    return pl.pallas_call(
        _experts_kernel,
        grid_spec=grid_spec,
        out_shape=jax.ShapeDtypeStruct((n_rows,) + TOKEN_TILE, F32),
        compiler_params=pltpu.CompilerParams(dimension_semantics=("arbitrary",), vmem_limit_bytes=VMEM_LIMIT),
        name="moe_experts",
    )(tile_e, tile_b, n_tiles, xs, wg, wu, wd)


def _combine_kernel(pos_ref, nxt_ref, x1_ref, route_ref, ys_ref, out_ref, y1_ref, y2_ref, sem):
    tm = x1_ref.shape[0]
    i = pl.program_id(0)
    slot = i % 2

    def issue(src_pos_ref, which):
        def body(t, carry):
            _row_copy(ys_ref, src_pos_ref[0, 0, 2 * t], y1_ref.at[which], t, sem.at[which]).start(priority=0)
            _row_copy(ys_ref, src_pos_ref[0, 0, 2 * t + 1], y2_ref.at[which], t, sem.at[which]).start(priority=1)
            return carry
        lax.fori_loop(0, tm, body, 0, unroll=_DMA_UNROLL)

    @pl.when(i == 0)
    def _():
        issue(pos_ref, slot)

    @pl.when(i + 1 < pl.num_programs(0))
    def _():
        issue(nxt_ref, 1 - slot)

    def drain(t, carry):
        _row_copy(ys_ref, 0, y1_ref.at[slot], 0, sem.at[slot]).wait()
        _row_copy(ys_ref, 0, y2_ref.at[slot], 0, sem.at[slot]).wait()
        return carry

    lax.fori_loop(0, tm, drain, 0, unroll=_DMA_UNROLL)
    lane = lax.broadcasted_iota(I32, (tm, LANES), 1)
    route = route_ref[...]
    w1 = jnp.sum(jnp.where(lane == 4, route, 0.0), axis=-1, keepdims=True)
    w2 = jnp.sum(jnp.where(lane == 5, route, 0.0), axis=-1, keepdims=True)
    out_ref[...] = x1_ref[...] + (_from_token_tiles(y1_ref.at[slot]) * w1 + _from_token_tiles(y2_ref.at[slot]) * w2)


def _combine(pos3, x1, route, ys):
    S = x1.shape[0]
    tm = _ROW_TM
    last = S // tm - 1
    return pl.pallas_call(
        _combine_kernel,
        grid=(S // tm,),
        in_specs=[
            pl.BlockSpec((1, 1, 2 * tm), lambda i: (i, 0, 0), memory_space=pltpu.SMEM),
            pl.BlockSpec((1, 1, 2 * tm), lambda i: (jnp.minimum(i + 1, last), 0, 0), memory_space=pltpu.SMEM),
            pl.BlockSpec((tm, D_MODEL), lambda i: (i, 0)),
            pl.BlockSpec((tm, LANES), lambda i: (i, 0)),
            pl.BlockSpec(memory_space=pl.ANY),
        ],
        out_specs=pl.BlockSpec((tm, D_MODEL), lambda i: (i, 0)),
        out_shape=jax.ShapeDtypeStruct((S, D_MODEL), F32),
        scratch_shapes=[pltpu.VMEM((2, tm) + TOKEN_TILE, F32), pltpu.VMEM((2, tm) + TOKEN_TILE, F32),
                        pltpu.SemaphoreType.DMA((2,))],
        compiler_params=pltpu.CompilerParams(dimension_semantics=("arbitrary",), vmem_limit_bytes=VMEM_LIMIT),
        name="moe_combine",
    )(pos3, pos3, x1, route, ys)


def _moe(h2, x1, route, counts, wg, wu, wd):
    S = h2.shape[0]
    tm = _EXP_TM
    n_tiles_max = 2 * S // tm + N_EXPERTS
    cnt = counts[0, 0:N_EXPERTS].astype(I32)
    tiles = (cnt + tm - 1) // tm
    first = jnp.cumsum(tiles) - tiles
    n_tiles = jnp.sum(tiles)
    idx = jnp.arange(n_tiles_max, dtype=I32)
    last = jnp.minimum(idx, n_tiles - 1)
    tile_e = (jnp.sum(last[:, None] >= first[None, :], axis=1) - 1).astype(I32)
    eid = route[:, 0:2].astype(I32)
    pos = first[eid] * tm + route[:, 2:4].astype(I32)
    pos3 = pos.reshape(S // _ROW_TM, 1, 2 * _ROW_TM)
    xs = _dispatch(pos3, h2, n_tiles_max * tm)
    ys = _experts(tile_e, idx, n_tiles.reshape(1), xs, wg, wu, wd)
    return _combine(pos3, x1, route, ys)


def _split_w_in(w):
    o = np.cumsum([0, ATTN_W] + [KV_W] * 6 + [N_BRANCH * N_HEADS] + [CONV_W] * 3)
    q, kc, vc, ksl, vsl, kwn, vwn, gts, cb, cc, ch = (w[:, o[i]:o[i + 1]] for i in range(11))
    perm = np.array([h * N_BRANCH + br for br in range(N_BRANCH) for h in range(N_HEADS)])
    wtok = jnp.concatenate([kc, vc, ksl, kwn, cb, cc, ch], axis=1).astype(BF16)
    feat = jnp.concatenate([q, vsl, vwn, gts[:, perm], jnp.zeros((D_MODEL, 8), w.dtype)], axis=1)
    return wtok, feat.T.astype(BF16)


def _expand_cmp_w1(w1k, w1v):
    kinds = jnp.stack([w1k, w1v]).reshape(2, CMP_LEN, HEAD_DIM, CMP_HIDDEN)
    per_col = jnp.repeat(kinds, N_KV, axis=0)
    eye = jnp.eye(2 * N_KV, dtype=F32)
    out = []
    for l0 in (0, CMP_STRIDE):
        w = per_col[:, l0:l0 + CMP_STRIDE].transpose(1, 0, 2, 3)
        blk = w[:, :, :, None, :] * eye[None, :, None, :, None]
        out.append(blk.reshape(CMP_STRIDE * 2 * KV_W, 2 * N_KV * CMP_HIDDEN).astype(BF16))
    return out


def kernel(x, rel_bias, norm1, w_in, q_norm, k_norm, cmp_pos_k, cmp_pos_v, cmp_k_w1, cmp_k_w2, cmp_v_w1, cmp_v_w2,
           conv_w, attn_out_norm, conv_out_norm, w_out, norm2, ffn_w_gate, ffn_w_up, ffn_w_down, router_w, router_b,
           moe_w_gate, moe_w_up, moe_w_down):
    B, S, _ = x.shape
    assert B == 1 and S % CHUNK == 0 and S >= WINDOW + Q_BLOCK
    depth = norm1.shape[0]
    x2 = x.reshape(S, D_MODEL)
    tables = _bias_tables(rel_bias)
    for layer in range(depth):
        wtok, wfeat = _split_w_in(w_in[layer])
        kgain = jnp.concatenate([jnp.tile(k_norm[layer, 1], N_KV), jnp.tile(k_norm[layer, 2], N_KV)])[None, :]
        gq = jnp.broadcast_to(q_norm[layer][:, None], (HEAD_DIM, _IN_TM))
        convw = jnp.pad(conv_w[layer], ((0, 8 - CONV_K), (0, 0)))
        qT, gT, vsT, vwT, ks, kw, kcv, conv_n = _in_proj(
            x2, norm1[layer][None, :], wtok, wfeat, kgain, gq, convw, conv_out_norm[layer][None, :])

        wtop, wbot = _expand_cmp_w1(cmp_k_w1[layer], cmp_v_w1[layer])
        posk = jnp.broadcast_to(cmp_pos_k[layer].reshape(1, -1), (8, CMP_LEN * HEAD_DIM))
        posv = jnp.broadcast_to(cmp_pos_v[layer].reshape(1, -1), (8, CMP_LEN * HEAD_DIM))
        w2k = jnp.pad(cmp_k_w2[layer], ((0, 0), (0, LANES - HEAD_DIM))).astype(BF16)
        w2vT = cmp_v_w2[layer].T.astype(BF16)
        kcgain = jnp.pad(k_norm[layer, 0], (0, LANES - HEAD_DIM))[None, :]
        kc, vcT = _compress(kcv, wtop, wbot, posk, posv, cmp_k_w1[layer], cmp_v_w1[layer], w2k, w2vT, kcgain)

        gain_b = jnp.broadcast_to(attn_out_norm[layer].reshape(N_KV, GQA, HEAD_DIM).transpose(0, 2, 1)[:, :, :, None],
                                  (N_KV, HEAD_DIM, GQA, LANES)).reshape(N_KV, HEAD_DIM, GQA * LANES)
        attn_n = _attention(qT, gT, ks, vsT, kw, vwT, kc, vcT, tables, gain_b)

        wout = w_out[layer].astype(BF16)
        g2 = norm2[layer][None, :]
        i = layer // 2
        if layer % 2 == 0:
            x2 = _outproj_ffn(x2, attn_n, conv_n, wout, g2, ffn_w_gate[i].astype(BF16), ffn_w_up[i].astype(BF16),
                              ffn_w_down[i].astype(BF16))
        else:
            rw = jnp.pad(router_w[i], ((0, 0), (0, LANES - N_EXPERTS))).astype(BF16)
            rb = jnp.pad(router_b[i], (0, LANES - N_EXPERTS))[None, :]
            x1, h2, route, counts = _outproj_router(x2, attn_n, conv_n, wout, g2, rw, rb)
            x2 = _moe(h2, x1, route, counts, moe_w_gate[i].astype(BF16), moe_w_up[i].astype(BF16),
                      moe_w_down[i].astype(BF16))
    return x2.reshape(B, S, D_MODEL)
```

```python
import functools
import math

import numpy as np
import jax
import jax.numpy as jnp
from jax import lax
from jax.experimental import pallas as pl
from jax.experimental.pallas import tpu as pltpu

F32 = jnp.float32
BF16 = jnp.bfloat16
I32 = jnp.int32

D_MODEL = 1024
HEAD_DIM = 64
N_HEADS = 8
N_KV = 2
GQA = N_HEADS // N_KV
ATTN_W = N_HEADS * HEAD_DIM
KV_W = N_KV * HEAD_DIM
N_BRANCH = 3
CONV_W = 512
CONV_K = 3
CMP_LEN = 32
CMP_STRIDE = 16
CMP_HIDDEN = 128
SEL_LEN = 64
SEL_TOPK = 16
WINDOW = 512
Q_BLOCK = 128
N_BUCKETS = 32
MAX_DISTANCE = 128
D_FF = 2816
N_EXPERTS = 8
EPS = 1e-6
NEG = -1e30
FORCE = 1e9
MASK_BIG = 2.0 ** 60
SCALE = HEAD_DIM ** -0.5
LOG2E = math.log2(math.e)

LANES = 128
V_ROWS = 80
CHUNK = 512
BLOCKS_PER_CHUNK = CHUNK // SEL_LEN
TC_LEAD = 24
TC_ROWS = 88
TC_WIN = 48
CMP_STEP = 256
VMEM_LIMIT = 56 * 1024 * 1024


def _bucket_np(dist):
    n = np.maximum(dist, 0)
    max_exact = N_BUCKETS // 2
    nf = np.maximum(n, max_exact).astype(np.float64)
    v = np.log(nf / max_exact) / math.log(MAX_DISTANCE / max_exact) * (N_BUCKETS - max_exact)
    frac = np.abs(v - np.round(v))
    assert np.all((frac > 1e-6) | (n <= max_exact) | (n >= MAX_DISTANCE)), "bucket boundary is precision dependent"
    large = np.minimum(max_exact + (v + 1e-9).astype(np.int32), N_BUCKETS - 1)
    return np.where(n < max_exact, n, large).astype(np.int32)


def _index_tables():
    tl = np.arange(Q_BLOCK)[None, :]
    r = np.arange(WINDOW + Q_BLOCK + WINDOW)[:, None]
    d = tl + WINDOW - r
    idx_w = np.where((d >= 0) & (d < WINDOW), _bucket_np(d), -1)
    r = np.arange(3 * Q_BLOCK)[:, None]
    d = tl + Q_BLOCK - r
    idx_s = np.where(d >= 0, _bucket_np(d), -1)
    r = np.arange(TC_ROWS)[:, None] - TC_LEAD
    d = tl - CMP_STRIDE * r + (CMP_STRIDE * 16 - (CMP_LEN - 1))
    idx_c = np.where((d >= 0) & (r < 32), _bucket_np(d), -1)
    return idx_w.astype(np.int32), idx_s.astype(np.int32), idx_c.astype(np.int32)


def _tables_kernel(rb_ref, iw_ref, is_ref, ic_ref, tw_ref, ts_ref, tc_ref, b31_ref):
    head_lanes = [(h // GQA, slice((h % GQA) * LANES, (h % GQA + 1) * LANES)) for h in range(N_HEADS)]
    for idx_ref, out_ref in ((iw_ref, tw_ref), (is_ref, ts_ref), (ic_ref, tc_ref)):
        out_ref[...] = jnp.full(out_ref.shape, NEG, F32)

        def body(b, carry, idx_ref=idx_ref, out_ref=out_ref):
            hit = idx_ref[...] == b
            for h, (g, lanes) in enumerate(head_lanes):
                out_ref[g, :, lanes] = jnp.where(hit, rb_ref[b, h] * LOG2E, out_ref[g, :, lanes])
            return carry

        lax.fori_loop(0, N_BUCKETS, body, 0)
    for h, (g, lanes) in enumerate(head_lanes):
        far_bias = rb_ref[N_BUCKETS - 1, h] * LOG2E
        tc_ref[g, :, lanes] = jnp.where(ic_ref[...] >= 0, tc_ref[g, :, lanes] - far_bias, NEG)
        b31_ref[g, :, lanes] = jnp.full((8, LANES), far_bias, F32)


def _bias_tables(rel_bias):
    idx_w, idx_s, idx_c = _index_tables()
    width = GQA * LANES
    out_shape = (
        jax.ShapeDtypeStruct((N_KV, idx_w.shape[0], width), F32),
        jax.ShapeDtypeStruct((N_KV, idx_s.shape[0], width), F32),
        jax.ShapeDtypeStruct((N_KV, idx_c.shape[0], width), F32),
        jax.ShapeDtypeStruct((N_KV, 8, width), F32),
    )
    vmem = pl.BlockSpec(memory_space=pltpu.VMEM)
    return pl.pallas_call(
        _tables_kernel,
        out_shape=out_shape,
        in_specs=[pl.BlockSpec(memory_space=pltpu.SMEM), vmem, vmem, vmem],
        out_specs=(vmem, vmem, vmem, vmem),
        name="bias_tables",
    )(rel_bias, jnp.asarray(idx_w), jnp.asarray(idx_s), jnp.asarray(idx_c))


def _in_proj_kernel(x_ref, g1_ref, wtok_ref, wfeat_ref, ind_ref, kgain_ref, aug_ref, gq_ref, convw_ref, cgain_ref,
                    qT_ref, gT_ref, vsT_ref, vwT_ref, ks_ref, kw_ref, kcv_ref, convn_ref, zs_ref, carry_ref):
    tm = _IN_TM
    i = pl.program_id(0)

    @pl.when(i == 0)
    def _():
        carry_ref[...] = jnp.zeros(carry_ref.shape, F32)

    prev_tail = carry_ref[...]
    for part in range(_IN_SPLIT):
        rows = slice(part * tm, (part + 1) * tm)
        x = x_ref[rows, :]
        ms = jnp.mean(x * x, axis=-1, keepdims=True)
        h = (x * lax.rsqrt(ms + EPS) * g1_ref[...]).astype(BF16)
        tok = jnp.dot(h, wtok_ref[...], preferred_element_type=F32)
        feat = lax.dot_general(wfeat_ref[...], h, (((1,), (1,)), ((), ())),
                               preferred_element_type=F32)

        kcv_ref[rows, :] = tok[:, 0:2 * KV_W].astype(BF16)
        kk = tok[:, 2 * KV_W:4 * KV_W]
        sq = kk * kk
        sq_hi = sq.astype(BF16)
        sq_lo = (sq - sq_hi.astype(F32)).astype(BF16)
        ssq = jnp.dot(jnp.concatenate([sq_hi, sq_lo], axis=1), ind_ref[...],
                      preferred_element_type=F32)
        kn = kk * lax.rsqrt(ssq * (1.0 / HEAD_DIM) + EPS) * kgain_ref[...]
        ksl = kn[:, 0:KV_W]
        lane = lax.broadcasted_iota(I32, (tm, LANES), 1)
        aug = aug_ref[...]
        ks_ref[0, rows, :] = jnp.where(lane < HEAD_DIM, ksl, aug).astype(BF16)
        ks_ref[1, rows, :] = jnp.where(lane < HEAD_DIM, pltpu.roll(ksl, HEAD_DIM, 1), aug).astype(BF16)
        kw_ref[rows, :] = kn[:, KV_W:2 * KV_W].astype(BF16)

        c0 = 4 * KV_W
        cb = tok[:, c0:c0 + CONV_W]
        cc = tok[:, c0 + CONV_W:c0 + 2 * CONV_W]
        ch = tok[:, c0 + 2 * CONV_W:c0 + 3 * CONV_W]
        z = cc * ch
        zs_ref[part, 0:8, :] = prev_tail
        zs_ref[part, 8:8 + tm, :] = z
        z1 = zs_ref[part, 7:7 + tm, :]
        z2 = zs_ref[part, 6:6 + tm, :]
        w = convw_ref[...]
        y = w[0:1, :] * z2 + w[1:2, :] * z1 + w[2:3, :] * z
        prev_tail = z[tm - 8:tm, :]
        oc = cb * y
        msc = jnp.mean(oc * oc, axis=-1, keepdims=True)
        convn_ref[rows, :] = (oc * lax.rsqrt(msc + EPS) * cgain_ref[...]).astype(BF16)

        q = feat[0:ATTN_W].reshape(N_HEADS, HEAD_DIM, tm)
        qss = jnp.sum(q * q, axis=1, keepdims=True)
        qn = q * lax.rsqrt(qss * (1.0 / HEAD_DIM) + EPS) * gq_ref[...][None]
        qT_ref[:, rows] = (qn * (SCALE * LOG2E)).reshape(ATTN_W, tm).astype(BF16)
        ones_rows = (lax.broadcasted_iota(I32, (V_ROWS - HEAD_DIM, tm), 0) == 0).astype(BF16)
        for g in range(N_KV):
            r0 = ATTN_W + g * HEAD_DIM
            vsT_ref[g, 0:HEAD_DIM, rows] = feat[r0:r0 + HEAD_DIM].astype(BF16)
            vsT_ref[g, HEAD_DIM:V_ROWS, rows] = ones_rows
            r1 = ATTN_W + KV_W + g * HEAD_DIM
            vwT_ref[g, 0:HEAD_DIM, rows] = feat[r1:r1 + HEAD_DIM].astype(BF16)
            vwT_ref[g, HEAD_DIM:V_ROWS, rows] = ones_rows
        g0 = ATTN_W + 2 * KV_W
        gT_ref[:, rows] = jax.nn.sigmoid(feat[g0:g0 + N_BRANCH * N_HEADS])
    carry_ref[...] = prev_tail


_IN_SPLIT = 2
_IN_TM = 512


def _in_proj(x2, g1, wtok, wfeat, kgain, gq, convw, cgain):
    S = x2.shape[0]
    tm = _IN_TM
    tb = _IN_SPLIT * tm
    nt = S // tb
    ind = np.kron(np.eye(2 * N_KV, dtype=np.float32), np.ones((HEAD_DIM, HEAD_DIM), np.float32))
    ind = np.concatenate([ind, ind], axis=0)
    aug = np.zeros((tm, LANES), np.float32)
    blk = (np.arange(tm) // SEL_LEN) % BLOCKS_PER_CHUNK
    aug[np.arange(tm), HEAD_DIM + blk] = 1.0
    aug[:, HEAD_DIM + BLOCKS_PER_CHUNK:HEAD_DIM + BLOCKS_PER_CHUNK + 2] = 1.0
    const = lambda shape: pl.BlockSpec(shape, lambda i: (0,) * len(shape))
    out_shape = (
        jax.ShapeDtypeStruct((ATTN_W, S), BF16),
        jax.ShapeDtypeStruct((N_BRANCH * N_HEADS, S), F32),
        jax.ShapeDtypeStruct((N_KV, V_ROWS, S), BF16),
        jax.ShapeDtypeStruct((N_KV, V_ROWS, S), BF16),
        jax.ShapeDtypeStruct((N_KV, S, LANES), BF16),
        jax.ShapeDtypeStruct((S, LANES), BF16),
        jax.ShapeDtypeStruct((S, 2 * KV_W), BF16),
        jax.ShapeDtypeStruct((S, CONV_W), BF16),
    )
    out_specs = (
        pl.BlockSpec((ATTN_W, tb), lambda i: (0, i)),
        pl.BlockSpec((N_BRANCH * N_HEADS, tb), lambda i: (0, i)),
        pl.BlockSpec((N_KV, V_ROWS, tb), lambda i: (0, 0, i)),
        pl.BlockSpec((N_KV, V_ROWS, tb), lambda i: (0, 0, i)),
        pl.BlockSpec((N_KV, tb, LANES), lambda i: (0, i, 0)),
        pl.BlockSpec((tb, LANES), lambda i: (i, 0)),
        pl.BlockSpec((tb, 2 * KV_W), lambda i: (i, 0)),
        pl.BlockSpec((tb, CONV_W), lambda i: (i, 0)),
    )
    in_specs = [
        pl.BlockSpec((tb, D_MODEL), lambda i: (i, 0)),
        const((1, D_MODEL)),
        const(wtok.shape),
        const(wfeat.shape),
        const(ind.shape),
        const((1, 2 * KV_W)),
        const(aug.shape),
        const((HEAD_DIM, tm)),
        const((8, CONV_W)),
        const((1, CONV_W)),
    ]
    return pl.pallas_call(
        _in_proj_kernel,
        grid=(nt,),
        in_specs=in_specs,
        out_specs=out_specs,
        out_shape=out_shape,
        scratch_shapes=[pltpu.VMEM((_IN_SPLIT, tm + 8, CONV_W), F32), pltpu.VMEM((8, CONV_W), F32)],
        compiler_params=pltpu.CompilerParams(dimension_semantics=("arbitrary",), vmem_limit_bytes=VMEM_LIMIT),
        name="in_proj",
    )(x2, g1, wtok, wfeat, jnp.asarray(ind, BF16), kgain, jnp.asarray(aug), gq, convw, cgain)


def _compress_kernel(r_ref, wtop_ref, wbot_ref, posk_ref, posv_ref, w1k_ref, w1v_ref, w2k_ref, w2vT_ref, kgain_ref,
                     kc_ref, vcT_ref):
    nr = r_ref.shape[0]
    r = r_ref[...]
    u = jnp.dot(r, wtop_ref[...], preferred_element_type=F32)
    lo = jnp.dot(r, wbot_ref[...], preferred_element_type=F32)
    bias_k = jnp.dot(posk_ref[...], w1k_ref[...], preferred_element_type=F32)[0:1, :]
    bias_v = jnp.dot(posv_ref[...], w1v_ref[...], preferred_element_type=F32)[0:1, :]
    bias4 = jnp.concatenate([bias_k] * N_KV + [bias_v] * N_KV, axis=1)
    hid = u + pltpu.roll(lo, nr - 1, 0) + bias4
    act = jax.nn.gelu(hid)
    for g in range(N_KV):
        ak = act[:, g * CMP_HIDDEN:(g + 1) * CMP_HIDDEN].astype(BF16)
        av = act[:, (N_KV + g) * CMP_HIDDEN:(N_KV + g + 1) * CMP_HIDDEN].astype(BF16)
        kc = jnp.dot(ak, w2k_ref[...], preferred_element_type=F32)
        ssq = jnp.sum(kc * kc, axis=-1, keepdims=True)
        kc_ref[g] = (kc * lax.rsqrt(ssq * (1.0 / HEAD_DIM) + EPS) * kgain_ref[...]).astype(BF16)
        vcT_ref[g] = lax.dot_general(w2vT_ref[...], av, (((1,), (1,)), ((), ())),
                                     preferred_element_type=F32).astype(BF16)


def _compress(kcv, wtop, wbot, posk, posv, w1k, w1v, w2k, w2vT, kgain):
    S = kcv.shape[0]
    nr = S // CMP_STRIDE
    r = kcv.reshape(nr, CMP_STRIDE * 2 * KV_W)
    vmem = pl.BlockSpec(memory_space=pltpu.VMEM)
    return pl.pallas_call(
        _compress_kernel,
        out_shape=(jax.ShapeDtypeStruct((N_KV, nr, LANES), BF16),
                   jax.ShapeDtypeStruct((N_KV, HEAD_DIM, nr), BF16)),
        in_specs=[vmem] * 10,
        out_specs=(vmem, vmem),
        compiler_params=pltpu.CompilerParams(vmem_limit_bytes=VMEM_LIMIT),
        name="compress",
    )(r, wtop, wbot, posk, posv, w1k, w1v, w2k, w2vT, kgain)


def _tile4(row):
    return jnp.concatenate([row] * GQA, axis=1)


def _attn_kernel(qT_ref, gT_ref, ks_ref, vsT_ref, kw_ref, vwT_ref, kc_ref, vcT_ref,
                 tw_ref, ts_ref, tc_ref, b31_ref, gain_ref, out_ref,
                 sc_ref, sw_ref, imp_ref, aoct_ref, aall_ref, qs_ref, sa_ref, sb_ref, qz_ref, qsb_ref, fin_ref, ocmp_ref, pk_ref):
    ncr = kc_ref.shape[1]
    nsel = aall_ref.shape[1] * BLOCKS_PER_CHUNK
    noct = aoct_ref.shape[1] - 1
    width = GQA * LANES
    c = pl.program_id(0)
    t0 = c * Q_BLOCK
    tl = lax.broadcasted_iota(I32, (1, LANES), 1)
    cur = 2 * c + (tl >= SEL_LEN).astype(I32)
    zb = jnp.maximum(2 * c - 2, 0)
    zs = pl.multiple_of(zb * SEL_LEN, Q_BLOCK)
    ts_off = pl.multiple_of(zs - (t0 - Q_BLOCK), Q_BLOCK)
    zeros_q = jnp.zeros((HEAD_DIM, width), BF16)

    col_ok = _tile4(t0 + tl) >= CMP_LEN - 1
    j = lax.broadcasted_iota(I32, (nsel, LANES), 0)
    forced = (j == 0) | (j == cur) | (j == cur - 1)
    @pl.when(c == 0)
    def _():
        imp_ref[...] = jnp.zeros(imp_ref.shape, F32)


    ws = pl.multiple_of(jnp.maximum(t0 - WINDOW, 0), Q_BLOCK)
    tw_off = pl.multiple_of(ws - (t0 - WINDOW), Q_BLOCK)
    nw = WINDOW + Q_BLOCK
    lo = pl.multiple_of(jnp.clip((8 * c - 16) // 16 * 16, 0, ncr - TC_WIN), 16)
    tc_off = pl.multiple_of(lo - (8 * c - 16) + TC_LEAD, 8)
    qgs = [jnp.concatenate([qT_ref[(GQA * g + r) * HEAD_DIM:(GQA * g + r + 1) * HEAD_DIM, :]
                            for r in range(GQA)], axis=1) for g in range(N_KV)]

    for g in range(N_KV):
        qw = jnp.concatenate([qgs[g], zeros_q] if g == 0 else [zeros_q, qgs[g]], axis=0)
        sw_ref[g] = (jnp.dot(kw_ref[pl.ds(ws, nw), :], qw, preferred_element_type=F32)
                     + tw_ref[g, pl.ds(tw_off, nw), :])

    def compressed(rows):
        for g in range(N_KV):
            qc = jnp.concatenate([qgs[g], zeros_q], axis=0)
            s = jnp.dot(kc_ref[g, 0:rows, :], qc, preferred_element_type=F32)
            row = lax.broadcasted_iota(I32, (rows, width), 0)
            sc_ref[g, 0:rows, :] = jnp.where(row < lo, s, NEG)
            s_loc = jnp.dot(kc_ref[g, pl.ds(lo, TC_WIN), :], qc, preferred_element_type=F32)
            sc_ref[g, pl.ds(lo, TC_WIN), :] = s_loc + tc_ref[g, pl.ds(tc_off, TC_WIN), :]
        for g in range(N_KV):
            s = sc_ref[g, 0:rows, :]
            m = jnp.max(s, axis=0, keepdims=True)
            e = jnp.exp2(s - m)
            l = jnp.sum(e, axis=0, keepdims=True)
            p = e * jnp.where(col_ok, 1.0 / jnp.maximum(l, 1e-30), 0.0)
            ocmp_ref[g] = jnp.dot(vcT_ref[g, :, 0:rows], p.astype(BF16), preferred_element_type=F32)
            imp_ref[g, 8:8 + rows, :] = (
                (p[:, 0:LANES] + p[:, LANES:2 * LANES]) + p[:, 2 * LANES:3 * LANES] + p[:, 3 * LANES:4 * LANES])

    sizes = list(range(CMP_STEP, ncr, CMP_STEP)) + [ncr]
    variant = jnp.minimum((lo + TC_WIN + CMP_STEP - 1) // CMP_STEP, len(sizes)) - 1
    for v, rows in enumerate(sizes):
        pl.when(variant == v)(functools.partial(compressed, rows))

    o_cmp, scores0 = [], []
    for g in range(N_KV):
        o_cmp.append(ocmp_ref[g])
        isel = jnp.zeros((nsel, LANES), F32)
        for off in (0, -1, 1, 0, 2, 1, 3, 2):
            isel = isel + imp_ref[g, pl.ds(8 + off, nsel, stride=4), :]
        score = jnp.where(forced, -jnp.inf, isel)
        scores0.append(jnp.where(j > cur, NEG, score))

    o_win = []
    for g in range(N_KV):
        sw = sw_ref[g]
        mw = jnp.max(sw, axis=0, keepdims=True)
        pw = jnp.exp2(sw - mw)
        accw = jnp.dot(vwT_ref[g, :, pl.ds(ws, nw)], pw.astype(BF16), preferred_element_type=F32)
        o_win.append(accw[0:HEAD_DIM] * (1.0 / jnp.maximum(accw[HEAD_DIM:HEAD_DIM + 1], 1e-30)))

    n_picks = SEL_TOPK - 3
    fast = list(scores0)
    for _ in range(n_picks):
        for g in range(N_KV):
            mx = jnp.max(fast[g], axis=0, keepdims=True)
            fast[g] = jnp.where(fast[g] == mx, -jnp.inf, fast[g])
    eligible = (j <= cur) & jnp.logical_not(forced)
    n_eligible = cur + 1 - (1 + (cur >= 1).astype(I32) + (cur >= 2).astype(I32))
    expect = jnp.minimum(n_picks, n_eligible).astype(F32)
    wrong = jnp.zeros((1, LANES), F32)
    for g in range(N_KV):
        pk_ref[g] = fast[g]
        got = jnp.sum(jnp.where((fast[g] == -jnp.inf) & eligible, 1.0, 0.0), axis=0, keepdims=True)
        wrong = jnp.maximum(wrong, jnp.abs(got - expect))

    @pl.when(jnp.max(wrong) > 0.0)
    def _():
        def pick(_, score):
            mx = jnp.max(score, axis=0, keepdims=True)
            first = jnp.min(jnp.where(score == mx, j, nsel), axis=0, keepdims=True)
            return jnp.where(j == first, -jnp.inf, score)

        for g in range(N_KV):
            pk_ref[g] = lax.fori_loop(0, n_picks, pick, scores0[g])

    picks = [pk_ref[g] for g in range(N_KV)]

    m_init, acc_init, gates = [], [], []
    for g in range(N_KV):
        qg = qgs[g]
        b31row = b31_ref[g, 0:1, :]
        chosen = (picks[g] == -jnp.inf) & (j <= cur)
        a_all = _tile4(jnp.where(chosen, 0.0, -MASK_BIG)).reshape(noct, BLOCKS_PER_CHUNK, width)
        a_far = _tile4(jnp.where(chosen & (j < zb), 0.0, -MASK_BIG)).reshape(noct, BLOCKS_PER_CHUNK, width)
        b_hi = b31row.astype(BF16).astype(F32)
        erow = lax.broadcasted_iota(I32, (8, width), 0)
        extra = jnp.where(erow == 0, b_hi, jnp.where(erow == 1, b31row - b_hi, 0.0))
        aoct_ref[g, 0:noct] = jnp.concatenate(
            [a_far, jnp.broadcast_to(extra[None], (noct, 8, width))], axis=1).astype(BF16)
        aoct_ref[g, noct] = jnp.concatenate([jnp.full((8, width), -MASK_BIG, F32), extra], axis=0).astype(BF16)
        aall_ref[g] = jnp.concatenate([a_all, jnp.zeros((noct, 8, width), F32)], axis=1).astype(BF16)
        for q_ref in (qs_ref, qsb_ref):
            q_ref[g, 0:HEAD_DIM, :] = qg
            q_ref[g, HEAD_DIM + 16:LANES, :] = jnp.zeros((LANES - HEAD_DIM - 16, width), BF16)

        halves = []
        for h in range(2):
            qz_ref[2 * g + h, HEAD_DIM:HEAD_DIM + 16, :] = aall_ref[g, (zb + 2 * h) // BLOCKS_PER_CHUNK]
            qz_ref[2 * g + h, 0:HEAD_DIM, :] = qg
            qz_ref[2 * g + h, HEAD_DIM + 16:LANES, :] = jnp.zeros((LANES - HEAD_DIM - 16, width), BF16)
            sh = jnp.dot(ks_ref[g, pl.ds(zs + h * Q_BLOCK, Q_BLOCK), :], qz_ref[2 * g + h], preferred_element_type=F32)
            halves.append(sh + ts_ref[g, pl.ds(ts_off + h * Q_BLOCK, Q_BLOCK), :])
        sz = jnp.concatenate(halves, axis=0)
        mz = jnp.max(sz, axis=0, keepdims=True)
        pz = jnp.exp2(sz - mz)
        m_init.append(mz)
        acc_init.append(jnp.dot(vsT_ref[g, :, pl.ds(zs, 2 * Q_BLOCK)], pz.astype(BF16), preferred_element_type=F32))
        gates.append([jnp.concatenate([gT_ref[br * N_HEADS + GQA * g + r:br * N_HEADS + GQA * g + r + 1, :]
                                       for r in range(GQA)], axis=1) for br in range(N_BRANCH)])

    nfar = (zb + BLOCKS_PER_CHUNK - 1) // BLOCKS_PER_CHUNK

    nsub = 1
    sub = CHUNK // nsub

    def score(g, u, buf_ref, h, q_ref):
        k0 = pl.multiple_of(jnp.minimum(u, noct - 1) * CHUNK + h * sub, sub)
        s = jnp.dot(ks_ref[g, pl.ds(k0, sub), :], q_ref[g], preferred_element_type=F32)
        buf_ref[g, h * sub:(h + 1) * sub, :] = s
        return jnp.max(s, axis=0, keepdims=True)

    def set_mask_rows(g, u, q_ref):
        oct_id = jnp.where(u < nfar, u, noct)
        q_ref[g, HEAD_DIM:HEAD_DIM + 16, :] = aoct_ref[g, oct_id]

    def accumulate(g, u, buf_ref, h, mn, acc):
        k0 = pl.multiple_of(jnp.minimum(u, noct - 1) * CHUNK + h * sub, sub)
        p = jnp.exp2(buf_ref[g, h * sub:(h + 1) * sub, :] - mn)
        return acc + jnp.dot(vsT_ref[g, :, pl.ds(k0, sub)], p.astype(BF16), preferred_element_type=F32)

    def step(u, cur_ref, nxt_ref, q_ref, state, cms):
        out_state, out_cms = [], []
        for g in range(N_KV):
            m, acc = state[2 * g], state[2 * g + 1]
            mn = jnp.maximum(m, cms[g])
            acc = jnp.exp2(m - mn) * acc
            set_mask_rows(g, u + 1, q_ref)
            cm = None
            for h in range(nsub):
                ch = score(g, u + 1, nxt_ref, h, q_ref)
                cm = ch if cm is None else jnp.maximum(cm, ch)
                acc = accumulate(g, u, cur_ref, h, mn, acc)
            out_cms.append(cm)
            out_state += [mn, acc]
        return out_state, out_cms

    def far(i, carry):
        state, cm_b = step(2 * i, sa_ref, sb_ref, qsb_ref, carry[0:4], carry[4:6])
        state, cm_a = step(2 * i + 1, sb_ref, sa_ref, qs_ref, state, cm_b)
        return tuple(state + cm_a)

    first = []
    for g in range(N_KV):
        set_mask_rows(g, 0, qs_ref)
        cm = None
        for h in range(nsub):
            ch = score(g, 0, sa_ref, h, qs_ref)
            cm = ch if cm is None else jnp.maximum(cm, ch)
        first.append(cm)
    nchunks = jnp.maximum(nfar, 1)
    npairs = (nchunks - 1) // 2
    carry = lax.fori_loop(0, npairs, far, (m_init[0], acc_init[0], m_init[1], acc_init[1], first[0], first[1]))
    u_last = 2 * npairs

    def finish(two_left):
        for g in range(N_KV):
            m, acc, cm = carry[2 * g], carry[2 * g + 1], carry[4 + g]
            mn = jnp.maximum(m, cm)
            acc = jnp.exp2(m - mn) * acc
            if two_left:
                set_mask_rows(g, u_last + 1, qsb_ref)
                cm_b = score(g, u_last + 1, sb_ref, 0, qsb_ref)
            acc = accumulate(g, u_last, sa_ref, 0, mn, acc)
            if two_left:
                mn_b = jnp.maximum(mn, cm_b)
                acc = accumulate(g, u_last + 1, sb_ref, 0, mn_b, jnp.exp2(mn - mn_b) * acc)
            fin_ref[g] = acc

    pl.when(nchunks - u_last == 1)(lambda: finish(False))
    pl.when(nchunks - u_last == 2)(lambda: finish(True))
    carry = [None, fin_ref[0], None, fin_ref[1]]

    o_all, ssq = [], jnp.zeros((1, LANES), F32)
    for g in range(N_KV):
        acc = carry[2 * g + 1]
        o_sel = acc[0:HEAD_DIM] * (1.0 / jnp.maximum(acc[HEAD_DIM:HEAD_DIM + 1], 1e-30))
        o = gates[g][0] * o_cmp[g] + gates[g][1] * o_sel + gates[g][2] * o_win[g]
        o_all.append(o)
        cs = jnp.sum(o * o, axis=0, keepdims=True)
        ssq = ssq + ((cs[:, 0:LANES] + cs[:, LANES:2 * LANES]) + (cs[:, 2 * LANES:3 * LANES] + cs[:, 3 * LANES:]))
    inv = _tile4(lax.rsqrt(ssq * (1.0 / ATTN_W) + EPS))
    for g in range(N_KV):
        on = o_all[g] * inv * gain_ref[g]
        for k in range(GQA // 2):
            pair = jnp.concatenate([on[:, (2 * k) * LANES:(2 * k + 1) * LANES],
                                    on[:, (2 * k + 1) * LANES:(2 * k + 2) * LANES]], axis=0)
            col = (GQA * g + 2 * k) * HEAD_DIM
            out_ref[:, col:col + 2 * HEAD_DIM] = pair.T.astype(BF16)


def _attention(qT, gT, ks, vsT, kw, vwT, kc, vcT, tables, gain_b):
    S = qT.shape[1]
    nq = S // Q_BLOCK
    ncr = kc.shape[1]
    nsel = S // SEL_LEN
    noct = S // CHUNK
    width = GQA * LANES
    tw, ts, tc, b31 = tables
    vmem = pl.BlockSpec(memory_space=pltpu.VMEM)
    in_specs = [
        pl.BlockSpec((ATTN_W, Q_BLOCK), lambda c: (0, c)),
        pl.BlockSpec((N_BRANCH * N_HEADS, Q_BLOCK), lambda c: (0, c)),
    ] + [vmem] * 11
    return pl.pallas_call(
        _attn_kernel,
        grid=(nq,),
        in_specs=in_specs,
        out_specs=pl.BlockSpec((Q_BLOCK, ATTN_W), lambda c: (c, 0)),
        out_shape=jax.ShapeDtypeStruct((S, ATTN_W), BF16),
        scratch_shapes=[
            pltpu.VMEM((N_KV, ncr, width), F32),
            pltpu.VMEM((N_KV, WINDOW + Q_BLOCK, width), F32),
            pltpu.VMEM((N_KV, ncr + 16, LANES), F32),
            pltpu.VMEM((N_KV, noct + 1, 16, width), BF16),
            pltpu.VMEM((N_KV, noct, 16, width), BF16),
            pltpu.VMEM((N_KV, LANES, width), BF16),
            pltpu.VMEM((N_KV, CHUNK, width), F32),
            pltpu.VMEM((N_KV, CHUNK, width), F32),
            pltpu.VMEM((2 * N_KV, LANES, width), BF16),
            pltpu.VMEM((N_KV, LANES, width), BF16),
            pltpu.VMEM((N_KV, V_ROWS, width), F32),
            pltpu.VMEM((N_KV, HEAD_DIM, width), F32),
            pltpu.VMEM((N_KV, nsel, LANES), F32),
        ],
        compiler_params=pltpu.CompilerParams(dimension_semantics=("arbitrary",), vmem_limit_bytes=VMEM_LIMIT),
        name="nsa_attention",
    )(qT, gT, ks, vsT, kw, vwT, kc, vcT, tw, ts, tc, b31, gain_b)


def _mix_and_norm(x_ref, attn_ref, conv_ref, wout_ref, g2_ref):
    x1 = x_ref[...] + jnp.dot(attn_ref[...], wout_ref[0:ATTN_W, :], preferred_element_type=F32) \
        + jnp.dot(conv_ref[...], wout_ref[ATTN_W:, :], preferred_element_type=F32)
    ms = jnp.mean(x1 * x1, axis=-1, keepdims=True)
    h2 = x1 * lax.rsqrt(ms + EPS) * g2_ref[...]
    return x1, h2


def _ffn_kernel(x_ref, attn_ref, conv_ref, wout_ref, g2_ref, wg_ref, wu_ref, wd_ref, out_ref):
    x1, h2 = _mix_and_norm(x_ref, attn_ref, conv_ref, wout_ref, g2_ref)
    h2 = h2.astype(BF16)
    a = jnp.dot(h2, wg_ref[...], preferred_element_type=F32)
    u = jnp.dot(h2, wu_ref[...], preferred_element_type=F32)
    y = (a * jax.nn.sigmoid(a) * u).astype(BF16)
    out_ref[...] = x1 + jnp.dot(y, wd_ref[...], preferred_element_type=F32)


_FFN_TM = 512


def _outproj_ffn(x2, attn_n, conv_n, wout, g2, wg, wu, wd):
    S = x2.shape[0]
    tm = _FFN_TM
    resident = pl.BlockSpec(memory_space=pltpu.VMEM)
    return pl.pallas_call(
        _ffn_kernel,
        grid=(S // tm,),
        in_specs=[
            pl.BlockSpec((tm, D_MODEL), lambda i: (i, 0)),
            pl.BlockSpec((tm, ATTN_W), lambda i: (i, 0)),
            pl.BlockSpec((tm, CONV_W), lambda i: (i, 0)),
            resident, resident, resident, resident, resident,
        ],
        out_specs=pl.BlockSpec((tm, D_MODEL), lambda i: (i, 0)),
        out_shape=jax.ShapeDtypeStruct((S, D_MODEL), F32),
        compiler_params=pltpu.CompilerParams(dimension_semantics=("arbitrary",), vmem_limit_bytes=VMEM_LIMIT),
        name="outproj_ffn",
    )(x2, attn_n, conv_n, wout, g2, wg, wu, wd)


TOKEN_TILE = (D_MODEL // LANES, LANES)


def _to_token_tiles(ref, rows):
    x = jnp.stack([rows[:, k * LANES:(k + 1) * LANES] for k in range(TOKEN_TILE[0])], axis=0)
    ref[...] = pltpu.einshape('ktl->tkl', x)


def _from_token_tiles(ref):
    x = pltpu.einshape('tkl->ktl', ref[...])
    return jnp.concatenate([x[k] for k in range(TOKEN_TILE[0])], axis=1)


def _router_kernel(x_ref, attn_ref, conv_ref, wout_ref, g2_ref, rw_ref, rb_ref, tri_ref,
                   x1_ref, h2_ref, route_ref, cnt_ref, run_ref):
    tm = x_ref.shape[0]
    i = pl.program_id(0)

    @pl.when(i == 0)
    def _():
        run_ref[...] = jnp.zeros(run_ref.shape, F32)

    x1, h2 = _mix_and_norm(x_ref, attn_ref, conv_ref, wout_ref, g2_ref)
    x1_ref[...] = x1
    h2b = h2.astype(BF16)
    _to_token_tiles(h2_ref, h2b.astype(F32))
    lane = lax.broadcasted_iota(I32, (tm, LANES), 1)
    logits = jnp.dot(h2b, rw_ref[...], preferred_element_type=F32) + rb_ref[...]
    logits = jnp.where(lane < N_EXPERTS, logits, -jnp.inf)
    m1 = jnp.max(logits, axis=-1, keepdims=True)
    i1 = jnp.min(jnp.where(logits == m1, lane, LANES), axis=-1, keepdims=True)
    rest = jnp.where(lane == i1, -jnp.inf, logits)
    m2 = jnp.max(rest, axis=-1, keepdims=True)
    i2 = jnp.min(jnp.where(rest == m2, lane, LANES), axis=-1, keepdims=True)
    e2 = jnp.exp(m2 - m1)
    den = 1.0 + e2
    oh1 = (lane == i1).astype(F32)
    oh2 = (lane == i2).astype(F32)
    both = oh1 + oh2
    before = run_ref[0:1, :] + jnp.dot(tri_ref[...], both.astype(BF16), preferred_element_type=F32)
    rank1 = jnp.sum(before * oh1, axis=-1, keepdims=True)
    rank2 = jnp.sum(before * oh2, axis=-1, keepdims=True)
    fields = (i1.astype(F32), i2.astype(F32), rank1, rank2, 1.0 / den, e2 / den)
    route = jnp.zeros((tm, LANES), F32)
    for k, v in enumerate(fields):
        route = jnp.where(lane == k, v, route)
    route_ref[...] = route
    run_ref[...] = run_ref[...] + jnp.sum(both, axis=0, keepdims=True)
    cnt_ref[...] = run_ref[...]


def _outproj_router(x2, attn_n, conv_n, wout, g2, rw, rb):
    S = x2.shape[0]
    tm = _FFN_TM
    tri = np.tril(np.ones((tm, tm), np.float32), -1)
    const = lambda shape: pl.BlockSpec(shape, lambda i: (0,) * len(shape))
    return pl.pallas_call(
        _router_kernel,
        grid=(S // tm,),
        in_specs=[
            pl.BlockSpec((tm, D_MODEL), lambda i: (i, 0)),
            pl.BlockSpec((tm, ATTN_W), lambda i: (i, 0)),
            pl.BlockSpec((tm, CONV_W), lambda i: (i, 0)),
            const(wout.shape), const((1, D_MODEL)), const(rw.shape), const((1, LANES)), const((tm, tm)),
        ],
        out_specs=(pl.BlockSpec((tm, D_MODEL), lambda i: (i, 0)),
                   pl.BlockSpec((tm,) + TOKEN_TILE, lambda i: (i, 0, 0)),
                   pl.BlockSpec((tm, LANES), lambda i: (i, 0)),
                   const((8, LANES))),
        out_shape=(jax.ShapeDtypeStruct((S, D_MODEL), F32),
                   jax.ShapeDtypeStruct((S,) + TOKEN_TILE, F32),
                   jax.ShapeDtypeStruct((S, LANES), F32),
                   jax.ShapeDtypeStruct((8, LANES), F32)),
        scratch_shapes=[pltpu.VMEM((8, LANES), F32)],
        compiler_params=pltpu.CompilerParams(dimension_semantics=("arbitrary",), vmem_limit_bytes=VMEM_LIMIT),
        name="outproj_router",
    )(x2, attn_n, conv_n, wout, g2, rw, rb, jnp.asarray(tri, BF16))


_ROW_TM = 256
_EXP_TM = 512
_DMA_UNROLL = 8


def _row_copy(src_ref, src_row, dst_ref, dst_row, sem):
    return pltpu.make_async_copy(src_ref.at[src_row], dst_ref.at[dst_row], sem)


def _dispatch_kernel(pos_ref, ztile_ref, h_ref, xs_ref, hbuf_ref, zbuf_ref, sem, zsem):
    tm = h_ref.shape[0]
    i = pl.program_id(0)
    slot = i % 2

    @pl.when(i == 0)
    def _():
        zbuf_ref[...] = jnp.zeros(zbuf_ref.shape, F32)
        for z in range(ztile_ref.shape[0]):
            clear = pltpu.make_async_copy(zbuf_ref, xs_ref.at[pl.ds(ztile_ref[z] * _EXP_TM, _EXP_TM)], zsem)
            clear.start()
            clear.wait()

    hbuf_ref[slot] = h_ref[...]

    def issue(t, carry):
        for k in range(2):
            _row_copy(hbuf_ref.at[slot], t, xs_ref, pos_ref[0, 0, 2 * t + k], sem.at[slot]).start(priority=k)
        return carry

    lax.fori_loop(0, tm, issue, 0, unroll=_DMA_UNROLL)

    def drain(which):
        def body(t, carry):
            for k in range(2):
                _row_copy(hbuf_ref.at[which], 0, xs_ref, 0, sem.at[which]).wait()
            return carry
        lax.fori_loop(0, tm, body, 0, unroll=_DMA_UNROLL)

    @pl.when(i > 0)
    def _():
        drain(1 - slot)

    @pl.when(i == pl.num_programs(0) - 1)
    def _():
        drain(slot)


def _dispatch(pos3, pad_tiles, h2, n_rows):
    S = h2.shape[0]
    tm = _ROW_TM
    return pl.pallas_call(
        _dispatch_kernel,
        grid=(S // tm,),
        in_specs=[
            pl.BlockSpec((1, 1, 2 * tm), lambda i: (i, 0, 0), memory_space=pltpu.SMEM),
            pl.BlockSpec(memory_space=pltpu.SMEM),
            pl.BlockSpec((tm,) + TOKEN_TILE, lambda i: (i, 0, 0)),
        ],
        out_specs=pl.BlockSpec(memory_space=pl.ANY),
        out_shape=jax.ShapeDtypeStruct((n_rows,) + TOKEN_TILE, F32),
        scratch_shapes=[pltpu.VMEM((2, tm) + TOKEN_TILE, F32), pltpu.VMEM((_EXP_TM,) + TOKEN_TILE, F32),
                        pltpu.SemaphoreType.DMA((2,)), pltpu.SemaphoreType.DMA],
        compiler_params=pltpu.CompilerParams(dimension_semantics=("arbitrary",), vmem_limit_bytes=VMEM_LIMIT),
        name="moe_dispatch",
    )(pos3, pad_tiles, h2)


def _experts_kernel(te_ref, tb_ref, nt_ref, xs_ref, wg_ref, wu_ref, wd_ref, ys_ref):
    i = pl.program_id(0)

    @pl.when(i < nt_ref[0])
    def _():
        x = _from_token_tiles(xs_ref).astype(BF16)
        a = jnp.dot(x, wg_ref[0], preferred_element_type=F32)
        u = jnp.dot(x, wu_ref[0], preferred_element_type=F32)
        y = (a * jax.nn.sigmoid(a) * u).astype(BF16)
        _to_token_tiles(ys_ref, jnp.dot(y, wd_ref[0], preferred_element_type=F32))

    @pl.when(i >= nt_ref[0])
    def _():
        ys_ref[...] = jnp.zeros(ys_ref.shape, F32)


def _experts(tile_e, tile_b, n_tiles, xs, wg, wu, wd):
    n_rows = xs.shape[0]
    tm = _EXP_TM
    weights = lambda shape: pl.BlockSpec(shape, lambda i, te, tb, nt: (te[i], 0, 0), pipeline_mode=pl.Buffered(1))
    grid_spec = pltpu.PrefetchScalarGridSpec(
        num_scalar_prefetch=3,
        grid=(n_rows // tm,),
        in_specs=[
            pl.BlockSpec((tm,) + TOKEN_TILE, lambda i, te, tb, nt: (tb[i], 0, 0)),
            weights((1, D_MODEL, D_FF)), weights((1, D_MODEL, D_FF)), weights((1, D_FF, D_MODEL)),
        ],
        out_specs=pl.BlockSpec((tm,) + TOKEN_TILE, lambda i, te, tb, nt: (tb[i], 0, 0)),
    )
    return pl.pallas_call(
        _experts_kernel,
        grid_spec=grid_spec,
        out_shape=jax.ShapeDtypeStruct((n_rows,) + TOKEN_TILE, F32),
        compiler_params=pltpu.CompilerParams(dimension_semantics=("arbitrary",), vmem_limit_bytes=VMEM_LIMIT),
        name="moe_experts",
    )(tile_e, tile_b, n_tiles, xs, wg, wu, wd)


def _combine_kernel(pos_ref, nxt_ref, x1_ref, route_ref, ys_ref, out_ref, y1_ref, y2_ref, sem):
    tm = x1_ref.shape[0]
    i = pl.program_id(0)
    slot = i % 2

    def issue(src_pos_ref, which):
        def body(t, carry):
            _row_copy(ys_ref, src_pos_ref[0, 0, 2 * t], y1_ref.at[which], t, sem.at[which]).start(priority=0)
            _row_copy(ys_ref, src_pos_ref[0, 0, 2 * t + 1], y2_ref.at[which], t, sem.at[which]).start(priority=1)
            return carry
        lax.fori_loop(0, tm, body, 0, unroll=_DMA_UNROLL)

    @pl.when(i == 0)
    def _():
        issue(pos_ref, slot)

    @pl.when(i + 1 < pl.num_programs(0))
    def _():
        issue(nxt_ref, 1 - slot)

    def drain(t, carry):
        _row_copy(ys_ref, 0, y1_ref.at[slot], 0, sem.at[slot]).wait()
        _row_copy(ys_ref, 0, y2_ref.at[slot], 0, sem.at[slot]).wait()
        return carry

    lax.fori_loop(0, tm, drain, 0, unroll=_DMA_UNROLL)
    lane = lax.broadcasted_iota(I32, (tm, LANES), 1)
    route = route_ref[...]
    w1 = jnp.sum(jnp.where(lane == 4, route, 0.0), axis=-1, keepdims=True)
    w2 = jnp.sum(jnp.where(lane == 5, route, 0.0), axis=-1, keepdims=True)
    out_ref[...] = x1_ref[...] + (_from_token_tiles(y1_ref.at[slot]) * w1 + _from_token_tiles(y2_ref.at[slot]) * w2)


def _combine(pos3, x1, route, ys):
    S = x1.shape[0]
    tm = _ROW_TM
    last = S // tm - 1
    return pl.pallas_call(
        _combine_kernel,
        grid=(S // tm,),
        in_specs=[
            pl.BlockSpec((1, 1, 2 * tm), lambda i: (i, 0, 0), memory_space=pltpu.SMEM),
            pl.BlockSpec((1, 1, 2 * tm), lambda i: (jnp.minimum(i + 1, last), 0, 0), memory_space=pltpu.SMEM),
            pl.BlockSpec((tm, D_MODEL), lambda i: (i, 0)),
            pl.BlockSpec((tm, LANES), lambda i: (i, 0)),
            pl.BlockSpec(memory_space=pl.ANY),
        ],
        out_specs=pl.BlockSpec((tm, D_MODEL), lambda i: (i, 0)),
        out_shape=jax.ShapeDtypeStruct((S, D_MODEL), F32),
        scratch_shapes=[pltpu.VMEM((2, tm) + TOKEN_TILE, F32), pltpu.VMEM((2, tm) + TOKEN_TILE, F32),
                        pltpu.SemaphoreType.DMA((2,))],
        compiler_params=pltpu.CompilerParams(dimension_semantics=("arbitrary",), vmem_limit_bytes=VMEM_LIMIT),
        name="moe_combine",
    )(pos3, pos3, x1, route, ys)


def _moe(h2, x1, route, counts, wg, wu, wd):
    S = h2.shape[0]
    tm = _EXP_TM
    n_tiles_max = 2 * S // tm + N_EXPERTS
    cnt = counts[0, 0:N_EXPERTS].astype(I32)
    tiles = (cnt + tm - 1) // tm
    first = jnp.cumsum(tiles) - tiles
    n_tiles = jnp.sum(tiles)
    idx = jnp.arange(n_tiles_max, dtype=I32)
    last = jnp.minimum(idx, n_tiles - 1)
    tile_e = (jnp.sum(last[:, None] >= first[None, :], axis=1) - 1).astype(I32)
    eid = route[:, 0:2].astype(I32)
    pos = first[eid] * tm + route[:, 2:4].astype(I32)
    pos3 = pos.reshape(S // _ROW_TM, 1, 2 * _ROW_TM)
    pad_tiles = jnp.concatenate([jnp.maximum(first + tiles - 1, 0),
                                 jnp.minimum(n_tiles + jnp.arange(N_EXPERTS, dtype=I32), n_tiles_max - 1)])
    xs = _dispatch(pos3, pad_tiles.astype(I32), h2, n_tiles_max * tm)
    ys = _experts(tile_e, idx, n_tiles.reshape(1), xs, wg, wu, wd)
    return _combine(pos3, x1, route, ys)


def _split_w_in(w):
    o = np.cumsum([0, ATTN_W] + [KV_W] * 6 + [N_BRANCH * N_HEADS] + [CONV_W] * 3)
    q, kc, vc, ksl, vsl, kwn, vwn, gts, cb, cc, ch = (w[:, o[i]:o[i + 1]] for i in range(11))
    perm = np.array([h * N_BRANCH + br for br in range(N_BRANCH) for h in range(N_HEADS)])
    wtok = jnp.concatenate([kc, vc, ksl, kwn, cb, cc, ch], axis=1).astype(BF16)
    feat = jnp.concatenate([q, vsl, vwn, gts[:, perm], jnp.zeros((D_MODEL, 8), w.dtype)], axis=1)
    return wtok, feat.T.astype(BF16)


def _expand_cmp_w1(w1k, w1v):
    kinds = jnp.stack([w1k, w1v]).reshape(2, CMP_LEN, HEAD_DIM, CMP_HIDDEN)
    per_col = jnp.repeat(kinds, N_KV, axis=0)
    eye = jnp.eye(2 * N_KV, dtype=F32)
    out = []
    for l0 in (0, CMP_STRIDE):
        w = per_col[:, l0:l0 + CMP_STRIDE].transpose(1, 0, 2, 3)
        blk = w[:, :, :, None, :] * eye[None, :, None, :, None]
        out.append(blk.reshape(CMP_STRIDE * 2 * KV_W, 2 * N_KV * CMP_HIDDEN).astype(BF16))
    return out


def kernel(x, rel_bias, norm1, w_in, q_norm, k_norm, cmp_pos_k, cmp_pos_v, cmp_k_w1, cmp_k_w2, cmp_v_w1, cmp_v_w2,
           conv_w, attn_out_norm, conv_out_norm, w_out, norm2, ffn_w_gate, ffn_w_up, ffn_w_down, router_w, router_b,
           moe_w_gate, moe_w_up, moe_w_down):
    B, S, _ = x.shape
    assert B == 1 and S % CHUNK == 0 and S >= WINDOW + Q_BLOCK
    depth = norm1.shape[0]
    x2 = x.reshape(S, D_MODEL)
    tables = _bias_tables(rel_bias)
    for layer in range(depth):
        wtok, wfeat = _split_w_in(w_in[layer])
        kgain = jnp.concatenate([jnp.tile(k_norm[layer, 1], N_KV), jnp.tile(k_norm[layer, 2], N_KV)])[None, :]
        gq = jnp.broadcast_to(q_norm[layer][:, None], (HEAD_DIM, _IN_TM))
        convw = jnp.pad(conv_w[layer], ((0, 8 - CONV_K), (0, 0)))
        qT, gT, vsT, vwT, ks, kw, kcv, conv_n = _in_proj(
            x2, norm1[layer][None, :], wtok, wfeat, kgain, gq, convw, conv_out_norm[layer][None, :])

        wtop, wbot = _expand_cmp_w1(cmp_k_w1[layer], cmp_v_w1[layer])
        posk = jnp.broadcast_to(cmp_pos_k[layer].reshape(1, -1), (8, CMP_LEN * HEAD_DIM))
        posv = jnp.broadcast_to(cmp_pos_v[layer].reshape(1, -1), (8, CMP_LEN * HEAD_DIM))
        w2k = jnp.pad(cmp_k_w2[layer], ((0, 0), (0, LANES - HEAD_DIM))).astype(BF16)
        w2vT = cmp_v_w2[layer].T.astype(BF16)
        kcgain = jnp.pad(k_norm[layer, 0], (0, LANES - HEAD_DIM))[None, :]
        kc, vcT = _compress(kcv, wtop, wbot, posk, posv, cmp_k_w1[layer], cmp_v_w1[layer], w2k, w2vT, kcgain)

        gain_b = jnp.broadcast_to(attn_out_norm[layer].reshape(N_KV, GQA, HEAD_DIM).transpose(0, 2, 1)[:, :, :, None],
                                  (N_KV, HEAD_DIM, GQA, LANES)).reshape(N_KV, HEAD_DIM, GQA * LANES)
        attn_n = _attention(qT, gT, ks, vsT, kw, vwT, kc, vcT, tables, gain_b)

        wout = w_out[layer].astype(BF16)
        g2 = norm2[layer][None, :]
        i = layer // 2
        if layer % 2 == 0:
            x2 = _outproj_ffn(x2, attn_n, conv_n, wout, g2, ffn_w_gate[i].astype(BF16), ffn_w_up[i].astype(BF16),
                              ffn_w_down[i].astype(BF16))
        else:
            rw = jnp.pad(router_w[i], ((0, 0), (0, LANES - N_EXPERTS))).astype(BF16)
            rb = jnp.pad(router_b[i], (0, LANES - N_EXPERTS))[None, :]
            x1, h2, route, counts = _outproj_router(x2, attn_n, conv_n, wout, g2, rw, rb)
            x2 = _moe(h2, x1, route, counts, moe_w_gate[i].astype(BF16), moe_w_up[i].astype(BF16),
                      moe_w_down[i].astype(BF16))
    return x2.reshape(B, S, D_MODEL)
```

```python
import functools
import math

import numpy as np
import jax
import jax.numpy as jnp
from jax import lax
from jax.experimental import pallas as pl
from jax.experimental.pallas import tpu as pltpu

F32 = jnp.float32
BF16 = jnp.bfloat16
I32 = jnp.int32

D_MODEL = 1024
HEAD_DIM = 64
N_HEADS = 8
N_KV = 2
GQA = N_HEADS // N_KV
ATTN_W = N_HEADS * HEAD_DIM
KV_W = N_KV * HEAD_DIM
N_BRANCH = 3
CONV_W = 512
CONV_K = 3
CMP_LEN = 32
CMP_STRIDE = 16
CMP_HIDDEN = 128
SEL_LEN = 64
SEL_TOPK = 16
WINDOW = 512
Q_BLOCK = 128
N_BUCKETS = 32
MAX_DISTANCE = 128
D_FF = 2816
N_EXPERTS = 8
EPS = 1e-6
NEG = -1e30
MASK_BIG = 2.0 ** 60
MASK_ROWS = 16
SCALE = HEAD_DIM ** -0.5
LOG2E = math.log2(math.e)

LANES = 128
V_ROWS = 80
CHUNK = 512
BLOCKS_PER_CHUNK = CHUNK // SEL_LEN
TC_LEAD = 24
TC_ROWS = 88
TC_WIN = 48
CMP_STEP = 256
VMEM_LIMIT = 56 * 1024 * 1024


def _bucket_np(dist):
    n = np.maximum(dist, 0)
    max_exact = N_BUCKETS // 2
    nf = np.maximum(n, max_exact).astype(np.float64)
    v = np.log(nf / max_exact) / math.log(MAX_DISTANCE / max_exact) * (N_BUCKETS - max_exact)
    frac = np.abs(v - np.round(v))
    assert np.all((frac > 1e-6) | (n <= max_exact) | (n >= MAX_DISTANCE)), "bucket boundary is precision dependent"
    large = np.minimum(max_exact + (v + 1e-9).astype(np.int32), N_BUCKETS - 1)
    return np.where(n < max_exact, n, large).astype(np.int32)


def _index_tables():
    tl = np.arange(Q_BLOCK)[None, :]
    r = np.arange(WINDOW + Q_BLOCK + WINDOW)[:, None]
    d = tl + WINDOW - r
    idx_w = np.where((d >= 0) & (d < WINDOW), _bucket_np(d), -1)
    r = np.arange(3 * Q_BLOCK)[:, None]
    d = tl + Q_BLOCK - r
    idx_s = np.where(d >= 0, _bucket_np(d), -1)
    r = np.arange(TC_ROWS)[:, None] - TC_LEAD
    d = tl - CMP_STRIDE * r + (CMP_STRIDE * 16 - (CMP_LEN - 1))
    idx_c = np.where((d >= 0) & (r < 32), _bucket_np(d), -1)
    return idx_w.astype(np.int32), idx_s.astype(np.int32), idx_c.astype(np.int32)


def _tables_kernel(rb_ref, iw_ref, is_ref, ic_ref, tw_ref, ts_ref, tc_ref, b31_ref):
    head_lanes = [(h // GQA, slice((h % GQA) * LANES, (h % GQA + 1) * LANES)) for h in range(N_HEADS)]
    for idx_ref, out_ref in ((iw_ref, tw_ref), (is_ref, ts_ref), (ic_ref, tc_ref)):
        out_ref[...] = jnp.full(out_ref.shape, NEG, F32)

        def body(b, carry, idx_ref=idx_ref, out_ref=out_ref):
            hit = idx_ref[...] == b
            for h, (g, lanes) in enumerate(head_lanes):
                out_ref[g, :, lanes] = jnp.where(hit, rb_ref[b, h] * LOG2E, out_ref[g, :, lanes])
            return carry

        lax.fori_loop(0, N_BUCKETS, body, 0)
    for h, (g, lanes) in enumerate(head_lanes):
        far_bias = rb_ref[N_BUCKETS - 1, h] * LOG2E
        tc_ref[g, :, lanes] = jnp.where(ic_ref[...] >= 0, tc_ref[g, :, lanes] - far_bias, NEG)
        b31_ref[g, :, lanes] = jnp.full((8, LANES), far_bias, F32)


def _bias_tables(rel_bias):
    idx_w, idx_s, idx_c = _index_tables()
    width = GQA * LANES
    out_shape = (
        jax.ShapeDtypeStruct((N_KV, idx_w.shape[0], width), F32),
        jax.ShapeDtypeStruct((N_KV, idx_s.shape[0], width), F32),
        jax.ShapeDtypeStruct((N_KV, idx_c.shape[0], width), F32),
        jax.ShapeDtypeStruct((N_KV, 8, width), F32),
    )
    vmem = pl.BlockSpec(memory_space=pltpu.VMEM)
    return pl.pallas_call(
        _tables_kernel,
        out_shape=out_shape,
        in_specs=[pl.BlockSpec(memory_space=pltpu.SMEM), vmem, vmem, vmem],
        out_specs=(vmem, vmem, vmem, vmem),
        name="bias_tables",
    )(rel_bias, jnp.asarray(idx_w), jnp.asarray(idx_s), jnp.asarray(idx_c))


def _in_proj_kernel(x_ref, g1_ref, wtok_ref, wfeat_ref, ind_ref, kgain_ref, aug_ref, gq_ref, convw_ref, cgain_ref,
                    qT_ref, gT_ref, vsT_ref, vwT_ref, ks_ref, kw_ref, kcv_ref, convn_ref, zs_ref, carry_ref):
    tm = _IN_TM
    i = pl.program_id(0)

    @pl.when(i == 0)
    def _():
        carry_ref[...] = jnp.zeros(carry_ref.shape, F32)

    prev_tail = carry_ref[...]
    for part in range(_IN_SPLIT):
        rows = slice(part * tm, (part + 1) * tm)
        x = x_ref[rows, :]
        ms = jnp.mean(x * x, axis=-1, keepdims=True)
        h = (x * lax.rsqrt(ms + EPS) * g1_ref[...]).astype(BF16)
        tok = jnp.dot(h, wtok_ref[...], preferred_element_type=F32)
        feat = lax.dot_general(wfeat_ref[...], h, (((1,), (1,)), ((), ())),
                               preferred_element_type=F32)

        kcv_ref[rows, :] = tok[:, 0:2 * KV_W].astype(BF16)
        kk = tok[:, 2 * KV_W:4 * KV_W]
        sq = kk * kk
        sq_hi = sq.astype(BF16)
        sq_lo = (sq - sq_hi.astype(F32)).astype(BF16)
        ssq = jnp.dot(jnp.concatenate([sq_hi, sq_lo], axis=1), ind_ref[...],
                      preferred_element_type=F32)
        kn = kk * lax.rsqrt(ssq * (1.0 / HEAD_DIM) + EPS) * kgain_ref[...]
        ksl = kn[:, 0:KV_W]
        lane = lax.broadcasted_iota(I32, (tm, LANES), 1)
        aug = aug_ref[...]
        ks_ref[0, rows, :] = jnp.where(lane < HEAD_DIM, ksl, aug).astype(BF16)
        ks_ref[1, rows, :] = jnp.where(lane < HEAD_DIM, pltpu.roll(ksl, HEAD_DIM, 1), aug).astype(BF16)
        kw_ref[rows, :] = kn[:, KV_W:2 * KV_W].astype(BF16)

        c0 = 4 * KV_W
        cb = tok[:, c0:c0 + CONV_W]
        cc = tok[:, c0 + CONV_W:c0 + 2 * CONV_W]
        ch = tok[:, c0 + 2 * CONV_W:c0 + 3 * CONV_W]
        z = cc * ch
        zs_ref[part, 0:8, :] = prev_tail
        zs_ref[part, 8:8 + tm, :] = z
        z1 = zs_ref[part, 7:7 + tm, :]
        z2 = zs_ref[part, 6:6 + tm, :]
        w = convw_ref[...]
        y = w[0:1, :] * z2 + w[1:2, :] * z1 + w[2:3, :] * z
        prev_tail = z[tm - 8:tm, :]
        oc = cb * y
        msc = jnp.mean(oc * oc, axis=-1, keepdims=True)
        convn_ref[rows, :] = (oc * lax.rsqrt(msc + EPS) * cgain_ref[...]).astype(BF16)

        q = feat[0:ATTN_W].reshape(N_HEADS, HEAD_DIM, tm)
        qss = jnp.sum(q * q, axis=1, keepdims=True)
        qn = q * lax.rsqrt(qss * (1.0 / HEAD_DIM) + EPS) * gq_ref[...][None]
        qT_ref[:, rows] = (qn * (SCALE * LOG2E)).reshape(ATTN_W, tm).astype(BF16)
        ones_rows = (lax.broadcasted_iota(I32, (V_ROWS - HEAD_DIM, tm), 0) == 0).astype(BF16)
        for g in range(N_KV):
            r0 = ATTN_W + g * HEAD_DIM
            vsT_ref[g, 0:HEAD_DIM, rows] = feat[r0:r0 + HEAD_DIM].astype(BF16)
            vsT_ref[g, HEAD_DIM:V_ROWS, rows] = ones_rows
            r1 = ATTN_W + KV_W + g * HEAD_DIM
            vwT_ref[g, 0:HEAD_DIM, rows] = feat[r1:r1 + HEAD_DIM].astype(BF16)
            vwT_ref[g, HEAD_DIM:V_ROWS, rows] = ones_rows
        g0 = ATTN_W + 2 * KV_W
        gT_ref[:, rows] = jax.nn.sigmoid(feat[g0:g0 + N_BRANCH * N_HEADS])
    carry_ref[...] = prev_tail


_IN_SPLIT = 2
_IN_TM = 512


def _in_proj(x2, g1, wtok, wfeat, kgain, gq, convw, cgain):
    S = x2.shape[0]
    tm = _IN_TM
    tb = _IN_SPLIT * tm
    nt = S // tb
    ind = np.kron(np.eye(2 * N_KV, dtype=np.float32), np.ones((HEAD_DIM, HEAD_DIM), np.float32))
    ind = np.concatenate([ind, ind], axis=0)
    aug = np.zeros((tm, LANES), np.float32)
    blk = (np.arange(tm) // SEL_LEN) % BLOCKS_PER_CHUNK
    aug[np.arange(tm), HEAD_DIM + blk] = 1.0
    aug[:, HEAD_DIM + BLOCKS_PER_CHUNK:HEAD_DIM + BLOCKS_PER_CHUNK + 2] = 1.0
    const = lambda shape: pl.BlockSpec(shape, lambda i: (0,) * len(shape))
    out_shape = (
        jax.ShapeDtypeStruct((ATTN_W, S), BF16),
        jax.ShapeDtypeStruct((N_BRANCH * N_HEADS, S), F32),
        jax.ShapeDtypeStruct((N_KV, V_ROWS, S), BF16),
        jax.ShapeDtypeStruct((N_KV, V_ROWS, S), BF16),
        jax.ShapeDtypeStruct((N_KV, S, LANES), BF16),
        jax.ShapeDtypeStruct((S, LANES), BF16),
        jax.ShapeDtypeStruct((S, 2 * KV_W), BF16),
        jax.ShapeDtypeStruct((S, CONV_W), BF16),
    )
    out_specs = (
        pl.BlockSpec((ATTN_W, tb), lambda i: (0, i)),
        pl.BlockSpec((N_BRANCH * N_HEADS, tb), lambda i: (0, i)),
        pl.BlockSpec((N_KV, V_ROWS, tb), lambda i: (0, 0, i)),
        pl.BlockSpec((N_KV, V_ROWS, tb), lambda i: (0, 0, i)),
        pl.BlockSpec((N_KV, tb, LANES), lambda i: (0, i, 0)),
        pl.BlockSpec((tb, LANES), lambda i: (i, 0)),
        pl.BlockSpec((tb, 2 * KV_W), lambda i: (i, 0)),
        pl.BlockSpec((tb, CONV_W), lambda i: (i, 0)),
    )
    in_specs = [
        pl.BlockSpec((tb, D_MODEL), lambda i: (i, 0)),
        const((1, D_MODEL)),
        const(wtok.shape),
        const(wfeat.shape),
        const(ind.shape),
        const((1, 2 * KV_W)),
        const(aug.shape),
        const((HEAD_DIM, tm)),
        const((8, CONV_W)),
        const((1, CONV_W)),
    ]
    return pl.pallas_call(
        _in_proj_kernel,
        grid=(nt,),
        in_specs=in_specs,
        out_specs=out_specs,
        out_shape=out_shape,
        scratch_shapes=[pltpu.VMEM((_IN_SPLIT, tm + 8, CONV_W), F32), pltpu.VMEM((8, CONV_W), F32)],
        compiler_params=pltpu.CompilerParams(dimension_semantics=("arbitrary",), vmem_limit_bytes=VMEM_LIMIT),
        name="in_proj",
    )(x2, g1, wtok, wfeat, jnp.asarray(ind, BF16), kgain, jnp.asarray(aug), gq, convw, cgain)


def _compress_kernel(r_ref, wtop_ref, wbot_ref, posk_ref, posv_ref, w1k_ref, w1v_ref, w2k_ref, w2vT_ref, kgain_ref,
                     kc_ref, vcT_ref):
    nr = r_ref.shape[0]
    r = r_ref[...]
    u = jnp.dot(r, wtop_ref[...], preferred_element_type=F32)
    lo = jnp.dot(r, wbot_ref[...], preferred_element_type=F32)
    bias_k = jnp.dot(posk_ref[...], w1k_ref[...], preferred_element_type=F32)[0:1, :]
    bias_v = jnp.dot(posv_ref[...], w1v_ref[...], preferred_element_type=F32)[0:1, :]
    bias4 = jnp.concatenate([bias_k] * N_KV + [bias_v] * N_KV, axis=1)
    hid = u + pltpu.roll(lo, nr - 1, 0) + bias4
    act = jax.nn.gelu(hid)
    for g in range(N_KV):
        ak = act[:, g * CMP_HIDDEN:(g + 1) * CMP_HIDDEN].astype(BF16)
        av = act[:, (N_KV + g) * CMP_HIDDEN:(N_KV + g + 1) * CMP_HIDDEN].astype(BF16)
        kc = jnp.dot(ak, w2k_ref[...], preferred_element_type=F32)
        ssq = jnp.sum(kc * kc, axis=-1, keepdims=True)
        kc_ref[g] = (kc * lax.rsqrt(ssq * (1.0 / HEAD_DIM) + EPS) * kgain_ref[...]).astype(BF16)
        vcT_ref[g] = lax.dot_general(w2vT_ref[...], av, (((1,), (1,)), ((), ())),
                                     preferred_element_type=F32).astype(BF16)


def _compress(kcv, wtop, wbot, posk, posv, w1k, w1v, w2k, w2vT, kgain):
    S = kcv.shape[0]
    nr = S // CMP_STRIDE
    r = kcv.reshape(nr, CMP_STRIDE * 2 * KV_W)
    vmem = pl.BlockSpec(memory_space=pltpu.VMEM)
    return pl.pallas_call(
        _compress_kernel,
        out_shape=(jax.ShapeDtypeStruct((N_KV, nr, LANES), BF16),
                   jax.ShapeDtypeStruct((N_KV, HEAD_DIM, nr), BF16)),
        in_specs=[vmem] * 10,
        out_specs=(vmem, vmem),
        compiler_params=pltpu.CompilerParams(vmem_limit_bytes=VMEM_LIMIT),
        name="compress",
    )(r, wtop, wbot, posk, posv, w1k, w1v, w2k, w2vT, kgain)


def _tile4(row):
    return jnp.concatenate([row] * GQA, axis=1)


def _attn_kernel(qT_ref, gT_ref, ks_ref, vsT_ref, kw_ref, vwT_ref, kc_ref, vcT_ref,
                 tw_ref, ts_ref, tc_ref, b31_ref, gain_ref, out_ref,
                 sc_ref, sw_ref, imp_ref, aoct_ref, aall_ref, qs_ref, sa_ref, sb_ref, qz_ref, qsb_ref, fin_ref, ocmp_ref, pk_ref):
    ncr = kc_ref.shape[1]
    nsel = aall_ref.shape[1] * BLOCKS_PER_CHUNK
    noct = aoct_ref.shape[1] - 1
    width = GQA * LANES
    c = pl.program_id(0)
    t0 = c * Q_BLOCK
    tl = lax.broadcasted_iota(I32, (1, LANES), 1)
    cur = 2 * c + (tl >= SEL_LEN).astype(I32)
    zb = jnp.maximum(2 * c - 2, 0)
    zs = pl.multiple_of(zb * SEL_LEN, Q_BLOCK)
    ts_off = pl.multiple_of(zs - (t0 - Q_BLOCK), Q_BLOCK)
    zeros_q = jnp.zeros((HEAD_DIM, width), BF16)

    col_ok = _tile4(t0 + tl) >= CMP_LEN - 1
    j = lax.broadcasted_iota(I32, (nsel, LANES), 0)
    forced = (j == 0) | (j == cur) | (j == cur - 1)
    @pl.when(c == 0)
    def _():
        imp_ref[...] = jnp.zeros(imp_ref.shape, F32)


    ws = pl.multiple_of(jnp.maximum(t0 - WINDOW, 0), Q_BLOCK)
    tw_off = pl.multiple_of(ws - (t0 - WINDOW), Q_BLOCK)
    nw = WINDOW + Q_BLOCK
    lo = pl.multiple_of(jnp.clip((8 * c - 16) // 16 * 16, 0, ncr - TC_WIN), 16)
    tc_off = pl.multiple_of(lo - (8 * c - 16) + TC_LEAD, 8)
    qgs = [jnp.concatenate([qT_ref[(GQA * g + r) * HEAD_DIM:(GQA * g + r + 1) * HEAD_DIM, :]
                            for r in range(GQA)], axis=1) for g in range(N_KV)]

    for g in range(N_KV):
        qw = jnp.concatenate([qgs[g], zeros_q] if g == 0 else [zeros_q, qgs[g]], axis=0)
        sw_ref[g] = (jnp.dot(kw_ref[pl.ds(ws, nw), :], qw, preferred_element_type=F32)
                     + tw_ref[g, pl.ds(tw_off, nw), :])

    def compressed(rows):
        for g in range(N_KV):
            qc = jnp.concatenate([qgs[g], zeros_q], axis=0)
            s = jnp.dot(kc_ref[g, 0:rows, :], qc, preferred_element_type=F32)
            row = lax.broadcasted_iota(I32, (rows, width), 0)
            sc_ref[g, 0:rows, :] = jnp.where(row < lo, s, NEG)
            s_loc = jnp.dot(kc_ref[g, pl.ds(lo, TC_WIN), :], qc, preferred_element_type=F32)
            sc_ref[g, pl.ds(lo, TC_WIN), :] = s_loc + tc_ref[g, pl.ds(tc_off, TC_WIN), :]
        for g in range(N_KV):
            s = sc_ref[g, 0:rows, :]
            m = jnp.max(s, axis=0, keepdims=True)
            e = jnp.exp2(s - m)
            l = jnp.sum(e, axis=0, keepdims=True)
            p = e * jnp.where(col_ok, 1.0 / jnp.maximum(l, 1e-30), 0.0)
            ocmp_ref[g] = jnp.dot(vcT_ref[g, :, 0:rows], p.astype(BF16), preferred_element_type=F32)
            imp_ref[g, 8:8 + rows, :] = (
                (p[:, 0:LANES] + p[:, LANES:2 * LANES]) + p[:, 2 * LANES:3 * LANES] + p[:, 3 * LANES:4 * LANES])

    sizes = list(range(CMP_STEP, ncr, CMP_STEP)) + [ncr]
    variant = jnp.minimum((lo + TC_WIN + CMP_STEP - 1) // CMP_STEP, len(sizes)) - 1
    for v, rows in enumerate(sizes):
        pl.when(variant == v)(functools.partial(compressed, rows))

    o_cmp, scores0 = [], []
    for g in range(N_KV):
        o_cmp.append(ocmp_ref[g])
        isel = jnp.zeros((nsel, LANES), F32)
        for off in (0, -1, 1, 0, 2, 1, 3, 2):
            isel = isel + imp_ref[g, pl.ds(8 + off, nsel, stride=4), :]
        score = jnp.where(forced, -jnp.inf, isel)
        scores0.append(jnp.where(j > cur, NEG, score))

    o_win = []
    for g in range(N_KV):
        sw = sw_ref[g]
        mw = jnp.max(sw, axis=0, keepdims=True)
        pw = jnp.exp2(sw - mw)
        accw = jnp.dot(vwT_ref[g, :, pl.ds(ws, nw)], pw.astype(BF16), preferred_element_type=F32)
        o_win.append(accw[0:HEAD_DIM] * (1.0 / jnp.maximum(accw[HEAD_DIM:HEAD_DIM + 1], 1e-30)))

    n_picks = SEL_TOPK - 3
    fast = list(scores0)
    for _ in range(n_picks):
        for g in range(N_KV):
            mx = jnp.max(fast[g], axis=0, keepdims=True)
            fast[g] = jnp.where(fast[g] == mx, -jnp.inf, fast[g])
    eligible = (j <= cur) & jnp.logical_not(forced)
    n_eligible = cur + 1 - (1 + (cur >= 1).astype(I32) + (cur >= 2).astype(I32))
    expect = jnp.minimum(n_picks, n_eligible).astype(F32)
    wrong = jnp.zeros((1, LANES), F32)
    for g in range(N_KV):
        pk_ref[g] = fast[g]
        got = jnp.sum(jnp.where((fast[g] == -jnp.inf) & eligible, 1.0, 0.0), axis=0, keepdims=True)
        wrong = jnp.maximum(wrong, jnp.abs(got - expect))

    @pl.when(jnp.max(wrong) > 0.0)
    def _():
        def pick(_, score):
            mx = jnp.max(score, axis=0, keepdims=True)
            first = jnp.min(jnp.where(score == mx, j, nsel), axis=0, keepdims=True)
            return jnp.where(j == first, -jnp.inf, score)

        for g in range(N_KV):
            pk_ref[g] = lax.fori_loop(0, n_picks, pick, scores0[g])

    picks = [pk_ref[g] for g in range(N_KV)]

    m_init, acc_init, gates = [], [], []
    for g in range(N_KV):
        qg = qgs[g]
        b31row = b31_ref[g, 0:1, :]
        chosen = (picks[g] == -jnp.inf) & (j <= cur)
        a_all = _tile4(jnp.where(chosen, 0.0, -MASK_BIG)).reshape(noct, BLOCKS_PER_CHUNK, width)
        a_far = _tile4(jnp.where(chosen & (j < zb), 0.0, -MASK_BIG)).reshape(noct, BLOCKS_PER_CHUNK, width)
        b_hi = b31row.astype(BF16).astype(F32)
        erow = lax.broadcasted_iota(I32, (8, width), 0)
        extra = jnp.where(erow == 0, b_hi, jnp.where(erow == 1, b31row - b_hi, 0.0))
        aoct_ref[g, 0:noct] = jnp.concatenate(
            [a_far, jnp.broadcast_to(extra[None], (noct, 8, width))], axis=1).astype(BF16)
        aoct_ref[g, noct] = jnp.concatenate([jnp.full((8, width), -MASK_BIG, F32), extra], axis=0).astype(BF16)
        aall_ref[g] = jnp.concatenate([a_all, jnp.zeros((noct, 8, width), F32)], axis=1).astype(BF16)
        for q_ref in (qs_ref, qsb_ref):
            q_ref[g, 0:HEAD_DIM, :] = qg
            q_ref[g, HEAD_DIM + MASK_ROWS:LANES, :] = jnp.zeros((LANES - HEAD_DIM - MASK_ROWS, width), BF16)

        halves = []
        for h in range(2):
            qz_ref[2 * g + h, HEAD_DIM:HEAD_DIM + MASK_ROWS, :] = aall_ref[g, (zb + 2 * h) // BLOCKS_PER_CHUNK]
            qz_ref[2 * g + h, 0:HEAD_DIM, :] = qg
            qz_ref[2 * g + h, HEAD_DIM + MASK_ROWS:LANES, :] = jnp.zeros((LANES - HEAD_DIM - MASK_ROWS, width), BF16)
            sh = jnp.dot(ks_ref[g, pl.ds(zs + h * Q_BLOCK, Q_BLOCK), :], qz_ref[2 * g + h], preferred_element_type=F32)
            halves.append(sh + ts_ref[g, pl.ds(ts_off + h * Q_BLOCK, Q_BLOCK), :])
        sz = jnp.concatenate(halves, axis=0)
        mz = jnp.max(sz, axis=0, keepdims=True)
        pz = jnp.exp2(sz - mz)
        m_init.append(mz)
        acc_init.append(jnp.dot(vsT_ref[g, :, pl.ds(zs, 2 * Q_BLOCK)], pz.astype(BF16), preferred_element_type=F32))
        gates.append([jnp.concatenate([gT_ref[br * N_HEADS + GQA * g + r:br * N_HEADS + GQA * g + r + 1, :]
                                       for r in range(GQA)], axis=1) for br in range(N_BRANCH)])

    nfar = (zb + BLOCKS_PER_CHUNK - 1) // BLOCKS_PER_CHUNK

    nsub = 1
    sub = CHUNK // nsub

    def score(g, u, buf_ref, h, q_ref):
        k0 = pl.multiple_of(jnp.minimum(u, noct - 1) * CHUNK + h * sub, sub)
        s = jnp.dot(ks_ref[g, pl.ds(k0, sub), :], q_ref[g], preferred_element_type=F32)
        buf_ref[g, h * sub:(h + 1) * sub, :] = s
        return jnp.max(s, axis=0, keepdims=True)

    def set_mask_rows(g, u, q_ref):
        oct_id = jnp.where(u < nfar, u, noct)
        q_ref[g, HEAD_DIM:HEAD_DIM + MASK_ROWS, :] = aoct_ref[g, oct_id]

    def accumulate(g, u, buf_ref, h, mn, acc):
        k0 = pl.multiple_of(jnp.minimum(u, noct - 1) * CHUNK + h * sub, sub)
        p = jnp.exp2(buf_ref[g, h * sub:(h + 1) * sub, :] - mn)
        return acc + jnp.dot(vsT_ref[g, :, pl.ds(k0, sub)], p.astype(BF16), preferred_element_type=F32)

    def step(u, cur_ref, nxt_ref, q_ref, state, cms):
        out_state, out_cms = [], []
        for g in range(N_KV):
            m, acc = state[2 * g], state[2 * g + 1]
            mn = jnp.maximum(m, cms[g])
            acc = jnp.exp2(m - mn) * acc
            set_mask_rows(g, u + 1, q_ref)
            cm = None
            for h in range(nsub):
                ch = score(g, u + 1, nxt_ref, h, q_ref)
                cm = ch if cm is None else jnp.maximum(cm, ch)
                acc = accumulate(g, u, cur_ref, h, mn, acc)
            out_cms.append(cm)
            out_state += [mn, acc]
        return out_state, out_cms

    def far(i, carry):
        state, cm_b = step(2 * i, sa_ref, sb_ref, qsb_ref, carry[0:4], carry[4:6])
        state, cm_a = step(2 * i + 1, sb_ref, sa_ref, qs_ref, state, cm_b)
        return tuple(state + cm_a)

    first = []
    for g in range(N_KV):
        set_mask_rows(g, 0, qs_ref)
        cm = None
        for h in range(nsub):
            ch = score(g, 0, sa_ref, h, qs_ref)
            cm = ch if cm is None else jnp.maximum(cm, ch)
        first.append(cm)
    nchunks = jnp.maximum(nfar, 1)
    npairs = (nchunks - 1) // 2
    carry = lax.fori_loop(0, npairs, far, (m_init[0], acc_init[0], m_init[1], acc_init[1], first[0], first[1]))
    u_last = 2 * npairs

    def finish(two_left):
        for g in range(N_KV):
            m, acc, cm = carry[2 * g], carry[2 * g + 1], carry[4 + g]
            mn = jnp.maximum(m, cm)
            acc = jnp.exp2(m - mn) * acc
            if two_left:
                set_mask_rows(g, u_last + 1, qsb_ref)
                cm_b = score(g, u_last + 1, sb_ref, 0, qsb_ref)
            acc = accumulate(g, u_last, sa_ref, 0, mn, acc)
            if two_left:
                mn_b = jnp.maximum(mn, cm_b)
                acc = accumulate(g, u_last + 1, sb_ref, 0, mn_b, jnp.exp2(mn - mn_b) * acc)
            fin_ref[g] = acc

    pl.when(nchunks - u_last == 1)(lambda: finish(False))
    pl.when(nchunks - u_last == 2)(lambda: finish(True))
    carry = [None, fin_ref[0], None, fin_ref[1]]

    o_all, ssq = [], jnp.zeros((1, LANES), F32)
    for g in range(N_KV):
        acc = carry[2 * g + 1]
        o_sel = acc[0:HEAD_DIM] * (1.0 / jnp.maximum(acc[HEAD_DIM:HEAD_DIM + 1], 1e-30))
        o = gates[g][0] * o_cmp[g] + gates[g][1] * o_sel + gates[g][2] * o_win[g]
        o_all.append(o)
        cs = jnp.sum(o * o, axis=0, keepdims=True)
        ssq = ssq + ((cs[:, 0:LANES] + cs[:, LANES:2 * LANES]) + (cs[:, 2 * LANES:3 * LANES] + cs[:, 3 * LANES:]))
    inv = _tile4(lax.rsqrt(ssq * (1.0 / ATTN_W) + EPS))
    for g in range(N_KV):
        on = o_all[g] * inv * gain_ref[g]
        for k in range(GQA // 2):
            pair = jnp.concatenate([on[:, (2 * k) * LANES:(2 * k + 1) * LANES],
                                    on[:, (2 * k + 1) * LANES:(2 * k + 2) * LANES]], axis=0)
            col = (GQA * g + 2 * k) * HEAD_DIM
            out_ref[:, col:col + 2 * HEAD_DIM] = pair.T.astype(BF16)


def _attention(qT, gT, ks, vsT, kw, vwT, kc, vcT, tables, gain_b):
    S = qT.shape[1]
    nq = S // Q_BLOCK
    ncr = kc.shape[1]
    nsel = S // SEL_LEN
    noct = S // CHUNK
    width = GQA * LANES
    tw, ts, tc, b31 = tables
    vmem = pl.BlockSpec(memory_space=pltpu.VMEM)
    in_specs = [
        pl.BlockSpec((ATTN_W, Q_BLOCK), lambda c: (0, c)),
        pl.BlockSpec((N_BRANCH * N_HEADS, Q_BLOCK), lambda c: (0, c)),
    ] + [vmem] * 11
    return pl.pallas_call(
        _attn_kernel,
        grid=(nq,),
        in_specs=in_specs,
        out_specs=pl.BlockSpec((Q_BLOCK, ATTN_W), lambda c: (c, 0)),
        out_shape=jax.ShapeDtypeStruct((S, ATTN_W), BF16),
        scratch_shapes=[
            pltpu.VMEM((N_KV, ncr, width), F32),
            pltpu.VMEM((N_KV, WINDOW + Q_BLOCK, width), F32),
            pltpu.VMEM((N_KV, ncr + 16, LANES), F32),
            pltpu.VMEM((N_KV, noct + 1, MASK_ROWS, width), BF16),
            pltpu.VMEM((N_KV, noct, MASK_ROWS, width), BF16),
            pltpu.VMEM((N_KV, LANES, width), BF16),
            pltpu.VMEM((N_KV, CHUNK, width), F32),
            pltpu.VMEM((N_KV, CHUNK, width), F32),
            pltpu.VMEM((2 * N_KV, LANES, width), BF16),
            pltpu.VMEM((N_KV, LANES, width), BF16),
            pltpu.VMEM((N_KV, V_ROWS, width), F32),
            pltpu.VMEM((N_KV, HEAD_DIM, width), F32),
            pltpu.VMEM((N_KV, nsel, LANES), F32),
        ],
        compiler_params=pltpu.CompilerParams(dimension_semantics=("arbitrary",), vmem_limit_bytes=VMEM_LIMIT),
        name="nsa_attention",
    )(qT, gT, ks, vsT, kw, vwT, kc, vcT, tw, ts, tc, b31, gain_b)


def _mix_and_norm(x_ref, attn_ref, conv_ref, wout_ref, g2_ref, rows=slice(None)):
    x1 = x_ref[rows, :] + jnp.dot(attn_ref[rows, :], wout_ref[0:ATTN_W, :], preferred_element_type=F32) \
        + jnp.dot(conv_ref[rows, :], wout_ref[ATTN_W:, :], preferred_element_type=F32)
    ms = jnp.mean(x1 * x1, axis=-1, keepdims=True)
    h2 = x1 * lax.rsqrt(ms + EPS) * g2_ref[...]
    return x1, h2


def _ffn_kernel(x_ref, attn_ref, conv_ref, wout_ref, g2_ref, wg_ref, wu_ref, wd_ref, out_ref):
    x1, h2 = _mix_and_norm(x_ref, attn_ref, conv_ref, wout_ref, g2_ref)
    h2 = h2.astype(BF16)
    a = jnp.dot(h2, wg_ref[...], preferred_element_type=F32)
    u = jnp.dot(h2, wu_ref[...], preferred_element_type=F32)
    y = (a * jax.nn.sigmoid(a) * u).astype(BF16)
    out_ref[...] = x1 + jnp.dot(y, wd_ref[...], preferred_element_type=F32)


_FFN_TM = 512


def _outproj_ffn(x2, attn_n, conv_n, wout, g2, wg, wu, wd):
    S = x2.shape[0]
    tm = _FFN_TM
    resident = pl.BlockSpec(memory_space=pltpu.VMEM)
    return pl.pallas_call(
        _ffn_kernel,
        grid=(S // tm,),
        in_specs=[
            pl.BlockSpec((tm, D_MODEL), lambda i: (i, 0)),
            pl.BlockSpec((tm, ATTN_W), lambda i: (i, 0)),
            pl.BlockSpec((tm, CONV_W), lambda i: (i, 0)),
            resident, resident, resident, resident, resident,
        ],
        out_specs=pl.BlockSpec((tm, D_MODEL), lambda i: (i, 0)),
        out_shape=jax.ShapeDtypeStruct((S, D_MODEL), F32),
        compiler_params=pltpu.CompilerParams(dimension_semantics=("arbitrary",), vmem_limit_bytes=VMEM_LIMIT),
        name="outproj_ffn",
    )(x2, attn_n, conv_n, wout, g2, wg, wu, wd)


TOKEN_TILE = (D_MODEL // LANES, LANES)


def _to_token_tiles(ref, rows):
    x = jnp.stack([rows[:, k * LANES:(k + 1) * LANES] for k in range(TOKEN_TILE[0])], axis=0)
    ref[...] = pltpu.einshape('ktl->tkl', x)


def _from_token_tiles(ref):
    x = pltpu.einshape('tkl->ktl', ref[...])
    return jnp.concatenate([x[k] for k in range(TOKEN_TILE[0])], axis=1)


_ROUTER_SPLIT = 2


def _router_kernel(x_ref, attn_ref, conv_ref, wout_ref, g2_ref, rw_ref, rb_ref, tri_ref,
                   x1_ref, h2_ref, route_ref, cnt_ref, run_ref):
    tm = _FFN_TM
    i = pl.program_id(0)

    @pl.when(i == 0)
    def _():
        run_ref[...] = jnp.zeros(run_ref.shape, F32)

    running = run_ref[0:1, :]
    for part in range(_ROUTER_SPLIT):
        rows = slice(part * tm, (part + 1) * tm)
        x1, h2 = _mix_and_norm(x_ref, attn_ref, conv_ref, wout_ref, g2_ref, rows)
        x1_ref[rows, :] = x1
        h2b = h2.astype(BF16)
        _to_token_tiles(h2_ref.at[rows], h2b.astype(F32))
        lane = lax.broadcasted_iota(I32, (tm, LANES), 1)
        logits = jnp.dot(h2b, rw_ref[...], preferred_element_type=F32) + rb_ref[...]
        logits = jnp.where(lane < N_EXPERTS, logits, -jnp.inf)
        m1 = jnp.max(logits, axis=-1, keepdims=True)
        i1 = jnp.min(jnp.where(logits == m1, lane, LANES), axis=-1, keepdims=True)
        rest = jnp.where(lane == i1, -jnp.inf, logits)
        m2 = jnp.max(rest, axis=-1, keepdims=True)
        i2 = jnp.min(jnp.where(rest == m2, lane, LANES), axis=-1, keepdims=True)
        e2 = jnp.exp(m2 - m1)
        den = 1.0 + e2
        oh1 = (lane == i1).astype(F32)
        oh2 = (lane == i2).astype(F32)
        both = oh1 + oh2
        before = running + jnp.dot(tri_ref[...], both.astype(BF16), preferred_element_type=F32)
        rank1 = jnp.sum(before * oh1, axis=-1, keepdims=True)
        rank2 = jnp.sum(before * oh2, axis=-1, keepdims=True)
        fields = (i1.astype(F32), i2.astype(F32), rank1, rank2, 1.0 / den, e2 / den)
        route = jnp.zeros((tm, LANES), F32)
        for k, v in enumerate(fields):
            route = jnp.where(lane == k, v, route)
        route_ref[rows, :] = route
        running = running + jnp.sum(both, axis=0, keepdims=True)
    run_ref[...] = jnp.broadcast_to(running, run_ref.shape)
    cnt_ref[...] = run_ref[...]


def _outproj_router(x2, attn_n, conv_n, wout, g2, rw, rb):
    S = x2.shape[0]
    tm = _FFN_TM
    tb = _ROUTER_SPLIT * tm
    tri = np.tril(np.ones((tm, tm), np.float32), -1)
    const = lambda shape: pl.BlockSpec(shape, lambda i: (0,) * len(shape))
    return pl.pallas_call(
        _router_kernel,
        grid=(S // tb,),
        in_specs=[
            pl.BlockSpec((tb, D_MODEL), lambda i: (i, 0)),
            pl.BlockSpec((tb, ATTN_W), lambda i: (i, 0)),
            pl.BlockSpec((tb, CONV_W), lambda i: (i, 0)),
            const(wout.shape), const((1, D_MODEL)), const(rw.shape), const((1, LANES)), const((tm, tm)),
        ],
        out_specs=(pl.BlockSpec((tb, D_MODEL), lambda i: (i, 0)),
                   pl.BlockSpec((tb,) + TOKEN_TILE, lambda i: (i, 0, 0)),
                   pl.BlockSpec((tb, LANES), lambda i: (i, 0)),
                   const((8, LANES))),
        out_shape=(jax.ShapeDtypeStruct((S, D_MODEL), F32),
                   jax.ShapeDtypeStruct((S,) + TOKEN_TILE, F32),
                   jax.ShapeDtypeStruct((S, LANES), F32),
                   jax.ShapeDtypeStruct((8, LANES), F32)),
        scratch_shapes=[pltpu.VMEM((8, LANES), F32)],
        compiler_params=pltpu.CompilerParams(dimension_semantics=("arbitrary",), vmem_limit_bytes=VMEM_LIMIT),
        name="outproj_router",
    )(x2, attn_n, conv_n, wout, g2, rw, rb, jnp.asarray(tri, BF16))


_ROW_TM = 256
_EXP_TM = 512
_DMA_UNROLL = 8


def _row_copy(src_ref, src_row, dst_ref, dst_row, sem):
    return pltpu.make_async_copy(src_ref.at[src_row], dst_ref.at[dst_row], sem)


def _dispatch_kernel(pos_ref, ztile_ref, h_ref, xs_ref, hbuf_ref, zbuf_ref, sem, zsem):
    tm = h_ref.shape[0]
    i = pl.program_id(0)
    slot = i % 2

    @pl.when(i == 0)
    def _():
        zbuf_ref[...] = jnp.zeros(zbuf_ref.shape, F32)
        for z in range(ztile_ref.shape[0]):
            clear = pltpu.make_async_copy(zbuf_ref, xs_ref.at[pl.ds(ztile_ref[z] * _EXP_TM, _EXP_TM)], zsem)
            clear.start()
            clear.wait()

    hbuf_ref[slot] = h_ref[...]

    def issue(t, carry):
        for k in range(2):
            _row_copy(hbuf_ref.at[slot], t, xs_ref, pos_ref[0, 0, 2 * t + k], sem.at[slot]).start(priority=k)
        return carry

    lax.fori_loop(0, tm, issue, 0, unroll=_DMA_UNROLL)

    def drain(which):
        def body(t, carry):
            for k in range(2):
                _row_copy(hbuf_ref.at[which], 0, xs_ref, 0, sem.at[which]).wait()
            return carry
        lax.fori_loop(0, tm, body, 0, unroll=_DMA_UNROLL)

    @pl.when(i > 0)
    def _():
        drain(1 - slot)

    @pl.when(i == pl.num_programs(0) - 1)
    def _():
        drain(slot)


def _dispatch(pos3, pad_tiles, h2, n_rows):
    S = h2.shape[0]
    tm = _ROW_TM
    return pl.pallas_call(
        _dispatch_kernel,
        grid=(S // tm,),
        in_specs=[
            pl.BlockSpec((1, 1, 2 * tm), lambda i: (i, 0, 0), memory_space=pltpu.SMEM),
            pl.BlockSpec(memory_space=pltpu.SMEM),
            pl.BlockSpec((tm,) + TOKEN_TILE, lambda i: (i, 0, 0)),
        ],
        out_specs=pl.BlockSpec(memory_space=pl.ANY),
        out_shape=jax.ShapeDtypeStruct((n_rows,) + TOKEN_TILE, F32),
        scratch_shapes=[pltpu.VMEM((2, tm) + TOKEN_TILE, F32), pltpu.VMEM((_EXP_TM,) + TOKEN_TILE, F32),
                        pltpu.SemaphoreType.DMA((2,)), pltpu.SemaphoreType.DMA],
        compiler_params=pltpu.CompilerParams(dimension_semantics=("arbitrary",), vmem_limit_bytes=VMEM_LIMIT),
        name="moe_dispatch",
    )(pos3, pad_tiles, h2)


def _experts_kernel(te_ref, tb_ref, nt_ref, xs_ref, wg_ref, wu_ref, wd_ref, ys_ref):
    i = pl.program_id(0)

    @pl.when(i < nt_ref[0])
    def _():
        x = _from_token_tiles(xs_ref).astype(BF16)
        a = jnp.dot(x, wg_ref[0], preferred_element_type=F32)
        u = jnp.dot(x, wu_ref[0], preferred_element_type=F32)
        y = (a * jax.nn.sigmoid(a) * u).astype(BF16)
        _to_token_tiles(ys_ref, jnp.dot(y, wd_ref[0], preferred_element_type=F32))

    @pl.when(i >= nt_ref[0])
    def _():
        ys_ref[...] = jnp.zeros(ys_ref.shape, F32)


def _experts(tile_e, tile_b, n_tiles, xs, wg, wu, wd):
    n_rows = xs.shape[0]
    tm = _EXP_TM
    weights = lambda shape: pl.BlockSpec(shape, lambda i, te, tb, nt: (te[i], 0, 0), pipeline_mode=pl.Buffered(1))
    grid_spec = pltpu.PrefetchScalarGridSpec(
        num_scalar_prefetch=3,
        grid=(n_rows // tm,),
        in_specs=[
            pl.BlockSpec((tm,) + TOKEN_TILE, lambda i, te, tb, nt: (tb[i], 0, 0)),
            weights((1, D_MODEL, D_FF)), weights((1, D_MODEL, D_FF)), weights((1, D_FF, D_MODEL)),
        ],
        out_specs=pl.BlockSpec((tm,) + TOKEN_TILE, lambda i, te, tb, nt: (tb[i], 0, 0)),
    )
    return pl.pallas_call(
        _experts_kernel,
        grid_spec=grid_spec,
        out_shape=jax.ShapeDtypeStruct((n_rows,) + TOKEN_TILE, F32),
        compiler_params=pltpu.CompilerParams(dimension_semantics=("arbitrary",), vmem_limit_bytes=VMEM_LIMIT),
        name="moe_experts",
    )(tile_e, tile_b, n_tiles, xs, wg, wu, wd)


def _combine_kernel(pos_ref, nxt_ref, x1_ref, route_ref, ys_ref, out_ref, y1_ref, y2_ref, sem):
    tm = x1_ref.shape[0]
    i = pl.program_id(0)
    slot = i % 2

    def issue(src_pos_ref, which):
        def body(t, carry):
            _row_copy(ys_ref, src_pos_ref[0, 0, 2 * t], y1_ref.at[which], t, sem.at[which]).start(priority=0)
            _row_copy(ys_ref, src_pos_ref[0, 0, 2 * t + 1], y2_ref.at[which], t, sem.at[which]).start(priority=1)
            return carry
        lax.fori_loop(0, tm, body, 0, unroll=_DMA_UNROLL)

    @pl.when(i == 0)
    def _():
        issue(pos_ref, slot)

    @pl.when(i + 1 < pl.num_programs(0))
    def _():
        issue(nxt_ref, 1 - slot)

    def drain(t, carry):
        _row_copy(ys_ref, 0, y1_ref.at[slot], 0, sem.at[slot]).wait()
        _row_copy(ys_ref, 0, y2_ref.at[slot], 0, sem.at[slot]).wait()
        return carry

    lax.fori_loop(0, tm, drain, 0, unroll=_DMA_UNROLL)
    lane = lax.broadcasted_iota(I32, (tm, LANES), 1)
    route = route_ref[...]
    w1 = jnp.sum(jnp.where(lane == 4, route, 0.0), axis=-1, keepdims=True)
    w2 = jnp.sum(jnp.where(lane == 5, route, 0.0), axis=-1, keepdims=True)
    out_ref[...] = x1_ref[...] + (_from_token_tiles(y1_ref.at[slot]) * w1 + _from_token_tiles(y2_ref.at[slot]) * w2)


def _combine(pos3, x1, route, ys):
    S = x1.shape[0]
    tm = _ROW_TM
    last = S // tm - 1
    return pl.pallas_call(
        _combine_kernel,
        grid=(S // tm,),
        in_specs=[
            pl.BlockSpec((1, 1, 2 * tm), lambda i: (i, 0, 0), memory_space=pltpu.SMEM),
            pl.BlockSpec((1, 1, 2 * tm), lambda i: (jnp.minimum(i + 1, last), 0, 0), memory_space=pltpu.SMEM),
            pl.BlockSpec((tm, D_MODEL), lambda i: (i, 0)),
            pl.BlockSpec((tm, LANES), lambda i: (i, 0)),
            pl.BlockSpec(memory_space=pl.ANY),
        ],
        out_specs=pl.BlockSpec((tm, D_MODEL), lambda i: (i, 0)),
        out_shape=jax.ShapeDtypeStruct((S, D_MODEL), F32),
        scratch_shapes=[pltpu.VMEM((2, tm) + TOKEN_TILE, F32), pltpu.VMEM((2, tm) + TOKEN_TILE, F32),
                        pltpu.SemaphoreType.DMA((2,))],
        compiler_params=pltpu.CompilerParams(dimension_semantics=("arbitrary",), vmem_limit_bytes=VMEM_LIMIT),
        name="moe_combine",
    )(pos3, pos3, x1, route, ys)


def _moe(h2, x1, route, counts, wg, wu, wd):
    S = h2.shape[0]
    tm = _EXP_TM
    n_tiles_max = 2 * S // tm + N_EXPERTS
    cnt = counts[0, 0:N_EXPERTS].astype(I32)
    tiles = (cnt + tm - 1) // tm
    first = jnp.cumsum(tiles) - tiles
    n_tiles = jnp.sum(tiles)
    idx = jnp.arange(n_tiles_max, dtype=I32)
    last = jnp.minimum(idx, n_tiles - 1)
    tile_e = (jnp.sum(last[:, None] >= first[None, :], axis=1) - 1).astype(I32)
    eid = route[:, 0:2].astype(I32)
    pos = first[eid] * tm + route[:, 2:4].astype(I32)
    pos3 = pos.reshape(S // _ROW_TM, 1, 2 * _ROW_TM)
    pad_tiles = jnp.concatenate([jnp.maximum(first + tiles - 1, 0),
                                 jnp.minimum(n_tiles + jnp.arange(N_EXPERTS, dtype=I32), n_tiles_max - 1)])
    xs = _dispatch(pos3, pad_tiles.astype(I32), h2, n_tiles_max * tm)
    ys = _experts(tile_e, idx, n_tiles.reshape(1), xs, wg, wu, wd)
    return _combine(pos3, x1, route, ys)


def _split_w_in(w):
    o = np.cumsum([0, ATTN_W] + [KV_W] * 6 + [N_BRANCH * N_HEADS] + [CONV_W] * 3)
    q, kc, vc, ksl, vsl, kwn, vwn, gts, cb, cc, ch = (w[:, o[i]:o[i + 1]] for i in range(11))
    perm = np.array([h * N_BRANCH + br for br in range(N_BRANCH) for h in range(N_HEADS)])
    wtok = jnp.concatenate([kc, vc, ksl, kwn, cb, cc, ch], axis=1).astype(BF16)
    feat = jnp.concatenate([q, vsl, vwn, gts[:, perm], jnp.zeros((D_MODEL, 8), w.dtype)], axis=1)
    return wtok, feat.T.astype(BF16)


def _expand_cmp_w1(w1k, w1v):
    kinds = jnp.stack([w1k, w1v]).reshape(2, CMP_LEN, HEAD_DIM, CMP_HIDDEN)
    per_col = jnp.repeat(kinds, N_KV, axis=0)
    eye = jnp.eye(2 * N_KV, dtype=F32)
    out = []
    for l0 in (0, CMP_STRIDE):
        w = per_col[:, l0:l0 + CMP_STRIDE].transpose(1, 0, 2, 3)
        blk = w[:, :, :, None, :] * eye[None, :, None, :, None]
        out.append(blk.reshape(CMP_STRIDE * 2 * KV_W, 2 * N_KV * CMP_HIDDEN).astype(BF16))
    return out


def kernel(x, rel_bias, norm1, w_in, q_norm, k_norm, cmp_pos_k, cmp_pos_v, cmp_k_w1, cmp_k_w2, cmp_v_w1, cmp_v_w2,
           conv_w, attn_out_norm, conv_out_norm, w_out, norm2, ffn_w_gate, ffn_w_up, ffn_w_down, router_w, router_b,
           moe_w_gate, moe_w_up, moe_w_down):
    B, S, _ = x.shape
    assert B == 1 and S % CHUNK == 0 and S >= WINDOW + Q_BLOCK
    depth = norm1.shape[0]
    x2 = x.reshape(S, D_MODEL)
    tables = _bias_tables(rel_bias)
    for layer in range(depth):
        wtok, wfeat = _split_w_in(w_in[layer])
        kgain = jnp.concatenate([jnp.tile(k_norm[layer, 1], N_KV), jnp.tile(k_norm[layer, 2], N_KV)])[None, :]
        gq = jnp.broadcast_to(q_norm[layer][:, None], (HEAD_DIM, _IN_TM))
        convw = jnp.pad(conv_w[layer], ((0, 8 - CONV_K), (0, 0)))
        qT, gT, vsT, vwT, ks, kw, kcv, conv_n = _in_proj(
            x2, norm1[layer][None, :], wtok, wfeat, kgain, gq, convw, conv_out_norm[layer][None, :])

        wtop, wbot = _expand_cmp_w1(cmp_k_w1[layer], cmp_v_w1[layer])
        posk = jnp.broadcast_to(cmp_pos_k[layer].reshape(1, -1), (8, CMP_LEN * HEAD_DIM))
        posv = jnp.broadcast_to(cmp_pos_v[layer].reshape(1, -1), (8, CMP_LEN * HEAD_DIM))
        w2k = jnp.pad(cmp_k_w2[layer], ((0, 0), (0, LANES - HEAD_DIM))).astype(BF16)
        w2vT = cmp_v_w2[layer].T.astype(BF16)
        kcgain = jnp.pad(k_norm[layer, 0], (0, LANES - HEAD_DIM))[None, :]
        kc, vcT = _compress(kcv, wtop, wbot, posk, posv, cmp_k_w1[layer], cmp_v_w1[layer], w2k, w2vT, kcgain)

        gain_b = jnp.broadcast_to(attn_out_norm[layer].reshape(N_KV, GQA, HEAD_DIM).transpose(0, 2, 1)[:, :, :, None],
                                  (N_KV, HEAD_DIM, GQA, LANES)).reshape(N_KV, HEAD_DIM, GQA * LANES)
        attn_n = _attention(qT, gT, ks, vsT, kw, vwT, kc, vcT, tables, gain_b)

        wout = w_out[layer].astype(BF16)
        g2 = norm2[layer][None, :]
        i = layer // 2
        if layer % 2 == 0:
            x2 = _outproj_ffn(x2, attn_n, conv_n, wout, g2, ffn_w_gate[i].astype(BF16), ffn_w_up[i].astype(BF16),
                              ffn_w_down[i].astype(BF16))
        else:
            rw = jnp.pad(router_w[i], ((0, 0), (0, LANES - N_EXPERTS))).astype(BF16)
            rb = jnp.pad(router_b[i], (0, LANES - N_EXPERTS))[None, :]
            x1, h2, route, counts = _outproj_router(x2, attn_n, conv_n, wout, g2, rw, rb)
            x2 = _moe(h2, x1, route, counts, moe_w_gate[i].astype(BF16), moe_w_up[i].astype(BF16),
                      moe_w_down[i].astype(BF16))
    return x2.reshape(B, S, D_MODEL)
```

```python
import functools
import math

import numpy as np
import jax
import jax.numpy as jnp
from jax import lax
from jax.experimental import pallas as pl
from jax.experimental.pallas import tpu as pltpu

F32 = jnp.float32
BF16 = jnp.bfloat16
I32 = jnp.int32

D_MODEL = 1024
HEAD_DIM = 64
N_HEADS = 8
N_KV = 2
GQA = N_HEADS // N_KV
ATTN_W = N_HEADS * HEAD_DIM
KV_W = N_KV * HEAD_DIM
N_BRANCH = 3
CONV_W = 512
CONV_K = 3
CMP_LEN = 32
CMP_STRIDE = 16
CMP_HIDDEN = 128
SEL_LEN = 64
SEL_TOPK = 16
WINDOW = 512
Q_BLOCK = 128
N_BUCKETS = 32
MAX_DISTANCE = 128
D_FF = 2816
N_EXPERTS = 8
EPS = 1e-6
NEG = -1e30
MASK_BIG = 2.0 ** 60
MASK_ROWS = 16
SCALE = HEAD_DIM ** -0.5
LOG2E = math.log2(math.e)

LANES = 128
V_ROWS = 80
CHUNK = 512
BLOCKS_PER_CHUNK = CHUNK // SEL_LEN
TC_LEAD = 24
TC_ROWS = 88
TC_WIN = 48
CMP_STEP = 256
VMEM_LIMIT = 56 * 1024 * 1024


def _bucket_np(dist):
    n = np.maximum(dist, 0)
    max_exact = N_BUCKETS // 2
    nf = np.maximum(n, max_exact).astype(np.float64)
    v = np.log(nf / max_exact) / math.log(MAX_DISTANCE / max_exact) * (N_BUCKETS - max_exact)
    frac = np.abs(v - np.round(v))
    assert np.all((frac > 1e-6) | (n <= max_exact) | (n >= MAX_DISTANCE)), "bucket boundary is precision dependent"
    large = np.minimum(max_exact + (v + 1e-9).astype(np.int32), N_BUCKETS - 1)
    return np.where(n < max_exact, n, large).astype(np.int32)


def _index_tables():
    tl = np.arange(Q_BLOCK)[None, :]
    r = np.arange(WINDOW + Q_BLOCK + WINDOW)[:, None]
    d = tl + WINDOW - r
    idx_w = np.where((d >= 0) & (d < WINDOW), _bucket_np(d), -1)
    r = np.arange(3 * Q_BLOCK)[:, None]
    d = tl + Q_BLOCK - r
    idx_s = np.where(d >= 0, _bucket_np(d), -1)
    r = np.arange(TC_ROWS)[:, None] - TC_LEAD
    d = tl - CMP_STRIDE * r + (CMP_STRIDE * 16 - (CMP_LEN - 1))
    idx_c = np.where((d >= 0) & (r < 32), _bucket_np(d), -1)
    return idx_w.astype(np.int32), idx_s.astype(np.int32), idx_c.astype(np.int32)


def _tables_kernel(rb_ref, iw_ref, is_ref, ic_ref, tw_ref, ts_ref, tc_ref, b31_ref):
    head_lanes = [(h // GQA, slice((h % GQA) * LANES, (h % GQA + 1) * LANES)) for h in range(N_HEADS)]
    for idx_ref, out_ref in ((iw_ref, tw_ref), (is_ref, ts_ref), (ic_ref, tc_ref)):
        out_ref[...] = jnp.full(out_ref.shape, NEG, F32)

        def body(b, carry, idx_ref=idx_ref, out_ref=out_ref):
            hit = idx_ref[...] == b
            for h, (g, lanes) in enumerate(head_lanes):
                out_ref[g, :, lanes] = jnp.where(hit, rb_ref[b, h] * LOG2E, out_ref[g, :, lanes])
            return carry

        lax.fori_loop(0, N_BUCKETS, body, 0)
    for h, (g, lanes) in enumerate(head_lanes):
        far_bias = rb_ref[N_BUCKETS - 1, h] * LOG2E
        tc_ref[g, :, lanes] = jnp.where(ic_ref[...] >= 0, tc_ref[g, :, lanes] - far_bias, NEG)
        b31_ref[g, :, lanes] = jnp.full((8, LANES), far_bias, F32)


def _bias_tables(rel_bias):
    idx_w, idx_s, idx_c = _index_tables()
    width = GQA * LANES
    out_shape = (
        jax.ShapeDtypeStruct((N_KV, idx_w.shape[0], width), F32),
        jax.ShapeDtypeStruct((N_KV, idx_s.shape[0], width), F32),
        jax.ShapeDtypeStruct((N_KV, idx_c.shape[0], width), F32),
        jax.ShapeDtypeStruct((N_KV, 8, width), F32),
    )
    vmem = pl.BlockSpec(memory_space=pltpu.VMEM)
    return pl.pallas_call(
        _tables_kernel,
        out_shape=out_shape,
        in_specs=[pl.BlockSpec(memory_space=pltpu.SMEM), vmem, vmem, vmem],
        out_specs=(vmem, vmem, vmem, vmem),
        name="bias_tables",
    )(rel_bias, jnp.asarray(idx_w), jnp.asarray(idx_s), jnp.asarray(idx_c))


def _in_proj_kernel(x_ref, g1_ref, wtok_ref, wfeat_ref, ind_ref, kgain_ref, aug_ref, gq_ref, convw_ref, cgain_ref,
                    qT_ref, gT_ref, vsT_ref, vwT_ref, ks_ref, kw_ref, kcv_ref, convn_ref, zs_ref, carry_ref):
    tm = _IN_TM
    i = pl.program_id(0)

    @pl.when(i == 0)
    def _():
        carry_ref[...] = jnp.zeros(carry_ref.shape, F32)

    prev_tail = carry_ref[...]
    for part in range(_IN_SPLIT):
        rows = slice(part * tm, (part + 1) * tm)
        x = x_ref[rows, :]
        ms = jnp.mean(x * x, axis=-1, keepdims=True)
        h = (x * lax.rsqrt(ms + EPS) * g1_ref[...]).astype(BF16)
        tok = jnp.dot(h, wtok_ref[...], preferred_element_type=F32)
        feat = lax.dot_general(wfeat_ref[...], h, (((1,), (1,)), ((), ())),
                               preferred_element_type=F32)

        kcv_ref[rows, :] = tok[:, 0:2 * KV_W].astype(BF16)
        kk = tok[:, 2 * KV_W:4 * KV_W]
        sq = kk * kk
        sq_hi = sq.astype(BF16)
        sq_lo = (sq - sq_hi.astype(F32)).astype(BF16)
        ssq = jnp.dot(jnp.concatenate([sq_hi, sq_lo], axis=1), ind_ref[...],
                      preferred_element_type=F32)
        kn = kk * lax.rsqrt(ssq * (1.0 / HEAD_DIM) + EPS) * kgain_ref[...]
        ksl = kn[:, 0:KV_W]
        lane = lax.broadcasted_iota(I32, (tm, LANES), 1)
        aug = aug_ref[...]
        ks_ref[0, rows, :] = jnp.where(lane < HEAD_DIM, ksl, aug).astype(BF16)
        ks_ref[1, rows, :] = jnp.where(lane < HEAD_DIM, pltpu.roll(ksl, HEAD_DIM, 1), aug).astype(BF16)
        kw_ref[rows, :] = kn[:, KV_W:2 * KV_W].astype(BF16)

        c0 = 4 * KV_W
        cb = tok[:, c0:c0 + CONV_W]
        cc = tok[:, c0 + CONV_W:c0 + 2 * CONV_W]
        ch = tok[:, c0 + 2 * CONV_W:c0 + 3 * CONV_W]
        z = cc * ch
        zs_ref[part, 0:8, :] = prev_tail
        zs_ref[part, 8:8 + tm, :] = z
        z1 = zs_ref[part, 7:7 + tm, :]
        z2 = zs_ref[part, 6:6 + tm, :]
        w = convw_ref[...]
        y = w[0:1, :] * z2 + w[1:2, :] * z1 + w[2:3, :] * z
        prev_tail = z[tm - 8:tm, :]
        oc = cb * y
        msc = jnp.mean(oc * oc, axis=-1, keepdims=True)
        convn_ref[rows, :] = (oc * lax.rsqrt(msc + EPS) * cgain_ref[...]).astype(BF16)

        q = feat[0:ATTN_W].reshape(N_HEADS, HEAD_DIM, tm)
        qss = jnp.sum(q * q, axis=1, keepdims=True)
        qn = q * lax.rsqrt(qss * (1.0 / HEAD_DIM) + EPS) * gq_ref[...][None]
        qT_ref[:, rows] = (qn * (SCALE * LOG2E)).reshape(ATTN_W, tm).astype(BF16)
        ones_rows = (lax.broadcasted_iota(I32, (V_ROWS - HEAD_DIM, tm), 0) == 0).astype(BF16)
        for g in range(N_KV):
            r0 = ATTN_W + g * HEAD_DIM
            vsT_ref[g, 0:HEAD_DIM, rows] = feat[r0:r0 + HEAD_DIM].astype(BF16)
            vsT_ref[g, HEAD_DIM:V_ROWS, rows] = ones_rows
            r1 = ATTN_W + KV_W + g * HEAD_DIM
            vwT_ref[g, 0:HEAD_DIM, rows] = feat[r1:r1 + HEAD_DIM].astype(BF16)
            vwT_ref[g, HEAD_DIM:V_ROWS, rows] = ones_rows
        g0 = ATTN_W + 2 * KV_W
        gT_ref[:, rows] = jax.nn.sigmoid(feat[g0:g0 + N_BRANCH * N_HEADS])
    carry_ref[...] = prev_tail


_IN_SPLIT = 2
_IN_TM = 512


def _in_proj(x2, g1, wtok, wfeat, kgain, gq, convw, cgain):
    S = x2.shape[0]
    tm = _IN_TM
    tb = _IN_SPLIT * tm
    nt = S // tb
    ind = np.kron(np.eye(2 * N_KV, dtype=np.float32), np.ones((HEAD_DIM, HEAD_DIM), np.float32))
    ind = np.concatenate([ind, ind], axis=0)
    aug = np.zeros((tm, LANES), np.float32)
    blk = (np.arange(tm) // SEL_LEN) % BLOCKS_PER_CHUNK
    aug[np.arange(tm), HEAD_DIM + blk] = 1.0
    aug[:, HEAD_DIM + BLOCKS_PER_CHUNK:HEAD_DIM + BLOCKS_PER_CHUNK + 2] = 1.0
    const = lambda shape: pl.BlockSpec(shape, lambda i: (0,) * len(shape))
    out_shape = (
        jax.ShapeDtypeStruct((ATTN_W, S), BF16),
        jax.ShapeDtypeStruct((N_BRANCH * N_HEADS, S), F32),
        jax.ShapeDtypeStruct((N_KV, V_ROWS, S), BF16),
        jax.ShapeDtypeStruct((N_KV, V_ROWS, S), BF16),
        jax.ShapeDtypeStruct((N_KV, S, LANES), BF16),
        jax.ShapeDtypeStruct((S, LANES), BF16),
        jax.ShapeDtypeStruct((S, 2 * KV_W), BF16),
        jax.ShapeDtypeStruct((S, CONV_W), BF16),
    )
    out_specs = (
        pl.BlockSpec((ATTN_W, tb), lambda i: (0, i)),
        pl.BlockSpec((N_BRANCH * N_HEADS, tb), lambda i: (0, i)),
        pl.BlockSpec((N_KV, V_ROWS, tb), lambda i: (0, 0, i)),
        pl.BlockSpec((N_KV, V_ROWS, tb), lambda i: (0, 0, i)),
        pl.BlockSpec((N_KV, tb, LANES), lambda i: (0, i, 0)),
        pl.BlockSpec((tb, LANES), lambda i: (i, 0)),
        pl.BlockSpec((tb, 2 * KV_W), lambda i: (i, 0)),
        pl.BlockSpec((tb, CONV_W), lambda i: (i, 0)),
    )
    in_specs = [
        pl.BlockSpec((tb, D_MODEL), lambda i: (i, 0)),
        const((1, D_MODEL)),
        const(wtok.shape),
        const(wfeat.shape),
        const(ind.shape),
        const((1, 2 * KV_W)),
        const(aug.shape),
        const((HEAD_DIM, tm)),
        const((8, CONV_W)),
        const((1, CONV_W)),
    ]
    return pl.pallas_call(
        _in_proj_kernel,
        grid=(nt,),
        in_specs=in_specs,
        out_specs=out_specs,
        out_shape=out_shape,
        scratch_shapes=[pltpu.VMEM((_IN_SPLIT, tm + 8, CONV_W), F32), pltpu.VMEM((8, CONV_W), F32)],
        compiler_params=pltpu.CompilerParams(dimension_semantics=("arbitrary",), vmem_limit_bytes=VMEM_LIMIT),
        name="in_proj",
    )(x2, g1, wtok, wfeat, jnp.asarray(ind, BF16), kgain, jnp.asarray(aug), gq, convw, cgain)


def _compress_kernel(r_ref, wtop_ref, wbot_ref, posk_ref, posv_ref, w1k_ref, w1v_ref, w2k_ref, w2vT_ref, kgain_ref,
                     kc_ref, vcT_ref):
    nr = r_ref.shape[0]
    r = r_ref[...]
    u = jnp.dot(r, wtop_ref[...], preferred_element_type=F32)
    lo = jnp.dot(r, wbot_ref[...], preferred_element_type=F32)
    bias_k = jnp.dot(posk_ref[...], w1k_ref[...], preferred_element_type=F32)[0:1, :]
    bias_v = jnp.dot(posv_ref[...], w1v_ref[...], preferred_element_type=F32)[0:1, :]
    bias4 = jnp.concatenate([bias_k] * N_KV + [bias_v] * N_KV, axis=1)
    hid = u + pltpu.roll(lo, nr - 1, 0) + bias4
    act = jax.nn.gelu(hid)
    for g in range(N_KV):
        ak = act[:, g * CMP_HIDDEN:(g + 1) * CMP_HIDDEN].astype(BF16)
        av = act[:, (N_KV + g) * CMP_HIDDEN:(N_KV + g + 1) * CMP_HIDDEN].astype(BF16)
        kc = jnp.dot(ak, w2k_ref[...], preferred_element_type=F32)
        ssq = jnp.sum(kc * kc, axis=-1, keepdims=True)
        kc_ref[g] = (kc * lax.rsqrt(ssq * (1.0 / HEAD_DIM) + EPS) * kgain_ref[...]).astype(BF16)
        vcT_ref[g] = lax.dot_general(w2vT_ref[...], av, (((1,), (1,)), ((), ())),
                                     preferred_element_type=F32).astype(BF16)


def _compress(kcv, wtop, wbot, posk, posv, w1k, w1v, w2k, w2vT, kgain):
    S = kcv.shape[0]
    nr = S // CMP_STRIDE
    r = kcv.reshape(nr, CMP_STRIDE * 2 * KV_W)
    vmem = pl.BlockSpec(memory_space=pltpu.VMEM)
    return pl.pallas_call(
        _compress_kernel,
        out_shape=(jax.ShapeDtypeStruct((N_KV, nr, LANES), BF16),
                   jax.ShapeDtypeStruct((N_KV, HEAD_DIM, nr), BF16)),
        in_specs=[vmem] * 10,
        out_specs=(vmem, vmem),
        compiler_params=pltpu.CompilerParams(vmem_limit_bytes=VMEM_LIMIT),
        name="compress",
    )(r, wtop, wbot, posk, posv, w1k, w1v, w2k, w2vT, kgain)


def _tile4(row):
    return jnp.concatenate([row] * GQA, axis=1)


def _attn_kernel(qT_ref, gT_ref, ks_ref, vsT_ref, kw_ref, vwT_ref, kc_ref, vcT_ref,
                 tw_ref, ts_ref, tc_ref, b31_ref, gain_ref, out_ref,
                 sc_ref, sw_ref, imp_ref, amask_ref, qs_ref, sa_ref, sb_ref, qz_ref, qsb_ref, fin_ref, ocmp_ref,
                 pk_ref):
    ncr = kc_ref.shape[1]
    noct = amask_ref.shape[1] - 1
    nsel = noct * BLOCKS_PER_CHUNK
    width = GQA * LANES
    c = pl.program_id(0)
    t0 = c * Q_BLOCK
    tl = lax.broadcasted_iota(I32, (1, LANES), 1)
    cur = 2 * c + (tl >= SEL_LEN).astype(I32)
    zb = jnp.maximum(2 * c - 2, 0)
    nfar = (zb + BLOCKS_PER_CHUNK - 1) // BLOCKS_PER_CHUNK
    zs = pl.multiple_of(zb * SEL_LEN, Q_BLOCK)
    ts_off = pl.multiple_of(zs - (t0 - Q_BLOCK), Q_BLOCK)
    zeros_q = jnp.zeros((HEAD_DIM, width), BF16)

    col_ok = _tile4(t0 + tl) >= CMP_LEN - 1
    j = lax.broadcasted_iota(I32, (nsel, LANES), 0)
    forced = (j == 0) | (j == cur) | (j == cur - 1)
    @pl.when(c == 0)
    def _():
        imp_ref[...] = jnp.zeros(imp_ref.shape, F32)


    ws = pl.multiple_of(jnp.maximum(t0 - WINDOW, 0), Q_BLOCK)
    tw_off = pl.multiple_of(ws - (t0 - WINDOW), Q_BLOCK)
    nw = WINDOW + Q_BLOCK
    lo = pl.multiple_of(jnp.clip((8 * c - 16) // 16 * 16, 0, ncr - TC_WIN), 16)
    tc_off = pl.multiple_of(lo - (8 * c - 16) + TC_LEAD, 8)
    qgs = [jnp.concatenate([qT_ref[(GQA * g + r) * HEAD_DIM:(GQA * g + r + 1) * HEAD_DIM, :]
                            for r in range(GQA)], axis=1) for g in range(N_KV)]

    for g in range(N_KV):
        qw = jnp.concatenate([qgs[g], zeros_q] if g == 0 else [zeros_q, qgs[g]], axis=0)
        sw_ref[g] = (jnp.dot(kw_ref[pl.ds(ws, nw), :], qw, preferred_element_type=F32)
                     + tw_ref[g, pl.ds(tw_off, nw), :])

    def compressed(rows):
        for g in range(N_KV):
            qc = jnp.concatenate([qgs[g], zeros_q], axis=0)
            s = jnp.dot(kc_ref[g, 0:rows, :], qc, preferred_element_type=F32)
            row = lax.broadcasted_iota(I32, (rows, width), 0)
            sc_ref[g, 0:rows, :] = jnp.where(row < lo, s, NEG)
            s_loc = jnp.dot(kc_ref[g, pl.ds(lo, TC_WIN), :], qc, preferred_element_type=F32)
            sc_ref[g, pl.ds(lo, TC_WIN), :] = s_loc + tc_ref[g, pl.ds(tc_off, TC_WIN), :]
        for g in range(N_KV):
            s = sc_ref[g, 0:rows, :]
            m = jnp.max(s, axis=0, keepdims=True)
            e = jnp.exp2(s - m)
            l = jnp.sum(e, axis=0, keepdims=True)
            p = e * jnp.where(col_ok, 1.0 / jnp.maximum(l, 1e-30), 0.0)
            ocmp_ref[g] = jnp.dot(vcT_ref[g, :, 0:rows], p.astype(BF16), preferred_element_type=F32)
            imp_ref[g, 8:8 + rows, :] = (
                (p[:, 0:LANES] + p[:, LANES:2 * LANES]) + p[:, 2 * LANES:3 * LANES] + p[:, 3 * LANES:4 * LANES])

    sizes = list(range(CMP_STEP, ncr, CMP_STEP)) + [ncr]
    variant = jnp.minimum((lo + TC_WIN + CMP_STEP - 1) // CMP_STEP, len(sizes)) - 1
    for v, rows in enumerate(sizes):
        pl.when(variant == v)(functools.partial(compressed, rows))

    o_cmp, scores0 = [], []
    for g in range(N_KV):
        o_cmp.append(ocmp_ref[g])
        isel = jnp.zeros((nsel, LANES), F32)
        for off in (0, -1, 1, 0, 2, 1, 3, 2):
            isel = isel + imp_ref[g, pl.ds(8 + off, nsel, stride=4), :]
        score = jnp.where(forced, -jnp.inf, isel)
        scores0.append(jnp.where(j > cur, NEG, score))

    o_win = []
    for g in range(N_KV):
        sw = sw_ref[g]
        mw = jnp.max(sw, axis=0, keepdims=True)
        pw = jnp.exp2(sw - mw)
        accw = jnp.dot(vwT_ref[g, :, pl.ds(ws, nw)], pw.astype(BF16), preferred_element_type=F32)
        o_win.append(accw[0:HEAD_DIM] * (1.0 / jnp.maximum(accw[HEAD_DIM:HEAD_DIM + 1], 1e-30)))

    n_picks = SEL_TOPK - 3
    fast = list(scores0)
    for _ in range(n_picks):
        for g in range(N_KV):
            mx = jnp.max(fast[g], axis=0, keepdims=True)
            fast[g] = jnp.where(fast[g] == mx, -jnp.inf, fast[g])
    eligible = (j <= cur) & jnp.logical_not(forced)
    n_eligible = cur + 1 - (1 + (cur >= 1).astype(I32) + (cur >= 2).astype(I32))
    expect = jnp.minimum(n_picks, n_eligible).astype(F32)
    wrong = jnp.zeros((1, LANES), F32)
    for g in range(N_KV):
        pk_ref[g] = fast[g]
        got = jnp.sum(jnp.where((fast[g] == -jnp.inf) & eligible, 1.0, 0.0), axis=0, keepdims=True)
        wrong = jnp.maximum(wrong, jnp.abs(got - expect))

    @pl.when(jnp.max(wrong) > 0.0)
    def _():
        def pick(_, score):
            mx = jnp.max(score, axis=0, keepdims=True)
            first = jnp.min(jnp.where(score == mx, j, nsel), axis=0, keepdims=True)
            return jnp.where(j == first, -jnp.inf, score)

        for g in range(N_KV):
            pk_ref[g] = lax.fori_loop(0, n_picks, pick, scores0[g])

    picks = [pk_ref[g] for g in range(N_KV)]

    m_init, acc_init, gates = [], [], []
    for g in range(N_KV):
        qg = qgs[g]
        b31row = b31_ref[g, 0:1, :]
        chosen = (picks[g] == -jnp.inf) & (j <= cur)
        a_all = _tile4(jnp.where(chosen, 0.0, -MASK_BIG)).reshape(noct, BLOCKS_PER_CHUNK, width)
        b_hi = b31row.astype(BF16).astype(F32)
        erow = lax.broadcasted_iota(I32, (8, width), 0)
        extra = jnp.where(erow == 0, b_hi, jnp.where(erow == 1, b31row - b_hi, 0.0))
        amask_ref[g, 0:noct] = jnp.concatenate(
            [a_all, jnp.broadcast_to(extra[None], (noct, 8, width))], axis=1).astype(BF16)
        far_last = jnp.maximum(nfar - 1, 0)
        orow = lax.broadcasted_iota(I32, (MASK_ROWS, width), 0)
        keep = (orow >= BLOCKS_PER_CHUNK) | (orow + BLOCKS_PER_CHUNK * far_last < zb)
        amask_ref[g, noct] = jnp.where(keep, amask_ref[g, far_last].astype(F32), -MASK_BIG).astype(BF16)
        for q_ref in (qs_ref, qsb_ref):
            q_ref[g, 0:HEAD_DIM, :] = qg
            q_ref[g, HEAD_DIM + MASK_ROWS:LANES, :] = jnp.zeros((LANES - HEAD_DIM - MASK_ROWS, width), BF16)

        halves = []
        for h in range(2):
            octet = amask_ref[g, (zb + 2 * h) // BLOCKS_PER_CHUNK].astype(F32)
            qz_ref[2 * g + h, HEAD_DIM:HEAD_DIM + MASK_ROWS, :] = jnp.where(
                orow < BLOCKS_PER_CHUNK, octet, 0.0).astype(BF16)
            qz_ref[2 * g + h, 0:HEAD_DIM, :] = qg
            qz_ref[2 * g + h, HEAD_DIM + MASK_ROWS:LANES, :] = jnp.zeros((LANES - HEAD_DIM - MASK_ROWS, width), BF16)
            sh = jnp.dot(ks_ref[g, pl.ds(zs + h * Q_BLOCK, Q_BLOCK), :], qz_ref[2 * g + h], preferred_element_type=F32)
            halves.append(sh + ts_ref[g, pl.ds(ts_off + h * Q_BLOCK, Q_BLOCK), :])
        sz = jnp.concatenate(halves, axis=0)
        mz = jnp.max(sz, axis=0, keepdims=True)
        pz = jnp.exp2(sz - mz)
        m_init.append(mz)
        acc_init.append(jnp.dot(vsT_ref[g, :, pl.ds(zs, 2 * Q_BLOCK)], pz.astype(BF16), preferred_element_type=F32))
        gates.append([jnp.concatenate([gT_ref[br * N_HEADS + GQA * g + r:br * N_HEADS + GQA * g + r + 1, :]
                                       for r in range(GQA)], axis=1) for br in range(N_BRANCH)])

    nsub = 1
    sub = CHUNK // nsub

    def score(g, u, buf_ref, h, q_ref):
        k0 = pl.multiple_of(jnp.minimum(u, noct - 1) * CHUNK + h * sub, sub)
        s = jnp.dot(ks_ref[g, pl.ds(k0, sub), :], q_ref[g], preferred_element_type=F32)
        buf_ref[g, h * sub:(h + 1) * sub, :] = s
        return jnp.max(s, axis=0, keepdims=True)

    def set_mask_rows(g, u, q_ref):
        oct_id = jnp.where(u < nfar - 1, u, noct)
        q_ref[g, HEAD_DIM:HEAD_DIM + MASK_ROWS, :] = amask_ref[g, oct_id]

    def accumulate(g, u, buf_ref, h, mn, acc):
        k0 = pl.multiple_of(jnp.minimum(u, noct - 1) * CHUNK + h * sub, sub)
        p = jnp.exp2(buf_ref[g, h * sub:(h + 1) * sub, :] - mn)
        return acc + jnp.dot(vsT_ref[g, :, pl.ds(k0, sub)], p.astype(BF16), preferred_element_type=F32)

    def step(u, cur_ref, nxt_ref, q_ref, state, cms):
        out_state, out_cms = [], []
        for g in range(N_KV):
            m, acc = state[2 * g], state[2 * g + 1]
            mn = jnp.maximum(m, cms[g])
            acc = jnp.exp2(m - mn) * acc
            set_mask_rows(g, u + 1, q_ref)
            cm = None
            for h in range(nsub):
                ch = score(g, u + 1, nxt_ref, h, q_ref)
                cm = ch if cm is None else jnp.maximum(cm, ch)
                acc = accumulate(g, u, cur_ref, h, mn, acc)
            out_cms.append(cm)
            out_state += [mn, acc]
        return out_state, out_cms

    def far(i, carry):
        state, cm_b = step(2 * i, sa_ref, sb_ref, qsb_ref, carry[0:4], carry[4:6])
        state, cm_a = step(2 * i + 1, sb_ref, sa_ref, qs_ref, state, cm_b)
        return tuple(state + cm_a)

    first = []
    for g in range(N_KV):
        set_mask_rows(g, 0, qs_ref)
        cm = None
        for h in range(nsub):
            ch = score(g, 0, sa_ref, h, qs_ref)
            cm = ch if cm is None else jnp.maximum(cm, ch)
        first.append(cm)
    nchunks = jnp.maximum(nfar, 1)
    npairs = (nchunks - 1) // 2
    carry = lax.fori_loop(0, npairs, far, (m_init[0], acc_init[0], m_init[1], acc_init[1], first[0], first[1]))
    u_last = 2 * npairs

    def finish(two_left):
        for g in range(N_KV):
            m, acc, cm = carry[2 * g], carry[2 * g + 1], carry[4 + g]
            mn = jnp.maximum(m, cm)
            acc = jnp.exp2(m - mn) * acc
            if two_left:
                set_mask_rows(g, u_last + 1, qsb_ref)
                cm_b = score(g, u_last + 1, sb_ref, 0, qsb_ref)
            acc = accumulate(g, u_last, sa_ref, 0, mn, acc)
            if two_left:
                mn_b = jnp.maximum(mn, cm_b)
                acc = accumulate(g, u_last + 1, sb_ref, 0, mn_b, jnp.exp2(mn - mn_b) * acc)
            fin_ref[g] = acc

    pl.when(nchunks - u_last == 1)(lambda: finish(False))
    pl.when(nchunks - u_last == 2)(lambda: finish(True))
    carry = [None, fin_ref[0], None, fin_ref[1]]

    o_all, ssq = [], jnp.zeros((1, LANES), F32)
    for g in range(N_KV):
        acc = carry[2 * g + 1]
        o_sel = acc[0:HEAD_DIM] * (1.0 / jnp.maximum(acc[HEAD_DIM:HEAD_DIM + 1], 1e-30))
        o = gates[g][0] * o_cmp[g] + gates[g][1] * o_sel + gates[g][2] * o_win[g]
        o_all.append(o)
        cs = jnp.sum(o * o, axis=0, keepdims=True)
        ssq = ssq + ((cs[:, 0:LANES] + cs[:, LANES:2 * LANES]) + (cs[:, 2 * LANES:3 * LANES] + cs[:, 3 * LANES:]))
    inv = _tile4(lax.rsqrt(ssq * (1.0 / ATTN_W) + EPS))
    for g in range(N_KV):
        on = o_all[g] * inv * gain_ref[g]
        for k in range(GQA // 2):
            pair = jnp.concatenate([on[:, (2 * k) * LANES:(2 * k + 1) * LANES],
                                    on[:, (2 * k + 1) * LANES:(2 * k + 2) * LANES]], axis=0)
            col = (GQA * g + 2 * k) * HEAD_DIM
            out_ref[:, col:col + 2 * HEAD_DIM] = pair.T.astype(BF16)


def _attention(qT, gT, ks, vsT, kw, vwT, kc, vcT, tables, gain_b):
    S = qT.shape[1]
    nq = S // Q_BLOCK
    ncr = kc.shape[1]
    nsel = S // SEL_LEN
    noct = S // CHUNK
    width = GQA * LANES
    tw, ts, tc, b31 = tables
    vmem = pl.BlockSpec(memory_space=pltpu.VMEM)
    in_specs = [
        pl.BlockSpec((ATTN_W, Q_BLOCK), lambda c: (0, c)),
        pl.BlockSpec((N_BRANCH * N_HEADS, Q_BLOCK), lambda c: (0, c)),
    ] + [vmem] * 11
    return pl.pallas_call(
        _attn_kernel,
        grid=(nq,),
        in_specs=in_specs,
        out_specs=pl.BlockSpec((Q_BLOCK, ATTN_W), lambda c: (c, 0)),
        out_shape=jax.ShapeDtypeStruct((S, ATTN_W), BF16),
        scratch_shapes=[
            pltpu.VMEM((N_KV, ncr, width), F32),
            pltpu.VMEM((N_KV, WINDOW + Q_BLOCK, width), F32),
            pltpu.VMEM((N_KV, ncr + 16, LANES), F32),
            pltpu.VMEM((N_KV, noct + 1, MASK_ROWS, width), BF16),
            pltpu.VMEM((N_KV, LANES, width), BF16),
            pltpu.VMEM((N_KV, CHUNK, width), F32),
            pltpu.VMEM((N_KV, CHUNK, width), F32),
            pltpu.VMEM((2 * N_KV, LANES, width), BF16),
            pltpu.VMEM((N_KV, LANES, width), BF16),
            pltpu.VMEM((N_KV, V_ROWS, width), F32),
            pltpu.VMEM((N_KV, HEAD_DIM, width), F32),
            pltpu.VMEM((N_KV, nsel, LANES), F32),
        ],
        compiler_params=pltpu.CompilerParams(dimension_semantics=("arbitrary",), vmem_limit_bytes=VMEM_LIMIT),
        name="nsa_attention",
    )(qT, gT, ks, vsT, kw, vwT, kc, vcT, tw, ts, tc, b31, gain_b)


def _mix_and_norm(x_ref, attn_ref, conv_ref, wout_ref, g2_ref, rows=slice(None)):
    x1 = x_ref[rows, :] + jnp.dot(attn_ref[rows, :], wout_ref[0:ATTN_W, :], preferred_element_type=F32) \
        + jnp.dot(conv_ref[rows, :], wout_ref[ATTN_W:, :], preferred_element_type=F32)
    ms = jnp.mean(x1 * x1, axis=-1, keepdims=True)
    h2 = x1 * lax.rsqrt(ms + EPS) * g2_ref[...]
    return x1, h2


def _ffn_kernel(x_ref, attn_ref, conv_ref, wout_ref, g2_ref, wg_ref, wu_ref, wd_ref, out_ref):
    x1, h2 = _mix_and_norm(x_ref, attn_ref, conv_ref, wout_ref, g2_ref)
    h2 = h2.astype(BF16)
    a = jnp.dot(h2, wg_ref[...], preferred_element_type=F32)
    u = jnp.dot(h2, wu_ref[...], preferred_element_type=F32)
    y = (a * jax.nn.sigmoid(a) * u).astype(BF16)
    out_ref[...] = x1 + jnp.dot(y, wd_ref[...], preferred_element_type=F32)


_FFN_TM = 512


def _outproj_ffn(x2, attn_n, conv_n, wout, g2, wg, wu, wd):
    S = x2.shape[0]
    tm = _FFN_TM
    resident = pl.BlockSpec(memory_space=pltpu.VMEM)
    return pl.pallas_call(
        _ffn_kernel,
        grid=(S // tm,),
        in_specs=[
            pl.BlockSpec((tm, D_MODEL), lambda i: (i, 0)),
            pl.BlockSpec((tm, ATTN_W), lambda i: (i, 0)),
            pl.BlockSpec((tm, CONV_W), lambda i: (i, 0)),
            resident, resident, resident, resident, resident,
        ],
        out_specs=pl.BlockSpec((tm, D_MODEL), lambda i: (i, 0)),
        out_shape=jax.ShapeDtypeStruct((S, D_MODEL), F32),
        compiler_params=pltpu.CompilerParams(dimension_semantics=("arbitrary",), vmem_limit_bytes=VMEM_LIMIT),
        name="outproj_ffn",
    )(x2, attn_n, conv_n, wout, g2, wg, wu, wd)


TOKEN_TILE = (D_MODEL // LANES, LANES)


def _to_token_tiles(ref, rows):
    x = jnp.stack([rows[:, k * LANES:(k + 1) * LANES] for k in range(TOKEN_TILE[0])], axis=0)
    ref[...] = pltpu.einshape('ktl->tkl', x)


def _from_token_tiles(ref):
    x = pltpu.einshape('tkl->ktl', ref[...])
    return jnp.concatenate([x[k] for k in range(TOKEN_TILE[0])], axis=1)


_ROUTER_SPLIT = 2


def _router_kernel(x_ref, attn_ref, conv_ref, wout_ref, g2_ref, rw_ref, rb_ref, tri_ref,
                   x1_ref, h2_ref, route_ref, cnt_ref, run_ref):
    tm = _FFN_TM
    i = pl.program_id(0)

    @pl.when(i == 0)
    def _():
        run_ref[...] = jnp.zeros(run_ref.shape, F32)

    running = run_ref[0:1, :]
    for part in range(_ROUTER_SPLIT):
        rows = slice(part * tm, (part + 1) * tm)
        x1, h2 = _mix_and_norm(x_ref, attn_ref, conv_ref, wout_ref, g2_ref, rows)
        x1_ref[rows, :] = x1
        h2b = h2.astype(BF16)
        _to_token_tiles(h2_ref.at[rows], h2b.astype(F32))
        lane = lax.broadcasted_iota(I32, (tm, LANES), 1)
        logits = jnp.dot(h2b, rw_ref[...], preferred_element_type=F32) + rb_ref[...]
        logits = jnp.where(lane < N_EXPERTS, logits, -jnp.inf)
        m1 = jnp.max(logits, axis=-1, keepdims=True)
        i1 = jnp.min(jnp.where(logits == m1, lane, LANES), axis=-1, keepdims=True)
        rest = jnp.where(lane == i1, -jnp.inf, logits)
        m2 = jnp.max(rest, axis=-1, keepdims=True)
        i2 = jnp.min(jnp.where(rest == m2, lane, LANES), axis=-1, keepdims=True)
        e2 = jnp.exp(m2 - m1)
        den = 1.0 + e2
        oh1 = (lane == i1).astype(F32)
        oh2 = (lane == i2).astype(F32)
        both = oh1 + oh2
        before = running + jnp.dot(tri_ref[...], both.astype(BF16), preferred_element_type=F32)
        rank1 = jnp.sum(before * oh1, axis=-1, keepdims=True)
        rank2 = jnp.sum(before * oh2, axis=-1, keepdims=True)
        fields = (i1.astype(F32), i2.astype(F32), rank1, rank2, 1.0 / den, e2 / den)
        route = jnp.zeros((tm, LANES), F32)
        for k, v in enumerate(fields):
            route = jnp.where(lane == k, v, route)
        route_ref[rows, :] = route
        running = running + jnp.sum(both, axis=0, keepdims=True)
    run_ref[...] = jnp.broadcast_to(running, run_ref.shape)
    cnt_ref[...] = run_ref[...]


def _outproj_router(x2, attn_n, conv_n, wout, g2, rw, rb):
    S = x2.shape[0]
    tm = _FFN_TM
    tb = _ROUTER_SPLIT * tm
    tri = np.tril(np.ones((tm, tm), np.float32), -1)
    const = lambda shape: pl.BlockSpec(shape, lambda i: (0,) * len(shape))
    return pl.pallas_call(
        _router_kernel,
        grid=(S // tb,),
        in_specs=[
            pl.BlockSpec((tb, D_MODEL), lambda i: (i, 0)),
            pl.BlockSpec((tb, ATTN_W), lambda i: (i, 0)),
            pl.BlockSpec((tb, CONV_W), lambda i: (i, 0)),
            const(wout.shape), const((1, D_MODEL)), const(rw.shape), const((1, LANES)), const((tm, tm)),
        ],
        out_specs=(pl.BlockSpec((tb, D_MODEL), lambda i: (i, 0)),
                   pl.BlockSpec((tb,) + TOKEN_TILE, lambda i: (i, 0, 0)),
                   pl.BlockSpec((tb, LANES), lambda i: (i, 0)),
                   const((8, LANES))),
        out_shape=(jax.ShapeDtypeStruct((S, D_MODEL), F32),
                   jax.ShapeDtypeStruct((S,) + TOKEN_TILE, F32),
                   jax.ShapeDtypeStruct((S, LANES), F32),
                   jax.ShapeDtypeStruct((8, LANES), F32)),
        scratch_shapes=[pltpu.VMEM((8, LANES), F32)],
        compiler_params=pltpu.CompilerParams(dimension_semantics=("arbitrary",), vmem_limit_bytes=VMEM_LIMIT),
        name="outproj_router",
    )(x2, attn_n, conv_n, wout, g2, rw, rb, jnp.asarray(tri, BF16))


_ROW_TM = 256
_EXP_TM = 512
_DMA_UNROLL = 8


def _row_copy(src_ref, src_row, dst_ref, dst_row, sem):
    return pltpu.make_async_copy(src_ref.at[src_row], dst_ref.at[dst_row], sem)


def _dispatch_kernel(pos_ref, ztile_ref, h_ref, xs_ref, hbuf_ref, zbuf_ref, sem, zsem):
    tm = h_ref.shape[0]
    i = pl.program_id(0)
    slot = i % 2

    @pl.when(i == 0)
    def _():
        zbuf_ref[...] = jnp.zeros(zbuf_ref.shape, F32)
        for z in range(ztile_ref.shape[0]):
            clear = pltpu.make_async_copy(zbuf_ref, xs_ref.at[pl.ds(ztile_ref[z] * _EXP_TM, _EXP_TM)], zsem)
            clear.start()
            clear.wait()

    hbuf_ref[slot] = h_ref[...]

    def issue(t, carry):
        for k in range(2):
            _row_copy(hbuf_ref.at[slot], t, xs_ref, pos_ref[0, 0, 2 * t + k], sem.at[slot]).start(priority=k)
        return carry

    lax.fori_loop(0, tm, issue, 0, unroll=_DMA_UNROLL)

    def drain(which):
        def body(t, carry):
            for k in range(2):
                _row_copy(hbuf_ref.at[which], 0, xs_ref, 0, sem.at[which]).wait()
            return carry
        lax.fori_loop(0, tm, body, 0, unroll=_DMA_UNROLL)

    @pl.when(i > 0)
    def _():
        drain(1 - slot)

    @pl.when(i == pl.num_programs(0) - 1)
    def _():
        drain(slot)


def _dispatch(pos3, pad_tiles, h2, n_rows):
    S = h2.shape[0]
    tm = _ROW_TM
    return pl.pallas_call(
        _dispatch_kernel,
        grid=(S // tm,),
        in_specs=[
            pl.BlockSpec((1, 1, 2 * tm), lambda i: (i, 0, 0), memory_space=pltpu.SMEM),
            pl.BlockSpec(memory_space=pltpu.SMEM),
            pl.BlockSpec((tm,) + TOKEN_TILE, lambda i: (i, 0, 0)),
        ],
        out_specs=pl.BlockSpec(memory_space=pl.ANY),
        out_shape=jax.ShapeDtypeStruct((n_rows,) + TOKEN_TILE, F32),
        scratch_shapes=[pltpu.VMEM((2, tm) + TOKEN_TILE, F32), pltpu.VMEM((_EXP_TM,) + TOKEN_TILE, F32),
                        pltpu.SemaphoreType.DMA((2,)), pltpu.SemaphoreType.DMA],
        compiler_params=pltpu.CompilerParams(dimension_semantics=("arbitrary",), vmem_limit_bytes=VMEM_LIMIT),
        name="moe_dispatch",
    )(pos3, pad_tiles, h2)


def _experts_kernel(te_ref, tb_ref, nt_ref, xs_ref, wg_ref, wu_ref, wd_ref, ys_ref):
    i = pl.program_id(0)

    @pl.when(i < nt_ref[0])
    def _():
        x = _from_token_tiles(xs_ref).astype(BF16)
        a = jnp.dot(x, wg_ref[0], preferred_element_type=F32)
        u = jnp.dot(x, wu_ref[0], preferred_element_type=F32)
        y = (a * jax.nn.sigmoid(a) * u).astype(BF16)
        _to_token_tiles(ys_ref, jnp.dot(y, wd_ref[0], preferred_element_type=F32))

    @pl.when(i >= nt_ref[0])
    def _():
        ys_ref[...] = jnp.zeros(ys_ref.shape, F32)


def _experts(tile_e, tile_b, n_tiles, xs, wg, wu, wd):
    n_rows = xs.shape[0]
    tm = _EXP_TM
    weights = lambda shape: pl.BlockSpec(shape, lambda i, te, tb, nt: (te[i], 0, 0), pipeline_mode=pl.Buffered(1))
    grid_spec = pltpu.PrefetchScalarGridSpec(
        num_scalar_prefetch=3,
        grid=(n_rows // tm,),
        in_specs=[
            pl.BlockSpec((tm,) + TOKEN_TILE, lambda i, te, tb, nt: (tb[i], 0, 0)),
            weights((1, D_MODEL, D_FF)), weights((1, D_MODEL, D_FF)), weights((1, D_FF, D_MODEL)),
        ],
        out_specs=pl.BlockSpec((tm,) + TOKEN_TILE, lambda i, te, tb, nt: (tb[i], 0, 0)),
    )
    return pl.pallas_call(
        _experts_kernel,
        grid_spec=grid_spec,
        out_shape=jax.ShapeDtypeStruct((n_rows,) + TOKEN_TILE, F32),
        compiler_params=pltpu.CompilerParams(dimension_semantics=("arbitrary",), vmem_limit_bytes=VMEM_LIMIT),
        name="moe_experts",
    )(tile_e, tile_b, n_tiles, xs, wg, wu, wd)


def _combine_kernel(pos_ref, nxt_ref, x1_ref, route_ref, ys_ref, out_ref, y1_ref, y2_ref, sem):
    tm = x1_ref.shape[0]
    i = pl.program_id(0)
    slot = i % 2

    def issue(src_pos_ref, which):
        def body(t, carry):
            _row_copy(ys_ref, src_pos_ref[0, 0, 2 * t], y1_ref.at[which], t, sem.at[which]).start(priority=0)
            _row_copy(ys_ref, src_pos_ref[0, 0, 2 * t + 1], y2_ref.at[which], t, sem.at[which]).start(priority=1)
            return carry
        lax.fori_loop(0, tm, body, 0, unroll=_DMA_UNROLL)

    @pl.when(i == 0)
    def _():
        issue(pos_ref, slot)

    @pl.when(i + 1 < pl.num_programs(0))
    def _():
        issue(nxt_ref, 1 - slot)

    def drain(t, carry):
        _row_copy(ys_ref, 0, y1_ref.at[slot], 0, sem.at[slot]).wait()
        _row_copy(ys_ref, 0, y2_ref.at[slot], 0, sem.at[slot]).wait()
        return carry

    lax.fori_loop(0, tm, drain, 0, unroll=_DMA_UNROLL)
    lane = lax.broadcasted_iota(I32, (tm, LANES), 1)
    route = route_ref[...]
    w1 = jnp.sum(jnp.where(lane == 4, route, 0.0), axis=-1, keepdims=True)
    w2 = jnp.sum(jnp.where(lane == 5, route, 0.0), axis=-1, keepdims=True)
    out_ref[...] = x1_ref[...] + (_from_token_tiles(y1_ref.at[slot]) * w1 + _from_token_tiles(y2_ref.at[slot]) * w2)


def _combine(pos3, x1, route, ys):
    S = x1.shape[0]
    tm = _ROW_TM
    last = S // tm - 1
    return pl.pallas_call(
        _combine_kernel,
        grid=(S // tm,),
        in_specs=[
            pl.BlockSpec((1, 1, 2 * tm), lambda i: (i, 0, 0), memory_space=pltpu.SMEM),
            pl.BlockSpec((1, 1, 2 * tm), lambda i: (jnp.minimum(i + 1, last), 0, 0), memory_space=pltpu.SMEM),
            pl.BlockSpec((tm, D_MODEL), lambda i: (i, 0)),
            pl.BlockSpec((tm, LANES), lambda i: (i, 0)),
            pl.BlockSpec(memory_space=pl.ANY),
        ],
        out_specs=pl.BlockSpec((tm, D_MODEL), lambda i: (i, 0)),
        out_shape=jax.ShapeDtypeStruct((S, D_MODEL), F32),
        scratch_shapes=[pltpu.VMEM((2, tm) + TOKEN_TILE, F32), pltpu.VMEM((2, tm) + TOKEN_TILE, F32),
                        pltpu.SemaphoreType.DMA((2,))],
        compiler_params=pltpu.CompilerParams(dimension_semantics=("arbitrary",), vmem_limit_bytes=VMEM_LIMIT),
        name="moe_combine",
    )(pos3, pos3, x1, route, ys)


def _moe(h2, x1, route, counts, wg, wu, wd):
    S = h2.shape[0]
    tm = _EXP_TM
    n_tiles_max = 2 * S // tm + N_EXPERTS
    cnt = counts[0, 0:N_EXPERTS].astype(I32)
    tiles = (cnt + tm - 1) // tm
    first = jnp.cumsum(tiles) - tiles
    n_tiles = jnp.sum(tiles)
    idx = jnp.arange(n_tiles_max, dtype=I32)
    last = jnp.minimum(idx, n_tiles - 1)
    tile_e = (jnp.sum(last[:, None] >= first[None, :], axis=1) - 1).astype(I32)
    eid = route[:, 0:2].astype(I32)
    pos = first[eid] * tm + route[:, 2:4].astype(I32)
    pos3 = pos.reshape(S // _ROW_TM, 1, 2 * _ROW_TM)
    pad_tiles = jnp.concatenate([jnp.maximum(first + tiles - 1, 0),
                                 jnp.minimum(n_tiles + jnp.arange(N_EXPERTS, dtype=I32), n_tiles_max - 1)])
    xs = _dispatch(pos3, pad_tiles.astype(I32), h2, n_tiles_max * tm)
    ys = _experts(tile_e, idx, n_tiles.reshape(1), xs, wg, wu, wd)
    return _combine(pos3, x1, route, ys)


def _split_w_in(w):
    o = np.cumsum([0, ATTN_W] + [KV_W] * 6 + [N_BRANCH * N_HEADS] + [CONV_W] * 3)
    q, kc, vc, ksl, vsl, kwn, vwn, gts, cb, cc, ch = (w[:, o[i]:o[i + 1]] for i in range(11))
    perm = np.array([h * N_BRANCH + br for br in range(N_BRANCH) for h in range(N_HEADS)])
    wtok = jnp.concatenate([kc, vc, ksl, kwn, cb, cc, ch], axis=1).astype(BF16)
    feat = jnp.concatenate([q, vsl, vwn, gts[:, perm], jnp.zeros((D_MODEL, 8), w.dtype)], axis=1)
    return wtok, feat.T.astype(BF16)


def _expand_cmp_w1(w1k, w1v):
    kinds = jnp.stack([w1k, w1v]).reshape(2, CMP_LEN, HEAD_DIM, CMP_HIDDEN)
    per_col = jnp.repeat(kinds, N_KV, axis=0)
    eye = jnp.eye(2 * N_KV, dtype=F32)
    out = []
    for l0 in (0, CMP_STRIDE):
        w = per_col[:, l0:l0 + CMP_STRIDE].transpose(1, 0, 2, 3)
        blk = w[:, :, :, None, :] * eye[None, :, None, :, None]
        out.append(blk.reshape(CMP_STRIDE * 2 * KV_W, 2 * N_KV * CMP_HIDDEN).astype(BF16))
    return out


def kernel(x, rel_bias, norm1, w_in, q_norm, k_norm, cmp_pos_k, cmp_pos_v, cmp_k_w1, cmp_k_w2, cmp_v_w1, cmp_v_w2,
           conv_w, attn_out_norm, conv_out_norm, w_out, norm2, ffn_w_gate, ffn_w_up, ffn_w_down, router_w, router_b,
           moe_w_gate, moe_w_up, moe_w_down):
    B, S, _ = x.shape
    assert B == 1 and S % CHUNK == 0 and S >= WINDOW + Q_BLOCK
    depth = norm1.shape[0]
    x2 = x.reshape(S, D_MODEL)
    tables = _bias_tables(rel_bias)
    for layer in range(depth):
        wtok, wfeat = _split_w_in(w_in[layer])
        kgain = jnp.concatenate([jnp.tile(k_norm[layer, 1], N_KV), jnp.tile(k_norm[layer, 2], N_KV)])[None, :]
        gq = jnp.broadcast_to(q_norm[layer][:, None], (HEAD_DIM, _IN_TM))
        convw = jnp.pad(conv_w[layer], ((0, 8 - CONV_K), (0, 0)))
        qT, gT, vsT, vwT, ks, kw, kcv, conv_n = _in_proj(
            x2, norm1[layer][None, :], wtok, wfeat, kgain, gq, convw, conv_out_norm[layer][None, :])

        wtop, wbot = _expand_cmp_w1(cmp_k_w1[layer], cmp_v_w1[layer])
        posk = jnp.broadcast_to(cmp_pos_k[layer].reshape(1, -1), (8, CMP_LEN * HEAD_DIM))
        posv = jnp.broadcast_to(cmp_pos_v[layer].reshape(1, -1), (8, CMP_LEN * HEAD_DIM))
        w2k = jnp.pad(cmp_k_w2[layer], ((0, 0), (0, LANES - HEAD_DIM))).astype(BF16)
        w2vT = cmp_v_w2[layer].T.astype(BF16)
        kcgain = jnp.pad(k_norm[layer, 0], (0, LANES - HEAD_DIM))[None, :]
        kc, vcT = _compress(kcv, wtop, wbot, posk, posv, cmp_k_w1[layer], cmp_v_w1[layer], w2k, w2vT, kcgain)

        gain_b = jnp.broadcast_to(attn_out_norm[layer].reshape(N_KV, GQA, HEAD_DIM).transpose(0, 2, 1)[:, :, :, None],
                                  (N_KV, HEAD_DIM, GQA, LANES)).reshape(N_KV, HEAD_DIM, GQA * LANES)
        attn_n = _attention(qT, gT, ks, vsT, kw, vwT, kc, vcT, tables, gain_b)

        wout = w_out[layer].astype(BF16)
        g2 = norm2[layer][None, :]
        i = layer // 2
        if layer % 2 == 0:
            x2 = _outproj_ffn(x2, attn_n, conv_n, wout, g2, ffn_w_gate[i].astype(BF16), ffn_w_up[i].astype(BF16),
                              ffn_w_down[i].astype(BF16))
        else:
            rw = jnp.pad(router_w[i], ((0, 0), (0, LANES - N_EXPERTS))).astype(BF16)
            rb = jnp.pad(router_b[i], (0, LANES - N_EXPERTS))[None, :]
            x1, h2, route, counts = _outproj_router(x2, attn_n, conv_n, wout, g2, rw, rb)
            x2 = _moe(h2, x1, route, counts, moe_w_gate[i].astype(BF16), moe_w_up[i].astype(BF16),
                      moe_w_down[i].astype(BF16))
    return x2.reshape(B, S, D_MODEL)
```

```python
import functools
import math

import numpy as np
import jax
import jax.numpy as jnp
from jax import lax
from jax.experimental import pallas as pl
from jax.experimental.pallas import tpu as pltpu

F32 = jnp.float32
BF16 = jnp.bfloat16
I32 = jnp.int32

D_MODEL = 1024
HEAD_DIM = 64
N_HEADS = 8
N_KV = 2
GQA = N_HEADS // N_KV
ATTN_W = N_HEADS * HEAD_DIM
KV_W = N_KV * HEAD_DIM
N_BRANCH = 3
CONV_W = 512
CONV_K = 3
CMP_LEN = 32
CMP_STRIDE = 16
CMP_HIDDEN = 128
SEL_LEN = 64
SEL_TOPK = 16
WINDOW = 512
Q_BLOCK = 128
N_BUCKETS = 32
MAX_DISTANCE = 128
D_FF = 2816
N_EXPERTS = 8
EPS = 1e-6
NEG = -1e30
MASK_BIG = 2.0 ** 60
MASK_ROWS = 16
SCALE = HEAD_DIM ** -0.5
LOG2E = math.log2(math.e)

LANES = 128
V_ROWS = 80
CHUNK = 512
BLOCKS_PER_CHUNK = CHUNK // SEL_LEN
TC_LEAD = 24
TC_ROWS = 88
TC_WIN = 48
CMP_STEP = 256
FAR_UNROLL = 4
VMEM_LIMIT = 56 * 1024 * 1024


def _bucket_np(dist):
    n = np.maximum(dist, 0)
    max_exact = N_BUCKETS // 2
    nf = np.maximum(n, max_exact).astype(np.float64)
    v = np.log(nf / max_exact) / math.log(MAX_DISTANCE / max_exact) * (N_BUCKETS - max_exact)
    frac = np.abs(v - np.round(v))
    assert np.all((frac > 1e-6) | (n <= max_exact) | (n >= MAX_DISTANCE)), "bucket boundary is precision dependent"
    large = np.minimum(max_exact + (v + 1e-9).astype(np.int32), N_BUCKETS - 1)
    return np.where(n < max_exact, n, large).astype(np.int32)


def _index_tables():
    tl = np.arange(Q_BLOCK)[None, :]
    r = np.arange(WINDOW + Q_BLOCK + WINDOW)[:, None]
    d = tl + WINDOW - r
    idx_w = np.where((d >= 0) & (d < WINDOW), _bucket_np(d), -1)
    r = np.arange(3 * Q_BLOCK)[:, None]
    d = tl + Q_BLOCK - r
    idx_s = np.where(d >= 0, _bucket_np(d), -1)
    r = np.arange(TC_ROWS)[:, None] - TC_LEAD
    d = tl - CMP_STRIDE * r + (CMP_STRIDE * 16 - (CMP_LEN - 1))
    idx_c = np.where((d >= 0) & (r < 32), _bucket_np(d), -1)
    return idx_w.astype(np.int32), idx_s.astype(np.int32), idx_c.astype(np.int32)


def _tables_kernel(rb_ref, iw_ref, is_ref, ic_ref, tw_ref, ts_ref, tc_ref, b31_ref):
    head_lanes = [(h // GQA, slice((h % GQA) * LANES, (h % GQA + 1) * LANES)) for h in range(N_HEADS)]
    for idx_ref, out_ref in ((iw_ref, tw_ref), (is_ref, ts_ref), (ic_ref, tc_ref)):
        out_ref[...] = jnp.full(out_ref.shape, NEG, F32)

        def body(b, carry, idx_ref=idx_ref, out_ref=out_ref):
            hit = idx_ref[...] == b
            for h, (g, lanes) in enumerate(head_lanes):
                out_ref[g, :, lanes] = jnp.where(hit, rb_ref[b, h] * LOG2E, out_ref[g, :, lanes])
            return carry

        lax.fori_loop(0, N_BUCKETS, body, 0)
    for h, (g, lanes) in enumerate(head_lanes):
        far_bias = rb_ref[N_BUCKETS - 1, h] * LOG2E
        tc_ref[g, :, lanes] = jnp.where(ic_ref[...] >= 0, tc_ref[g, :, lanes] - far_bias, NEG)
        b31_ref[g, :, lanes] = jnp.full((8, LANES), far_bias, F32)


def _bias_tables(rel_bias):
    idx_w, idx_s, idx_c = _index_tables()
    width = GQA * LANES
    out_shape = (
        jax.ShapeDtypeStruct((N_KV, idx_w.shape[0], width), F32),
        jax.ShapeDtypeStruct((N_KV, idx_s.shape[0], width), F32),
        jax.ShapeDtypeStruct((N_KV, idx_c.shape[0], width), F32),
        jax.ShapeDtypeStruct((N_KV, 8, width), F32),
    )
    vmem = pl.BlockSpec(memory_space=pltpu.VMEM)
    return pl.pallas_call(
        _tables_kernel,
        out_shape=out_shape,
        in_specs=[pl.BlockSpec(memory_space=pltpu.SMEM), vmem, vmem, vmem],
        out_specs=(vmem, vmem, vmem, vmem),
        name="bias_tables",
    )(rel_bias, jnp.asarray(idx_w), jnp.asarray(idx_s), jnp.asarray(idx_c))


def _in_proj_kernel(x_ref, g1_ref, wtok_ref, wfeat_ref, ind_ref, kgain_ref, aug_ref, gq_ref, convw_ref, cgain_ref,
                    qT_ref, gT_ref, vsT_ref, vwT_ref, ks_ref, kw_ref, kcv_ref, convn_ref, zs_ref, carry_ref):
    tm = _IN_TM
    i = pl.program_id(0)

    @pl.when(i == 0)
    def _():
        carry_ref[...] = jnp.zeros(carry_ref.shape, F32)

    prev_tail = carry_ref[...]
    for part in range(_IN_SPLIT):
        rows = slice(part * tm, (part + 1) * tm)
        x = x_ref[rows, :]
        ms = jnp.mean(x * x, axis=-1, keepdims=True)
        h = (x * lax.rsqrt(ms + EPS) * g1_ref[...]).astype(BF16)
        tok = jnp.dot(h, wtok_ref[...], preferred_element_type=F32)
        feat = lax.dot_general(wfeat_ref[...], h, (((1,), (1,)), ((), ())),
                               preferred_element_type=F32)

        kcv_ref[rows, :] = tok[:, 0:2 * KV_W].astype(BF16)
        kk = tok[:, 2 * KV_W:4 * KV_W]
        sq = kk * kk
        sq_hi = sq.astype(BF16)
        sq_lo = (sq - sq_hi.astype(F32)).astype(BF16)
        ssq = jnp.dot(jnp.concatenate([sq_hi, sq_lo], axis=1), ind_ref[...],
                      preferred_element_type=F32)
        kn = kk * lax.rsqrt(ssq * (1.0 / HEAD_DIM) + EPS) * kgain_ref[...]
        ksl = kn[:, 0:KV_W]
        lane = lax.broadcasted_iota(I32, (tm, LANES), 1)
        aug = aug_ref[...]
        ks_ref[0, rows, :] = jnp.where(lane < HEAD_DIM, ksl, aug).astype(BF16)
        ks_ref[1, rows, :] = jnp.where(lane < HEAD_DIM, pltpu.roll(ksl, HEAD_DIM, 1), aug).astype(BF16)
        kw_ref[rows, :] = kn[:, KV_W:2 * KV_W].astype(BF16)

        c0 = 4 * KV_W
        cb = tok[:, c0:c0 + CONV_W]
        cc = tok[:, c0 + CONV_W:c0 + 2 * CONV_W]
        ch = tok[:, c0 + 2 * CONV_W:c0 + 3 * CONV_W]
        z = cc * ch
        zs_ref[part, 0:8, :] = prev_tail
        zs_ref[part, 8:8 + tm, :] = z
        z1 = zs_ref[part, 7:7 + tm, :]
        z2 = zs_ref[part, 6:6 + tm, :]
        w = convw_ref[...]
        y = w[0:1, :] * z2 + w[1:2, :] * z1 + w[2:3, :] * z
        prev_tail = z[tm - 8:tm, :]
        oc = cb * y
        msc = jnp.mean(oc * oc, axis=-1, keepdims=True)
        convn_ref[rows, :] = (oc * lax.rsqrt(msc + EPS) * cgain_ref[...]).astype(BF16)

        q = feat[0:ATTN_W].reshape(N_HEADS, HEAD_DIM, tm)
        qss = jnp.sum(q * q, axis=1, keepdims=True)
        qn = q * lax.rsqrt(qss * (1.0 / HEAD_DIM) + EPS) * gq_ref[...][None]
        qT_ref[:, rows] = (qn * (SCALE * LOG2E)).reshape(ATTN_W, tm).astype(BF16)
        ones_rows = (lax.broadcasted_iota(I32, (V_ROWS - HEAD_DIM, tm), 0) == 0).astype(BF16)
        for g in range(N_KV):
            r0 = ATTN_W + g * HEAD_DIM
            vsT_ref[g, 0:HEAD_DIM, rows] = feat[r0:r0 + HEAD_DIM].astype(BF16)
            vsT_ref[g, HEAD_DIM:V_ROWS, rows] = ones_rows
            r1 = ATTN_W + KV_W + g * HEAD_DIM
            vwT_ref[g, 0:HEAD_DIM, rows] = feat[r1:r1 + HEAD_DIM].astype(BF16)
            vwT_ref[g, HEAD_DIM:V_ROWS, rows] = ones_rows
        g0 = ATTN_W + 2 * KV_W
        gT_ref[:, rows] = jax.nn.sigmoid(feat[g0:g0 + N_BRANCH * N_HEADS])
    carry_ref[...] = prev_tail


_IN_SPLIT = 2
_IN_TM = 512


def _in_proj(x2, g1, wtok, wfeat, kgain, gq, convw, cgain):
    S = x2.shape[0]
    tm = _IN_TM
    tb = _IN_SPLIT * tm
    nt = S // tb
    ind = np.kron(np.eye(2 * N_KV, dtype=np.float32), np.ones((HEAD_DIM, HEAD_DIM), np.float32))
    ind = np.concatenate([ind, ind], axis=0)
    aug = np.zeros((tm, LANES), np.float32)
    blk = (np.arange(tm) // SEL_LEN) % BLOCKS_PER_CHUNK
    aug[np.arange(tm), HEAD_DIM + blk] = 1.0
    aug[:, HEAD_DIM + BLOCKS_PER_CHUNK:HEAD_DIM + BLOCKS_PER_CHUNK + 2] = 1.0
    const = lambda shape: pl.BlockSpec(shape, lambda i: (0,) * len(shape))
    out_shape = (
        jax.ShapeDtypeStruct((ATTN_W, S), BF16),
        jax.ShapeDtypeStruct((N_BRANCH * N_HEADS, S), F32),
        jax.ShapeDtypeStruct((N_KV, V_ROWS, S), BF16),
        jax.ShapeDtypeStruct((N_KV, V_ROWS, S), BF16),
        jax.ShapeDtypeStruct((N_KV, S, LANES), BF16),
        jax.ShapeDtypeStruct((S, LANES), BF16),
        jax.ShapeDtypeStruct((S, 2 * KV_W), BF16),
        jax.ShapeDtypeStruct((S, CONV_W), BF16),
    )
    out_specs = (
        pl.BlockSpec((ATTN_W, tb), lambda i: (0, i)),
        pl.BlockSpec((N_BRANCH * N_HEADS, tb), lambda i: (0, i)),
        pl.BlockSpec((N_KV, V_ROWS, tb), lambda i: (0, 0, i)),
        pl.BlockSpec((N_KV, V_ROWS, tb), lambda i: (0, 0, i)),
        pl.BlockSpec((N_KV, tb, LANES), lambda i: (0, i, 0)),
        pl.BlockSpec((tb, LANES), lambda i: (i, 0)),
        pl.BlockSpec((tb, 2 * KV_W), lambda i: (i, 0)),
        pl.BlockSpec((tb, CONV_W), lambda i: (i, 0)),
    )
    in_specs = [
        pl.BlockSpec((tb, D_MODEL), lambda i: (i, 0)),
        const((1, D_MODEL)),
        const(wtok.shape),
        const(wfeat.shape),
        const(ind.shape),
        const((1, 2 * KV_W)),
        const(aug.shape),
        const((HEAD_DIM, tm)),
        const((8, CONV_W)),
        const((1, CONV_W)),
    ]
    return pl.pallas_call(
        _in_proj_kernel,
        grid=(nt,),
        in_specs=in_specs,
        out_specs=out_specs,
        out_shape=out_shape,
        scratch_shapes=[pltpu.VMEM((_IN_SPLIT, tm + 8, CONV_W), F32), pltpu.VMEM((8, CONV_W), F32)],
        compiler_params=pltpu.CompilerParams(dimension_semantics=("arbitrary",), vmem_limit_bytes=VMEM_LIMIT),
        name="in_proj",
    )(x2, g1, wtok, wfeat, jnp.asarray(ind, BF16), kgain, jnp.asarray(aug), gq, convw, cgain)


def _compress_kernel(r_ref, wtop_ref, wbot_ref, posk_ref, posv_ref, w1k_ref, w1v_ref, w2k_ref, w2vT_ref, kgain_ref,
                     kc_ref, vcT_ref):
    nr = r_ref.shape[0]
    r = r_ref[...]
    u = jnp.dot(r, wtop_ref[...], preferred_element_type=F32)
    lo = jnp.dot(r, wbot_ref[...], preferred_element_type=F32)
    bias_k = jnp.dot(posk_ref[...], w1k_ref[...], preferred_element_type=F32)[0:1, :]
    bias_v = jnp.dot(posv_ref[...], w1v_ref[...], preferred_element_type=F32)[0:1, :]
    bias4 = jnp.concatenate([bias_k] * N_KV + [bias_v] * N_KV, axis=1)
    hid = u + pltpu.roll(lo, nr - 1, 0) + bias4
    act = jax.nn.gelu(hid)
    for g in range(N_KV):
        ak = act[:, g * CMP_HIDDEN:(g + 1) * CMP_HIDDEN].astype(BF16)
        av = act[:, (N_KV + g) * CMP_HIDDEN:(N_KV + g + 1) * CMP_HIDDEN].astype(BF16)
        kc = jnp.dot(ak, w2k_ref[...], preferred_element_type=F32)
        ssq = jnp.sum(kc * kc, axis=-1, keepdims=True)
        kc_ref[g] = (kc * lax.rsqrt(ssq * (1.0 / HEAD_DIM) + EPS) * kgain_ref[...]).astype(BF16)
        vcT_ref[g] = lax.dot_general(w2vT_ref[...], av, (((1,), (1,)), ((), ())),
                                     preferred_element_type=F32).astype(BF16)


def _compress(kcv, wtop, wbot, posk, posv, w1k, w1v, w2k, w2vT, kgain):
    S = kcv.shape[0]
    nr = S // CMP_STRIDE
    r = kcv.reshape(nr, CMP_STRIDE * 2 * KV_W)
    vmem = pl.BlockSpec(memory_space=pltpu.VMEM)
    return pl.pallas_call(
        _compress_kernel,
        out_shape=(jax.ShapeDtypeStruct((N_KV, nr, LANES), BF16),
                   jax.ShapeDtypeStruct((N_KV, HEAD_DIM, nr), BF16)),
        in_specs=[vmem] * 10,
        out_specs=(vmem, vmem),
        compiler_params=pltpu.CompilerParams(vmem_limit_bytes=VMEM_LIMIT),
        name="compress",
    )(r, wtop, wbot, posk, posv, w1k, w1v, w2k, w2vT, kgain)


def _tile4(row):
    return jnp.concatenate([row] * GQA, axis=1)


def _attn_kernel(qT_ref, gT_ref, ks_ref, vsT_ref, kw_ref, vwT_ref, kc_ref, vcT_ref,
                 tw_ref, ts_ref, tc_ref, b31_ref, gain_ref, out_ref,
                 sc_ref, sw_ref, imp_ref, amask_ref, qs_ref, sa_ref, sb_ref, qz_ref, qw_ref, fin_ref, ocmp_ref,
                 pk_ref):
    qw_refs = tuple(qw_ref.at[k] for k in range(FAR_UNROLL))
    ncr = kc_ref.shape[1]
    noct = amask_ref.shape[1] - 1
    nsel = noct * BLOCKS_PER_CHUNK
    width = GQA * LANES
    c = pl.program_id(0)
    t0 = c * Q_BLOCK
    tl = lax.broadcasted_iota(I32, (1, LANES), 1)
    cur = 2 * c + (tl >= SEL_LEN).astype(I32)
    zb = jnp.maximum(2 * c - 2, 0)
    nfar = (zb + BLOCKS_PER_CHUNK - 1) // BLOCKS_PER_CHUNK
    zs = pl.multiple_of(zb * SEL_LEN, Q_BLOCK)
    ts_off = pl.multiple_of(zs - (t0 - Q_BLOCK), Q_BLOCK)
    zeros_q = jnp.zeros((HEAD_DIM, width), BF16)

    col_ok = _tile4(t0 + tl) >= CMP_LEN - 1
    j = lax.broadcasted_iota(I32, (nsel, LANES), 0)
    forced = (j == 0) | (j == cur) | (j == cur - 1)
    @pl.when(c == 0)
    def _():
        imp_ref[...] = jnp.zeros(imp_ref.shape, F32)


    ws = pl.multiple_of(jnp.maximum(t0 - WINDOW, 0), Q_BLOCK)
    tw_off = pl.multiple_of(ws - (t0 - WINDOW), Q_BLOCK)
    nw = WINDOW + Q_BLOCK
    lo = pl.multiple_of(jnp.clip((8 * c - 16) // 16 * 16, 0, ncr - TC_WIN), 16)
    tc_off = pl.multiple_of(lo - (8 * c - 16) + TC_LEAD, 8)
    qgs = [jnp.concatenate([qT_ref[(GQA * g + r) * HEAD_DIM:(GQA * g + r + 1) * HEAD_DIM, :]
                            for r in range(GQA)], axis=1) for g in range(N_KV)]

    for g in range(N_KV):
        qw = jnp.concatenate([qgs[g], zeros_q] if g == 0 else [zeros_q, qgs[g]], axis=0)
        sw_ref[g] = (jnp.dot(kw_ref[pl.ds(ws, nw), :], qw, preferred_element_type=F32)
                     + tw_ref[g, pl.ds(tw_off, nw), :])

    def compressed(rows):
        for g in range(N_KV):
            qc = jnp.concatenate([qgs[g], zeros_q], axis=0)
            s = jnp.dot(kc_ref[g, 0:rows, :], qc, preferred_element_type=F32)
            row = lax.broadcasted_iota(I32, (rows, width), 0)
            sc_ref[g, 0:rows, :] = jnp.where(row < lo, s, NEG)
            s_loc = jnp.dot(kc_ref[g, pl.ds(lo, TC_WIN), :], qc, preferred_element_type=F32)
            sc_ref[g, pl.ds(lo, TC_WIN), :] = s_loc + tc_ref[g, pl.ds(tc_off, TC_WIN), :]
        for g in range(N_KV):
            s = sc_ref[g, 0:rows, :]
            m = jnp.max(s, axis=0, keepdims=True)
            e = jnp.exp2(s - m)
            l = jnp.sum(e, axis=0, keepdims=True)
            p = e * jnp.where(col_ok, 1.0 / jnp.maximum(l, 1e-30), 0.0)
            ocmp_ref[g] = jnp.dot(vcT_ref[g, :, 0:rows], p.astype(BF16), preferred_element_type=F32)
            imp_ref[g, 8:8 + rows, :] = (
                (p[:, 0:LANES] + p[:, LANES:2 * LANES]) + p[:, 2 * LANES:3 * LANES] + p[:, 3 * LANES:4 * LANES])

    sizes = list(range(CMP_STEP, ncr, CMP_STEP)) + [ncr]
    variant = jnp.minimum((lo + TC_WIN + CMP_STEP - 1) // CMP_STEP, len(sizes)) - 1
    for v, rows in enumerate(sizes):
        pl.when(variant == v)(functools.partial(compressed, rows))

    o_cmp, scores0 = [], []
    for g in range(N_KV):
        o_cmp.append(ocmp_ref[g])
        isel = jnp.zeros((nsel, LANES), F32)
        for off in (0, -1, 1, 0, 2, 1, 3, 2):
            isel = isel + imp_ref[g, pl.ds(8 + off, nsel, stride=4), :]
        score = jnp.where(forced, -jnp.inf, isel)
        scores0.append(jnp.where(j > cur, NEG, score))

    o_win = []
    for g in range(N_KV):
        sw = sw_ref[g]
        mw = jnp.max(sw, axis=0, keepdims=True)
        pw = jnp.exp2(sw - mw)
        accw = jnp.dot(vwT_ref[g, :, pl.ds(ws, nw)], pw.astype(BF16), preferred_element_type=F32)
        o_win.append(accw[0:HEAD_DIM] * (1.0 / jnp.maximum(accw[HEAD_DIM:HEAD_DIM + 1], 1e-30)))

    n_picks = SEL_TOPK - 3
    fast = list(scores0)
    for _ in range(n_picks):
        for g in range(N_KV):
            mx = jnp.max(fast[g], axis=0, keepdims=True)
            fast[g] = jnp.where(fast[g] == mx, -jnp.inf, fast[g])
    eligible = (j <= cur) & jnp.logical_not(forced)
    n_eligible = cur + 1 - (1 + (cur >= 1).astype(I32) + (cur >= 2).astype(I32))
    expect = jnp.minimum(n_picks, n_eligible).astype(F32)
    wrong = jnp.zeros((1, LANES), F32)
    for g in range(N_KV):
        pk_ref[g] = fast[g]
        got = jnp.sum(jnp.where((fast[g] == -jnp.inf) & eligible, 1.0, 0.0), axis=0, keepdims=True)
        wrong = jnp.maximum(wrong, jnp.abs(got - expect))

    @pl.when(jnp.max(wrong) > 0.0)
    def _():
        def pick(_, score):
            mx = jnp.max(score, axis=0, keepdims=True)
            first = jnp.min(jnp.where(score == mx, j, nsel), axis=0, keepdims=True)
            return jnp.where(j == first, -jnp.inf, score)

        for g in range(N_KV):
            pk_ref[g] = lax.fori_loop(0, n_picks, pick, scores0[g])

    picks = [pk_ref[g] for g in range(N_KV)]

    m_init, acc_init, gates = [], [], []
    for g in range(N_KV):
        qg = qgs[g]
        b31row = b31_ref[g, 0:1, :]
        chosen = (picks[g] == -jnp.inf) & (j <= cur)
        a_all = _tile4(jnp.where(chosen, 0.0, -MASK_BIG)).reshape(noct, BLOCKS_PER_CHUNK, width)
        b_hi = b31row.astype(BF16).astype(F32)
        erow = lax.broadcasted_iota(I32, (8, width), 0)
        extra = jnp.where(erow == 0, b_hi, jnp.where(erow == 1, b31row - b_hi, 0.0))
        amask_ref[g, 0:noct] = jnp.concatenate(
            [a_all, jnp.broadcast_to(extra[None], (noct, 8, width))], axis=1).astype(BF16)
        far_last = jnp.maximum(nfar - 1, 0)
        orow = lax.broadcasted_iota(I32, (MASK_ROWS, width), 0)
        keep = (orow >= BLOCKS_PER_CHUNK) | (orow + BLOCKS_PER_CHUNK * far_last < zb)
        amask_ref[g, noct] = jnp.where(keep, amask_ref[g, far_last].astype(F32), -MASK_BIG).astype(BF16)
        for q_ref in (qs_ref,) + qw_refs:
            q_ref[g, 0:HEAD_DIM, :] = qg
            q_ref[g, HEAD_DIM + MASK_ROWS:LANES, :] = jnp.zeros((LANES - HEAD_DIM - MASK_ROWS, width), BF16)

        halves = []
        for h in range(2):
            octet = amask_ref[g, (zb + 2 * h) // BLOCKS_PER_CHUNK].astype(F32)
            qz_ref[2 * g + h, HEAD_DIM:HEAD_DIM + MASK_ROWS, :] = jnp.where(
                orow < BLOCKS_PER_CHUNK, octet, 0.0).astype(BF16)
            qz_ref[2 * g + h, 0:HEAD_DIM, :] = qg
            qz_ref[2 * g + h, HEAD_DIM + MASK_ROWS:LANES, :] = jnp.zeros((LANES - HEAD_DIM - MASK_ROWS, width), BF16)
            sh = jnp.dot(ks_ref[g, pl.ds(zs + h * Q_BLOCK, Q_BLOCK), :], qz_ref[2 * g + h], preferred_element_type=F32)
            halves.append(sh + ts_ref[g, pl.ds(ts_off + h * Q_BLOCK, Q_BLOCK), :])
        sz = jnp.concatenate(halves, axis=0)
        mz = jnp.max(sz, axis=0, keepdims=True)
        pz = jnp.exp2(sz - mz)
        m_init.append(mz)
        acc_init.append(jnp.dot(vsT_ref[g, :, pl.ds(zs, 2 * Q_BLOCK)], pz.astype(BF16), preferred_element_type=F32))
        gates.append([jnp.concatenate([gT_ref[br * N_HEADS + GQA * g + r:br * N_HEADS + GQA * g + r + 1, :]
                                       for r in range(GQA)], axis=1) for br in range(N_BRANCH)])

    nsub = 1
    sub = CHUNK // nsub

    def score(g, u, buf_ref, h, q_ref):
        k0 = pl.multiple_of(jnp.minimum(u, noct - 1) * CHUNK + h * sub, sub)
        s = jnp.dot(ks_ref[g, pl.ds(k0, sub), :], q_ref[g], preferred_element_type=F32)
        buf_ref[g, h * sub:(h + 1) * sub, :] = s
        return jnp.max(s, axis=0, keepdims=True)

    def set_mask_rows(g, u, q_ref):
        oct_id = jnp.where(u < nfar - 1, u, noct)
        q_ref[g, HEAD_DIM:HEAD_DIM + MASK_ROWS, :] = amask_ref[g, oct_id]

    def accumulate(g, u, buf_ref, h, mn, acc):
        k0 = pl.multiple_of(jnp.minimum(u, noct - 1) * CHUNK + h * sub, sub)
        p = jnp.exp2(buf_ref[g, h * sub:(h + 1) * sub, :] - mn)
        return acc + jnp.dot(vsT_ref[g, :, pl.ds(k0, sub)], p.astype(BF16), preferred_element_type=F32)

    def step(u, cur_ref, nxt_ref, q_ref, state, cms):
        out_state, out_cms = [], []
        for g in range(N_KV):
            m, acc = state[2 * g], state[2 * g + 1]
            mn = jnp.maximum(m, cms[g])
            acc = jnp.exp2(m - mn) * acc
            set_mask_rows(g, u + 1, q_ref)
            cm = None
            for h in range(nsub):
                ch = score(g, u + 1, nxt_ref, h, q_ref)
                cm = ch if cm is None else jnp.maximum(cm, ch)
                acc = accumulate(g, u, cur_ref, h, mn, acc)
            out_cms.append(cm)
            out_state += [mn, acc]
        return out_state, out_cms

    bufs = (sa_ref, sb_ref)

    def far(i, carry):
        state, cms = list(carry[0:4]), list(carry[4:6])
        for k in range(FAR_UNROLL):
            state, cms = step(FAR_UNROLL * i + k, bufs[k % 2], bufs[(k + 1) % 2], qw_refs[k], state, cms)
        return tuple(state + cms)

    first = []
    for g in range(N_KV):
        set_mask_rows(g, 0, qs_ref)
        cm = None
        for h in range(nsub):
            ch = score(g, 0, sa_ref, h, qs_ref)
            cm = ch if cm is None else jnp.maximum(cm, ch)
        first.append(cm)
    nchunks = jnp.maximum(nfar, 1)
    nloops = (nchunks - 1) // FAR_UNROLL
    carry = lax.fori_loop(0, nloops, far, (m_init[0], acc_init[0], m_init[1], acc_init[1], first[0], first[1]))
    u_last = FAR_UNROLL * nloops

    def finish(left):
        for g in range(N_KV):
            m, acc, cm = carry[2 * g], carry[2 * g + 1], carry[4 + g]
            for k in range(left):
                mn = jnp.maximum(m, cm)
                acc = jnp.exp2(m - mn) * acc
                if k + 1 < left:
                    set_mask_rows(g, u_last + k + 1, qw_refs[k])
                    cm = score(g, u_last + k + 1, bufs[(k + 1) % 2], 0, qw_refs[k])
                acc = accumulate(g, u_last + k, bufs[k % 2], 0, mn, acc)
                m = mn
            fin_ref[g] = acc

    for left in range(1, FAR_UNROLL + 1):
        pl.when(nchunks - u_last == left)(functools.partial(finish, left))
    carry = [None, fin_ref[0], None, fin_ref[1]]

    o_all, ssq = [], jnp.zeros((1, LANES), F32)
    for g in range(N_KV):
        acc = carry[2 * g + 1]
        o_sel = acc[0:HEAD_DIM] * (1.0 / jnp.maximum(acc[HEAD_DIM:HEAD_DIM + 1], 1e-30))
        o = gates[g][0] * o_cmp[g] + gates[g][1] * o_sel + gates[g][2] * o_win[g]
        o_all.append(o)
        cs = jnp.sum(o * o, axis=0, keepdims=True)
        ssq = ssq + ((cs[:, 0:LANES] + cs[:, LANES:2 * LANES]) + (cs[:, 2 * LANES:3 * LANES] + cs[:, 3 * LANES:]))
    inv = _tile4(lax.rsqrt(ssq * (1.0 / ATTN_W) + EPS))
    for g in range(N_KV):
        on = o_all[g] * inv * gain_ref[g]
        for k in range(GQA // 2):
            pair = jnp.concatenate([on[:, (2 * k) * LANES:(2 * k + 1) * LANES],
                                    on[:, (2 * k + 1) * LANES:(2 * k + 2) * LANES]], axis=0)
            col = (GQA * g + 2 * k) * HEAD_DIM
            out_ref[:, col:col + 2 * HEAD_DIM] = pair.T.astype(BF16)


def _attention(qT, gT, ks, vsT, kw, vwT, kc, vcT, tables, gain_b):
    S = qT.shape[1]
    nq = S // Q_BLOCK
    ncr = kc.shape[1]
    nsel = S // SEL_LEN
    noct = S // CHUNK
    width = GQA * LANES
    tw, ts, tc, b31 = tables
    vmem = pl.BlockSpec(memory_space=pltpu.VMEM)
    in_specs = [
        pl.BlockSpec((ATTN_W, Q_BLOCK), lambda c: (0, c)),
        pl.BlockSpec((N_BRANCH * N_HEADS, Q_BLOCK), lambda c: (0, c)),
    ] + [vmem] * 11
    return pl.pallas_call(
        _attn_kernel,
        grid=(nq,),
        in_specs=in_specs,
        out_specs=pl.BlockSpec((Q_BLOCK, ATTN_W), lambda c: (c, 0)),
        out_shape=jax.ShapeDtypeStruct((S, ATTN_W), BF16),
        scratch_shapes=[
            pltpu.VMEM((N_KV, ncr, width), F32),
            pltpu.VMEM((N_KV, WINDOW + Q_BLOCK, width), F32),
            pltpu.VMEM((N_KV, ncr + 16, LANES), F32),
            pltpu.VMEM((N_KV, noct + 1, MASK_ROWS, width), BF16),
            pltpu.VMEM((N_KV, LANES, width), BF16),
            pltpu.VMEM((N_KV, CHUNK, width), F32),
            pltpu.VMEM((N_KV, CHUNK, width), F32),
            pltpu.VMEM((2 * N_KV, LANES, width), BF16),
            pltpu.VMEM((FAR_UNROLL, N_KV, LANES, width), BF16),
            pltpu.VMEM((N_KV, V_ROWS, width), F32),
            pltpu.VMEM((N_KV, HEAD_DIM, width), F32),
            pltpu.VMEM((N_KV, nsel, LANES), F32),
        ],
        compiler_params=pltpu.CompilerParams(dimension_semantics=("arbitrary",), vmem_limit_bytes=VMEM_LIMIT),
        name="nsa_attention",
    )(qT, gT, ks, vsT, kw, vwT, kc, vcT, tw, ts, tc, b31, gain_b)


def _mix_and_norm(x_ref, attn_ref, conv_ref, wout_ref, g2_ref, rows=slice(None)):
    x1 = x_ref[rows, :] + jnp.dot(attn_ref[rows, :], wout_ref[0:ATTN_W, :], preferred_element_type=F32) \
        + jnp.dot(conv_ref[rows, :], wout_ref[ATTN_W:, :], preferred_element_type=F32)
    ms = jnp.mean(x1 * x1, axis=-1, keepdims=True)
    h2 = x1 * lax.rsqrt(ms + EPS) * g2_ref[...]
    return x1, h2


def _ffn_kernel(x_ref, attn_ref, conv_ref, wout_ref, g2_ref, wg_ref, wu_ref, wd_ref, out_ref):
    x1, h2 = _mix_and_norm(x_ref, attn_ref, conv_ref, wout_ref, g2_ref)
    h2 = h2.astype(BF16)
    a = jnp.dot(h2, wg_ref[...], preferred_element_type=F32)
    u = jnp.dot(h2, wu_ref[...], preferred_element_type=F32)
    y = (a * jax.nn.sigmoid(a) * u).astype(BF16)
    out_ref[...] = x1 + jnp.dot(y, wd_ref[...], preferred_element_type=F32)


_FFN_TM = 512


def _outproj_ffn(x2, attn_n, conv_n, wout, g2, wg, wu, wd):
    S = x2.shape[0]
    tm = _FFN_TM
    resident = pl.BlockSpec(memory_space=pltpu.VMEM)
    return pl.pallas_call(
        _ffn_kernel,
        grid=(S // tm,),
        in_specs=[
            pl.BlockSpec((tm, D_MODEL), lambda i: (i, 0)),
            pl.BlockSpec((tm, ATTN_W), lambda i: (i, 0)),
            pl.BlockSpec((tm, CONV_W), lambda i: (i, 0)),
            resident, resident, resident, resident, resident,
        ],
        out_specs=pl.BlockSpec((tm, D_MODEL), lambda i: (i, 0)),
        out_shape=jax.ShapeDtypeStruct((S, D_MODEL), F32),
        compiler_params=pltpu.CompilerParams(dimension_semantics=("arbitrary",), vmem_limit_bytes=VMEM_LIMIT),
        name="outproj_ffn",
    )(x2, attn_n, conv_n, wout, g2, wg, wu, wd)


TOKEN_TILE = (D_MODEL // LANES, LANES)


def _to_token_tiles(ref, rows):
    x = jnp.stack([rows[:, k * LANES:(k + 1) * LANES] for k in range(TOKEN_TILE[0])], axis=0)
    ref[...] = pltpu.einshape('ktl->tkl', x)


def _from_token_tiles(ref):
    x = pltpu.einshape('tkl->ktl', ref[...])
    return jnp.concatenate([x[k] for k in range(TOKEN_TILE[0])], axis=1)


_ROUTER_SPLIT = 2


def _router_kernel(x_ref, attn_ref, conv_ref, wout_ref, g2_ref, rw_ref, rb_ref, tri_ref,
                   x1_ref, h2_ref, route_ref, cnt_ref, run_ref):
    tm = _FFN_TM
    i = pl.program_id(0)

    @pl.when(i == 0)
    def _():
        run_ref[...] = jnp.zeros(run_ref.shape, F32)

    running = run_ref[0:1, :]
    for part in range(_ROUTER_SPLIT):
        rows = slice(part * tm, (part + 1) * tm)
        x1, h2 = _mix_and_norm(x_ref, attn_ref, conv_ref, wout_ref, g2_ref, rows)
        x1_ref[rows, :] = x1
        h2b = h2.astype(BF16)
        _to_token_tiles(h2_ref.at[rows], h2b.astype(F32))
        lane = lax.broadcasted_iota(I32, (tm, LANES), 1)
        logits = jnp.dot(h2b, rw_ref[...], preferred_element_type=F32) + rb_ref[...]
        logits = jnp.where(lane < N_EXPERTS, logits, -jnp.inf)
        m1 = jnp.max(logits, axis=-1, keepdims=True)
        i1 = jnp.min(jnp.where(logits == m1, lane, LANES), axis=-1, keepdims=True)
        rest = jnp.where(lane == i1, -jnp.inf, logits)
        m2 = jnp.max(rest, axis=-1, keepdims=True)
        i2 = jnp.min(jnp.where(rest == m2, lane, LANES), axis=-1, keepdims=True)
        e2 = jnp.exp(m2 - m1)
        den = 1.0 + e2
        oh1 = (lane == i1).astype(F32)
        oh2 = (lane == i2).astype(F32)
        both = oh1 + oh2
        before = running + jnp.dot(tri_ref[...], both.astype(BF16), preferred_element_type=F32)
        rank1 = jnp.sum(before * oh1, axis=-1, keepdims=True)
        rank2 = jnp.sum(before * oh2, axis=-1, keepdims=True)
        fields = (i1.astype(F32), i2.astype(F32), rank1, rank2, 1.0 / den, e2 / den)
        route = jnp.zeros((tm, LANES), F32)
        for k, v in enumerate(fields):
            route = jnp.where(lane == k, v, route)
        route_ref[rows, :] = route
        running = running + jnp.sum(both, axis=0, keepdims=True)
    run_ref[...] = jnp.broadcast_to(running, run_ref.shape)
    cnt_ref[...] = run_ref[...]


def _outproj_router(x2, attn_n, conv_n, wout, g2, rw, rb):
    S = x2.shape[0]
    tm = _FFN_TM
    tb = _ROUTER_SPLIT * tm
    tri = np.tril(np.ones((tm, tm), np.float32), -1)
    const = lambda shape: pl.BlockSpec(shape, lambda i: (0,) * len(shape))
    return pl.pallas_call(
        _router_kernel,
        grid=(S // tb,),
        in_specs=[
            pl.BlockSpec((tb, D_MODEL), lambda i: (i, 0)),
            pl.BlockSpec((tb, ATTN_W), lambda i: (i, 0)),
            pl.BlockSpec((tb, CONV_W), lambda i: (i, 0)),
            const(wout.shape), const((1, D_MODEL)), const(rw.shape), const((1, LANES)), const((tm, tm)),
        ],
        out_specs=(pl.BlockSpec((tb, D_MODEL), lambda i: (i, 0)),
                   pl.BlockSpec((tb,) + TOKEN_TILE, lambda i: (i, 0, 0)),
                   pl.BlockSpec((tb, LANES), lambda i: (i, 0)),
                   const((8, LANES))),
        out_shape=(jax.ShapeDtypeStruct((S, D_MODEL), F32),
                   jax.ShapeDtypeStruct((S,) + TOKEN_TILE, F32),
                   jax.ShapeDtypeStruct((S, LANES), F32),
                   jax.ShapeDtypeStruct((8, LANES), F32)),
        scratch_shapes=[pltpu.VMEM((8, LANES), F32)],
        compiler_params=pltpu.CompilerParams(dimension_semantics=("arbitrary",), vmem_limit_bytes=VMEM_LIMIT),
        name="outproj_router",
    )(x2, attn_n, conv_n, wout, g2, rw, rb, jnp.asarray(tri, BF16))


_ROW_TM = 256
_EXP_TM = 512
_DMA_UNROLL = 8


def _row_copy(src_ref, src_row, dst_ref, dst_row, sem):
    return pltpu.make_async_copy(src_ref.at[src_row], dst_ref.at[dst_row], sem)


def _dispatch_kernel(pos_ref, ztile_ref, h_ref, xs_ref, hbuf_ref, zbuf_ref, sem, zsem):
    tm = h_ref.shape[0]
    i = pl.program_id(0)
    slot = i % 2

    @pl.when(i == 0)
    def _():
        zbuf_ref[...] = jnp.zeros(zbuf_ref.shape, F32)
        for z in range(ztile_ref.shape[0]):
            clear = pltpu.make_async_copy(zbuf_ref, xs_ref.at[pl.ds(ztile_ref[z] * _EXP_TM, _EXP_TM)], zsem)
            clear.start()
            clear.wait()

    hbuf_ref[slot] = h_ref[...]

    def issue(t, carry):
        for k in range(2):
            _row_copy(hbuf_ref.at[slot], t, xs_ref, pos_ref[0, 0, 2 * t + k], sem.at[slot]).start(priority=k)
        return carry

    lax.fori_loop(0, tm, issue, 0, unroll=_DMA_UNROLL)

    def drain(which):
        def body(t, carry):
            for k in range(2):
                _row_copy(hbuf_ref.at[which], 0, xs_ref, 0, sem.at[which]).wait()
            return carry
        lax.fori_loop(0, tm, body, 0, unroll=_DMA_UNROLL)

    @pl.when(i > 0)
    def _():
        drain(1 - slot)

    @pl.when(i == pl.num_programs(0) - 1)
    def _():
        drain(slot)


def _dispatch(pos3, pad_tiles, h2, n_rows):
    S = h2.shape[0]
    tm = _ROW_TM
    return pl.pallas_call(
        _dispatch_kernel,
        grid=(S // tm,),
        in_specs=[
            pl.BlockSpec((1, 1, 2 * tm), lambda i: (i, 0, 0), memory_space=pltpu.SMEM),
            pl.BlockSpec(memory_space=pltpu.SMEM),
            pl.BlockSpec((tm,) + TOKEN_TILE, lambda i: (i, 0, 0)),
        ],
        out_specs=pl.BlockSpec(memory_space=pl.ANY),
        out_shape=jax.ShapeDtypeStruct((n_rows,) + TOKEN_TILE, F32),
        scratch_shapes=[pltpu.VMEM((2, tm) + TOKEN_TILE, F32), pltpu.VMEM((_EXP_TM,) + TOKEN_TILE, F32),
                        pltpu.SemaphoreType.DMA((2,)), pltpu.SemaphoreType.DMA],
        compiler_params=pltpu.CompilerParams(dimension_semantics=("arbitrary",), vmem_limit_bytes=VMEM_LIMIT),
        name="moe_dispatch",
    )(pos3, pad_tiles, h2)


def _experts_kernel(te_ref, tb_ref, nt_ref, xs_ref, wg_ref, wu_ref, wd_ref, ys_ref):
    i = pl.program_id(0)

    @pl.when(i < nt_ref[0])
    def _():
        x = _from_token_tiles(xs_ref).astype(BF16)
        a = jnp.dot(x, wg_ref[0], preferred_element_type=F32)
        u = jnp.dot(x, wu_ref[0], preferred_element_type=F32)
        y = (a * jax.nn.sigmoid(a) * u).astype(BF16)
        _to_token_tiles(ys_ref, jnp.dot(y, wd_ref[0], preferred_element_type=F32))

    @pl.when(i >= nt_ref[0])
    def _():
        ys_ref[...] = jnp.zeros(ys_ref.shape, F32)


def _experts(tile_e, tile_b, n_tiles, xs, wg, wu, wd):
    n_rows = xs.shape[0]
    tm = _EXP_TM
    weights = lambda shape: pl.BlockSpec(shape, lambda i, te, tb, nt: (te[i], 0, 0), pipeline_mode=pl.Buffered(1))
    grid_spec = pltpu.PrefetchScalarGridSpec(
        num_scalar_prefetch=3,
        grid=(n_rows // tm,),
        in_specs=[
            pl.BlockSpec((tm,) + TOKEN_TILE, lambda i, te, tb, nt: (tb[i], 0, 0)),
            weights((1, D_MODEL, D_FF)), weights((1, D_MODEL, D_FF)), weights((1, D_FF, D_MODEL)),
        ],
        out_specs=pl.BlockSpec((tm,) + TOKEN_TILE, lambda i, te, tb, nt: (tb[i], 0, 0)),
    )
    return pl.pallas_call(
        _experts_kernel,
        grid_spec=grid_spec,
        out_shape=jax.ShapeDtypeStruct((n_rows,) + TOKEN_TILE, F32),
        compiler_params=pltpu.CompilerParams(dimension_semantics=("arbitrary",), vmem_limit_bytes=VMEM_LIMIT),
        name="moe_experts",
    )(tile_e, tile_b, n_tiles, xs, wg, wu, wd)


def _combine_kernel(pos_ref, nxt_ref, x1_ref, route_ref, ys_ref, out_ref, y1_ref, y2_ref, sem):
    tm = x1_ref.shape[0]
    i = pl.program_id(0)
    slot = i % 2

    def issue(src_pos_ref, which):
        def body(t, carry):
            _row_copy(ys_ref, src_pos_ref[0, 0, 2 * t], y1_ref.at[which], t, sem.at[which]).start(priority=0)
            _row_copy(ys_ref, src_pos_ref[0, 0, 2 * t + 1], y2_ref.at[which], t, sem.at[which]).start(priority=1)
            return carry
        lax.fori_loop(0, tm, body, 0, unroll=_DMA_UNROLL)

    @pl.when(i == 0)
    def _():
        issue(pos_ref, slot)

    @pl.when(i + 1 < pl.num_programs(0))
    def _():
        issue(nxt_ref, 1 - slot)

    def drain(t, carry):
        _row_copy(ys_ref, 0, y1_ref.at[slot], 0, sem.at[slot]).wait()
        _row_copy(ys_ref, 0, y2_ref.at[slot], 0, sem.at[slot]).wait()
        return carry

    lax.fori_loop(0, tm, drain, 0, unroll=_DMA_UNROLL)
    lane = lax.broadcasted_iota(I32, (tm, LANES), 1)
    route = route_ref[...]
    w1 = jnp.sum(jnp.where(lane == 4, route, 0.0), axis=-1, keepdims=True)
    w2 = jnp.sum(jnp.where(lane == 5, route, 0.0), axis=-1, keepdims=True)
    out_ref[...] = x1_ref[...] + (_from_token_tiles(y1_ref.at[slot]) * w1 + _from_token_tiles(y2_ref.at[slot]) * w2)


def _combine(pos3, x1, route, ys):
    S = x1.shape[0]
    tm = _ROW_TM
    last = S // tm - 1
    return pl.pallas_call(
        _combine_kernel,
        grid=(S // tm,),
        in_specs=[
            pl.BlockSpec((1, 1, 2 * tm), lambda i: (i, 0, 0), memory_space=pltpu.SMEM),
            pl.BlockSpec((1, 1, 2 * tm), lambda i: (jnp.minimum(i + 1, last), 0, 0), memory_space=pltpu.SMEM),
            pl.BlockSpec((tm, D_MODEL), lambda i: (i, 0)),
            pl.BlockSpec((tm, LANES), lambda i: (i, 0)),
            pl.BlockSpec(memory_space=pl.ANY),
        ],
        out_specs=pl.BlockSpec((tm, D_MODEL), lambda i: (i, 0)),
        out_shape=jax.ShapeDtypeStruct((S, D_MODEL), F32),
        scratch_shapes=[pltpu.VMEM((2, tm) + TOKEN_TILE, F32), pltpu.VMEM((2, tm) + TOKEN_TILE, F32),
                        pltpu.SemaphoreType.DMA((2,))],
        compiler_params=pltpu.CompilerParams(dimension_semantics=("arbitrary",), vmem_limit_bytes=VMEM_LIMIT),
        name="moe_combine",
    )(pos3, pos3, x1, route, ys)


def _moe(h2, x1, route, counts, wg, wu, wd):
    S = h2.shape[0]
    tm = _EXP_TM
    n_tiles_max = 2 * S // tm + N_EXPERTS
    cnt = counts[0, 0:N_EXPERTS].astype(I32)
    tiles = (cnt + tm - 1) // tm
    first = jnp.cumsum(tiles) - tiles
    n_tiles = jnp.sum(tiles)
    idx = jnp.arange(n_tiles_max, dtype=I32)
    last = jnp.minimum(idx, n_tiles - 1)
    tile_e = (jnp.sum(last[:, None] >= first[None, :], axis=1) - 1).astype(I32)
    eid = route[:, 0:2].astype(I32)
    pos = first[eid] * tm + route[:, 2:4].astype(I32)
    pos3 = pos.reshape(S // _ROW_TM, 1, 2 * _ROW_TM)
    pad_tiles = jnp.concatenate([jnp.maximum(first + tiles - 1, 0),
                                 jnp.minimum(n_tiles + jnp.arange(N_EXPERTS, dtype=I32), n_tiles_max - 1)])
    xs = _dispatch(pos3, pad_tiles.astype(I32), h2, n_tiles_max * tm)
    ys = _experts(tile_e, idx, n_tiles.reshape(1), xs, wg, wu, wd)
    return _combine(pos3, x1, route, ys)


def _split_w_in(w):
    o = np.cumsum([0, ATTN_W] + [KV_W] * 6 + [N_BRANCH * N_HEADS] + [CONV_W] * 3)
    q, kc, vc, ksl, vsl, kwn, vwn, gts, cb, cc, ch = (w[:, o[i]:o[i + 1]] for i in range(11))
    perm = np.array([h * N_BRANCH + br for br in range(N_BRANCH) for h in range(N_HEADS)])
    wtok = jnp.concatenate([kc, vc, ksl, kwn, cb, cc, ch], axis=1).astype(BF16)
    feat = jnp.concatenate([q, vsl, vwn, gts[:, perm], jnp.zeros((D_MODEL, 8), w.dtype)], axis=1)
    return wtok, feat.T.astype(BF16)


def _expand_cmp_w1(w1k, w1v):
    kinds = jnp.stack([w1k, w1v]).reshape(2, CMP_LEN, HEAD_DIM, CMP_HIDDEN)
    per_col = jnp.repeat(kinds, N_KV, axis=0)
    eye = jnp.eye(2 * N_KV, dtype=F32)
    out = []
    for l0 in (0, CMP_STRIDE):
        w = per_col[:, l0:l0 + CMP_STRIDE].transpose(1, 0, 2, 3)
        blk = w[:, :, :, None, :] * eye[None, :, None, :, None]
        out.append(blk.reshape(CMP_STRIDE * 2 * KV_W, 2 * N_KV * CMP_HIDDEN).astype(BF16))
    return out


def kernel(x, rel_bias, norm1, w_in, q_norm, k_norm, cmp_pos_k, cmp_pos_v, cmp_k_w1, cmp_k_w2, cmp_v_w1, cmp_v_w2,
           conv_w, attn_out_norm, conv_out_norm, w_out, norm2, ffn_w_gate, ffn_w_up, ffn_w_down, router_w, router_b,
           moe_w_gate, moe_w_up, moe_w_down):
    B, S, _ = x.shape
    assert B == 1 and S % CHUNK == 0 and S >= WINDOW + Q_BLOCK
    depth = norm1.shape[0]
    x2 = x.reshape(S, D_MODEL)
    tables = _bias_tables(rel_bias)
    for layer in range(depth):
        wtok, wfeat = _split_w_in(w_in[layer])
        kgain = jnp.concatenate([jnp.tile(k_norm[layer, 1], N_KV), jnp.tile(k_norm[layer, 2], N_KV)])[None, :]
        gq = jnp.broadcast_to(q_norm[layer][:, None], (HEAD_DIM, _IN_TM))
        convw = jnp.pad(conv_w[layer], ((0, 8 - CONV_K), (0, 0)))
        qT, gT, vsT, vwT, ks, kw, kcv, conv_n = _in_proj(
            x2, norm1[layer][None, :], wtok, wfeat, kgain, gq, convw, conv_out_norm[layer][None, :])

        wtop, wbot = _expand_cmp_w1(cmp_k_w1[layer], cmp_v_w1[layer])
        posk = jnp.broadcast_to(cmp_pos_k[layer].reshape(1, -1), (8, CMP_LEN * HEAD_DIM))
        posv = jnp.broadcast_to(cmp_pos_v[layer].reshape(1, -1), (8, CMP_LEN * HEAD_DIM))
        w2k = jnp.pad(cmp_k_w2[layer], ((0, 0), (0, LANES - HEAD_DIM))).astype(BF16)
        w2vT = cmp_v_w2[layer].T.astype(BF16)
        kcgain = jnp.pad(k_norm[layer, 0], (0, LANES - HEAD_DIM))[None, :]
        kc, vcT = _compress(kcv, wtop, wbot, posk, posv, cmp_k_w1[layer], cmp_v_w1[layer], w2k, w2vT, kcgain)

        gain_b = jnp.broadcast_to(attn_out_norm[layer].reshape(N_KV, GQA, HEAD_DIM).transpose(0, 2, 1)[:, :, :, None],
                                  (N_KV, HEAD_DIM, GQA, LANES)).reshape(N_KV, HEAD_DIM, GQA * LANES)
        attn_n = _attention(qT, gT, ks, vsT, kw, vwT, kc, vcT, tables, gain_b)

        wout = w_out[layer].astype(BF16)
        g2 = norm2[layer][None, :]
        i = layer // 2
        if layer % 2 == 0:
            x2 = _outproj_ffn(x2, attn_n, conv_n, wout, g2, ffn_w_gate[i].astype(BF16), ffn_w_up[i].astype(BF16),
                              ffn_w_down[i].astype(BF16))
        else:
            rw = jnp.pad(router_w[i], ((0, 0), (0, LANES - N_EXPERTS))).astype(BF16)
            rb = jnp.pad(router_b[i], (0, LANES - N_EXPERTS))[None, :]
            x1, h2, route, counts = _outproj_router(x2, attn_n, conv_n, wout, g2, rw, rb)
            x2 = _moe(h2, x1, route, counts, moe_w_gate[i].astype(BF16), moe_w_up[i].astype(BF16),
                      moe_w_down[i].astype(BF16))
    return x2.reshape(B, S, D_MODEL)
```

```python
import functools
import math

import numpy as np
import jax
import jax.numpy as jnp
from jax import lax
from jax.experimental import pallas as pl
from jax.experimental.pallas import tpu as pltpu

F32 = jnp.float32
BF16 = jnp.bfloat16
I32 = jnp.int32

D_MODEL = 1024
HEAD_DIM = 64
N_HEADS = 8
N_KV = 2
GQA = N_HEADS // N_KV
ATTN_W = N_HEADS * HEAD_DIM
KV_W = N_KV * HEAD_DIM
N_BRANCH = 3
CONV_W = 512
CONV_K = 3
CMP_LEN = 32
CMP_STRIDE = 16
CMP_HIDDEN = 128
SEL_LEN = 64
SEL_TOPK = 16
WINDOW = 512
Q_BLOCK = 128
N_BUCKETS = 32
MAX_DISTANCE = 128
D_FF = 2816
N_EXPERTS = 8
EPS = 1e-6
NEG = -1e30
MASK_BIG = 2.0 ** 60
MASK_ROWS = 16
SCALE = HEAD_DIM ** -0.5
LOG2E = math.log2(math.e)

LANES = 128
V_ROWS = 80
CHUNK = 512
BLOCKS_PER_CHUNK = CHUNK // SEL_LEN
TC_LEAD = 24
TC_ROWS = 88
TC_WIN = 48
CMP_STEP = 256
FAR_UNROLL = 6
VMEM_LIMIT = 56 * 1024 * 1024


def _bucket_np(dist):
    n = np.maximum(dist, 0)
    max_exact = N_BUCKETS // 2
    nf = np.maximum(n, max_exact).astype(np.float64)
    v = np.log(nf / max_exact) / math.log(MAX_DISTANCE / max_exact) * (N_BUCKETS - max_exact)
    frac = np.abs(v - np.round(v))
    assert np.all((frac > 1e-6) | (n <= max_exact) | (n >= MAX_DISTANCE)), "bucket boundary is precision dependent"
    large = np.minimum(max_exact + (v + 1e-9).astype(np.int32), N_BUCKETS - 1)
    return np.where(n < max_exact, n, large).astype(np.int32)


def _index_tables():
    tl = np.arange(Q_BLOCK)[None, :]
    r = np.arange(WINDOW + Q_BLOCK + WINDOW)[:, None]
    d = tl + WINDOW - r
    idx_w = np.where((d >= 0) & (d < WINDOW), _bucket_np(d), -1)
    r = np.arange(3 * Q_BLOCK)[:, None]
    d = tl + Q_BLOCK - r
    idx_s = np.where(d >= 0, _bucket_np(d), -1)
    r = np.arange(TC_ROWS)[:, None] - TC_LEAD
    d = tl - CMP_STRIDE * r + (CMP_STRIDE * 16 - (CMP_LEN - 1))
    idx_c = np.where((d >= 0) & (r < 32), _bucket_np(d), -1)
    return idx_w.astype(np.int32), idx_s.astype(np.int32), idx_c.astype(np.int32)


def _tables_kernel(rb_ref, iw_ref, is_ref, ic_ref, tw_ref, ts_ref, tc_ref, b31_ref):
    head_lanes = [(h // GQA, slice((h % GQA) * LANES, (h % GQA + 1) * LANES)) for h in range(N_HEADS)]
    for idx_ref, out_ref in ((iw_ref, tw_ref), (is_ref, ts_ref), (ic_ref, tc_ref)):
        out_ref[...] = jnp.full(out_ref.shape, NEG, F32)

        def body(b, carry, idx_ref=idx_ref, out_ref=out_ref):
            hit = idx_ref[...] == b
            for h, (g, lanes) in enumerate(head_lanes):
                out_ref[g, :, lanes] = jnp.where(hit, rb_ref[b, h] * LOG2E, out_ref[g, :, lanes])
            return carry

        lax.fori_loop(0, N_BUCKETS, body, 0)
    for h, (g, lanes) in enumerate(head_lanes):
        far_bias = rb_ref[N_BUCKETS - 1, h] * LOG2E
        tc_ref[g, :, lanes] = jnp.where(ic_ref[...] >= 0, tc_ref[g, :, lanes] - far_bias, NEG)
        b31_ref[g, :, lanes] = jnp.full((8, LANES), far_bias, F32)


def _bias_tables(rel_bias):
    idx_w, idx_s, idx_c = _index_tables()
    width = GQA * LANES
    out_shape = (
        jax.ShapeDtypeStruct((N_KV, idx_w.shape[0], width), F32),
        jax.ShapeDtypeStruct((N_KV, idx_s.shape[0], width), F32),
        jax.ShapeDtypeStruct((N_KV, idx_c.shape[0], width), F32),
        jax.ShapeDtypeStruct((N_KV, 8, width), F32),
    )
    vmem = pl.BlockSpec(memory_space=pltpu.VMEM)
    return pl.pallas_call(
        _tables_kernel,
        out_shape=out_shape,
        in_specs=[pl.BlockSpec(memory_space=pltpu.SMEM), vmem, vmem, vmem],
        out_specs=(vmem, vmem, vmem, vmem),
        name="bias_tables",
    )(rel_bias, jnp.asarray(idx_w), jnp.asarray(idx_s), jnp.asarray(idx_c))


def _in_proj_kernel(x_ref, g1_ref, wtok_ref, wfeat_ref, ind_ref, kgain_ref, aug_ref, gq_ref, convw_ref, cgain_ref,
                    qT_ref, gT_ref, vsT_ref, vwT_ref, ks_ref, kw_ref, kcv_ref, convn_ref, zs_ref, carry_ref):
    tm = _IN_TM
    i = pl.program_id(0)

    @pl.when(i == 0)
    def _():
        carry_ref[...] = jnp.zeros(carry_ref.shape, F32)

    prev_tail = carry_ref[...]
    for part in range(_IN_SPLIT):
        rows = slice(part * tm, (part + 1) * tm)
        x = x_ref[rows, :]
        ms = jnp.mean(x * x, axis=-1, keepdims=True)
        h = (x * lax.rsqrt(ms + EPS) * g1_ref[...]).astype(BF16)
        tok = jnp.dot(h, wtok_ref[...], preferred_element_type=F32)
        feat = lax.dot_general(wfeat_ref[...], h, (((1,), (1,)), ((), ())),
                               preferred_element_type=F32)

        kcv_ref[rows, :] = tok[:, 0:2 * KV_W].astype(BF16)
        kk = tok[:, 2 * KV_W:4 * KV_W]
        sq = kk * kk
        sq_hi = sq.astype(BF16)
        sq_lo = (sq - sq_hi.astype(F32)).astype(BF16)
        ssq = jnp.dot(jnp.concatenate([sq_hi, sq_lo], axis=1), ind_ref[...],
                      preferred_element_type=F32)
        kn = kk * lax.rsqrt(ssq * (1.0 / HEAD_DIM) + EPS) * kgain_ref[...]
        ksl = kn[:, 0:KV_W]
        lane = lax.broadcasted_iota(I32, (tm, LANES), 1)
        aug = aug_ref[...]
        ks_ref[0, rows, :] = jnp.where(lane < HEAD_DIM, ksl, aug).astype(BF16)
        ks_ref[1, rows, :] = jnp.where(lane < HEAD_DIM, pltpu.roll(ksl, HEAD_DIM, 1), aug).astype(BF16)
        kw_ref[rows, :] = kn[:, KV_W:2 * KV_W].astype(BF16)

        c0 = 4 * KV_W
        cb = tok[:, c0:c0 + CONV_W]
        cc = tok[:, c0 + CONV_W:c0 + 2 * CONV_W]
        ch = tok[:, c0 + 2 * CONV_W:c0 + 3 * CONV_W]
        z = cc * ch
        zs_ref[part, 0:8, :] = prev_tail
        zs_ref[part, 8:8 + tm, :] = z
        z1 = zs_ref[part, 7:7 + tm, :]
        z2 = zs_ref[part, 6:6 + tm, :]
        w = convw_ref[...]
        y = w[0:1, :] * z2 + w[1:2, :] * z1 + w[2:3, :] * z
        prev_tail = z[tm - 8:tm, :]
        oc = cb * y
        msc = jnp.mean(oc * oc, axis=-1, keepdims=True)
        convn_ref[rows, :] = (oc * lax.rsqrt(msc + EPS) * cgain_ref[...]).astype(BF16)

        q = feat[0:ATTN_W].reshape(N_HEADS, HEAD_DIM, tm)
        qss = jnp.sum(q * q, axis=1, keepdims=True)
        qn = q * lax.rsqrt(qss * (1.0 / HEAD_DIM) + EPS) * gq_ref[...][None]
        qT_ref[:, rows] = (qn * (SCALE * LOG2E)).reshape(ATTN_W, tm).astype(BF16)
        ones_rows = (lax.broadcasted_iota(I32, (V_ROWS - HEAD_DIM, tm), 0) == 0).astype(BF16)
        for g in range(N_KV):
            r0 = ATTN_W + g * HEAD_DIM
            vsT_ref[g, 0:HEAD_DIM, rows] = feat[r0:r0 + HEAD_DIM].astype(BF16)
            vsT_ref[g, HEAD_DIM:V_ROWS, rows] = ones_rows
            r1 = ATTN_W + KV_W + g * HEAD_DIM
            vwT_ref[g, 0:HEAD_DIM, rows] = feat[r1:r1 + HEAD_DIM].astype(BF16)
            vwT_ref[g, HEAD_DIM:V_ROWS, rows] = ones_rows
        g0 = ATTN_W + 2 * KV_W
        gT_ref[:, rows] = jax.nn.sigmoid(feat[g0:g0 + N_BRANCH * N_HEADS])
    carry_ref[...] = prev_tail


_IN_SPLIT = 2
_IN_TM = 512


def _in_proj(x2, g1, wtok, wfeat, kgain, gq, convw, cgain):
    S = x2.shape[0]
    tm = _IN_TM
    tb = _IN_SPLIT * tm
    nt = S // tb
    ind = np.kron(np.eye(2 * N_KV, dtype=np.float32), np.ones((HEAD_DIM, HEAD_DIM), np.float32))
    ind = np.concatenate([ind, ind], axis=0)
    aug = np.zeros((tm, LANES), np.float32)
    blk = (np.arange(tm) // SEL_LEN) % BLOCKS_PER_CHUNK
    aug[np.arange(tm), HEAD_DIM + blk] = 1.0
    aug[:, HEAD_DIM + BLOCKS_PER_CHUNK:HEAD_DIM + BLOCKS_PER_CHUNK + 2] = 1.0
    const = lambda shape: pl.BlockSpec(shape, lambda i: (0,) * len(shape))
    out_shape = (
        jax.ShapeDtypeStruct((ATTN_W, S), BF16),
        jax.ShapeDtypeStruct((N_BRANCH * N_HEADS, S), F32),
        jax.ShapeDtypeStruct((N_KV, V_ROWS, S), BF16),
        jax.ShapeDtypeStruct((N_KV, V_ROWS, S), BF16),
        jax.ShapeDtypeStruct((N_KV, S, LANES), BF16),
        jax.ShapeDtypeStruct((S, LANES), BF16),
        jax.ShapeDtypeStruct((S, 2 * KV_W), BF16),
        jax.ShapeDtypeStruct((S, CONV_W), BF16),
    )
    out_specs = (
        pl.BlockSpec((ATTN_W, tb), lambda i: (0, i)),
        pl.BlockSpec((N_BRANCH * N_HEADS, tb), lambda i: (0, i)),
        pl.BlockSpec((N_KV, V_ROWS, tb), lambda i: (0, 0, i)),
        pl.BlockSpec((N_KV, V_ROWS, tb), lambda i: (0, 0, i)),
        pl.BlockSpec((N_KV, tb, LANES), lambda i: (0, i, 0)),
        pl.BlockSpec((tb, LANES), lambda i: (i, 0)),
        pl.BlockSpec((tb, 2 * KV_W), lambda i: (i, 0)),
        pl.BlockSpec((tb, CONV_W), lambda i: (i, 0)),
    )
    in_specs = [
        pl.BlockSpec((tb, D_MODEL), lambda i: (i, 0)),
        const((1, D_MODEL)),
        const(wtok.shape),
        const(wfeat.shape),
        const(ind.shape),
        const((1, 2 * KV_W)),
        const(aug.shape),
        const((HEAD_DIM, tm)),
        const((8, CONV_W)),
        const((1, CONV_W)),
    ]
    return pl.pallas_call(
        _in_proj_kernel,
        grid=(nt,),
        in_specs=in_specs,
        out_specs=out_specs,
        out_shape=out_shape,
        scratch_shapes=[pltpu.VMEM((_IN_SPLIT, tm + 8, CONV_W), F32), pltpu.VMEM((8, CONV_W), F32)],
        compiler_params=pltpu.CompilerParams(dimension_semantics=("arbitrary",), vmem_limit_bytes=VMEM_LIMIT),
        name="in_proj",
    )(x2, g1, wtok, wfeat, jnp.asarray(ind, BF16), kgain, jnp.asarray(aug), gq, convw, cgain)


def _compress_kernel(r_ref, wtop_ref, wbot_ref, posk_ref, posv_ref, w1k_ref, w1v_ref, w2k_ref, w2vT_ref, kgain_ref,
                     kc_ref, vcT_ref):
    nr = r_ref.shape[0]
    r = r_ref[...]
    u = jnp.dot(r, wtop_ref[...], preferred_element_type=F32)
    lo = jnp.dot(r, wbot_ref[...], preferred_element_type=F32)
    bias_k = jnp.dot(posk_ref[...], w1k_ref[...], preferred_element_type=F32)[0:1, :]
    bias_v = jnp.dot(posv_ref[...], w1v_ref[...], preferred_element_type=F32)[0:1, :]
    bias4 = jnp.concatenate([bias_k] * N_KV + [bias_v] * N_KV, axis=1)
    hid = u + pltpu.roll(lo, nr - 1, 0) + bias4
    act = jax.nn.gelu(hid)
    for g in range(N_KV):
        ak = act[:, g * CMP_HIDDEN:(g + 1) * CMP_HIDDEN].astype(BF16)
        av = act[:, (N_KV + g) * CMP_HIDDEN:(N_KV + g + 1) * CMP_HIDDEN].astype(BF16)
        kc = jnp.dot(ak, w2k_ref[...], preferred_element_type=F32)
        ssq = jnp.sum(kc * kc, axis=-1, keepdims=True)
        kc_ref[g] = (kc * lax.rsqrt(ssq * (1.0 / HEAD_DIM) + EPS) * kgain_ref[...]).astype(BF16)
        vcT_ref[g] = lax.dot_general(w2vT_ref[...], av, (((1,), (1,)), ((), ())),
                                     preferred_element_type=F32).astype(BF16)


def _compress(kcv, wtop, wbot, posk, posv, w1k, w1v, w2k, w2vT, kgain):
    S = kcv.shape[0]
    nr = S // CMP_STRIDE
    r = kcv.reshape(nr, CMP_STRIDE * 2 * KV_W)
    vmem = pl.BlockSpec(memory_space=pltpu.VMEM)
    return pl.pallas_call(
        _compress_kernel,
        out_shape=(jax.ShapeDtypeStruct((N_KV, nr, LANES), BF16),
                   jax.ShapeDtypeStruct((N_KV, HEAD_DIM, nr), BF16)),
        in_specs=[vmem] * 10,
        out_specs=(vmem, vmem),
        compiler_params=pltpu.CompilerParams(vmem_limit_bytes=VMEM_LIMIT),
        name="compress",
    )(r, wtop, wbot, posk, posv, w1k, w1v, w2k, w2vT, kgain)


def _tile4(row):
    return jnp.concatenate([row] * GQA, axis=1)


def _attn_kernel(qT_ref, gT_ref, ks_ref, vsT_ref, kw_ref, vwT_ref, kc_ref, vcT_ref,
                 tw_ref, ts_ref, tc_ref, b31_ref, gain_ref, out_ref,
                 sc_ref, sw_ref, imp_ref, amask_ref, qs_ref, sa_ref, sb_ref, qz_ref, qw_ref, fin_ref, ocmp_ref,
                 pk_ref):
    qw_refs = tuple(qw_ref.at[k] for k in range(FAR_UNROLL))
    ncr = kc_ref.shape[1]
    noct = amask_ref.shape[1] - 1
    nsel = noct * BLOCKS_PER_CHUNK
    width = GQA * LANES
    c = pl.program_id(0)
    t0 = c * Q_BLOCK
    tl = lax.broadcasted_iota(I32, (1, LANES), 1)
    cur = 2 * c + (tl >= SEL_LEN).astype(I32)
    zb = jnp.maximum(2 * c - 2, 0)
    nfar = (zb + BLOCKS_PER_CHUNK - 1) // BLOCKS_PER_CHUNK
    zs = pl.multiple_of(zb * SEL_LEN, Q_BLOCK)
    ts_off = pl.multiple_of(zs - (t0 - Q_BLOCK), Q_BLOCK)
    zeros_q = jnp.zeros((HEAD_DIM, width), BF16)

    col_ok = _tile4(t0 + tl) >= CMP_LEN - 1
    j = lax.broadcasted_iota(I32, (nsel, LANES), 0)
    forced = (j == 0) | (j == cur) | (j == cur - 1)
    @pl.when(c == 0)
    def _():
        imp_ref[...] = jnp.zeros(imp_ref.shape, F32)


    ws = pl.multiple_of(jnp.maximum(t0 - WINDOW, 0), Q_BLOCK)
    tw_off = pl.multiple_of(ws - (t0 - WINDOW), Q_BLOCK)
    nw = WINDOW + Q_BLOCK
    lo = pl.multiple_of(jnp.clip((8 * c - 16) // 16 * 16, 0, ncr - TC_WIN), 16)
    tc_off = pl.multiple_of(lo - (8 * c - 16) + TC_LEAD, 8)
    qgs = [jnp.concatenate([qT_ref[(GQA * g + r) * HEAD_DIM:(GQA * g + r + 1) * HEAD_DIM, :]
                            for r in range(GQA)], axis=1) for g in range(N_KV)]

    for g in range(N_KV):
        qw = jnp.concatenate([qgs[g], zeros_q] if g == 0 else [zeros_q, qgs[g]], axis=0)
        sw_ref[g] = (jnp.dot(kw_ref[pl.ds(ws, nw), :], qw, preferred_element_type=F32)
                     + tw_ref[g, pl.ds(tw_off, nw), :])

    def compressed(rows):
        for g in range(N_KV):
            qc = jnp.concatenate([qgs[g], zeros_q], axis=0)
            s = jnp.dot(kc_ref[g, 0:rows, :], qc, preferred_element_type=F32)
            row = lax.broadcasted_iota(I32, (rows, width), 0)
            sc_ref[g, 0:rows, :] = jnp.where(row < lo, s, NEG)
            s_loc = jnp.dot(kc_ref[g, pl.ds(lo, TC_WIN), :], qc, preferred_element_type=F32)
            sc_ref[g, pl.ds(lo, TC_WIN), :] = s_loc + tc_ref[g, pl.ds(tc_off, TC_WIN), :]
        for g in range(N_KV):
            s = sc_ref[g, 0:rows, :]
            m = jnp.max(s, axis=0, keepdims=True)
            e = jnp.exp2(s - m)
            l = jnp.sum(e, axis=0, keepdims=True)
            p = e * jnp.where(col_ok, 1.0 / jnp.maximum(l, 1e-30), 0.0)
            ocmp_ref[g] = jnp.dot(vcT_ref[g, :, 0:rows], p.astype(BF16), preferred_element_type=F32)
            imp_ref[g, 8:8 + rows, :] = (
                (p[:, 0:LANES] + p[:, LANES:2 * LANES]) + p[:, 2 * LANES:3 * LANES] + p[:, 3 * LANES:4 * LANES])

    sizes = list(range(CMP_STEP, ncr, CMP_STEP)) + [ncr]
    variant = jnp.minimum((lo + TC_WIN + CMP_STEP - 1) // CMP_STEP, len(sizes)) - 1
    for v, rows in enumerate(sizes):
        pl.when(variant == v)(functools.partial(compressed, rows))

    o_cmp, scores0 = [], []
    for g in range(N_KV):
        o_cmp.append(ocmp_ref[g])
        isel = jnp.zeros((nsel, LANES), F32)
        for off in (0, -1, 1, 0, 2, 1, 3, 2):
            isel = isel + imp_ref[g, pl.ds(8 + off, nsel, stride=4), :]
        score = jnp.where(forced, -jnp.inf, isel)
        scores0.append(jnp.where(j > cur, NEG, score))

    o_win = []
    for g in range(N_KV):
        sw = sw_ref[g]
        mw = jnp.max(sw, axis=0, keepdims=True)
        pw = jnp.exp2(sw - mw)
        accw = jnp.dot(vwT_ref[g, :, pl.ds(ws, nw)], pw.astype(BF16), preferred_element_type=F32)
        o_win.append(accw[0:HEAD_DIM] * (1.0 / jnp.maximum(accw[HEAD_DIM:HEAD_DIM + 1], 1e-30)))

    n_picks = SEL_TOPK - 3
    fast = list(scores0)
    for _ in range(n_picks):
        for g in range(N_KV):
            mx = jnp.max(fast[g], axis=0, keepdims=True)
            fast[g] = jnp.where(fast[g] == mx, -jnp.inf, fast[g])
    eligible = (j <= cur) & jnp.logical_not(forced)
    n_eligible = cur + 1 - (1 + (cur >= 1).astype(I32) + (cur >= 2).astype(I32))
    expect = jnp.minimum(n_picks, n_eligible).astype(F32)
    wrong = jnp.zeros((1, LANES), F32)
    for g in range(N_KV):
        pk_ref[g] = fast[g]
        got = jnp.sum(jnp.where((fast[g] == -jnp.inf) & eligible, 1.0, 0.0), axis=0, keepdims=True)
        wrong = jnp.maximum(wrong, jnp.abs(got - expect))

    @pl.when(jnp.max(wrong) > 0.0)
    def _():
        def pick(_, score):
            mx = jnp.max(score, axis=0, keepdims=True)
            first = jnp.min(jnp.where(score == mx, j, nsel), axis=0, keepdims=True)
            return jnp.where(j == first, -jnp.inf, score)

        for g in range(N_KV):
            pk_ref[g] = lax.fori_loop(0, n_picks, pick, scores0[g])

    picks = [pk_ref[g] for g in range(N_KV)]

    m_init, acc_init, gates = [], [], []
    for g in range(N_KV):
        qg = qgs[g]
        b31row = b31_ref[g, 0:1, :]
        chosen = (picks[g] == -jnp.inf) & (j <= cur)
        a_all = _tile4(jnp.where(chosen, 0.0, -MASK_BIG)).reshape(noct, BLOCKS_PER_CHUNK, width)
        b_hi = b31row.astype(BF16).astype(F32)
        erow = lax.broadcasted_iota(I32, (8, width), 0)
        extra = jnp.where(erow == 0, b_hi, jnp.where(erow == 1, b31row - b_hi, 0.0))
        amask_ref[g, 0:noct] = jnp.concatenate(
            [a_all, jnp.broadcast_to(extra[None], (noct, 8, width))], axis=1).astype(BF16)
        far_last = jnp.maximum(nfar - 1, 0)
        orow = lax.broadcasted_iota(I32, (MASK_ROWS, width), 0)
        keep = (orow >= BLOCKS_PER_CHUNK) | (orow + BLOCKS_PER_CHUNK * far_last < zb)
        amask_ref[g, noct] = jnp.where(keep, amask_ref[g, far_last].astype(F32), -MASK_BIG).astype(BF16)
        for q_ref in (qs_ref,) + qw_refs:
            q_ref[g, 0:HEAD_DIM, :] = qg
            q_ref[g, HEAD_DIM + MASK_ROWS:LANES, :] = jnp.zeros((LANES - HEAD_DIM - MASK_ROWS, width), BF16)

        halves = []
        for h in range(2):
            octet = amask_ref[g, (zb + 2 * h) // BLOCKS_PER_CHUNK].astype(F32)
            qz_ref[2 * g + h, HEAD_DIM:HEAD_DIM + MASK_ROWS, :] = jnp.where(
                orow < BLOCKS_PER_CHUNK, octet, 0.0).astype(BF16)
            qz_ref[2 * g + h, 0:HEAD_DIM, :] = qg
            qz_ref[2 * g + h, HEAD_DIM + MASK_ROWS:LANES, :] = jnp.zeros((LANES - HEAD_DIM - MASK_ROWS, width), BF16)
            sh = jnp.dot(ks_ref[g, pl.ds(zs + h * Q_BLOCK, Q_BLOCK), :], qz_ref[2 * g + h], preferred_element_type=F32)
            halves.append(sh + ts_ref[g, pl.ds(ts_off + h * Q_BLOCK, Q_BLOCK), :])
        sz = jnp.concatenate(halves, axis=0)
        mz = jnp.max(sz, axis=0, keepdims=True)
        pz = jnp.exp2(sz - mz)
        m_init.append(mz)
        acc_init.append(jnp.dot(vsT_ref[g, :, pl.ds(zs, 2 * Q_BLOCK)], pz.astype(BF16), preferred_element_type=F32))
        gates.append([jnp.concatenate([gT_ref[br * N_HEADS + GQA * g + r:br * N_HEADS + GQA * g + r + 1, :]
                                       for r in range(GQA)], axis=1) for br in range(N_BRANCH)])

    nsub = 1
    sub = CHUNK // nsub

    def score(g, u, buf_ref, h, q_ref):
        k0 = pl.multiple_of(jnp.minimum(u, noct - 1) * CHUNK + h * sub, sub)
        s = jnp.dot(ks_ref[g, pl.ds(k0, sub), :], q_ref[g], preferred_element_type=F32)
        buf_ref[g, h * sub:(h + 1) * sub, :] = s
        return jnp.max(s, axis=0, keepdims=True)

    def set_mask_rows(g, u, q_ref):
        oct_id = jnp.where(u < nfar - 1, u, noct)
        q_ref[g, HEAD_DIM:HEAD_DIM + MASK_ROWS, :] = amask_ref[g, oct_id]

    def accumulate(g, u, buf_ref, h, mn, acc):
        k0 = pl.multiple_of(jnp.minimum(u, noct - 1) * CHUNK + h * sub, sub)
        p = jnp.exp2(buf_ref[g, h * sub:(h + 1) * sub, :] - mn)
        return acc + jnp.dot(vsT_ref[g, :, pl.ds(k0, sub)], p.astype(BF16), preferred_element_type=F32)

    def step(u, cur_ref, nxt_ref, q_ref, state, cms):
        out_state, out_cms = [], []
        for g in range(N_KV):
            m, acc = state[2 * g], state[2 * g + 1]
            mn = jnp.maximum(m, cms[g])
            acc = jnp.exp2(m - mn) * acc
            set_mask_rows(g, u + 1, q_ref)
            cm = None
            for h in range(nsub):
                ch = score(g, u + 1, nxt_ref, h, q_ref)
                cm = ch if cm is None else jnp.maximum(cm, ch)
                acc = accumulate(g, u, cur_ref, h, mn, acc)
            out_cms.append(cm)
            out_state += [mn, acc]
        return out_state, out_cms

    bufs = (sa_ref, sb_ref)

    def far(i, carry):
        state, cms = list(carry[0:4]), list(carry[4:6])
        for k in range(FAR_UNROLL):
            state, cms = step(FAR_UNROLL * i + k, bufs[k % 2], bufs[(k + 1) % 2], qw_refs[k], state, cms)
        return tuple(state + cms)

    first = []
    for g in range(N_KV):
        set_mask_rows(g, 0, qs_ref)
        cm = None
        for h in range(nsub):
            ch = score(g, 0, sa_ref, h, qs_ref)
            cm = ch if cm is None else jnp.maximum(cm, ch)
        first.append(cm)
    nchunks = jnp.maximum(nfar, 1)
    nloops = (nchunks - 1) // FAR_UNROLL
    carry = lax.fori_loop(0, nloops, far, (m_init[0], acc_init[0], m_init[1], acc_init[1], first[0], first[1]))
    u_last = FAR_UNROLL * nloops

    def finish(left):
        for g in range(N_KV):
            m, acc, cm = carry[2 * g], carry[2 * g + 1], carry[4 + g]
            for k in range(left):
                mn = jnp.maximum(m, cm)
                acc = jnp.exp2(m - mn) * acc
                if k + 1 < left:
                    set_mask_rows(g, u_last + k + 1, qw_refs[k])
                    cm = score(g, u_last + k + 1, bufs[(k + 1) % 2], 0, qw_refs[k])
                acc = accumulate(g, u_last + k, bufs[k % 2], 0, mn, acc)
                m = mn
            fin_ref[g] = acc

    for left in range(1, FAR_UNROLL + 1):
        pl.when(nchunks - u_last == left)(functools.partial(finish, left))
    carry = [None, fin_ref[0], None, fin_ref[1]]

    o_all, ssq = [], jnp.zeros((1, LANES), F32)
    for g in range(N_KV):
        acc = carry[2 * g + 1]
        o_sel = acc[0:HEAD_DIM] * (1.0 / jnp.maximum(acc[HEAD_DIM:HEAD_DIM + 1], 1e-30))
        o = gates[g][0] * o_cmp[g] + gates[g][1] * o_sel + gates[g][2] * o_win[g]
        o_all.append(o)
        cs = jnp.sum(o * o, axis=0, keepdims=True)
        ssq = ssq + ((cs[:, 0:LANES] + cs[:, LANES:2 * LANES]) + (cs[:, 2 * LANES:3 * LANES] + cs[:, 3 * LANES:]))
    inv = _tile4(lax.rsqrt(ssq * (1.0 / ATTN_W) + EPS))
    for g in range(N_KV):
        on = o_all[g] * inv * gain_ref[g]
        for k in range(GQA // 2):
            pair = jnp.concatenate([on[:, (2 * k) * LANES:(2 * k + 1) * LANES],
                                    on[:, (2 * k + 1) * LANES:(2 * k + 2) * LANES]], axis=0)
            col = (GQA * g + 2 * k) * HEAD_DIM
            out_ref[:, col:col + 2 * HEAD_DIM] = pair.T.astype(BF16)


def _attention(qT, gT, ks, vsT, kw, vwT, kc, vcT, tables, gain_b):
    S = qT.shape[1]
    nq = S // Q_BLOCK
    ncr = kc.shape[1]
    nsel = S // SEL_LEN
    noct = S // CHUNK
    width = GQA * LANES
    tw, ts, tc, b31 = tables
    vmem = pl.BlockSpec(memory_space=pltpu.VMEM)
    in_specs = [
        pl.BlockSpec((ATTN_W, Q_BLOCK), lambda c: (0, c)),
        pl.BlockSpec((N_BRANCH * N_HEADS, Q_BLOCK), lambda c: (0, c)),
    ] + [vmem] * 11
    return pl.pallas_call(
        _attn_kernel,
        grid=(nq,),
        in_specs=in_specs,
        out_specs=pl.BlockSpec((Q_BLOCK, ATTN_W), lambda c: (c, 0)),
        out_shape=jax.ShapeDtypeStruct((S, ATTN_W), BF16),
        scratch_shapes=[
            pltpu.VMEM((N_KV, ncr, width), F32),
            pltpu.VMEM((N_KV, WINDOW + Q_BLOCK, width), F32),
            pltpu.VMEM((N_KV, ncr + 16, LANES), F32),
            pltpu.VMEM((N_KV, noct + 1, MASK_ROWS, width), BF16),
            pltpu.VMEM((N_KV, LANES, width), BF16),
            pltpu.VMEM((N_KV, CHUNK, width), F32),
            pltpu.VMEM((N_KV, CHUNK, width), F32),
            pltpu.VMEM((2 * N_KV, LANES, width), BF16),
            pltpu.VMEM((FAR_UNROLL, N_KV, LANES, width), BF16),
            pltpu.VMEM((N_KV, V_ROWS, width), F32),
            pltpu.VMEM((N_KV, HEAD_DIM, width), F32),
            pltpu.VMEM((N_KV, nsel, LANES), F32),
        ],
        compiler_params=pltpu.CompilerParams(dimension_semantics=("arbitrary",), vmem_limit_bytes=VMEM_LIMIT),
        name="nsa_attention",
    )(qT, gT, ks, vsT, kw, vwT, kc, vcT, tw, ts, tc, b31, gain_b)


def _mix_and_norm(x_ref, attn_ref, conv_ref, wout_ref, g2_ref, rows=slice(None)):
    x1 = x_ref[rows, :] + jnp.dot(attn_ref[rows, :], wout_ref[0:ATTN_W, :], preferred_element_type=F32) \
        + jnp.dot(conv_ref[rows, :], wout_ref[ATTN_W:, :], preferred_element_type=F32)
    ms = jnp.mean(x1 * x1, axis=-1, keepdims=True)
    h2 = x1 * lax.rsqrt(ms + EPS) * g2_ref[...]
    return x1, h2


def _ffn_kernel(x_ref, attn_ref, conv_ref, wout_ref, g2_ref, wg_ref, wu_ref, wd_ref, out_ref):
    x1, h2 = _mix_and_norm(x_ref, attn_ref, conv_ref, wout_ref, g2_ref)
    h2 = h2.astype(BF16)
    a = jnp.dot(h2, wg_ref[...], preferred_element_type=F32)
    u = jnp.dot(h2, wu_ref[...], preferred_element_type=F32)
    y = (a * jax.nn.sigmoid(a) * u).astype(BF16)
    out_ref[...] = x1 + jnp.dot(y, wd_ref[...], preferred_element_type=F32)


_FFN_TM = 512


def _outproj_ffn(x2, attn_n, conv_n, wout, g2, wg, wu, wd):
    S = x2.shape[0]
    tm = _FFN_TM
    resident = pl.BlockSpec(memory_space=pltpu.VMEM)
    return pl.pallas_call(
        _ffn_kernel,
        grid=(S // tm,),
        in_specs=[
            pl.BlockSpec((tm, D_MODEL), lambda i: (i, 0)),
            pl.BlockSpec((tm, ATTN_W), lambda i: (i, 0)),
            pl.BlockSpec((tm, CONV_W), lambda i: (i, 0)),
            resident, resident, resident, resident, resident,
        ],
        out_specs=pl.BlockSpec((tm, D_MODEL), lambda i: (i, 0)),
        out_shape=jax.ShapeDtypeStruct((S, D_MODEL), F32),
        compiler_params=pltpu.CompilerParams(dimension_semantics=("arbitrary",), vmem_limit_bytes=VMEM_LIMIT),
        name="outproj_ffn",
    )(x2, attn_n, conv_n, wout, g2, wg, wu, wd)


TOKEN_TILE = (D_MODEL // LANES, LANES)


def _to_token_tiles(ref, rows):
    x = jnp.stack([rows[:, k * LANES:(k + 1) * LANES] for k in range(TOKEN_TILE[0])], axis=0)
    ref[...] = pltpu.einshape('ktl->tkl', x)


def _from_token_tiles(ref):
    x = pltpu.einshape('tkl->ktl', ref[...])
    return jnp.concatenate([x[k] for k in range(TOKEN_TILE[0])], axis=1)


_ROUTER_SPLIT = 2


def _router_kernel(x_ref, attn_ref, conv_ref, wout_ref, g2_ref, rw_ref, rb_ref, tri_ref,
                   x1_ref, h2_ref, route_ref, cnt_ref, run_ref):
    tm = _FFN_TM
    i = pl.program_id(0)

    @pl.when(i == 0)
    def _():
        run_ref[...] = jnp.zeros(run_ref.shape, F32)

    running = run_ref[0:1, :]
    for part in range(_ROUTER_SPLIT):
        rows = slice(part * tm, (part + 1) * tm)
        x1, h2 = _mix_and_norm(x_ref, attn_ref, conv_ref, wout_ref, g2_ref, rows)
        x1_ref[rows, :] = x1
        h2b = h2.astype(BF16)
        _to_token_tiles(h2_ref.at[rows], h2b.astype(F32))
        lane = lax.broadcasted_iota(I32, (tm, LANES), 1)
        logits = jnp.dot(h2b, rw_ref[...], preferred_element_type=F32) + rb_ref[...]
        logits = jnp.where(lane < N_EXPERTS, logits, -jnp.inf)
        m1 = jnp.max(logits, axis=-1, keepdims=True)
        i1 = jnp.min(jnp.where(logits == m1, lane, LANES), axis=-1, keepdims=True)
        rest = jnp.where(lane == i1, -jnp.inf, logits)
        m2 = jnp.max(rest, axis=-1, keepdims=True)
        i2 = jnp.min(jnp.where(rest == m2, lane, LANES), axis=-1, keepdims=True)
        e2 = jnp.exp(m2 - m1)
        den = 1.0 + e2
        oh1 = (lane == i1).astype(F32)
        oh2 = (lane == i2).astype(F32)
        both = oh1 + oh2
        before = running + jnp.dot(tri_ref[...], both.astype(BF16), preferred_element_type=F32)
        rank1 = jnp.sum(before * oh1, axis=-1, keepdims=True)
        rank2 = jnp.sum(before * oh2, axis=-1, keepdims=True)
        fields = (i1.astype(F32), i2.astype(F32), rank1, rank2, 1.0 / den, e2 / den)
        route = jnp.zeros((tm, LANES), F32)
        for k, v in enumerate(fields):
            route = jnp.where(lane == k, v, route)
        route_ref[rows, :] = route
        running = running + jnp.sum(both, axis=0, keepdims=True)
    run_ref[...] = jnp.broadcast_to(running, run_ref.shape)
    cnt_ref[...] = run_ref[...]


def _outproj_router(x2, attn_n, conv_n, wout, g2, rw, rb):
    S = x2.shape[0]
    tm = _FFN_TM
    tb = _ROUTER_SPLIT * tm
    tri = np.tril(np.ones((tm, tm), np.float32), -1)
    const = lambda shape: pl.BlockSpec(shape, lambda i: (0,) * len(shape))
    return pl.pallas_call(
        _router_kernel,
        grid=(S // tb,),
        in_specs=[
            pl.BlockSpec((tb, D_MODEL), lambda i: (i, 0)),
            pl.BlockSpec((tb, ATTN_W), lambda i: (i, 0)),
            pl.BlockSpec((tb, CONV_W), lambda i: (i, 0)),
            const(wout.shape), const((1, D_MODEL)), const(rw.shape), const((1, LANES)), const((tm, tm)),
        ],
        out_specs=(pl.BlockSpec((tb, D_MODEL), lambda i: (i, 0)),
                   pl.BlockSpec((tb,) + TOKEN_TILE, lambda i: (i, 0, 0)),
                   pl.BlockSpec((tb, LANES), lambda i: (i, 0)),
                   const((8, LANES))),
        out_shape=(jax.ShapeDtypeStruct((S, D_MODEL), F32),
                   jax.ShapeDtypeStruct((S,) + TOKEN_TILE, F32),
                   jax.ShapeDtypeStruct((S, LANES), F32),
                   jax.ShapeDtypeStruct((8, LANES), F32)),
        scratch_shapes=[pltpu.VMEM((8, LANES), F32)],
        compiler_params=pltpu.CompilerParams(dimension_semantics=("arbitrary",), vmem_limit_bytes=VMEM_LIMIT),
        name="outproj_router",
    )(x2, attn_n, conv_n, wout, g2, rw, rb, jnp.asarray(tri, BF16))


_ROW_TM = 256
_EXP_TM = 512
_DMA_UNROLL = 8


def _row_copy(src_ref, src_row, dst_ref, dst_row, sem):
    return pltpu.make_async_copy(src_ref.at[src_row], dst_ref.at[dst_row], sem)


def _dispatch_kernel(pos_ref, ztile_ref, h_ref, xs_ref, hbuf_ref, zbuf_ref, sem, zsem):
    tm = h_ref.shape[0]
    i = pl.program_id(0)
    slot = i % 2

    @pl.when(i == 0)
    def _():
        zbuf_ref[...] = jnp.zeros(zbuf_ref.shape, F32)
        for z in range(ztile_ref.shape[0]):
            clear = pltpu.make_async_copy(zbuf_ref, xs_ref.at[pl.ds(ztile_ref[z] * _EXP_TM, _EXP_TM)], zsem)
            clear.start()
            clear.wait()

    hbuf_ref[slot] = h_ref[...]

    def issue(t, carry):
        for k in range(2):
            _row_copy(hbuf_ref.at[slot], t, xs_ref, pos_ref[0, 0, 2 * t + k], sem.at[slot]).start(priority=k)
        return carry

    lax.fori_loop(0, tm, issue, 0, unroll=_DMA_UNROLL)

    def drain(which):
        def body(t, carry):
            for k in range(2):
                _row_copy(hbuf_ref.at[which], 0, xs_ref, 0, sem.at[which]).wait()
            return carry
        lax.fori_loop(0, tm, body, 0, unroll=_DMA_UNROLL)

    @pl.when(i > 0)
    def _():
        drain(1 - slot)

    @pl.when(i == pl.num_programs(0) - 1)
    def _():
        drain(slot)


def _dispatch(pos3, pad_tiles, h2, n_rows):
    S = h2.shape[0]
    tm = _ROW_TM
    return pl.pallas_call(
        _dispatch_kernel,
        grid=(S // tm,),
        in_specs=[
            pl.BlockSpec((1, 1, 2 * tm), lambda i: (i, 0, 0), memory_space=pltpu.SMEM),
            pl.BlockSpec(memory_space=pltpu.SMEM),
            pl.BlockSpec((tm,) + TOKEN_TILE, lambda i: (i, 0, 0)),
        ],
        out_specs=pl.BlockSpec(memory_space=pl.ANY),
        out_shape=jax.ShapeDtypeStruct((n_rows,) + TOKEN_TILE, F32),
        scratch_shapes=[pltpu.VMEM((2, tm) + TOKEN_TILE, F32), pltpu.VMEM((_EXP_TM,) + TOKEN_TILE, F32),
                        pltpu.SemaphoreType.DMA((2,)), pltpu.SemaphoreType.DMA],
        compiler_params=pltpu.CompilerParams(dimension_semantics=("arbitrary",), vmem_limit_bytes=VMEM_LIMIT),
        name="moe_dispatch",
    )(pos3, pad_tiles, h2)


def _experts_kernel(te_ref, tb_ref, nt_ref, xs_ref, wg_ref, wu_ref, wd_ref, ys_ref):
    i = pl.program_id(0)

    @pl.when(i < nt_ref[0])
    def _():
        x = _from_token_tiles(xs_ref).astype(BF16)
        a = jnp.dot(x, wg_ref[0], preferred_element_type=F32)
        u = jnp.dot(x, wu_ref[0], preferred_element_type=F32)
        y = (a * jax.nn.sigmoid(a) * u).astype(BF16)
        _to_token_tiles(ys_ref, jnp.dot(y, wd_ref[0], preferred_element_type=F32))

    @pl.when(i >= nt_ref[0])
    def _():
        ys_ref[...] = jnp.zeros(ys_ref.shape, F32)


def _experts(tile_e, tile_b, n_tiles, xs, wg, wu, wd):
    n_rows = xs.shape[0]
    tm = _EXP_TM
    weights = lambda shape: pl.BlockSpec(shape, lambda i, te, tb, nt: (te[i], 0, 0), pipeline_mode=pl.Buffered(1))
    grid_spec = pltpu.PrefetchScalarGridSpec(
        num_scalar_prefetch=3,
        grid=(n_rows // tm,),
        in_specs=[
            pl.BlockSpec((tm,) + TOKEN_TILE, lambda i, te, tb, nt: (tb[i], 0, 0)),
            weights((1, D_MODEL, D_FF)), weights((1, D_MODEL, D_FF)), weights((1, D_FF, D_MODEL)),
        ],
        out_specs=pl.BlockSpec((tm,) + TOKEN_TILE, lambda i, te, tb, nt: (tb[i], 0, 0)),
    )
    return pl.pallas_call(
        _experts_kernel,
        grid_spec=grid_spec,
        out_shape=jax.ShapeDtypeStruct((n_rows,) + TOKEN_TILE, F32),
        compiler_params=pltpu.CompilerParams(dimension_semantics=("arbitrary",), vmem_limit_bytes=VMEM_LIMIT),
        name="moe_experts",
    )(tile_e, tile_b, n_tiles, xs, wg, wu, wd)


def _combine_kernel(pos_ref, nxt_ref, x1_ref, route_ref, ys_ref, out_ref, y1_ref, y2_ref, sem):
    tm = x1_ref.shape[0]
    i = pl.program_id(0)
    slot = i % 2

    def issue(src_pos_ref, which):
        def body(t, carry):
            _row_copy(ys_ref, src_pos_ref[0, 0, 2 * t], y1_ref.at[which], t, sem.at[which]).start(priority=0)
            _row_copy(ys_ref, src_pos_ref[0, 0, 2 * t + 1], y2_ref.at[which], t, sem.at[which]).start(priority=1)
            return carry
        lax.fori_loop(0, tm, body, 0, unroll=_DMA_UNROLL)

    @pl.when(i == 0)
    def _():
        issue(pos_ref, slot)

    @pl.when(i + 1 < pl.num_programs(0))
    def _():
        issue(nxt_ref, 1 - slot)

    def drain(t, carry):
        _row_copy(ys_ref, 0, y1_ref.at[slot], 0, sem.at[slot]).wait()
        _row_copy(ys_ref, 0, y2_ref.at[slot], 0, sem.at[slot]).wait()
        return carry

    lax.fori_loop(0, tm, drain, 0, unroll=_DMA_UNROLL)
    lane = lax.broadcasted_iota(I32, (tm, LANES), 1)
    route = route_ref[...]
    w1 = jnp.sum(jnp.where(lane == 4, route, 0.0), axis=-1, keepdims=True)
    w2 = jnp.sum(jnp.where(lane == 5, route, 0.0), axis=-1, keepdims=True)
    out_ref[...] = x1_ref[...] + (_from_token_tiles(y1_ref.at[slot]) * w1 + _from_token_tiles(y2_ref.at[slot]) * w2)


def _combine(pos3, x1, route, ys):
    S = x1.shape[0]
    tm = _ROW_TM
    last = S // tm - 1
    return pl.pallas_call(
        _combine_kernel,
        grid=(S // tm,),
        in_specs=[
            pl.BlockSpec((1, 1, 2 * tm), lambda i: (i, 0, 0), memory_space=pltpu.SMEM),
            pl.BlockSpec((1, 1, 2 * tm), lambda i: (jnp.minimum(i + 1, last), 0, 0), memory_space=pltpu.SMEM),
            pl.BlockSpec((tm, D_MODEL), lambda i: (i, 0)),
            pl.BlockSpec((tm, LANES), lambda i: (i, 0)),
            pl.BlockSpec(memory_space=pl.ANY),
        ],
        out_specs=pl.BlockSpec((tm, D_MODEL), lambda i: (i, 0)),
        out_shape=jax.ShapeDtypeStruct((S, D_MODEL), F32),
        scratch_shapes=[pltpu.VMEM((2, tm) + TOKEN_TILE, F32), pltpu.VMEM((2, tm) + TOKEN_TILE, F32),
                        pltpu.SemaphoreType.DMA((2,))],
        compiler_params=pltpu.CompilerParams(dimension_semantics=("arbitrary",), vmem_limit_bytes=VMEM_LIMIT),
        name="moe_combine",
    )(pos3, pos3, x1, route, ys)


def _moe(h2, x1, route, counts, wg, wu, wd):
    S = h2.shape[0]
    tm = _EXP_TM
    n_tiles_max = 2 * S // tm + N_EXPERTS
    cnt = counts[0, 0:N_EXPERTS].astype(I32)
    tiles = (cnt + tm - 1) // tm
    first = jnp.cumsum(tiles) - tiles
    n_tiles = jnp.sum(tiles)
    idx = jnp.arange(n_tiles_max, dtype=I32)
    last = jnp.minimum(idx, n_tiles - 1)
    tile_e = (jnp.sum(last[:, None] >= first[None, :], axis=1) - 1).astype(I32)
    eid = route[:, 0:2].astype(I32)
    pos = first[eid] * tm + route[:, 2:4].astype(I32)
    pos3 = pos.reshape(S // _ROW_TM, 1, 2 * _ROW_TM)
    pad_tiles = jnp.concatenate([jnp.maximum(first + tiles - 1, 0),
                                 jnp.minimum(n_tiles + jnp.arange(N_EXPERTS, dtype=I32), n_tiles_max - 1)])
    xs = _dispatch(pos3, pad_tiles.astype(I32), h2, n_tiles_max * tm)
    ys = _experts(tile_e, idx, n_tiles.reshape(1), xs, wg, wu, wd)
    return _combine(pos3, x1, route, ys)


def _split_w_in(w):
    o = np.cumsum([0, ATTN_W] + [KV_W] * 6 + [N_BRANCH * N_HEADS] + [CONV_W] * 3)
    q, kc, vc, ksl, vsl, kwn, vwn, gts, cb, cc, ch = (w[:, o[i]:o[i + 1]] for i in range(11))
    perm = np.array([h * N_BRANCH + br for br in range(N_BRANCH) for h in range(N_HEADS)])
    wtok = jnp.concatenate([kc, vc, ksl, kwn, cb, cc, ch], axis=1).astype(BF16)
    feat = jnp.concatenate([q, vsl, vwn, gts[:, perm], jnp.zeros((D_MODEL, 8), w.dtype)], axis=1)
    return wtok, feat.T.astype(BF16)


def _expand_cmp_w1(w1k, w1v):
    kinds = jnp.stack([w1k, w1v]).reshape(2, CMP_LEN, HEAD_DIM, CMP_HIDDEN)
    per_col = jnp.repeat(kinds, N_KV, axis=0)
    eye = jnp.eye(2 * N_KV, dtype=F32)
    out = []
    for l0 in (0, CMP_STRIDE):
        w = per_col[:, l0:l0 + CMP_STRIDE].transpose(1, 0, 2, 3)
        blk = w[:, :, :, None, :] * eye[None, :, None, :, None]
        out.append(blk.reshape(CMP_STRIDE * 2 * KV_W, 2 * N_KV * CMP_HIDDEN).astype(BF16))
    return out


def kernel(x, rel_bias, norm1, w_in, q_norm, k_norm, cmp_pos_k, cmp_pos_v, cmp_k_w1, cmp_k_w2, cmp_v_w1, cmp_v_w2,
           conv_w, attn_out_norm, conv_out_norm, w_out, norm2, ffn_w_gate, ffn_w_up, ffn_w_down, router_w, router_b,
           moe_w_gate, moe_w_up, moe_w_down):
    B, S, _ = x.shape
    assert B == 1 and S % CHUNK == 0 and S >= WINDOW + Q_BLOCK
    depth = norm1.shape[0]
    x2 = x.reshape(S, D_MODEL)
    tables = _bias_tables(rel_bias)
    for layer in range(depth):
        wtok, wfeat = _split_w_in(w_in[layer])
        kgain = jnp.concatenate([jnp.tile(k_norm[layer, 1], N_KV), jnp.tile(k_norm[layer, 2], N_KV)])[None, :]
        gq = jnp.broadcast_to(q_norm[layer][:, None], (HEAD_DIM, _IN_TM))
        convw = jnp.pad(conv_w[layer], ((0, 8 - CONV_K), (0, 0)))
        qT, gT, vsT, vwT, ks, kw, kcv, conv_n = _in_proj(
            x2, norm1[layer][None, :], wtok, wfeat, kgain, gq, convw, conv_out_norm[layer][None, :])

        wtop, wbot = _expand_cmp_w1(cmp_k_w1[layer], cmp_v_w1[layer])
        posk = jnp.broadcast_to(cmp_pos_k[layer].reshape(1, -1), (8, CMP_LEN * HEAD_DIM))
        posv = jnp.broadcast_to(cmp_pos_v[layer].reshape(1, -1), (8, CMP_LEN * HEAD_DIM))
        w2k = jnp.pad(cmp_k_w2[layer], ((0, 0), (0, LANES - HEAD_DIM))).astype(BF16)
        w2vT = cmp_v_w2[layer].T.astype(BF16)
        kcgain = jnp.pad(k_norm[layer, 0], (0, LANES - HEAD_DIM))[None, :]
        kc, vcT = _compress(kcv, wtop, wbot, posk, posv, cmp_k_w1[layer], cmp_v_w1[layer], w2k, w2vT, kcgain)

        gain_b = jnp.broadcast_to(attn_out_norm[layer].reshape(N_KV, GQA, HEAD_DIM).transpose(0, 2, 1)[:, :, :, None],
                                  (N_KV, HEAD_DIM, GQA, LANES)).reshape(N_KV, HEAD_DIM, GQA * LANES)
        attn_n = _attention(qT, gT, ks, vsT, kw, vwT, kc, vcT, tables, gain_b)

        wout = w_out[layer].astype(BF16)
        g2 = norm2[layer][None, :]
        i = layer // 2
        if layer % 2 == 0:
            x2 = _outproj_ffn(x2, attn_n, conv_n, wout, g2, ffn_w_gate[i].astype(BF16), ffn_w_up[i].astype(BF16),
                              ffn_w_down[i].astype(BF16))
        else:
            rw = jnp.pad(router_w[i], ((0, 0), (0, LANES - N_EXPERTS))).astype(BF16)
            rb = jnp.pad(router_b[i], (0, LANES - N_EXPERTS))[None, :]
            x1, h2, route, counts = _outproj_router(x2, attn_n, conv_n, wout, g2, rw, rb)
            x2 = _moe(h2, x1, route, counts, moe_w_gate[i].astype(BF16), moe_w_up[i].astype(BF16),
                      moe_w_down[i].astype(BF16))
    return x2.reshape(B, S, D_MODEL)
```

```python
import functools
import math

import numpy as np
import jax
import jax.numpy as jnp
from jax import lax
from jax.experimental import pallas as pl
from jax.experimental.pallas import tpu as pltpu

F32 = jnp.float32
BF16 = jnp.bfloat16
I32 = jnp.int32

D_MODEL = 1024
HEAD_DIM = 64
N_HEADS = 8
N_KV = 2
GQA = N_HEADS // N_KV
ATTN_W = N_HEADS * HEAD_DIM
KV_W = N_KV * HEAD_DIM
N_BRANCH = 3
CONV_W = 512
CONV_K = 3
CMP_LEN = 32
CMP_STRIDE = 16
CMP_HIDDEN = 128
SEL_LEN = 64
SEL_TOPK = 16
WINDOW = 512
Q_BLOCK = 128
N_BUCKETS = 32
MAX_DISTANCE = 128
D_FF = 2816
N_EXPERTS = 8
EPS = 1e-6
NEG = -1e30
MASK_BIG = 2.0 ** 60
MASK_ROWS = 16
SCALE = HEAD_DIM ** -0.5
LOG2E = math.log2(math.e)

LANES = 128
V_ROWS = 80
CHUNK = 512
BLOCKS_PER_CHUNK = CHUNK // SEL_LEN
TC_LEAD = 24
TC_ROWS = 88
TC_WIN = 48
CMP_STEP = 128
FAR_UNROLL = 6
VMEM_LIMIT = 56 * 1024 * 1024


def _bucket_np(dist):
    n = np.maximum(dist, 0)
    max_exact = N_BUCKETS // 2
    nf = np.maximum(n, max_exact).astype(np.float64)
    v = np.log(nf / max_exact) / math.log(MAX_DISTANCE / max_exact) * (N_BUCKETS - max_exact)
    frac = np.abs(v - np.round(v))
    assert np.all((frac > 1e-6) | (n <= max_exact) | (n >= MAX_DISTANCE)), "bucket boundary is precision dependent"
    large = np.minimum(max_exact + (v + 1e-9).astype(np.int32), N_BUCKETS - 1)
    return np.where(n < max_exact, n, large).astype(np.int32)


def _index_tables():
    tl = np.arange(Q_BLOCK)[None, :]
    r = np.arange(WINDOW + Q_BLOCK + WINDOW)[:, None]
    d = tl + WINDOW - r
    idx_w = np.where((d >= 0) & (d < WINDOW), _bucket_np(d), -1)
    r = np.arange(3 * Q_BLOCK)[:, None]
    d = tl + Q_BLOCK - r
    idx_s = np.where(d >= 0, _bucket_np(d), -1)
    r = np.arange(TC_ROWS)[:, None] - TC_LEAD
    d = tl - CMP_STRIDE * r + (CMP_STRIDE * 16 - (CMP_LEN - 1))
    idx_c = np.where((d >= 0) & (r < 32), _bucket_np(d), -1)
    return idx_w.astype(np.int32), idx_s.astype(np.int32), idx_c.astype(np.int32)


def _tables_kernel(rb_ref, iw_ref, is_ref, ic_ref, tw_ref, ts_ref, tc_ref, b31_ref):
    head_lanes = [(h // GQA, slice((h % GQA) * LANES, (h % GQA + 1) * LANES)) for h in range(N_HEADS)]
    for idx_ref, out_ref in ((iw_ref, tw_ref), (is_ref, ts_ref), (ic_ref, tc_ref)):
        out_ref[...] = jnp.full(out_ref.shape, NEG, F32)

        def body(b, carry, idx_ref=idx_ref, out_ref=out_ref):
            hit = idx_ref[...] == b
            for h, (g, lanes) in enumerate(head_lanes):
                out_ref[g, :, lanes] = jnp.where(hit, rb_ref[b, h] * LOG2E, out_ref[g, :, lanes])
            return carry

        lax.fori_loop(0, N_BUCKETS, body, 0)
    for h, (g, lanes) in enumerate(head_lanes):
        far_bias = rb_ref[N_BUCKETS - 1, h] * LOG2E
        tc_ref[g, :, lanes] = jnp.where(ic_ref[...] >= 0, tc_ref[g, :, lanes] - far_bias, NEG)
        b31_ref[g, :, lanes] = jnp.full((8, LANES), far_bias, F32)


def _bias_tables(rel_bias):
    idx_w, idx_s, idx_c = _index_tables()
    width = GQA * LANES
    out_shape = (
        jax.ShapeDtypeStruct((N_KV, idx_w.shape[0], width), F32),
        jax.ShapeDtypeStruct((N_KV, idx_s.shape[0], width), F32),
        jax.ShapeDtypeStruct((N_KV, idx_c.shape[0], width), F32),
        jax.ShapeDtypeStruct((N_KV, 8, width), F32),
    )
    vmem = pl.BlockSpec(memory_space=pltpu.VMEM)
    return pl.pallas_call(
        _tables_kernel,
        out_shape=out_shape,
        in_specs=[pl.BlockSpec(memory_space=pltpu.SMEM), vmem, vmem, vmem],
        out_specs=(vmem, vmem, vmem, vmem),
        name="bias_tables",
    )(rel_bias, jnp.asarray(idx_w), jnp.asarray(idx_s), jnp.asarray(idx_c))


def _in_proj_kernel(x_ref, g1_ref, wtok_ref, wfeat_ref, ind_ref, kgain_ref, aug_ref, gq_ref, convw_ref, cgain_ref,
                    qT_ref, gT_ref, vsT_ref, vwT_ref, ks_ref, kw_ref, kcv_ref, convn_ref, zs_ref, carry_ref):
    tm = _IN_TM
    i = pl.program_id(0)

    @pl.when(i == 0)
    def _():
        carry_ref[...] = jnp.zeros(carry_ref.shape, F32)

    prev_tail = carry_ref[...]
    for part in range(_IN_SPLIT):
        rows = slice(part * tm, (part + 1) * tm)
        x = x_ref[rows, :]
        ms = jnp.mean(x * x, axis=-1, keepdims=True)
        h = (x * lax.rsqrt(ms + EPS) * g1_ref[...]).astype(BF16)
        tok = jnp.dot(h, wtok_ref[...], preferred_element_type=F32)
        feat = lax.dot_general(wfeat_ref[...], h, (((1,), (1,)), ((), ())),
                               preferred_element_type=F32)

        kcv_ref[rows, :] = tok[:, 0:2 * KV_W].astype(BF16)
        kk = tok[:, 2 * KV_W:4 * KV_W]
        sq = kk * kk
        sq_hi = sq.astype(BF16)
        sq_lo = (sq - sq_hi.astype(F32)).astype(BF16)
        ssq = jnp.dot(jnp.concatenate([sq_hi, sq_lo], axis=1), ind_ref[...],
                      preferred_element_type=F32)
        kn = kk * lax.rsqrt(ssq * (1.0 / HEAD_DIM) + EPS) * kgain_ref[...]
        ksl = kn[:, 0:KV_W]
        lane = lax.broadcasted_iota(I32, (tm, LANES), 1)
        aug = aug_ref[...]
        ks_ref[0, rows, :] = jnp.where(lane < HEAD_DIM, ksl, aug).astype(BF16)
        ks_ref[1, rows, :] = jnp.where(lane < HEAD_DIM, pltpu.roll(ksl, HEAD_DIM, 1), aug).astype(BF16)
        kw_ref[rows, :] = kn[:, KV_W:2 * KV_W].astype(BF16)

        c0 = 4 * KV_W
        cb = tok[:, c0:c0 + CONV_W]
        cc = tok[:, c0 + CONV_W:c0 + 2 * CONV_W]
        ch = tok[:, c0 + 2 * CONV_W:c0 + 3 * CONV_W]
        z = cc * ch
        zs_ref[part, 0:8, :] = prev_tail
        zs_ref[part, 8:8 + tm, :] = z
        z1 = zs_ref[part, 7:7 + tm, :]
        z2 = zs_ref[part, 6:6 + tm, :]
        w = convw_ref[...]
        y = w[0:1, :] * z2 + w[1:2, :] * z1 + w[2:3, :] * z
        prev_tail = z[tm - 8:tm, :]
        oc = cb * y
        msc = jnp.mean(oc * oc, axis=-1, keepdims=True)
        convn_ref[rows, :] = (oc * lax.rsqrt(msc + EPS) * cgain_ref[...]).astype(BF16)

        q = feat[0:ATTN_W].reshape(N_HEADS, HEAD_DIM, tm)
        qss = jnp.sum(q * q, axis=1, keepdims=True)
        qn = q * lax.rsqrt(qss * (1.0 / HEAD_DIM) + EPS) * gq_ref[...][None]
        qT_ref[:, rows] = (qn * (SCALE * LOG2E)).reshape(ATTN_W, tm).astype(BF16)
        ones_rows = (lax.broadcasted_iota(I32, (V_ROWS - HEAD_DIM, tm), 0) == 0).astype(BF16)
        for g in range(N_KV):
            r0 = ATTN_W + g * HEAD_DIM
            vsT_ref[g, 0:HEAD_DIM, rows] = feat[r0:r0 + HEAD_DIM].astype(BF16)
            vsT_ref[g, HEAD_DIM:V_ROWS, rows] = ones_rows
            r1 = ATTN_W + KV_W + g * HEAD_DIM
            vwT_ref[g, 0:HEAD_DIM, rows] = feat[r1:r1 + HEAD_DIM].astype(BF16)
            vwT_ref[g, HEAD_DIM:V_ROWS, rows] = ones_rows
        g0 = ATTN_W + 2 * KV_W
        gT_ref[:, rows] = jax.nn.sigmoid(feat[g0:g0 + N_BRANCH * N_HEADS])
    carry_ref[...] = prev_tail


_IN_SPLIT = 2
_IN_TM = 512


def _in_proj(x2, g1, wtok, wfeat, kgain, gq, convw, cgain):
    S = x2.shape[0]
    tm = _IN_TM
    tb = _IN_SPLIT * tm
    nt = S // tb
    ind = np.kron(np.eye(2 * N_KV, dtype=np.float32), np.ones((HEAD_DIM, HEAD_DIM), np.float32))
    ind = np.concatenate([ind, ind], axis=0)
    aug = np.zeros((tm, LANES), np.float32)
    blk = (np.arange(tm) // SEL_LEN) % BLOCKS_PER_CHUNK
    aug[np.arange(tm), HEAD_DIM + blk] = 1.0
    aug[:, HEAD_DIM + BLOCKS_PER_CHUNK:HEAD_DIM + BLOCKS_PER_CHUNK + 2] = 1.0
    const = lambda shape: pl.BlockSpec(shape, lambda i: (0,) * len(shape))
    out_shape = (
        jax.ShapeDtypeStruct((ATTN_W, S), BF16),
        jax.ShapeDtypeStruct((N_BRANCH * N_HEADS, S), F32),
        jax.ShapeDtypeStruct((N_KV, V_ROWS, S), BF16),
        jax.ShapeDtypeStruct((N_KV, V_ROWS, S), BF16),
        jax.ShapeDtypeStruct((N_KV, S, LANES), BF16),
        jax.ShapeDtypeStruct((S, LANES), BF16),
        jax.ShapeDtypeStruct((S, 2 * KV_W), BF16),
        jax.ShapeDtypeStruct((S, CONV_W), BF16),
    )
    out_specs = (
        pl.BlockSpec((ATTN_W, tb), lambda i: (0, i)),
        pl.BlockSpec((N_BRANCH * N_HEADS, tb), lambda i: (0, i)),
        pl.BlockSpec((N_KV, V_ROWS, tb), lambda i: (0, 0, i)),
        pl.BlockSpec((N_KV, V_ROWS, tb), lambda i: (0, 0, i)),
        pl.BlockSpec((N_KV, tb, LANES), lambda i: (0, i, 0)),
        pl.BlockSpec((tb, LANES), lambda i: (i, 0)),
        pl.BlockSpec((tb, 2 * KV_W), lambda i: (i, 0)),
        pl.BlockSpec((tb, CONV_W), lambda i: (i, 0)),
    )
    in_specs = [
        pl.BlockSpec((tb, D_MODEL), lambda i: (i, 0)),
        const((1, D_MODEL)),
        const(wtok.shape),
        const(wfeat.shape),
        const(ind.shape),
        const((1, 2 * KV_W)),
        const(aug.shape),
        const((HEAD_DIM, tm)),
        const((8, CONV_W)),
        const((1, CONV_W)),
    ]
    return pl.pallas_call(
        _in_proj_kernel,
        grid=(nt,),
        in_specs=in_specs,
        out_specs=out_specs,
        out_shape=out_shape,
        scratch_shapes=[pltpu.VMEM((_IN_SPLIT, tm + 8, CONV_W), F32), pltpu.VMEM((8, CONV_W), F32)],
        compiler_params=pltpu.CompilerParams(dimension_semantics=("arbitrary",), vmem_limit_bytes=VMEM_LIMIT),
        name="in_proj",
    )(x2, g1, wtok, wfeat, jnp.asarray(ind, BF16), kgain, jnp.asarray(aug), gq, convw, cgain)


def _compress_kernel(r_ref, wtop_ref, wbot_ref, posk_ref, posv_ref, w1k_ref, w1v_ref, w2k_ref, w2vT_ref, kgain_ref,
                     kc_ref, vcT_ref):
    nr = r_ref.shape[0]
    r = r_ref[...]
    u = jnp.dot(r, wtop_ref[...], preferred_element_type=F32)
    lo = jnp.dot(r, wbot_ref[...], preferred_element_type=F32)
    bias_k = jnp.dot(posk_ref[...], w1k_ref[...], preferred_element_type=F32)[0:1, :]
    bias_v = jnp.dot(posv_ref[...], w1v_ref[...], preferred_element_type=F32)[0:1, :]
    bias4 = jnp.concatenate([bias_k] * N_KV + [bias_v] * N_KV, axis=1)
    hid = u + pltpu.roll(lo, nr - 1, 0) + bias4
    act = jax.nn.gelu(hid)
    for g in range(N_KV):
        ak = act[:, g * CMP_HIDDEN:(g + 1) * CMP_HIDDEN].astype(BF16)
        av = act[:, (N_KV + g) * CMP_HIDDEN:(N_KV + g + 1) * CMP_HIDDEN].astype(BF16)
        kc = jnp.dot(ak, w2k_ref[...], preferred_element_type=F32)
        ssq = jnp.sum(kc * kc, axis=-1, keepdims=True)
        kc_ref[g] = (kc * lax.rsqrt(ssq * (1.0 / HEAD_DIM) + EPS) * kgain_ref[...]).astype(BF16)
        vcT_ref[g] = lax.dot_general(w2vT_ref[...], av, (((1,), (1,)), ((), ())),
                                     preferred_element_type=F32).astype(BF16)


def _compress(kcv, wtop, wbot, posk, posv, w1k, w1v, w2k, w2vT, kgain):
    S = kcv.shape[0]
    nr = S // CMP_STRIDE
    r = kcv.reshape(nr, CMP_STRIDE * 2 * KV_W)
    vmem = pl.BlockSpec(memory_space=pltpu.VMEM)
    return pl.pallas_call(
        _compress_kernel,
        out_shape=(jax.ShapeDtypeStruct((N_KV, nr, LANES), BF16),
                   jax.ShapeDtypeStruct((N_KV, HEAD_DIM, nr), BF16)),
        in_specs=[vmem] * 10,
        out_specs=(vmem, vmem),
        compiler_params=pltpu.CompilerParams(vmem_limit_bytes=VMEM_LIMIT),
        name="compress",
    )(r, wtop, wbot, posk, posv, w1k, w1v, w2k, w2vT, kgain)


def _tile4(row):
    return jnp.concatenate([row] * GQA, axis=1)


def _attn_kernel(qT_ref, gT_ref, ks_ref, vsT_ref, kw_ref, vwT_ref, kc_ref, vcT_ref,
                 tw_ref, ts_ref, tc_ref, b31_ref, gain_ref, out_ref,
                 sc_ref, sw_ref, imp_ref, amask_ref, qs_ref, sa_ref, sb_ref, qz_ref, qw_ref, fin_ref, ocmp_ref,
                 pk_ref):
    qw_refs = tuple(qw_ref.at[k] for k in range(FAR_UNROLL))
    ncr = kc_ref.shape[1]
    noct = amask_ref.shape[1] - 1
    nsel = noct * BLOCKS_PER_CHUNK
    width = GQA * LANES
    c = pl.program_id(0)
    t0 = c * Q_BLOCK
    tl = lax.broadcasted_iota(I32, (1, LANES), 1)
    cur = 2 * c + (tl >= SEL_LEN).astype(I32)
    zb = jnp.maximum(2 * c - 2, 0)
    nfar = (zb + BLOCKS_PER_CHUNK - 1) // BLOCKS_PER_CHUNK
    zs = pl.multiple_of(zb * SEL_LEN, Q_BLOCK)
    ts_off = pl.multiple_of(zs - (t0 - Q_BLOCK), Q_BLOCK)
    zeros_q = jnp.zeros((HEAD_DIM, width), BF16)

    col_ok = _tile4(t0 + tl) >= CMP_LEN - 1
    j = lax.broadcasted_iota(I32, (nsel, LANES), 0)
    forced = (j == 0) | (j == cur) | (j == cur - 1)
    @pl.when(c == 0)
    def _():
        imp_ref[...] = jnp.zeros(imp_ref.shape, F32)


    ws = pl.multiple_of(jnp.maximum(t0 - WINDOW, 0), Q_BLOCK)
    tw_off = pl.multiple_of(ws - (t0 - WINDOW), Q_BLOCK)
    nw = WINDOW + Q_BLOCK
    lo = pl.multiple_of(jnp.clip((8 * c - 16) // 16 * 16, 0, ncr - TC_WIN), 16)
    tc_off = pl.multiple_of(lo - (8 * c - 16) + TC_LEAD, 8)
    qgs = [jnp.concatenate([qT_ref[(GQA * g + r) * HEAD_DIM:(GQA * g + r + 1) * HEAD_DIM, :]
                            for r in range(GQA)], axis=1) for g in range(N_KV)]

    for g in range(N_KV):
        qw = jnp.concatenate([qgs[g], zeros_q] if g == 0 else [zeros_q, qgs[g]], axis=0)
        sw_ref[g] = (jnp.dot(kw_ref[pl.ds(ws, nw), :], qw, preferred_element_type=F32)
                     + tw_ref[g, pl.ds(tw_off, nw), :])

    def compressed(rows):
        for g in range(N_KV):
            qc = jnp.concatenate([qgs[g], zeros_q], axis=0)
            s = jnp.dot(kc_ref[g, 0:rows, :], qc, preferred_element_type=F32)
            row = lax.broadcasted_iota(I32, (rows, width), 0)
            sc_ref[g, 0:rows, :] = jnp.where(row < lo, s, NEG)
            s_loc = jnp.dot(kc_ref[g, pl.ds(lo, TC_WIN), :], qc, preferred_element_type=F32)
            sc_ref[g, pl.ds(lo, TC_WIN), :] = s_loc + tc_ref[g, pl.ds(tc_off, TC_WIN), :]
        for g in range(N_KV):
            s = sc_ref[g, 0:rows, :]
            m = jnp.max(s, axis=0, keepdims=True)
            e = jnp.exp2(s - m)
            l = jnp.sum(e, axis=0, keepdims=True)
            p = e * jnp.where(col_ok, 1.0 / jnp.maximum(l, 1e-30), 0.0)
            ocmp_ref[g] = jnp.dot(vcT_ref[g, :, 0:rows], p.astype(BF16), preferred_element_type=F32)
            imp_ref[g, 8:8 + rows, :] = (
                (p[:, 0:LANES] + p[:, LANES:2 * LANES]) + p[:, 2 * LANES:3 * LANES] + p[:, 3 * LANES:4 * LANES])

    sizes = list(range(CMP_STEP, ncr, CMP_STEP)) + [ncr]
    variant = jnp.minimum((lo + TC_WIN + CMP_STEP - 1) // CMP_STEP, len(sizes)) - 1
    for v, rows in enumerate(sizes):
        pl.when(variant == v)(functools.partial(compressed, rows))

    o_cmp, scores0 = [], []
    for g in range(N_KV):
        o_cmp.append(ocmp_ref[g])
        isel = jnp.zeros((nsel, LANES), F32)
        for off in (0, -1, 1, 0, 2, 1, 3, 2):
            isel = isel + imp_ref[g, pl.ds(8 + off, nsel, stride=4), :]
        score = jnp.where(forced, -jnp.inf, isel)
        scores0.append(jnp.where(j > cur, NEG, score))

    o_win = []
    for g in range(N_KV):
        sw = sw_ref[g]
        mw = jnp.max(sw, axis=0, keepdims=True)
        pw = jnp.exp2(sw - mw)
        accw = jnp.dot(vwT_ref[g, :, pl.ds(ws, nw)], pw.astype(BF16), preferred_element_type=F32)
        o_win.append(accw[0:HEAD_DIM] * (1.0 / jnp.maximum(accw[HEAD_DIM:HEAD_DIM + 1], 1e-30)))

    n_picks = SEL_TOPK - 3
    fast = list(scores0)
    for _ in range(n_picks):
        for g in range(N_KV):
            mx = jnp.max(fast[g], axis=0, keepdims=True)
            fast[g] = jnp.where(fast[g] == mx, -jnp.inf, fast[g])
    eligible = (j <= cur) & jnp.logical_not(forced)
    n_eligible = cur + 1 - (1 + (cur >= 1).astype(I32) + (cur >= 2).astype(I32))
    expect = jnp.minimum(n_picks, n_eligible).astype(F32)
    wrong = jnp.zeros((1, LANES), F32)
    for g in range(N_KV):
        pk_ref[g] = fast[g]
        got = jnp.sum(jnp.where((fast[g] == -jnp.inf) & eligible, 1.0, 0.0), axis=0, keepdims=True)
        wrong = jnp.maximum(wrong, jnp.abs(got - expect))

    @pl.when(jnp.max(wrong) > 0.0)
    def _():
        def pick(_, score):
            mx = jnp.max(score, axis=0, keepdims=True)
            first = jnp.min(jnp.where(score == mx, j, nsel), axis=0, keepdims=True)
            return jnp.where(j == first, -jnp.inf, score)

        for g in range(N_KV):
            pk_ref[g] = lax.fori_loop(0, n_picks, pick, scores0[g])

    picks = [pk_ref[g] for g in range(N_KV)]

    m_init, acc_init, gates = [], [], []
    for g in range(N_KV):
        qg = qgs[g]
        b31row = b31_ref[g, 0:1, :]
        chosen = (picks[g] == -jnp.inf) & (j <= cur)
        a_all = _tile4(jnp.where(chosen, 0.0, -MASK_BIG)).reshape(noct, BLOCKS_PER_CHUNK, width)
        b_hi = b31row.astype(BF16).astype(F32)
        erow = lax.broadcasted_iota(I32, (8, width), 0)
        extra = jnp.where(erow == 0, b_hi, jnp.where(erow == 1, b31row - b_hi, 0.0))
        amask_ref[g, 0:noct] = jnp.concatenate(
            [a_all, jnp.broadcast_to(extra[None], (noct, 8, width))], axis=1).astype(BF16)
        far_last = jnp.maximum(nfar - 1, 0)
        orow = lax.broadcasted_iota(I32, (MASK_ROWS, width), 0)
        keep = (orow >= BLOCKS_PER_CHUNK) | (orow + BLOCKS_PER_CHUNK * far_last < zb)
        amask_ref[g, noct] = jnp.where(keep, amask_ref[g, far_last].astype(F32), -MASK_BIG).astype(BF16)
        for q_ref in (qs_ref,) + qw_refs:
            q_ref[g, 0:HEAD_DIM, :] = qg
            q_ref[g, HEAD_DIM + MASK_ROWS:LANES, :] = jnp.zeros((LANES - HEAD_DIM - MASK_ROWS, width), BF16)

        halves = []
        for h in range(2):
            octet = amask_ref[g, (zb + 2 * h) // BLOCKS_PER_CHUNK].astype(F32)
            qz_ref[2 * g + h, HEAD_DIM:HEAD_DIM + MASK_ROWS, :] = jnp.where(
                orow < BLOCKS_PER_CHUNK, octet, 0.0).astype(BF16)
            qz_ref[2 * g + h, 0:HEAD_DIM, :] = qg
            qz_ref[2 * g + h, HEAD_DIM + MASK_ROWS:LANES, :] = jnp.zeros((LANES - HEAD_DIM - MASK_ROWS, width), BF16)
            sh = jnp.dot(ks_ref[g, pl.ds(zs + h * Q_BLOCK, Q_BLOCK), :], qz_ref[2 * g + h], preferred_element_type=F32)
            halves.append(sh + ts_ref[g, pl.ds(ts_off + h * Q_BLOCK, Q_BLOCK), :])
        sz = jnp.concatenate(halves, axis=0)
        mz = jnp.max(sz, axis=0, keepdims=True)
        pz = jnp.exp2(sz - mz)
        m_init.append(mz)
        acc_init.append(jnp.dot(vsT_ref[g, :, pl.ds(zs, 2 * Q_BLOCK)], pz.astype(BF16), preferred_element_type=F32))
        gates.append([jnp.concatenate([gT_ref[br * N_HEADS + GQA * g + r:br * N_HEADS + GQA * g + r + 1, :]
                                       for r in range(GQA)], axis=1) for br in range(N_BRANCH)])

    nsub = 1
    sub = CHUNK // nsub

    def score(g, u, buf_ref, h, q_ref):
        k0 = pl.multiple_of(jnp.minimum(u, noct - 1) * CHUNK + h * sub, sub)
        s = jnp.dot(ks_ref[g, pl.ds(k0, sub), :], q_ref[g], preferred_element_type=F32)
        buf_ref[g, h * sub:(h + 1) * sub, :] = s
        return jnp.max(s, axis=0, keepdims=True)

    def set_mask_rows(g, u, q_ref):
        oct_id = jnp.where(u < nfar - 1, u, noct)
        q_ref[g, HEAD_DIM:HEAD_DIM + MASK_ROWS, :] = amask_ref[g, oct_id]

    def accumulate(g, u, buf_ref, h, mn, acc):
        k0 = pl.multiple_of(jnp.minimum(u, noct - 1) * CHUNK + h * sub, sub)
        p = jnp.exp2(buf_ref[g, h * sub:(h + 1) * sub, :] - mn)
        return acc + jnp.dot(vsT_ref[g, :, pl.ds(k0, sub)], p.astype(BF16), preferred_element_type=F32)

    def step(u, cur_ref, nxt_ref, q_ref, state, cms):
        out_state, out_cms = [], []
        for g in range(N_KV):
            m, acc = state[2 * g], state[2 * g + 1]
            mn = jnp.maximum(m, cms[g])
            acc = jnp.exp2(m - mn) * acc
            set_mask_rows(g, u + 1, q_ref)
            cm = None
            for h in range(nsub):
                ch = score(g, u + 1, nxt_ref, h, q_ref)
                cm = ch if cm is None else jnp.maximum(cm, ch)
                acc = accumulate(g, u, cur_ref, h, mn, acc)
            out_cms.append(cm)
            out_state += [mn, acc]
        return out_state, out_cms

    bufs = (sa_ref, sb_ref)

    def far(i, carry):
        state, cms = list(carry[0:4]), list(carry[4:6])
        for k in range(FAR_UNROLL):
            state, cms = step(FAR_UNROLL * i + k, bufs[k % 2], bufs[(k + 1) % 2], qw_refs[k], state, cms)
        return tuple(state + cms)

    first = []
    for g in range(N_KV):
        set_mask_rows(g, 0, qs_ref)
        cm = None
        for h in range(nsub):
            ch = score(g, 0, sa_ref, h, qs_ref)
            cm = ch if cm is None else jnp.maximum(cm, ch)
        first.append(cm)
    nchunks = jnp.maximum(nfar, 1)
    nloops = (nchunks - 1) // FAR_UNROLL
    carry = lax.fori_loop(0, nloops, far, (m_init[0], acc_init[0], m_init[1], acc_init[1], first[0], first[1]))
    u_last = FAR_UNROLL * nloops

    def finish(left):
        for g in range(N_KV):
            m, acc, cm = carry[2 * g], carry[2 * g + 1], carry[4 + g]
            for k in range(left):
                mn = jnp.maximum(m, cm)
                acc = jnp.exp2(m - mn) * acc
                if k + 1 < left:
                    set_mask_rows(g, u_last + k + 1, qw_refs[k])
                    cm = score(g, u_last + k + 1, bufs[(k + 1) % 2], 0, qw_refs[k])
                acc = accumulate(g, u_last + k, bufs[k % 2], 0, mn, acc)
                m = mn
            fin_ref[g] = acc

    for left in range(1, FAR_UNROLL + 1):
        pl.when(nchunks - u_last == left)(functools.partial(finish, left))
    carry = [None, fin_ref[0], None, fin_ref[1]]

    o_all, ssq = [], jnp.zeros((1, LANES), F32)
    for g in range(N_KV):
        acc = carry[2 * g + 1]
        o_sel = acc[0:HEAD_DIM] * (1.0 / jnp.maximum(acc[HEAD_DIM:HEAD_DIM + 1], 1e-30))
        o = gates[g][0] * o_cmp[g] + gates[g][1] * o_sel + gates[g][2] * o_win[g]
        o_all.append(o)
        cs = jnp.sum(o * o, axis=0, keepdims=True)
        ssq = ssq + ((cs[:, 0:LANES] + cs[:, LANES:2 * LANES]) + (cs[:, 2 * LANES:3 * LANES] + cs[:, 3 * LANES:]))
    inv = _tile4(lax.rsqrt(ssq * (1.0 / ATTN_W) + EPS))
    for g in range(N_KV):
        on = o_all[g] * inv * gain_ref[g]
        for k in range(GQA // 2):
            pair = jnp.concatenate([on[:, (2 * k) * LANES:(2 * k + 1) * LANES],
                                    on[:, (2 * k + 1) * LANES:(2 * k + 2) * LANES]], axis=0)
            col = (GQA * g + 2 * k) * HEAD_DIM
            out_ref[:, col:col + 2 * HEAD_DIM] = pair.T.astype(BF16)


def _attention(qT, gT, ks, vsT, kw, vwT, kc, vcT, tables, gain_b):
    S = qT.shape[1]
    nq = S // Q_BLOCK
    ncr = kc.shape[1]
    nsel = S // SEL_LEN
    noct = S // CHUNK
    width = GQA * LANES
    tw, ts, tc, b31 = tables
    vmem = pl.BlockSpec(memory_space=pltpu.VMEM)
    in_specs = [
        pl.BlockSpec((ATTN_W, Q_BLOCK), lambda c: (0, c)),
        pl.BlockSpec((N_BRANCH * N_HEADS, Q_BLOCK), lambda c: (0, c)),
    ] + [vmem] * 11
    return pl.pallas_call(
        _attn_kernel,
        grid=(nq,),
        in_specs=in_specs,
        out_specs=pl.BlockSpec((Q_BLOCK, ATTN_W), lambda c: (c, 0)),
        out_shape=jax.ShapeDtypeStruct((S, ATTN_W), BF16),
        scratch_shapes=[
            pltpu.VMEM((N_KV, ncr, width), F32),
            pltpu.VMEM((N_KV, WINDOW + Q_BLOCK, width), F32),
            pltpu.VMEM((N_KV, ncr + 16, LANES), F32),
            pltpu.VMEM((N_KV, noct + 1, MASK_ROWS, width), BF16),
            pltpu.VMEM((N_KV, LANES, width), BF16),
            pltpu.VMEM((N_KV, CHUNK, width), F32),
            pltpu.VMEM((N_KV, CHUNK, width), F32),
            pltpu.VMEM((2 * N_KV, LANES, width), BF16),
            pltpu.VMEM((FAR_UNROLL, N_KV, LANES, width), BF16),
            pltpu.VMEM((N_KV, V_ROWS, width), F32),
            pltpu.VMEM((N_KV, HEAD_DIM, width), F32),
            pltpu.VMEM((N_KV, nsel, LANES), F32),
        ],
        compiler_params=pltpu.CompilerParams(dimension_semantics=("arbitrary",), vmem_limit_bytes=VMEM_LIMIT),
        name="nsa_attention",
    )(qT, gT, ks, vsT, kw, vwT, kc, vcT, tw, ts, tc, b31, gain_b)


def _mix_and_norm(x_ref, attn_ref, conv_ref, wout_ref, g2_ref, rows=slice(None)):
    x1 = x_ref[rows, :] + jnp.dot(attn_ref[rows, :], wout_ref[0:ATTN_W, :], preferred_element_type=F32) \
        + jnp.dot(conv_ref[rows, :], wout_ref[ATTN_W:, :], preferred_element_type=F32)
    ms = jnp.mean(x1 * x1, axis=-1, keepdims=True)
    h2 = x1 * lax.rsqrt(ms + EPS) * g2_ref[...]
    return x1, h2


def _ffn_kernel(x_ref, attn_ref, conv_ref, wout_ref, g2_ref, wg_ref, wu_ref, wd_ref, out_ref):
    x1, h2 = _mix_and_norm(x_ref, attn_ref, conv_ref, wout_ref, g2_ref)
    h2 = h2.astype(BF16)
    a = jnp.dot(h2, wg_ref[...], preferred_element_type=F32)
    u = jnp.dot(h2, wu_ref[...], preferred_element_type=F32)
    y = (a * jax.nn.sigmoid(a) * u).astype(BF16)
    out_ref[...] = x1 + jnp.dot(y, wd_ref[...], preferred_element_type=F32)


_FFN_TM = 512


def _outproj_ffn(x2, attn_n, conv_n, wout, g2, wg, wu, wd):
    S = x2.shape[0]
    tm = _FFN_TM
    resident = pl.BlockSpec(memory_space=pltpu.VMEM)
    return pl.pallas_call(
        _ffn_kernel,
        grid=(S // tm,),
        in_specs=[
            pl.BlockSpec((tm, D_MODEL), lambda i: (i, 0)),
            pl.BlockSpec((tm, ATTN_W), lambda i: (i, 0)),
            pl.BlockSpec((tm, CONV_W), lambda i: (i, 0)),
            resident, resident, resident, resident, resident,
        ],
        out_specs=pl.BlockSpec((tm, D_MODEL), lambda i: (i, 0)),
        out_shape=jax.ShapeDtypeStruct((S, D_MODEL), F32),
        compiler_params=pltpu.CompilerParams(dimension_semantics=("arbitrary",), vmem_limit_bytes=VMEM_LIMIT),
        name="outproj_ffn",
    )(x2, attn_n, conv_n, wout, g2, wg, wu, wd)


TOKEN_TILE = (D_MODEL // LANES, LANES)


def _to_token_tiles(ref, rows):
    x = jnp.stack([rows[:, k * LANES:(k + 1) * LANES] for k in range(TOKEN_TILE[0])], axis=0)
    ref[...] = pltpu.einshape('ktl->tkl', x)


def _from_token_tiles(ref):
    x = pltpu.einshape('tkl->ktl', ref[...])
    return jnp.concatenate([x[k] for k in range(TOKEN_TILE[0])], axis=1)


_ROUTER_SPLIT = 2


def _router_kernel(x_ref, attn_ref, conv_ref, wout_ref, g2_ref, rw_ref, rb_ref, tri_ref,
                   x1_ref, h2_ref, route_ref, cnt_ref, run_ref):
    tm = _FFN_TM
    i = pl.program_id(0)

    @pl.when(i == 0)
    def _():
        run_ref[...] = jnp.zeros(run_ref.shape, F32)

    running = run_ref[0:1, :]
    for part in range(_ROUTER_SPLIT):
        rows = slice(part * tm, (part + 1) * tm)
        x1, h2 = _mix_and_norm(x_ref, attn_ref, conv_ref, wout_ref, g2_ref, rows)
        x1_ref[rows, :] = x1
        h2b = h2.astype(BF16)
        _to_token_tiles(h2_ref.at[rows], h2b.astype(F32))
        lane = lax.broadcasted_iota(I32, (tm, LANES), 1)
        logits = jnp.dot(h2b, rw_ref[...], preferred_element_type=F32) + rb_ref[...]
        logits = jnp.where(lane < N_EXPERTS, logits, -jnp.inf)
        m1 = jnp.max(logits, axis=-1, keepdims=True)
        i1 = jnp.min(jnp.where(logits == m1, lane, LANES), axis=-1, keepdims=True)
        rest = jnp.where(lane == i1, -jnp.inf, logits)
        m2 = jnp.max(rest, axis=-1, keepdims=True)
        i2 = jnp.min(jnp.where(rest == m2, lane, LANES), axis=-1, keepdims=True)
        e2 = jnp.exp(m2 - m1)
        den = 1.0 + e2
        oh1 = (lane == i1).astype(F32)
        oh2 = (lane == i2).astype(F32)
        both = oh1 + oh2
        before = running + jnp.dot(tri_ref[...], both.astype(BF16), preferred_element_type=F32)
        rank1 = jnp.sum(before * oh1, axis=-1, keepdims=True)
        rank2 = jnp.sum(before * oh2, axis=-1, keepdims=True)
        fields = (i1.astype(F32), i2.astype(F32), rank1, rank2, 1.0 / den, e2 / den)
        route = jnp.zeros((tm, LANES), F32)
        for k, v in enumerate(fields):
            route = jnp.where(lane == k, v, route)
        route_ref[rows, :] = route
        running = running + jnp.sum(both, axis=0, keepdims=True)
    run_ref[...] = jnp.broadcast_to(running, run_ref.shape)
    cnt_ref[...] = run_ref[...]


def _outproj_router(x2, attn_n, conv_n, wout, g2, rw, rb):
    S = x2.shape[0]
    tm = _FFN_TM
    tb = _ROUTER_SPLIT * tm
    tri = np.tril(np.ones((tm, tm), np.float32), -1)
    const = lambda shape: pl.BlockSpec(shape, lambda i: (0,) * len(shape))
    return pl.pallas_call(
        _router_kernel,
        grid=(S // tb,),
        in_specs=[
            pl.BlockSpec((tb, D_MODEL), lambda i: (i, 0)),
            pl.BlockSpec((tb, ATTN_W), lambda i: (i, 0)),
            pl.BlockSpec((tb, CONV_W), lambda i: (i, 0)),
            const(wout.shape), const((1, D_MODEL)), const(rw.shape), const((1, LANES)), const((tm, tm)),
        ],
        out_specs=(pl.BlockSpec((tb, D_MODEL), lambda i: (i, 0)),
                   pl.BlockSpec((tb,) + TOKEN_TILE, lambda i: (i, 0, 0)),
                   pl.BlockSpec((tb, LANES), lambda i: (i, 0)),
                   const((8, LANES))),
        out_shape=(jax.ShapeDtypeStruct((S, D_MODEL), F32),
                   jax.ShapeDtypeStruct((S,) + TOKEN_TILE, F32),
                   jax.ShapeDtypeStruct((S, LANES), F32),
                   jax.ShapeDtypeStruct((8, LANES), F32)),
        scratch_shapes=[pltpu.VMEM((8, LANES), F32)],
        compiler_params=pltpu.CompilerParams(dimension_semantics=("arbitrary",), vmem_limit_bytes=VMEM_LIMIT),
        name="outproj_router",
    )(x2, attn_n, conv_n, wout, g2, rw, rb, jnp.asarray(tri, BF16))


_ROW_TM = 256
_EXP_TM = 512
_DMA_UNROLL = 8


def _row_copy(src_ref, src_row, dst_ref, dst_row, sem):
    return pltpu.make_async_copy(src_ref.at[src_row], dst_ref.at[dst_row], sem)


def _dispatch_kernel(pos_ref, ztile_ref, h_ref, xs_ref, hbuf_ref, zbuf_ref, sem, zsem):
    tm = h_ref.shape[0]
    i = pl.program_id(0)
    slot = i % 2

    @pl.when(i == 0)
    def _():
        zbuf_ref[...] = jnp.zeros(zbuf_ref.shape, F32)
        for z in range(ztile_ref.shape[0]):
            clear = pltpu.make_async_copy(zbuf_ref, xs_ref.at[pl.ds(ztile_ref[z] * _EXP_TM, _EXP_TM)], zsem)
            clear.start()
            clear.wait()

    hbuf_ref[slot] = h_ref[...]

    def issue(t, carry):
        for k in range(2):
            _row_copy(hbuf_ref.at[slot], t, xs_ref, pos_ref[0, 0, 2 * t + k], sem.at[slot]).start(priority=k)
        return carry

    lax.fori_loop(0, tm, issue, 0, unroll=_DMA_UNROLL)

    def drain(which):
        def body(t, carry):
            for k in range(2):
                _row_copy(hbuf_ref.at[which], 0, xs_ref, 0, sem.at[which]).wait()
            return carry
        lax.fori_loop(0, tm, body, 0, unroll=_DMA_UNROLL)

    @pl.when(i > 0)
    def _():
        drain(1 - slot)

    @pl.when(i == pl.num_programs(0) - 1)
    def _():
        drain(slot)


def _dispatch(pos3, pad_tiles, h2, n_rows):
    S = h2.shape[0]
    tm = _ROW_TM
    return pl.pallas_call(
        _dispatch_kernel,
        grid=(S // tm,),
        in_specs=[
            pl.BlockSpec((1, 1, 2 * tm), lambda i: (i, 0, 0), memory_space=pltpu.SMEM),
            pl.BlockSpec(memory_space=pltpu.SMEM),
            pl.BlockSpec((tm,) + TOKEN_TILE, lambda i: (i, 0, 0)),
        ],
        out_specs=pl.BlockSpec(memory_space=pl.ANY),
        out_shape=jax.ShapeDtypeStruct((n_rows,) + TOKEN_TILE, F32),
        scratch_shapes=[pltpu.VMEM((2, tm) + TOKEN_TILE, F32), pltpu.VMEM((_EXP_TM,) + TOKEN_TILE, F32),
                        pltpu.SemaphoreType.DMA((2,)), pltpu.SemaphoreType.DMA],
        compiler_params=pltpu.CompilerParams(dimension_semantics=("arbitrary",), vmem_limit_bytes=VMEM_LIMIT),
        name="moe_dispatch",
    )(pos3, pad_tiles, h2)


def _experts_kernel(te_ref, tb_ref, nt_ref, xs_ref, wg_ref, wu_ref, wd_ref, ys_ref):
    i = pl.program_id(0)

    @pl.when(i < nt_ref[0])
    def _():
        x = _from_token_tiles(xs_ref).astype(BF16)
        a = jnp.dot(x, wg_ref[0], preferred_element_type=F32)
        u = jnp.dot(x, wu_ref[0], preferred_element_type=F32)
        y = (a * jax.nn.sigmoid(a) * u).astype(BF16)
        _to_token_tiles(ys_ref, jnp.dot(y, wd_ref[0], preferred_element_type=F32))

    @pl.when(i >= nt_ref[0])
    def _():
        ys_ref[...] = jnp.zeros(ys_ref.shape, F32)


def _experts(tile_e, tile_b, n_tiles, xs, wg, wu, wd):
    n_rows = xs.shape[0]
    tm = _EXP_TM
    weights = lambda shape: pl.BlockSpec(shape, lambda i, te, tb, nt: (te[i], 0, 0), pipeline_mode=pl.Buffered(1))
    grid_spec = pltpu.PrefetchScalarGridSpec(
        num_scalar_prefetch=3,
        grid=(n_rows // tm,),
        in_specs=[
            pl.BlockSpec((tm,) + TOKEN_TILE, lambda i, te, tb, nt: (tb[i], 0, 0)),
            weights((1, D_MODEL, D_FF)), weights((1, D_MODEL, D_FF)), weights((1, D_FF, D_MODEL)),
        ],
        out_specs=pl.BlockSpec((tm,) + TOKEN_TILE, lambda i, te, tb, nt: (tb[i], 0, 0)),
    )
    return pl.pallas_call(
        _experts_kernel,
        grid_spec=grid_spec,
        out_shape=jax.ShapeDtypeStruct((n_rows,) + TOKEN_TILE, F32),
        compiler_params=pltpu.CompilerParams(dimension_semantics=("arbitrary",), vmem_limit_bytes=VMEM_LIMIT),
        name="moe_experts",
    )(tile_e, tile_b, n_tiles, xs, wg, wu, wd)


def _combine_kernel(pos_ref, nxt_ref, x1_ref, route_ref, ys_ref, out_ref, y1_ref, y2_ref, sem):
    tm = x1_ref.shape[0]
    i = pl.program_id(0)
    slot = i % 2

    def issue(src_pos_ref, which):
        def body(t, carry):
            _row_copy(ys_ref, src_pos_ref[0, 0, 2 * t], y1_ref.at[which], t, sem.at[which]).start(priority=0)
            _row_copy(ys_ref, src_pos_ref[0, 0, 2 * t + 1], y2_ref.at[which], t, sem.at[which]).start(priority=1)
            return carry
        lax.fori_loop(0, tm, body, 0, unroll=_DMA_UNROLL)

    @pl.when(i == 0)
    def _():
        issue(pos_ref, slot)

    @pl.when(i + 1 < pl.num_programs(0))
    def _():
        issue(nxt_ref, 1 - slot)

    def drain(t, carry):
        _row_copy(ys_ref, 0, y1_ref.at[slot], 0, sem.at[slot]).wait()
        _row_copy(ys_ref, 0, y2_ref.at[slot], 0, sem.at[slot]).wait()
        return carry

    lax.fori_loop(0, tm, drain, 0, unroll=_DMA_UNROLL)
    lane = lax.broadcasted_iota(I32, (tm, LANES), 1)
    route = route_ref[...]
    w1 = jnp.sum(jnp.where(lane == 4, route, 0.0), axis=-1, keepdims=True)
    w2 = jnp.sum(jnp.where(lane == 5, route, 0.0), axis=-1, keepdims=True)
    out_ref[...] = x1_ref[...] + (_from_token_tiles(y1_ref.at[slot]) * w1 + _from_token_tiles(y2_ref.at[slot]) * w2)


def _combine(pos3, x1, route, ys):
    S = x1.shape[0]
    tm = _ROW_TM
    last = S // tm - 1
    return pl.pallas_call(
        _combine_kernel,
        grid=(S // tm,),
        in_specs=[
            pl.BlockSpec((1, 1, 2 * tm), lambda i: (i, 0, 0), memory_space=pltpu.SMEM),
            pl.BlockSpec((1, 1, 2 * tm), lambda i: (jnp.minimum(i + 1, last), 0, 0), memory_space=pltpu.SMEM),
            pl.BlockSpec((tm, D_MODEL), lambda i: (i, 0)),
            pl.BlockSpec((tm, LANES), lambda i: (i, 0)),
            pl.BlockSpec(memory_space=pl.ANY),
        ],
        out_specs=pl.BlockSpec((tm, D_MODEL), lambda i: (i, 0)),
        out_shape=jax.ShapeDtypeStruct((S, D_MODEL), F32),
        scratch_shapes=[pltpu.VMEM((2, tm) + TOKEN_TILE, F32), pltpu.VMEM((2, tm) + TOKEN_TILE, F32),
                        pltpu.SemaphoreType.DMA((2,))],
        compiler_params=pltpu.CompilerParams(dimension_semantics=("arbitrary",), vmem_limit_bytes=VMEM_LIMIT),
        name="moe_combine",
    )(pos3, pos3, x1, route, ys)


def _moe(h2, x1, route, counts, wg, wu, wd):
    S = h2.shape[0]
    tm = _EXP_TM
    n_tiles_max = 2 * S // tm + N_EXPERTS
    cnt = counts[0, 0:N_EXPERTS].astype(I32)
    tiles = (cnt + tm - 1) // tm
    first = jnp.cumsum(tiles) - tiles
    n_tiles = jnp.sum(tiles)
    idx = jnp.arange(n_tiles_max, dtype=I32)
    last = jnp.minimum(idx, n_tiles - 1)
    tile_e = (jnp.sum(last[:, None] >= first[None, :], axis=1) - 1).astype(I32)
    eid = route[:, 0:2].astype(I32)
    pos = first[eid] * tm + route[:, 2:4].astype(I32)
    pos3 = pos.reshape(S // _ROW_TM, 1, 2 * _ROW_TM)
    pad_tiles = jnp.concatenate([jnp.maximum(first + tiles - 1, 0),
                                 jnp.minimum(n_tiles + jnp.arange(N_EXPERTS, dtype=I32), n_tiles_max - 1)])
    xs = _dispatch(pos3, pad_tiles.astype(I32), h2, n_tiles_max * tm)
    ys = _experts(tile_e, idx, n_tiles.reshape(1), xs, wg, wu, wd)
    return _combine(pos3, x1, route, ys)


def _split_w_in(w):
    o = np.cumsum([0, ATTN_W] + [KV_W] * 6 + [N_BRANCH * N_HEADS] + [CONV_W] * 3)
    q, kc, vc, ksl, vsl, kwn, vwn, gts, cb, cc, ch = (w[:, o[i]:o[i + 1]] for i in range(11))
    perm = np.array([h * N_BRANCH + br for br in range(N_BRANCH) for h in range(N_HEADS)])
    wtok = jnp.concatenate([kc, vc, ksl, kwn, cb, cc, ch], axis=1).astype(BF16)
    feat = jnp.concatenate([q, vsl, vwn, gts[:, perm], jnp.zeros((D_MODEL, 8), w.dtype)], axis=1)
    return wtok, feat.T.astype(BF16)


def _expand_cmp_w1(w1k, w1v):
    kinds = jnp.stack([w1k, w1v]).reshape(2, CMP_LEN, HEAD_DIM, CMP_HIDDEN)
    per_col = jnp.repeat(kinds, N_KV, axis=0)
    eye = jnp.eye(2 * N_KV, dtype=F32)
    out = []
    for l0 in (0, CMP_STRIDE):
        w = per_col[:, l0:l0 + CMP_STRIDE].transpose(1, 0, 2, 3)
        blk = w[:, :, :, None, :] * eye[None, :, None, :, None]
        out.append(blk.reshape(CMP_STRIDE * 2 * KV_W, 2 * N_KV * CMP_HIDDEN).astype(BF16))
    return out


def kernel(x, rel_bias, norm1, w_in, q_norm, k_norm, cmp_pos_k, cmp_pos_v, cmp_k_w1, cmp_k_w2, cmp_v_w1, cmp_v_w2,
           conv_w, attn_out_norm, conv_out_norm, w_out, norm2, ffn_w_gate, ffn_w_up, ffn_w_down, router_w, router_b,
           moe_w_gate, moe_w_up, moe_w_down):
    B, S, _ = x.shape
    assert B == 1 and S % CHUNK == 0 and S >= WINDOW + Q_BLOCK
    depth = norm1.shape[0]
    x2 = x.reshape(S, D_MODEL)
    tables = _bias_tables(rel_bias)
    for layer in range(depth):
        wtok, wfeat = _split_w_in(w_in[layer])
        kgain = jnp.concatenate([jnp.tile(k_norm[layer, 1], N_KV), jnp.tile(k_norm[layer, 2], N_KV)])[None, :]
        gq = jnp.broadcast_to(q_norm[layer][:, None], (HEAD_DIM, _IN_TM))
        convw = jnp.pad(conv_w[layer], ((0, 8 - CONV_K), (0, 0)))
        qT, gT, vsT, vwT, ks, kw, kcv, conv_n = _in_proj(
            x2, norm1[layer][None, :], wtok, wfeat, kgain, gq, convw, conv_out_norm[layer][None, :])

        wtop, wbot = _expand_cmp_w1(cmp_k_w1[layer], cmp_v_w1[layer])
        posk = jnp.broadcast_to(cmp_pos_k[layer].reshape(1, -1), (8, CMP_LEN * HEAD_DIM))
        posv = jnp.broadcast_to(cmp_pos_v[layer].reshape(1, -1), (8, CMP_LEN * HEAD_DIM))
        w2k = jnp.pad(cmp_k_w2[layer], ((0, 0), (0, LANES - HEAD_DIM))).astype(BF16)
        w2vT = cmp_v_w2[layer].T.astype(BF16)
        kcgain = jnp.pad(k_norm[layer, 0], (0, LANES - HEAD_DIM))[None, :]
        kc, vcT = _compress(kcv, wtop, wbot, posk, posv, cmp_k_w1[layer], cmp_v_w1[layer], w2k, w2vT, kcgain)

        gain_b = jnp.broadcast_to(attn_out_norm[layer].reshape(N_KV, GQA, HEAD_DIM).transpose(0, 2, 1)[:, :, :, None],
                                  (N_KV, HEAD_DIM, GQA, LANES)).reshape(N_KV, HEAD_DIM, GQA * LANES)
        attn_n = _attention(qT, gT, ks, vsT, kw, vwT, kc, vcT, tables, gain_b)

        wout = w_out[layer].astype(BF16)
        g2 = norm2[layer][None, :]
        i = layer // 2
        if layer % 2 == 0:
            x2 = _outproj_ffn(x2, attn_n, conv_n, wout, g2, ffn_w_gate[i].astype(BF16), ffn_w_up[i].astype(BF16),
                              ffn_w_down[i].astype(BF16))
        else:
            rw = jnp.pad(router_w[i], ((0, 0), (0, LANES - N_EXPERTS))).astype(BF16)
            rb = jnp.pad(router_b[i], (0, LANES - N_EXPERTS))[None, :]
            x1, h2, route, counts = _outproj_router(x2, attn_n, conv_n, wout, g2, rw, rb)
            x2 = _moe(h2, x1, route, counts, moe_w_gate[i].astype(BF16), moe_w_up[i].astype(BF16),
                      moe_w_down[i].astype(BF16))
    return x2.reshape(B, S, D_MODEL)
```

```python
import functools
import math

import numpy as np
import jax
import jax.numpy as jnp
from jax import lax
from jax.experimental import pallas as pl
from jax.experimental.pallas import tpu as pltpu

F32 = jnp.float32
BF16 = jnp.bfloat16
I32 = jnp.int32

D_MODEL = 1024
HEAD_DIM = 64
N_HEADS = 8
N_KV = 2
GQA = N_HEADS // N_KV
ATTN_W = N_HEADS * HEAD_DIM
KV_W = N_KV * HEAD_DIM
N_BRANCH = 3
CONV_W = 512
CONV_K = 3
CMP_LEN = 32
CMP_STRIDE = 16
CMP_HIDDEN = 128
SEL_LEN = 64
SEL_TOPK = 16
WINDOW = 512
Q_BLOCK = 128
N_BUCKETS = 32
MAX_DISTANCE = 128
D_FF = 2816
N_EXPERTS = 8
EPS = 1e-6
NEG = -1e30
MASK_BIG = 2.0 ** 60
MASK_ROWS = 16
SCALE = HEAD_DIM ** -0.5
LOG2E = math.log2(math.e)

LANES = 128
V_ROWS = 80
CHUNK = 512
BLOCKS_PER_CHUNK = CHUNK // SEL_LEN
TC_LEAD = 24
TC_ROWS = 88
TC_WIN = 48
CMP_STEP = 128
FAR_UNROLL = 8
VMEM_LIMIT = 56 * 1024 * 1024


def _bucket_np(dist):
    n = np.maximum(dist, 0)
    max_exact = N_BUCKETS // 2
    nf = np.maximum(n, max_exact).astype(np.float64)
    v = np.log(nf / max_exact) / math.log(MAX_DISTANCE / max_exact) * (N_BUCKETS - max_exact)
    frac = np.abs(v - np.round(v))
    assert np.all((frac > 1e-6) | (n <= max_exact) | (n >= MAX_DISTANCE)), "bucket boundary is precision dependent"
    large = np.minimum(max_exact + (v + 1e-9).astype(np.int32), N_BUCKETS - 1)
    return np.where(n < max_exact, n, large).astype(np.int32)


def _index_tables():
    tl = np.arange(Q_BLOCK)[None, :]
    r = np.arange(WINDOW + Q_BLOCK + WINDOW)[:, None]
    d = tl + WINDOW - r
    idx_w = np.where((d >= 0) & (d < WINDOW), _bucket_np(d), -1)
    r = np.arange(3 * Q_BLOCK)[:, None]
    d = tl + Q_BLOCK - r
    idx_s = np.where(d >= 0, _bucket_np(d), -1)
    r = np.arange(TC_ROWS)[:, None] - TC_LEAD
    d = tl - CMP_STRIDE * r + (CMP_STRIDE * 16 - (CMP_LEN - 1))
    idx_c = np.where((d >= 0) & (r < 32), _bucket_np(d), -1)
    return idx_w.astype(np.int32), idx_s.astype(np.int32), idx_c.astype(np.int32)


def _tables_kernel(rb_ref, iw_ref, is_ref, ic_ref, tw_ref, ts_ref, tc_ref, b31_ref):
    head_lanes = [(h // GQA, slice((h % GQA) * LANES, (h % GQA + 1) * LANES)) for h in range(N_HEADS)]
    for idx_ref, out_ref in ((iw_ref, tw_ref), (is_ref, ts_ref), (ic_ref, tc_ref)):
        out_ref[...] = jnp.full(out_ref.shape, NEG, F32)

        def body(b, carry, idx_ref=idx_ref, out_ref=out_ref):
            hit = idx_ref[...] == b
            for h, (g, lanes) in enumerate(head_lanes):
                out_ref[g, :, lanes] = jnp.where(hit, rb_ref[b, h] * LOG2E, out_ref[g, :, lanes])
            return carry

        lax.fori_loop(0, N_BUCKETS, body, 0)
    for h, (g, lanes) in enumerate(head_lanes):
        far_bias = rb_ref[N_BUCKETS - 1, h] * LOG2E
        tc_ref[g, :, lanes] = jnp.where(ic_ref[...] >= 0, tc_ref[g, :, lanes] - far_bias, NEG)
        b31_ref[g, :, lanes] = jnp.full((8, LANES), far_bias, F32)


def _bias_tables(rel_bias):
    idx_w, idx_s, idx_c = _index_tables()
    width = GQA * LANES
    out_shape = (
        jax.ShapeDtypeStruct((N_KV, idx_w.shape[0], width), F32),
        jax.ShapeDtypeStruct((N_KV, idx_s.shape[0], width), F32),
        jax.ShapeDtypeStruct((N_KV, idx_c.shape[0], width), F32),
        jax.ShapeDtypeStruct((N_KV, 8, width), F32),
    )
    vmem = pl.BlockSpec(memory_space=pltpu.VMEM)
    return pl.pallas_call(
        _tables_kernel,
        out_shape=out_shape,
        in_specs=[pl.BlockSpec(memory_space=pltpu.SMEM), vmem, vmem, vmem],
        out_specs=(vmem, vmem, vmem, vmem),
        name="bias_tables",
    )(rel_bias, jnp.asarray(idx_w), jnp.asarray(idx_s), jnp.asarray(idx_c))


def _in_proj_kernel(x_ref, g1_ref, wtok_ref, wfeat_ref, ind_ref, kgain_ref, aug_ref, gq_ref, convw_ref, cgain_ref,
                    qT_ref, gT_ref, vsT_ref, vwT_ref, ks_ref, kw_ref, kcv_ref, convn_ref, zs_ref, carry_ref):
    tm = _IN_TM
    i = pl.program_id(0)

    @pl.when(i == 0)
    def _():
        carry_ref[...] = jnp.zeros(carry_ref.shape, F32)

    prev_tail = carry_ref[...]
    for part in range(_IN_SPLIT):
        rows = slice(part * tm, (part + 1) * tm)
        x = x_ref[rows, :]
        ms = jnp.mean(x * x, axis=-1, keepdims=True)
        h = (x * lax.rsqrt(ms + EPS) * g1_ref[...]).astype(BF16)
        tok = jnp.dot(h, wtok_ref[...], preferred_element_type=F32)
        feat = lax.dot_general(wfeat_ref[...], h, (((1,), (1,)), ((), ())),
                               preferred_element_type=F32)

        kcv_ref[rows, :] = tok[:, 0:2 * KV_W].astype(BF16)
        kk = tok[:, 2 * KV_W:4 * KV_W]
        sq = kk * kk
        sq_hi = sq.astype(BF16)
        sq_lo = (sq - sq_hi.astype(F32)).astype(BF16)
        ssq = jnp.dot(jnp.concatenate([sq_hi, sq_lo], axis=1), ind_ref[...],
                      preferred_element_type=F32)
        kn = kk * lax.rsqrt(ssq * (1.0 / HEAD_DIM) + EPS) * kgain_ref[...]
        ksl = kn[:, 0:KV_W]
        lane = lax.broadcasted_iota(I32, (tm, LANES), 1)
        aug = aug_ref[...]
        ks_ref[0, rows, :] = jnp.where(lane < HEAD_DIM, ksl, aug).astype(BF16)
        ks_ref[1, rows, :] = jnp.where(lane < HEAD_DIM, pltpu.roll(ksl, HEAD_DIM, 1), aug).astype(BF16)
        kw_ref[rows, :] = kn[:, KV_W:2 * KV_W].astype(BF16)

        c0 = 4 * KV_W
        cb = tok[:, c0:c0 + CONV_W]
        cc = tok[:, c0 + CONV_W:c0 + 2 * CONV_W]
        ch = tok[:, c0 + 2 * CONV_W:c0 + 3 * CONV_W]
        z = cc * ch
        zs_ref[part, 0:8, :] = prev_tail
        zs_ref[part, 8:8 + tm, :] = z
        z1 = zs_ref[part, 7:7 + tm, :]
        z2 = zs_ref[part, 6:6 + tm, :]
        w = convw_ref[...]
        y = w[0:1, :] * z2 + w[1:2, :] * z1 + w[2:3, :] * z
        prev_tail = z[tm - 8:tm, :]
        oc = cb * y
        msc = jnp.mean(oc * oc, axis=-1, keepdims=True)
        convn_ref[rows, :] = (oc * lax.rsqrt(msc + EPS) * cgain_ref[...]).astype(BF16)

        q = feat[0:ATTN_W].reshape(N_HEADS, HEAD_DIM, tm)
        qss = jnp.sum(q * q, axis=1, keepdims=True)
        qn = q * lax.rsqrt(qss * (1.0 / HEAD_DIM) + EPS) * gq_ref[...][None]
        qT_ref[:, rows] = (qn * (SCALE * LOG2E)).reshape(ATTN_W, tm).astype(BF16)
        ones_rows = (lax.broadcasted_iota(I32, (V_ROWS - HEAD_DIM, tm), 0) == 0).astype(BF16)
        for g in range(N_KV):
            r0 = ATTN_W + g * HEAD_DIM
            vsT_ref[g, 0:HEAD_DIM, rows] = feat[r0:r0 + HEAD_DIM].astype(BF16)
            vsT_ref[g, HEAD_DIM:V_ROWS, rows] = ones_rows
            r1 = ATTN_W + KV_W + g * HEAD_DIM
            vwT_ref[g, 0:HEAD_DIM, rows] = feat[r1:r1 + HEAD_DIM].astype(BF16)
            vwT_ref[g, HEAD_DIM:V_ROWS, rows] = ones_rows
        g0 = ATTN_W + 2 * KV_W
        gT_ref[:, rows] = jax.nn.sigmoid(feat[g0:g0 + N_BRANCH * N_HEADS])
    carry_ref[...] = prev_tail


_IN_SPLIT = 2
_IN_TM = 512


def _in_proj(x2, g1, wtok, wfeat, kgain, gq, convw, cgain):
    S = x2.shape[0]
    tm = _IN_TM
    tb = _IN_SPLIT * tm
    nt = S // tb
    ind = np.kron(np.eye(2 * N_KV, dtype=np.float32), np.ones((HEAD_DIM, HEAD_DIM), np.float32))
    ind = np.concatenate([ind, ind], axis=0)
    aug = np.zeros((tm, LANES), np.float32)
    blk = (np.arange(tm) // SEL_LEN) % BLOCKS_PER_CHUNK
    aug[np.arange(tm), HEAD_DIM + blk] = 1.0
    aug[:, HEAD_DIM + BLOCKS_PER_CHUNK:HEAD_DIM + BLOCKS_PER_CHUNK + 2] = 1.0
    const = lambda shape: pl.BlockSpec(shape, lambda i: (0,) * len(shape))
    out_shape = (
        jax.ShapeDtypeStruct((ATTN_W, S), BF16),
        jax.ShapeDtypeStruct((N_BRANCH * N_HEADS, S), F32),
        jax.ShapeDtypeStruct((N_KV, V_ROWS, S), BF16),
        jax.ShapeDtypeStruct((N_KV, V_ROWS, S), BF16),
        jax.ShapeDtypeStruct((N_KV, S, LANES), BF16),
        jax.ShapeDtypeStruct((S, LANES), BF16),
        jax.ShapeDtypeStruct((S, 2 * KV_W), BF16),
        jax.ShapeDtypeStruct((S, CONV_W), BF16),
    )
    out_specs = (
        pl.BlockSpec((ATTN_W, tb), lambda i: (0, i)),
        pl.BlockSpec((N_BRANCH * N_HEADS, tb), lambda i: (0, i)),
        pl.BlockSpec((N_KV, V_ROWS, tb), lambda i: (0, 0, i)),
        pl.BlockSpec((N_KV, V_ROWS, tb), lambda i: (0, 0, i)),
        pl.BlockSpec((N_KV, tb, LANES), lambda i: (0, i, 0)),
        pl.BlockSpec((tb, LANES), lambda i: (i, 0)),
        pl.BlockSpec((tb, 2 * KV_W), lambda i: (i, 0)),
        pl.BlockSpec((tb, CONV_W), lambda i: (i, 0)),
    )
    in_specs = [
        pl.BlockSpec((tb, D_MODEL), lambda i: (i, 0)),
        const((1, D_MODEL)),
        const(wtok.shape),
        const(wfeat.shape),
        const(ind.shape),
        const((1, 2 * KV_W)),
        const(aug.shape),
        const((HEAD_DIM, tm)),
        const((8, CONV_W)),
        const((1, CONV_W)),
    ]
    return pl.pallas_call(
        _in_proj_kernel,
        grid=(nt,),
        in_specs=in_specs,
        out_specs=out_specs,
        out_shape=out_shape,
        scratch_shapes=[pltpu.VMEM((_IN_SPLIT, tm + 8, CONV_W), F32), pltpu.VMEM((8, CONV_W), F32)],
        compiler_params=pltpu.CompilerParams(dimension_semantics=("arbitrary",), vmem_limit_bytes=VMEM_LIMIT),
        name="in_proj",
    )(x2, g1, wtok, wfeat, jnp.asarray(ind, BF16), kgain, jnp.asarray(aug), gq, convw, cgain)


def _compress_kernel(r_ref, wtop_ref, wbot_ref, posk_ref, posv_ref, w1k_ref, w1v_ref, w2k_ref, w2vT_ref, kgain_ref,
                     kc_ref, vcT_ref):
    nr = r_ref.shape[0]
    r = r_ref[...]
    u = jnp.dot(r, wtop_ref[...], preferred_element_type=F32)
    lo = jnp.dot(r, wbot_ref[...], preferred_element_type=F32)
    bias_k = jnp.dot(posk_ref[...], w1k_ref[...], preferred_element_type=F32)[0:1, :]
    bias_v = jnp.dot(posv_ref[...], w1v_ref[...], preferred_element_type=F32)[0:1, :]
    bias4 = jnp.concatenate([bias_k] * N_KV + [bias_v] * N_KV, axis=1)
    hid = u + pltpu.roll(lo, nr - 1, 0) + bias4
    act = jax.nn.gelu(hid)
    for g in range(N_KV):
        ak = act[:, g * CMP_HIDDEN:(g + 1) * CMP_HIDDEN].astype(BF16)
        av = act[:, (N_KV + g) * CMP_HIDDEN:(N_KV + g + 1) * CMP_HIDDEN].astype(BF16)
        kc = jnp.dot(ak, w2k_ref[...], preferred_element_type=F32)
        ssq = jnp.sum(kc * kc, axis=-1, keepdims=True)
        kc_ref[g] = (kc * lax.rsqrt(ssq * (1.0 / HEAD_DIM) + EPS) * kgain_ref[...]).astype(BF16)
        vcT_ref[g] = lax.dot_general(w2vT_ref[...], av, (((1,), (1,)), ((), ())),
                                     preferred_element_type=F32).astype(BF16)


def _compress(kcv, wtop, wbot, posk, posv, w1k, w1v, w2k, w2vT, kgain):
    S = kcv.shape[0]
    nr = S // CMP_STRIDE
    r = kcv.reshape(nr, CMP_STRIDE * 2 * KV_W)
    vmem = pl.BlockSpec(memory_space=pltpu.VMEM)
    return pl.pallas_call(
        _compress_kernel,
        out_shape=(jax.ShapeDtypeStruct((N_KV, nr, LANES), BF16),
                   jax.ShapeDtypeStruct((N_KV, HEAD_DIM, nr), BF16)),
        in_specs=[vmem] * 10,
        out_specs=(vmem, vmem),
        compiler_params=pltpu.CompilerParams(vmem_limit_bytes=VMEM_LIMIT),
        name="compress",
    )(r, wtop, wbot, posk, posv, w1k, w1v, w2k, w2vT, kgain)


def _tile4(row):
    return jnp.concatenate([row] * GQA, axis=1)


def _attn_kernel(qT_ref, gT_ref, ks_ref, vsT_ref, kw_ref, vwT_ref, kc_ref, vcT_ref,
                 tw_ref, ts_ref, tc_ref, b31_ref, gain_ref, out_ref,
                 sc_ref, sw_ref, imp_ref, amask_ref, qs_ref, sa_ref, sb_ref, qz_ref, qw_ref, fin_ref, ocmp_ref,
                 pk_ref):
    qw_refs = tuple(qw_ref.at[k] for k in range(FAR_UNROLL))
    ncr = kc_ref.shape[1]
    noct = amask_ref.shape[1] - 1
    nsel = noct * BLOCKS_PER_CHUNK
    width = GQA * LANES
    c = pl.program_id(0)
    t0 = c * Q_BLOCK
    tl = lax.broadcasted_iota(I32, (1, LANES), 1)
    cur = 2 * c + (tl >= SEL_LEN).astype(I32)
    zb = jnp.maximum(2 * c - 2, 0)
    nfar = (zb + BLOCKS_PER_CHUNK - 1) // BLOCKS_PER_CHUNK
    zs = pl.multiple_of(zb * SEL_LEN, Q_BLOCK)
    ts_off = pl.multiple_of(zs - (t0 - Q_BLOCK), Q_BLOCK)
    zeros_q = jnp.zeros((HEAD_DIM, width), BF16)

    col_ok = _tile4(t0 + tl) >= CMP_LEN - 1
    j = lax.broadcasted_iota(I32, (nsel, LANES), 0)
    forced = (j == 0) | (j == cur) | (j == cur - 1)
    @pl.when(c == 0)
    def _():
        imp_ref[...] = jnp.zeros(imp_ref.shape, F32)


    ws = pl.multiple_of(jnp.maximum(t0 - WINDOW, 0), Q_BLOCK)
    tw_off = pl.multiple_of(ws - (t0 - WINDOW), Q_BLOCK)
    nw = WINDOW + Q_BLOCK
    lo = pl.multiple_of(jnp.clip((8 * c - 16) // 16 * 16, 0, ncr - TC_WIN), 16)
    tc_off = pl.multiple_of(lo - (8 * c - 16) + TC_LEAD, 8)
    qgs = [jnp.concatenate([qT_ref[(GQA * g + r) * HEAD_DIM:(GQA * g + r + 1) * HEAD_DIM, :]
                            for r in range(GQA)], axis=1) for g in range(N_KV)]

    for g in range(N_KV):
        qw = jnp.concatenate([qgs[g], zeros_q] if g == 0 else [zeros_q, qgs[g]], axis=0)
        sw_ref[g] = (jnp.dot(kw_ref[pl.ds(ws, nw), :], qw, preferred_element_type=F32)
                     + tw_ref[g, pl.ds(tw_off, nw), :])

    def compressed(rows):
        for g in range(N_KV):
            qc = jnp.concatenate([qgs[g], zeros_q], axis=0)
            s = jnp.dot(kc_ref[g, 0:rows, :], qc, preferred_element_type=F32)
            row = lax.broadcasted_iota(I32, (rows, width), 0)
            sc_ref[g, 0:rows, :] = jnp.where(row < lo, s, NEG)
            s_loc = jnp.dot(kc_ref[g, pl.ds(lo, TC_WIN), :], qc, preferred_element_type=F32)
            sc_ref[g, pl.ds(lo, TC_WIN), :] = s_loc + tc_ref[g, pl.ds(tc_off, TC_WIN), :]
        for g in range(N_KV):
            s = sc_ref[g, 0:rows, :]
            m = jnp.max(s, axis=0, keepdims=True)
            e = jnp.exp2(s - m)
            l = jnp.sum(e, axis=0, keepdims=True)
            p = e * jnp.where(col_ok, 1.0 / jnp.maximum(l, 1e-30), 0.0)
            ocmp_ref[g] = jnp.dot(vcT_ref[g, :, 0:rows], p.astype(BF16), preferred_element_type=F32)
            imp_ref[g, 8:8 + rows, :] = (
                (p[:, 0:LANES] + p[:, LANES:2 * LANES]) + p[:, 2 * LANES:3 * LANES] + p[:, 3 * LANES:4 * LANES])

    sizes = list(range(CMP_STEP, ncr, CMP_STEP)) + [ncr]
    variant = jnp.minimum((lo + TC_WIN + CMP_STEP - 1) // CMP_STEP, len(sizes)) - 1
    for v, rows in enumerate(sizes):
        pl.when(variant == v)(functools.partial(compressed, rows))

    o_cmp, scores0 = [], []
    for g in range(N_KV):
        o_cmp.append(ocmp_ref[g])
        isel = jnp.zeros((nsel, LANES), F32)
        for off in (0, -1, 1, 0, 2, 1, 3, 2):
            isel = isel + imp_ref[g, pl.ds(8 + off, nsel, stride=4), :]
        score = jnp.where(forced, -jnp.inf, isel)
        scores0.append(jnp.where(j > cur, NEG, score))

    o_win = []
    for g in range(N_KV):
        sw = sw_ref[g]
        mw = jnp.max(sw, axis=0, keepdims=True)
        pw = jnp.exp2(sw - mw)
        accw = jnp.dot(vwT_ref[g, :, pl.ds(ws, nw)], pw.astype(BF16), preferred_element_type=F32)
        o_win.append(accw[0:HEAD_DIM] * (1.0 / jnp.maximum(accw[HEAD_DIM:HEAD_DIM + 1], 1e-30)))

    n_picks = SEL_TOPK - 3
    fast = list(scores0)
    for _ in range(n_picks):
        for g in range(N_KV):
            mx = jnp.max(fast[g], axis=0, keepdims=True)
            fast[g] = jnp.where(fast[g] == mx, -jnp.inf, fast[g])
    eligible = (j <= cur) & jnp.logical_not(forced)
    n_eligible = cur + 1 - (1 + (cur >= 1).astype(I32) + (cur >= 2).astype(I32))
    expect = jnp.minimum(n_picks, n_eligible).astype(F32)
    wrong = jnp.zeros((1, LANES), F32)
    for g in range(N_KV):
        pk_ref[g] = fast[g]
        got = jnp.sum(jnp.where((fast[g] == -jnp.inf) & eligible, 1.0, 0.0), axis=0, keepdims=True)
        wrong = jnp.maximum(wrong, jnp.abs(got - expect))

    @pl.when(jnp.max(wrong) > 0.0)
    def _():
        def pick(_, score):
            mx = jnp.max(score, axis=0, keepdims=True)
            first = jnp.min(jnp.where(score == mx, j, nsel), axis=0, keepdims=True)
            return jnp.where(j == first, -jnp.inf, score)

        for g in range(N_KV):
            pk_ref[g] = lax.fori_loop(0, n_picks, pick, scores0[g])

    picks = [pk_ref[g] for g in range(N_KV)]

    m_init, acc_init, gates = [], [], []
    for g in range(N_KV):
        qg = qgs[g]
        b31row = b31_ref[g, 0:1, :]
        chosen = (picks[g] == -jnp.inf) & (j <= cur)
        a_all = _tile4(jnp.where(chosen, 0.0, -MASK_BIG)).reshape(noct, BLOCKS_PER_CHUNK, width)
        b_hi = b31row.astype(BF16).astype(F32)
        erow = lax.broadcasted_iota(I32, (8, width), 0)
        extra = jnp.where(erow == 0, b_hi, jnp.where(erow == 1, b31row - b_hi, 0.0))
        amask_ref[g, 0:noct] = jnp.concatenate(
            [a_all, jnp.broadcast_to(extra[None], (noct, 8, width))], axis=1).astype(BF16)
        far_last = jnp.maximum(nfar - 1, 0)
        orow = lax.broadcasted_iota(I32, (MASK_ROWS, width), 0)
        keep = (orow >= BLOCKS_PER_CHUNK) | (orow + BLOCKS_PER_CHUNK * far_last < zb)
        amask_ref[g, noct] = jnp.where(keep, amask_ref[g, far_last].astype(F32), -MASK_BIG).astype(BF16)
        for q_ref in (qs_ref,) + qw_refs:
            q_ref[g, 0:HEAD_DIM, :] = qg
            q_ref[g, HEAD_DIM + MASK_ROWS:LANES, :] = jnp.zeros((LANES - HEAD_DIM - MASK_ROWS, width), BF16)

        halves = []
        for h in range(2):
            octet = amask_ref[g, (zb + 2 * h) // BLOCKS_PER_CHUNK].astype(F32)
            qz_ref[2 * g + h, HEAD_DIM:HEAD_DIM + MASK_ROWS, :] = jnp.where(
                orow < BLOCKS_PER_CHUNK, octet, 0.0).astype(BF16)
            qz_ref[2 * g + h, 0:HEAD_DIM, :] = qg
            qz_ref[2 * g + h, HEAD_DIM + MASK_ROWS:LANES, :] = jnp.zeros((LANES - HEAD_DIM - MASK_ROWS, width), BF16)
            sh = jnp.dot(ks_ref[g, pl.ds(zs + h * Q_BLOCK, Q_BLOCK), :], qz_ref[2 * g + h], preferred_element_type=F32)
            halves.append(sh + ts_ref[g, pl.ds(ts_off + h * Q_BLOCK, Q_BLOCK), :])
        sz = jnp.concatenate(halves, axis=0)
        mz = jnp.max(sz, axis=0, keepdims=True)
        pz = jnp.exp2(sz - mz)
        m_init.append(mz)
        acc_init.append(jnp.dot(vsT_ref[g, :, pl.ds(zs, 2 * Q_BLOCK)], pz.astype(BF16), preferred_element_type=F32))
        gates.append([jnp.concatenate([gT_ref[br * N_HEADS + GQA * g + r:br * N_HEADS + GQA * g + r + 1, :]
                                       for r in range(GQA)], axis=1) for br in range(N_BRANCH)])

    nsub = 1
    sub = CHUNK // nsub

    def score(g, u, buf_ref, h, q_ref):
        k0 = pl.multiple_of(jnp.minimum(u, noct - 1) * CHUNK + h * sub, sub)
        s = jnp.dot(ks_ref[g, pl.ds(k0, sub), :], q_ref[g], preferred_element_type=F32)
        buf_ref[g, h * sub:(h + 1) * sub, :] = s
        return jnp.max(s, axis=0, keepdims=True)

    def set_mask_rows(g, u, q_ref):
        oct_id = jnp.where(u < nfar - 1, u, noct)
        q_ref[g, HEAD_DIM:HEAD_DIM + MASK_ROWS, :] = amask_ref[g, oct_id]

    def accumulate(g, u, buf_ref, h, mn, acc):
        k0 = pl.multiple_of(jnp.minimum(u, noct - 1) * CHUNK + h * sub, sub)
        p = jnp.exp2(buf_ref[g, h * sub:(h + 1) * sub, :] - mn)
        return acc + jnp.dot(vsT_ref[g, :, pl.ds(k0, sub)], p.astype(BF16), preferred_element_type=F32)

    def step(u, cur_ref, nxt_ref, q_ref, state, cms):
        out_state, out_cms = [], []
        for g in range(N_KV):
            m, acc = state[2 * g], state[2 * g + 1]
            mn = jnp.maximum(m, cms[g])
            acc = jnp.exp2(m - mn) * acc
            set_mask_rows(g, u + 1, q_ref)
            cm = None
            for h in range(nsub):
                ch = score(g, u + 1, nxt_ref, h, q_ref)
                cm = ch if cm is None else jnp.maximum(cm, ch)
                acc = accumulate(g, u, cur_ref, h, mn, acc)
            out_cms.append(cm)
            out_state += [mn, acc]
        return out_state, out_cms

    bufs = (sa_ref, sb_ref)

    def far(i, carry):
        state, cms = list(carry[0:4]), list(carry[4:6])
        for k in range(FAR_UNROLL):
            state, cms = step(FAR_UNROLL * i + k, bufs[k % 2], bufs[(k + 1) % 2], qw_refs[k], state, cms)
        return tuple(state + cms)

    first = []
    for g in range(N_KV):
        set_mask_rows(g, 0, qs_ref)
        cm = None
        for h in range(nsub):
            ch = score(g, 0, sa_ref, h, qs_ref)
            cm = ch if cm is None else jnp.maximum(cm, ch)
        first.append(cm)
    nchunks = jnp.maximum(nfar, 1)
    nloops = (nchunks - 1) // FAR_UNROLL
    carry = lax.fori_loop(0, nloops, far, (m_init[0], acc_init[0], m_init[1], acc_init[1], first[0], first[1]))
    u_last = FAR_UNROLL * nloops

    def finish(left):
        for g in range(N_KV):
            m, acc, cm = carry[2 * g], carry[2 * g + 1], carry[4 + g]
            for k in range(left):
                mn = jnp.maximum(m, cm)
                acc = jnp.exp2(m - mn) * acc
                if k + 1 < left:
                    set_mask_rows(g, u_last + k + 1, qw_refs[k])
                    cm = score(g, u_last + k + 1, bufs[(k + 1) % 2], 0, qw_refs[k])
                acc = accumulate(g, u_last + k, bufs[k % 2], 0, mn, acc)
                m = mn
            fin_ref[g] = acc

    for left in range(1, FAR_UNROLL + 1):
        pl.when(nchunks - u_last == left)(functools.partial(finish, left))
    carry = [None, fin_ref[0], None, fin_ref[1]]

    o_all, ssq = [], jnp.zeros((1, LANES), F32)
    for g in range(N_KV):
        acc = carry[2 * g + 1]
        o_sel = acc[0:HEAD_DIM] * (1.0 / jnp.maximum(acc[HEAD_DIM:HEAD_DIM + 1], 1e-30))
        o = gates[g][0] * o_cmp[g] + gates[g][1] * o_sel + gates[g][2] * o_win[g]
        o_all.append(o)
        cs = jnp.sum(o * o, axis=0, keepdims=True)
        ssq = ssq + ((cs[:, 0:LANES] + cs[:, LANES:2 * LANES]) + (cs[:, 2 * LANES:3 * LANES] + cs[:, 3 * LANES:]))
    inv = _tile4(lax.rsqrt(ssq * (1.0 / ATTN_W) + EPS))
    for g in range(N_KV):
        on = o_all[g] * inv * gain_ref[g]
        for k in range(GQA // 2):
            pair = jnp.concatenate([on[:, (2 * k) * LANES:(2 * k + 1) * LANES],
                                    on[:, (2 * k + 1) * LANES:(2 * k + 2) * LANES]], axis=0)
            col = (GQA * g + 2 * k) * HEAD_DIM
            out_ref[:, col:col + 2 * HEAD_DIM] = pair.T.astype(BF16)


def _attention(qT, gT, ks, vsT, kw, vwT, kc, vcT, tables, gain_b):
    S = qT.shape[1]
    nq = S // Q_BLOCK
    ncr = kc.shape[1]
    nsel = S // SEL_LEN
    noct = S // CHUNK
    width = GQA * LANES
    tw, ts, tc, b31 = tables
    vmem = pl.BlockSpec(memory_space=pltpu.VMEM)
    in_specs = [
        pl.BlockSpec((ATTN_W, Q_BLOCK), lambda c: (0, c)),
        pl.BlockSpec((N_BRANCH * N_HEADS, Q_BLOCK), lambda c: (0, c)),
    ] + [vmem] * 11
    return pl.pallas_call(
        _attn_kernel,
        grid=(nq,),
        in_specs=in_specs,
        out_specs=pl.BlockSpec((Q_BLOCK, ATTN_W), lambda c: (c, 0)),
        out_shape=jax.ShapeDtypeStruct((S, ATTN_W), BF16),
        scratch_shapes=[
            pltpu.VMEM((N_KV, ncr, width), F32),
            pltpu.VMEM((N_KV, WINDOW + Q_BLOCK, width), F32),
            pltpu.VMEM((N_KV, ncr + 16, LANES), F32),
            pltpu.VMEM((N_KV, noct + 1, MASK_ROWS, width), BF16),
            pltpu.VMEM((N_KV, LANES, width), BF16),
            pltpu.VMEM((N_KV, CHUNK, width), F32),
            pltpu.VMEM((N_KV, CHUNK, width), F32),
            pltpu.VMEM((2 * N_KV, LANES, width), BF16),
            pltpu.VMEM((FAR_UNROLL, N_KV, LANES, width), BF16),
            pltpu.VMEM((N_KV, V_ROWS, width), F32),
            pltpu.VMEM((N_KV, HEAD_DIM, width), F32),
            pltpu.VMEM((N_KV, nsel, LANES), F32),
        ],
        compiler_params=pltpu.CompilerParams(dimension_semantics=("arbitrary",), vmem_limit_bytes=VMEM_LIMIT),
        name="nsa_attention",
    )(qT, gT, ks, vsT, kw, vwT, kc, vcT, tw, ts, tc, b31, gain_b)


def _mix_and_norm(x_ref, attn_ref, conv_ref, wout_ref, g2_ref, rows=slice(None)):
    x1 = x_ref[rows, :] + jnp.dot(attn_ref[rows, :], wout_ref[0:ATTN_W, :], preferred_element_type=F32) \
        + jnp.dot(conv_ref[rows, :], wout_ref[ATTN_W:, :], preferred_element_type=F32)
    ms = jnp.mean(x1 * x1, axis=-1, keepdims=True)
    h2 = x1 * lax.rsqrt(ms + EPS) * g2_ref[...]
    return x1, h2


def _ffn_kernel(x_ref, attn_ref, conv_ref, wout_ref, g2_ref, wg_ref, wu_ref, wd_ref, out_ref):
    x1, h2 = _mix_and_norm(x_ref, attn_ref, conv_ref, wout_ref, g2_ref)
    h2 = h2.astype(BF16)
    a = jnp.dot(h2, wg_ref[...], preferred_element_type=F32)
    u = jnp.dot(h2, wu_ref[...], preferred_element_type=F32)
    y = (a * jax.nn.sigmoid(a) * u).astype(BF16)
    out_ref[...] = x1 + jnp.dot(y, wd_ref[...], preferred_element_type=F32)


_FFN_TM = 512


def _outproj_ffn(x2, attn_n, conv_n, wout, g2, wg, wu, wd):
    S = x2.shape[0]
    tm = _FFN_TM
    resident = pl.BlockSpec(memory_space=pltpu.VMEM)
    return pl.pallas_call(
        _ffn_kernel,
        grid=(S // tm,),
        in_specs=[
            pl.BlockSpec((tm, D_MODEL), lambda i: (i, 0)),
            pl.BlockSpec((tm, ATTN_W), lambda i: (i, 0)),
            pl.BlockSpec((tm, CONV_W), lambda i: (i, 0)),
            resident, resident, resident, resident, resident,
        ],
        out_specs=pl.BlockSpec((tm, D_MODEL), lambda i: (i, 0)),
        out_shape=jax.ShapeDtypeStruct((S, D_MODEL), F32),
        compiler_params=pltpu.CompilerParams(dimension_semantics=("arbitrary",), vmem_limit_bytes=VMEM_LIMIT),
        name="outproj_ffn",
    )(x2, attn_n, conv_n, wout, g2, wg, wu, wd)


TOKEN_TILE = (D_MODEL // LANES, LANES)


def _to_token_tiles(ref, rows):
    x = jnp.stack([rows[:, k * LANES:(k + 1) * LANES] for k in range(TOKEN_TILE[0])], axis=0)
    ref[...] = pltpu.einshape('ktl->tkl', x)


def _from_token_tiles(ref):
    x = pltpu.einshape('tkl->ktl', ref[...])
    return jnp.concatenate([x[k] for k in range(TOKEN_TILE[0])], axis=1)


_ROUTER_SPLIT = 2


def _router_kernel(x_ref, attn_ref, conv_ref, wout_ref, g2_ref, rw_ref, rb_ref, tri_ref,
                   x1_ref, h2_ref, route_ref, cnt_ref, run_ref):
    tm = _FFN_TM
    i = pl.program_id(0)

    @pl.when(i == 0)
    def _():
        run_ref[...] = jnp.zeros(run_ref.shape, F32)

    running = run_ref[0:1, :]
    for part in range(_ROUTER_SPLIT):
        rows = slice(part * tm, (part + 1) * tm)
        x1, h2 = _mix_and_norm(x_ref, attn_ref, conv_ref, wout_ref, g2_ref, rows)
        x1_ref[rows, :] = x1
        h2b = h2.astype(BF16)
        _to_token_tiles(h2_ref.at[rows], h2b.astype(F32))
        lane = lax.broadcasted_iota(I32, (tm, LANES), 1)
        logits = jnp.dot(h2b, rw_ref[...], preferred_element_type=F32) + rb_ref[...]
        logits = jnp.where(lane < N_EXPERTS, logits, -jnp.inf)
        m1 = jnp.max(logits, axis=-1, keepdims=True)
        i1 = jnp.min(jnp.where(logits == m1, lane, LANES), axis=-1, keepdims=True)
        rest = jnp.where(lane == i1, -jnp.inf, logits)
        m2 = jnp.max(rest, axis=-1, keepdims=True)
        i2 = jnp.min(jnp.where(rest == m2, lane, LANES), axis=-1, keepdims=True)
        e2 = jnp.exp(m2 - m1)
        den = 1.0 + e2
        oh1 = (lane == i1).astype(F32)
        oh2 = (lane == i2).astype(F32)
        both = oh1 + oh2
        before = running + jnp.dot(tri_ref[...], both.astype(BF16), preferred_element_type=F32)
        rank1 = jnp.sum(before * oh1, axis=-1, keepdims=True)
        rank2 = jnp.sum(before * oh2, axis=-1, keepdims=True)
        fields = (i1.astype(F32), i2.astype(F32), rank1, rank2, 1.0 / den, e2 / den)
        route = jnp.zeros((tm, LANES), F32)
        for k, v in enumerate(fields):
            route = jnp.where(lane == k, v, route)
        route_ref[rows, :] = route
        running = running + jnp.sum(both, axis=0, keepdims=True)
    run_ref[...] = jnp.broadcast_to(running, run_ref.shape)
    cnt_ref[...] = run_ref[...]


def _outproj_router(x2, attn_n, conv_n, wout, g2, rw, rb):
    S = x2.shape[0]
    tm = _FFN_TM
    tb = _ROUTER_SPLIT * tm
    tri = np.tril(np.ones((tm, tm), np.float32), -1)
    const = lambda shape: pl.BlockSpec(shape, lambda i: (0,) * len(shape))
    return pl.pallas_call(
        _router_kernel,
        grid=(S // tb,),
        in_specs=[
            pl.BlockSpec((tb, D_MODEL), lambda i: (i, 0)),
            pl.BlockSpec((tb, ATTN_W), lambda i: (i, 0)),
            pl.BlockSpec((tb, CONV_W), lambda i: (i, 0)),
            const(wout.shape), const((1, D_MODEL)), const(rw.shape), const((1, LANES)), const((tm, tm)),
        ],
        out_specs=(pl.BlockSpec((tb, D_MODEL), lambda i: (i, 0)),
                   pl.BlockSpec((tb,) + TOKEN_TILE, lambda i: (i, 0, 0)),
                   pl.BlockSpec((tb, LANES), lambda i: (i, 0)),
                   const((8, LANES))),
        out_shape=(jax.ShapeDtypeStruct((S, D_MODEL), F32),
                   jax.ShapeDtypeStruct((S,) + TOKEN_TILE, F32),
                   jax.ShapeDtypeStruct((S, LANES), F32),
                   jax.ShapeDtypeStruct((8, LANES), F32)),
        scratch_shapes=[pltpu.VMEM((8, LANES), F32)],
        compiler_params=pltpu.CompilerParams(dimension_semantics=("arbitrary",), vmem_limit_bytes=VMEM_LIMIT),
        name="outproj_router",
    )(x2, attn_n, conv_n, wout, g2, rw, rb, jnp.asarray(tri, BF16))


_ROW_TM = 512
_EXP_TM = 512
_DMA_UNROLL = 8


def _row_copy(src_ref, src_row, dst_ref, dst_row, sem):
    return pltpu.make_async_copy(src_ref.at[src_row], dst_ref.at[dst_row], sem)


def _dispatch_kernel(pos_ref, ztile_ref, h_ref, xs_ref, hbuf_ref, zbuf_ref, sem, zsem):
    tm = h_ref.shape[0]
    i = pl.program_id(0)
    slot = i % 2

    @pl.when(i == 0)
    def _():
        zbuf_ref[...] = jnp.zeros(zbuf_ref.shape, F32)
        for z in range(ztile_ref.shape[0]):
            clear = pltpu.make_async_copy(zbuf_ref, xs_ref.at[pl.ds(ztile_ref[z] * _EXP_TM, _EXP_TM)], zsem)
            clear.start()
            clear.wait()

    hbuf_ref[slot] = h_ref[...]

    def issue(t, carry):
        for k in range(2):
            _row_copy(hbuf_ref.at[slot], t, xs_ref, pos_ref[0, 0, 2 * t + k], sem.at[slot]).start(priority=k)
        return carry

    lax.fori_loop(0, tm, issue, 0, unroll=_DMA_UNROLL)

    def drain(which):
        def body(t, carry):
            for k in range(2):
                _row_copy(hbuf_ref.at[which], 0, xs_ref, 0, sem.at[which]).wait()
            return carry
        lax.fori_loop(0, tm, body, 0, unroll=_DMA_UNROLL)

    @pl.when(i > 0)
    def _():
        drain(1 - slot)

    @pl.when(i == pl.num_programs(0) - 1)
    def _():
        drain(slot)


def _dispatch(pos3, pad_tiles, h2, n_rows):
    S = h2.shape[0]
    tm = _ROW_TM
    return pl.pallas_call(
        _dispatch_kernel,
        grid=(S // tm,),
        in_specs=[
            pl.BlockSpec((1, 1, 2 * tm), lambda i: (i, 0, 0), memory_space=pltpu.SMEM),
            pl.BlockSpec(memory_space=pltpu.SMEM),
            pl.BlockSpec((tm,) + TOKEN_TILE, lambda i: (i, 0, 0)),
        ],
        out_specs=pl.BlockSpec(memory_space=pl.ANY),
        out_shape=jax.ShapeDtypeStruct((n_rows,) + TOKEN_TILE, F32),
        scratch_shapes=[pltpu.VMEM((2, tm) + TOKEN_TILE, F32), pltpu.VMEM((_EXP_TM,) + TOKEN_TILE, F32),
                        pltpu.SemaphoreType.DMA((2,)), pltpu.SemaphoreType.DMA],
        compiler_params=pltpu.CompilerParams(dimension_semantics=("arbitrary",), vmem_limit_bytes=VMEM_LIMIT),
        name="moe_dispatch",
    )(pos3, pad_tiles, h2)


def _experts_kernel(te_ref, tb_ref, nt_ref, xs_ref, wg_ref, wu_ref, wd_ref, ys_ref):
    i = pl.program_id(0)

    @pl.when(i < nt_ref[0])
    def _():
        x = _from_token_tiles(xs_ref).astype(BF16)
        a = jnp.dot(x, wg_ref[0], preferred_element_type=F32)
        u = jnp.dot(x, wu_ref[0], preferred_element_type=F32)
        y = (a * jax.nn.sigmoid(a) * u).astype(BF16)
        _to_token_tiles(ys_ref, jnp.dot(y, wd_ref[0], preferred_element_type=F32))

    @pl.when(i >= nt_ref[0])
    def _():
        ys_ref[...] = jnp.zeros(ys_ref.shape, F32)


def _experts(tile_e, tile_b, n_tiles, xs, wg, wu, wd):
    n_rows = xs.shape[0]
    tm = _EXP_TM
    weights = lambda shape: pl.BlockSpec(shape, lambda i, te, tb, nt: (te[i], 0, 0), pipeline_mode=pl.Buffered(1))
    grid_spec = pltpu.PrefetchScalarGridSpec(
        num_scalar_prefetch=3,
        grid=(n_rows // tm,),
        in_specs=[
            pl.BlockSpec((tm,) + TOKEN_TILE, lambda i, te, tb, nt: (tb[i], 0, 0)),
            weights((1, D_MODEL, D_FF)), weights((1, D_MODEL, D_FF)), weights((1, D_FF, D_MODEL)),
        ],
        out_specs=pl.BlockSpec((tm,) + TOKEN_TILE, lambda i, te, tb, nt: (tb[i], 0, 0)),
    )
    return pl.pallas_call(
        _experts_kernel,
        grid_spec=grid_spec,
        out_shape=jax.ShapeDtypeStruct((n_rows,) + TOKEN_TILE, F32),
        compiler_params=pltpu.CompilerParams(dimension_semantics=("arbitrary",), vmem_limit_bytes=VMEM_LIMIT),
        name="moe_experts",
    )(tile_e, tile_b, n_tiles, xs, wg, wu, wd)


def _combine_kernel(pos_ref, nxt_ref, x1_ref, route_ref, ys_ref, out_ref, y1_ref, y2_ref, sem):
    tm = x1_ref.shape[0]
    i = pl.program_id(0)
    slot = i % 2

    def issue(src_pos_ref, which):
        def body(t, carry):
            _row_copy(ys_ref, src_pos_ref[0, 0, 2 * t], y1_ref.at[which], t, sem.at[which]).start(priority=0)
            _row_copy(ys_ref, src_pos_ref[0, 0, 2 * t + 1], y2_ref.at[which], t, sem.at[which]).start(priority=1)
            return carry
        lax.fori_loop(0, tm, body, 0, unroll=_DMA_UNROLL)

    @pl.when(i == 0)
    def _():
        issue(pos_ref, slot)

    @pl.when(i + 1 < pl.num_programs(0))
    def _():
        issue(nxt_ref, 1 - slot)

    def drain(t, carry):
        _row_copy(ys_ref, 0, y1_ref.at[slot], 0, sem.at[slot]).wait()
        _row_copy(ys_ref, 0, y2_ref.at[slot], 0, sem.at[slot]).wait()
        return carry

    lax.fori_loop(0, tm, drain, 0, unroll=_DMA_UNROLL)
    lane = lax.broadcasted_iota(I32, (tm, LANES), 1)
    route = route_ref[...]
    w1 = jnp.sum(jnp.where(lane == 4, route, 0.0), axis=-1, keepdims=True)
    w2 = jnp.sum(jnp.where(lane == 5, route, 0.0), axis=-1, keepdims=True)
    out_ref[...] = x1_ref[...] + (_from_token_tiles(y1_ref.at[slot]) * w1 + _from_token_tiles(y2_ref.at[slot]) * w2)


def _combine(pos3, x1, route, ys):
    S = x1.shape[0]
    tm = _ROW_TM
    last = S // tm - 1
    return pl.pallas_call(
        _combine_kernel,
        grid=(S // tm,),
        in_specs=[
            pl.BlockSpec((1, 1, 2 * tm), lambda i: (i, 0, 0), memory_space=pltpu.SMEM),
            pl.BlockSpec((1, 1, 2 * tm), lambda i: (jnp.minimum(i + 1, last), 0, 0), memory_space=pltpu.SMEM),
            pl.BlockSpec((tm, D_MODEL), lambda i: (i, 0)),
            pl.BlockSpec((tm, LANES), lambda i: (i, 0)),
            pl.BlockSpec(memory_space=pl.ANY),
        ],
        out_specs=pl.BlockSpec((tm, D_MODEL), lambda i: (i, 0)),
        out_shape=jax.ShapeDtypeStruct((S, D_MODEL), F32),
        scratch_shapes=[pltpu.VMEM((2, tm) + TOKEN_TILE, F32), pltpu.VMEM((2, tm) + TOKEN_TILE, F32),
                        pltpu.SemaphoreType.DMA((2,))],
        compiler_params=pltpu.CompilerParams(dimension_semantics=("arbitrary",), vmem_limit_bytes=VMEM_LIMIT),
        name="moe_combine",
    )(pos3, pos3, x1, route, ys)


def _moe(h2, x1, route, counts, wg, wu, wd):
    S = h2.shape[0]
    tm = _EXP_TM
    n_tiles_max = 2 * S // tm + N_EXPERTS
    cnt = counts[0, 0:N_EXPERTS].astype(I32)
    tiles = (cnt + tm - 1) // tm
    first = jnp.cumsum(tiles) - tiles
    n_tiles = jnp.sum(tiles)
    idx = jnp.arange(n_tiles_max, dtype=I32)
    last = jnp.minimum(idx, n_tiles - 1)
    tile_e = (jnp.sum(last[:, None] >= first[None, :], axis=1) - 1).astype(I32)
    eid = route[:, 0:2].astype(I32)
    pos = first[eid] * tm + route[:, 2:4].astype(I32)
    pos3 = pos.reshape(S // _ROW_TM, 1, 2 * _ROW_TM)
    pad_tiles = jnp.concatenate([jnp.maximum(first + tiles - 1, 0),
                                 jnp.minimum(n_tiles + jnp.arange(N_EXPERTS, dtype=I32), n_tiles_max - 1)])
    xs = _dispatch(pos3, pad_tiles.astype(I32), h2, n_tiles_max * tm)
    ys = _experts(tile_e, idx, n_tiles.reshape(1), xs, wg, wu, wd)
    return _combine(pos3, x1, route, ys)


def _split_w_in(w):
    o = np.cumsum([0, ATTN_W] + [KV_W] * 6 + [N_BRANCH * N_HEADS] + [CONV_W] * 3)
    q, kc, vc, ksl, vsl, kwn, vwn, gts, cb, cc, ch = (w[:, o[i]:o[i + 1]] for i in range(11))
    perm = np.array([h * N_BRANCH + br for br in range(N_BRANCH) for h in range(N_HEADS)])
    wtok = jnp.concatenate([kc, vc, ksl, kwn, cb, cc, ch], axis=1).astype(BF16)
    feat = jnp.concatenate([q, vsl, vwn, gts[:, perm], jnp.zeros((D_MODEL, 8), w.dtype)], axis=1)
    return wtok, feat.T.astype(BF16)


def _expand_cmp_w1(w1k, w1v):
    kinds = jnp.stack([w1k, w1v]).reshape(2, CMP_LEN, HEAD_DIM, CMP_HIDDEN)
    per_col = jnp.repeat(kinds, N_KV, axis=0)
    eye = jnp.eye(2 * N_KV, dtype=F32)
    out = []
    for l0 in (0, CMP_STRIDE):
        w = per_col[:, l0:l0 + CMP_STRIDE].transpose(1, 0, 2, 3)
        blk = w[:, :, :, None, :] * eye[None, :, None, :, None]
        out.append(blk.reshape(CMP_STRIDE * 2 * KV_W, 2 * N_KV * CMP_HIDDEN).astype(BF16))
    return out


def kernel(x, rel_bias, norm1, w_in, q_norm, k_norm, cmp_pos_k, cmp_pos_v, cmp_k_w1, cmp_k_w2, cmp_v_w1, cmp_v_w2,
           conv_w, attn_out_norm, conv_out_norm, w_out, norm2, ffn_w_gate, ffn_w_up, ffn_w_down, router_w, router_b,
           moe_w_gate, moe_w_up, moe_w_down):
    B, S, _ = x.shape
    assert B == 1 and S % CHUNK == 0 and S >= WINDOW + Q_BLOCK
    depth = norm1.shape[0]
    x2 = x.reshape(S, D_MODEL)
    tables = _bias_tables(rel_bias)
    for layer in range(depth):
        wtok, wfeat = _split_w_in(w_in[layer])
        kgain = jnp.concatenate([jnp.tile(k_norm[layer, 1], N_KV), jnp.tile(k_norm[layer, 2], N_KV)])[None, :]
        gq = jnp.broadcast_to(q_norm[layer][:, None], (HEAD_DIM, _IN_TM))
        convw = jnp.pad(conv_w[layer], ((0, 8 - CONV_K), (0, 0)))
        qT, gT, vsT, vwT, ks, kw, kcv, conv_n = _in_proj(
            x2, norm1[layer][None, :], wtok, wfeat, kgain, gq, convw, conv_out_norm[layer][None, :])

        wtop, wbot = _expand_cmp_w1(cmp_k_w1[layer], cmp_v_w1[layer])
        posk = jnp.broadcast_to(cmp_pos_k[layer].reshape(1, -1), (8, CMP_LEN * HEAD_DIM))
        posv = jnp.broadcast_to(cmp_pos_v[layer].reshape(1, -1), (8, CMP_LEN * HEAD_DIM))
        w2k = jnp.pad(cmp_k_w2[layer], ((0, 0), (0, LANES - HEAD_DIM))).astype(BF16)
        w2vT = cmp_v_w2[layer].T.astype(BF16)
        kcgain = jnp.pad(k_norm[layer, 0], (0, LANES - HEAD_DIM))[None, :]
        kc, vcT = _compress(kcv, wtop, wbot, posk, posv, cmp_k_w1[layer], cmp_v_w1[layer], w2k, w2vT, kcgain)

        gain_b = jnp.broadcast_to(attn_out_norm[layer].reshape(N_KV, GQA, HEAD_DIM).transpose(0, 2, 1)[:, :, :, None],
                                  (N_KV, HEAD_DIM, GQA, LANES)).reshape(N_KV, HEAD_DIM, GQA * LANES)
        attn_n = _attention(qT, gT, ks, vsT, kw, vwT, kc, vcT, tables, gain_b)

        wout = w_out[layer].astype(BF16)
        g2 = norm2[layer][None, :]
        i = layer // 2
        if layer % 2 == 0:
            x2 = _outproj_ffn(x2, attn_n, conv_n, wout, g2, ffn_w_gate[i].astype(BF16), ffn_w_up[i].astype(BF16),
                              ffn_w_down[i].astype(BF16))
        else:
            rw = jnp.pad(router_w[i], ((0, 0), (0, LANES - N_EXPERTS))).astype(BF16)
            rb = jnp.pad(router_b[i], (0, LANES - N_EXPERTS))[None, :]
            x1, h2, route, counts = _outproj_router(x2, attn_n, conv_n, wout, g2, rw, rb)
            x2 = _moe(h2, x1, route, counts, moe_w_gate[i].astype(BF16), moe_w_up[i].astype(BF16),
                      moe_w_down[i].astype(BF16))
    return x2.reshape(B, S, D_MODEL)
```
